```python
import math
import jax
import jax.numpy as jnp
from jax import lax
import numpy as np

D_MODEL = 1024
BATCH = 4
SEQ = 8192
DEPTH = 4

CTX_LEN = 256
GRID_W = 64
N_EVEN = (DEPTH + 1) // 2
N_ODD = DEPTH // 2
N_MOD = 9
NORM_EPS = 1e-6
D_FF = 2816

RWKV_HEAD_DIM = 64
RWKV_HEADS = D_MODEL // 128
RWKV_WIDTH = RWKV_HEADS * RWKV_HEAD_DIM
DECAY_LORA = 64
ICLR_LORA = 64
GATE_LORA = 128
RWKV_SPLITS = (RWKV_WIDTH, 2 * RWKV_WIDTH, 3 * RWKV_WIDTH, 3 * RWKV_WIDTH + 2 * DECAY_LORA, 3 * RWKV_WIDTH + 2 * DECAY_LORA + 2 * ICLR_LORA)
RWKV_COLS = RWKV_SPLITS[-1] + GATE_LORA
GN_EPS = 64e-5

S5_WIDTH = D_MODEL // 2
S5_GROUP = 16
S5_GROUPS = S5_WIDTH // S5_GROUP
S5_STATE = 64
DT_MIN = 1e-3
DT_MAX = 1e-1

EVEN_IN = RWKV_COLS + S5_WIDTH
EVEN_OUT = RWKV_WIDTH + S5_WIDTH

HY_WIDTH = D_MODEL
HY_ORDER = 2
HY_IN = (HY_ORDER + 1) * HY_WIDTH
HY_SHORT = 3
HY_EMB = 33
HY_BANDS = (HY_EMB - 1) // 2
HY_HIDDEN = 64
HY_MIN_DECAY = math.log(1e-2) / 1.5
HY_MAX_DECAY = math.log(1e-2) / 0.3

kernel_name = 'hybrid_rwkv7_s5_hyena_flow_trunk'


def rms_norm(x, gain):
    xf = x.astype(jnp.float32)
    y = xf * lax.rsqrt(jnp.mean(xf * xf, axis=-1, keepdims=True) + NORM_EPS)
    return (y * gain.astype(jnp.float32)).astype(x.dtype)


def modulate(x, gain, shift, scale):
    return rms_norm(x, gain) * (1 + scale) + shift


def ada_mods(cond, w, b):
    m = jax.nn.silu(cond) @ w + b
    m = m.reshape(m.shape[:-1] + (1, m.shape[-1]))
    return jnp.split(m, N_MOD, axis=-1)


def ffn_half(x, gain, shift, scale, gate, wg, wu, wd):
    h = modulate(x, gain, shift, scale)
    return x + 0.5 * gate * ((jax.nn.silu(h @ wg) * (h @ wu)) @ wd)


def shift_axis(x, offset, axis):
    n = x.shape[axis]
    pad = [(0, 0)] * x.ndim
    if offset > 0:
        pad[axis] = (offset, 0)
        kept = lax.slice_in_dim(x, 0, n - offset, axis=axis)
    else:
        pad[axis] = (0, -offset)
        kept = lax.slice_in_dim(x, -offset, n, axis=axis)
    return jnp.pad(kept, pad)


def qshift_latent(p):
    bsz, L, C = p.shape
    rows = L // GRID_W
    g = p.reshape(bsz, rows, GRID_W, C // 4, 4)
    parts = [shift_axis(g[..., 0], 1, 2), shift_axis(g[..., 1], -1, 2),
             shift_axis(g[..., 2], 1, 1), shift_axis(g[..., 3], -1, 1)]
    return jnp.stack(parts, axis=-1).reshape(bsz, L, C)


def qshift_context(p):
    bsz, L, C = p.shape
    g = p.reshape(bsz, L, C // 4, 4)
    parts = [shift_axis(g[..., 0], 1, 1), shift_axis(g[..., 1], -1, 1),
             shift_axis(g[..., 2], 1, 1), shift_axis(g[..., 3], -1, 1)]
    return jnp.stack(parts, axis=-1).reshape(bsz, L, C)


def rwkv_features(p, shifted, mu, w0, w_up, a0, a_up, g_up, k_k, k_a):
    f32 = jnp.float32
    q = (p + mu * (shifted - p)).astype(f32)
    bsz, L, _ = q.shape
    r, k, v, wd, ad, gd = jnp.split(q, RWKV_SPLITS, axis=-1)
    wd = wd.reshape(bsz, L, 2, DECAY_LORA)
    ad = ad.reshape(bsz, L, 2, ICLR_LORA)
    w_log = -jax.nn.softplus(-(w0 + jnp.einsum('blds,dsc->bldc', jnp.tanh(wd), w_up))) - 0.5
    decay = jnp.exp(-jnp.exp(w_log))
    a = jax.nn.sigmoid(a0 + jnp.einsum('blds,dsc->bldc', ad, a_up))
    g = jax.nn.sigmoid(gd) @ g_up

    def heads(t):
        return t.reshape(t.shape[:-1] + (RWKV_HEADS, RWKV_HEAD_DIM))

    kk = heads(k * k_k)
    kk = kk / jnp.maximum(jnp.sqrt(jnp.sum(kk * kk, axis=-1, keepdims=True)), 1e-12)
    k_dir = k[:, :, None] * (1 + (a - 1) * k_a)
    return heads(r), heads(v), kk, g, heads(decay), heads(k_dir), heads(a)


def rwkv_scan(r, w, k, v, kk, a, s0, reverse):
    def step(s, inp):
        r_t, w_t, k_t, v_t, kk_t, a_t = inp
        sa = jnp.einsum('bhij,bhj->bhi', s, kk_t)
        s = (s * w_t[:, :, None, :] - sa[..., None] * (kk_t * a_t)[:, :, None, :]
             + v_t[..., None] * k_t[:, :, None, :])
        return s, jnp.einsum('bhij,bhj->bhi', s, r_t)

    xs = tuple(jnp.moveaxis(t, 1, 0) for t in (r, w, k, v, kk, a))
    s_final, out = lax.scan(step, s0, xs, reverse=reverse)
    return jnp.moveaxis(out, 0, 1), s_final


def rwkv_readout(o_f, o_b, r, k_f, k_b, v, g, r_k, gn_g, gn_b):
    o = o_f + o_b
    mean = jnp.mean(o, axis=-1, keepdims=True)
    var = jnp.mean(jnp.square(o - mean), axis=-1, keepdims=True)
    o = (o - mean) * lax.rsqrt(var + GN_EPS)
    bonus = jnp.sum(r * (k_f + k_b) * r_k, axis=-1, keepdims=True) * v
    bsz, L = o.shape[:2]
    o = o.reshape(bsz, L, RWKV_WIDTH) * gn_g + gn_b + bonus.reshape(bsz, L, RWKV_WIDTH)
    return o * g


def s5_discretise(lam_re, lam_im, log_dt, b_re, b_im):
    f32 = jnp.float32
    lam = lax.complex(lam_re.astype(f32), lam_im.astype(f32))
    dt = jnp.exp(log_dt.astype(f32))[..., None]
    a_bar = jnp.exp(lam * dt)
    b_bar = ((a_bar - 1) / lam)[..., None] * lax.complex(b_re.astype(f32), b_im.astype(f32))
    return a_bar, b_bar


def s5_scan(u, a_bar, b_bar, h0, reverse):
    bu = jnp.einsum('gph,blgh->blgp', b_bar, u)
    idx = -1 if reverse else 0
    bu = bu.at[:, idx].add(a_bar[None] * h0)
    L = u.shape[1]
    a = jnp.broadcast_to(a_bar[None, None], (1, L) + a_bar.shape)

    def combine(e1, e2):
        a1, b1 = e1
        a2, b2 = e2
        return a2 * a1, a2 * b1 + b2

    _, h = lax.associative_scan(combine, (a, bu), reverse=reverse, axis=1)
    return h


def s5_readout(h_f, h_b, u, c_mat, d_skip, w_glu, b_glu):
    y = jnp.real(jnp.einsum('ghp,blgp->blgh', c_mat[0], h_f) + jnp.einsum('ghp,blgp->blgh', c_mat[1], h_b))
    bsz, L = u.shape[:2]
    y = y.reshape(bsz, L, S5_WIDTH) + d_skip * u.reshape(bsz, L, S5_WIDTH)
    y = jax.nn.gelu(y)
    return y * jax.nn.sigmoid(y @ w_glu + b_glu)


def even_mixer(h_lat, h_ctx, ctx_out, prm):
    (w_in, mu, w0, w_up, a0, a_up, g_up, k_k, k_a, r_k, gn_g, gn_b,
     lam_re, lam_im, log_dt, b_re, b_im, c_re, c_im, d_skip, w_glu, b_glu, w_out) = prm
    f32 = jnp.float32
    bsz = h_lat.shape[0]
    p_lat = h_lat @ w_in
    p_ctx = h_ctx @ w_in

    def rwkv_feats(p, shift_fn):
        pr = p[..., :RWKV_COLS]
        return rwkv_features(pr, shift_fn(pr), mu, w0, w_up, a0, a_up, g_up, k_k, k_a)

    r_c, v_c, kk_c, g_c, w_c, k_c, a_c = rwkv_feats(p_ctx, qshift_context)
    r_l, v_l, kk_l, g_l, w_l, k_l, a_l = rwkv_feats(p_lat, qshift_latent)
    s_zero = jnp.zeros((bsz, RWKV_HEADS, RWKV_HEAD_DIM, RWKV_HEAD_DIM), f32)
    oc_f, sc_f = rwkv_scan(r_c, w_c[:, :, 0], k_c[:, :, 0], v_c, kk_c, a_c[:, :, 0], s_zero, False)
    oc_b, sc_b = rwkv_scan(r_c, w_c[:, :, 1], k_c[:, :, 1], v_c, kk_c, a_c[:, :, 1], s_zero, True)
    ol_f, _ = rwkv_scan(r_l, w_l[:, :, 0], k_l[:, :, 0], v_l, kk_l, a_l[:, :, 0], sc_f, False)
    ol_b, _ = rwkv_scan(r_l, w_l[:, :, 1], k_l[:, :, 1], v_l, kk_l, a_l[:, :, 1], sc_b, True)

    a_bar, b_bar = s5_discretise(lam_re, lam_im, log_dt, b_re, b_im)
    c_mat = lax.complex(c_re.astype(f32), c_im.astype(f32))
    u_c = p_ctx[..., RWKV_COLS:].astype(f32).reshape(bsz, -1, S5_GROUPS, S5_GROUP)
    u_l = p_lat[..., RWKV_COLS:].astype(f32).reshape(bsz, -1, S5_GROUPS, S5_GROUP)
    h_zero = jnp.zeros((bsz, S5_GROUPS, S5_STATE), jnp.complex64)
    hc_f = s5_scan(u_c, a_bar[0], b_bar[0], h_zero, False)
    hc_b = s5_scan(u_c, a_bar[1], b_bar[1], h_zero, True)
    hl_f = s5_scan(u_l, a_bar[0], b_bar[0], hc_f[:, -1], False)
    hl_b = s5_scan(u_l, a_bar[1], b_bar[1], hc_b[:, 0], True)

    def merge(o_f, o_b, r, k, v, g, h_f, h_b, u):
        y_rwkv = rwkv_readout(o_f, o_b, r, k[:, :, 0], k[:, :, 1], v, g, r_k, gn_g, gn_b)
        y_s5 = s5_readout(h_f, h_b, u, c_mat, d_skip, w_glu, b_glu)
        return jnp.concatenate([y_rwkv, y_s5], axis=-1).astype(h_lat.dtype) @ w_out

    y_lat = merge(ol_f, ol_b, r_l, k_l, v_l, g_l, hl_f, hl_b, u_l)
    y_ctx = merge(oc_f, oc_b, r_c, k_c, v_c, g_c, hc_f, hc_b, u_c) if ctx_out else None
    return y_lat, y_ctx


def short_conv(p, w, b):
    return w[0] * shift_axis(p, 1, 1) + w[1] * p + w[2] * shift_axis(p, -1, 1) + b


def hyena_filter_spectrum(L, fw1, fb1, fw2, fb2, fw3, fb3, fw4, freq):
    f32 = jnp.float32
    pos = jnp.arange(L, dtype=f32)[:, None]
    t = pos / max(L - 1, 1)
    ang = 2 * math.pi * pos / L
    bands = jnp.linspace(1e-4, HY_BANDS - 1, HY_BANDS, dtype=f32)[None]
    feats = jnp.concatenate([t, jnp.cos(bands * ang), -jnp.sin(bands * ang)], axis=-1)
    fr = freq.astype(f32)
    hdn = jnp.sin(fr * (feats @ fw1.astype(f32) + fb1.astype(f32)))
    hdn = jnp.sin(fr * (hdn @ fw2.astype(f32) + fb2.astype(f32)))
    hdn = jnp.sin(fr * (hdn @ fw3.astype(f32) + fb3.astype(f32)))
    filt = (hdn @ fw4.astype(f32)).reshape(L, HY_ORDER, 2, HY_WIDTH)
    deltas = jnp.abs(jnp.linspace(HY_MIN_DECAY, HY_MAX_DECAY, HY_WIDTH, dtype=f32))
    filt = filt * jnp.exp(-t[:, :, None, None] * deltas)
    fwd, bwd = filt[:, :, 0], filt[:, :, 1]
    kern = jnp.concatenate([fwd, jnp.zeros_like(fwd[:1]), bwd[:0:-1]], axis=0)
    kern = kern * lax.rsqrt(jnp.sum(kern * kern, axis=0, keepdims=True) + 1e-6)
    return jnp.fft.rfft(kern, axis=0)


def hyena_mixer(h, prm):
    (w_in, conv_w, conv_b, fw1, fb1, fw2, fb2, fw3, fb3, fw4, freq, bias_d, w_out) = prm
    f32 = jnp.float32
    L = h.shape[1]
    p = short_conv(h @ w_in, conv_w, conv_b).astype(f32)
    parts = jnp.split(p, HY_ORDER + 1, axis=-1)
    kf = hyena_filter_spectrum(L, fw1, fb1, fw2, fb2, fw3, fb3, fw4, freq)
    z = parts[0]
    for n, gate in enumerate(parts[1:]):
        zf = jnp.fft.rfft(z, n=2 * L, axis=1)
        conv = jnp.fft.irfft(zf * kf[None, :, n], n=2 * L, axis=1)[:, :L]
        z = gate * (conv + bias_d[n] * z)
    return z.astype(h.dtype) @ w_out


def setup_inputs(seed: int = 0) -> dict:
    key = jax.random.key(seed)
    ks = iter(jax.random.split(key, 64))
    f32 = jnp.float32

    def nrm(shape, std):
        return jax.random.normal(next(ks), shape, f32) * std

    def uni(shape, lo, hi):
        return jax.random.uniform(next(ks), shape, f32, lo, hi)

    D = D_MODEL
    NE, NO = N_EVEN, N_ODD
    W = RWKV_WIDTH
    G, P = S5_GROUPS, S5_STATE
    n_idx = jnp.arange(P, dtype=f32)
    return {
        'x': nrm((BATCH, SEQ, D), 1.0),
        'c': nrm((BATCH, D), 1.0),
        'ctx': nrm((BATCH, CTX_LEN, D), 1.0),
        'c_ctx': nrm((D,), 1.0),
        'norm_g': 1.0 + nrm((DEPTH, 3, D), 0.02),
        'ada_w': nrm((DEPTH, D, N_MOD * D), 0.5 * D ** -0.5),
        'ada_b': nrm((DEPTH, N_MOD * D), 0.02),
        'ffn_wg': nrm((DEPTH, 2, D, D_FF), D ** -0.5),
        'ffn_wu': nrm((DEPTH, 2, D, D_FF), D ** -0.5),
        'ffn_wd': nrm((DEPTH, 2, D_FF, D), D_FF ** -0.5),
        'final_g': 1.0 + nrm((D,), 0.02),
        'ev_w_in': nrm((NE, D, EVEN_IN), D ** -0.5),
        'ev_mu': uni((NE, RWKV_COLS), 0.0, 1.0),
        'ev_w0': uni((NE, 2, W), -6.5, -1.5),
        'ev_w_up': nrm((NE, 2, DECAY_LORA, W), 0.1 * DECAY_LORA ** -0.5),
        'ev_a0': nrm((NE, 2, W), 0.1),
        'ev_a_up': nrm((NE, 2, ICLR_LORA, W), ICLR_LORA ** -0.5),
        'ev_g_up': nrm((NE, GATE_LORA, W), GATE_LORA ** -0.5),
        'ev_k_k': 0.85 + nrm((NE, W), 0.02),
        'ev_k_a': 1.0 + nrm((NE, W), 0.02),
        'ev_r_k': -0.04 + nrm((NE, RWKV_HEADS, RWKV_HEAD_DIM), 0.02),
        'ev_gn_g': 1.0 + nrm((NE, W), 0.02),
        'ev_gn_b': nrm((NE, W), 0.02),
        'ev_lam_re': -0.5 + nrm((NE, 2, G, P), 0.01),
        'ev_lam_im': math.pi * n_idx + nrm((NE, 2, G, P), 0.01),
        'ev_log_dt': uni((NE, 2, G), math.log(DT_MIN), math.log(DT_MAX)),
        'ev_b_re': nrm((NE, 2, G, P, S5_GROUP), (2 * S5_GROUP) ** -0.5),
        'ev_b_im': nrm((NE, 2, G, P, S5_GROUP), (2 * S5_GROUP) ** -0.5),
        'ev_c_re': nrm((NE, 2, G, S5_GROUP, P), P ** -0.5),
        'ev_c_im': nrm((NE, 2, G, S5_GROUP, P), P ** -0.5),
        'ev_d': nrm((NE, S5_WIDTH), 1.0),
        'ev_w_glu': nrm((NE, S5_WIDTH, S5_WIDTH), S5_WIDTH ** -0.5),
        'ev_b_glu': nrm((NE, S5_WIDTH), 0.02),
        'ev_w_out': nrm((NE, EVEN_OUT, D), EVEN_OUT ** -0.5),
        'od_w_in': nrm((NO, D, HY_IN), D ** -0.5),
        'od_conv_w': nrm((NO, HY_SHORT, HY_IN), HY_SHORT ** -0.5),
        'od_conv_b': nrm((NO, HY_IN), 0.02),
        'od_fw1': nrm((NO, HY_EMB, HY_HIDDEN), HY_EMB ** -0.5),
        'od_fb1': nrm((NO, HY_HIDDEN), 0.02),
        'od_fw2': nrm((NO, HY_HIDDEN, HY_HIDDEN), HY_HIDDEN ** -0.5),
        'od_fb2': nrm((NO, HY_HIDDEN), 0.02),
        'od_fw3': nrm((NO, HY_HIDDEN, HY_HIDDEN), HY_HIDDEN ** -0.5),
        'od_fb3': nrm((NO, HY_HIDDEN), 0.02),
        'od_fw4': nrm((NO, HY_HIDDEN, HY_ORDER * 2 * HY_WIDTH), HY_HIDDEN ** -0.5),
        'od_freq': 1.0 + nrm((NO, HY_HIDDEN), 0.02),
        'od_bias': nrm((NO, HY_ORDER, HY_WIDTH), 0.5),
        'od_w_out': nrm((NO, HY_WIDTH, D), HY_WIDTH ** -0.5),
    }


def reference(x, c, ctx, c_ctx, norm_g, ada_w, ada_b, ffn_wg, ffn_wu, ffn_wd, final_g,
              ev_w_in, ev_mu, ev_w0, ev_w_up, ev_a0, ev_a_up, ev_g_up, ev_k_k, ev_k_a, ev_r_k,
              ev_gn_g, ev_gn_b, ev_lam_re, ev_lam_im, ev_log_dt, ev_b_re, ev_b_im, ev_c_re, ev_c_im,
              ev_d, ev_w_glu, ev_b_glu, ev_w_out,
              od_w_in, od_conv_w, od_conv_b, od_fw1, od_fb1, od_fw2, od_fb2, od_fw3, od_fb3, od_fw4,
              od_freq, od_bias, od_w_out):
    lat, cx = x, ctx
    last_ctx = 2 * (N_EVEN - 1)
    for l in range(DEPTH):
        run_ctx = l <= last_ctx
        ctx_out = l < last_ctx
        i = l // 2
        m_l = ada_mods(c, ada_w[l], ada_b[l])
        m_c = ada_mods(c_ctx, ada_w[l], ada_b[l]) if run_ctx else None
        lat = ffn_half(lat, norm_g[l, 0], m_l[0], m_l[1], m_l[2], ffn_wg[l, 0], ffn_wu[l, 0], ffn_wd[l, 0])
        if run_ctx:
            cx = ffn_half(cx, norm_g[l, 0], m_c[0], m_c[1], m_c[2], ffn_wg[l, 0], ffn_wu[l, 0], ffn_wd[l, 0])
        h_lat = modulate(lat, norm_g[l, 1], m_l[3], m_l[4])
        if l % 2 == 0:
            h_ctx = modulate(cx, norm_g[l, 1], m_c[3], m_c[4])
            prm = (ev_w_in[i], ev_mu[i], ev_w0[i], ev_w_up[i], ev_a0[i], ev_a_up[i], ev_g_up[i],
                   ev_k_k[i], ev_k_a[i], ev_r_k[i], ev_gn_g[i], ev_gn_b[i],
                   ev_lam_re[i], ev_lam_im[i], ev_log_dt[i], ev_b_re[i], ev_b_im[i], ev_c_re[i], ev_c_im[i],
                   ev_d[i], ev_w_glu[i], ev_b_glu[i], ev_w_out[i])
            y_lat, y_ctx = even_mixer(h_lat, h_ctx, ctx_out, prm)
        else:
            prm = (od_w_in[i], od_conv_w[i], od_conv_b[i], od_fw1[i], od_fb1[i], od_fw2[i], od_fb2[i],
                   od_fw3[i], od_fb3[i], od_fw4[i], od_freq[i], od_bias[i], od_w_out[i])
            y_lat = hyena_mixer(h_lat, prm)
            y_ctx = hyena_mixer(modulate(cx, norm_g[l, 1], m_c[3], m_c[4]), prm) if ctx_out else None
        lat = lat + m_l[5] * y_lat
        lat = ffn_half(lat, norm_g[l, 2], m_l[6], m_l[7], m_l[8], ffn_wg[l, 1], ffn_wu[l, 1], ffn_wd[l, 1])
        if ctx_out:
            cx = cx + m_c[5] * y_ctx
            cx = ffn_half(cx, norm_g[l, 2], m_c[6], m_c[7], m_c[8], ffn_wg[l, 1], ffn_wu[l, 1], ffn_wd[l, 1])
    return rms_norm(lat, final_g)
```

```python
import functools
import math

import numpy as np
import jax
import jax.numpy as jnp
from jax import lax
from jax.experimental import pallas as pl
from jax.experimental.pallas import tpu as pltpu

F32 = jnp.float32
BF16 = jnp.bfloat16
HIGHEST = lax.Precision.HIGHEST

N_MOD = 9
NORM_EPS = 1e-6
GN_EPS = 64e-5
GRID_W = 64
HEAD_DIM = 64
S5_GROUP = 16
S5_STATE = 64
S5_SUPER = 4
HY_ORDER = 2
HY_EMB = 33
HY_MIN_DECAY = math.log(1e-2) / 1.5
HY_MAX_DECAY = math.log(1e-2) / 0.3
LANES = 128
FFT_N2 = 128
VMEM_LIMIT = 56 * 1024 * 1024


def _cparams(sem, vmem=VMEM_LIMIT):
    return pltpu.CompilerParams(dimension_semantics=sem, vmem_limit_bytes=vmem)


def _dot(a, b):
    return jnp.dot(a, b, preferred_element_type=F32)


def _dot_exact(a, b):
    return jnp.dot(a, b, preferred_element_type=F32, precision=HIGHEST)


def _rms_mod(x, gain, shift, scale):
    ms = jnp.mean(x * x, axis=-1, keepdims=True)
    return x * lax.rsqrt(ms + NORM_EPS) * gain * (1.0 + scale) + shift


def _sigmoid(x):
    return 1.0 / (1.0 + jnp.exp(-x))


def _silu(x):
    return x * _sigmoid(x)


def _ada_kernel(c_ref, w_ref, b_ref, o_ref):
    s = _silu(c_ref[...])
    o_ref[0] = _dot(s.astype(BF16), w_ref[0].astype(BF16)) + b_ref[0]


def ada_mods_all(cond8, ada_w, ada_b):
    depth, d, n = ada_w.shape
    tn = n // 8
    return pl.pallas_call(
        _ada_kernel,
        grid=(depth, n // tn),
        in_specs=[pl.BlockSpec((8, d), lambda l, j: (0, 0)),
                  pl.BlockSpec((1, d, tn), lambda l, j: (l, 0, j)),
                  pl.BlockSpec((1, 1, tn), lambda l, j: (l, 0, j))],
        out_specs=pl.BlockSpec((1, 8, tn), lambda l, j: (l, 0, j)),
        out_shape=jax.ShapeDtypeStruct((depth, 8, n), F32),
        compiler_params=_cparams(("parallel", "parallel")),
        name="ada_mods",
    )(cond8, ada_w, ada_b.reshape(depth, 1, n))


def _ffn_kernel(x_ref, gain_ref, shift_ref, scale_ref, gate_ref, wg_ref, wu_ref, wd_ref, fg_ref,
                o_ref, h_ref, acc_ref, *, final_norm):
    k = pl.program_id(2)

    @pl.when(k == 0)
    def _():
        h = _rms_mod(x_ref[0], gain_ref[...], shift_ref[0], scale_ref[0])
        h_ref[...] = h.astype(BF16)
        acc_ref[...] = jnp.zeros_like(acc_ref)

    h = h_ref[...]
    g = _dot(h, wg_ref[...])
    u = _dot(h, wu_ref[...])
    a = (_silu(g) * u).astype(BF16)
    acc_ref[...] += _dot(a, wd_ref[...])

    @pl.when(k == pl.num_programs(2) - 1)
    def _():
        y = x_ref[0] + 0.5 * gate_ref[0] * acc_ref[...]
        if final_norm:
            ms = jnp.mean(y * y, axis=-1, keepdims=True)
            y = y * lax.rsqrt(ms + NORM_EPS) * fg_ref[...]
        o_ref[0] = y


def ffn_half(x, gain, shift, scale, gate, wg, wu, wd, final_g=None):
    bsz, t, d = x.shape
    ff = wg.shape[1]
    tm = min(512, t)
    tf = ff // 2
    fg = jnp.ones((1, d), F32) if final_g is None else final_g.reshape(1, d)
    vec = pl.BlockSpec((1, 1, d), lambda b, i, k: (b, 0, 0))
    return pl.pallas_call(
        functools.partial(_ffn_kernel, final_norm=final_g is not None),
        grid=(bsz, t // tm, ff // tf),
        in_specs=[pl.BlockSpec((1, tm, d), lambda b, i, k: (b, i, 0)),
                  pl.BlockSpec((1, d), lambda b, i, k: (0, 0)),
                  vec, vec, vec,
                  pl.BlockSpec((d, tf), lambda b, i, k: (0, k)),
                  pl.BlockSpec((d, tf), lambda b, i, k: (0, k)),
                  pl.BlockSpec((tf, d), lambda b, i, k: (k, 0)),
                  pl.BlockSpec((1, d), lambda b, i, k: (0, 0))],
        out_specs=pl.BlockSpec((1, tm, d), lambda b, i, k: (b, i, 0)),
        out_shape=jax.ShapeDtypeStruct((bsz, t, d), F32),
        scratch_shapes=[pltpu.VMEM((tm, d), BF16), pltpu.VMEM((tm, d), F32)],
        compiler_params=_cparams(("parallel", "parallel", "arbitrary")),
        name="ffn_half",
    )(x, gain.reshape(1, d), shift, scale, gate, wg, wu, wd, fg)


def _modmm_kernel(x_ref, gain_ref, shift_ref, scale_ref, w_ref, o_ref):
    h = _rms_mod(x_ref[0], gain_ref[...], shift_ref[0], scale_ref[0])
    o_ref[0] = _dot(h.astype(BF16), w_ref[...])


def mod_matmul(x, gain, shift, scale, w):
    bsz, t, d = x.shape
    n = w.shape[1]
    tm = min(512, t)
    vec = pl.BlockSpec((1, 1, d), lambda b, i: (b, 0, 0))
    return pl.pallas_call(
        _modmm_kernel,
        grid=(bsz, t // tm),
        in_specs=[pl.BlockSpec((1, tm, d), lambda b, i: (b, i, 0)),
                  pl.BlockSpec((1, d), lambda b, i: (0, 0)),
                  vec, vec,
                  pl.BlockSpec((d, n), lambda b, i: (0, 0))],
        out_specs=pl.BlockSpec((1, tm, n), lambda b, i: (b, i, 0)),
        out_shape=jax.ShapeDtypeStruct((bsz, t, n), F32),
        compiler_params=_cparams(("parallel", "parallel")),
        name="mod_matmul",
    )(x, gain.reshape(1, d), shift, scale, w)


def _rwkv_feat_kernel(p_ref, pu_ref, pd_ref, mu_ref, wup_ref, w0_ref, aup_ref, a0_ref, gup_ref,
                      kk_s_ref, ka_ref, rk_ref, ones_ref, sel_ref,
                      kk_o, v_o, g_o, bonus_o, wr_o, w_o, kka_o, kd_o, c_o, *, latent, nt, width):
    i = pl.program_id(1)
    p = p_ref[0]
    tm = p.shape[0]
    row = lax.broadcasted_iota(jnp.int32, p.shape, 0)
    lane = lax.broadcasted_iota(jnp.int32, p.shape, 1) % 4
    prev = pltpu.roll(p, 1, 0)
    nxt = pltpu.roll(p, tm - 1, 0)
    if latent:
        col = row % GRID_W
        left = jnp.where(col == 0, 0.0, prev)
        right = jnp.where(col == GRID_W - 1, 0.0, nxt)
        up_halo = jnp.where(i > 0, pu_ref[0], 0.0)
        dn_halo = jnp.where(i < nt - 1, pd_ref[0], 0.0)
        up = jnp.concatenate([up_halo, p[:tm - GRID_W]], axis=0)
        down = jnp.concatenate([p[GRID_W:], dn_halo], axis=0)
    else:
        left = jnp.where(row == 0, 0.0, prev)
        right = jnp.where(row == tm - 1, 0.0, nxt)
        up, down = left, right
    shifted = jnp.where(lane == 0, left, jnp.where(lane == 1, right, jnp.where(lane == 2, up, down)))
    q = p + mu_ref[...] * (shifted - p)

    w = width
    r, k, v = q[:, :w], q[:, w:2 * w], q[:, 2 * w:3 * w]
    wd = q[:, 3 * w:3 * w + LANES]
    ad = q[:, 3 * w + LANES:3 * w + 2 * LANES]
    gd = q[:, 3 * w + 2 * LANES:3 * w + 3 * LANES]

    zlin = w0_ref[...] + _dot(jnp.tanh(wd).astype(BF16), wup_ref[...])
    neg = -zlin
    softplus = jnp.maximum(neg, 0.0) + jnp.log(1.0 + jnp.exp(-jnp.abs(neg)))
    decay = jnp.exp(-jnp.exp(-softplus - 0.5))
    a = _sigmoid(a0_ref[...] + _dot(ad.astype(BF16), aup_ref[...]))
    g_o[0] = _dot(_sigmoid(gd).astype(BF16), gup_ref[...])

    ones = ones_ref[...]
    kk = k * kk_s_ref[...]
    ss = _dot_exact(kk * kk, ones)
    kk = kk / jnp.maximum(jnp.sqrt(ss), 1e-12)
    kk_o[0] = kk
    v_o[0] = v

    ksum = jnp.zeros_like(k)
    prods = []
    for d in range(2):
        a_d = a[:, d * w:(d + 1) * w]
        w_d = decay[:, d * w:(d + 1) * w]
        kd = k * (1.0 + (a_d - 1.0) * ka_ref[...])
        kka = kk * a_d
        ksum = ksum + kd
        wr_o[d, 0] = w_d * r
        w_o[d, 0] = w_d
        kka_o[d, 0] = kka
        kd_o[d, 0] = kd
        prods += [kka * r, kd * r]
    c_o[0] = _dot_exact(jnp.concatenate(prods, axis=1), sel_ref[...])
    bonus_o[0] = _dot_exact(r * ksum * rk_ref[...], ones) * v


def rwkv_features(p, n_cols, width, consts, latent):
    bsz, t, _ = p.shape
    tm = min(512, t) if latent else t
    nt = t // tm
    hb = tm // GRID_W if latent else 1
    halo_rows = GRID_W if latent else 8
    nhb = t // halo_rows
    w = width
    full = lambda a: pl.BlockSpec(a.shape, lambda b, i: (0,) * a.ndim)
    tok = pl.BlockSpec((1, tm, w), lambda b, i: (b, i, 0))
    tok2 = pl.BlockSpec((2, 1, tm, w), lambda b, i: (0, b, i, 0))
    sds = jax.ShapeDtypeStruct((bsz, t, w), F32)
    sds2 = jax.ShapeDtypeStruct((2, bsz, t, w), F32)
    return pl.pallas_call(
        functools.partial(_rwkv_feat_kernel, latent=latent, nt=nt, width=w),
        grid=(bsz, nt),
        in_specs=[pl.BlockSpec((1, tm, n_cols), lambda b, i: (b, i, 0)),
                  pl.BlockSpec((1, halo_rows, n_cols), lambda b, i: (b, jnp.maximum(i * hb - 1, 0), 0)),
                  pl.BlockSpec((1, halo_rows, n_cols), lambda b, i: (b, jnp.minimum((i + 1) * hb, nhb - 1), 0))]
                 + [full(a) for a in consts],
        out_specs=[tok, tok, tok, tok, tok2, tok2, tok2, tok2,
                   pl.BlockSpec((1, tm, LANES), lambda b, i: (b, i, 0))],
        out_shape=[sds, sds, sds, sds, sds2, sds2, sds2, sds2,
                   jax.ShapeDtypeStruct((bsz, t, LANES), F32)],
        compiler_params=_cparams(("parallel", "parallel")),
        name="rwkv_features",
    )(p, p, p, *consts)


def _rwkv_scan_kernel(kk_ref, wr_ref, w_ref, kka_ref, k_ref, v_ref, c1_ref, c2_ref, o_ref,
                      s_ref, dup_ref, *, tc):
    @pl.when(pl.program_id(0) == 0)
    def _():
        s_ref[...] = jnp.zeros_like(s_ref)

    half = LANES // 2
    for n, ref in enumerate((kk_ref, wr_ref, w_ref, kka_ref, k_ref)):
        x = ref[...]
        dup_ref[n, :, :, 0:half] = x
        dup_ref[n, :, :, half:LANES] = x

    nj = s_ref.shape[0]

    def step(t, carry):
        sa = jnp.zeros(s_ref.shape[1:], F32)
        o1 = jnp.zeros(s_ref.shape[1:], F32)
        for j in range(nj):
            sj = s_ref[j]
            sa = sa + sj * dup_ref[0, t, j:j + 1, :]
            o1 = o1 + sj * dup_ref[1, t, j:j + 1, :]
        v = v_ref[t]
        o_ref[t] = o1 - sa * c1_ref[pl.ds(t, 1), :] + v * c2_ref[pl.ds(t, 1), :]
        for j in range(nj):
            s_ref[j] = (s_ref[j] * dup_ref[2, t, j:j + 1, :] - sa * dup_ref[3, t, j:j + 1, :]
                        + v * dup_ref[4, t, j:j + 1, :])
        return carry

    lax.fori_loop(0, tc, step, 0)


def rwkv_scan(kk, wr, w, kka, k, v, c1, c2):
    ttot, nj, nc = kk.shape
    ni = v.shape[1]
    tc = 32
    jspec = pl.BlockSpec((tc, nj, nc), lambda g: (g, 0, 0))
    ispec = pl.BlockSpec((tc, ni, LANES), lambda g: (g, 0, 0))
    cspec = pl.BlockSpec((tc, LANES), lambda g: (g, 0))
    return pl.pallas_call(
        functools.partial(_rwkv_scan_kernel, tc=tc),
        grid=(ttot // tc,),
        in_specs=[jspec] * 5 + [ispec, cspec, cspec],
        out_specs=ispec,
        out_shape=jax.ShapeDtypeStruct((ttot, ni, LANES), F32),
        scratch_shapes=[pltpu.VMEM((nj, ni, LANES), F32), pltpu.VMEM((5, tc, nj, LANES), F32)],
        compiler_params=_cparams(("arbitrary",)),
        name="rwkv_scan",
    )(kk, wr, w, kka, k, v, c1, c2)


def _s5_kernel(u_ref, bm_ref, cm_ref, a_ref, y_ref, h_ref, bu_ref, *, tt):
    @pl.when(pl.program_id(0) == 0)
    def _():
        h_ref[...] = jnp.zeros_like(h_ref)

    nsb = bm_ref.shape[0]
    kin = bm_ref.shape[1]
    sw = bm_ref.shape[2]
    hw = sw // 2
    u = u_ref[...].reshape(tt * 8, nsb * kin)
    for sb in range(nsb):
        bu_ref[:, sb * sw:(sb + 1) * sw] = _dot(u[:, sb * kin:(sb + 1) * kin], bm_ref[sb])

    def step(t, hs):
        rows = pl.ds(pl.multiple_of(t * 8, 8), 8)
        out = []
        for sb in range(nsb):
            hr, hi = hs[2 * sb], hs[2 * sb + 1]
            lo = sb * sw
            ar = a_ref[0, :, lo:lo + hw]
            ai = a_ref[0, :, lo + hw:lo + sw]
            nr = ar * hr - ai * hi + bu_ref[rows, lo:lo + hw]
            ni = ar * hi + ai * hr + bu_ref[rows, lo + hw:lo + sw]
            bu_ref[rows, lo:lo + hw] = nr
            bu_ref[rows, lo + hw:lo + sw] = ni
            out += [nr, ni]
        return tuple(out)

    h0 = []
    for sb in range(nsb):
        h0 += [h_ref[:, sb * sw:sb * sw + hw], h_ref[:, sb * sw + hw:(sb + 1) * sw]]
    hs = lax.fori_loop(0, tt, step, tuple(h0))
    for sb in range(nsb):
        h_ref[:, sb * sw:sb * sw + hw] = hs[2 * sb]
        h_ref[:, sb * sw + hw:(sb + 1) * sw] = hs[2 * sb + 1]

    nout = cm_ref.shape[2] // 2
    fwd_row = (lax.broadcasted_iota(jnp.int32, (tt * 8, nout), 0) % 8) < 4
    ys = []
    for sb in range(nsb):
        yy = _dot(bu_ref[:, sb * sw:(sb + 1) * sw].astype(BF16), cm_ref[sb])
        ys.append(jnp.where(fwd_row, yy[:, :nout], yy[:, nout:]))
    y_ref[...] = jnp.concatenate(ys, axis=1).reshape(tt, 8, nsb * nout)


def s5_scan(u, bm, cm, a):
    ttot = u.shape[0]
    nsb, kin, sw = bm.shape
    nout = cm.shape[2] // 2
    tt = 64
    full = lambda x: pl.BlockSpec(x.shape, lambda g: (0,) * x.ndim)
    return pl.pallas_call(
        functools.partial(_s5_kernel, tt=tt),
        grid=(ttot // tt,),
        in_specs=[pl.BlockSpec((tt, 8, nsb * kin), lambda g: (g, 0, 0)), full(bm), full(cm), full(a)],
        out_specs=pl.BlockSpec((tt, 8, nsb * nout), lambda g: (g, 0, 0)),
        out_shape=jax.ShapeDtypeStruct((ttot, 8, nsb * nout), F32),
        scratch_shapes=[pltpu.VMEM((8, nsb * sw), F32), pltpu.VMEM((tt * 8, nsb * sw), F32)],
        compiler_params=_cparams(("arbitrary",)),
        name="s5_scan",
    )(u, bm, cm, a)


def _even_out_kernel(x_ref, of_ref, ob_ref, bonus_ref, g_ref, ys_ref, u_ref, m5_ref,
                     ones_ref, gng_ref, gnb_ref, dskip_ref, wglu_ref, bglu_ref, wo1_ref, wo2_ref, o_ref):
    ones = ones_ref[...]
    inv = 1.0 / HEAD_DIM
    o = of_ref[0] + ob_ref[0]
    mean = _dot_exact(o, ones) * inv
    oc = o - mean
    var = _dot_exact(oc * oc, ones) * inv
    y1 = (oc * lax.rsqrt(var + GN_EPS) * gng_ref[...] + gnb_ref[...] + bonus_ref[0]) * g_ref[0]
    y = ys_ref[0] + dskip_ref[...] * u_ref[0]
    y = jax.nn.gelu(y)
    y2 = y * _sigmoid(_dot(y.astype(BF16), wglu_ref[...]) + bglu_ref[...])
    out = _dot(y1.astype(BF16), wo1_ref[...]) + _dot(y2.astype(BF16), wo2_ref[...])
    o_ref[0] = x_ref[0] + m5_ref[0] * out


def even_out(x, o_f, o_b, bonus, g, ys, u, m5, consts):
    bsz, t, d = x.shape
    w = o_f.shape[2]
    tm = min(512, t)
    tokd = pl.BlockSpec((1, tm, d), lambda b, i: (b, i, 0))
    tokw = pl.BlockSpec((1, tm, w), lambda b, i: (b, i, 0))
    toks = pl.BlockSpec((1, tm, u.shape[2]), lambda b, i: (b, i, 0))
    full = lambda a: pl.BlockSpec(a.shape, lambda b, i: (0,) * a.ndim)
    return pl.pallas_call(
        _even_out_kernel,
        grid=(bsz, t // tm),
        in_specs=[tokd, tokw, tokw, tokw, tokw, toks, toks,
                  pl.BlockSpec((1, 1, d), lambda b, i: (b, 0, 0))] + [full(a) for a in consts],
        out_specs=tokd,
        out_shape=jax.ShapeDtypeStruct((bsz, t, d), F32),
        compiler_params=_cparams(("parallel", "parallel")),
        name="even_out",
    )(x, o_f, o_b, bonus, g, ys, u, m5, *consts)


def _hy_in_kernel(x_ref, xp_ref, xn_ref, gain_ref, shift_ref, scale_ref, w_ref, cw_ref, cb_ref,
                  z_ref, zb_ref, g1_ref, g2_ref, *, nt, c):
    i = pl.program_id(1)
    gain, shift, scale = gain_ref[...], shift_ref[0], scale_ref[0]
    h = _rms_mod(x_ref[0], gain, shift, scale).astype(BF16)
    hp = _rms_mod(xp_ref[0], gain, shift, scale).astype(BF16)
    hn = _rms_mod(xn_ref[0], gain, shift, scale).astype(BF16)
    tm = h.shape[0]
    row = lax.broadcasted_iota(jnp.int32, (tm, c), 0)
    outs = (z_ref, g1_ref, g2_ref)
    for part in range(3):
        wp = w_ref[:, part * c:(part + 1) * c]
        p = _dot(h, wp)
        pp = jnp.where(i > 0, _dot(hp, wp)[7:8], 0.0)
        pn = jnp.where(i < nt - 1, _dot(hn, wp)[0:1], 0.0)
        pm1 = jnp.where(row == 0, pp, pltpu.roll(p, 1, 0))
        pp1 = jnp.where(row == tm - 1, pn, pltpu.roll(p, tm - 1, 0))
        cw = cw_ref[:, part * c:(part + 1) * c]
        q = cw[0:1] * pm1 + cw[1:2] * p + cw[2:3] * pp1 + cb_ref[:, part * c:(part + 1) * c]
        outs[part][0] = q
        if part == 0:
            zb_ref[0] = q.astype(BF16)


def hyena_in(x, gain, shift, scale, w, conv_w, conv_b):
    bsz, t, d = x.shape
    c = w.shape[1] // 3
    tm = min(512, t)
    nt = t // tm
    hb = tm // 8
    nhb = t // 8
    vec = pl.BlockSpec((1, 1, d), lambda b, i: (b, 0, 0))
    tok = pl.BlockSpec((1, tm, c), lambda b, i: (b, i, 0))
    sds = jax.ShapeDtypeStruct((bsz, t, c), F32)
    return pl.pallas_call(
        functools.partial(_hy_in_kernel, nt=nt, c=c),
        grid=(bsz, nt),
        in_specs=[pl.BlockSpec((1, tm, d), lambda b, i: (b, i, 0)),
                  pl.BlockSpec((1, 8, d), lambda b, i: (b, jnp.maximum(i * hb - 1, 0), 0)),
                  pl.BlockSpec((1, 8, d), lambda b, i: (b, jnp.minimum((i + 1) * hb, nhb - 1), 0)),
                  pl.BlockSpec((1, d), lambda b, i: (0, 0)),
                  vec, vec,
                  pl.BlockSpec(w.shape, lambda b, i: (0, 0)),
                  pl.BlockSpec(conv_w.shape, lambda b, i: (0, 0)),
                  pl.BlockSpec((1, 3 * c), lambda b, i: (0, 0))],
        out_specs=[tok, tok, tok, tok],
        out_shape=[sds, jax.ShapeDtypeStruct((bsz, t, c), BF16), sds, sds],
        compiler_params=_cparams(("parallel", "parallel")),
        name="hyena_in",
    )(x, x, x, gain.reshape(1, d), shift, scale, w, conv_w, conv_b.reshape(1, 3 * c))


def _fft_conv_kernel(z_ref, kf_ref, w1_ref, f2_ref, g2i_ref, f1c_ref, tw_ref, twt_ref, o_ref, a_ref,
                     *, n1, k1c):
    s = pl.program_id(2)
    n2 = FFT_N2
    nd = n1 // 2
    slab = n1 // k1c

    @pl.when(s == 0)
    def _():
        w1 = w1_ref[...]

        def body(j, carry):
            x = jnp.concatenate([z_ref[0, j], z_ref[1, j]], axis=0)
            a_ref[pl.ds(pl.multiple_of(j * 2 * n1, 2 * n1), 2 * n1), :] = _dot(w1, x)
            return carry

        lax.fori_loop(0, n2, body, 0)

    @pl.when((s > 0) & (s <= k1c))
    def _():
        f2r, f2i = f2_ref[0], f2_ref[1]
        g2i = g2i_ref[...]

        def body(kk, carry):
            k1 = (s - 1) * slab + kk
            twr = tw_ref[0, pl.ds(k1, 1), :]
            twi = tw_ref[1, pl.ds(k1, 1), :]
            gr = f2r * twr - f2i * twi
            gi = f2r * twi + f2i * twr
            gmat = jnp.concatenate([jnp.concatenate([gr, -gi], axis=1),
                                    jnp.concatenate([gi, gr], axis=1)], axis=0).astype(BF16)
            re_rows = pl.ds(k1, n2, stride=2 * n1)
            im_rows = pl.ds(n1 + k1, n2, stride=2 * n1)
            ak = jnp.concatenate([a_ref[re_rows, :], a_ref[im_rows, :]], axis=0).astype(BF16)
            x = _dot(gmat, ak)
            xr, xi = x[:n2], x[n2:]
            kr, ki = kf_ref[0, kk], kf_ref[1, kk]
            pr = xr * kr - xi * ki
            pi = xr * ki + xi * kr
            tt = _dot(g2i, jnp.concatenate([pr, pi], axis=0).astype(BF16))
            a_ref[re_rows, :] = tt[:n2]
            a_ref[im_rows, :] = tt[n2:]
            return carry

        lax.fori_loop(0, slab, body, 0)

    @pl.when(s == k1c + 1)
    def _():
        f1r, f1i = f1c_ref[0], f1c_ref[1]

        def body(j, carry):
            twr = twt_ref[0, pl.ds(j, 1), :]
            twi = twt_ref[1, pl.ds(j, 1), :]
            wr = f1r * twr + f1i * twi
            wi = f1i * twr - f1r * twi
            w3 = jnp.concatenate([jnp.concatenate([wr, -wi], axis=1),
                                  jnp.concatenate([wi, wr], axis=1)], axis=0).astype(BF16)
            tn = a_ref[pl.ds(pl.multiple_of(j * 2 * n1, 2 * n1), 2 * n1), :].astype(BF16)
            y = _dot(w3, tn)
            o_ref[0, j] = y[:nd]
            o_ref[1, j] = y[nd:]
            return carry

        lax.fori_loop(0, n2, body, 0)


def _fft_consts(n1):
    n2 = FFT_N2
    n = n1 * n2
    nd = n1 // 2
    k1 = np.arange(n1)
    f1 = np.exp(-2j * np.pi * np.outer(k1, np.arange(nd)) / n1)
    w1 = np.block([[f1.real, -f1.imag], [f1.imag, f1.real]])
    f2 = np.exp(-2j * np.pi * np.outer(np.arange(n2), np.arange(n2)) / n2)
    f2c = np.conj(f2)
    g2i = np.block([[f2c.real, -f2c.imag], [f2c.imag, f2c.real]])
    f1c = np.exp(2j * np.pi * np.outer(np.arange(nd), k1) / n1) / n
    tw = np.exp(-2j * np.pi * np.outer(k1, np.arange(n2)) / n)
    cplx = lambda m: jnp.asarray(np.stack([m.real, m.imag]), F32)
    return (jnp.asarray(w1, BF16), cplx(f2), jnp.asarray(g2i, BF16), cplx(f1c), cplx(tw), cplx(tw.T))


def fft_conv(zt, kf):
    bsz, n2, nd, c = zt.shape
    n1 = 2 * nd
    cb = LANES
    k1c = 8 if n1 % 8 == 0 and n1 >= 64 else 2
    slab = n1 // k1c
    consts = _fft_consts(n1)
    full = lambda a: pl.BlockSpec(a.shape, lambda j, p, s: (0,) * a.ndim)
    blk = pl.BlockSpec((2, n2, nd, cb), lambda j, p, s: (p, 0, 0, j))
    return pl.pallas_call(
        functools.partial(_fft_conv_kernel, n1=n1, k1c=k1c),
        grid=(c // cb, bsz // 2, k1c + 2),
        in_specs=[blk,
                  pl.BlockSpec((2, slab, n2, cb), lambda j, p, s: (0, jnp.clip(s - 1, 0, k1c - 1), 0, j))]
                 + [full(a) for a in consts],
        out_specs=blk,
        out_shape=jax.ShapeDtypeStruct(zt.shape, F32),
        scratch_shapes=[pltpu.VMEM((n2 * 2 * n1, cb), F32)],
        compiler_params=_cparams(("parallel", "parallel", "arbitrary")),
        name="fft_conv",
    )(zt, kf, *consts)


def _dft_conv_kernel(z_ref, kf_ref, fw_ref, iv_ref, o_ref, *, t):
    x = jnp.concatenate([z_ref[0], z_ref[1]], axis=0)
    spec = _dot(fw_ref[...], x)
    xr, xi = spec[:2 * t], spec[2 * t:]
    kr, ki = kf_ref[0], kf_ref[1]
    pr = xr * kr - xi * ki
    pi = xr * ki + xi * kr
    y = _dot(iv_ref[...], jnp.concatenate([pr, pi], axis=0).astype(BF16))
    o_ref[0] = y[:t]
    o_ref[1] = y[t:]


def dft_conv(zb, kf):
    bsz, t, c = zb.shape
    n = 2 * t
    f = np.exp(-2j * np.pi * np.outer(np.arange(n), np.arange(t)) / n)
    fw = np.block([[f.real, -f.imag], [f.imag, f.real]])
    fi = np.exp(2j * np.pi * np.outer(np.arange(t), np.arange(n)) / n) / n
    iv = np.block([[fi.real, -fi.imag], [fi.imag, fi.real]])
    fw, iv = jnp.asarray(fw, BF16), jnp.asarray(iv, BF16)
    cb = LANES
    blk = pl.BlockSpec((2, t, cb), lambda j, p: (p, 0, j))
    return pl.pallas_call(
        functools.partial(_dft_conv_kernel, t=t),
        grid=(c // cb, bsz // 2),
        in_specs=[blk, pl.BlockSpec((2, n, cb), lambda j, p: (0, 0, j)),
                  pl.BlockSpec(fw.shape, lambda j, p: (0, 0)), pl.BlockSpec(iv.shape, lambda j, p: (0, 0))],
        out_specs=blk,
        out_shape=jax.ShapeDtypeStruct(zb.shape, F32),
        compiler_params=_cparams(("parallel", "parallel")),
        name="dft_conv",
    )(zb, kf, fw, iv)


def _hy_gate_kernel(conv_ref, z_ref, gate_ref, bias_ref, o_ref, ob_ref):
    y = gate_ref[0] * (conv_ref[0] + bias_ref[...] * z_ref[0])
    o_ref[0] = y
    ob_ref[0] = y.astype(BF16)


def hyena_gate(conv, z, gate, bias):
    bsz, t, c = z.shape
    tm = min(512, t)
    tok = pl.BlockSpec((1, tm, c), lambda b, i: (b, i, 0))
    return pl.pallas_call(
        _hy_gate_kernel,
        grid=(bsz, t // tm),
        in_specs=[tok, tok, tok, pl.BlockSpec((1, c), lambda b, i: (0, 0))],
        out_specs=[tok, tok],
        out_shape=[jax.ShapeDtypeStruct(z.shape, F32), jax.ShapeDtypeStruct(z.shape, BF16)],
        compiler_params=_cparams(("parallel", "parallel")),
        name="hyena_gate",
    )(conv, z, gate, bias.reshape(1, c))


def _hy_out_kernel(x_ref, conv_ref, z_ref, gate_ref, bias_ref, w_ref, m5_ref, o_ref):
    y = gate_ref[0] * (conv_ref[0] + bias_ref[...] * z_ref[0])
    o_ref[0] = x_ref[0] + m5_ref[0] * _dot(y.astype(BF16), w_ref[...])


def hyena_out(x, conv, z, gate, bias, w, m5):
    bsz, t, d = x.shape
    c = z.shape[2]
    tm = min(512, t)
    tokd = pl.BlockSpec((1, tm, d), lambda b, i: (b, i, 0))
    tokc = pl.BlockSpec((1, tm, c), lambda b, i: (b, i, 0))
    return pl.pallas_call(
        _hy_out_kernel,
        grid=(bsz, t // tm),
        in_specs=[tokd, tokc, tokc, tokc, pl.BlockSpec((1, c), lambda b, i: (0, 0)),
                  pl.BlockSpec(w.shape, lambda b, i: (0, 0)), pl.BlockSpec((1, 1, d), lambda b, i: (b, 0, 0))],
        out_specs=tokd,
        out_shape=jax.ShapeDtypeStruct(x.shape, F32),
        compiler_params=_cparams(("parallel", "parallel")),
        name="hyena_out",
    )(x, conv, z, gate, bias.reshape(1, c), w, m5)


def _hyena_filter_spectrum(t, fw1, fb1, fw2, fb2, fw3, fb3, fw4, freq, width):
    pos = jnp.arange(t, dtype=F32)[:, None]
    tt = pos / max(t - 1, 1)
    ang = 2 * math.pi * pos / t
    nb = (HY_EMB - 1) // 2
    bands = jnp.linspace(1e-4, nb - 1, nb, dtype=F32)[None]
    feats = jnp.concatenate([tt, jnp.cos(bands * ang), -jnp.sin(bands * ang)], axis=-1)
    hdn = jnp.sin(freq * (feats @ fw1 + fb1))
    hdn = jnp.sin(freq * (hdn @ fw2 + fb2))
    hdn = jnp.sin(freq * (hdn @ fw3 + fb3))
    filt = (hdn @ fw4).reshape(t, HY_ORDER, 2, width)
    deltas = jnp.abs(jnp.linspace(HY_MIN_DECAY, HY_MAX_DECAY, width, dtype=F32))
    filt = filt * jnp.exp(-tt[:, :, None, None] * deltas)
    fwd, bwd = filt[:, :, 0], filt[:, :, 1]
    kern = jnp.concatenate([fwd, jnp.zeros_like(fwd[:1]), bwd[:0:-1]], axis=0)
    kern = kern * lax.rsqrt(jnp.sum(kern * kern, axis=0, keepdims=True) + 1e-6)
    spec = jnp.fft.fft(kern, axis=0)
    return jnp.stack([jnp.real(spec), jnp.imag(spec)], axis=1).transpose(2, 1, 0, 3).astype(F32)


def _block_diag_pair(m):
    z = jnp.zeros_like(m[0])
    return jnp.concatenate([jnp.concatenate([m[0], z], axis=1), jnp.concatenate([z, m[1]], axis=1)], axis=0)


def _time_major(seq_ctx, seq_lat):
    fwd = jnp.concatenate([seq_ctx[0], seq_lat[0]], axis=1)
    bwd = jnp.concatenate([seq_ctx[1][:, ::-1], seq_lat[1][:, ::-1]], axis=1)
    return jnp.stack([fwd, bwd], axis=0)


def _from_time_major(x, tc):
    f_ctx, f_lat = x[0, :, :tc], x[0, :, tc:]
    b_ctx, b_lat = x[1, :, :tc][:, ::-1], x[1, :, tc:][:, ::-1]
    return (f_ctx, b_ctx), (f_lat, b_lat)


def _even_mixer(lat, cx, ml, mc, gain, prm, ctx_out):
    (w_in, mu, w0, w_up, a0, a_up, g_up, k_k, k_a, r_k, gn_g, gn_b,
     lam_re, lam_im, log_dt, b_re, b_im, c_re, c_im, d_skip, w_glu, b_glu, w_out) = prm
    bsz, t_lat, d = lat.shape
    t_ctx = cx.shape[1]
    width = k_k.shape[0]
    heads = width // HEAD_DIM
    n_cols = mu.shape[0]
    s5w = d_skip.shape[0]

    w_in_b = w_in.astype(BF16)
    p_lat = mod_matmul(lat, gain, ml[3], ml[4], w_in_b)
    p_ctx = mod_matmul(cx, gain, mc[3], mc[4], w_in_b)

    head_of = np.arange(width) // HEAD_DIM
    ones = jnp.asarray(head_of[:, None] == head_of[None, :], F32)
    sel = np.zeros((4 * width, LANES), np.float32)
    for qn in range(4):
        sel[qn * width + np.arange(width), qn * heads + head_of] = 1.0
    feat_consts = (mu.reshape(1, -1), _block_diag_pair(w_up).astype(BF16), w0.reshape(1, -1),
                   _block_diag_pair(a_up).astype(BF16), a0.reshape(1, -1), g_up.astype(BF16),
                   k_k.reshape(1, -1), k_a.reshape(1, -1), r_k.reshape(1, -1), ones, jnp.asarray(sel))
    f_lat = rwkv_features(p_lat, n_cols, width, feat_consts, latent=True)
    f_ctx = rwkv_features(p_ctx, n_cols, width, feat_consts, latent=False)
    kk_l, v_l, g_l, bonus_l, wr_l, w_l, kka_l, kd_l, c_l = f_lat
    kk_c, v_c, g_c, bonus_c, wr_c, w_c, kka_c, kd_c, c_c = f_ctx

    ttot = t_ctx + t_lat
    nchain = 2 * bsz * heads

    def chains_j(xc, xl):
        x = _time_major(xc, xl).reshape(2, bsz, ttot, heads, HEAD_DIM)
        return x.transpose(2, 4, 0, 1, 3).reshape(ttot, HEAD_DIM, nchain)

    def chains_i(xc, xl):
        x = _time_major(xc, xl).reshape(2, bsz, ttot, heads, 2, HEAD_DIM // 2)
        return x.transpose(2, 5, 4, 0, 1, 3).reshape(ttot, HEAD_DIM // 2, 2 * nchain)

    both = lambda a: (a, a)
    cc = _time_major((c_c[..., :2 * heads], c_c[..., 2 * heads:4 * heads]),
                     (c_l[..., :2 * heads], c_l[..., 2 * heads:4 * heads]))
    cc = cc.reshape(2, bsz, ttot, 2, heads).transpose(3, 2, 0, 1, 4).reshape(2, ttot, nchain)
    cc = jnp.concatenate([cc, cc], axis=-1)
    o = rwkv_scan(chains_j(both(kk_c), both(kk_l)), chains_j(wr_c, wr_l), chains_j(w_c, w_l),
                  chains_j(kka_c, kka_l), chains_j(kd_c, kd_l), chains_i(both(v_c), both(v_l)),
                  cc[0], cc[1])
    o = o.reshape(ttot, HEAD_DIM // 2, 2, 2, bsz, heads).transpose(3, 4, 0, 5, 2, 1)
    (oc_f, oc_b), (ol_f, ol_b) = _from_time_major(o.reshape(2, bsz, ttot, width), t_ctx)

    ng = lam_re.shape[1]
    gps = ng // S5_SUPER
    lam = lax.complex(lam_re, lam_im)
    dt = jnp.exp(log_dt)[..., None]
    a_bar = jnp.exp(lam * dt)
    b_bar = ((a_bar - 1) / lam)[..., None] * lax.complex(b_re, b_im)
    eye = jnp.eye(gps, dtype=F32)

    def b_mat(x):
        x = x.reshape(2, S5_SUPER, gps, S5_STATE, S5_GROUP)
        m = jnp.einsum('dsgph,gk->sdghkp', x, eye)
        return m.reshape(S5_SUPER, 2 * gps * S5_GROUP, gps * S5_STATE)

    def c_mat(x):
        x = x.reshape(2, S5_SUPER, gps, S5_GROUP, S5_STATE)
        m = jnp.einsum('dsghp,gk->skpdgh', x, eye)
        return m.reshape(S5_SUPER, gps * S5_STATE, 2 * gps * S5_GROUP)

    bm = jnp.concatenate([b_mat(jnp.real(b_bar)), b_mat(jnp.imag(b_bar))], axis=2).astype(BF16)
    cm = jnp.concatenate([c_mat(c_re), -c_mat(c_im)], axis=1).astype(BF16)

    def a_rows(x):
        x = x.reshape(2, 1, S5_SUPER, gps * S5_STATE)
        return jnp.broadcast_to(x, (2, bsz, S5_SUPER, gps * S5_STATE)).reshape(2 * bsz, S5_SUPER, -1)

    a_arr = jnp.concatenate([a_rows(jnp.real(a_bar)), a_rows(jnp.imag(a_bar))], axis=2)
    a_arr = a_arr.reshape(1, 2 * bsz, -1)

    u_l, u_c = p_lat[..., n_cols:], p_ctx[..., n_cols:]
    u_tm = _time_major(both(u_c), both(u_l)).reshape(2, bsz, ttot, S5_SUPER, gps * S5_GROUP)
    zeros = jnp.zeros_like(u_tm[0])
    u_in = jnp.stack([jnp.concatenate([u_tm[0], zeros], axis=-1),
                      jnp.concatenate([zeros, u_tm[1]], axis=-1)], axis=0)
    u_in = u_in.transpose(2, 0, 1, 3, 4).reshape(ttot, 2 * bsz, -1).astype(BF16)
    ys = s5_scan(u_in, bm, cm, a_arr)
    ys = ys.reshape(ttot, 2, bsz, s5w).transpose(1, 2, 0, 3)
    (yc_f, yc_b), (yl_f, yl_b) = _from_time_major(ys, t_ctx)

    wo = w_out.astype(BF16)
    out_consts = (ones, gn_g.reshape(1, -1), gn_b.reshape(1, -1), d_skip.reshape(1, -1),
                  w_glu.astype(BF16), b_glu.reshape(1, -1), wo[:width], wo[width:])
    lat = even_out(lat, ol_f, ol_b, bonus_l, g_l, yl_f + yl_b, u_l, ml[5], out_consts)
    if ctx_out:
        cx = even_out(cx, oc_f, oc_b, bonus_c, g_c, yc_f + yc_b, u_c, mc[5], out_consts)
    return lat, cx


def _hyena_mixer(x, gain, m, prm):
    (w_in, conv_w, conv_b, fw1, fb1, fw2, fb2, fw3, fb3, fw4, freq, bias_d, w_out) = prm
    bsz, t, d = x.shape
    c = w_out.shape[0]
    kf = _hyena_filter_spectrum(t, fw1, fb1, fw2, fb2, fw3, fb3, fw4, freq, c)
    z, zb, g1, g2 = hyena_in(x, gain, m[3], m[4], w_in.astype(BF16), conv_w, conv_b)
    gates = (g1, g2)
    use_fft = t % (FFT_N2 * 2) == 0 and t >= 4 * FFT_N2
    for n in range(HY_ORDER):
        if use_fft:
            nd = t // FFT_N2
            n1 = 2 * nd
            kfn = kf[n].reshape(2, FFT_N2, n1, c).transpose(0, 2, 1, 3)
            zt = zb.reshape(bsz, nd, FFT_N2, c).transpose(0, 2, 1, 3)
            conv = fft_conv(zt, kfn).transpose(0, 2, 1, 3).reshape(bsz, t, c)
        else:
            conv = dft_conv(zb, kf[n])
        if n < HY_ORDER - 1:
            z, zb = hyena_gate(conv, z, gates[n], bias_d[n])
        else:
            return hyena_out(x, conv, z, gates[n], bias_d[n], w_out.astype(BF16), m[5])


def kernel(x, c, ctx, c_ctx, norm_g, ada_w, ada_b, ffn_wg, ffn_wu, ffn_wd, final_g, ev_w_in, ev_mu, ev_w0, ev_w_up, ev_a0, ev_a_up, ev_g_up, ev_k_k, ev_k_a, ev_r_k, ev_gn_g, ev_gn_b, ev_lam_re, ev_lam_im, ev_log_dt, ev_b_re, ev_b_im, ev_c_re, ev_c_im, ev_d, ev_w_glu, ev_b_glu, ev_w_out, od_w_in, od_conv_w, od_conv_b, od_fw1, od_fb1, od_fw2, od_fb2, od_fw3, od_fb3, od_fw4, od_freq, od_bias, od_w_out):
    depth = norm_g.shape[0]
    bsz, _, d = x.shape
    n_even = (depth + 1) // 2
    last_ctx = 2 * (n_even - 1)

    cond8 = jnp.concatenate([c, c_ctx[None], jnp.zeros((8 - bsz - 1, d), F32)], axis=0)
    mods = ada_mods_all(cond8, ada_w, ada_b)

    wg, wu, wd = ffn_wg.astype(BF16), ffn_wu.astype(BF16), ffn_wd.astype(BF16)
    lat, cx = x, ctx
    for l in range(depth):
        run_ctx = l <= last_ctx
        ctx_out = l < last_ctx
        i = l // 2
        ml = [mods[l, :bsz, None, k * d:(k + 1) * d] for k in range(N_MOD)]
        mc = [jnp.broadcast_to(mods[l, bsz:bsz + 1, None, k * d:(k + 1) * d], (bsz, 1, d))
              for k in range(N_MOD)]
        lat = ffn_half(lat, norm_g[l, 0], ml[0], ml[1], ml[2], wg[l, 0], wu[l, 0], wd[l, 0])
        if run_ctx:
            cx = ffn_half(cx, norm_g[l, 0], mc[0], mc[1], mc[2], wg[l, 0], wu[l, 0], wd[l, 0])
        if l % 2 == 0:
            prm = (ev_w_in[i], ev_mu[i], ev_w0[i], ev_w_up[i], ev_a0[i], ev_a_up[i], ev_g_up[i],
                   ev_k_k[i], ev_k_a[i], ev_r_k[i], ev_gn_g[i], ev_gn_b[i],
                   ev_lam_re[i], ev_lam_im[i], ev_log_dt[i], ev_b_re[i], ev_b_im[i], ev_c_re[i], ev_c_im[i],
                   ev_d[i], ev_w_glu[i], ev_b_glu[i], ev_w_out[i])
            lat, cx = _even_mixer(lat, cx, ml, mc, norm_g[l, 1], prm, ctx_out)
        else:
            prm = (od_w_in[i], od_conv_w[i], od_conv_b[i], od_fw1[i], od_fb1[i], od_fw2[i], od_fb2[i],
                   od_fw3[i], od_fb3[i], od_fw4[i], od_freq[i], od_bias[i], od_w_out[i])
            lat = _hyena_mixer(lat, norm_g[l, 1], ml, prm)
            if ctx_out:
                cx = _hyena_mixer(cx, norm_g[l, 1], mc, prm)
        fin = final_g if l == depth - 1 else None
        lat = ffn_half(lat, norm_g[l, 2], ml[6], ml[7], ml[8], wg[l, 1], wu[l, 1], wd[l, 1], fin)
        if ctx_out:
            cx = ffn_half(cx, norm_g[l, 2], mc[6], mc[7], mc[8], wg[l, 1], wu[l, 1], wd[l, 1])
    return lat
```

```python
import functools
import math

import numpy as np
import jax
import jax.numpy as jnp
from jax import lax
from jax.experimental import pallas as pl
from jax.experimental.pallas import tpu as pltpu

F32 = jnp.float32
BF16 = jnp.bfloat16
HIGHEST = lax.Precision.HIGHEST

N_MOD = 9
NORM_EPS = 1e-6
GN_EPS = 64e-5
GRID_W = 64
HEAD_DIM = 64
HALF_HEAD = HEAD_DIM // 2
S5_GROUP = 16
S5_STATE = 64
S5_SUPER = 4
HY_ORDER = 2
HY_EMB = 33
HY_MIN_DECAY = math.log(1e-2) / 1.5
HY_MAX_DECAY = math.log(1e-2) / 0.3
LANES = 128
SUBLANES = 8
FFT_N2 = 128
VMEM_LIMIT = 56 * 1024 * 1024


def _cparams(sem, vmem=VMEM_LIMIT):
    return pltpu.CompilerParams(dimension_semantics=sem, vmem_limit_bytes=vmem)


def _dot(a, b):
    return jnp.dot(a, b, preferred_element_type=F32)


def _dot_exact(a, b):
    return jnp.dot(a, b, preferred_element_type=F32, precision=HIGHEST)


def _rms_mod(x, gain, shift, scale):
    ms = jnp.mean(x * x, axis=-1, keepdims=True)
    return x * lax.rsqrt(ms + NORM_EPS) * gain * (1.0 + scale) + shift


def _sigmoid(x):
    return 1.0 / (1.0 + jnp.exp(-x))


def _silu(x):
    return x * _sigmoid(x)


def _ada_kernel(c_ref, w_ref, b_ref, o_ref):
    s = _silu(c_ref[...])
    o_ref[0] = _dot(s.astype(BF16), w_ref[0].astype(BF16)) + b_ref[0]


def ada_mods_all(cond8, ada_w, ada_b):
    depth, d, n = ada_w.shape
    tn = n // 8
    return pl.pallas_call(
        _ada_kernel,
        grid=(depth, n // tn),
        in_specs=[pl.BlockSpec((8, d), lambda l, j: (0, 0)),
                  pl.BlockSpec((1, d, tn), lambda l, j: (l, 0, j)),
                  pl.BlockSpec((1, 1, tn), lambda l, j: (l, 0, j))],
        out_specs=pl.BlockSpec((1, 8, tn), lambda l, j: (l, 0, j)),
        out_shape=jax.ShapeDtypeStruct((depth, 8, n), F32),
        compiler_params=_cparams(("parallel", "parallel")),
        name="ada_mods",
    )(cond8, ada_w, ada_b.reshape(depth, 1, n))


def _ffn_kernel(x_ref, gain_ref, shift_ref, scale_ref, gate_ref, wg_ref, wu_ref, wd_ref, fg_ref,
                o_ref, h_ref, acc_ref, *, final_norm):
    k = pl.program_id(2)

    @pl.when(k == 0)
    def _():
        h = _rms_mod(x_ref[0], gain_ref[...], shift_ref[0], scale_ref[0])
        h_ref[...] = h.astype(BF16)
        acc_ref[...] = jnp.zeros_like(acc_ref)

    h = h_ref[...]
    g = _dot(h, wg_ref[...])
    u = _dot(h, wu_ref[...])
    a = (_silu(g) * u).astype(BF16)
    acc_ref[...] += _dot(a, wd_ref[...])

    @pl.when(k == pl.num_programs(2) - 1)
    def _():
        y = x_ref[0] + 0.5 * gate_ref[0] * acc_ref[...]
        if final_norm:
            ms = jnp.mean(y * y, axis=-1, keepdims=True)
            y = y * lax.rsqrt(ms + NORM_EPS) * fg_ref[...]
        o_ref[0] = y


def ffn_half(x, gain, shift, scale, gate, wg, wu, wd, final_g=None):
    bsz, t, d = x.shape
    ff = wg.shape[1]
    tm = min(512, t)
    tf = ff // 2
    fg = jnp.ones((1, d), F32) if final_g is None else final_g.reshape(1, d)
    vec = pl.BlockSpec((1, 1, d), lambda b, i, k: (b, 0, 0))
    return pl.pallas_call(
        functools.partial(_ffn_kernel, final_norm=final_g is not None),
        grid=(bsz, t // tm, ff // tf),
        in_specs=[pl.BlockSpec((1, tm, d), lambda b, i, k: (b, i, 0)),
                  pl.BlockSpec((1, d), lambda b, i, k: (0, 0)),
                  vec, vec, vec,
                  pl.BlockSpec((d, tf), lambda b, i, k: (0, k)),
                  pl.BlockSpec((d, tf), lambda b, i, k: (0, k)),
                  pl.BlockSpec((tf, d), lambda b, i, k: (k, 0)),
                  pl.BlockSpec((1, d), lambda b, i, k: (0, 0))],
        out_specs=pl.BlockSpec((1, tm, d), lambda b, i, k: (b, i, 0)),
        out_shape=jax.ShapeDtypeStruct((bsz, t, d), F32),
        scratch_shapes=[pltpu.VMEM((tm, d), BF16), pltpu.VMEM((tm, d), F32)],
        compiler_params=_cparams(("parallel", "parallel", "arbitrary")),
        name="ffn_half",
    )(x, gain.reshape(1, d), shift, scale, gate, wg, wu, wd, fg)


def _modmm_kernel(cx_ref, lat_ref, gain_ref, shc_ref, scc_ref, shl_ref, scl_ref, w_ref, o_ref, *, nc):
    is_ctx = pl.program_id(1) < nc
    x = jnp.where(is_ctx, cx_ref[0], lat_ref[0])
    shift = jnp.where(is_ctx, shc_ref[0], shl_ref[0])
    scale = jnp.where(is_ctx, scc_ref[0], scl_ref[0])
    h = _rms_mod(x, gain_ref[...], shift, scale)
    o_ref[0] = _dot(h.astype(BF16), w_ref[...])


def mod_matmul_stream(cx, lat, gain, mc, ml, w, tm):
    bsz, t_ctx, d = cx.shape
    t_lat = lat.shape[1]
    n = w.shape[1]
    nc = t_ctx // tm
    nt = nc + t_lat // tm
    vec = pl.BlockSpec((1, 1, d), lambda b, i: (b, 0, 0))
    return pl.pallas_call(
        functools.partial(_modmm_kernel, nc=nc),
        grid=(bsz, nt),
        in_specs=[pl.BlockSpec((1, tm, d), lambda b, i: (b, jnp.minimum(i, nc - 1), 0)),
                  pl.BlockSpec((1, tm, d), lambda b, i: (b, jnp.maximum(i - nc, 0), 0)),
                  pl.BlockSpec((1, d), lambda b, i: (0, 0)),
                  vec, vec, vec, vec,
                  pl.BlockSpec((d, n), lambda b, i: (0, 0))],
        out_specs=pl.BlockSpec((1, tm, n), lambda b, i: (b, i, 0)),
        out_shape=jax.ShapeDtypeStruct((bsz, t_ctx + t_lat, n), F32),
        compiler_params=_cparams(("parallel", "parallel")),
        name="mod_matmul",
    )(cx, lat, gain.reshape(1, d), mc[3], mc[4], ml[3], ml[4], w)


def _rwkv_feat_kernel(p_ref, pu_ref, pd_ref, mu_ref, wup_ref, w0_ref, aup_ref, a0_ref, gup_ref,
                      kk_s_ref, ka_ref, rk_ref, ones_ref,
                      kk_o, v_o, g_o, bonus_o, r_o, k_o, w_o, a_o, *, nt, width):
    i = pl.program_id(1)
    is_ctx = i == 0
    p = p_ref[0]
    tm = p.shape[0]
    row = lax.broadcasted_iota(jnp.int32, p.shape, 0)
    lane = lax.broadcasted_iota(jnp.int32, p.shape, 1) % 4
    prev = pltpu.roll(p, 1, 0)
    nxt = pltpu.roll(p, tm - 1, 0)
    col = jnp.where(is_ctx, row, row % GRID_W)
    last = jnp.where(is_ctx, tm - 1, GRID_W - 1)
    left = jnp.where(col == 0, 0.0, prev)
    right = jnp.where(col == last, 0.0, nxt)
    up_halo = jnp.where(i > 1, pu_ref[0], 0.0)
    dn_halo = jnp.where(i < nt - 1, pd_ref[0], 0.0)
    up = jnp.where(is_ctx, left, jnp.concatenate([up_halo, p[:tm - GRID_W]], axis=0))
    down = jnp.where(is_ctx, right, jnp.concatenate([p[GRID_W:], dn_halo], axis=0))
    shifted = jnp.where(lane == 0, left, jnp.where(lane == 1, right, jnp.where(lane == 2, up, down)))
    q = p + mu_ref[...] * (shifted - p)

    w = width
    r, k, v = q[:, :w], q[:, w:2 * w], q[:, 2 * w:3 * w]
    wd = q[:, 3 * w:3 * w + LANES]
    ad = q[:, 3 * w + LANES:3 * w + 2 * LANES]
    gd = q[:, 3 * w + 2 * LANES:3 * w + 3 * LANES]

    zlin = w0_ref[...] + _dot(jnp.tanh(wd).astype(BF16), wup_ref[...])
    neg = -zlin
    softplus = jnp.maximum(neg, 0.0) + jnp.log(1.0 + jnp.exp(-jnp.abs(neg)))
    decay = jnp.exp(-jnp.exp(-softplus - 0.5))
    a = _sigmoid(a0_ref[...] + _dot(ad.astype(BF16), aup_ref[...]))
    g_o[0] = _dot(_sigmoid(gd).astype(BF16), gup_ref[...])

    ones = ones_ref[...]
    kk = k * kk_s_ref[...]
    ss = _dot_exact(kk * kk, ones)
    kk_o[0] = kk / jnp.maximum(jnp.sqrt(ss), 1e-12)
    v_o[0] = v
    r_o[0] = r
    k_o[0] = k
    ksum = jnp.zeros_like(k)
    for d in range(2):
        a_d = a[:, d * w:(d + 1) * w]
        ksum = ksum + k * (1.0 + (a_d - 1.0) * ka_ref[...])
        w_o[d, 0] = decay[:, d * w:(d + 1) * w]
        a_o[d, 0] = a_d
    bonus_o[0] = _dot_exact(r * ksum * rk_ref[...], ones) * v


def rwkv_features(p, n_cols, width, consts, tm):
    bsz, t, _ = p.shape
    nt = t // tm
    hb = tm // GRID_W
    nhb = t // GRID_W
    w = width
    full = lambda a: pl.BlockSpec(a.shape, lambda b, i: (0,) * a.ndim)
    tok = pl.BlockSpec((1, tm, w), lambda b, i: (b, i, 0))
    tok2 = pl.BlockSpec((2, 1, tm, w), lambda b, i: (0, b, i, 0))
    sds = jax.ShapeDtypeStruct((bsz, t, w), F32)
    sds2 = jax.ShapeDtypeStruct((2, bsz, t, w), F32)
    return pl.pallas_call(
        functools.partial(_rwkv_feat_kernel, nt=nt, width=w),
        grid=(bsz, nt),
        in_specs=[pl.BlockSpec((1, tm, n_cols), lambda b, i: (b, i, 0)),
                  pl.BlockSpec((1, GRID_W, n_cols), lambda b, i: (b, jnp.maximum(i * hb - 1, 0), 0)),
                  pl.BlockSpec((1, GRID_W, n_cols), lambda b, i: (b, jnp.minimum((i + 1) * hb, nhb - 1), 0))]
                 + [full(a) for a in consts],
        out_specs=[tok] * 6 + [tok2, tok2],
        out_shape=[sds] * 6 + [sds2, sds2],
        compiler_params=_cparams(("parallel", "parallel")),
        name="rwkv_features",
    )(p, p, p, *consts)


def _rwkv_scan_kernel(rf, rb, kkf, kkb, kf, kb, vf, vb, w0, w1, a0, a1, ka_ref, of_ref, ob_ref,
                      s_ref, vec_ref, v_ref, c_ref, *, tc):
    @pl.when(pl.program_id(0) == 0)
    def _():
        s_ref[...] = jnp.zeros_like(s_ref)

    nj, ni = s_ref.shape[0], s_ref.shape[1]
    quarter = LANES // 4

    def fold(x):
        hi = (lax.broadcasted_iota(jnp.int32, x.shape, 1) % (2 * quarter)) >= quarter
        return x + jnp.where(hi, pltpu.roll(x, quarter, 1), pltpu.roll(x, 3 * quarter, 1))

    def jsum(x):
        s = jnp.broadcast_to(jnp.sum(x, axis=0, keepdims=True), (SUBLANES, LANES))
        return fold(s)[0:1]

    ka = ka_ref[...]
    for t in range(tc):
        merged = lambda f, b: jnp.concatenate([f[t], b[tc - 1 - t]], axis=-1)
        r, kk, k = merged(rf, rb), merged(kkf, kkb), merged(kf, kb)
        w, a = merged(w0, w1), merged(a0, a1)
        kka = kk * a
        kd = k * (1.0 + (a - 1.0) * ka)
        vec_ref[0, t] = kk
        vec_ref[1, t] = w * r
        vec_ref[2, t] = w
        vec_ref[3, t] = kka
        vec_ref[4, t] = kd
        c_ref[t, 0:1, :] = jsum(kka * r)
        c_ref[t, 1:2, :] = jsum(kd * r)
        v_ref[t] = merged(vf, vb)

    def step(t, carry):
        sa = jnp.zeros((ni, LANES), F32)
        o1 = jnp.zeros((ni, LANES), F32)
        for j in range(nj):
            sj = s_ref[j]
            sa = sa + sj * vec_ref[0, t, j:j + 1, :]
            o1 = o1 + sj * vec_ref[1, t, j:j + 1, :]
        sa = fold(sa)
        o1 = fold(o1)
        v = v_ref[t]
        out = o1 - sa * c_ref[t, 0:1, :] + v * c_ref[t, 1:2, :]
        of_ref[t] = out[:, :LANES // 2]
        ob_ref[tc - 1 - t] = out[:, LANES // 2:]
        for j in range(nj):
            s_ref[j] = (s_ref[j] * vec_ref[2, t, j:j + 1, :] - sa * vec_ref[3, t, j:j + 1, :]
                        + v * vec_ref[4, t, j:j + 1, :])
        return carry

    lax.fori_loop(0, tc, step, 0)


def rwkv_scan(r, kk, k, v, w0, w1, a0, a1, ka, t_ctx):
    ttot, nj, nc = r.shape
    ni = v.shape[1]
    tc = 32
    ncb, ntb = t_ctx // tc, ttot // tc
    fwd = lambda g: (g, 0, 0)
    bwd = lambda g: (jnp.where(g < ncb, ncb - 1 - g, ntb + ncb - 1 - g), 0, 0)
    jf, jb = pl.BlockSpec((tc, nj, nc), fwd), pl.BlockSpec((tc, nj, nc), bwd)
    vf, vb = pl.BlockSpec((tc, ni, nc), fwd), pl.BlockSpec((tc, ni, nc), bwd)
    osd = jax.ShapeDtypeStruct((ttot, ni, nc), F32)
    return pl.pallas_call(
        functools.partial(_rwkv_scan_kernel, tc=tc),
        grid=(ntb,),
        in_specs=[jf, jb, jf, jb, jf, jb, vf, vb, jf, jb, jf, jb, pl.BlockSpec(ka.shape, lambda g: (0, 0))],
        out_specs=[vf, vb],
        out_shape=[osd, osd],
        scratch_shapes=[pltpu.VMEM((nj, ni, LANES), F32), pltpu.VMEM((5, tc, nj, LANES), F32),
                        pltpu.VMEM((tc, ni, LANES), F32), pltpu.VMEM((tc, SUBLANES, LANES), F32)],
        compiler_params=_cparams(("arbitrary",)),
        name="rwkv_scan",
    )(r, r, kk, kk, k, k, v, v, w0, w1, a0, a1, ka)


def _s5_kernel(uf_ref, ub_ref, bm_ref, cm_ref, a_ref, yf_ref, yb_ref, h_ref, bu_ref, ubr_ref, *, tt):
    @pl.when(pl.program_id(0) == 0)
    def _():
        h_ref[...] = jnp.zeros_like(h_ref)

    nsb = bm_ref.shape[0]
    kin = bm_ref.shape[1] // 2
    sw = bm_ref.shape[2]
    hw = sw // 2
    for t in range(tt):
        ubr_ref[t] = ub_ref[tt - 1 - t]
    fwd_row3 = lax.broadcasted_iota(jnp.int32, uf_ref.shape, 1) < SUBLANES // 2
    uf = jnp.where(fwd_row3, uf_ref[...], 0.0)
    ub = jnp.where(fwd_row3, 0.0, ubr_ref[...])
    for sb in range(nsb):
        lhs = jnp.concatenate([uf[:, :, sb * kin:(sb + 1) * kin], ub[:, :, sb * kin:(sb + 1) * kin]], axis=-1)
        lhs = lhs.reshape(tt * SUBLANES, 2 * kin).astype(BF16)
        bu_ref[:, sb * sw:(sb + 1) * sw] = _dot(lhs, bm_ref[sb])

    def step(t, hs):
        rows = pl.ds(pl.multiple_of(t * SUBLANES, SUBLANES), SUBLANES)
        out = []
        for sb in range(nsb):
            hr, hi = hs[2 * sb], hs[2 * sb + 1]
            lo = sb * sw
            ar = a_ref[0, :, lo:lo + hw]
            ai = a_ref[0, :, lo + hw:lo + sw]
            nr = ar * hr - ai * hi + bu_ref[rows, lo:lo + hw]
            ni = ar * hi + ai * hr + bu_ref[rows, lo + hw:lo + sw]
            bu_ref[rows, lo:lo + hw] = nr
            bu_ref[rows, lo + hw:lo + sw] = ni
            out += [nr, ni]
        return tuple(out)

    h0 = []
    for sb in range(nsb):
        h0 += [h_ref[:, sb * sw:sb * sw + hw], h_ref[:, sb * sw + hw:(sb + 1) * sw]]
    hs = lax.fori_loop(0, tt, step, tuple(h0))
    for sb in range(nsb):
        h_ref[:, sb * sw:sb * sw + hw] = hs[2 * sb]
        h_ref[:, sb * sw + hw:(sb + 1) * sw] = hs[2 * sb + 1]

    nout = cm_ref.shape[2] // 2
    fwd_row = (lax.broadcasted_iota(jnp.int32, (tt * SUBLANES, nout), 0) % SUBLANES) < SUBLANES // 2
    ys = []
    for sb in range(nsb):
        yy = _dot(bu_ref[:, sb * sw:(sb + 1) * sw].astype(BF16), cm_ref[sb])
        ys.append(jnp.where(fwd_row, yy[:, :nout], yy[:, nout:]))
    y = jnp.concatenate(ys, axis=1).reshape(tt, SUBLANES, nsb * nout)
    yf_ref[...] = y
    for t in range(tt):
        yb_ref[tt - 1 - t] = y[t]


def s5_scan(u, bm, cm, a, t_ctx):
    ttot, rows, width = u.shape
    nsb, _, sw = bm.shape
    tt = 64
    ncb, ntb = t_ctx // tt, ttot // tt
    fwd = pl.BlockSpec((tt, rows, width), lambda g: (g, 0, 0))
    bwd = pl.BlockSpec((tt, rows, width), lambda g: (jnp.where(g < ncb, ncb - 1 - g, ntb + ncb - 1 - g), 0, 0))
    full = lambda x: pl.BlockSpec(x.shape, lambda g: (0,) * x.ndim)
    osd = jax.ShapeDtypeStruct(u.shape, F32)
    return pl.pallas_call(
        functools.partial(_s5_kernel, tt=tt),
        grid=(ntb,),
        in_specs=[fwd, bwd, full(bm), full(cm), full(a)],
        out_specs=[fwd, bwd],
        out_shape=[osd, osd],
        scratch_shapes=[pltpu.VMEM((rows, nsb * sw), F32), pltpu.VMEM((tt * rows, nsb * sw), F32),
                        pltpu.VMEM((tt, rows, width), F32)],
        compiler_params=_cparams(("arbitrary",)),
        name="s5_scan",
    )(u, u, bm, cm, a)


def _even_out_kernel(x_ref, of_ref, ob_ref, bonus_ref, g_ref, ys_ref, u_ref, m5_ref,
                     ones_ref, gng_ref, gnb_ref, dskip_ref, wglu_ref, bglu_ref, wo1_ref, wo2_ref, o_ref):
    ones = ones_ref[...]
    inv = 1.0 / HEAD_DIM
    o = of_ref[0] + ob_ref[0]
    mean = _dot_exact(o, ones) * inv
    oc = o - mean
    var = _dot_exact(oc * oc, ones) * inv
    y1 = (oc * lax.rsqrt(var + GN_EPS) * gng_ref[...] + gnb_ref[...] + bonus_ref[0]) * g_ref[0]
    y = ys_ref[0] + dskip_ref[...] * u_ref[0]
    y = jax.nn.gelu(y)
    y2 = y * _sigmoid(_dot(y.astype(BF16), wglu_ref[...]) + bglu_ref[...])
    out = _dot(y1.astype(BF16), wo1_ref[...]) + _dot(y2.astype(BF16), wo2_ref[...])
    o_ref[0] = x_ref[0] + m5_ref[0] * out


def even_out(x, off, o_f, o_b, bonus, g, ys, u, m5, consts, tm):
    bsz, t, d = x.shape
    w = o_f.shape[2]
    tokd = pl.BlockSpec((1, tm, d), lambda b, i: (b, i, 0))
    tokw = pl.BlockSpec((1, tm, w), lambda b, i: (b, i + off, 0))
    toks = pl.BlockSpec((1, tm, u.shape[2]), lambda b, i: (b, i + off, 0))
    full = lambda a: pl.BlockSpec(a.shape, lambda b, i: (0,) * a.ndim)
    return pl.pallas_call(
        _even_out_kernel,
        grid=(bsz, t // tm),
        in_specs=[tokd, tokw, tokw, tokw, tokw, toks, toks,
                  pl.BlockSpec((1, 1, d), lambda b, i: (b, 0, 0))] + [full(a) for a in consts],
        out_specs=tokd,
        out_shape=jax.ShapeDtypeStruct((bsz, t, d), F32),
        compiler_params=_cparams(("parallel", "parallel")),
        name="even_out",
    )(x, o_f, o_b, bonus, g, ys, u, m5, *consts)


def _hy_in_kernel(x_ref, xp_ref, xn_ref, gain_ref, shift_ref, scale_ref, w_ref, cw_ref, cb_ref,
                  z_ref, zb_ref, g1_ref, g2_ref, *, nt, c):
    i = pl.program_id(1)
    gain, shift, scale = gain_ref[...], shift_ref[0], scale_ref[0]
    h = _rms_mod(x_ref[0], gain, shift, scale).astype(BF16)
    hp = _rms_mod(xp_ref[0], gain, shift, scale).astype(BF16)
    hn = _rms_mod(xn_ref[0], gain, shift, scale).astype(BF16)
    tm = h.shape[0]
    row = lax.broadcasted_iota(jnp.int32, (tm, c), 0)
    outs = (z_ref, g1_ref, g2_ref)
    for part in range(3):
        wp = w_ref[:, part * c:(part + 1) * c]
        p = _dot(h, wp)
        pp = jnp.where(i > 0, _dot(hp, wp)[7:8], 0.0)
        pn = jnp.where(i < nt - 1, _dot(hn, wp)[0:1], 0.0)
        pm1 = jnp.where(row == 0, pp, pltpu.roll(p, 1, 0))
        pp1 = jnp.where(row == tm - 1, pn, pltpu.roll(p, tm - 1, 0))
        cw = cw_ref[:, part * c:(part + 1) * c]
        q = cw[0:1] * pm1 + cw[1:2] * p + cw[2:3] * pp1 + cb_ref[:, part * c:(part + 1) * c]
        outs[part][0] = q
        if part == 0:
            zb_ref[0] = q.astype(BF16)


def hyena_in(x, gain, shift, scale, w, conv_w, conv_b):
    bsz, t, d = x.shape
    c = w.shape[1] // 3
    tm = min(512, t)
    nt = t // tm
    hb = tm // 8
    nhb = t // 8
    vec = pl.BlockSpec((1, 1, d), lambda b, i: (b, 0, 0))
    tok = pl.BlockSpec((1, tm, c), lambda b, i: (b, i, 0))
    sds = jax.ShapeDtypeStruct((bsz, t, c), F32)
    return pl.pallas_call(
        functools.partial(_hy_in_kernel, nt=nt, c=c),
        grid=(bsz, nt),
        in_specs=[pl.BlockSpec((1, tm, d), lambda b, i: (b, i, 0)),
                  pl.BlockSpec((1, 8, d), lambda b, i: (b, jnp.maximum(i * hb - 1, 0), 0)),
                  pl.BlockSpec((1, 8, d), lambda b, i: (b, jnp.minimum((i + 1) * hb, nhb - 1), 0)),
                  pl.BlockSpec((1, d), lambda b, i: (0, 0)),
                  vec, vec,
                  pl.BlockSpec(w.shape, lambda b, i: (0, 0)),
                  pl.BlockSpec(conv_w.shape, lambda b, i: (0, 0)),
                  pl.BlockSpec((1, 3 * c), lambda b, i: (0, 0))],
        out_specs=[tok, tok, tok, tok],
        out_shape=[sds, jax.ShapeDtypeStruct((bsz, t, c), BF16), sds, sds],
        compiler_params=_cparams(("parallel", "parallel")),
        name="hyena_in",
    )(x, x, x, gain.reshape(1, d), shift, scale, w, conv_w, conv_b.reshape(1, 3 * c))


def _fft_conv_kernel(z_ref, kf_ref, w1_ref, f2_ref, g2i_ref, f1c_ref, tw_ref, twt_ref, o_ref, a_ref,
                     *, n1, k1c):
    s = pl.program_id(2)
    n2 = FFT_N2
    nd = n1 // 2
    slab = n1 // k1c

    @pl.when(s == 0)
    def _():
        w1 = w1_ref[...]

        def body(j, carry):
            x = jnp.concatenate([z_ref[0, j], z_ref[1, j]], axis=0)
            a_ref[pl.ds(pl.multiple_of(j * 2 * n1, 2 * n1), 2 * n1), :] = _dot(w1, x)
            return carry

        lax.fori_loop(0, n2, body, 0)

    @pl.when((s > 0) & (s <= k1c))
    def _():
        f2r, f2i = f2_ref[0], f2_ref[1]
        g2i = g2i_ref[...]

        def body(kk, carry):
            k1 = (s - 1) * slab + kk
            twr = tw_ref[0, pl.ds(k1, 1), :]
            twi = tw_ref[1, pl.ds(k1, 1), :]
            gr = f2r * twr - f2i * twi
            gi = f2r * twi + f2i * twr
            gmat = jnp.concatenate([jnp.concatenate([gr, -gi], axis=1),
                                    jnp.concatenate([gi, gr], axis=1)], axis=0).astype(BF16)
            re_rows = pl.ds(k1, n2, stride=2 * n1)
            im_rows = pl.ds(n1 + k1, n2, stride=2 * n1)
            ak = jnp.concatenate([a_ref[re_rows, :], a_ref[im_rows, :]], axis=0).astype(BF16)
            x = _dot(gmat, ak)
            xr, xi = x[:n2], x[n2:]
            kr, ki = kf_ref[0, kk], kf_ref[1, kk]
            pr = xr * kr - xi * ki
            pi = xr * ki + xi * kr
            tt = _dot(g2i, jnp.concatenate([pr, pi], axis=0).astype(BF16))
            a_ref[re_rows, :] = tt[:n2]
            a_ref[im_rows, :] = tt[n2:]
            return carry

        lax.fori_loop(0, slab, body, 0)

    @pl.when(s == k1c + 1)
    def _():
        f1r, f1i = f1c_ref[0], f1c_ref[1]

        def body(j, carry):
            twr = twt_ref[0, pl.ds(j, 1), :]
            twi = twt_ref[1, pl.ds(j, 1), :]
            wr = f1r * twr + f1i * twi
            wi = f1i * twr - f1r * twi
            w3 = jnp.concatenate([jnp.concatenate([wr, -wi], axis=1),
                                  jnp.concatenate([wi, wr], axis=1)], axis=0).astype(BF16)
            tn = a_ref[pl.ds(pl.multiple_of(j * 2 * n1, 2 * n1), 2 * n1), :].astype(BF16)
            y = _dot(w3, tn)
            o_ref[0, j] = y[:nd]
            o_ref[1, j] = y[nd:]
            return carry

        lax.fori_loop(0, n2, body, 0)


def _fft_consts(n1):
    n2 = FFT_N2
    n = n1 * n2
    nd = n1 // 2
    k1 = np.arange(n1)
    f1 = np.exp(-2j * np.pi * np.outer(k1, np.arange(nd)) / n1)
    w1 = np.block([[f1.real, -f1.imag], [f1.imag, f1.real]])
    f2 = np.exp(-2j * np.pi * np.outer(np.arange(n2), np.arange(n2)) / n2)
    f2c = np.conj(f2)
    g2i = np.block([[f2c.real, -f2c.imag], [f2c.imag, f2c.real]])
    f1c = np.exp(2j * np.pi * np.outer(np.arange(nd), k1) / n1) / n
    tw = np.exp(-2j * np.pi * np.outer(k1, np.arange(n2)) / n)
    cplx = lambda m: jnp.asarray(np.stack([m.real, m.imag]), F32)
    return (jnp.asarray(w1, BF16), cplx(f2), jnp.asarray(g2i, BF16), cplx(f1c), cplx(tw), cplx(tw.T))


def fft_conv(zt, kf):
    bsz, n2, nd, c = zt.shape
    n1 = 2 * nd
    cb = LANES
    k1c = 8 if n1 % 8 == 0 and n1 >= 64 else 2
    slab = n1 // k1c
    consts = _fft_consts(n1)
    full = lambda a: pl.BlockSpec(a.shape, lambda j, p, s: (0,) * a.ndim)
    blk = pl.BlockSpec((2, n2, nd, cb), lambda j, p, s: (p, 0, 0, j))
    return pl.pallas_call(
        functools.partial(_fft_conv_kernel, n1=n1, k1c=k1c),
        grid=(c // cb, bsz // 2, k1c + 2),
        in_specs=[blk,
                  pl.BlockSpec((2, slab, n2, cb), lambda j, p, s: (0, jnp.clip(s - 1, 0, k1c - 1), 0, j))]
                 + [full(a) for a in consts],
        out_specs=blk,
        out_shape=jax.ShapeDtypeStruct(zt.shape, F32),
        scratch_shapes=[pltpu.VMEM((n2 * 2 * n1, cb), F32)],
        compiler_params=_cparams(("parallel", "parallel", "arbitrary")),
        name="fft_conv",
    )(zt, kf, *consts)


def _dft_conv_kernel(z_ref, kf_ref, fw_ref, iv_ref, o_ref, *, t):
    x = jnp.concatenate([z_ref[0], z_ref[1]], axis=0)
    spec = _dot(fw_ref[...], x)
    xr, xi = spec[:2 * t], spec[2 * t:]
    kr, ki = kf_ref[0], kf_ref[1]
    pr = xr * kr - xi * ki
    pi = xr * ki + xi * kr
    y = _dot(iv_ref[...], jnp.concatenate([pr, pi], axis=0).astype(BF16))
    o_ref[0] = y[:t]
    o_ref[1] = y[t:]


def dft_conv(zb, kf):
    bsz, t, c = zb.shape
    n = 2 * t
    f = np.exp(-2j * np.pi * np.outer(np.arange(n), np.arange(t)) / n)
    fw = np.block([[f.real, -f.imag], [f.imag, f.real]])
    fi = np.exp(2j * np.pi * np.outer(np.arange(t), np.arange(n)) / n) / n
    iv = np.block([[fi.real, -fi.imag], [fi.imag, fi.real]])
    fw, iv = jnp.asarray(fw, BF16), jnp.asarray(iv, BF16)
    cb = LANES
    blk = pl.BlockSpec((2, t, cb), lambda j, p: (p, 0, j))
    return pl.pallas_call(
        functools.partial(_dft_conv_kernel, t=t),
        grid=(c // cb, bsz // 2),
        in_specs=[blk, pl.BlockSpec((2, n, cb), lambda j, p: (0, 0, j)),
                  pl.BlockSpec(fw.shape, lambda j, p: (0, 0)), pl.BlockSpec(iv.shape, lambda j, p: (0, 0))],
        out_specs=blk,
        out_shape=jax.ShapeDtypeStruct(zb.shape, F32),
        compiler_params=_cparams(("parallel", "parallel")),
        name="dft_conv",
    )(zb, kf, fw, iv)


def _hy_gate_kernel(conv_ref, z_ref, gate_ref, bias_ref, o_ref, ob_ref):
    y = gate_ref[0] * (conv_ref[0] + bias_ref[...] * z_ref[0])
    o_ref[0] = y
    ob_ref[0] = y.astype(BF16)


def hyena_gate(conv, z, gate, bias):
    bsz, t, c = z.shape
    tm = min(512, t)
    tok = pl.BlockSpec((1, tm, c), lambda b, i: (b, i, 0))
    return pl.pallas_call(
        _hy_gate_kernel,
        grid=(bsz, t // tm),
        in_specs=[tok, tok, tok, pl.BlockSpec((1, c), lambda b, i: (0, 0))],
        out_specs=[tok, tok],
        out_shape=[jax.ShapeDtypeStruct(z.shape, F32), jax.ShapeDtypeStruct(z.shape, BF16)],
        compiler_params=_cparams(("parallel", "parallel")),
        name="hyena_gate",
    )(conv, z, gate, bias.reshape(1, c))


def _hy_out_kernel(x_ref, conv_ref, z_ref, gate_ref, bias_ref, w_ref, m5_ref, o_ref):
    y = gate_ref[0] * (conv_ref[0] + bias_ref[...] * z_ref[0])
    o_ref[0] = x_ref[0] + m5_ref[0] * _dot(y.astype(BF16), w_ref[...])


def hyena_out(x, conv, z, gate, bias, w, m5):
    bsz, t, d = x.shape
    c = z.shape[2]
    tm = min(512, t)
    tokd = pl.BlockSpec((1, tm, d), lambda b, i: (b, i, 0))
    tokc = pl.BlockSpec((1, tm, c), lambda b, i: (b, i, 0))
    return pl.pallas_call(
        _hy_out_kernel,
        grid=(bsz, t // tm),
        in_specs=[tokd, tokc, tokc, tokc, pl.BlockSpec((1, c), lambda b, i: (0, 0)),
                  pl.BlockSpec(w.shape, lambda b, i: (0, 0)), pl.BlockSpec((1, 1, d), lambda b, i: (b, 0, 0))],
        out_specs=tokd,
        out_shape=jax.ShapeDtypeStruct(x.shape, F32),
        compiler_params=_cparams(("parallel", "parallel")),
        name="hyena_out",
    )(x, conv, z, gate, bias.reshape(1, c), w, m5)


def _hyena_filter_spectrum(t, fw1, fb1, fw2, fb2, fw3, fb3, fw4, freq, width):
    pos = jnp.arange(t, dtype=F32)[:, None]
    tt = pos / max(t - 1, 1)
    ang = 2 * math.pi * pos / t
    nb = (HY_EMB - 1) // 2
    bands = jnp.linspace(1e-4, nb - 1, nb, dtype=F32)[None]
    feats = jnp.concatenate([tt, jnp.cos(bands * ang), -jnp.sin(bands * ang)], axis=-1)
    hdn = jnp.sin(freq * (feats @ fw1 + fb1))
    hdn = jnp.sin(freq * (hdn @ fw2 + fb2))
    hdn = jnp.sin(freq * (hdn @ fw3 + fb3))
    filt = (hdn @ fw4).reshape(t, HY_ORDER, 2, width)
    deltas = jnp.abs(jnp.linspace(HY_MIN_DECAY, HY_MAX_DECAY, width, dtype=F32))
    filt = filt * jnp.exp(-tt[:, :, None, None] * deltas)
    fwd, bwd = filt[:, :, 0], filt[:, :, 1]
    kern = jnp.concatenate([fwd, jnp.zeros_like(fwd[:1]), bwd[:0:-1]], axis=0)
    kern = kern * lax.rsqrt(jnp.sum(kern * kern, axis=0, keepdims=True) + 1e-6)
    spec = jnp.fft.fft(kern, axis=0)
    return jnp.stack([jnp.real(spec), jnp.imag(spec)], axis=1).transpose(2, 1, 0, 3).astype(F32)


def _block_diag_pair(m):
    z = jnp.zeros_like(m[0])
    return jnp.concatenate([jnp.concatenate([m[0], z], axis=1), jnp.concatenate([z, m[1]], axis=1)], axis=0)


def _even_mixer(lat, cx, ml, mc, gain, prm, ctx_out):
    (w_in, mu, w0, w_up, a0, a_up, g_up, k_k, k_a, r_k, gn_g, gn_b,
     lam_re, lam_im, log_dt, b_re, b_im, c_re, c_im, d_skip, w_glu, b_glu, w_out) = prm
    bsz, t_lat, d = lat.shape
    t_ctx = cx.shape[1]
    width = k_k.shape[0]
    heads = width // HEAD_DIM
    n_cols = mu.shape[0]
    s5w = d_skip.shape[0]
    ttot = t_ctx + t_lat
    nch = bsz * heads
    tm = t_ctx
    assert tm % GRID_W == 0 and t_lat % tm == 0 and 4 * nch == LANES and 2 * bsz == SUBLANES

    p_all = mod_matmul_stream(cx, lat, gain, mc, ml, w_in.astype(BF16), tm)

    head_of = np.arange(width) // HEAD_DIM
    ones = jnp.asarray(head_of[:, None] == head_of[None, :], F32)
    feat_consts = (mu.reshape(1, -1), _block_diag_pair(w_up).astype(BF16), w0.reshape(1, -1),
                   _block_diag_pair(a_up).astype(BF16), a0.reshape(1, -1), g_up.astype(BF16),
                   k_k.reshape(1, -1), k_a.reshape(1, -1), r_k.reshape(1, -1), ones)
    kk, v, g, bonus, r, k, w2, a2 = rwkv_features(p_all, n_cols, width, feat_consts, tm)

    def key_major(x):
        x = x.reshape(bsz, ttot, heads, 2, HALF_HEAD)
        return x.transpose(1, 4, 3, 0, 2).reshape(ttot, HALF_HEAD, 2 * nch)

    def value_major(x):
        x = x.reshape(bsz, ttot, heads, HEAD_DIM).transpose(1, 3, 0, 2).reshape(ttot, HEAD_DIM, nch)
        return jnp.concatenate([x, x], axis=-1)

    ka_t = k_a.reshape(heads, 2, HALF_HEAD).transpose(2, 1, 0)[:, None, :, None, :]
    ka_t = jnp.broadcast_to(ka_t, (HALF_HEAD, 2, 2, bsz, heads)).reshape(HALF_HEAD, LANES)
    o_f, o_b = rwkv_scan(key_major(r), key_major(kk), key_major(k), value_major(v),
                         key_major(w2[0]), key_major(w2[1]), key_major(a2[0]), key_major(a2[1]),
                         ka_t, t_ctx)

    def token_major(o):
        o = o[..., :nch].reshape(ttot, HEAD_DIM, bsz, heads)
        return o.transpose(2, 0, 3, 1).reshape(bsz, ttot, width)

    o_f, o_b = token_major(o_f), token_major(o_b)

    ng = lam_re.shape[1]
    gps = ng // S5_SUPER
    lam = lax.complex(lam_re, lam_im)
    dt = jnp.exp(log_dt)[..., None]
    a_bar = jnp.exp(lam * dt)
    b_bar = ((a_bar - 1) / lam)[..., None] * lax.complex(b_re, b_im)
    eye = jnp.eye(gps, dtype=F32)

    def b_mat(x):
        x = x.reshape(2, S5_SUPER, gps, S5_STATE, S5_GROUP)
        m = jnp.einsum('dsgph,gk->sdghkp', x, eye)
        return m.reshape(S5_SUPER, 2 * gps * S5_GROUP, gps * S5_STATE)

    def c_mat(x):
        x = x.reshape(2, S5_SUPER, gps, S5_GROUP, S5_STATE)
        m = jnp.einsum('dsghp,gk->skpdgh', x, eye)
        return m.reshape(S5_SUPER, gps * S5_STATE, 2 * gps * S5_GROUP)

    bm = jnp.concatenate([b_mat(jnp.real(b_bar)), b_mat(jnp.imag(b_bar))], axis=2).astype(BF16)
    cm = jnp.concatenate([c_mat(c_re), -c_mat(c_im)], axis=1).astype(BF16)

    def a_rows(x):
        x = x.reshape(2, 1, S5_SUPER, gps * S5_STATE)
        return jnp.broadcast_to(x, (2, bsz, S5_SUPER, gps * S5_STATE)).reshape(2 * bsz, S5_SUPER, -1)

    a_arr = jnp.concatenate([a_rows(jnp.real(a_bar)), a_rows(jnp.imag(a_bar))], axis=2)
    a_arr = a_arr.reshape(1, 2 * bsz, -1)

    u_all = p_all[..., n_cols:]
    u_tm = u_all.transpose(1, 0, 2)
    y_f, y_b = s5_scan(jnp.concatenate([u_tm, u_tm], axis=1), bm, cm, a_arr, t_ctx)
    ys = (y_f[:, :bsz] + y_b[:, bsz:]).transpose(1, 0, 2)

    wo = w_out.astype(BF16)
    out_consts = (ones, gn_g.reshape(1, -1), gn_b.reshape(1, -1), d_skip.reshape(1, -1),
                  w_glu.astype(BF16), b_glu.reshape(1, -1), wo[:width], wo[width:])
    lat = even_out(lat, 1, o_f, o_b, bonus, g, ys, u_all, ml[5], out_consts, tm)
    if ctx_out:
        cx = even_out(cx, 0, o_f, o_b, bonus, g, ys, u_all, mc[5], out_consts, tm)
    return lat, cx


def _hyena_mixer(x, gain, m, prm):
    (w_in, conv_w, conv_b, fw1, fb1, fw2, fb2, fw3, fb3, fw4, freq, bias_d, w_out) = prm
    bsz, t, d = x.shape
    c = w_out.shape[0]
    kf = _hyena_filter_spectrum(t, fw1, fb1, fw2, fb2, fw3, fb3, fw4, freq, c)
    z, zb, g1, g2 = hyena_in(x, gain, m[3], m[4], w_in.astype(BF16), conv_w, conv_b)
    gates = (g1, g2)
    use_fft = t % (FFT_N2 * 2) == 0 and t >= 4 * FFT_N2
    for n in range(HY_ORDER):
        if use_fft:
            nd = t // FFT_N2
            n1 = 2 * nd
            kfn = kf[n].reshape(2, FFT_N2, n1, c).transpose(0, 2, 1, 3)
            zt = zb.reshape(bsz, nd, FFT_N2, c).transpose(0, 2, 1, 3)
            conv = fft_conv(zt, kfn).transpose(0, 2, 1, 3).reshape(bsz, t, c)
        else:
            conv = dft_conv(zb, kf[n])
        if n < HY_ORDER - 1:
            z, zb = hyena_gate(conv, z, gates[n], bias_d[n])
        else:
            return hyena_out(x, conv, z, gates[n], bias_d[n], w_out.astype(BF16), m[5])


def kernel(x, c, ctx, c_ctx, norm_g, ada_w, ada_b, ffn_wg, ffn_wu, ffn_wd, final_g, ev_w_in, ev_mu, ev_w0, ev_w_up, ev_a0, ev_a_up, ev_g_up, ev_k_k, ev_k_a, ev_r_k, ev_gn_g, ev_gn_b, ev_lam_re, ev_lam_im, ev_log_dt, ev_b_re, ev_b_im, ev_c_re, ev_c_im, ev_d, ev_w_glu, ev_b_glu, ev_w_out, od_w_in, od_conv_w, od_conv_b, od_fw1, od_fb1, od_fw2, od_fb2, od_fw3, od_fb3, od_fw4, od_freq, od_bias, od_w_out):
    depth = norm_g.shape[0]
    bsz, _, d = x.shape
    n_even = (depth + 1) // 2
    last_ctx = 2 * (n_even - 1)

    cond8 = jnp.concatenate([c, c_ctx[None], jnp.zeros((8 - bsz - 1, d), F32)], axis=0)
    mods = ada_mods_all(cond8, ada_w, ada_b)

    wg, wu, wd = ffn_wg.astype(BF16), ffn_wu.astype(BF16), ffn_wd.astype(BF16)
    lat, cx = x, ctx
    for l in range(depth):
        run_ctx = l <= last_ctx
        ctx_out = l < last_ctx
        i = l // 2
        ml = [mods[l, :bsz, None, k * d:(k + 1) * d] for k in range(N_MOD)]
        mc = [jnp.broadcast_to(mods[l, bsz:bsz + 1, None, k * d:(k + 1) * d], (bsz, 1, d))
              for k in range(N_MOD)]
        lat = ffn_half(lat, norm_g[l, 0], ml[0], ml[1], ml[2], wg[l, 0], wu[l, 0], wd[l, 0])
        if run_ctx:
            cx = ffn_half(cx, norm_g[l, 0], mc[0], mc[1], mc[2], wg[l, 0], wu[l, 0], wd[l, 0])
        if l % 2 == 0:
            prm = (ev_w_in[i], ev_mu[i], ev_w0[i], ev_w_up[i], ev_a0[i], ev_a_up[i], ev_g_up[i],
                   ev_k_k[i], ev_k_a[i], ev_r_k[i], ev_gn_g[i], ev_gn_b[i],
                   ev_lam_re[i], ev_lam_im[i], ev_log_dt[i], ev_b_re[i], ev_b_im[i], ev_c_re[i], ev_c_im[i],
                   ev_d[i], ev_w_glu[i], ev_b_glu[i], ev_w_out[i])
            lat, cx = _even_mixer(lat, cx, ml, mc, norm_g[l, 1], prm, ctx_out)
        else:
            prm = (od_w_in[i], od_conv_w[i], od_conv_b[i], od_fw1[i], od_fb1[i], od_fw2[i], od_fb2[i],
                   od_fw3[i], od_fb3[i], od_fw4[i], od_freq[i], od_bias[i], od_w_out[i])
            lat = _hyena_mixer(lat, norm_g[l, 1], ml, prm)
            if ctx_out:
                cx = _hyena_mixer(cx, norm_g[l, 1], mc, prm)
        fin = final_g if l == depth - 1 else None
        lat = ffn_half(lat, norm_g[l, 2], ml[6], ml[7], ml[8], wg[l, 1], wu[l, 1], wd[l, 1], fin)
        if ctx_out:
            cx = ffn_half(cx, norm_g[l, 2], mc[6], mc[7], mc[8], wg[l, 1], wu[l, 1], wd[l, 1])
    return lat
```

```python
import functools
import math

import numpy as np
import jax
import jax.numpy as jnp
from jax import lax
from jax.experimental import pallas as pl
from jax.experimental.pallas import tpu as pltpu

F32 = jnp.float32
BF16 = jnp.bfloat16
HIGHEST = lax.Precision.HIGHEST

N_MOD = 9
NORM_EPS = 1e-6
GN_EPS = 64e-5
GRID_W = 64
HEAD_DIM = 64
HALF_HEAD = HEAD_DIM // 2
S5_GROUP = 16
S5_STATE = 64
S5_SUPER = 4
HY_ORDER = 2
HY_EMB = 33
HY_MIN_DECAY = math.log(1e-2) / 1.5
HY_MAX_DECAY = math.log(1e-2) / 0.3
LANES = 128
SUBLANES = 8
FFT_N2 = 128
VMEM_LIMIT = 56 * 1024 * 1024


def _cparams(sem, vmem=VMEM_LIMIT):
    return pltpu.CompilerParams(dimension_semantics=sem, vmem_limit_bytes=vmem)


def _dot(a, b):
    return jnp.dot(a, b, preferred_element_type=F32)


def _dot_exact(a, b):
    return jnp.dot(a, b, preferred_element_type=F32, precision=HIGHEST)


def _rms_mod(x, gain, shift, scale):
    ms = jnp.mean(x * x, axis=-1, keepdims=True)
    return x * lax.rsqrt(ms + NORM_EPS) * gain * (1.0 + scale) + shift


def _sigmoid(x):
    return 1.0 / (1.0 + jnp.exp(-x))


def _silu(x):
    return x * _sigmoid(x)


def _ada_kernel(c_ref, w_ref, b_ref, o_ref):
    s = _silu(c_ref[...])
    o_ref[0] = _dot(s.astype(BF16), w_ref[0].astype(BF16)) + b_ref[0]


def ada_mods_all(cond8, ada_w, ada_b):
    depth, d, n = ada_w.shape
    tn = n // 8
    return pl.pallas_call(
        _ada_kernel,
        grid=(depth, n // tn),
        in_specs=[pl.BlockSpec((8, d), lambda l, j: (0, 0)),
                  pl.BlockSpec((1, d, tn), lambda l, j: (l, 0, j)),
                  pl.BlockSpec((1, 1, tn), lambda l, j: (l, 0, j))],
        out_specs=pl.BlockSpec((1, 8, tn), lambda l, j: (l, 0, j)),
        out_shape=jax.ShapeDtypeStruct((depth, 8, n), F32),
        compiler_params=_cparams(("parallel", "parallel")),
        name="ada_mods",
    )(cond8, ada_w, ada_b.reshape(depth, 1, n))


def _ffn_kernel(x_ref, gain_ref, shift_ref, scale_ref, gate_ref, wg_ref, wu_ref, wd_ref, fg_ref,
                o_ref, h_ref, acc_ref, *, final_norm):
    k = pl.program_id(2)

    @pl.when(k == 0)
    def _():
        h = _rms_mod(x_ref[0], gain_ref[...], shift_ref[0], scale_ref[0])
        h_ref[...] = h.astype(BF16)
        acc_ref[...] = jnp.zeros_like(acc_ref)

    h = h_ref[...]
    g = _dot(h, wg_ref[...])
    u = _dot(h, wu_ref[...])
    a = (_silu(g) * u).astype(BF16)
    acc_ref[...] += _dot(a, wd_ref[...])

    @pl.when(k == pl.num_programs(2) - 1)
    def _():
        y = x_ref[0] + 0.5 * gate_ref[0] * acc_ref[...]
        if final_norm:
            ms = jnp.mean(y * y, axis=-1, keepdims=True)
            y = y * lax.rsqrt(ms + NORM_EPS) * fg_ref[...]
        o_ref[0] = y


def ffn_half(x, gain, shift, scale, gate, wg, wu, wd, final_g=None):
    bsz, t, d = x.shape
    ff = wg.shape[1]
    tm = min(512, t)
    tf = ff // 2
    fg = jnp.ones((1, d), F32) if final_g is None else final_g.reshape(1, d)
    vec = pl.BlockSpec((1, 1, d), lambda b, i, k: (b, 0, 0))
    return pl.pallas_call(
        functools.partial(_ffn_kernel, final_norm=final_g is not None),
        grid=(bsz, t // tm, ff // tf),
        in_specs=[pl.BlockSpec((1, tm, d), lambda b, i, k: (b, i, 0)),
                  pl.BlockSpec((1, d), lambda b, i, k: (0, 0)),
                  vec, vec, vec,
                  pl.BlockSpec((d, tf), lambda b, i, k: (0, k)),
                  pl.BlockSpec((d, tf), lambda b, i, k: (0, k)),
                  pl.BlockSpec((tf, d), lambda b, i, k: (k, 0)),
                  pl.BlockSpec((1, d), lambda b, i, k: (0, 0))],
        out_specs=pl.BlockSpec((1, tm, d), lambda b, i, k: (b, i, 0)),
        out_shape=jax.ShapeDtypeStruct((bsz, t, d), F32),
        scratch_shapes=[pltpu.VMEM((tm, d), BF16), pltpu.VMEM((tm, d), F32)],
        compiler_params=_cparams(("parallel", "parallel", "arbitrary")),
        name="ffn_half",
    )(x, gain.reshape(1, d), shift, scale, gate, wg, wu, wd, fg)


def _modmm_kernel(cx_ref, lat_ref, gain_ref, shc_ref, scc_ref, shl_ref, scl_ref, w_ref, o_ref, *, nc):
    is_ctx = pl.program_id(1) < nc
    x = jnp.where(is_ctx, cx_ref[0], lat_ref[0])
    shift = jnp.where(is_ctx, shc_ref[0], shl_ref[0])
    scale = jnp.where(is_ctx, scc_ref[0], scl_ref[0])
    h = _rms_mod(x, gain_ref[...], shift, scale)
    o_ref[0] = _dot(h.astype(BF16), w_ref[...])


def mod_matmul_stream(cx, lat, gain, mc, ml, w, tm):
    bsz, t_ctx, d = cx.shape
    t_lat = lat.shape[1]
    n = w.shape[1]
    nc = t_ctx // tm
    nt = nc + t_lat // tm
    vec = pl.BlockSpec((1, 1, d), lambda b, i: (b, 0, 0))
    return pl.pallas_call(
        functools.partial(_modmm_kernel, nc=nc),
        grid=(bsz, nt),
        in_specs=[pl.BlockSpec((1, tm, d), lambda b, i: (b, jnp.minimum(i, nc - 1), 0)),
                  pl.BlockSpec((1, tm, d), lambda b, i: (b, jnp.maximum(i - nc, 0), 0)),
                  pl.BlockSpec((1, d), lambda b, i: (0, 0)),
                  vec, vec, vec, vec,
                  pl.BlockSpec((d, n), lambda b, i: (0, 0))],
        out_specs=pl.BlockSpec((1, tm, n), lambda b, i: (b, i, 0)),
        out_shape=jax.ShapeDtypeStruct((bsz, t_ctx + t_lat, n), F32),
        compiler_params=_cparams(("parallel", "parallel")),
        name="mod_matmul",
    )(cx, lat, gain.reshape(1, d), mc[3], mc[4], ml[3], ml[4], w)


def _rwkv_feat_kernel(p_ref, pu_ref, pd_ref, mu_ref, wup_ref, w0_ref, aup_ref, a0_ref, gup_ref,
                      kk_s_ref, ka_ref, rk_ref, ones_ref,
                      kk_o, v_o, g_o, bonus_o, r_o, k_o, w_o, a_o, *, nt, width):
    i = pl.program_id(1)
    is_ctx = i == 0
    p = p_ref[0]
    tm = p.shape[0]
    row = lax.broadcasted_iota(jnp.int32, p.shape, 0)
    lane = lax.broadcasted_iota(jnp.int32, p.shape, 1) % 4
    prev = pltpu.roll(p, 1, 0)
    nxt = pltpu.roll(p, tm - 1, 0)
    col = jnp.where(is_ctx, row, row % GRID_W)
    last = jnp.where(is_ctx, tm - 1, GRID_W - 1)
    left = jnp.where(col == 0, 0.0, prev)
    right = jnp.where(col == last, 0.0, nxt)
    up_halo = jnp.where(i > 1, pu_ref[0], 0.0)
    dn_halo = jnp.where(i < nt - 1, pd_ref[0], 0.0)
    up = jnp.where(is_ctx, left, jnp.concatenate([up_halo, p[:tm - GRID_W]], axis=0))
    down = jnp.where(is_ctx, right, jnp.concatenate([p[GRID_W:], dn_halo], axis=0))
    shifted = jnp.where(lane == 0, left, jnp.where(lane == 1, right, jnp.where(lane == 2, up, down)))
    q = p + mu_ref[...] * (shifted - p)

    w = width
    r, k, v = q[:, :w], q[:, w:2 * w], q[:, 2 * w:3 * w]
    wd = q[:, 3 * w:3 * w + LANES]
    ad = q[:, 3 * w + LANES:3 * w + 2 * LANES]
    gd = q[:, 3 * w + 2 * LANES:3 * w + 3 * LANES]

    zlin = w0_ref[...] + _dot(jnp.tanh(wd).astype(BF16), wup_ref[...])
    neg = -zlin
    softplus = jnp.maximum(neg, 0.0) + jnp.log(1.0 + jnp.exp(-jnp.abs(neg)))
    decay = jnp.exp(-jnp.exp(-softplus - 0.5))
    a = _sigmoid(a0_ref[...] + _dot(ad.astype(BF16), aup_ref[...]))
    g_o[0] = _dot(_sigmoid(gd).astype(BF16), gup_ref[...])

    ones = ones_ref[...]
    kk = k * kk_s_ref[...]
    ss = _dot_exact(kk * kk, ones)
    kk_o[0] = kk / jnp.maximum(jnp.sqrt(ss), 1e-12)
    v_o[0] = v
    r_o[0] = r
    k_o[0] = k
    ksum = jnp.zeros_like(k)
    for d in range(2):
        a_d = a[:, d * w:(d + 1) * w]
        ksum = ksum + k * (1.0 + (a_d - 1.0) * ka_ref[...])
        w_o[d, 0] = decay[:, d * w:(d + 1) * w]
        a_o[d, 0] = a_d
    bonus_o[0] = _dot_exact(r * ksum * rk_ref[...], ones) * v


def rwkv_features(p, n_cols, width, consts, tm):
    bsz, t, _ = p.shape
    nt = t // tm
    hb = tm // GRID_W
    nhb = t // GRID_W
    w = width
    full = lambda a: pl.BlockSpec(a.shape, lambda b, i: (0,) * a.ndim)
    tok = pl.BlockSpec((1, tm, w), lambda b, i: (b, i, 0))
    tok2 = pl.BlockSpec((2, 1, tm, w), lambda b, i: (0, b, i, 0))
    sds = jax.ShapeDtypeStruct((bsz, t, w), F32)
    sds2 = jax.ShapeDtypeStruct((2, bsz, t, w), F32)
    return pl.pallas_call(
        functools.partial(_rwkv_feat_kernel, nt=nt, width=w),
        grid=(bsz, nt),
        in_specs=[pl.BlockSpec((1, tm, n_cols), lambda b, i: (b, i, 0)),
                  pl.BlockSpec((1, GRID_W, n_cols), lambda b, i: (b, jnp.maximum(i * hb - 1, 0), 0)),
                  pl.BlockSpec((1, GRID_W, n_cols), lambda b, i: (b, jnp.minimum((i + 1) * hb, nhb - 1), 0))]
                 + [full(a) for a in consts],
        out_specs=[tok] * 6 + [tok2, tok2],
        out_shape=[sds] * 6 + [sds2, sds2],
        compiler_params=_cparams(("parallel", "parallel")),
        name="rwkv_features",
    )(p, p, p, *consts)


def _rwkv_scan_kernel(rf, rb, kkf, kkb, kf, kb, wf, wb, af, ab, vf, vb, ka_ref, of_ref, ob_ref,
                      s_ref, vec_ref, v_ref, c_ref, *, tc):
    @pl.when(pl.program_id(0) == 0)
    def _():
        s_ref[...] = jnp.zeros_like(s_ref)
        vec_ref[:, tc] = jnp.zeros((vec_ref.shape[0],) + vec_ref.shape[2:], F32)

    nj, ni = s_ref.shape[0], s_ref.shape[1]
    quarter = LANES // 4

    def fold(x):
        hi = (lax.broadcasted_iota(jnp.int32, x.shape, 1) % (2 * quarter)) >= quarter
        return x + jnp.where(hi, pltpu.roll(x, quarter, 1), pltpu.roll(x, 3 * quarter, 1))

    def jsum(x):
        s = jnp.broadcast_to(jnp.sum(x, axis=0, keepdims=True), (SUBLANES, LANES))
        return fold(s)[0:1]

    ka = ka_ref[...]
    fwd_lane = lax.broadcasted_iota(jnp.int32, (nj, LANES), 1) < LANES // 2
    for t in range(tc):
        merged = lambda f, b: jnp.where(fwd_lane, f[t], b[tc - 1 - t])
        r, kk, k = merged(rf, rb), merged(kkf, kkb), merged(kf, kb)
        w, a = merged(wf, wb), merged(af, ab)
        kka = kk * a
        kd = k * (1.0 + (a - 1.0) * ka)
        vec_ref[0, t] = kk
        vec_ref[1, t] = w * r
        vec_ref[2, t] = w
        vec_ref[3, t] = kka
        vec_ref[4, t] = kd
        c_ref[t, 0:1, :] = jsum(kka * r)
        c_ref[t, 1:2, :] = jsum(kd * r)
        v_ref[t] = jnp.concatenate([vf[t], vb[tc - 1 - t]], axis=-1)

    nh = 2
    hr = ni // nh

    def partial_sums(t, rows):
        sums = []
        for h in range(nh):
            sa = jnp.zeros((hr, LANES), F32)
            o1 = jnp.zeros((hr, LANES), F32)
            for j in range(nj):
                sj = rows(j, h)
                sa = sa + sj * vec_ref[0, t, j:j + 1, :]
                o1 = o1 + sj * vec_ref[1, t, j:j + 1, :]
            sums += [sa, o1]
        return tuple(sums)

    def step(t, carry):
        sas, vs = [], []
        for h in range(nh):
            sa, o1 = fold(carry[2 * h]), fold(carry[2 * h + 1])
            v = v_ref[t, h * hr:(h + 1) * hr, :]
            out = o1 - sa * c_ref[t, 0:1, :] + v * c_ref[t, 1:2, :]
            of_ref[t, h * hr:(h + 1) * hr, :] = out[:, :LANES // 2]
            ob_ref[tc - 1 - t, h * hr:(h + 1) * hr, :] = out[:, LANES // 2:]
            sas.append(sa)
            vs.append(v)

        def updated(j, h):
            rows = slice(h * hr, (h + 1) * hr)
            sj = (s_ref[j, rows, :] * vec_ref[2, t, j:j + 1, :] - sas[h] * vec_ref[3, t, j:j + 1, :]
                  + vs[h] * vec_ref[4, t, j:j + 1, :])
            s_ref[j, rows, :] = sj
            return sj

        return partial_sums(t + 1, updated)

    lax.fori_loop(0, tc, step, partial_sums(0, lambda j, h: s_ref[j, h * hr:(h + 1) * hr, :]))


def rwkv_scan(r, kk, k, w, a, v, ka, t_ctx):
    ttot, nj, _ = r.shape
    ni, nc = v.shape[1], v.shape[2]
    tc = 32
    ncb, ntb = t_ctx // tc, ttot // tc
    fwd = lambda g: (g, 0, 0)
    bwd = lambda g: (jnp.where(g < ncb, ncb - 1 - g, ntb + ncb - 1 - g), 0, 0)
    jf, jb = pl.BlockSpec((tc, nj, LANES), fwd), pl.BlockSpec((tc, nj, LANES), bwd)
    vf, vb = pl.BlockSpec((tc, ni, nc), fwd), pl.BlockSpec((tc, ni, nc), bwd)
    osd = jax.ShapeDtypeStruct((ttot, ni, nc), F32)
    return pl.pallas_call(
        functools.partial(_rwkv_scan_kernel, tc=tc),
        grid=(ntb,),
        in_specs=[jf, jb] * 5 + [vf, vb, pl.BlockSpec(ka.shape, lambda g: (0, 0))],
        out_specs=[vf, vb],
        out_shape=[osd, osd],
        scratch_shapes=[pltpu.VMEM((nj, ni, LANES), F32), pltpu.VMEM((5, tc + 1, nj, LANES), F32),
                        pltpu.VMEM((tc, ni, LANES), F32), pltpu.VMEM((tc, SUBLANES, LANES), F32)],
        compiler_params=_cparams(("arbitrary",)),
        name="rwkv_scan",
    )(r, r, kk, kk, k, k, w, w, a, a, v, v, ka)


def _s5_kernel(uf_ref, ub_ref, bm_ref, cm_ref, a_ref, yf_ref, yb_ref, h_ref, bu_ref, ubr_ref, *, tt):
    @pl.when(pl.program_id(0) == 0)
    def _():
        h_ref[...] = jnp.zeros_like(h_ref)

    nsb = bm_ref.shape[0]
    kin = bm_ref.shape[1] // 2
    sw = bm_ref.shape[2]
    hw = sw // 2
    for t in range(tt):
        ubr_ref[t] = ub_ref[tt - 1 - t]
    fwd_row3 = lax.broadcasted_iota(jnp.int32, uf_ref.shape, 1) < SUBLANES // 2
    uf = jnp.where(fwd_row3, uf_ref[...], 0.0)
    ub = jnp.where(fwd_row3, 0.0, ubr_ref[...])
    for sb in range(nsb):
        lhs = jnp.concatenate([uf[:, :, sb * kin:(sb + 1) * kin], ub[:, :, sb * kin:(sb + 1) * kin]], axis=-1)
        lhs = lhs.reshape(tt * SUBLANES, 2 * kin).astype(BF16)
        bu_ref[:, sb * sw:(sb + 1) * sw] = _dot(lhs, bm_ref[sb])

    def step(t, hs):
        rows = pl.ds(pl.multiple_of(t * SUBLANES, SUBLANES), SUBLANES)
        out = []
        for sb in range(nsb):
            hr, hi = hs[2 * sb], hs[2 * sb + 1]
            lo = sb * sw
            ar = a_ref[0, :, lo:lo + hw]
            ai = a_ref[0, :, lo + hw:lo + sw]
            nr = ar * hr - ai * hi + bu_ref[rows, lo:lo + hw]
            ni = ar * hi + ai * hr + bu_ref[rows, lo + hw:lo + sw]
            bu_ref[rows, lo:lo + hw] = nr
            bu_ref[rows, lo + hw:lo + sw] = ni
            out += [nr, ni]
        return tuple(out)

    h0 = []
    for sb in range(nsb):
        h0 += [h_ref[:, sb * sw:sb * sw + hw], h_ref[:, sb * sw + hw:(sb + 1) * sw]]
    hs = lax.fori_loop(0, tt, step, tuple(h0))
    for sb in range(nsb):
        h_ref[:, sb * sw:sb * sw + hw] = hs[2 * sb]
        h_ref[:, sb * sw + hw:(sb + 1) * sw] = hs[2 * sb + 1]

    nout = cm_ref.shape[2] // 2
    fwd_row = (lax.broadcasted_iota(jnp.int32, (tt * SUBLANES, nout), 0) % SUBLANES) < SUBLANES // 2
    ys = []
    for sb in range(nsb):
        yy = _dot(bu_ref[:, sb * sw:(sb + 1) * sw].astype(BF16), cm_ref[sb])
        ys.append(jnp.where(fwd_row, yy[:, :nout], yy[:, nout:]))
    y = jnp.concatenate(ys, axis=1).reshape(tt, SUBLANES, nsb * nout)
    yf_ref[...] = y
    for t in range(tt):
        yb_ref[tt - 1 - t] = y[t]


def s5_scan(u, bm, cm, a, t_ctx):
    ttot, rows, width = u.shape
    nsb, _, sw = bm.shape
    tt = 64
    ncb, ntb = t_ctx // tt, ttot // tt
    fwd = pl.BlockSpec((tt, rows, width), lambda g: (g, 0, 0))
    bwd = pl.BlockSpec((tt, rows, width), lambda g: (jnp.where(g < ncb, ncb - 1 - g, ntb + ncb - 1 - g), 0, 0))
    full = lambda x: pl.BlockSpec(x.shape, lambda g: (0,) * x.ndim)
    osd = jax.ShapeDtypeStruct(u.shape, F32)
    return pl.pallas_call(
        functools.partial(_s5_kernel, tt=tt),
        grid=(ntb,),
        in_specs=[fwd, bwd, full(bm), full(cm), full(a)],
        out_specs=[fwd, bwd],
        out_shape=[osd, osd],
        scratch_shapes=[pltpu.VMEM((rows, nsb * sw), F32), pltpu.VMEM((tt * rows, nsb * sw), F32),
                        pltpu.VMEM((tt, rows, width), F32)],
        compiler_params=_cparams(("arbitrary",)),
        name="s5_scan",
    )(u, u, bm, cm, a)


def _even_out_kernel(x_ref, of_ref, ob_ref, bonus_ref, g_ref, ys_ref, u_ref, m5_ref,
                     ones_ref, gng_ref, gnb_ref, dskip_ref, wglu_ref, bglu_ref, wo1_ref, wo2_ref, o_ref):
    ones = ones_ref[...]
    inv = 1.0 / HEAD_DIM
    o = of_ref[0] + ob_ref[0]
    mean = _dot_exact(o, ones) * inv
    oc = o - mean
    var = _dot_exact(oc * oc, ones) * inv
    y1 = (oc * lax.rsqrt(var + GN_EPS) * gng_ref[...] + gnb_ref[...] + bonus_ref[0]) * g_ref[0]
    y = ys_ref[0] + dskip_ref[...] * u_ref[0]
    y = jax.nn.gelu(y)
    y2 = y * _sigmoid(_dot(y.astype(BF16), wglu_ref[...]) + bglu_ref[...])
    out = _dot(y1.astype(BF16), wo1_ref[...]) + _dot(y2.astype(BF16), wo2_ref[...])
    o_ref[0] = x_ref[0] + m5_ref[0] * out


def even_out(x, off, o_f, o_b, bonus, g, ys, u, m5, consts, tm):
    bsz, t, d = x.shape
    w = o_f.shape[2]
    tokd = pl.BlockSpec((1, tm, d), lambda b, i: (b, i, 0))
    tokw = pl.BlockSpec((1, tm, w), lambda b, i: (b, i + off, 0))
    toks = pl.BlockSpec((1, tm, u.shape[2]), lambda b, i: (b, i + off, 0))
    full = lambda a: pl.BlockSpec(a.shape, lambda b, i: (0,) * a.ndim)
    return pl.pallas_call(
        _even_out_kernel,
        grid=(bsz, t // tm),
        in_specs=[tokd, tokw, tokw, tokw, tokw, toks, toks,
                  pl.BlockSpec((1, 1, d), lambda b, i: (b, 0, 0))] + [full(a) for a in consts],
        out_specs=tokd,
        out_shape=jax.ShapeDtypeStruct((bsz, t, d), F32),
        compiler_params=_cparams(("parallel", "parallel")),
        name="even_out",
    )(x, o_f, o_b, bonus, g, ys, u, m5, *consts)


def _hy_in_kernel(x_ref, xp_ref, xn_ref, gain_ref, shift_ref, scale_ref, w_ref, cw_ref, cb_ref,
                  z_ref, zb_ref, g1_ref, g2_ref, *, nt, c):
    i = pl.program_id(1)
    gain, shift, scale = gain_ref[...], shift_ref[0], scale_ref[0]
    h = _rms_mod(x_ref[0], gain, shift, scale).astype(BF16)
    hp = _rms_mod(xp_ref[0], gain, shift, scale).astype(BF16)
    hn = _rms_mod(xn_ref[0], gain, shift, scale).astype(BF16)
    tm = h.shape[0]
    row = lax.broadcasted_iota(jnp.int32, (tm, c), 0)
    outs = (z_ref, g1_ref, g2_ref)
    for part in range(3):
        wp = w_ref[:, part * c:(part + 1) * c]
        p = _dot(h, wp)
        pp = jnp.where(i > 0, _dot(hp, wp)[7:8], 0.0)
        pn = jnp.where(i < nt - 1, _dot(hn, wp)[0:1], 0.0)
        pm1 = jnp.where(row == 0, pp, pltpu.roll(p, 1, 0))
        pp1 = jnp.where(row == tm - 1, pn, pltpu.roll(p, tm - 1, 0))
        cw = cw_ref[:, part * c:(part + 1) * c]
        q = cw[0:1] * pm1 + cw[1:2] * p + cw[2:3] * pp1 + cb_ref[:, part * c:(part + 1) * c]
        outs[part][0] = q
        if part == 0:
            zb_ref[0] = q.astype(BF16)


def hyena_in(x, gain, shift, scale, w, conv_w, conv_b):
    bsz, t, d = x.shape
    c = w.shape[1] // 3
    tm = min(512, t)
    nt = t // tm
    hb = tm // 8
    nhb = t // 8
    vec = pl.BlockSpec((1, 1, d), lambda b, i: (b, 0, 0))
    tok = pl.BlockSpec((1, tm, c), lambda b, i: (b, i, 0))
    sds = jax.ShapeDtypeStruct((bsz, t, c), F32)
    return pl.pallas_call(
        functools.partial(_hy_in_kernel, nt=nt, c=c),
        grid=(bsz, nt),
        in_specs=[pl.BlockSpec((1, tm, d), lambda b, i: (b, i, 0)),
                  pl.BlockSpec((1, 8, d), lambda b, i: (b, jnp.maximum(i * hb - 1, 0), 0)),
                  pl.BlockSpec((1, 8, d), lambda b, i: (b, jnp.minimum((i + 1) * hb, nhb - 1), 0)),
                  pl.BlockSpec((1, d), lambda b, i: (0, 0)),
                  vec, vec,
                  pl.BlockSpec(w.shape, lambda b, i: (0, 0)),
                  pl.BlockSpec(conv_w.shape, lambda b, i: (0, 0)),
                  pl.BlockSpec((1, 3 * c), lambda b, i: (0, 0))],
        out_specs=[tok, tok, tok, tok],
        out_shape=[sds, jax.ShapeDtypeStruct((bsz, t, c), BF16), sds, sds],
        compiler_params=_cparams(("parallel", "parallel")),
        name="hyena_in",
    )(x, x, x, gain.reshape(1, d), shift, scale, w, conv_w, conv_b.reshape(1, 3 * c))


def _pack_complex(re, im):
    bits = lambda x: lax.bitcast_convert_type(x.astype(BF16).astype(F32), jnp.uint32)
    return lax.shift_right_logical(bits(re), jnp.uint32(16)) | bits(im)


def _unpack_complex(w):
    re = lax.bitcast_convert_type(lax.shift_left(w, jnp.uint32(16)), F32)
    im = lax.bitcast_convert_type(w & jnp.uint32(0xFFFF0000), F32)
    return jnp.concatenate([re, im], axis=0).astype(BF16)


def _fft_conv_kernel(z_ref, kf_ref, w1_ref, f2_ref, g2i_ref, f1c_ref, tw_ref, twt_ref, o_ref, a_ref, t_ref,
                     *, n1, k1c):
    s = pl.program_id(2)
    n2 = FFT_N2
    nd = n1 // 2
    slab = n1 // k1c

    @pl.when(s == 0)
    def _():
        w1 = w1_ref[...]

        def body(j, carry):
            x = jnp.concatenate([z_ref[0, j], z_ref[1, j]], axis=0)
            a = _dot(w1, x)
            a_ref[pl.ds(pl.multiple_of(j * n1, n1), n1), :] = _pack_complex(a[:n1], a[n1:])
            return carry

        lax.fori_loop(0, n2, body, 0, unroll=8)

    @pl.when((s > 0) & (s <= k1c))
    def _():
        f2r, f2i = f2_ref[0], f2_ref[1]
        g2i = g2i_ref[...]
        base = (s - 1) * slab

        def body(kk, carry):
            k1 = base + kk
            twr = tw_ref[0, pl.ds(k1, 1), :]
            twi = tw_ref[1, pl.ds(k1, 1), :]
            gr = f2r * twr - f2i * twi
            gi = f2r * twi + f2i * twr
            gmat = jnp.concatenate([jnp.concatenate([gr, -gi], axis=1),
                                    jnp.concatenate([gi, gr], axis=1)], axis=0).astype(BF16)
            ak = _unpack_complex(a_ref[pl.ds(k1, n2, stride=n1), :])
            x = _dot(gmat, ak)
            xr, xi = x[:n2], x[n2:]
            kr, ki = kf_ref[0, kk], kf_ref[1, kk]
            pr = xr * kr - xi * ki
            pi = xr * ki + xi * kr
            tt = _dot(g2i, jnp.concatenate([pr, pi], axis=0).astype(BF16))
            t_ref[pl.ds(pl.multiple_of(k1 * n2, n2), n2), :] = _pack_complex(tt[:n2], tt[n2:])
            return carry

        lax.fori_loop(0, slab, body, 0, unroll=4)

    @pl.when(s == k1c + 1)
    def _():
        f1r, f1i = f1c_ref[0], f1c_ref[1]

        def body(j, carry):
            twr = twt_ref[0, pl.ds(j, 1), :]
            twi = twt_ref[1, pl.ds(j, 1), :]
            wr = f1r * twr + f1i * twi
            wi = f1i * twr - f1r * twi
            w3 = jnp.concatenate([jnp.concatenate([wr, -wi], axis=1),
                                  jnp.concatenate([wi, wr], axis=1)], axis=0).astype(BF16)
            tn = _unpack_complex(t_ref[pl.ds(j, n1, stride=n2), :])
            y = _dot(w3, tn)
            o_ref[0, j] = y[:nd]
            o_ref[1, j] = y[nd:]
            return carry

        lax.fori_loop(0, n2, body, 0, unroll=8)


def _fft_consts(n1):
    n2 = FFT_N2
    n = n1 * n2
    nd = n1 // 2
    k1 = np.arange(n1)
    f1 = np.exp(-2j * np.pi * np.outer(k1, np.arange(nd)) / n1)
    w1 = np.block([[f1.real, -f1.imag], [f1.imag, f1.real]])
    f2 = np.exp(-2j * np.pi * np.outer(np.arange(n2), np.arange(n2)) / n2)
    f2c = np.conj(f2)
    g2i = np.block([[f2c.real, -f2c.imag], [f2c.imag, f2c.real]])
    f1c = np.exp(2j * np.pi * np.outer(np.arange(nd), k1) / n1) / n
    tw = np.exp(-2j * np.pi * np.outer(k1, np.arange(n2)) / n)
    cplx = lambda m: jnp.asarray(np.stack([m.real, m.imag]), F32)
    return (jnp.asarray(w1, BF16), cplx(f2), jnp.asarray(g2i, BF16), cplx(f1c), cplx(tw), cplx(tw.T))


def fft_conv(zt, kf):
    bsz, n2, nd, c = zt.shape
    n1 = 2 * nd
    cb = LANES
    k1c = 8 if n1 % 8 == 0 and n1 >= 64 else 2
    slab = n1 // k1c
    consts = _fft_consts(n1)
    full = lambda a: pl.BlockSpec(a.shape, lambda j, p, s: (0,) * a.ndim)
    blk = pl.BlockSpec((2, n2, nd, cb), lambda j, p, s: (p, 0, 0, j))
    return pl.pallas_call(
        functools.partial(_fft_conv_kernel, n1=n1, k1c=k1c),
        grid=(c // cb, bsz // 2, k1c + 2),
        in_specs=[blk,
                  pl.BlockSpec((2, slab, n2, cb), lambda j, p, s: (0, jnp.clip(s - 1, 0, k1c - 1), 0, j))]
                 + [full(a) for a in consts],
        out_specs=blk,
        out_shape=jax.ShapeDtypeStruct(zt.shape, F32),
        scratch_shapes=[pltpu.VMEM((n2 * n1, cb), jnp.uint32), pltpu.VMEM((n1 * n2, cb), jnp.uint32)],
        compiler_params=_cparams(("parallel", "parallel", "arbitrary")),
        name="fft_conv",
    )(zt, kf, *consts)


def _dft_conv_kernel(z_ref, kf_ref, fw_ref, iv_ref, o_ref, *, t):
    x = jnp.concatenate([z_ref[0], z_ref[1]], axis=0)
    spec = _dot(fw_ref[...], x)
    xr, xi = spec[:2 * t], spec[2 * t:]
    kr, ki = kf_ref[0], kf_ref[1]
    pr = xr * kr - xi * ki
    pi = xr * ki + xi * kr
    y = _dot(iv_ref[...], jnp.concatenate([pr, pi], axis=0).astype(BF16))
    o_ref[0] = y[:t]
    o_ref[1] = y[t:]


def dft_conv(zb, kf):
    bsz, t, c = zb.shape
    n = 2 * t
    f = np.exp(-2j * np.pi * np.outer(np.arange(n), np.arange(t)) / n)
    fw = np.block([[f.real, -f.imag], [f.imag, f.real]])
    fi = np.exp(2j * np.pi * np.outer(np.arange(t), np.arange(n)) / n) / n
    iv = np.block([[fi.real, -fi.imag], [fi.imag, fi.real]])
    fw, iv = jnp.asarray(fw, BF16), jnp.asarray(iv, BF16)
    cb = LANES
    blk = pl.BlockSpec((2, t, cb), lambda j, p: (p, 0, j))
    return pl.pallas_call(
        functools.partial(_dft_conv_kernel, t=t),
        grid=(c // cb, bsz // 2),
        in_specs=[blk, pl.BlockSpec((2, n, cb), lambda j, p: (0, 0, j)),
                  pl.BlockSpec(fw.shape, lambda j, p: (0, 0)), pl.BlockSpec(iv.shape, lambda j, p: (0, 0))],
        out_specs=blk,
        out_shape=jax.ShapeDtypeStruct(zb.shape, F32),
        compiler_params=_cparams(("parallel", "parallel")),
        name="dft_conv",
    )(zb, kf, fw, iv)


def _hy_gate_kernel(conv_ref, z_ref, gate_ref, bias_ref, o_ref, ob_ref):
    y = gate_ref[0] * (conv_ref[0] + bias_ref[...] * z_ref[0])
    o_ref[0] = y
    ob_ref[0] = y.astype(BF16)


def hyena_gate(conv, z, gate, bias):
    bsz, t, c = z.shape
    tm = min(512, t)
    tok = pl.BlockSpec((1, tm, c), lambda b, i: (b, i, 0))
    return pl.pallas_call(
        _hy_gate_kernel,
        grid=(bsz, t // tm),
        in_specs=[tok, tok, tok, pl.BlockSpec((1, c), lambda b, i: (0, 0))],
        out_specs=[tok, tok],
        out_shape=[jax.ShapeDtypeStruct(z.shape, F32), jax.ShapeDtypeStruct(z.shape, BF16)],
        compiler_params=_cparams(("parallel", "parallel")),
        name="hyena_gate",
    )(conv, z, gate, bias.reshape(1, c))


def _hy_out_kernel(x_ref, conv_ref, z_ref, gate_ref, bias_ref, w_ref, m5_ref, o_ref):
    y = gate_ref[0] * (conv_ref[0] + bias_ref[...] * z_ref[0])
    o_ref[0] = x_ref[0] + m5_ref[0] * _dot(y.astype(BF16), w_ref[...])


def hyena_out(x, conv, z, gate, bias, w, m5):
    bsz, t, d = x.shape
    c = z.shape[2]
    tm = min(512, t)
    tokd = pl.BlockSpec((1, tm, d), lambda b, i: (b, i, 0))
    tokc = pl.BlockSpec((1, tm, c), lambda b, i: (b, i, 0))
    return pl.pallas_call(
        _hy_out_kernel,
        grid=(bsz, t // tm),
        in_specs=[tokd, tokc, tokc, tokc, pl.BlockSpec((1, c), lambda b, i: (0, 0)),
                  pl.BlockSpec(w.shape, lambda b, i: (0, 0)), pl.BlockSpec((1, 1, d), lambda b, i: (b, 0, 0))],
        out_specs=tokd,
        out_shape=jax.ShapeDtypeStruct(x.shape, F32),
        compiler_params=_cparams(("parallel", "parallel")),
        name="hyena_out",
    )(x, conv, z, gate, bias.reshape(1, c), w, m5)


def _hyena_filter_spectrum(t, fw1, fb1, fw2, fb2, fw3, fb3, fw4, freq, width):
    pos = jnp.arange(t, dtype=F32)[:, None]
    tt = pos / max(t - 1, 1)
    ang = 2 * math.pi * pos / t
    nb = (HY_EMB - 1) // 2
    bands = jnp.linspace(1e-4, nb - 1, nb, dtype=F32)[None]
    feats = jnp.concatenate([tt, jnp.cos(bands * ang), -jnp.sin(bands * ang)], axis=-1)
    hdn = jnp.sin(freq * (feats @ fw1 + fb1))
    hdn = jnp.sin(freq * (hdn @ fw2 + fb2))
    hdn = jnp.sin(freq * (hdn @ fw3 + fb3))
    filt = (hdn @ fw4).reshape(t, HY_ORDER, 2, width)
    deltas = jnp.abs(jnp.linspace(HY_MIN_DECAY, HY_MAX_DECAY, width, dtype=F32))
    filt = filt * jnp.exp(-tt[:, :, None, None] * deltas)
    fwd, bwd = filt[:, :, 0], filt[:, :, 1]
    kern = jnp.concatenate([fwd, jnp.zeros_like(fwd[:1]), bwd[:0:-1]], axis=0)
    kern = kern * lax.rsqrt(jnp.sum(kern * kern, axis=0, keepdims=True) + 1e-6)
    spec = jnp.fft.fft(kern, axis=0)
    return jnp.stack([jnp.real(spec), jnp.imag(spec)], axis=1).transpose(2, 1, 0, 3).astype(F32)


def _block_diag_pair(m):
    z = jnp.zeros_like(m[0])
    return jnp.concatenate([jnp.concatenate([m[0], z], axis=1), jnp.concatenate([z, m[1]], axis=1)], axis=0)


def _even_mixer(lat, cx, ml, mc, gain, prm, ctx_out):
    (w_in, mu, w0, w_up, a0, a_up, g_up, k_k, k_a, r_k, gn_g, gn_b,
     lam_re, lam_im, log_dt, b_re, b_im, c_re, c_im, d_skip, w_glu, b_glu, w_out) = prm
    bsz, t_lat, d = lat.shape
    t_ctx = cx.shape[1]
    width = k_k.shape[0]
    heads = width // HEAD_DIM
    n_cols = mu.shape[0]
    s5w = d_skip.shape[0]
    ttot = t_ctx + t_lat
    nch = bsz * heads
    tm = t_ctx
    assert tm % GRID_W == 0 and t_lat % tm == 0 and 4 * nch == LANES and 2 * bsz == SUBLANES

    p_all = mod_matmul_stream(cx, lat, gain, mc, ml, w_in.astype(BF16), tm)

    head_of = np.arange(width) // HEAD_DIM
    ones = jnp.asarray(head_of[:, None] == head_of[None, :], F32)
    feat_consts = (mu.reshape(1, -1), _block_diag_pair(w_up).astype(BF16), w0.reshape(1, -1),
                   _block_diag_pair(a_up).astype(BF16), a0.reshape(1, -1), g_up.astype(BF16),
                   k_k.reshape(1, -1), k_a.reshape(1, -1), r_k.reshape(1, -1), ones)
    kk, v, g, bonus, r, k, w2, a2 = rwkv_features(p_all, n_cols, width, feat_consts, tm)

    def key_major(x):
        x = x.reshape(bsz, ttot, heads, 2, HALF_HEAD).transpose(1, 4, 3, 0, 2)[:, :, None]
        return jnp.broadcast_to(x, (ttot, HALF_HEAD, 2, 2, bsz, heads)).reshape(ttot, HALF_HEAD, LANES)

    def key_major2(x):
        x = x.reshape(2, bsz, ttot, heads, 2, HALF_HEAD)
        return x.transpose(2, 5, 0, 4, 1, 3).reshape(ttot, HALF_HEAD, LANES)

    def value_major(x):
        x = x.reshape(bsz, ttot, heads, HEAD_DIM).transpose(1, 3, 0, 2).reshape(ttot, HEAD_DIM, nch)
        return jnp.concatenate([x, x], axis=-1)

    ka_t = k_a.reshape(heads, 2, HALF_HEAD).transpose(2, 1, 0)[:, None, :, None, :]
    ka_t = jnp.broadcast_to(ka_t, (HALF_HEAD, 2, 2, bsz, heads)).reshape(HALF_HEAD, LANES)
    o_f, o_b = rwkv_scan(key_major(r), key_major(kk), key_major(k), key_major2(w2), key_major2(a2),
                         value_major(v), ka_t, t_ctx)

    def token_major(o):
        o = o[..., :nch].reshape(ttot, HEAD_DIM, bsz, heads)
        return o.transpose(2, 0, 3, 1).reshape(bsz, ttot, width)

    o_f, o_b = token_major(o_f), token_major(o_b)

    ng = lam_re.shape[1]
    gps = ng // S5_SUPER
    lam = lax.complex(lam_re, lam_im)
    dt = jnp.exp(log_dt)[..., None]
    a_bar = jnp.exp(lam * dt)
    b_bar = ((a_bar - 1) / lam)[..., None] * lax.complex(b_re, b_im)
    eye = jnp.eye(gps, dtype=F32)

    def b_mat(x):
        x = x.reshape(2, S5_SUPER, gps, S5_STATE, S5_GROUP)
        m = jnp.einsum('dsgph,gk->sdghkp', x, eye)
        return m.reshape(S5_SUPER, 2 * gps * S5_GROUP, gps * S5_STATE)

    def c_mat(x):
        x = x.reshape(2, S5_SUPER, gps, S5_GROUP, S5_STATE)
        m = jnp.einsum('dsghp,gk->skpdgh', x, eye)
        return m.reshape(S5_SUPER, gps * S5_STATE, 2 * gps * S5_GROUP)

    bm = jnp.concatenate([b_mat(jnp.real(b_bar)), b_mat(jnp.imag(b_bar))], axis=2).astype(BF16)
    cm = jnp.concatenate([c_mat(c_re), -c_mat(c_im)], axis=1).astype(BF16)

    def a_rows(x):
        x = x.reshape(2, 1, S5_SUPER, gps * S5_STATE)
        return jnp.broadcast_to(x, (2, bsz, S5_SUPER, gps * S5_STATE)).reshape(2 * bsz, S5_SUPER, -1)

    a_arr = jnp.concatenate([a_rows(jnp.real(a_bar)), a_rows(jnp.imag(a_bar))], axis=2)
    a_arr = a_arr.reshape(1, 2 * bsz, -1)

    u_all = p_all[..., n_cols:]
    u_tm = u_all.transpose(1, 0, 2)
    y_f, y_b = s5_scan(jnp.concatenate([u_tm, u_tm], axis=1), bm, cm, a_arr, t_ctx)
    ys = (y_f[:, :bsz] + y_b[:, bsz:]).transpose(1, 0, 2)

    wo = w_out.astype(BF16)
    out_consts = (ones, gn_g.reshape(1, -1), gn_b.reshape(1, -1), d_skip.reshape(1, -1),
                  w_glu.astype(BF16), b_glu.reshape(1, -1), wo[:width], wo[width:])
    lat = even_out(lat, 1, o_f, o_b, bonus, g, ys, u_all, ml[5], out_consts, tm)
    if ctx_out:
        cx = even_out(cx, 0, o_f, o_b, bonus, g, ys, u_all, mc[5], out_consts, tm)
    return lat, cx


def _hyena_mixer(x, gain, m, prm):
    (w_in, conv_w, conv_b, fw1, fb1, fw2, fb2, fw3, fb3, fw4, freq, bias_d, w_out) = prm
    bsz, t, d = x.shape
    c = w_out.shape[0]
    kf = _hyena_filter_spectrum(t, fw1, fb1, fw2, fb2, fw3, fb3, fw4, freq, c)
    z, zb, g1, g2 = hyena_in(x, gain, m[3], m[4], w_in.astype(BF16), conv_w, conv_b)
    gates = (g1, g2)
    use_fft = t % (FFT_N2 * 2) == 0 and t >= 4 * FFT_N2
    for n in range(HY_ORDER):
        if use_fft:
            nd = t // FFT_N2
            n1 = 2 * nd
            kfn = kf[n].reshape(2, FFT_N2, n1, c).transpose(0, 2, 1, 3)
            zt = zb.reshape(bsz, nd, FFT_N2, c).transpose(0, 2, 1, 3)
            conv = fft_conv(zt, kfn).transpose(0, 2, 1, 3).reshape(bsz, t, c)
        else:
            conv = dft_conv(zb, kf[n])
        if n < HY_ORDER - 1:
            z, zb = hyena_gate(conv, z, gates[n], bias_d[n])
        else:
            return hyena_out(x, conv, z, gates[n], bias_d[n], w_out.astype(BF16), m[5])


def kernel(x, c, ctx, c_ctx, norm_g, ada_w, ada_b, ffn_wg, ffn_wu, ffn_wd, final_g, ev_w_in, ev_mu, ev_w0, ev_w_up, ev_a0, ev_a_up, ev_g_up, ev_k_k, ev_k_a, ev_r_k, ev_gn_g, ev_gn_b, ev_lam_re, ev_lam_im, ev_log_dt, ev_b_re, ev_b_im, ev_c_re, ev_c_im, ev_d, ev_w_glu, ev_b_glu, ev_w_out, od_w_in, od_conv_w, od_conv_b, od_fw1, od_fb1, od_fw2, od_fb2, od_fw3, od_fb3, od_fw4, od_freq, od_bias, od_w_out):
    depth = norm_g.shape[0]
    bsz, _, d = x.shape
    n_even = (depth + 1) // 2
    last_ctx = 2 * (n_even - 1)

    cond8 = jnp.concatenate([c, c_ctx[None], jnp.zeros((8 - bsz - 1, d), F32)], axis=0)
    mods = ada_mods_all(cond8, ada_w, ada_b)

    wg, wu, wd = ffn_wg.astype(BF16), ffn_wu.astype(BF16), ffn_wd.astype(BF16)
    lat, cx = x, ctx
    for l in range(depth):
        run_ctx = l <= last_ctx
        ctx_out = l < last_ctx
        i = l // 2
        ml = [mods[l, :bsz, None, k * d:(k + 1) * d] for k in range(N_MOD)]
        mc = [jnp.broadcast_to(mods[l, bsz:bsz + 1, None, k * d:(k + 1) * d], (bsz, 1, d))
              for k in range(N_MOD)]
        lat = ffn_half(lat, norm_g[l, 0], ml[0], ml[1], ml[2], wg[l, 0], wu[l, 0], wd[l, 0])
        if run_ctx:
            cx = ffn_half(cx, norm_g[l, 0], mc[0], mc[1], mc[2], wg[l, 0], wu[l, 0], wd[l, 0])
        if l % 2 == 0:
            prm = (ev_w_in[i], ev_mu[i], ev_w0[i], ev_w_up[i], ev_a0[i], ev_a_up[i], ev_g_up[i],
                   ev_k_k[i], ev_k_a[i], ev_r_k[i], ev_gn_g[i], ev_gn_b[i],
                   ev_lam_re[i], ev_lam_im[i], ev_log_dt[i], ev_b_re[i], ev_b_im[i], ev_c_re[i], ev_c_im[i],
                   ev_d[i], ev_w_glu[i], ev_b_glu[i], ev_w_out[i])
            lat, cx = _even_mixer(lat, cx, ml, mc, norm_g[l, 1], prm, ctx_out)
        else:
            prm = (od_w_in[i], od_conv_w[i], od_conv_b[i], od_fw1[i], od_fb1[i], od_fw2[i], od_fb2[i],
                   od_fw3[i], od_fb3[i], od_fw4[i], od_freq[i], od_bias[i], od_w_out[i])
            lat = _hyena_mixer(lat, norm_g[l, 1], ml, prm)
            if ctx_out:
                cx = _hyena_mixer(cx, norm_g[l, 1], mc, prm)
        fin = final_g if l == depth - 1 else None
        lat = ffn_half(lat, norm_g[l, 2], ml[6], ml[7], ml[8], wg[l, 1], wu[l, 1], wd[l, 1], fin)
        if ctx_out:
            cx = ffn_half(cx, norm_g[l, 2], mc[6], mc[7], mc[8], wg[l, 1], wu[l, 1], wd[l, 1])
    return lat
```

```python
import functools
import math

import numpy as np
import jax
import jax.numpy as jnp
from jax import lax
from jax.experimental import pallas as pl
from jax.experimental.pallas import tpu as pltpu

F32 = jnp.float32
BF16 = jnp.bfloat16
HIGHEST = lax.Precision.HIGHEST

N_MOD = 9
NORM_EPS = 1e-6
GN_EPS = 64e-5
GRID_W = 64
HEAD_DIM = 64
HALF_HEAD = HEAD_DIM // 2
S5_GROUP = 16
S5_STATE = 64
S5_SUPER = 4
HY_ORDER = 2
HY_EMB = 33
HY_MIN_DECAY = math.log(1e-2) / 1.5
HY_MAX_DECAY = math.log(1e-2) / 0.3
LANES = 128
SUBLANES = 8
FFT_N2 = 128
VMEM_LIMIT = 56 * 1024 * 1024


def _cparams(sem, vmem=VMEM_LIMIT):
    return pltpu.CompilerParams(dimension_semantics=sem, vmem_limit_bytes=vmem)


def _dot(a, b):
    return jnp.dot(a, b, preferred_element_type=F32)


def _dot_exact(a, b):
    return jnp.dot(a, b, preferred_element_type=F32, precision=HIGHEST)


def _rms_mod(x, gain, shift, scale):
    ms = jnp.mean(x * x, axis=-1, keepdims=True)
    return x * lax.rsqrt(ms + NORM_EPS) * gain * (1.0 + scale) + shift


def _sigmoid(x):
    return 1.0 / (1.0 + jnp.exp(-x))


def _silu(x):
    return x * _sigmoid(x)


def _ada_kernel(c_ref, w_ref, b_ref, o_ref):
    s = _silu(c_ref[...])
    o_ref[0] = _dot(s.astype(BF16), w_ref[0].astype(BF16)) + b_ref[0]


def ada_mods_all(cond8, ada_w, ada_b):
    depth, d, n = ada_w.shape
    tn = n // 8
    return pl.pallas_call(
        _ada_kernel,
        grid=(depth, n // tn),
        in_specs=[pl.BlockSpec((8, d), lambda l, j: (0, 0)),
                  pl.BlockSpec((1, d, tn), lambda l, j: (l, 0, j)),
                  pl.BlockSpec((1, 1, tn), lambda l, j: (l, 0, j))],
        out_specs=pl.BlockSpec((1, 8, tn), lambda l, j: (l, 0, j)),
        out_shape=jax.ShapeDtypeStruct((depth, 8, n), F32),
        compiler_params=_cparams(("parallel", "parallel")),
        name="ada_mods",
    )(cond8, ada_w, ada_b.reshape(depth, 1, n))


def _ffn_kernel(x_ref, gain_ref, shift_ref, scale_ref, gate_ref, wg_ref, wu_ref, wd_ref, fg_ref,
                o_ref, h_ref, acc_ref, *, final_norm):
    k = pl.program_id(2)

    @pl.when(k == 0)
    def _():
        h = _rms_mod(x_ref[0], gain_ref[...], shift_ref[0], scale_ref[0])
        h_ref[...] = h.astype(BF16)
        acc_ref[...] = jnp.zeros_like(acc_ref)

    h = h_ref[...]
    g = _dot(h, wg_ref[...])
    u = _dot(h, wu_ref[...])
    a = (_silu(g) * u).astype(BF16)
    acc_ref[...] += _dot(a, wd_ref[...])

    @pl.when(k == pl.num_programs(2) - 1)
    def _():
        y = x_ref[0] + 0.5 * gate_ref[0] * acc_ref[...]
        if final_norm:
            ms = jnp.mean(y * y, axis=-1, keepdims=True)
            y = y * lax.rsqrt(ms + NORM_EPS) * fg_ref[...]
        o_ref[0] = y


def ffn_half(x, gain, shift, scale, gate, wg, wu, wd, final_g=None):
    bsz, t, d = x.shape
    ff = wg.shape[1]
    tm = min(512, t)
    tf = ff // 2
    fg = jnp.ones((1, d), F32) if final_g is None else final_g.reshape(1, d)
    vec = pl.BlockSpec((1, 1, d), lambda b, i, k: (b, 0, 0))
    return pl.pallas_call(
        functools.partial(_ffn_kernel, final_norm=final_g is not None),
        grid=(bsz, t // tm, ff // tf),
        in_specs=[pl.BlockSpec((1, tm, d), lambda b, i, k: (b, i, 0)),
                  pl.BlockSpec((1, d), lambda b, i, k: (0, 0)),
                  vec, vec, vec,
                  pl.BlockSpec((d, tf), lambda b, i, k: (0, k)),
                  pl.BlockSpec((d, tf), lambda b, i, k: (0, k)),
                  pl.BlockSpec((tf, d), lambda b, i, k: (k, 0)),
                  pl.BlockSpec((1, d), lambda b, i, k: (0, 0))],
        out_specs=pl.BlockSpec((1, tm, d), lambda b, i, k: (b, i, 0)),
        out_shape=jax.ShapeDtypeStruct((bsz, t, d), F32),
        scratch_shapes=[pltpu.VMEM((tm, d), BF16), pltpu.VMEM((tm, d), F32)],
        compiler_params=_cparams(("parallel", "parallel", "arbitrary")),
        name="ffn_half",
    )(x, gain.reshape(1, d), shift, scale, gate, wg, wu, wd, fg)


def _modmm_kernel(cx_ref, lat_ref, gain_ref, shc_ref, scc_ref, shl_ref, scl_ref, w_ref, o_ref, *, nc):
    is_ctx = pl.program_id(1) < nc
    x = jnp.where(is_ctx, cx_ref[0], lat_ref[0])
    shift = jnp.where(is_ctx, shc_ref[0], shl_ref[0])
    scale = jnp.where(is_ctx, scc_ref[0], scl_ref[0])
    h = _rms_mod(x, gain_ref[...], shift, scale)
    o_ref[0] = _dot(h.astype(BF16), w_ref[...])


def mod_matmul_stream(cx, lat, gain, mc, ml, w, tm):
    bsz, t_ctx, d = cx.shape
    t_lat = lat.shape[1]
    n = w.shape[1]
    nc = t_ctx // tm
    nt = nc + t_lat // tm
    vec = pl.BlockSpec((1, 1, d), lambda b, i: (b, 0, 0))
    return pl.pallas_call(
        functools.partial(_modmm_kernel, nc=nc),
        grid=(bsz, nt),
        in_specs=[pl.BlockSpec((1, tm, d), lambda b, i: (b, jnp.minimum(i, nc - 1), 0)),
                  pl.BlockSpec((1, tm, d), lambda b, i: (b, jnp.maximum(i - nc, 0), 0)),
                  pl.BlockSpec((1, d), lambda b, i: (0, 0)),
                  vec, vec, vec, vec,
                  pl.BlockSpec((d, n), lambda b, i: (0, 0))],
        out_specs=pl.BlockSpec((1, tm, n), lambda b, i: (b, i, 0)),
        out_shape=jax.ShapeDtypeStruct((bsz, t_ctx + t_lat, n), F32),
        compiler_params=_cparams(("parallel", "parallel")),
        name="mod_matmul",
    )(cx, lat, gain.reshape(1, d), mc[3], mc[4], ml[3], ml[4], w)


def _rwkv_feat_kernel(p_ref, pu_ref, pd_ref, mu_ref, wup_ref, w0_ref, aup_ref, a0_ref, gup_ref,
                      kk_s_ref, ka_ref, rk_ref, ones_ref,
                      kk_o, v_o, g_o, bonus_o, r_o, k_o, w_o, a_o, *, nt, width):
    i = pl.program_id(1)
    is_ctx = i == 0
    p = p_ref[0]
    tm = p.shape[0]
    row = lax.broadcasted_iota(jnp.int32, p.shape, 0)
    lane = lax.broadcasted_iota(jnp.int32, p.shape, 1) % 4
    prev = pltpu.roll(p, 1, 0)
    nxt = pltpu.roll(p, tm - 1, 0)
    col = jnp.where(is_ctx, row, row % GRID_W)
    last = jnp.where(is_ctx, tm - 1, GRID_W - 1)
    left = jnp.where(col == 0, 0.0, prev)
    right = jnp.where(col == last, 0.0, nxt)
    up_halo = jnp.where(i > 1, pu_ref[0], 0.0)
    dn_halo = jnp.where(i < nt - 1, pd_ref[0], 0.0)
    up = jnp.where(is_ctx, left, jnp.concatenate([up_halo, p[:tm - GRID_W]], axis=0))
    down = jnp.where(is_ctx, right, jnp.concatenate([p[GRID_W:], dn_halo], axis=0))
    shifted = jnp.where(lane == 0, left, jnp.where(lane == 1, right, jnp.where(lane == 2, up, down)))
    q = p + mu_ref[...] * (shifted - p)

    w = width
    r, k, v = q[:, :w], q[:, w:2 * w], q[:, 2 * w:3 * w]
    wd = q[:, 3 * w:3 * w + LANES]
    ad = q[:, 3 * w + LANES:3 * w + 2 * LANES]
    gd = q[:, 3 * w + 2 * LANES:3 * w + 3 * LANES]

    zlin = w0_ref[...] + _dot(jnp.tanh(wd).astype(BF16), wup_ref[...])
    neg = -zlin
    softplus = jnp.maximum(neg, 0.0) + jnp.log(1.0 + jnp.exp(-jnp.abs(neg)))
    decay = jnp.exp(-jnp.exp(-softplus - 0.5))
    a = _sigmoid(a0_ref[...] + _dot(ad.astype(BF16), aup_ref[...]))
    g_o[0] = _dot(_sigmoid(gd).astype(BF16), gup_ref[...])

    ones = ones_ref[...]
    kk = k * kk_s_ref[...]
    ss = _dot_exact(kk * kk, ones)
    kk_o[0] = kk / jnp.maximum(jnp.sqrt(ss), 1e-12)
    v_o[0] = v
    r_o[0] = r
    k_o[0] = k
    ksum = jnp.zeros_like(k)
    for d in range(2):
        a_d = a[:, d * w:(d + 1) * w]
        ksum = ksum + k * (1.0 + (a_d - 1.0) * ka_ref[...])
        w_o[d, 0] = decay[:, d * w:(d + 1) * w]
        a_o[d, 0] = a_d
    bonus_o[0] = _dot_exact(r * ksum * rk_ref[...], ones) * v


def rwkv_features(p, n_cols, width, consts, tm):
    bsz, t, _ = p.shape
    nt = t // tm
    hb = tm // GRID_W
    nhb = t // GRID_W
    w = width
    full = lambda a: pl.BlockSpec(a.shape, lambda b, i: (0,) * a.ndim)
    tok = pl.BlockSpec((1, tm, w), lambda b, i: (b, i, 0))
    tok2 = pl.BlockSpec((2, 1, tm, w), lambda b, i: (0, b, i, 0))
    sds = jax.ShapeDtypeStruct((bsz, t, w), F32)
    sds2 = jax.ShapeDtypeStruct((2, bsz, t, w), F32)
    return pl.pallas_call(
        functools.partial(_rwkv_feat_kernel, nt=nt, width=w),
        grid=(bsz, nt),
        in_specs=[pl.BlockSpec((1, tm, n_cols), lambda b, i: (b, i, 0)),
                  pl.BlockSpec((1, GRID_W, n_cols), lambda b, i: (b, jnp.maximum(i * hb - 1, 0), 0)),
                  pl.BlockSpec((1, GRID_W, n_cols), lambda b, i: (b, jnp.minimum((i + 1) * hb, nhb - 1), 0))]
                 + [full(a) for a in consts],
        out_specs=[tok] * 6 + [tok2, tok2],
        out_shape=[sds] * 6 + [sds2, sds2],
        compiler_params=_cparams(("parallel", "parallel")),
        name="rwkv_features",
    )(p, p, p, *consts)


def _rwkv_scan_kernel(rf, rb, kkf, kkb, kf, kb, wf, wb, af, ab, vf, vb, ka_ref, of_ref, ob_ref,
                      s_ref, vec_ref, v_ref, c_ref, *, tc):
    @pl.when(pl.program_id(0) == 0)
    def _():
        s_ref[...] = jnp.zeros_like(s_ref)
        vec_ref[:, tc] = jnp.zeros((vec_ref.shape[0],) + vec_ref.shape[2:], F32)

    nj, ni = s_ref.shape[0], s_ref.shape[1]
    quarter = LANES // 4

    def fold(x):
        hi = (lax.broadcasted_iota(jnp.int32, x.shape, 1) % (2 * quarter)) >= quarter
        return x + jnp.where(hi, pltpu.roll(x, quarter, 1), pltpu.roll(x, 3 * quarter, 1))

    def jsum(x):
        s = jnp.broadcast_to(jnp.sum(x, axis=0, keepdims=True), (SUBLANES, LANES))
        return fold(s)[0:1]

    ka = ka_ref[...]
    fwd_lane = lax.broadcasted_iota(jnp.int32, (nj, LANES), 1) < LANES // 2
    for t in range(tc):
        merged = lambda f, b: jnp.where(fwd_lane, f[t], b[tc - 1 - t])
        r, kk, k = merged(rf, rb), merged(kkf, kkb), merged(kf, kb)
        w, a = merged(wf, wb), merged(af, ab)
        kka = kk * a
        kd = k * (1.0 + (a - 1.0) * ka)
        vec_ref[0, t] = kk
        vec_ref[1, t] = w * r
        vec_ref[2, t] = w
        vec_ref[3, t] = kka
        vec_ref[4, t] = kd
        c_ref[t, 0:1, :] = jsum(kka * r)
        c_ref[t, 1:2, :] = jsum(kd * r)
        v_ref[t] = jnp.concatenate([vf[t], vb[tc - 1 - t]], axis=-1)

    nh = 2
    hr = ni // nh

    def partial_sums(t, rows):
        sums = []
        for h in range(nh):
            sa = jnp.zeros((hr, LANES), F32)
            o1 = jnp.zeros((hr, LANES), F32)
            for j in range(nj):
                sj = rows(j, h)
                sa = sa + sj * vec_ref[0, t, j:j + 1, :]
                o1 = o1 + sj * vec_ref[1, t, j:j + 1, :]
            sums += [sa, o1]
        return tuple(sums)

    def step(t, carry):
        sas, vs = [], []
        for h in range(nh):
            sa, o1 = fold(carry[2 * h]), fold(carry[2 * h + 1])
            v = v_ref[t, h * hr:(h + 1) * hr, :]
            out = o1 - sa * c_ref[t, 0:1, :] + v * c_ref[t, 1:2, :]
            of_ref[t, h * hr:(h + 1) * hr, :] = out[:, :LANES // 2]
            ob_ref[tc - 1 - t, h * hr:(h + 1) * hr, :] = out[:, LANES // 2:]
            sas.append(sa)
            vs.append(v)

        def updated(j, h):
            rows = slice(h * hr, (h + 1) * hr)
            sj = (s_ref[j, rows, :] * vec_ref[2, t, j:j + 1, :] - sas[h] * vec_ref[3, t, j:j + 1, :]
                  + vs[h] * vec_ref[4, t, j:j + 1, :])
            s_ref[j, rows, :] = sj
            return sj

        return partial_sums(t + 1, updated)

    lax.fori_loop(0, tc, step, partial_sums(0, lambda j, h: s_ref[j, h * hr:(h + 1) * hr, :]))


def rwkv_scan(r, kk, k, w, a, v, ka, t_ctx):
    ttot, nj, _ = r.shape
    ni, nc = v.shape[1], v.shape[2]
    tc = 32
    ncb, ntb = t_ctx // tc, ttot // tc
    fwd = lambda g: (g, 0, 0)
    bwd = lambda g: (jnp.where(g < ncb, ncb - 1 - g, ntb + ncb - 1 - g), 0, 0)
    jf, jb = pl.BlockSpec((tc, nj, LANES), fwd), pl.BlockSpec((tc, nj, LANES), bwd)
    vf, vb = pl.BlockSpec((tc, ni, nc), fwd), pl.BlockSpec((tc, ni, nc), bwd)
    osd = jax.ShapeDtypeStruct((ttot, ni, nc), F32)
    return pl.pallas_call(
        functools.partial(_rwkv_scan_kernel, tc=tc),
        grid=(ntb,),
        in_specs=[jf, jb] * 5 + [vf, vb, pl.BlockSpec(ka.shape, lambda g: (0, 0))],
        out_specs=[vf, vb],
        out_shape=[osd, osd],
        scratch_shapes=[pltpu.VMEM((nj, ni, LANES), F32), pltpu.VMEM((5, tc + 1, nj, LANES), F32),
                        pltpu.VMEM((tc, ni, LANES), F32), pltpu.VMEM((tc, SUBLANES, LANES), F32)],
        compiler_params=_cparams(("arbitrary",)),
        name="rwkv_scan",
    )(r, r, kk, kk, k, k, w, w, a, a, v, v, ka)


def _s5_kernel(uf_ref, ub_ref, bm_ref, cm_ref, a_ref, yf_ref, yb_ref, h_ref, bu_ref, ubr_ref, *, tt):
    @pl.when(pl.program_id(0) == 0)
    def _():
        h_ref[...] = jnp.zeros_like(h_ref)

    nsb = bm_ref.shape[0]
    kin = bm_ref.shape[1] // 2
    sw = bm_ref.shape[2]
    hw = sw // 2
    for t in range(tt):
        ubr_ref[t] = ub_ref[tt - 1 - t]
    fwd_row3 = lax.broadcasted_iota(jnp.int32, uf_ref.shape, 1) < SUBLANES // 2
    uf = jnp.where(fwd_row3, uf_ref[...], 0.0)
    ub = jnp.where(fwd_row3, 0.0, ubr_ref[...])
    for sb in range(nsb):
        lhs = jnp.concatenate([uf[:, :, sb * kin:(sb + 1) * kin], ub[:, :, sb * kin:(sb + 1) * kin]], axis=-1)
        lhs = lhs.reshape(tt * SUBLANES, 2 * kin).astype(BF16)
        bu_ref[:, sb * sw:(sb + 1) * sw] = _dot(lhs, bm_ref[sb])

    def step(t, hs):
        rows = pl.ds(pl.multiple_of(t * SUBLANES, SUBLANES), SUBLANES)
        out = []
        for sb in range(nsb):
            hr, hi = hs[2 * sb], hs[2 * sb + 1]
            lo = sb * sw
            ar = a_ref[0, :, lo:lo + hw]
            ai = a_ref[0, :, lo + hw:lo + sw]
            nr = ar * hr - ai * hi + bu_ref[rows, lo:lo + hw]
            ni = ar * hi + ai * hr + bu_ref[rows, lo + hw:lo + sw]
            bu_ref[rows, lo:lo + hw] = nr
            bu_ref[rows, lo + hw:lo + sw] = ni
            out += [nr, ni]
        return tuple(out)

    h0 = []
    for sb in range(nsb):
        h0 += [h_ref[:, sb * sw:sb * sw + hw], h_ref[:, sb * sw + hw:(sb + 1) * sw]]
    hs = lax.fori_loop(0, tt, step, tuple(h0))
    for sb in range(nsb):
        h_ref[:, sb * sw:sb * sw + hw] = hs[2 * sb]
        h_ref[:, sb * sw + hw:(sb + 1) * sw] = hs[2 * sb + 1]

    nout = cm_ref.shape[2] // 2
    fwd_row = (lax.broadcasted_iota(jnp.int32, (tt * SUBLANES, nout), 0) % SUBLANES) < SUBLANES // 2
    ys = []
    for sb in range(nsb):
        yy = _dot(bu_ref[:, sb * sw:(sb + 1) * sw].astype(BF16), cm_ref[sb])
        ys.append(jnp.where(fwd_row, yy[:, :nout], yy[:, nout:]))
    y = jnp.concatenate(ys, axis=1).reshape(tt, SUBLANES, nsb * nout)
    yf_ref[...] = y
    for t in range(tt):
        yb_ref[tt - 1 - t] = y[t]


def s5_scan(u, bm, cm, a, t_ctx):
    ttot, rows, width = u.shape
    nsb, _, sw = bm.shape
    tt = 64
    ncb, ntb = t_ctx // tt, ttot // tt
    fwd = pl.BlockSpec((tt, rows, width), lambda g: (g, 0, 0))
    bwd = pl.BlockSpec((tt, rows, width), lambda g: (jnp.where(g < ncb, ncb - 1 - g, ntb + ncb - 1 - g), 0, 0))
    full = lambda x: pl.BlockSpec(x.shape, lambda g: (0,) * x.ndim)
    osd = jax.ShapeDtypeStruct(u.shape, F32)
    return pl.pallas_call(
        functools.partial(_s5_kernel, tt=tt),
        grid=(ntb,),
        in_specs=[fwd, bwd, full(bm), full(cm), full(a)],
        out_specs=[fwd, bwd],
        out_shape=[osd, osd],
        scratch_shapes=[pltpu.VMEM((rows, nsb * sw), F32), pltpu.VMEM((tt * rows, nsb * sw), F32),
                        pltpu.VMEM((tt, rows, width), F32)],
        compiler_params=_cparams(("arbitrary",)),
        name="s5_scan",
    )(u, u, bm, cm, a)


def _even_out_kernel(x_ref, of_ref, ob_ref, bonus_ref, g_ref, ys_ref, u_ref, m5_ref,
                     ones_ref, gng_ref, gnb_ref, dskip_ref, wglu_ref, bglu_ref, wo1_ref, wo2_ref, o_ref):
    ones = ones_ref[...]
    inv = 1.0 / HEAD_DIM
    o = of_ref[0] + ob_ref[0]
    mean = _dot_exact(o, ones) * inv
    oc = o - mean
    var = _dot_exact(oc * oc, ones) * inv
    y1 = (oc * lax.rsqrt(var + GN_EPS) * gng_ref[...] + gnb_ref[...] + bonus_ref[0]) * g_ref[0]
    y = ys_ref[0] + dskip_ref[...] * u_ref[0]
    y = jax.nn.gelu(y)
    y2 = y * _sigmoid(_dot(y.astype(BF16), wglu_ref[...]) + bglu_ref[...])
    out = _dot(y1.astype(BF16), wo1_ref[...]) + _dot(y2.astype(BF16), wo2_ref[...])
    o_ref[0] = x_ref[0] + m5_ref[0] * out


def even_out(x, off, o_f, o_b, bonus, g, ys, u, m5, consts, tm):
    bsz, t, d = x.shape
    w = o_f.shape[2]
    tokd = pl.BlockSpec((1, tm, d), lambda b, i: (b, i, 0))
    tokw = pl.BlockSpec((1, tm, w), lambda b, i: (b, i + off, 0))
    toks = pl.BlockSpec((1, tm, u.shape[2]), lambda b, i: (b, i + off, 0))
    full = lambda a: pl.BlockSpec(a.shape, lambda b, i: (0,) * a.ndim)
    return pl.pallas_call(
        _even_out_kernel,
        grid=(bsz, t // tm),
        in_specs=[tokd, tokw, tokw, tokw, tokw, toks, toks,
                  pl.BlockSpec((1, 1, d), lambda b, i: (b, 0, 0))] + [full(a) for a in consts],
        out_specs=tokd,
        out_shape=jax.ShapeDtypeStruct((bsz, t, d), F32),
        compiler_params=_cparams(("parallel", "parallel")),
        name="even_out",
    )(x, o_f, o_b, bonus, g, ys, u, m5, *consts)


def _hy_in_kernel(x_ref, xp_ref, xn_ref, gain_ref, shift_ref, scale_ref, w_ref, cw_ref, cb_ref,
                  z_ref, zb_ref, g1_ref, g2_ref, *, nt, c):
    i = pl.program_id(1)
    gain, shift, scale = gain_ref[...], shift_ref[0], scale_ref[0]
    h = _rms_mod(x_ref[0], gain, shift, scale).astype(BF16)
    hp = _rms_mod(xp_ref[0], gain, shift, scale).astype(BF16)
    hn = _rms_mod(xn_ref[0], gain, shift, scale).astype(BF16)
    tm = h.shape[0]
    row = lax.broadcasted_iota(jnp.int32, (tm, c), 0)
    outs = (z_ref, g1_ref, g2_ref)
    for part in range(3):
        wp = w_ref[:, part * c:(part + 1) * c]
        p = _dot(h, wp)
        pp = jnp.where(i > 0, _dot(hp, wp)[7:8], 0.0)
        pn = jnp.where(i < nt - 1, _dot(hn, wp)[0:1], 0.0)
        pm1 = jnp.where(row == 0, pp, pltpu.roll(p, 1, 0))
        pp1 = jnp.where(row == tm - 1, pn, pltpu.roll(p, tm - 1, 0))
        cw = cw_ref[:, part * c:(part + 1) * c]
        q = cw[0:1] * pm1 + cw[1:2] * p + cw[2:3] * pp1 + cb_ref[:, part * c:(part + 1) * c]
        outs[part][0] = q
        if part == 0:
            zb_ref[0] = q.astype(BF16)


def hyena_in(x, gain, shift, scale, w, conv_w, conv_b):
    bsz, t, d = x.shape
    c = w.shape[1] // 3
    tm = min(512, t)
    nt = t // tm
    hb = tm // 8
    nhb = t // 8
    vec = pl.BlockSpec((1, 1, d), lambda b, i: (b, 0, 0))
    tok = pl.BlockSpec((1, tm, c), lambda b, i: (b, i, 0))
    sds = jax.ShapeDtypeStruct((bsz, t, c), F32)
    return pl.pallas_call(
        functools.partial(_hy_in_kernel, nt=nt, c=c),
        grid=(bsz, nt),
        in_specs=[pl.BlockSpec((1, tm, d), lambda b, i: (b, i, 0)),
                  pl.BlockSpec((1, 8, d), lambda b, i: (b, jnp.maximum(i * hb - 1, 0), 0)),
                  pl.BlockSpec((1, 8, d), lambda b, i: (b, jnp.minimum((i + 1) * hb, nhb - 1), 0)),
                  pl.BlockSpec((1, d), lambda b, i: (0, 0)),
                  vec, vec,
                  pl.BlockSpec(w.shape, lambda b, i: (0, 0)),
                  pl.BlockSpec(conv_w.shape, lambda b, i: (0, 0)),
                  pl.BlockSpec((1, 3 * c), lambda b, i: (0, 0))],
        out_specs=[tok, tok, tok, tok],
        out_shape=[sds, jax.ShapeDtypeStruct((bsz, t, c), BF16), sds, sds],
        compiler_params=_cparams(("parallel", "parallel")),
        name="hyena_in",
    )(x, x, x, gain.reshape(1, d), shift, scale, w, conv_w, conv_b.reshape(1, 3 * c))


def _pack_complex(re, im):
    bits = lambda x: lax.bitcast_convert_type(x.astype(BF16).astype(F32), jnp.uint32)
    return lax.shift_right_logical(bits(re), jnp.uint32(16)) | bits(im)


def _unpack_complex(w):
    re = lax.bitcast_convert_type(lax.shift_left(w, jnp.uint32(16)), F32)
    im = lax.bitcast_convert_type(w & jnp.uint32(0xFFFF0000), F32)
    return jnp.concatenate([re, im], axis=0).astype(BF16)


def _fft_conv_kernel(z_ref, kf_ref, w1_ref, f2_ref, g2i_ref, f1c_ref, tw_ref, twt_ref, o_ref, a_ref, t_ref,
                     *, n1, k1c):
    s = pl.program_id(2)
    n2 = FFT_N2
    nd = n1 // 2
    slab = n1 // k1c

    @pl.when(s == 0)
    def _():
        w1 = w1_ref[...]

        def body(j, carry):
            x = jnp.concatenate([z_ref[0, j], z_ref[1, j]], axis=0)
            a = _dot(w1, x)
            a_ref[pl.ds(pl.multiple_of(j * n1, n1), n1), :] = _pack_complex(a[:n1], a[n1:])
            return carry

        lax.fori_loop(0, n2, body, 0, unroll=8)

    @pl.when((s > 0) & (s <= k1c))
    def _():
        f2r, f2i = f2_ref[0], f2_ref[1]
        g2i = g2i_ref[...]
        base = (s - 1) * slab

        def body(kk, carry):
            k1 = base + kk
            twr = tw_ref[0, pl.ds(k1, 1), :]
            twi = tw_ref[1, pl.ds(k1, 1), :]
            gr = f2r * twr - f2i * twi
            gi = f2r * twi + f2i * twr
            gmat = jnp.concatenate([jnp.concatenate([gr, -gi], axis=1),
                                    jnp.concatenate([gi, gr], axis=1)], axis=0).astype(BF16)
            ak = _unpack_complex(a_ref[pl.ds(k1, n2, stride=n1), :])
            x = _dot(gmat, ak)
            xr, xi = x[:n2], x[n2:]
            kr, ki = kf_ref[0, kk], kf_ref[1, kk]
            pr = xr * kr - xi * ki
            pi = xr * ki + xi * kr
            tt = _dot(g2i, jnp.concatenate([pr, pi], axis=0).astype(BF16))
            t_ref[pl.ds(pl.multiple_of(k1 * n2, n2), n2), :] = _pack_complex(tt[:n2], tt[n2:])
            return carry

        lax.fori_loop(0, slab, body, 0, unroll=4)

    @pl.when(s == k1c + 1)
    def _():
        f1r, f1i = f1c_ref[0], f1c_ref[1]

        def body(j, carry):
            twr = twt_ref[0, pl.ds(j, 1), :]
            twi = twt_ref[1, pl.ds(j, 1), :]
            wr = f1r * twr + f1i * twi
            wi = f1i * twr - f1r * twi
            w3 = jnp.concatenate([jnp.concatenate([wr, -wi], axis=1),
                                  jnp.concatenate([wi, wr], axis=1)], axis=0).astype(BF16)
            tn = _unpack_complex(t_ref[pl.ds(j, n1, stride=n2), :])
            y = _dot(w3, tn)
            o_ref[0, j] = y[:nd]
            o_ref[1, j] = y[nd:]
            return carry

        lax.fori_loop(0, n2, body, 0, unroll=8)


def _fft_consts(n1):
    n2 = FFT_N2
    n = n1 * n2
    nd = n1 // 2
    k1 = np.arange(n1)
    f1 = np.exp(-2j * np.pi * np.outer(k1, np.arange(nd)) / n1)
    w1 = np.block([[f1.real, -f1.imag], [f1.imag, f1.real]])
    f2 = np.exp(-2j * np.pi * np.outer(np.arange(n2), np.arange(n2)) / n2)
    f2c = np.conj(f2)
    g2i = np.block([[f2c.real, -f2c.imag], [f2c.imag, f2c.real]])
    f1c = np.exp(2j * np.pi * np.outer(np.arange(nd), k1) / n1) / n
    tw = np.exp(-2j * np.pi * np.outer(k1, np.arange(n2)) / n)
    cplx = lambda m: jnp.asarray(np.stack([m.real, m.imag]), F32)
    return (jnp.asarray(w1, BF16), cplx(f2), jnp.asarray(g2i, BF16), cplx(f1c), cplx(tw), cplx(tw.T))


def fft_conv(zt, kf):
    bsz, n2, nd, c = zt.shape
    n1 = 2 * nd
    cb = LANES
    k1c = 8 if n1 % 8 == 0 and n1 >= 64 else 2
    slab = n1 // k1c
    consts = _fft_consts(n1)
    full = lambda a: pl.BlockSpec(a.shape, lambda j, p, s: (0,) * a.ndim)
    blk = pl.BlockSpec((2, n2, nd, cb), lambda j, p, s: (p, 0, 0, j))
    return pl.pallas_call(
        functools.partial(_fft_conv_kernel, n1=n1, k1c=k1c),
        grid=(c // cb, bsz // 2, k1c + 2),
        in_specs=[blk,
                  pl.BlockSpec((2, slab, n2, cb), lambda j, p, s: (0, jnp.clip(s - 1, 0, k1c - 1), 0, j))]
                 + [full(a) for a in consts],
        out_specs=blk,
        out_shape=jax.ShapeDtypeStruct(zt.shape, F32),
        scratch_shapes=[pltpu.VMEM((n2 * n1, cb), jnp.uint32), pltpu.VMEM((n1 * n2, cb), jnp.uint32)],
        compiler_params=_cparams(("parallel", "parallel", "arbitrary")),
        name="fft_conv",
    )(zt, kf, *consts)


def _fft_fwd_kernel(z_ref, w1_ref, f2_ref, tw_ref, o_ref, a_ref, *, n1, k1c):
    s = pl.program_id(2)
    n2 = FFT_N2
    slab = n1 // k1c

    @pl.when(s == 0)
    def _():
        w1 = w1_ref[...]

        def body(j, carry):
            a = _dot(w1, z_ref[0, j])
            a_ref[pl.ds(pl.multiple_of(j * n1, n1), n1), :] = _pack_complex(a[:n1], a[n1:])
            return carry

        lax.fori_loop(0, n2, body, 0, unroll=8)

    @pl.when(s > 0)
    def _():
        f2r, f2i = f2_ref[0], f2_ref[1]
        base = (s - 1) * slab

        def body(kk, carry):
            k1 = base + kk
            twr = tw_ref[0, pl.ds(k1, 1), :]
            twi = tw_ref[1, pl.ds(k1, 1), :]
            gr = f2r * twr - f2i * twi
            gi = f2r * twi + f2i * twr
            gmat = jnp.concatenate([jnp.concatenate([gr, -gi], axis=1),
                                    jnp.concatenate([gi, gr], axis=1)], axis=0).astype(BF16)
            x = _dot(gmat, _unpack_complex(a_ref[pl.ds(k1, n2, stride=n1), :]))
            o_ref[0, 0, kk] = x[:n2]
            o_ref[0, 1, kk] = x[n2:]
            return carry

        lax.fori_loop(0, slab, body, 0, unroll=4)


def fft_forward(zt):
    rows, n2, nd, c = zt.shape
    n1 = 2 * nd
    cb = LANES
    k1c = 8 if n1 % 8 == 0 and n1 >= 64 else 2
    slab = n1 // k1c
    w1, f2, _, _, tw, _ = _fft_consts(n1)
    w1 = w1[:, :nd]
    full = lambda a: pl.BlockSpec(a.shape, lambda j, r, s: (0,) * a.ndim)
    return pl.pallas_call(
        functools.partial(_fft_fwd_kernel, n1=n1, k1c=k1c),
        grid=(c // cb, rows, k1c + 1),
        in_specs=[pl.BlockSpec((1, n2, nd, cb), lambda j, r, s: (r, 0, 0, j)), full(w1), full(f2), full(tw)],
        out_specs=pl.BlockSpec((1, 2, slab, n2, cb), lambda j, r, s: (r, 0, jnp.maximum(s - 1, 0), 0, j)),
        out_shape=jax.ShapeDtypeStruct((rows, 2, n1, n2, c), F32),
        scratch_shapes=[pltpu.VMEM((n2 * n1, cb), jnp.uint32)],
        compiler_params=_cparams(("parallel", "parallel", "arbitrary")),
        name="fft_forward",
    )(zt, w1, f2, tw)


def _hy_filter_kernel(feat_ref, fw1_ref, fb1_ref, fw2_ref, fb2_ref, fw3_ref, fb3_ref, fw4_ref, freq_ref,
                      delta_ref, z_ref, ss_ref):
    @pl.when(pl.program_id(1) == 0)
    def _():
        ss_ref[...] = jnp.zeros_like(ss_ref)

    feats = feat_ref[...]
    t = feats[:, 0:1]
    fr = freq_ref[...]
    h = jnp.sin(fr * (_dot_exact(feats, fw1_ref[...]) + fb1_ref[...]))
    h = jnp.sin(fr * (_dot_exact(h, fw2_ref[...]) + fb2_ref[...]))
    h = jnp.sin(fr * (_dot_exact(h, fw3_ref[...]) + fb3_ref[...]))
    lag0 = jnp.where(pl.program_id(0) % 2 == 1, 0.0, 1.0)
    filt = _dot_exact(h, fw4_ref[...]) * (jnp.exp(-t * delta_ref[...]) * jnp.where(t == 0.0, lag0, 1.0))
    z_ref[0] = filt.astype(BF16)
    ss_ref[0] += jnp.sum(filt * filt, axis=0, keepdims=True)


def hyena_filter(t, fw1, fb1, fw2, fb2, fw3, fb3, fw4, freq, width):
    nd = t // FFT_N2
    pos = (jnp.arange(FFT_N2, dtype=F32)[:, None] + FFT_N2 * jnp.arange(nd, dtype=F32)[None, :]).reshape(-1, 1)
    tt = pos / max(t - 1, 1)
    ang = 2 * math.pi * pos / t
    nb = (HY_EMB - 1) // 2
    bands = jnp.linspace(1e-4, nb - 1, nb, dtype=F32)[None]
    feats = jnp.concatenate([tt, jnp.cos(bands * ang), -jnp.sin(bands * ang),
                             jnp.zeros((t, LANES - HY_EMB), F32)], axis=-1)
    deltas = jnp.abs(jnp.linspace(HY_MIN_DECAY, HY_MAX_DECAY, width, dtype=F32)).reshape(1, width)
    rows = 2 * HY_ORDER
    tm = min(512, t)
    hid = fw1.shape[1]
    full = lambda a: pl.BlockSpec(a.shape, lambda r, i: (0,) * a.ndim)
    fw1p = jnp.concatenate([fw1, jnp.zeros((LANES - HY_EMB, hid), F32)], axis=0)
    vecs = [fw1p, fb1.reshape(1, hid), fw2, fb2.reshape(1, hid), fw3, fb3.reshape(1, hid)]
    return pl.pallas_call(
        _hy_filter_kernel,
        grid=(rows, t // tm),
        in_specs=[pl.BlockSpec((tm, LANES), lambda r, i: (i, 0))] + [full(a) for a in vecs]
                 + [pl.BlockSpec((hid, width), lambda r, i: (0, r)), full(freq.reshape(1, hid)), full(deltas)],
        out_specs=[pl.BlockSpec((1, tm, width), lambda r, i: (r, i, 0)),
                   pl.BlockSpec((1, 1, width), lambda r, i: (r, 0, 0))],
        out_shape=[jax.ShapeDtypeStruct((rows, t, width), BF16), jax.ShapeDtypeStruct((rows, 1, width), F32)],
        compiler_params=_cparams(("parallel", "arbitrary")),
        name="hyena_filter",
    )(feats, *vecs, fw4, freq.reshape(1, hid), deltas)


def hyena_filter_spectrum_fft(t, fw1, fb1, fw2, fb2, fw3, fb3, fw4, freq, width):
    taps, ss = hyena_filter(t, fw1, fb1, fw2, fb2, fw3, fb3, fw4, freq, width)
    nd = t // FFT_N2
    spec = fft_forward(taps.reshape(2 * HY_ORDER, FFT_N2, nd, width))
    spec = spec.reshape(HY_ORDER, 2, 2, 2 * nd, FFT_N2, width)
    scale = lax.rsqrt(ss.reshape(HY_ORDER, 2, width).sum(axis=1) + 1e-6)[:, None, None, :]
    return jnp.stack([(spec[:, 0, 0] + spec[:, 1, 0]) * scale, (spec[:, 0, 1] - spec[:, 1, 1]) * scale], axis=1)


def _dft_conv_kernel(z_ref, kf_ref, fw_ref, iv_ref, o_ref, *, t):
    x = jnp.concatenate([z_ref[0], z_ref[1]], axis=0)
    spec = _dot(fw_ref[...], x)
    xr, xi = spec[:2 * t], spec[2 * t:]
    kr, ki = kf_ref[0], kf_ref[1]
    pr = xr * kr - xi * ki
    pi = xr * ki + xi * kr
    y = _dot(iv_ref[...], jnp.concatenate([pr, pi], axis=0).astype(BF16))
    o_ref[0] = y[:t]
    o_ref[1] = y[t:]


def dft_conv(zb, kf):
    bsz, t, c = zb.shape
    n = 2 * t
    f = np.exp(-2j * np.pi * np.outer(np.arange(n), np.arange(t)) / n)
    fw = np.block([[f.real, -f.imag], [f.imag, f.real]])
    fi = np.exp(2j * np.pi * np.outer(np.arange(t), np.arange(n)) / n) / n
    iv = np.block([[fi.real, -fi.imag], [fi.imag, fi.real]])
    fw, iv = jnp.asarray(fw, BF16), jnp.asarray(iv, BF16)
    cb = LANES
    blk = pl.BlockSpec((2, t, cb), lambda j, p: (p, 0, j))
    return pl.pallas_call(
        functools.partial(_dft_conv_kernel, t=t),
        grid=(c // cb, bsz // 2),
        in_specs=[blk, pl.BlockSpec((2, n, cb), lambda j, p: (0, 0, j)),
                  pl.BlockSpec(fw.shape, lambda j, p: (0, 0)), pl.BlockSpec(iv.shape, lambda j, p: (0, 0))],
        out_specs=blk,
        out_shape=jax.ShapeDtypeStruct(zb.shape, F32),
        compiler_params=_cparams(("parallel", "parallel")),
        name="dft_conv",
    )(zb, kf, fw, iv)


def _hy_gate_kernel(conv_ref, z_ref, gate_ref, bias_ref, o_ref, ob_ref):
    y = gate_ref[0] * (conv_ref[0] + bias_ref[...] * z_ref[0])
    o_ref[0] = y
    ob_ref[0] = y.astype(BF16)


def hyena_gate(conv, z, gate, bias):
    bsz, t, c = z.shape
    tm = min(512, t)
    tok = pl.BlockSpec((1, tm, c), lambda b, i: (b, i, 0))
    return pl.pallas_call(
        _hy_gate_kernel,
        grid=(bsz, t // tm),
        in_specs=[tok, tok, tok, pl.BlockSpec((1, c), lambda b, i: (0, 0))],
        out_specs=[tok, tok],
        out_shape=[jax.ShapeDtypeStruct(z.shape, F32), jax.ShapeDtypeStruct(z.shape, BF16)],
        compiler_params=_cparams(("parallel", "parallel")),
        name="hyena_gate",
    )(conv, z, gate, bias.reshape(1, c))


def _hy_out_kernel(x_ref, conv_ref, z_ref, gate_ref, bias_ref, w_ref, m5_ref, o_ref):
    y = gate_ref[0] * (conv_ref[0] + bias_ref[...] * z_ref[0])
    o_ref[0] = x_ref[0] + m5_ref[0] * _dot(y.astype(BF16), w_ref[...])


def hyena_out(x, conv, z, gate, bias, w, m5):
    bsz, t, d = x.shape
    c = z.shape[2]
    tm = min(512, t)
    tokd = pl.BlockSpec((1, tm, d), lambda b, i: (b, i, 0))
    tokc = pl.BlockSpec((1, tm, c), lambda b, i: (b, i, 0))
    return pl.pallas_call(
        _hy_out_kernel,
        grid=(bsz, t // tm),
        in_specs=[tokd, tokc, tokc, tokc, pl.BlockSpec((1, c), lambda b, i: (0, 0)),
                  pl.BlockSpec(w.shape, lambda b, i: (0, 0)), pl.BlockSpec((1, 1, d), lambda b, i: (b, 0, 0))],
        out_specs=tokd,
        out_shape=jax.ShapeDtypeStruct(x.shape, F32),
        compiler_params=_cparams(("parallel", "parallel")),
        name="hyena_out",
    )(x, conv, z, gate, bias.reshape(1, c), w, m5)


def _hyena_filter_spectrum(t, fw1, fb1, fw2, fb2, fw3, fb3, fw4, freq, width):
    pos = jnp.arange(t, dtype=F32)[:, None]
    tt = pos / max(t - 1, 1)
    ang = 2 * math.pi * pos / t
    nb = (HY_EMB - 1) // 2
    bands = jnp.linspace(1e-4, nb - 1, nb, dtype=F32)[None]
    feats = jnp.concatenate([tt, jnp.cos(bands * ang), -jnp.sin(bands * ang)], axis=-1)
    hdn = jnp.sin(freq * (feats @ fw1 + fb1))
    hdn = jnp.sin(freq * (hdn @ fw2 + fb2))
    hdn = jnp.sin(freq * (hdn @ fw3 + fb3))
    filt = (hdn @ fw4).reshape(t, HY_ORDER, 2, width)
    deltas = jnp.abs(jnp.linspace(HY_MIN_DECAY, HY_MAX_DECAY, width, dtype=F32))
    filt = filt * jnp.exp(-tt[:, :, None, None] * deltas)
    fwd, bwd = filt[:, :, 0], filt[:, :, 1]
    kern = jnp.concatenate([fwd, jnp.zeros_like(fwd[:1]), bwd[:0:-1]], axis=0)
    kern = kern * lax.rsqrt(jnp.sum(kern * kern, axis=0, keepdims=True) + 1e-6)
    spec = jnp.fft.fft(kern, axis=0)
    return jnp.stack([jnp.real(spec), jnp.imag(spec)], axis=1).transpose(2, 1, 0, 3).astype(F32)


def _block_diag_pair(m):
    z = jnp.zeros_like(m[0])
    return jnp.concatenate([jnp.concatenate([m[0], z], axis=1), jnp.concatenate([z, m[1]], axis=1)], axis=0)


def _even_mixer(lat, cx, ml, mc, gain, prm, ctx_out):
    (w_in, mu, w0, w_up, a0, a_up, g_up, k_k, k_a, r_k, gn_g, gn_b,
     lam_re, lam_im, log_dt, b_re, b_im, c_re, c_im, d_skip, w_glu, b_glu, w_out) = prm
    bsz, t_lat, d = lat.shape
    t_ctx = cx.shape[1]
    width = k_k.shape[0]
    heads = width // HEAD_DIM
    n_cols = mu.shape[0]
    s5w = d_skip.shape[0]
    ttot = t_ctx + t_lat
    nch = bsz * heads
    tm = t_ctx
    assert tm % GRID_W == 0 and t_lat % tm == 0 and 4 * nch == LANES and 2 * bsz == SUBLANES

    p_all = mod_matmul_stream(cx, lat, gain, mc, ml, w_in.astype(BF16), tm)

    head_of = np.arange(width) // HEAD_DIM
    ones = jnp.asarray(head_of[:, None] == head_of[None, :], F32)
    feat_consts = (mu.reshape(1, -1), _block_diag_pair(w_up).astype(BF16), w0.reshape(1, -1),
                   _block_diag_pair(a_up).astype(BF16), a0.reshape(1, -1), g_up.astype(BF16),
                   k_k.reshape(1, -1), k_a.reshape(1, -1), r_k.reshape(1, -1), ones)
    kk, v, g, bonus, r, k, w2, a2 = rwkv_features(p_all, n_cols, width, feat_consts, tm)

    def key_major(x):
        x = x.reshape(bsz, ttot, heads, 2, HALF_HEAD).transpose(1, 4, 3, 0, 2)[:, :, None]
        return jnp.broadcast_to(x, (ttot, HALF_HEAD, 2, 2, bsz, heads)).reshape(ttot, HALF_HEAD, LANES)

    def key_major2(x):
        x = x.reshape(2, bsz, ttot, heads, 2, HALF_HEAD)
        return x.transpose(2, 5, 0, 4, 1, 3).reshape(ttot, HALF_HEAD, LANES)

    def value_major(x):
        x = x.reshape(bsz, ttot, heads, HEAD_DIM).transpose(1, 3, 0, 2).reshape(ttot, HEAD_DIM, nch)
        return jnp.concatenate([x, x], axis=-1)

    ka_t = k_a.reshape(heads, 2, HALF_HEAD).transpose(2, 1, 0)[:, None, :, None, :]
    ka_t = jnp.broadcast_to(ka_t, (HALF_HEAD, 2, 2, bsz, heads)).reshape(HALF_HEAD, LANES)
    o_f, o_b = rwkv_scan(key_major(r), key_major(kk), key_major(k), key_major2(w2), key_major2(a2),
                         value_major(v), ka_t, t_ctx)

    def token_major(o):
        o = o[..., :nch].reshape(ttot, HEAD_DIM, bsz, heads)
        return o.transpose(2, 0, 3, 1).reshape(bsz, ttot, width)

    o_f, o_b = token_major(o_f), token_major(o_b)

    ng = lam_re.shape[1]
    gps = ng // S5_SUPER
    lam = lax.complex(lam_re, lam_im)
    dt = jnp.exp(log_dt)[..., None]
    a_bar = jnp.exp(lam * dt)
    b_bar = ((a_bar - 1) / lam)[..., None] * lax.complex(b_re, b_im)
    eye = jnp.eye(gps, dtype=F32)

    def b_mat(x):
        x = x.reshape(2, S5_SUPER, gps, S5_STATE, S5_GROUP)
        m = jnp.einsum('dsgph,gk->sdghkp', x, eye)
        return m.reshape(S5_SUPER, 2 * gps * S5_GROUP, gps * S5_STATE)

    def c_mat(x):
        x = x.reshape(2, S5_SUPER, gps, S5_GROUP, S5_STATE)
        m = jnp.einsum('dsghp,gk->skpdgh', x, eye)
        return m.reshape(S5_SUPER, gps * S5_STATE, 2 * gps * S5_GROUP)

    bm = jnp.concatenate([b_mat(jnp.real(b_bar)), b_mat(jnp.imag(b_bar))], axis=2).astype(BF16)
    cm = jnp.concatenate([c_mat(c_re), -c_mat(c_im)], axis=1).astype(BF16)

    def a_rows(x):
        x = x.reshape(2, 1, S5_SUPER, gps * S5_STATE)
        return jnp.broadcast_to(x, (2, bsz, S5_SUPER, gps * S5_STATE)).reshape(2 * bsz, S5_SUPER, -1)

    a_arr = jnp.concatenate([a_rows(jnp.real(a_bar)), a_rows(jnp.imag(a_bar))], axis=2)
    a_arr = a_arr.reshape(1, 2 * bsz, -1)

    u_all = p_all[..., n_cols:]
    u_tm = u_all.transpose(1, 0, 2)
    y_f, y_b = s5_scan(jnp.concatenate([u_tm, u_tm], axis=1), bm, cm, a_arr, t_ctx)
    ys = (y_f[:, :bsz] + y_b[:, bsz:]).transpose(1, 0, 2)

    wo = w_out.astype(BF16)
    out_consts = (ones, gn_g.reshape(1, -1), gn_b.reshape(1, -1), d_skip.reshape(1, -1),
                  w_glu.astype(BF16), b_glu.reshape(1, -1), wo[:width], wo[width:])
    lat = even_out(lat, 1, o_f, o_b, bonus, g, ys, u_all, ml[5], out_consts, tm)
    if ctx_out:
        cx = even_out(cx, 0, o_f, o_b, bonus, g, ys, u_all, mc[5], out_consts, tm)
    return lat, cx


def _hyena_mixer(x, gain, m, prm):
    (w_in, conv_w, conv_b, fw1, fb1, fw2, fb2, fw3, fb3, fw4, freq, bias_d, w_out) = prm
    bsz, t, d = x.shape
    c = w_out.shape[0]
    use_fft = t % (FFT_N2 * 2) == 0 and t >= 4 * FFT_N2
    if use_fft:
        kf = hyena_filter_spectrum_fft(t, fw1, fb1, fw2, fb2, fw3, fb3, fw4, freq, c)
    else:
        kf = _hyena_filter_spectrum(t, fw1, fb1, fw2, fb2, fw3, fb3, fw4, freq, c)
    z, zb, g1, g2 = hyena_in(x, gain, m[3], m[4], w_in.astype(BF16), conv_w, conv_b)
    gates = (g1, g2)
    for n in range(HY_ORDER):
        if use_fft:
            nd = t // FFT_N2
            zt = zb.reshape(bsz, nd, FFT_N2, c).transpose(0, 2, 1, 3)
            conv = fft_conv(zt, kf[n]).transpose(0, 2, 1, 3).reshape(bsz, t, c)
        else:
            conv = dft_conv(zb, kf[n])
        if n < HY_ORDER - 1:
            z, zb = hyena_gate(conv, z, gates[n], bias_d[n])
        else:
            return hyena_out(x, conv, z, gates[n], bias_d[n], w_out.astype(BF16), m[5])


def kernel(x, c, ctx, c_ctx, norm_g, ada_w, ada_b, ffn_wg, ffn_wu, ffn_wd, final_g, ev_w_in, ev_mu, ev_w0, ev_w_up, ev_a0, ev_a_up, ev_g_up, ev_k_k, ev_k_a, ev_r_k, ev_gn_g, ev_gn_b, ev_lam_re, ev_lam_im, ev_log_dt, ev_b_re, ev_b_im, ev_c_re, ev_c_im, ev_d, ev_w_glu, ev_b_glu, ev_w_out, od_w_in, od_conv_w, od_conv_b, od_fw1, od_fb1, od_fw2, od_fb2, od_fw3, od_fb3, od_fw4, od_freq, od_bias, od_w_out):
    depth = norm_g.shape[0]
    bsz, _, d = x.shape
    n_even = (depth + 1) // 2
    last_ctx = 2 * (n_even - 1)

    cond8 = jnp.concatenate([c, c_ctx[None], jnp.zeros((8 - bsz - 1, d), F32)], axis=0)
    mods = ada_mods_all(cond8, ada_w, ada_b)

    wg, wu, wd = ffn_wg.astype(BF16), ffn_wu.astype(BF16), ffn_wd.astype(BF16)
    lat, cx = x, ctx
    for l in range(depth):
        run_ctx = l <= last_ctx
        ctx_out = l < last_ctx
        i = l // 2
        ml = [mods[l, :bsz, None, k * d:(k + 1) * d] for k in range(N_MOD)]
        mc = [jnp.broadcast_to(mods[l, bsz:bsz + 1, None, k * d:(k + 1) * d], (bsz, 1, d))
              for k in range(N_MOD)]
        lat = ffn_half(lat, norm_g[l, 0], ml[0], ml[1], ml[2], wg[l, 0], wu[l, 0], wd[l, 0])
        if run_ctx:
            cx = ffn_half(cx, norm_g[l, 0], mc[0], mc[1], mc[2], wg[l, 0], wu[l, 0], wd[l, 0])
        if l % 2 == 0:
            prm = (ev_w_in[i], ev_mu[i], ev_w0[i], ev_w_up[i], ev_a0[i], ev_a_up[i], ev_g_up[i],
                   ev_k_k[i], ev_k_a[i], ev_r_k[i], ev_gn_g[i], ev_gn_b[i],
                   ev_lam_re[i], ev_lam_im[i], ev_log_dt[i], ev_b_re[i], ev_b_im[i], ev_c_re[i], ev_c_im[i],
                   ev_d[i], ev_w_glu[i], ev_b_glu[i], ev_w_out[i])
            lat, cx = _even_mixer(lat, cx, ml, mc, norm_g[l, 1], prm, ctx_out)
        else:
            prm = (od_w_in[i], od_conv_w[i], od_conv_b[i], od_fw1[i], od_fb1[i], od_fw2[i], od_fb2[i],
                   od_fw3[i], od_fb3[i], od_fw4[i], od_freq[i], od_bias[i], od_w_out[i])
            lat = _hyena_mixer(lat, norm_g[l, 1], ml, prm)
            if ctx_out:
                cx = _hyena_mixer(cx, norm_g[l, 1], mc, prm)
        fin = final_g if l == depth - 1 else None
        lat = ffn_half(lat, norm_g[l, 2], ml[6], ml[7], ml[8], wg[l, 1], wu[l, 1], wd[l, 1], fin)
        if ctx_out:
            cx = ffn_half(cx, norm_g[l, 2], mc[6], mc[7], mc[8], wg[l, 1], wu[l, 1], wd[l, 1])
    return lat
```

```python
import functools
import math

import numpy as np
import jax
import jax.numpy as jnp
from jax import lax
from jax.experimental import pallas as pl
from jax.experimental.pallas import tpu as pltpu

F32 = jnp.float32
BF16 = jnp.bfloat16
HIGHEST = lax.Precision.HIGHEST

N_MOD = 9
NORM_EPS = 1e-6
GN_EPS = 64e-5
GRID_W = 64
HEAD_DIM = 64
HALF_HEAD = HEAD_DIM // 2
S5_GROUP = 16
S5_STATE = 64
S5_SUPER = 4
HY_ORDER = 2
HY_EMB = 33
HY_MIN_DECAY = math.log(1e-2) / 1.5
HY_MAX_DECAY = math.log(1e-2) / 0.3
LANES = 128
SUBLANES = 8
MXU_DIM = 256
FFT_N2 = 128
VMEM_LIMIT = 56 * 1024 * 1024


def _cparams(sem, vmem=VMEM_LIMIT):
    return pltpu.CompilerParams(dimension_semantics=sem, vmem_limit_bytes=vmem)


def _dot(a, b):
    return jnp.dot(a, b, preferred_element_type=F32)


def _dot_exact(a, b):
    return jnp.dot(a, b, preferred_element_type=F32, precision=HIGHEST)


def _rms_mod(x, gain, shift, scale):
    ms = jnp.mean(x * x, axis=-1, keepdims=True)
    return x * lax.rsqrt(ms + NORM_EPS) * gain * (1.0 + scale) + shift


def _sigmoid(x):
    return 1.0 / (1.0 + jnp.exp(-x))


def _silu(x):
    return x * _sigmoid(x)


def _ada_kernel(c_ref, w_ref, b_ref, o_ref):
    s = _silu(c_ref[...])
    o_ref[0] = _dot(s.astype(BF16), w_ref[0].astype(BF16)) + b_ref[0]


def ada_mods_all(cond8, ada_w, ada_b):
    depth, d, n = ada_w.shape
    tn = n // 8
    return pl.pallas_call(
        _ada_kernel,
        grid=(depth, n // tn),
        in_specs=[pl.BlockSpec((8, d), lambda l, j: (0, 0)),
                  pl.BlockSpec((1, d, tn), lambda l, j: (l, 0, j)),
                  pl.BlockSpec((1, 1, tn), lambda l, j: (l, 0, j))],
        out_specs=pl.BlockSpec((1, 8, tn), lambda l, j: (l, 0, j)),
        out_shape=jax.ShapeDtypeStruct((depth, 8, n), F32),
        compiler_params=_cparams(("parallel", "parallel")),
        name="ada_mods",
    )(cond8, ada_w, ada_b.reshape(depth, 1, n))


def _ffn_kernel(x_ref, gain_ref, shift_ref, scale_ref, gate_ref, wg_ref, wu_ref, wd_ref, fg_ref,
                o_ref, *, final_norm, chunks):
    x = x_ref[0]
    h = _rms_mod(x, gain_ref[...], shift_ref[0], scale_ref[0]).astype(BF16)
    acc = None
    for lo, hi in chunks:
        g = _dot(h, wg_ref[:, lo:hi])
        u = _dot(h, wu_ref[:, lo:hi])
        part = _dot((_silu(g) * u).astype(BF16), wd_ref[lo:hi, :])
        acc = part if acc is None else acc + part
    y = x + 0.5 * gate_ref[0] * acc
    if final_norm:
        ms = jnp.mean(y * y, axis=-1, keepdims=True)
        y = y * lax.rsqrt(ms + NORM_EPS) * fg_ref[...]
    o_ref[0] = y


def ffn_half(x, gain, shift, scale, gate, wg, wu, wd, final_g=None):
    bsz, t, d = x.shape
    ff = wg.shape[1]
    tm = min(512, t)
    step = 4 * MXU_DIM
    chunks = tuple((lo, min(lo + step, ff)) for lo in range(0, ff, step))
    fg = jnp.ones((1, d), F32) if final_g is None else final_g.reshape(1, d)
    vec = pl.BlockSpec((1, 1, d), lambda b, i: (b, 0, 0))
    resident = lambda a: pl.BlockSpec(a.shape, lambda b, i: (0, 0), pipeline_mode=pl.Buffered(1))
    return pl.pallas_call(
        functools.partial(_ffn_kernel, final_norm=final_g is not None, chunks=chunks),
        grid=(bsz, t // tm),
        in_specs=[pl.BlockSpec((1, tm, d), lambda b, i: (b, i, 0)),
                  pl.BlockSpec((1, d), lambda b, i: (0, 0)),
                  vec, vec, vec, resident(wg), resident(wu), resident(wd),
                  pl.BlockSpec((1, d), lambda b, i: (0, 0))],
        out_specs=pl.BlockSpec((1, tm, d), lambda b, i: (b, i, 0)),
        out_shape=jax.ShapeDtypeStruct((bsz, t, d), F32),
        compiler_params=_cparams(("parallel", "parallel")),
        name="ffn_half",
    )(x, gain.reshape(1, d), shift, scale, gate, wg, wu, wd, fg)


def _modmm_kernel(cx_ref, lat_ref, gain_ref, shc_ref, scc_ref, shl_ref, scl_ref, w_ref, o_ref, *, nc):
    is_ctx = pl.program_id(1) < nc
    x = jnp.where(is_ctx, cx_ref[0], lat_ref[0])
    shift = jnp.where(is_ctx, shc_ref[0], shl_ref[0])
    scale = jnp.where(is_ctx, scc_ref[0], scl_ref[0])
    h = _rms_mod(x, gain_ref[...], shift, scale)
    o_ref[0] = _dot(h.astype(BF16), w_ref[...])


def mod_matmul_stream(cx, lat, gain, mc, ml, w, tm):
    bsz, t_ctx, d = cx.shape
    t_lat = lat.shape[1]
    n = w.shape[1]
    nc = t_ctx // tm
    nt = nc + t_lat // tm
    vec = pl.BlockSpec((1, 1, d), lambda b, i: (b, 0, 0))
    return pl.pallas_call(
        functools.partial(_modmm_kernel, nc=nc),
        grid=(bsz, nt),
        in_specs=[pl.BlockSpec((1, tm, d), lambda b, i: (b, jnp.minimum(i, nc - 1), 0)),
                  pl.BlockSpec((1, tm, d), lambda b, i: (b, jnp.maximum(i - nc, 0), 0)),
                  pl.BlockSpec((1, d), lambda b, i: (0, 0)),
                  vec, vec, vec, vec,
                  pl.BlockSpec((d, n), lambda b, i: (0, 0))],
        out_specs=pl.BlockSpec((1, tm, n), lambda b, i: (b, i, 0)),
        out_shape=jax.ShapeDtypeStruct((bsz, t_ctx + t_lat, n), F32),
        compiler_params=_cparams(("parallel", "parallel")),
        name="mod_matmul",
    )(cx, lat, gain.reshape(1, d), mc[3], mc[4], ml[3], ml[4], w)


def _rwkv_feat_kernel(p_ref, pu_ref, pd_ref, mu_ref, wup_ref, w0_ref, aup_ref, a0_ref, gup_ref,
                      kk_s_ref, ka_ref, rk_ref, ones_ref,
                      kk_o, v_o, g_o, bonus_o, r_o, k_o, w_o, a_o, *, nt, width):
    i = pl.program_id(1)
    is_ctx = i == 0
    p = p_ref[0]
    tm = p.shape[0]
    row = lax.broadcasted_iota(jnp.int32, p.shape, 0)
    lane = lax.broadcasted_iota(jnp.int32, p.shape, 1) % 4
    prev = pltpu.roll(p, 1, 0)
    nxt = pltpu.roll(p, tm - 1, 0)
    col = jnp.where(is_ctx, row, row % GRID_W)
    last = jnp.where(is_ctx, tm - 1, GRID_W - 1)
    left = jnp.where(col == 0, 0.0, prev)
    right = jnp.where(col == last, 0.0, nxt)
    up_halo = jnp.where(i > 1, pu_ref[0], 0.0)
    dn_halo = jnp.where(i < nt - 1, pd_ref[0], 0.0)
    up = jnp.where(is_ctx, left, jnp.concatenate([up_halo, p[:tm - GRID_W]], axis=0))
    down = jnp.where(is_ctx, right, jnp.concatenate([p[GRID_W:], dn_halo], axis=0))
    shifted = jnp.where(lane == 0, left, jnp.where(lane == 1, right, jnp.where(lane == 2, up, down)))
    q = p + mu_ref[...] * (shifted - p)

    w = width
    r, k, v = q[:, :w], q[:, w:2 * w], q[:, 2 * w:3 * w]
    wd = q[:, 3 * w:3 * w + LANES]
    ad = q[:, 3 * w + LANES:3 * w + 2 * LANES]
    gd = q[:, 3 * w + 2 * LANES:3 * w + 3 * LANES]

    zlin = w0_ref[...] + _dot(jnp.tanh(wd).astype(BF16), wup_ref[...])
    neg = -zlin
    softplus = jnp.maximum(neg, 0.0) + jnp.log(1.0 + jnp.exp(-jnp.abs(neg)))
    decay = jnp.exp(-jnp.exp(-softplus - 0.5))
    a = _sigmoid(a0_ref[...] + _dot(ad.astype(BF16), aup_ref[...]))
    g_o[0] = _dot(_sigmoid(gd).astype(BF16), gup_ref[...])

    ones = ones_ref[...]
    kk = k * kk_s_ref[...]
    ss = _dot_exact(kk * kk, ones)
    kk_o[0] = kk / jnp.maximum(jnp.sqrt(ss), 1e-12)
    v_o[0] = v
    r_o[0] = r
    k_o[0] = k
    ksum = jnp.zeros_like(k)
    for d in range(2):
        a_d = a[:, d * w:(d + 1) * w]
        ksum = ksum + k * (1.0 + (a_d - 1.0) * ka_ref[...])
        w_o[d, 0] = decay[:, d * w:(d + 1) * w]
        a_o[d, 0] = a_d
    bonus_o[0] = _dot_exact(r * ksum * rk_ref[...], ones) * v


def rwkv_features(p, n_cols, width, consts, tm):
    bsz, t, _ = p.shape
    nt = t // tm
    hb = tm // GRID_W
    nhb = t // GRID_W
    w = width
    full = lambda a: pl.BlockSpec(a.shape, lambda b, i: (0,) * a.ndim)
    tok = pl.BlockSpec((1, tm, w), lambda b, i: (b, i, 0))
    tok2 = pl.BlockSpec((2, 1, tm, w), lambda b, i: (0, b, i, 0))
    sds = jax.ShapeDtypeStruct((bsz, t, w), F32)
    sds2 = jax.ShapeDtypeStruct((2, bsz, t, w), F32)
    return pl.pallas_call(
        functools.partial(_rwkv_feat_kernel, nt=nt, width=w),
        grid=(bsz, nt),
        in_specs=[pl.BlockSpec((1, tm, n_cols), lambda b, i: (b, i, 0)),
                  pl.BlockSpec((1, GRID_W, n_cols), lambda b, i: (b, jnp.maximum(i * hb - 1, 0), 0)),
                  pl.BlockSpec((1, GRID_W, n_cols), lambda b, i: (b, jnp.minimum((i + 1) * hb, nhb - 1), 0))]
                 + [full(a) for a in consts],
        out_specs=[tok] * 6 + [tok2, tok2],
        out_shape=[sds] * 6 + [sds2, sds2],
        compiler_params=_cparams(("parallel", "parallel")),
        name="rwkv_features",
    )(p, p, p, *consts)


def _rwkv_scan_kernel(rf, rb, kkf, kkb, kf, kb, wf, wb, af, ab, vf, vb, ka_ref, of_ref, ob_ref,
                      s_ref, vec_ref, v_ref, c_ref, *, tc):
    @pl.when(pl.program_id(0) == 0)
    def _():
        s_ref[...] = jnp.zeros_like(s_ref)

    nj, ni = s_ref.shape[0], s_ref.shape[1]
    quarter = LANES // 4

    def fold(x):
        hi = (lax.broadcasted_iota(jnp.int32, x.shape, 1) % (2 * quarter)) >= quarter
        return x + jnp.where(hi, pltpu.roll(x, quarter, 1), pltpu.roll(x, 3 * quarter, 1))

    def hsum(x):
        return jnp.sum(x, axis=0, keepdims=True)

    def jsum(x):
        return fold(jnp.broadcast_to(hsum(x), (SUBLANES, LANES)))[0:1]

    ka = ka_ref[...]
    fwd_lane = lax.broadcasted_iota(jnp.int32, (nj, LANES), 1) < LANES // 2
    kk_next = jnp.zeros((nj, LANES), F32)
    wr_next = jnp.zeros((nj, LANES), F32)
    for t in reversed(range(tc)):
        merged = lambda f, b: jnp.where(fwd_lane, f[t], b[tc - 1 - t])
        r, kk, k = merged(rf, rb), merged(kkf, kkb), merged(kf, kb)
        w, a = merged(wf, wb), merged(af, ab)
        kka = kk * a
        kd = k * (1.0 + (a - 1.0) * ka)
        wr = w * r
        vec_ref[0, t] = w
        vec_ref[1, t] = kka
        vec_ref[2, t] = kd
        vec_ref[3, t] = w * kk_next
        vec_ref[4, t] = w * wr_next
        c_ref[t, 0:1, :] = jsum(kka * r)
        c_ref[t, 1:2, :] = jsum(kd * r)
        c_ref[t, 2:3, :] = hsum(kka * kk_next)
        c_ref[t, 3:4, :] = hsum(kd * kk_next)
        c_ref[t, 4:5, :] = hsum(kka * wr_next)
        c_ref[t, 5:6, :] = hsum(kd * wr_next)
        v_ref[t] = jnp.concatenate([vf[t], vb[tc - 1 - t]], axis=-1)
        kk_next, wr_next = kk, wr

    def state_sums(y1, y2):
        s1 = jnp.zeros((ni, LANES), F32)
        s2 = jnp.zeros((ni, LANES), F32)
        for j in range(nj):
            sj = s_ref[j]
            s1 = s1 + sj * y1(j)
            s2 = s2 + sj * y2(j)
        return s1, s2

    def step(t, carry):
        row = lambda n: (lambda j: vec_ref[n, t, j:j + 1, :])
        a1, a2 = state_sums(row(3), row(4))
        sa, o1 = fold(carry[0]), fold(carry[1])
        v = v_ref[t]
        out = o1 - sa * c_ref[t, 0:1, :] + v * c_ref[t, 1:2, :]
        of_ref[t] = out
        ob_ref[tc - 1 - t] = out
        nxt = (a1 - sa * c_ref[t, 2:3, :] + v * c_ref[t, 3:4, :],
               a2 - sa * c_ref[t, 4:5, :] + v * c_ref[t, 5:6, :])
        for j in range(nj):
            s_ref[j] = s_ref[j] * row(0)(j) - sa * row(1)(j) + v * row(2)(j)
        return nxt

    first = state_sums(lambda j: kk_next[j:j + 1, :], lambda j: wr_next[j:j + 1, :])
    lax.fori_loop(0, tc, step, first)


def rwkv_scan(r, kk, k, w, a, v, ka, t_ctx):
    ttot, nj, _ = r.shape
    ni, nc = v.shape[1], v.shape[2]
    tc = 32
    ncb, ntb = t_ctx // tc, ttot // tc
    fwd = lambda g: (g, 0, 0)
    bwd = lambda g: (jnp.where(g < ncb, ncb - 1 - g, ntb + ncb - 1 - g), 0, 0)
    jf, jb = pl.BlockSpec((tc, nj, LANES), fwd), pl.BlockSpec((tc, nj, LANES), bwd)
    vf, vb = pl.BlockSpec((tc, ni, nc), fwd), pl.BlockSpec((tc, ni, nc), bwd)
    of, ob = pl.BlockSpec((tc, ni, LANES), fwd), pl.BlockSpec((tc, ni, LANES), bwd)
    osd = jax.ShapeDtypeStruct((ttot, ni, LANES), F32)
    return pl.pallas_call(
        functools.partial(_rwkv_scan_kernel, tc=tc),
        grid=(ntb,),
        in_specs=[jf, jb] * 5 + [vf, vb, pl.BlockSpec(ka.shape, lambda g: (0, 0))],
        out_specs=[of, ob],
        out_shape=[osd, osd],
        scratch_shapes=[pltpu.VMEM((nj, ni, LANES), F32), pltpu.VMEM((5, tc, nj, LANES), F32),
                        pltpu.VMEM((tc, ni, LANES), F32), pltpu.VMEM((tc, SUBLANES, LANES), F32)],
        compiler_params=_cparams(("arbitrary",)),
        name="rwkv_scan",
    )(r, r, kk, kk, k, k, w, w, a, a, v, v, ka)


def _s5_kernel(uf_ref, ub_ref, bm_ref, cm_ref, a_ref, yf_ref, yb_ref, h_ref, bu_ref, ubr_ref, *, tt):
    @pl.when(pl.program_id(0) == 0)
    def _():
        h_ref[...] = jnp.zeros_like(h_ref)

    nsb = bm_ref.shape[0]
    kin = bm_ref.shape[1] // 2
    sw = bm_ref.shape[2]
    hw = sw // 2
    for t in range(tt):
        ubr_ref[t] = ub_ref[tt - 1 - t]
    fwd_row3 = lax.broadcasted_iota(jnp.int32, uf_ref.shape, 1) < SUBLANES // 2
    uf = jnp.where(fwd_row3, uf_ref[...], 0.0)
    ub = jnp.where(fwd_row3, 0.0, ubr_ref[...])
    for sb in range(nsb):
        lhs = jnp.concatenate([uf[:, :, sb * kin:(sb + 1) * kin], ub[:, :, sb * kin:(sb + 1) * kin]], axis=-1)
        lhs = lhs.reshape(tt * SUBLANES, 2 * kin).astype(BF16)
        bu_ref[:, sb * sw:(sb + 1) * sw] = _dot(lhs, bm_ref[sb])

    def step(t, hs):
        rows = pl.ds(pl.multiple_of(t * SUBLANES, SUBLANES), SUBLANES)
        out = []
        for sb in range(nsb):
            hr, hi = hs[2 * sb], hs[2 * sb + 1]
            lo = sb * sw
            ar = a_ref[0, :, lo:lo + hw]
            ai = a_ref[0, :, lo + hw:lo + sw]
            nr = ar * hr - ai * hi + bu_ref[rows, lo:lo + hw]
            ni = ar * hi + ai * hr + bu_ref[rows, lo + hw:lo + sw]
            bu_ref[rows, lo:lo + hw] = nr
            bu_ref[rows, lo + hw:lo + sw] = ni
            out += [nr, ni]
        return tuple(out)

    h0 = []
    for sb in range(nsb):
        h0 += [h_ref[:, sb * sw:sb * sw + hw], h_ref[:, sb * sw + hw:(sb + 1) * sw]]
    hs = lax.fori_loop(0, tt, step, tuple(h0))
    for sb in range(nsb):
        h_ref[:, sb * sw:sb * sw + hw] = hs[2 * sb]
        h_ref[:, sb * sw + hw:(sb + 1) * sw] = hs[2 * sb + 1]

    nout = cm_ref.shape[2] // 2
    fwd_row = (lax.broadcasted_iota(jnp.int32, (tt * SUBLANES, nout), 0) % SUBLANES) < SUBLANES // 2
    ys = []
    for sb in range(nsb):
        yy = _dot(bu_ref[:, sb * sw:(sb + 1) * sw].astype(BF16), cm_ref[sb])
        ys.append(jnp.where(fwd_row, yy[:, :nout], yy[:, nout:]))
    y = jnp.concatenate(ys, axis=1).reshape(tt, SUBLANES, nsb * nout)
    yf_ref[...] = y
    for t in range(tt):
        yb_ref[tt - 1 - t] = y[t]


def s5_scan(u, bm, cm, a, t_ctx):
    ttot, rows, width = u.shape
    nsb, _, sw = bm.shape
    tt = 64
    ncb, ntb = t_ctx // tt, ttot // tt
    fwd = pl.BlockSpec((tt, rows, width), lambda g: (g, 0, 0))
    bwd = pl.BlockSpec((tt, rows, width), lambda g: (jnp.where(g < ncb, ncb - 1 - g, ntb + ncb - 1 - g), 0, 0))
    full = lambda x: pl.BlockSpec(x.shape, lambda g: (0,) * x.ndim)
    osd = jax.ShapeDtypeStruct(u.shape, F32)
    return pl.pallas_call(
        functools.partial(_s5_kernel, tt=tt),
        grid=(ntb,),
        in_specs=[fwd, bwd, full(bm), full(cm), full(a)],
        out_specs=[fwd, bwd],
        out_shape=[osd, osd],
        scratch_shapes=[pltpu.VMEM((rows, nsb * sw), F32), pltpu.VMEM((tt * rows, nsb * sw), F32),
                        pltpu.VMEM((tt, rows, width), F32)],
        compiler_params=_cparams(("arbitrary",)),
        name="s5_scan",
    )(u, u, bm, cm, a)


def _even_out_kernel(x_ref, of_ref, ob_ref, bonus_ref, g_ref, ys_ref, u_ref, m5_ref,
                     ones_ref, gng_ref, gnb_ref, dskip_ref, wglu_ref, bglu_ref, wo1_ref, wo2_ref, o_ref):
    ones = ones_ref[...]
    inv = 1.0 / HEAD_DIM
    o = of_ref[0] + ob_ref[0]
    mean = _dot_exact(o, ones) * inv
    oc = o - mean
    var = _dot_exact(oc * oc, ones) * inv
    y1 = (oc * lax.rsqrt(var + GN_EPS) * gng_ref[...] + gnb_ref[...] + bonus_ref[0]) * g_ref[0]
    y = ys_ref[0] + dskip_ref[...] * u_ref[0]
    y = jax.nn.gelu(y)
    y2 = y * _sigmoid(_dot(y.astype(BF16), wglu_ref[...]) + bglu_ref[...])
    out = _dot(y1.astype(BF16), wo1_ref[...]) + _dot(y2.astype(BF16), wo2_ref[...])
    o_ref[0] = x_ref[0] + m5_ref[0] * out


def even_out(x, off, o_f, o_b, bonus, g, ys, u, m5, consts, tm):
    bsz, t, d = x.shape
    w = o_f.shape[2]
    tokd = pl.BlockSpec((1, tm, d), lambda b, i: (b, i, 0))
    tokw = pl.BlockSpec((1, tm, w), lambda b, i: (b, i + off, 0))
    toks = pl.BlockSpec((1, tm, u.shape[2]), lambda b, i: (b, i + off, 0))
    full = lambda a: pl.BlockSpec(a.shape, lambda b, i: (0,) * a.ndim)
    return pl.pallas_call(
        _even_out_kernel,
        grid=(bsz, t // tm),
        in_specs=[tokd, tokw, tokw, tokw, tokw, toks, toks,
                  pl.BlockSpec((1, 1, d), lambda b, i: (b, 0, 0))] + [full(a) for a in consts],
        out_specs=tokd,
        out_shape=jax.ShapeDtypeStruct((bsz, t, d), F32),
        compiler_params=_cparams(("parallel", "parallel")),
        name="even_out",
    )(x, o_f, o_b, bonus, g, ys, u, m5, *consts)


def _hy_in_kernel(x_ref, xp_ref, xn_ref, gain_ref, shift_ref, scale_ref, w_ref, cw_ref, cb_ref,
                  z_ref, zb_ref, g1_ref, g2_ref, *, nt, c):
    i = pl.program_id(1)
    gain, shift, scale = gain_ref[...], shift_ref[0], scale_ref[0]
    h = _rms_mod(x_ref[0], gain, shift, scale).astype(BF16)
    hp = _rms_mod(xp_ref[0], gain, shift, scale).astype(BF16)
    hn = _rms_mod(xn_ref[0], gain, shift, scale).astype(BF16)
    tm = h.shape[0]
    row = lax.broadcasted_iota(jnp.int32, (tm, c), 0)
    outs = (z_ref, g1_ref, g2_ref)
    for part in range(3):
        wp = w_ref[:, part * c:(part + 1) * c]
        p = _dot(h, wp)
        pp = jnp.where(i > 0, _dot(hp, wp)[7:8], 0.0)
        pn = jnp.where(i < nt - 1, _dot(hn, wp)[0:1], 0.0)
        pm1 = jnp.where(row == 0, pp, pltpu.roll(p, 1, 0))
        pp1 = jnp.where(row == tm - 1, pn, pltpu.roll(p, tm - 1, 0))
        cw = cw_ref[:, part * c:(part + 1) * c]
        q = cw[0:1] * pm1 + cw[1:2] * p + cw[2:3] * pp1 + cb_ref[:, part * c:(part + 1) * c]
        outs[part][0] = q
        if part == 0:
            zb_ref[0] = q.astype(BF16)


def hyena_in(x, gain, shift, scale, w, conv_w, conv_b):
    bsz, t, d = x.shape
    c = w.shape[1] // 3
    tm = min(512, t)
    nt = t // tm
    hb = tm // 8
    nhb = t // 8
    vec = pl.BlockSpec((1, 1, d), lambda b, i: (b, 0, 0))
    tok = pl.BlockSpec((1, tm, c), lambda b, i: (b, i, 0))
    sds = jax.ShapeDtypeStruct((bsz, t, c), F32)
    return pl.pallas_call(
        functools.partial(_hy_in_kernel, nt=nt, c=c),
        grid=(bsz, nt),
        in_specs=[pl.BlockSpec((1, tm, d), lambda b, i: (b, i, 0)),
                  pl.BlockSpec((1, 8, d), lambda b, i: (b, jnp.maximum(i * hb - 1, 0), 0)),
                  pl.BlockSpec((1, 8, d), lambda b, i: (b, jnp.minimum((i + 1) * hb, nhb - 1), 0)),
                  pl.BlockSpec((1, d), lambda b, i: (0, 0)),
                  vec, vec,
                  pl.BlockSpec(w.shape, lambda b, i: (0, 0)),
                  pl.BlockSpec(conv_w.shape, lambda b, i: (0, 0)),
                  pl.BlockSpec((1, 3 * c), lambda b, i: (0, 0))],
        out_specs=[tok, tok, tok, tok],
        out_shape=[sds, jax.ShapeDtypeStruct((bsz, t, c), BF16), sds, sds],
        compiler_params=_cparams(("parallel", "parallel")),
        name="hyena_in",
    )(x, x, x, gain.reshape(1, d), shift, scale, w, conv_w, conv_b.reshape(1, 3 * c))


def _pack_complex(re, im):
    bits = lambda x: lax.bitcast_convert_type(x.astype(BF16).astype(F32), jnp.uint32)
    return lax.shift_right_logical(bits(re), jnp.uint32(16)) | bits(im)


def _unpack_complex(w):
    re = lax.bitcast_convert_type(lax.shift_left(w, jnp.uint32(16)), F32)
    im = lax.bitcast_convert_type(w & jnp.uint32(0xFFFF0000), F32)
    return jnp.concatenate([re, im], axis=0).astype(BF16)


def _fft_conv_kernel(z_ref, kf_ref, w1_ref, f2_ref, g2i_ref, f1c_ref, tw_ref, twt_ref, o_ref, a_ref, t_ref,
                     *, n1, k1c):
    s = pl.program_id(2)
    n2 = FFT_N2
    nd = n1 // 2
    slab = n1 // k1c

    @pl.when(s == 0)
    def _():
        w1 = w1_ref[...]

        def body(j, carry):
            x = jnp.concatenate([z_ref[0, j], z_ref[1, j]], axis=0)
            a = _dot(w1, x)
            a_ref[pl.ds(pl.multiple_of(j * n1, n1), n1), :] = _pack_complex(a[:n1], a[n1:])
            return carry

        lax.fori_loop(0, n2, body, 0, unroll=8)

    @pl.when((s > 0) & (s <= k1c))
    def _():
        f2r, f2i = f2_ref[0], f2_ref[1]
        g2i = g2i_ref[...]
        base = (s - 1) * slab

        def body(kk, carry):
            k1 = base + kk
            twr = tw_ref[0, pl.ds(k1, 1), :]
            twi = tw_ref[1, pl.ds(k1, 1), :]
            gr = f2r * twr - f2i * twi
            gi = f2r * twi + f2i * twr
            gmat = jnp.concatenate([jnp.concatenate([gr, -gi], axis=1),
                                    jnp.concatenate([gi, gr], axis=1)], axis=0).astype(BF16)
            ak = _unpack_complex(a_ref[pl.ds(k1, n2, stride=n1), :])
            x = _dot(gmat, ak)
            xr, xi = x[:n2], x[n2:]
            kr, ki = kf_ref[0, kk], kf_ref[1, kk]
            pr = xr * kr - xi * ki
            pi = xr * ki + xi * kr
            tt = _dot(g2i, jnp.concatenate([pr, pi], axis=0).astype(BF16))
            t_ref[pl.ds(pl.multiple_of(k1 * n2, n2), n2), :] = _pack_complex(tt[:n2], tt[n2:])
            return carry

        lax.fori_loop(0, slab, body, 0, unroll=4)

    @pl.when(s == k1c + 1)
    def _():
        f1r, f1i = f1c_ref[0], f1c_ref[1]

        def body(j, carry):
            twr = twt_ref[0, pl.ds(j, 1), :]
            twi = twt_ref[1, pl.ds(j, 1), :]
            wr = f1r * twr + f1i * twi
            wi = f1i * twr - f1r * twi
            w3 = jnp.concatenate([jnp.concatenate([wr, -wi], axis=1),
                                  jnp.concatenate([wi, wr], axis=1)], axis=0).astype(BF16)
            tn = _unpack_complex(t_ref[pl.ds(j, n1, stride=n2), :])
            y = _dot(w3, tn)
            o_ref[0, j] = y[:nd]
            o_ref[1, j] = y[nd:]
            return carry

        lax.fori_loop(0, n2, body, 0, unroll=8)


def _fft_consts(n1):
    n2 = FFT_N2
    n = n1 * n2
    nd = n1 // 2
    k1 = np.arange(n1)
    f1 = np.exp(-2j * np.pi * np.outer(k1, np.arange(nd)) / n1)
    w1 = np.block([[f1.real, -f1.imag], [f1.imag, f1.real]])
    f2 = np.exp(-2j * np.pi * np.outer(np.arange(n2), np.arange(n2)) / n2)
    f2c = np.conj(f2)
    g2i = np.block([[f2c.real, -f2c.imag], [f2c.imag, f2c.real]])
    f1c = np.exp(2j * np.pi * np.outer(np.arange(nd), k1) / n1) / n
    tw = np.exp(-2j * np.pi * np.outer(k1, np.arange(n2)) / n)
    cplx = lambda m: jnp.asarray(np.stack([m.real, m.imag]), F32)
    return (jnp.asarray(w1, BF16), cplx(f2), jnp.asarray(g2i, BF16), cplx(f1c), cplx(tw), cplx(tw.T))


def fft_conv(zt, kf):
    bsz, n2, nd, c = zt.shape
    n1 = 2 * nd
    cb = LANES
    k1c = 8 if n1 % 8 == 0 and n1 >= 64 else 2
    slab = n1 // k1c
    consts = _fft_consts(n1)
    full = lambda a: pl.BlockSpec(a.shape, lambda j, p, s: (0,) * a.ndim)
    blk = pl.BlockSpec((2, n2, nd, cb), lambda j, p, s: (p, 0, 0, j))
    return pl.pallas_call(
        functools.partial(_fft_conv_kernel, n1=n1, k1c=k1c),
        grid=(c // cb, bsz // 2, k1c + 2),
        in_specs=[blk,
                  pl.BlockSpec((2, slab, n2, cb), lambda j, p, s: (0, jnp.clip(s - 1, 0, k1c - 1), 0, j))]
                 + [full(a) for a in consts],
        out_specs=blk,
        out_shape=jax.ShapeDtypeStruct(zt.shape, F32),
        scratch_shapes=[pltpu.VMEM((n2 * n1, cb), jnp.uint32), pltpu.VMEM((n1 * n2, cb), jnp.uint32)],
        compiler_params=_cparams(("parallel", "parallel", "arbitrary")),
        name="fft_conv",
    )(zt, kf, *consts)


def _fft_fwd_kernel(z_ref, w1_ref, f2_ref, tw_ref, o_ref, a_ref, *, n1, k1c):
    s = pl.program_id(2)
    n2 = FFT_N2
    slab = n1 // k1c

    @pl.when(s == 0)
    def _():
        w1 = w1_ref[...]

        def body(j, carry):
            a = _dot(w1, z_ref[0, j])
            a_ref[pl.ds(pl.multiple_of(j * n1, n1), n1), :] = _pack_complex(a[:n1], a[n1:])
            return carry

        lax.fori_loop(0, n2, body, 0, unroll=8)

    @pl.when(s > 0)
    def _():
        f2r, f2i = f2_ref[0], f2_ref[1]
        base = (s - 1) * slab

        def body(kk, carry):
            k1 = base + kk
            twr = tw_ref[0, pl.ds(k1, 1), :]
            twi = tw_ref[1, pl.ds(k1, 1), :]
            gr = f2r * twr - f2i * twi
            gi = f2r * twi + f2i * twr
            gmat = jnp.concatenate([jnp.concatenate([gr, -gi], axis=1),
                                    jnp.concatenate([gi, gr], axis=1)], axis=0).astype(BF16)
            x = _dot(gmat, _unpack_complex(a_ref[pl.ds(k1, n2, stride=n1), :]))
            o_ref[0, 0, kk] = x[:n2]
            o_ref[0, 1, kk] = x[n2:]
            return carry

        lax.fori_loop(0, slab, body, 0, unroll=4)


def fft_forward(zt):
    rows, n2, nd, c = zt.shape
    n1 = 2 * nd
    cb = LANES
    k1c = 8 if n1 % 8 == 0 and n1 >= 64 else 2
    slab = n1 // k1c
    w1, f2, _, _, tw, _ = _fft_consts(n1)
    w1 = w1[:, :nd]
    full = lambda a: pl.BlockSpec(a.shape, lambda j, r, s: (0,) * a.ndim)
    return pl.pallas_call(
        functools.partial(_fft_fwd_kernel, n1=n1, k1c=k1c),
        grid=(c // cb, rows, k1c + 1),
        in_specs=[pl.BlockSpec((1, n2, nd, cb), lambda j, r, s: (r, 0, 0, j)), full(w1), full(f2), full(tw)],
        out_specs=pl.BlockSpec((1, 2, slab, n2, cb), lambda j, r, s: (r, 0, jnp.maximum(s - 1, 0), 0, j)),
        out_shape=jax.ShapeDtypeStruct((rows, 2, n1, n2, c), F32),
        scratch_shapes=[pltpu.VMEM((n2 * n1, cb), jnp.uint32)],
        compiler_params=_cparams(("parallel", "parallel", "arbitrary")),
        name="fft_forward",
    )(zt, w1, f2, tw)


def _hy_filter_kernel(feat_ref, fw1_ref, fb1_ref, fw2_ref, fb2_ref, fw3_ref, fb3_ref, fw4_ref, freq_ref,
                      delta_ref, z_ref, ss_ref):
    @pl.when(pl.program_id(1) == 0)
    def _():
        ss_ref[...] = jnp.zeros_like(ss_ref)

    feats = feat_ref[...]
    t = feats[:, 0:1]
    fr = freq_ref[...]
    h = jnp.sin(fr * (_dot_exact(feats, fw1_ref[...]) + fb1_ref[...]))
    h = jnp.sin(fr * (_dot_exact(h, fw2_ref[...]) + fb2_ref[...]))
    h = jnp.sin(fr * (_dot_exact(h, fw3_ref[...]) + fb3_ref[...]))
    lag0 = jnp.where(pl.program_id(0) % 2 == 1, 0.0, 1.0)
    filt = _dot_exact(h, fw4_ref[...]) * (jnp.exp(-t * delta_ref[...]) * jnp.where(t == 0.0, lag0, 1.0))
    z_ref[0] = filt.astype(BF16)
    ss_ref[0] += jnp.sum(filt * filt, axis=0, keepdims=True)


def hyena_filter(t, fw1, fb1, fw2, fb2, fw3, fb3, fw4, freq, width):
    nd = t // FFT_N2
    pos = (jnp.arange(FFT_N2, dtype=F32)[:, None] + FFT_N2 * jnp.arange(nd, dtype=F32)[None, :]).reshape(-1, 1)
    tt = pos / max(t - 1, 1)
    ang = 2 * math.pi * pos / t
    nb = (HY_EMB - 1) // 2
    bands = jnp.linspace(1e-4, nb - 1, nb, dtype=F32)[None]
    feats = jnp.concatenate([tt, jnp.cos(bands * ang), -jnp.sin(bands * ang),
                             jnp.zeros((t, LANES - HY_EMB), F32)], axis=-1)
    deltas = jnp.abs(jnp.linspace(HY_MIN_DECAY, HY_MAX_DECAY, width, dtype=F32)).reshape(1, width)
    rows = 2 * HY_ORDER
    tm = min(512, t)
    hid = fw1.shape[1]
    full = lambda a: pl.BlockSpec(a.shape, lambda r, i: (0,) * a.ndim)
    fw1p = jnp.concatenate([fw1, jnp.zeros((LANES - HY_EMB, hid), F32)], axis=0)
    vecs = [fw1p, fb1.reshape(1, hid), fw2, fb2.reshape(1, hid), fw3, fb3.reshape(1, hid)]
    return pl.pallas_call(
        _hy_filter_kernel,
        grid=(rows, t // tm),
        in_specs=[pl.BlockSpec((tm, LANES), lambda r, i: (i, 0))] + [full(a) for a in vecs]
                 + [pl.BlockSpec((hid, width), lambda r, i: (0, r)), full(freq.reshape(1, hid)), full(deltas)],
        out_specs=[pl.BlockSpec((1, tm, width), lambda r, i: (r, i, 0)),
                   pl.BlockSpec((1, 1, width), lambda r, i: (r, 0, 0))],
        out_shape=[jax.ShapeDtypeStruct((rows, t, width), BF16), jax.ShapeDtypeStruct((rows, 1, width), F32)],
        compiler_params=_cparams(("parallel", "arbitrary")),
        name="hyena_filter",
    )(feats, *vecs, fw4, freq.reshape(1, hid), deltas)


def hyena_filter_spectrum_fft(t, fw1, fb1, fw2, fb2, fw3, fb3, fw4, freq, width):
    taps, ss = hyena_filter(t, fw1, fb1, fw2, fb2, fw3, fb3, fw4, freq, width)
    nd = t // FFT_N2
    spec = fft_forward(taps.reshape(2 * HY_ORDER, FFT_N2, nd, width))
    spec = spec.reshape(HY_ORDER, 2, 2, 2 * nd, FFT_N2, width)
    scale = lax.rsqrt(ss.reshape(HY_ORDER, 2, width).sum(axis=1) + 1e-6)[:, None, None, :]
    return jnp.stack([(spec[:, 0, 0] + spec[:, 1, 0]) * scale, (spec[:, 0, 1] - spec[:, 1, 1]) * scale], axis=1)


def _dft_conv_kernel(z_ref, kf_ref, fw_ref, iv_ref, o_ref, *, t):
    x = jnp.concatenate([z_ref[0], z_ref[1]], axis=0)
    spec = _dot(fw_ref[...], x)
    xr, xi = spec[:2 * t], spec[2 * t:]
    kr, ki = kf_ref[0], kf_ref[1]
    pr = xr * kr - xi * ki
    pi = xr * ki + xi * kr
    y = _dot(iv_ref[...], jnp.concatenate([pr, pi], axis=0).astype(BF16))
    o_ref[0] = y[:t]
    o_ref[1] = y[t:]


def dft_conv(zb, kf):
    bsz, t, c = zb.shape
    n = 2 * t
    f = np.exp(-2j * np.pi * np.outer(np.arange(n), np.arange(t)) / n)
    fw = np.block([[f.real, -f.imag], [f.imag, f.real]])
    fi = np.exp(2j * np.pi * np.outer(np.arange(t), np.arange(n)) / n) / n
    iv = np.block([[fi.real, -fi.imag], [fi.imag, fi.real]])
    fw, iv = jnp.asarray(fw, BF16), jnp.asarray(iv, BF16)
    cb = LANES
    blk = pl.BlockSpec((2, t, cb), lambda j, p: (p, 0, j))
    return pl.pallas_call(
        functools.partial(_dft_conv_kernel, t=t),
        grid=(c // cb, bsz // 2),
        in_specs=[blk, pl.BlockSpec((2, n, cb), lambda j, p: (0, 0, j)),
                  pl.BlockSpec(fw.shape, lambda j, p: (0, 0)), pl.BlockSpec(iv.shape, lambda j, p: (0, 0))],
        out_specs=blk,
        out_shape=jax.ShapeDtypeStruct(zb.shape, F32),
        compiler_params=_cparams(("parallel", "parallel")),
        name="dft_conv",
    )(zb, kf, fw, iv)


def _hy_gate_kernel(conv_ref, z_ref, gate_ref, bias_ref, o_ref, ob_ref):
    y = gate_ref[0] * (conv_ref[0] + bias_ref[...] * z_ref[0])
    o_ref[0] = y
    ob_ref[0] = y.astype(BF16)


def hyena_gate(conv, z, gate, bias):
    bsz, t, c = z.shape
    tm = min(512, t)
    tok = pl.BlockSpec((1, tm, c), lambda b, i: (b, i, 0))
    return pl.pallas_call(
        _hy_gate_kernel,
        grid=(bsz, t // tm),
        in_specs=[tok, tok, tok, pl.BlockSpec((1, c), lambda b, i: (0, 0))],
        out_specs=[tok, tok],
        out_shape=[jax.ShapeDtypeStruct(z.shape, F32), jax.ShapeDtypeStruct(z.shape, BF16)],
        compiler_params=_cparams(("parallel", "parallel")),
        name="hyena_gate",
    )(conv, z, gate, bias.reshape(1, c))


def _hy_out_kernel(x_ref, conv_ref, z_ref, gate_ref, bias_ref, w_ref, m5_ref, o_ref):
    y = gate_ref[0] * (conv_ref[0] + bias_ref[...] * z_ref[0])
    o_ref[0] = x_ref[0] + m5_ref[0] * _dot(y.astype(BF16), w_ref[...])


def hyena_out(x, conv, z, gate, bias, w, m5):
    bsz, t, d = x.shape
    c = z.shape[2]
    tm = min(512, t)
    tokd = pl.BlockSpec((1, tm, d), lambda b, i: (b, i, 0))
    tokc = pl.BlockSpec((1, tm, c), lambda b, i: (b, i, 0))
    return pl.pallas_call(
        _hy_out_kernel,
        grid=(bsz, t // tm),
        in_specs=[tokd, tokc, tokc, tokc, pl.BlockSpec((1, c), lambda b, i: (0, 0)),
                  pl.BlockSpec(w.shape, lambda b, i: (0, 0)), pl.BlockSpec((1, 1, d), lambda b, i: (b, 0, 0))],
        out_specs=tokd,
        out_shape=jax.ShapeDtypeStruct(x.shape, F32),
        compiler_params=_cparams(("parallel", "parallel")),
        name="hyena_out",
    )(x, conv, z, gate, bias.reshape(1, c), w, m5)


def _hyena_filter_spectrum(t, fw1, fb1, fw2, fb2, fw3, fb3, fw4, freq, width):
    pos = jnp.arange(t, dtype=F32)[:, None]
    tt = pos / max(t - 1, 1)
    ang = 2 * math.pi * pos / t
    nb = (HY_EMB - 1) // 2
    bands = jnp.linspace(1e-4, nb - 1, nb, dtype=F32)[None]
    feats = jnp.concatenate([tt, jnp.cos(bands * ang), -jnp.sin(bands * ang)], axis=-1)
    hdn = jnp.sin(freq * (feats @ fw1 + fb1))
    hdn = jnp.sin(freq * (hdn @ fw2 + fb2))
    hdn = jnp.sin(freq * (hdn @ fw3 + fb3))
    filt = (hdn @ fw4).reshape(t, HY_ORDER, 2, width)
    deltas = jnp.abs(jnp.linspace(HY_MIN_DECAY, HY_MAX_DECAY, width, dtype=F32))
    filt = filt * jnp.exp(-tt[:, :, None, None] * deltas)
    fwd, bwd = filt[:, :, 0], filt[:, :, 1]
    kern = jnp.concatenate([fwd, jnp.zeros_like(fwd[:1]), bwd[:0:-1]], axis=0)
    kern = kern * lax.rsqrt(jnp.sum(kern * kern, axis=0, keepdims=True) + 1e-6)
    spec = jnp.fft.fft(kern, axis=0)
    return jnp.stack([jnp.real(spec), jnp.imag(spec)], axis=1).transpose(2, 1, 0, 3).astype(F32)


def _block_diag_pair(m):
    z = jnp.zeros_like(m[0])
    return jnp.concatenate([jnp.concatenate([m[0], z], axis=1), jnp.concatenate([z, m[1]], axis=1)], axis=0)


def _even_mixer(lat, cx, ml, mc, gain, prm, ctx_out):
    (w_in, mu, w0, w_up, a0, a_up, g_up, k_k, k_a, r_k, gn_g, gn_b,
     lam_re, lam_im, log_dt, b_re, b_im, c_re, c_im, d_skip, w_glu, b_glu, w_out) = prm
    bsz, t_lat, d = lat.shape
    t_ctx = cx.shape[1]
    width = k_k.shape[0]
    heads = width // HEAD_DIM
    n_cols = mu.shape[0]
    s5w = d_skip.shape[0]
    ttot = t_ctx + t_lat
    nch = bsz * heads
    tm = t_ctx
    assert tm % GRID_W == 0 and t_lat % tm == 0 and 4 * nch == LANES and 2 * bsz == SUBLANES

    p_all = mod_matmul_stream(cx, lat, gain, mc, ml, w_in.astype(BF16), tm)

    head_of = np.arange(width) // HEAD_DIM
    ones = jnp.asarray(head_of[:, None] == head_of[None, :], F32)
    feat_consts = (mu.reshape(1, -1), _block_diag_pair(w_up).astype(BF16), w0.reshape(1, -1),
                   _block_diag_pair(a_up).astype(BF16), a0.reshape(1, -1), g_up.astype(BF16),
                   k_k.reshape(1, -1), k_a.reshape(1, -1), r_k.reshape(1, -1), ones)
    kk, v, g, bonus, r, k, w2, a2 = rwkv_features(p_all, n_cols, width, feat_consts, tm)

    def key_major(x):
        x = x.reshape(bsz, ttot, heads, 2, HALF_HEAD).transpose(1, 4, 3, 0, 2)[:, :, None]
        return jnp.broadcast_to(x, (ttot, HALF_HEAD, 2, 2, bsz, heads)).reshape(ttot, HALF_HEAD, LANES)

    def key_major2(x):
        x = x.reshape(2, bsz, ttot, heads, 2, HALF_HEAD)
        return x.transpose(2, 5, 0, 4, 1, 3).reshape(ttot, HALF_HEAD, LANES)

    def value_major(x):
        x = x.reshape(bsz, ttot, heads, HEAD_DIM).transpose(1, 3, 0, 2).reshape(ttot, HEAD_DIM, nch)
        return jnp.concatenate([x, x], axis=-1)

    ka_t = k_a.reshape(heads, 2, HALF_HEAD).transpose(2, 1, 0)[:, None, :, None, :]
    ka_t = jnp.broadcast_to(ka_t, (HALF_HEAD, 2, 2, bsz, heads)).reshape(HALF_HEAD, LANES)
    o_f, o_b = rwkv_scan(key_major(r), key_major(kk), key_major(k), key_major2(w2), key_major2(a2),
                         value_major(v), ka_t, t_ctx)

    def token_major(o, lane0):
        o = o[..., lane0:lane0 + nch].reshape(ttot, HEAD_DIM, bsz, heads)
        return o.transpose(2, 0, 3, 1).reshape(bsz, ttot, width)

    o_f, o_b = token_major(o_f, 0), token_major(o_b, LANES // 2)

    ng = lam_re.shape[1]
    gps = ng // S5_SUPER
    lam = lax.complex(lam_re, lam_im)
    dt = jnp.exp(log_dt)[..., None]
    a_bar = jnp.exp(lam * dt)
    b_bar = ((a_bar - 1) / lam)[..., None] * lax.complex(b_re, b_im)
    eye = jnp.eye(gps, dtype=F32)

    def b_mat(x):
        x = x.reshape(2, S5_SUPER, gps, S5_STATE, S5_GROUP)
        m = jnp.einsum('dsgph,gk->sdghkp', x, eye)
        return m.reshape(S5_SUPER, 2 * gps * S5_GROUP, gps * S5_STATE)

    def c_mat(x):
        x = x.reshape(2, S5_SUPER, gps, S5_GROUP, S5_STATE)
        m = jnp.einsum('dsghp,gk->skpdgh', x, eye)
        return m.reshape(S5_SUPER, gps * S5_STATE, 2 * gps * S5_GROUP)

    bm = jnp.concatenate([b_mat(jnp.real(b_bar)), b_mat(jnp.imag(b_bar))], axis=2).astype(BF16)
    cm = jnp.concatenate([c_mat(c_re), -c_mat(c_im)], axis=1).astype(BF16)

    def a_rows(x):
        x = x.reshape(2, 1, S5_SUPER, gps * S5_STATE)
        return jnp.broadcast_to(x, (2, bsz, S5_SUPER, gps * S5_STATE)).reshape(2 * bsz, S5_SUPER, -1)

    a_arr = jnp.concatenate([a_rows(jnp.real(a_bar)), a_rows(jnp.imag(a_bar))], axis=2)
    a_arr = a_arr.reshape(1, 2 * bsz, -1)

    u_all = p_all[..., n_cols:]
    u_tm = u_all.transpose(1, 0, 2)
    y_f, y_b = s5_scan(jnp.concatenate([u_tm, u_tm], axis=1), bm, cm, a_arr, t_ctx)
    ys = (y_f[:, :bsz] + y_b[:, bsz:]).transpose(1, 0, 2)

    wo = w_out.astype(BF16)
    out_consts = (ones, gn_g.reshape(1, -1), gn_b.reshape(1, -1), d_skip.reshape(1, -1),
                  w_glu.astype(BF16), b_glu.reshape(1, -1), wo[:width], wo[width:])
    lat = even_out(lat, 1, o_f, o_b, bonus, g, ys, u_all, ml[5], out_consts, tm)
    if ctx_out:
        cx = even_out(cx, 0, o_f, o_b, bonus, g, ys, u_all, mc[5], out_consts, tm)
    return lat, cx


def _hyena_mixer(x, gain, m, prm):
    (w_in, conv_w, conv_b, fw1, fb1, fw2, fb2, fw3, fb3, fw4, freq, bias_d, w_out) = prm
    bsz, t, d = x.shape
    c = w_out.shape[0]
    use_fft = t % (FFT_N2 * 2) == 0 and t >= 4 * FFT_N2
    if use_fft:
        kf = hyena_filter_spectrum_fft(t, fw1, fb1, fw2, fb2, fw3, fb3, fw4, freq, c)
    else:
        kf = _hyena_filter_spectrum(t, fw1, fb1, fw2, fb2, fw3, fb3, fw4, freq, c)
    z, zb, g1, g2 = hyena_in(x, gain, m[3], m[4], w_in.astype(BF16), conv_w, conv_b)
    gates = (g1, g2)
    for n in range(HY_ORDER):
        if use_fft:
            nd = t // FFT_N2
            zt = zb.reshape(bsz, nd, FFT_N2, c).transpose(0, 2, 1, 3)
            conv = fft_conv(zt, kf[n]).transpose(0, 2, 1, 3).reshape(bsz, t, c)
        else:
            conv = dft_conv(zb, kf[n])
        if n < HY_ORDER - 1:
            z, zb = hyena_gate(conv, z, gates[n], bias_d[n])
        else:
            return hyena_out(x, conv, z, gates[n], bias_d[n], w_out.astype(BF16), m[5])


def kernel(x, c, ctx, c_ctx, norm_g, ada_w, ada_b, ffn_wg, ffn_wu, ffn_wd, final_g, ev_w_in, ev_mu, ev_w0, ev_w_up, ev_a0, ev_a_up, ev_g_up, ev_k_k, ev_k_a, ev_r_k, ev_gn_g, ev_gn_b, ev_lam_re, ev_lam_im, ev_log_dt, ev_b_re, ev_b_im, ev_c_re, ev_c_im, ev_d, ev_w_glu, ev_b_glu, ev_w_out, od_w_in, od_conv_w, od_conv_b, od_fw1, od_fb1, od_fw2, od_fb2, od_fw3, od_fb3, od_fw4, od_freq, od_bias, od_w_out):
    depth = norm_g.shape[0]
    bsz, _, d = x.shape
    n_even = (depth + 1) // 2
    last_ctx = 2 * (n_even - 1)

    cond8 = jnp.concatenate([c, c_ctx[None], jnp.zeros((8 - bsz - 1, d), F32)], axis=0)
    mods = ada_mods_all(cond8, ada_w, ada_b)

    wg, wu, wd = ffn_wg.astype(BF16), ffn_wu.astype(BF16), ffn_wd.astype(BF16)
    lat, cx = x, ctx
    for l in range(depth):
        run_ctx = l <= last_ctx
        ctx_out = l < last_ctx
        i = l // 2
        ml = [mods[l, :bsz, None, k * d:(k + 1) * d] for k in range(N_MOD)]
        mc = [jnp.broadcast_to(mods[l, bsz:bsz + 1, None, k * d:(k + 1) * d], (bsz, 1, d))
              for k in range(N_MOD)]
        lat = ffn_half(lat, norm_g[l, 0], ml[0], ml[1], ml[2], wg[l, 0], wu[l, 0], wd[l, 0])
        if run_ctx:
            cx = ffn_half(cx, norm_g[l, 0], mc[0], mc[1], mc[2], wg[l, 0], wu[l, 0], wd[l, 0])
        if l % 2 == 0:
            prm = (ev_w_in[i], ev_mu[i], ev_w0[i], ev_w_up[i], ev_a0[i], ev_a_up[i], ev_g_up[i],
                   ev_k_k[i], ev_k_a[i], ev_r_k[i], ev_gn_g[i], ev_gn_b[i],
                   ev_lam_re[i], ev_lam_im[i], ev_log_dt[i], ev_b_re[i], ev_b_im[i], ev_c_re[i], ev_c_im[i],
                   ev_d[i], ev_w_glu[i], ev_b_glu[i], ev_w_out[i])
            lat, cx = _even_mixer(lat, cx, ml, mc, norm_g[l, 1], prm, ctx_out)
        else:
            prm = (od_w_in[i], od_conv_w[i], od_conv_b[i], od_fw1[i], od_fb1[i], od_fw2[i], od_fb2[i],
                   od_fw3[i], od_fb3[i], od_fw4[i], od_freq[i], od_bias[i], od_w_out[i])
            lat = _hyena_mixer(lat, norm_g[l, 1], ml, prm)
            if ctx_out:
                cx = _hyena_mixer(cx, norm_g[l, 1], mc, prm)
        fin = final_g if l == depth - 1 else None
        lat = ffn_half(lat, norm_g[l, 2], ml[6], ml[7], ml[8], wg[l, 1], wu[l, 1], wd[l, 1], fin)
        if ctx_out:
            cx = ffn_half(cx, norm_g[l, 2], mc[6], mc[7], mc[8], wg[l, 1], wu[l, 1], wd[l, 1])
    return lat
```

```python
import functools
import math

import numpy as np
import jax
import jax.numpy as jnp
from jax import lax
from jax.experimental import pallas as pl
from jax.experimental.pallas import tpu as pltpu

F32 = jnp.float32
BF16 = jnp.bfloat16
HIGHEST = lax.Precision.HIGHEST

N_MOD = 9
NORM_EPS = 1e-6
GN_EPS = 64e-5
GRID_W = 64
HEAD_DIM = 64
HALF_HEAD = HEAD_DIM // 2
S5_GROUP = 16
S5_STATE = 64
S5_SUPER = 4
HY_ORDER = 2
HY_EMB = 33
HY_MIN_DECAY = math.log(1e-2) / 1.5
HY_MAX_DECAY = math.log(1e-2) / 0.3
LANES = 128
SUBLANES = 8
MXU_DIM = 256
FFT_N2 = 128
VMEM_LIMIT = 56 * 1024 * 1024


def _cparams(sem, vmem=VMEM_LIMIT):
    return pltpu.CompilerParams(dimension_semantics=sem, vmem_limit_bytes=vmem)


def _dot(a, b):
    return jnp.dot(a, b, preferred_element_type=F32)


def _dot_exact(a, b):
    return jnp.dot(a, b, preferred_element_type=F32, precision=HIGHEST)


def _rms_mod(x, gain, shift, scale):
    ms = jnp.mean(x * x, axis=-1, keepdims=True)
    return x * lax.rsqrt(ms + NORM_EPS) * gain * (1.0 + scale) + shift


def _sigmoid(x):
    return 1.0 / (1.0 + jnp.exp(-x))


def _silu(x):
    return x * _sigmoid(x)


def _ada_kernel(c_ref, w_ref, b_ref, o_ref):
    s = _silu(c_ref[...])
    o_ref[0] = _dot(s.astype(BF16), w_ref[0].astype(BF16)) + b_ref[0]


def ada_mods_all(cond8, ada_w, ada_b):
    depth, d, n = ada_w.shape
    tn = n // 8
    return pl.pallas_call(
        _ada_kernel,
        grid=(depth, n // tn),
        in_specs=[pl.BlockSpec((8, d), lambda l, j: (0, 0)),
                  pl.BlockSpec((1, d, tn), lambda l, j: (l, 0, j)),
                  pl.BlockSpec((1, 1, tn), lambda l, j: (l, 0, j))],
        out_specs=pl.BlockSpec((1, 8, tn), lambda l, j: (l, 0, j)),
        out_shape=jax.ShapeDtypeStruct((depth, 8, n), F32),
        compiler_params=_cparams(("parallel", "parallel")),
        name="ada_mods",
    )(cond8, ada_w, ada_b.reshape(depth, 1, n))


def _ffn_kernel(x_ref, gain_ref, shift_ref, scale_ref, gate_ref, wg_ref, wu_ref, wd_ref, fg_ref,
                o_ref, *, final_norm, chunks):
    x = x_ref[0]
    h = _rms_mod(x, gain_ref[...], shift_ref[0], scale_ref[0]).astype(BF16)
    acc = None
    for lo, hi in chunks:
        g = _dot(h, wg_ref[:, lo:hi])
        u = _dot(h, wu_ref[:, lo:hi])
        part = _dot((_silu(g) * u).astype(BF16), wd_ref[lo:hi, :])
        acc = part if acc is None else acc + part
    y = x + 0.5 * gate_ref[0] * acc
    if final_norm:
        ms = jnp.mean(y * y, axis=-1, keepdims=True)
        y = y * lax.rsqrt(ms + NORM_EPS) * fg_ref[...]
    o_ref[0] = y


def ffn_half(x, gain, shift, scale, gate, wg, wu, wd, final_g=None):
    bsz, t, d = x.shape
    ff = wg.shape[1]
    tm = min(512, t)
    step = 4 * MXU_DIM
    chunks = tuple((lo, min(lo + step, ff)) for lo in range(0, ff, step))
    fg = jnp.ones((1, d), F32) if final_g is None else final_g.reshape(1, d)
    vec = pl.BlockSpec((1, 1, d), lambda b, i: (b, 0, 0))
    resident = lambda a: pl.BlockSpec(a.shape, lambda b, i: (0, 0), pipeline_mode=pl.Buffered(1))
    return pl.pallas_call(
        functools.partial(_ffn_kernel, final_norm=final_g is not None, chunks=chunks),
        grid=(bsz, t // tm),
        in_specs=[pl.BlockSpec((1, tm, d), lambda b, i: (b, i, 0)),
                  pl.BlockSpec((1, d), lambda b, i: (0, 0)),
                  vec, vec, vec, resident(wg), resident(wu), resident(wd),
                  pl.BlockSpec((1, d), lambda b, i: (0, 0))],
        out_specs=pl.BlockSpec((1, tm, d), lambda b, i: (b, i, 0)),
        out_shape=jax.ShapeDtypeStruct((bsz, t, d), F32),
        compiler_params=_cparams(("parallel", "parallel")),
        name="ffn_half",
    )(x, gain.reshape(1, d), shift, scale, gate, wg, wu, wd, fg)


def _modmm_kernel(cx_ref, lat_ref, gain_ref, shc_ref, scc_ref, shl_ref, scl_ref, w_ref, o_ref, u_ref, *, nc):
    is_ctx = pl.program_id(1) < nc
    x = jnp.where(is_ctx, cx_ref[0], lat_ref[0])
    shift = jnp.where(is_ctx, shc_ref[0], shl_ref[0])
    scale = jnp.where(is_ctx, scc_ref[0], scl_ref[0])
    h = _rms_mod(x, gain_ref[...], shift, scale).astype(BF16)
    n1 = o_ref.shape[2]
    o_ref[0] = _dot(h, w_ref[:, :n1])
    u_ref[0] = _dot(h, w_ref[:, n1:])


def mod_matmul_stream(cx, lat, gain, mc, ml, w, tm, n_first):
    bsz, t_ctx, d = cx.shape
    t_lat = lat.shape[1]
    n = w.shape[1]
    nc = t_ctx // tm
    nt = nc + t_lat // tm
    vec = pl.BlockSpec((1, 1, d), lambda b, i: (b, 0, 0))
    return pl.pallas_call(
        functools.partial(_modmm_kernel, nc=nc),
        grid=(bsz, nt),
        in_specs=[pl.BlockSpec((1, tm, d), lambda b, i: (b, jnp.minimum(i, nc - 1), 0)),
                  pl.BlockSpec((1, tm, d), lambda b, i: (b, jnp.maximum(i - nc, 0), 0)),
                  pl.BlockSpec((1, d), lambda b, i: (0, 0)),
                  vec, vec, vec, vec,
                  pl.BlockSpec((d, n), lambda b, i: (0, 0))],
        out_specs=[pl.BlockSpec((1, tm, n_first), lambda b, i: (b, i, 0)),
                   pl.BlockSpec((1, tm, n - n_first), lambda b, i: (b, i, 0))],
        out_shape=[jax.ShapeDtypeStruct((bsz, t_ctx + t_lat, n_first), F32),
                   jax.ShapeDtypeStruct((bsz, t_ctx + t_lat, n - n_first), F32)],
        compiler_params=_cparams(("parallel", "parallel")),
        name="mod_matmul",
    )(cx, lat, gain.reshape(1, d), mc[3], mc[4], ml[3], ml[4], w)


def _rwkv_feat_kernel(p_ref, pu_ref, pd_ref, mu_ref, wup_ref, w0_ref, aup_ref, a0_ref, gup_ref,
                      kk_s_ref, ka_ref, rk_ref, ones_ref,
                      kk_o, v_o, g_o, bonus_o, r_o, k_o, w_o, a_o, *, nt, width):
    i = pl.program_id(1)
    is_ctx = i == 0
    p = p_ref[0]
    tm = p.shape[0]
    row = lax.broadcasted_iota(jnp.int32, p.shape, 0)
    lane = lax.broadcasted_iota(jnp.int32, p.shape, 1) % 4
    prev = pltpu.roll(p, 1, 0)
    nxt = pltpu.roll(p, tm - 1, 0)
    col = jnp.where(is_ctx, row, row % GRID_W)
    last = jnp.where(is_ctx, tm - 1, GRID_W - 1)
    left = jnp.where(col == 0, 0.0, prev)
    right = jnp.where(col == last, 0.0, nxt)
    up_halo = jnp.where(i > 1, pu_ref[0], 0.0)
    dn_halo = jnp.where(i < nt - 1, pd_ref[0], 0.0)
    up = jnp.where(is_ctx, left, jnp.concatenate([up_halo, p[:tm - GRID_W]], axis=0))
    down = jnp.where(is_ctx, right, jnp.concatenate([p[GRID_W:], dn_halo], axis=0))
    shifted = jnp.where(lane == 0, left, jnp.where(lane == 1, right, jnp.where(lane == 2, up, down)))
    q = p + mu_ref[...] * (shifted - p)

    w = width
    r, k, v = q[:, :w], q[:, w:2 * w], q[:, 2 * w:3 * w]
    wd = q[:, 3 * w:3 * w + LANES]
    ad = q[:, 3 * w + LANES:3 * w + 2 * LANES]
    gd = q[:, 3 * w + 2 * LANES:3 * w + 3 * LANES]

    zlin = w0_ref[...] + _dot(jnp.tanh(wd).astype(BF16), wup_ref[...])
    neg = -zlin
    softplus = jnp.maximum(neg, 0.0) + jnp.log(1.0 + jnp.exp(-jnp.abs(neg)))
    decay = jnp.exp(-jnp.exp(-softplus - 0.5))
    a = _sigmoid(a0_ref[...] + _dot(ad.astype(BF16), aup_ref[...]))
    g_o[0] = _dot(_sigmoid(gd).astype(BF16), gup_ref[...])

    ones = ones_ref[...]
    kk = k * kk_s_ref[...]
    ss = _dot_exact(kk * kk, ones)
    kk_o[0] = kk / jnp.maximum(jnp.sqrt(ss), 1e-12)
    v_o[0] = v
    r_o[0] = r
    k_o[0] = k
    ksum = jnp.zeros_like(k)
    for d in range(2):
        a_d = a[:, d * w:(d + 1) * w]
        ksum = ksum + k * (1.0 + (a_d - 1.0) * ka_ref[...])
        w_o[d, 0] = decay[:, d * w:(d + 1) * w]
        a_o[d, 0] = a_d
    bonus_o[0] = _dot_exact(r * ksum * rk_ref[...], ones) * v


def rwkv_features(p, n_cols, width, consts, tm):
    bsz, t, _ = p.shape
    nt = t // tm
    hb = tm // GRID_W
    nhb = t // GRID_W
    w = width
    full = lambda a: pl.BlockSpec(a.shape, lambda b, i: (0,) * a.ndim)
    tok = pl.BlockSpec((1, tm, w), lambda b, i: (b, i, 0))
    tok2 = pl.BlockSpec((2, 1, tm, w), lambda b, i: (0, b, i, 0))
    sds = jax.ShapeDtypeStruct((bsz, t, w), F32)
    sds2 = jax.ShapeDtypeStruct((2, bsz, t, w), F32)
    return pl.pallas_call(
        functools.partial(_rwkv_feat_kernel, nt=nt, width=w),
        grid=(bsz, nt),
        in_specs=[pl.BlockSpec((1, tm, n_cols), lambda b, i: (b, i, 0)),
                  pl.BlockSpec((1, GRID_W, n_cols), lambda b, i: (b, jnp.maximum(i * hb - 1, 0), 0)),
                  pl.BlockSpec((1, GRID_W, n_cols), lambda b, i: (b, jnp.minimum((i + 1) * hb, nhb - 1), 0))]
                 + [full(a) for a in consts],
        out_specs=[tok] * 6 + [tok2, tok2],
        out_shape=[sds] * 6 + [sds2, sds2],
        compiler_params=_cparams(("parallel", "parallel")),
        name="rwkv_features",
    )(p, p, p, *consts)


def _rwkv_scan_kernel(rf, rb, kkf, kkb, kf, kb, wf, wb, af, ab, vf, vb, ka_ref, of_ref, ob_ref,
                      s_ref, vec_ref, v_ref, c_ref, *, tc):
    @pl.when(pl.program_id(0) == 0)
    def _():
        s_ref[...] = jnp.zeros_like(s_ref)

    nj, ni = s_ref.shape[0], s_ref.shape[1]
    quarter = LANES // 4

    def fold(x):
        hi = (lax.broadcasted_iota(jnp.int32, x.shape, 1) % (2 * quarter)) >= quarter
        return x + jnp.where(hi, pltpu.roll(x, quarter, 1), pltpu.roll(x, 3 * quarter, 1))

    def hsum(x):
        return jnp.sum(x, axis=0, keepdims=True)

    def jsum(x):
        return fold(jnp.broadcast_to(hsum(x), (SUBLANES, LANES)))[0:1]

    ka = ka_ref[...]
    fwd_lane = lax.broadcasted_iota(jnp.int32, (nj, LANES), 1) < LANES // 2
    kk_next = jnp.zeros((nj, LANES), F32)
    wr_next = jnp.zeros((nj, LANES), F32)
    for t in reversed(range(tc)):
        merged = lambda f, b: jnp.where(fwd_lane, f[t], b[tc - 1 - t])
        r, kk, k = merged(rf, rb), merged(kkf, kkb), merged(kf, kb)
        w, a = merged(wf, wb), merged(af, ab)
        kka = kk * a
        kd = k * (1.0 + (a - 1.0) * ka)
        wr = w * r
        vec_ref[0, t] = w
        vec_ref[1, t] = kka
        vec_ref[2, t] = kd
        vec_ref[3, t] = w * kk_next
        vec_ref[4, t] = w * wr_next
        c_ref[t, 0:1, :] = jsum(kka * r)
        c_ref[t, 1:2, :] = jsum(kd * r)
        c_ref[t, 2:3, :] = hsum(kka * kk_next)
        c_ref[t, 3:4, :] = hsum(kd * kk_next)
        c_ref[t, 4:5, :] = hsum(kka * wr_next)
        c_ref[t, 5:6, :] = hsum(kd * wr_next)
        v_ref[t] = jnp.concatenate([vf[t], vb[tc - 1 - t]], axis=-1)
        kk_next, wr_next = kk, wr

    def state_sums(y1, y2):
        s1 = jnp.zeros((ni, LANES), F32)
        s2 = jnp.zeros((ni, LANES), F32)
        for j in range(nj):
            sj = s_ref[j]
            s1 = s1 + sj * y1(j)
            s2 = s2 + sj * y2(j)
        return s1, s2

    def step(t, carry):
        row = lambda n: (lambda j: vec_ref[n, t, j:j + 1, :])
        a1, a2 = state_sums(row(3), row(4))
        sa, o1 = fold(carry[0]), fold(carry[1])
        v = v_ref[t]
        out = o1 - sa * c_ref[t, 0:1, :] + v * c_ref[t, 1:2, :]
        of_ref[t] = out
        ob_ref[tc - 1 - t] = out
        nxt = (a1 - sa * c_ref[t, 2:3, :] + v * c_ref[t, 3:4, :],
               a2 - sa * c_ref[t, 4:5, :] + v * c_ref[t, 5:6, :])
        for j in range(nj):
            s_ref[j] = s_ref[j] * row(0)(j) - sa * row(1)(j) + v * row(2)(j)
        return nxt

    first = state_sums(lambda j: kk_next[j:j + 1, :], lambda j: wr_next[j:j + 1, :])
    lax.fori_loop(0, tc, step, first)


def rwkv_scan(r, kk, k, w, a, v, ka, t_ctx):
    ttot, nj, _ = r.shape
    ni, nc = v.shape[1], v.shape[2]
    tc = 32
    ncb, ntb = t_ctx // tc, ttot // tc
    fwd = lambda g: (g, 0, 0)
    bwd = lambda g: (jnp.where(g < ncb, ncb - 1 - g, ntb + ncb - 1 - g), 0, 0)
    jf, jb = pl.BlockSpec((tc, nj, LANES), fwd), pl.BlockSpec((tc, nj, LANES), bwd)
    vf, vb = pl.BlockSpec((tc, ni, nc), fwd), pl.BlockSpec((tc, ni, nc), bwd)
    of, ob = pl.BlockSpec((tc, ni, LANES), fwd), pl.BlockSpec((tc, ni, LANES), bwd)
    osd = jax.ShapeDtypeStruct((ttot, ni, LANES), F32)
    return pl.pallas_call(
        functools.partial(_rwkv_scan_kernel, tc=tc),
        grid=(ntb,),
        in_specs=[jf, jb] * 5 + [vf, vb, pl.BlockSpec(ka.shape, lambda g: (0, 0))],
        out_specs=[of, ob],
        out_shape=[osd, osd],
        scratch_shapes=[pltpu.VMEM((nj, ni, LANES), F32), pltpu.VMEM((5, tc, nj, LANES), F32),
                        pltpu.VMEM((tc, ni, LANES), F32), pltpu.VMEM((tc, SUBLANES, LANES), F32)],
        compiler_params=_cparams(("arbitrary",)),
        name="rwkv_scan",
    )(r, r, kk, kk, k, k, w, w, a, a, v, v, ka)


def _s5_kernel(uf_ref, ub_ref, bm_ref, cm_ref, a_ref, yf_ref, yb_ref, h_ref, bu_ref, ubr_ref, *, tt):
    @pl.when(pl.program_id(0) == 0)
    def _():
        h_ref[...] = jnp.zeros_like(h_ref)

    nsb = bm_ref.shape[0]
    kin = bm_ref.shape[1] // 2
    sw = bm_ref.shape[2]
    hw = sw // 2
    for t in range(tt):
        ubr_ref[t] = ub_ref[tt - 1 - t]
    nothing = jnp.zeros(uf_ref.shape, F32)
    uf = jnp.concatenate([uf_ref[...], nothing], axis=1)
    ub = jnp.concatenate([nothing, ubr_ref[...]], axis=1)
    for sb in range(nsb):
        lhs = jnp.concatenate([uf[:, :, sb * kin:(sb + 1) * kin], ub[:, :, sb * kin:(sb + 1) * kin]], axis=-1)
        lhs = lhs.reshape(tt * SUBLANES, 2 * kin).astype(BF16)
        bu_ref[:, sb * sw:(sb + 1) * sw] = _dot(lhs, bm_ref[sb])

    def step(t, hs):
        rows = pl.ds(pl.multiple_of(t * SUBLANES, SUBLANES), SUBLANES)
        out = []
        for sb in range(nsb):
            hr, hi = hs[2 * sb], hs[2 * sb + 1]
            lo = sb * sw
            ar = a_ref[0, :, lo:lo + hw]
            ai = a_ref[0, :, lo + hw:lo + sw]
            nr = ar * hr - ai * hi + bu_ref[rows, lo:lo + hw]
            ni = ar * hi + ai * hr + bu_ref[rows, lo + hw:lo + sw]
            bu_ref[rows, lo:lo + hw] = nr
            bu_ref[rows, lo + hw:lo + sw] = ni
            out += [nr, ni]
        return tuple(out)

    h0 = []
    for sb in range(nsb):
        h0 += [h_ref[:, sb * sw:sb * sw + hw], h_ref[:, sb * sw + hw:(sb + 1) * sw]]
    hs = lax.fori_loop(0, tt, step, tuple(h0))
    for sb in range(nsb):
        h_ref[:, sb * sw:sb * sw + hw] = hs[2 * sb]
        h_ref[:, sb * sw + hw:(sb + 1) * sw] = hs[2 * sb + 1]

    nout = cm_ref.shape[2] // 2
    fwd_row = (lax.broadcasted_iota(jnp.int32, (tt * SUBLANES, nout), 0) % SUBLANES) < SUBLANES // 2
    ys = []
    for sb in range(nsb):
        yy = _dot(bu_ref[:, sb * sw:(sb + 1) * sw].astype(BF16), cm_ref[sb])
        ys.append(jnp.where(fwd_row, yy[:, :nout], yy[:, nout:]))
    y = jnp.concatenate(ys, axis=1).reshape(tt, SUBLANES, nsb * nout)
    yf_ref[...] = y
    for t in range(tt):
        yb_ref[tt - 1 - t] = y[t]


def s5_scan(u, bm, cm, a, t_ctx):
    ttot, bsz, width = u.shape
    rows = 2 * bsz
    nsb, _, sw = bm.shape
    tt = 64
    ncb, ntb = t_ctx // tt, ttot // tt
    fmap = lambda g: (g, 0, 0)
    bmap = lambda g: (jnp.where(g < ncb, ncb - 1 - g, ntb + ncb - 1 - g), 0, 0)
    full = lambda x: pl.BlockSpec(x.shape, lambda g: (0,) * x.ndim)
    osd = jax.ShapeDtypeStruct((ttot, rows, width), F32)
    return pl.pallas_call(
        functools.partial(_s5_kernel, tt=tt),
        grid=(ntb,),
        in_specs=[pl.BlockSpec((tt, bsz, width), fmap), pl.BlockSpec((tt, bsz, width), bmap),
                  full(bm), full(cm), full(a)],
        out_specs=[pl.BlockSpec((tt, rows, width), fmap), pl.BlockSpec((tt, rows, width), bmap)],
        out_shape=[osd, osd],
        scratch_shapes=[pltpu.VMEM((rows, nsb * sw), F32), pltpu.VMEM((tt * rows, nsb * sw), F32),
                        pltpu.VMEM((tt, bsz, width), F32)],
        compiler_params=_cparams(("arbitrary",)),
        name="s5_scan",
    )(u, u, bm, cm, a)


def _even_out_kernel(x_ref, of_ref, ob_ref, bonus_ref, g_ref, ys_ref, u_ref, m5_ref,
                     ones_ref, gng_ref, gnb_ref, dskip_ref, wglu_ref, bglu_ref, wo1_ref, wo2_ref, o_ref):
    ones = ones_ref[...]
    inv = 1.0 / HEAD_DIM
    o = of_ref[0] + ob_ref[0]
    mean = _dot_exact(o, ones) * inv
    oc = o - mean
    var = _dot_exact(oc * oc, ones) * inv
    y1 = (oc * lax.rsqrt(var + GN_EPS) * gng_ref[...] + gnb_ref[...] + bonus_ref[0]) * g_ref[0]
    y = ys_ref[0] + dskip_ref[...] * u_ref[0]
    y = jax.nn.gelu(y)
    y2 = y * _sigmoid(_dot(y.astype(BF16), wglu_ref[...]) + bglu_ref[...])
    out = _dot(y1.astype(BF16), wo1_ref[...]) + _dot(y2.astype(BF16), wo2_ref[...])
    o_ref[0] = x_ref[0] + m5_ref[0] * out


def even_out(x, off, o_f, o_b, bonus, g, ys, u, m5, consts, tm):
    bsz, t, d = x.shape
    w = o_f.shape[2]
    tokd = pl.BlockSpec((1, tm, d), lambda b, i: (b, i, 0))
    tokw = pl.BlockSpec((1, tm, w), lambda b, i: (b, i + off, 0))
    toks = pl.BlockSpec((1, tm, u.shape[2]), lambda b, i: (b, i + off, 0))
    full = lambda a: pl.BlockSpec(a.shape, lambda b, i: (0,) * a.ndim)
    return pl.pallas_call(
        _even_out_kernel,
        grid=(bsz, t // tm),
        in_specs=[tokd, tokw, tokw, tokw, tokw, toks, toks,
                  pl.BlockSpec((1, 1, d), lambda b, i: (b, 0, 0))] + [full(a) for a in consts],
        out_specs=tokd,
        out_shape=jax.ShapeDtypeStruct((bsz, t, d), F32),
        compiler_params=_cparams(("parallel", "parallel")),
        name="even_out",
    )(x, o_f, o_b, bonus, g, ys, u, m5, *consts)


def _hy_in_kernel(x_ref, xp_ref, xn_ref, gain_ref, shift_ref, scale_ref, w_ref, cw_ref, cb_ref,
                  z_ref, zb_ref, g1_ref, g2_ref, *, nt, c):
    i = pl.program_id(1)
    gain, shift, scale = gain_ref[...], shift_ref[0], scale_ref[0]
    h = _rms_mod(x_ref[0], gain, shift, scale).astype(BF16)
    hp = _rms_mod(xp_ref[0], gain, shift, scale).astype(BF16)
    hn = _rms_mod(xn_ref[0], gain, shift, scale).astype(BF16)
    tm = h.shape[0]
    row = lax.broadcasted_iota(jnp.int32, (tm, c), 0)
    outs = (z_ref, g1_ref, g2_ref)
    for part in range(3):
        wp = w_ref[:, part * c:(part + 1) * c]
        p = _dot(h, wp)
        pp = jnp.where(i > 0, _dot(hp, wp)[7:8], 0.0)
        pn = jnp.where(i < nt - 1, _dot(hn, wp)[0:1], 0.0)
        pm1 = jnp.where(row == 0, pp, pltpu.roll(p, 1, 0))
        pp1 = jnp.where(row == tm - 1, pn, pltpu.roll(p, tm - 1, 0))
        cw = cw_ref[:, part * c:(part + 1) * c]
        q = cw[0:1] * pm1 + cw[1:2] * p + cw[2:3] * pp1 + cb_ref[:, part * c:(part + 1) * c]
        outs[part][0] = q
        if part == 0:
            zb_ref[0] = q.astype(BF16)


def hyena_in(x, gain, shift, scale, w, conv_w, conv_b):
    bsz, t, d = x.shape
    c = w.shape[1] // 3
    tm = min(512, t)
    nt = t // tm
    hb = tm // 8
    nhb = t // 8
    vec = pl.BlockSpec((1, 1, d), lambda b, i: (b, 0, 0))
    tok = pl.BlockSpec((1, tm, c), lambda b, i: (b, i, 0))
    sds = jax.ShapeDtypeStruct((bsz, t, c), F32)
    return pl.pallas_call(
        functools.partial(_hy_in_kernel, nt=nt, c=c),
        grid=(bsz, nt),
        in_specs=[pl.BlockSpec((1, tm, d), lambda b, i: (b, i, 0)),
                  pl.BlockSpec((1, 8, d), lambda b, i: (b, jnp.maximum(i * hb - 1, 0), 0)),
                  pl.BlockSpec((1, 8, d), lambda b, i: (b, jnp.minimum((i + 1) * hb, nhb - 1), 0)),
                  pl.BlockSpec((1, d), lambda b, i: (0, 0)),
                  vec, vec,
                  pl.BlockSpec(w.shape, lambda b, i: (0, 0)),
                  pl.BlockSpec(conv_w.shape, lambda b, i: (0, 0)),
                  pl.BlockSpec((1, 3 * c), lambda b, i: (0, 0))],
        out_specs=[tok, tok, tok, tok],
        out_shape=[sds, jax.ShapeDtypeStruct((bsz, t, c), BF16), sds, sds],
        compiler_params=_cparams(("parallel", "parallel")),
        name="hyena_in",
    )(x, x, x, gain.reshape(1, d), shift, scale, w, conv_w, conv_b.reshape(1, 3 * c))


def _pack_complex(re, im):
    bits = lambda x: lax.bitcast_convert_type(x.astype(BF16).astype(F32), jnp.uint32)
    return lax.shift_right_logical(bits(re), jnp.uint32(16)) | bits(im)


def _unpack_complex(w):
    re = lax.bitcast_convert_type(lax.shift_left(w, jnp.uint32(16)), F32)
    im = lax.bitcast_convert_type(w & jnp.uint32(0xFFFF0000), F32)
    return jnp.concatenate([re, im], axis=0).astype(BF16)


def _fft_conv_kernel(z_ref, kf_ref, w1_ref, f2_ref, g2i_ref, f1c_ref, tw_ref, twt_ref, o_ref, a_ref, t_ref,
                     *, n1, k1c):
    s = pl.program_id(2)
    n2 = FFT_N2
    nd = n1 // 2
    slab = n1 // k1c

    @pl.when(s == 0)
    def _():
        w1 = w1_ref[...]

        def body(j, carry):
            x = jnp.concatenate([z_ref[0, j], z_ref[1, j]], axis=0)
            a = _dot(w1, x)
            a_ref[pl.ds(pl.multiple_of(j * n1, n1), n1), :] = _pack_complex(a[:n1], a[n1:])
            return carry

        lax.fori_loop(0, n2, body, 0, unroll=8)

    @pl.when((s > 0) & (s <= k1c))
    def _():
        f2r, f2i = f2_ref[0], f2_ref[1]
        g2i = g2i_ref[...]
        base = (s - 1) * slab

        def body(kk, carry):
            k1 = base + kk
            twr = tw_ref[0, pl.ds(k1, 1), :]
            twi = tw_ref[1, pl.ds(k1, 1), :]
            gr = f2r * twr - f2i * twi
            gi = f2r * twi + f2i * twr
            gmat = jnp.concatenate([jnp.concatenate([gr, -gi], axis=1),
                                    jnp.concatenate([gi, gr], axis=1)], axis=0).astype(BF16)
            ak = _unpack_complex(a_ref[pl.ds(k1, n2, stride=n1), :])
            x = _dot(gmat, ak)
            xr, xi = x[:n2], x[n2:]
            kr, ki = kf_ref[0, kk], kf_ref[1, kk]
            pr = xr * kr - xi * ki
            pi = xr * ki + xi * kr
            tt = _dot(g2i, jnp.concatenate([pr, pi], axis=0).astype(BF16))
            t_ref[pl.ds(pl.multiple_of(k1 * n2, n2), n2), :] = _pack_complex(tt[:n2], tt[n2:])
            return carry

        lax.fori_loop(0, slab, body, 0, unroll=4)

    @pl.when(s == k1c + 1)
    def _():
        f1r, f1i = f1c_ref[0], f1c_ref[1]

        def body(j, carry):
            twr = twt_ref[0, pl.ds(j, 1), :]
            twi = twt_ref[1, pl.ds(j, 1), :]
            wr = f1r * twr + f1i * twi
            wi = f1i * twr - f1r * twi
            w3 = jnp.concatenate([jnp.concatenate([wr, -wi], axis=1),
                                  jnp.concatenate([wi, wr], axis=1)], axis=0).astype(BF16)
            tn = _unpack_complex(t_ref[pl.ds(j, n1, stride=n2), :])
            y = _dot(w3, tn)
            o_ref[0, j] = y[:nd]
            o_ref[1, j] = y[nd:]
            return carry

        lax.fori_loop(0, n2, body, 0, unroll=8)


def _fft_consts(n1):
    n2 = FFT_N2
    n = n1 * n2
    nd = n1 // 2
    k1 = np.arange(n1)
    f1 = np.exp(-2j * np.pi * np.outer(k1, np.arange(nd)) / n1)
    w1 = np.block([[f1.real, -f1.imag], [f1.imag, f1.real]])
    f2 = np.exp(-2j * np.pi * np.outer(np.arange(n2), np.arange(n2)) / n2)
    f2c = np.conj(f2)
    g2i = np.block([[f2c.real, -f2c.imag], [f2c.imag, f2c.real]])
    f1c = np.exp(2j * np.pi * np.outer(np.arange(nd), k1) / n1) / n
    tw = np.exp(-2j * np.pi * np.outer(k1, np.arange(n2)) / n)
    cplx = lambda m: jnp.asarray(np.stack([m.real, m.imag]), F32)
    return (jnp.asarray(w1, BF16), cplx(f2), jnp.asarray(g2i, BF16), cplx(f1c), cplx(tw), cplx(tw.T))


def fft_conv(zt, kf):
    bsz, n2, nd, c = zt.shape
    n1 = 2 * nd
    cb = LANES
    k1c = 8 if n1 % 8 == 0 and n1 >= 64 else 2
    slab = n1 // k1c
    consts = _fft_consts(n1)
    full = lambda a: pl.BlockSpec(a.shape, lambda j, p, s: (0,) * a.ndim)
    blk = pl.BlockSpec((2, n2, nd, cb), lambda j, p, s: (p, 0, 0, j))
    return pl.pallas_call(
        functools.partial(_fft_conv_kernel, n1=n1, k1c=k1c),
        grid=(c // cb, bsz // 2, k1c + 2),
        in_specs=[blk,
                  pl.BlockSpec((2, slab, n2, cb), lambda j, p, s: (0, jnp.clip(s - 1, 0, k1c - 1), 0, j))]
                 + [full(a) for a in consts],
        out_specs=blk,
        out_shape=jax.ShapeDtypeStruct(zt.shape, F32),
        scratch_shapes=[pltpu.VMEM((n2 * n1, cb), jnp.uint32), pltpu.VMEM((n1 * n2, cb), jnp.uint32)],
        compiler_params=_cparams(("parallel", "parallel", "arbitrary")),
        name="fft_conv",
    )(zt, kf, *consts)


def _fft_fwd_kernel(z_ref, w1_ref, f2_ref, tw_ref, o_ref, a_ref, *, n1, k1c):
    s = pl.program_id(2)
    n2 = FFT_N2
    slab = n1 // k1c

    @pl.when(s == 0)
    def _():
        w1 = w1_ref[...]

        def body(j, carry):
            a = _dot(w1, z_ref[0, j])
            a_ref[pl.ds(pl.multiple_of(j * n1, n1), n1), :] = _pack_complex(a[:n1], a[n1:])
            return carry

        lax.fori_loop(0, n2, body, 0, unroll=8)

    @pl.when(s > 0)
    def _():
        f2r, f2i = f2_ref[0], f2_ref[1]
        base = (s - 1) * slab

        def body(kk, carry):
            k1 = base + kk
            twr = tw_ref[0, pl.ds(k1, 1), :]
            twi = tw_ref[1, pl.ds(k1, 1), :]
            gr = f2r * twr - f2i * twi
            gi = f2r * twi + f2i * twr
            gmat = jnp.concatenate([jnp.concatenate([gr, -gi], axis=1),
                                    jnp.concatenate([gi, gr], axis=1)], axis=0).astype(BF16)
            x = _dot(gmat, _unpack_complex(a_ref[pl.ds(k1, n2, stride=n1), :]))
            o_ref[0, 0, kk] = x[:n2]
            o_ref[0, 1, kk] = x[n2:]
            return carry

        lax.fori_loop(0, slab, body, 0, unroll=4)


def fft_forward(zt):
    rows, n2, nd, c = zt.shape
    n1 = 2 * nd
    cb = LANES
    k1c = 8 if n1 % 8 == 0 and n1 >= 64 else 2
    slab = n1 // k1c
    w1, f2, _, _, tw, _ = _fft_consts(n1)
    w1 = w1[:, :nd]
    full = lambda a: pl.BlockSpec(a.shape, lambda j, r, s: (0,) * a.ndim)
    return pl.pallas_call(
        functools.partial(_fft_fwd_kernel, n1=n1, k1c=k1c),
        grid=(c // cb, rows, k1c + 1),
        in_specs=[pl.BlockSpec((1, n2, nd, cb), lambda j, r, s: (r, 0, 0, j)), full(w1), full(f2), full(tw)],
        out_specs=pl.BlockSpec((1, 2, slab, n2, cb), lambda j, r, s: (r, 0, jnp.maximum(s - 1, 0), 0, j)),
        out_shape=jax.ShapeDtypeStruct((rows, 2, n1, n2, c), F32),
        scratch_shapes=[pltpu.VMEM((n2 * n1, cb), jnp.uint32)],
        compiler_params=_cparams(("parallel", "parallel", "arbitrary")),
        name="fft_forward",
    )(zt, w1, f2, tw)


def _hy_filter_kernel(feat_ref, fw1_ref, fb1_ref, fw2_ref, fb2_ref, fw3_ref, fb3_ref, fw4_ref, freq_ref,
                      delta_ref, z_ref, ss_ref):
    @pl.when(pl.program_id(1) == 0)
    def _():
        ss_ref[...] = jnp.zeros_like(ss_ref)

    feats = feat_ref[...]
    t = feats[:, 0:1]
    fr = freq_ref[...]
    h = jnp.sin(fr * (_dot_exact(feats, fw1_ref[...]) + fb1_ref[...]))
    h = jnp.sin(fr * (_dot_exact(h, fw2_ref[...]) + fb2_ref[...]))
    h = jnp.sin(fr * (_dot_exact(h, fw3_ref[...]) + fb3_ref[...]))
    lag0 = jnp.where(pl.program_id(0) % 2 == 1, 0.0, 1.0)
    filt = _dot(h.astype(BF16), fw4_ref[...]) * (jnp.exp(-t * delta_ref[...]) * jnp.where(t == 0.0, lag0, 1.0))
    z_ref[0] = filt.astype(BF16)
    ss_ref[0] += jnp.sum(filt * filt, axis=0, keepdims=True)


def hyena_filter(t, fw1, fb1, fw2, fb2, fw3, fb3, fw4, freq, width):
    nd = t // FFT_N2
    pos = (jnp.arange(FFT_N2, dtype=F32)[:, None] + FFT_N2 * jnp.arange(nd, dtype=F32)[None, :]).reshape(-1, 1)
    tt = pos / max(t - 1, 1)
    ang = 2 * math.pi * pos / t
    nb = (HY_EMB - 1) // 2
    bands = jnp.linspace(1e-4, nb - 1, nb, dtype=F32)[None]
    feats = jnp.concatenate([tt, jnp.cos(bands * ang), -jnp.sin(bands * ang),
                             jnp.zeros((t, LANES - HY_EMB), F32)], axis=-1)
    deltas = jnp.abs(jnp.linspace(HY_MIN_DECAY, HY_MAX_DECAY, width, dtype=F32)).reshape(1, width)
    rows = 2 * HY_ORDER
    tm = min(512, t)
    hid = fw1.shape[1]
    full = lambda a: pl.BlockSpec(a.shape, lambda r, i: (0,) * a.ndim)
    fw1p = jnp.concatenate([fw1, jnp.zeros((LANES - HY_EMB, hid), F32)], axis=0)
    vecs = [fw1p, fb1.reshape(1, hid), fw2, fb2.reshape(1, hid), fw3, fb3.reshape(1, hid)]
    return pl.pallas_call(
        _hy_filter_kernel,
        grid=(rows, t // tm),
        in_specs=[pl.BlockSpec((tm, LANES), lambda r, i: (i, 0))] + [full(a) for a in vecs]
                 + [pl.BlockSpec((hid, width), lambda r, i: (0, r)), full(freq.reshape(1, hid)), full(deltas)],
        out_specs=[pl.BlockSpec((1, tm, width), lambda r, i: (r, i, 0)),
                   pl.BlockSpec((1, 1, width), lambda r, i: (r, 0, 0))],
        out_shape=[jax.ShapeDtypeStruct((rows, t, width), BF16), jax.ShapeDtypeStruct((rows, 1, width), F32)],
        compiler_params=_cparams(("parallel", "arbitrary")),
        name="hyena_filter",
    )(feats, *vecs, fw4.astype(BF16), freq.reshape(1, hid), deltas)


def hyena_filter_spectrum_fft(t, fw1, fb1, fw2, fb2, fw3, fb3, fw4, freq, width):
    taps, ss = hyena_filter(t, fw1, fb1, fw2, fb2, fw3, fb3, fw4, freq, width)
    nd = t // FFT_N2
    spec = fft_forward(taps.reshape(2 * HY_ORDER, FFT_N2, nd, width))
    spec = spec.reshape(HY_ORDER, 2, 2, 2 * nd, FFT_N2, width)
    scale = lax.rsqrt(ss.reshape(HY_ORDER, 2, width).sum(axis=1) + 1e-6)[:, None, None, :]
    return jnp.stack([(spec[:, 0, 0] + spec[:, 1, 0]) * scale, (spec[:, 0, 1] - spec[:, 1, 1]) * scale], axis=1)


def _dft_conv_kernel(z_ref, kf_ref, fw_ref, iv_ref, o_ref, *, t):
    x = jnp.concatenate([z_ref[0], z_ref[1]], axis=0)
    spec = _dot(fw_ref[...], x)
    xr, xi = spec[:2 * t], spec[2 * t:]
    kr, ki = kf_ref[0], kf_ref[1]
    pr = xr * kr - xi * ki
    pi = xr * ki + xi * kr
    y = _dot(iv_ref[...], jnp.concatenate([pr, pi], axis=0).astype(BF16))
    o_ref[0] = y[:t]
    o_ref[1] = y[t:]


def dft_conv(zb, kf):
    bsz, t, c = zb.shape
    n = 2 * t
    f = np.exp(-2j * np.pi * np.outer(np.arange(n), np.arange(t)) / n)
    fw = np.block([[f.real, -f.imag], [f.imag, f.real]])
    fi = np.exp(2j * np.pi * np.outer(np.arange(t), np.arange(n)) / n) / n
    iv = np.block([[fi.real, -fi.imag], [fi.imag, fi.real]])
    fw, iv = jnp.asarray(fw, BF16), jnp.asarray(iv, BF16)
    cb = LANES
    blk = pl.BlockSpec((2, t, cb), lambda j, p: (p, 0, j))
    return pl.pallas_call(
        functools.partial(_dft_conv_kernel, t=t),
        grid=(c // cb, bsz // 2),
        in_specs=[blk, pl.BlockSpec((2, n, cb), lambda j, p: (0, 0, j)),
                  pl.BlockSpec(fw.shape, lambda j, p: (0, 0)), pl.BlockSpec(iv.shape, lambda j, p: (0, 0))],
        out_specs=blk,
        out_shape=jax.ShapeDtypeStruct(zb.shape, F32),
        compiler_params=_cparams(("parallel", "parallel")),
        name="dft_conv",
    )(zb, kf, fw, iv)


def _conv_rows(conv_ref):
    if len(conv_ref.shape) == 3:
        return conv_ref[0]
    return jnp.concatenate([conv_ref[0, :, k, :] for k in range(conv_ref.shape[2])], axis=0)


def _conv_spec(conv, tm, c):
    if conv.ndim == 3:
        return pl.BlockSpec((1, tm, c), lambda b, i: (b, i, 0))
    return pl.BlockSpec((1, FFT_N2, tm // FFT_N2, c), lambda b, i: (b, 0, i, 0))


def _hy_gate_kernel(conv_ref, z_ref, gate_ref, bias_ref, o_ref, ob_ref):
    y = gate_ref[0] * (_conv_rows(conv_ref) + bias_ref[...] * z_ref[0])
    o_ref[0] = y
    ob_ref[0] = y.astype(BF16)


def hyena_gate(conv, z, gate, bias):
    bsz, t, c = z.shape
    tm = min(SUBLANES * FFT_N2, t)
    tok = pl.BlockSpec((1, tm, c), lambda b, i: (b, i, 0))
    return pl.pallas_call(
        _hy_gate_kernel,
        grid=(bsz, t // tm),
        in_specs=[_conv_spec(conv, tm, c), tok, tok, pl.BlockSpec((1, c), lambda b, i: (0, 0))],
        out_specs=[tok, tok],
        out_shape=[jax.ShapeDtypeStruct(z.shape, F32), jax.ShapeDtypeStruct(z.shape, BF16)],
        compiler_params=_cparams(("parallel", "parallel")),
        name="hyena_gate",
    )(conv, z, gate, bias.reshape(1, c))


def _hy_out_kernel(x_ref, conv_ref, z_ref, gate_ref, bias_ref, w_ref, m5_ref, o_ref):
    y = gate_ref[0] * (_conv_rows(conv_ref) + bias_ref[...] * z_ref[0])
    o_ref[0] = x_ref[0] + m5_ref[0] * _dot(y.astype(BF16), w_ref[...])


def hyena_out(x, conv, z, gate, bias, w, m5):
    bsz, t, d = x.shape
    c = z.shape[2]
    tm = min(SUBLANES * FFT_N2, t)
    tokd = pl.BlockSpec((1, tm, d), lambda b, i: (b, i, 0))
    tokc = pl.BlockSpec((1, tm, c), lambda b, i: (b, i, 0))
    return pl.pallas_call(
        _hy_out_kernel,
        grid=(bsz, t // tm),
        in_specs=[tokd, _conv_spec(conv, tm, c), tokc, tokc, pl.BlockSpec((1, c), lambda b, i: (0, 0)),
                  pl.BlockSpec(w.shape, lambda b, i: (0, 0)), pl.BlockSpec((1, 1, d), lambda b, i: (b, 0, 0))],
        out_specs=tokd,
        out_shape=jax.ShapeDtypeStruct(x.shape, F32),
        compiler_params=_cparams(("parallel", "parallel")),
        name="hyena_out",
    )(x, conv, z, gate, bias.reshape(1, c), w, m5)


def _hyena_filter_spectrum(t, fw1, fb1, fw2, fb2, fw3, fb3, fw4, freq, width):
    pos = jnp.arange(t, dtype=F32)[:, None]
    tt = pos / max(t - 1, 1)
    ang = 2 * math.pi * pos / t
    nb = (HY_EMB - 1) // 2
    bands = jnp.linspace(1e-4, nb - 1, nb, dtype=F32)[None]
    feats = jnp.concatenate([tt, jnp.cos(bands * ang), -jnp.sin(bands * ang)], axis=-1)
    hdn = jnp.sin(freq * (feats @ fw1 + fb1))
    hdn = jnp.sin(freq * (hdn @ fw2 + fb2))
    hdn = jnp.sin(freq * (hdn @ fw3 + fb3))
    filt = (hdn @ fw4).reshape(t, HY_ORDER, 2, width)
    deltas = jnp.abs(jnp.linspace(HY_MIN_DECAY, HY_MAX_DECAY, width, dtype=F32))
    filt = filt * jnp.exp(-tt[:, :, None, None] * deltas)
    fwd, bwd = filt[:, :, 0], filt[:, :, 1]
    kern = jnp.concatenate([fwd, jnp.zeros_like(fwd[:1]), bwd[:0:-1]], axis=0)
    kern = kern * lax.rsqrt(jnp.sum(kern * kern, axis=0, keepdims=True) + 1e-6)
    spec = jnp.fft.fft(kern, axis=0)
    return jnp.stack([jnp.real(spec), jnp.imag(spec)], axis=1).transpose(2, 1, 0, 3).astype(F32)


def _block_diag_pair(m):
    z = jnp.zeros_like(m[0])
    return jnp.concatenate([jnp.concatenate([m[0], z], axis=1), jnp.concatenate([z, m[1]], axis=1)], axis=0)


def _even_mixer(lat, cx, ml, mc, gain, prm, ctx_out):
    (w_in, mu, w0, w_up, a0, a_up, g_up, k_k, k_a, r_k, gn_g, gn_b,
     lam_re, lam_im, log_dt, b_re, b_im, c_re, c_im, d_skip, w_glu, b_glu, w_out) = prm
    bsz, t_lat, d = lat.shape
    t_ctx = cx.shape[1]
    width = k_k.shape[0]
    heads = width // HEAD_DIM
    n_cols = mu.shape[0]
    s5w = d_skip.shape[0]
    ttot = t_ctx + t_lat
    nch = bsz * heads
    tm = t_ctx
    assert tm % GRID_W == 0 and t_lat % tm == 0 and 4 * nch == LANES and 2 * bsz == SUBLANES

    p_all, u_all = mod_matmul_stream(cx, lat, gain, mc, ml, w_in.astype(BF16), tm, n_cols)

    head_of = np.arange(width) // HEAD_DIM
    ones = jnp.asarray(head_of[:, None] == head_of[None, :], F32)
    feat_consts = (mu.reshape(1, -1), _block_diag_pair(w_up).astype(BF16), w0.reshape(1, -1),
                   _block_diag_pair(a_up).astype(BF16), a0.reshape(1, -1), g_up.astype(BF16),
                   k_k.reshape(1, -1), k_a.reshape(1, -1), r_k.reshape(1, -1), ones)
    kk, v, g, bonus, r, k, w2, a2 = rwkv_features(p_all, n_cols, width, feat_consts, tm)

    def key_major(x):
        x = x.reshape(bsz, ttot, heads, 2, HALF_HEAD).transpose(1, 4, 3, 0, 2)[:, :, None]
        return jnp.broadcast_to(x, (ttot, HALF_HEAD, 2, 2, bsz, heads)).reshape(ttot, HALF_HEAD, LANES)

    def key_major2(x):
        x = x.reshape(2, bsz, ttot, heads, 2, HALF_HEAD)
        return x.transpose(2, 5, 0, 4, 1, 3).reshape(ttot, HALF_HEAD, LANES)

    def value_major(x):
        x = x.reshape(bsz, ttot, heads, HEAD_DIM).transpose(1, 3, 0, 2).reshape(ttot, HEAD_DIM, nch)
        return jnp.concatenate([x, x], axis=-1)

    ka_t = k_a.reshape(heads, 2, HALF_HEAD).transpose(2, 1, 0)[:, None, :, None, :]
    ka_t = jnp.broadcast_to(ka_t, (HALF_HEAD, 2, 2, bsz, heads)).reshape(HALF_HEAD, LANES)
    o_f, o_b = rwkv_scan(key_major(r), key_major(kk), key_major(k), key_major2(w2), key_major2(a2),
                         value_major(v), ka_t, t_ctx)

    def token_major(o, lane0):
        o = o[..., lane0:lane0 + nch].reshape(ttot, HEAD_DIM, bsz, heads)
        return o.transpose(2, 0, 3, 1).reshape(bsz, ttot, width)

    o_f, o_b = token_major(o_f, 0), token_major(o_b, LANES // 2)

    ng = lam_re.shape[1]
    gps = ng // S5_SUPER
    lam = lax.complex(lam_re, lam_im)
    dt = jnp.exp(log_dt)[..., None]
    a_bar = jnp.exp(lam * dt)
    b_bar = ((a_bar - 1) / lam)[..., None] * lax.complex(b_re, b_im)
    eye = jnp.eye(gps, dtype=F32)

    def b_mat(x):
        x = x.reshape(2, S5_SUPER, gps, S5_STATE, S5_GROUP)
        m = jnp.einsum('dsgph,gk->sdghkp', x, eye)
        return m.reshape(S5_SUPER, 2 * gps * S5_GROUP, gps * S5_STATE)

    def c_mat(x):
        x = x.reshape(2, S5_SUPER, gps, S5_GROUP, S5_STATE)
        m = jnp.einsum('dsghp,gk->skpdgh', x, eye)
        return m.reshape(S5_SUPER, gps * S5_STATE, 2 * gps * S5_GROUP)

    bm = jnp.concatenate([b_mat(jnp.real(b_bar)), b_mat(jnp.imag(b_bar))], axis=2).astype(BF16)
    cm = jnp.concatenate([c_mat(c_re), -c_mat(c_im)], axis=1).astype(BF16)

    def a_rows(x):
        x = x.reshape(2, 1, S5_SUPER, gps * S5_STATE)
        return jnp.broadcast_to(x, (2, bsz, S5_SUPER, gps * S5_STATE)).reshape(2 * bsz, S5_SUPER, -1)

    a_arr = jnp.concatenate([a_rows(jnp.real(a_bar)), a_rows(jnp.imag(a_bar))], axis=2)
    a_arr = a_arr.reshape(1, 2 * bsz, -1)

    y_f, y_b = s5_scan(u_all.transpose(1, 0, 2), bm, cm, a_arr, t_ctx)
    ys = (y_f[:, :bsz] + y_b[:, bsz:]).transpose(1, 0, 2)

    wo = w_out.astype(BF16)
    out_consts = (ones, gn_g.reshape(1, -1), gn_b.reshape(1, -1), d_skip.reshape(1, -1),
                  w_glu.astype(BF16), b_glu.reshape(1, -1), wo[:width], wo[width:])
    lat = even_out(lat, 1, o_f, o_b, bonus, g, ys, u_all, ml[5], out_consts, tm)
    if ctx_out:
        cx = even_out(cx, 0, o_f, o_b, bonus, g, ys, u_all, mc[5], out_consts, tm)
    return lat, cx


def _hyena_mixer(x, gain, m, prm):
    (w_in, conv_w, conv_b, fw1, fb1, fw2, fb2, fw3, fb3, fw4, freq, bias_d, w_out) = prm
    bsz, t, d = x.shape
    c = w_out.shape[0]
    use_fft = t % (FFT_N2 * 2) == 0 and t >= 4 * FFT_N2
    if use_fft:
        kf = hyena_filter_spectrum_fft(t, fw1, fb1, fw2, fb2, fw3, fb3, fw4, freq, c)
    else:
        kf = _hyena_filter_spectrum(t, fw1, fb1, fw2, fb2, fw3, fb3, fw4, freq, c)
    z, zb, g1, g2 = hyena_in(x, gain, m[3], m[4], w_in.astype(BF16), conv_w, conv_b)
    gates = (g1, g2)
    for n in range(HY_ORDER):
        if use_fft:
            nd = t // FFT_N2
            zt = zb.reshape(bsz, nd, FFT_N2, c).transpose(0, 2, 1, 3)
            conv = fft_conv(zt, kf[n])
        else:
            conv = dft_conv(zb, kf[n])
        if n < HY_ORDER - 1:
            z, zb = hyena_gate(conv, z, gates[n], bias_d[n])
        else:
            return hyena_out(x, conv, z, gates[n], bias_d[n], w_out.astype(BF16), m[5])


def kernel(x, c, ctx, c_ctx, norm_g, ada_w, ada_b, ffn_wg, ffn_wu, ffn_wd, final_g, ev_w_in, ev_mu, ev_w0, ev_w_up, ev_a0, ev_a_up, ev_g_up, ev_k_k, ev_k_a, ev_r_k, ev_gn_g, ev_gn_b, ev_lam_re, ev_lam_im, ev_log_dt, ev_b_re, ev_b_im, ev_c_re, ev_c_im, ev_d, ev_w_glu, ev_b_glu, ev_w_out, od_w_in, od_conv_w, od_conv_b, od_fw1, od_fb1, od_fw2, od_fb2, od_fw3, od_fb3, od_fw4, od_freq, od_bias, od_w_out):
    depth = norm_g.shape[0]
    bsz, _, d = x.shape
    n_even = (depth + 1) // 2
    last_ctx = 2 * (n_even - 1)

    cond8 = jnp.concatenate([c, c_ctx[None], jnp.zeros((8 - bsz - 1, d), F32)], axis=0)
    mods = ada_mods_all(cond8, ada_w, ada_b)

    wg, wu, wd = ffn_wg.astype(BF16), ffn_wu.astype(BF16), ffn_wd.astype(BF16)
    lat, cx = x, ctx
    for l in range(depth):
        run_ctx = l <= last_ctx
        ctx_out = l < last_ctx
        i = l // 2
        ml = [mods[l, :bsz, None, k * d:(k + 1) * d] for k in range(N_MOD)]
        mc = [jnp.broadcast_to(mods[l, bsz:bsz + 1, None, k * d:(k + 1) * d], (bsz, 1, d))
              for k in range(N_MOD)]
        lat = ffn_half(lat, norm_g[l, 0], ml[0], ml[1], ml[2], wg[l, 0], wu[l, 0], wd[l, 0])
        if run_ctx:
            cx = ffn_half(cx, norm_g[l, 0], mc[0], mc[1], mc[2], wg[l, 0], wu[l, 0], wd[l, 0])
        if l % 2 == 0:
            prm = (ev_w_in[i], ev_mu[i], ev_w0[i], ev_w_up[i], ev_a0[i], ev_a_up[i], ev_g_up[i],
                   ev_k_k[i], ev_k_a[i], ev_r_k[i], ev_gn_g[i], ev_gn_b[i],
                   ev_lam_re[i], ev_lam_im[i], ev_log_dt[i], ev_b_re[i], ev_b_im[i], ev_c_re[i], ev_c_im[i],
                   ev_d[i], ev_w_glu[i], ev_b_glu[i], ev_w_out[i])
            lat, cx = _even_mixer(lat, cx, ml, mc, norm_g[l, 1], prm, ctx_out)
        else:
            prm = (od_w_in[i], od_conv_w[i], od_conv_b[i], od_fw1[i], od_fb1[i], od_fw2[i], od_fb2[i],
                   od_fw3[i], od_fb3[i], od_fw4[i], od_freq[i], od_bias[i], od_w_out[i])
            lat = _hyena_mixer(lat, norm_g[l, 1], ml, prm)
            if ctx_out:
                cx = _hyena_mixer(cx, norm_g[l, 1], mc, prm)
        fin = final_g if l == depth - 1 else None
        lat = ffn_half(lat, norm_g[l, 2], ml[6], ml[7], ml[8], wg[l, 1], wu[l, 1], wd[l, 1], fin)
        if ctx_out:
            cx = ffn_half(cx, norm_g[l, 2], mc[6], mc[7], mc[8], wg[l, 1], wu[l, 1], wd[l, 1])
    return lat
```

```python
import functools
import math

import numpy as np
import jax
import jax.numpy as jnp
from jax import lax
from jax.experimental import pallas as pl
from jax.experimental.pallas import tpu as pltpu

F32 = jnp.float32
BF16 = jnp.bfloat16
HIGHEST = lax.Precision.HIGHEST

N_MOD = 9
NORM_EPS = 1e-6
GN_EPS = 64e-5
GRID_W = 64
HEAD_DIM = 64
HALF_HEAD = HEAD_DIM // 2
S5_GROUP = 16
S5_STATE = 64
S5_SUPER = 4
HY_ORDER = 2
HY_EMB = 33
HY_MIN_DECAY = math.log(1e-2) / 1.5
HY_MAX_DECAY = math.log(1e-2) / 0.3
LANES = 128
SUBLANES = 8
MXU_DIM = 256
FFT_N2 = 128
VMEM_LIMIT = 56 * 1024 * 1024


def _cparams(sem, vmem=VMEM_LIMIT):
    return pltpu.CompilerParams(dimension_semantics=sem, vmem_limit_bytes=vmem)


def _dot(a, b):
    return jnp.dot(a, b, preferred_element_type=F32)


def _dot_exact(a, b):
    return jnp.dot(a, b, preferred_element_type=F32, precision=HIGHEST)


def _rms_mod(x, gain, shift, scale):
    ms = jnp.mean(x * x, axis=-1, keepdims=True)
    return x * lax.rsqrt(ms + NORM_EPS) * gain * (1.0 + scale) + shift


def _sigmoid(x):
    return 1.0 / (1.0 + jnp.exp(-x))


def _silu(x):
    return x * _sigmoid(x)


def _ada_kernel(c_ref, w_ref, b_ref, o_ref):
    s = _silu(c_ref[...])
    o_ref[0] = _dot(s.astype(BF16), w_ref[0].astype(BF16)) + b_ref[0]


def ada_mods_all(cond8, ada_w, ada_b):
    depth, d, n = ada_w.shape
    tn = n // 8
    return pl.pallas_call(
        _ada_kernel,
        grid=(depth, n // tn),
        in_specs=[pl.BlockSpec((8, d), lambda l, j: (0, 0)),
                  pl.BlockSpec((1, d, tn), lambda l, j: (l, 0, j)),
                  pl.BlockSpec((1, 1, tn), lambda l, j: (l, 0, j))],
        out_specs=pl.BlockSpec((1, 8, tn), lambda l, j: (l, 0, j)),
        out_shape=jax.ShapeDtypeStruct((depth, 8, n), F32),
        compiler_params=_cparams(("parallel", "parallel")),
        name="ada_mods",
    )(cond8, ada_w, ada_b.reshape(depth, 1, n))


def _ffn_kernel(x_ref, gain_ref, shift_ref, scale_ref, gate_ref, wg_ref, wu_ref, wd_ref, fg_ref,
                o_ref, *, final_norm, chunks):
    x = x_ref[0]
    h = _rms_mod(x, gain_ref[...], shift_ref[0], scale_ref[0]).astype(BF16)
    acc = None
    for lo, hi in chunks:
        g = _dot(h, wg_ref[:, lo:hi])
        u = _dot(h, wu_ref[:, lo:hi])
        part = _dot((_silu(g) * u).astype(BF16), wd_ref[lo:hi, :])
        acc = part if acc is None else acc + part
    y = x + 0.5 * gate_ref[0] * acc
    if final_norm:
        ms = jnp.mean(y * y, axis=-1, keepdims=True)
        y = y * lax.rsqrt(ms + NORM_EPS) * fg_ref[...]
    o_ref[0] = y


def ffn_half(x, gain, shift, scale, gate, wg, wu, wd, final_g=None):
    bsz, t, d = x.shape
    ff = wg.shape[1]
    tm = min(512, t)
    step = 4 * MXU_DIM
    chunks = tuple((lo, min(lo + step, ff)) for lo in range(0, ff, step))
    fg = jnp.ones((1, d), F32) if final_g is None else final_g.reshape(1, d)
    vec = pl.BlockSpec((1, 1, d), lambda b, i: (b, 0, 0))
    resident = lambda a: pl.BlockSpec(a.shape, lambda b, i: (0, 0), pipeline_mode=pl.Buffered(1))
    return pl.pallas_call(
        functools.partial(_ffn_kernel, final_norm=final_g is not None, chunks=chunks),
        grid=(bsz, t // tm),
        in_specs=[pl.BlockSpec((1, tm, d), lambda b, i: (b, i, 0)),
                  pl.BlockSpec((1, d), lambda b, i: (0, 0)),
                  vec, vec, vec, resident(wg), resident(wu), resident(wd),
                  pl.BlockSpec((1, d), lambda b, i: (0, 0))],
        out_specs=pl.BlockSpec((1, tm, d), lambda b, i: (b, i, 0)),
        out_shape=jax.ShapeDtypeStruct((bsz, t, d), F32),
        compiler_params=_cparams(("parallel", "parallel")),
        name="ffn_half",
    )(x, gain.reshape(1, d), shift, scale, gate, wg, wu, wd, fg)


def _modmm_kernel(cx_ref, lat_ref, gain_ref, shc_ref, scc_ref, shl_ref, scl_ref, w_ref, o_ref, u_ref, *, nc):
    is_ctx = pl.program_id(1) < nc
    x = jnp.where(is_ctx, cx_ref[0], lat_ref[0])
    shift = jnp.where(is_ctx, shc_ref[0], shl_ref[0])
    scale = jnp.where(is_ctx, scc_ref[0], scl_ref[0])
    h = _rms_mod(x, gain_ref[...], shift, scale).astype(BF16)
    n1 = o_ref.shape[2]
    o_ref[0] = _dot(h, w_ref[:, :n1])
    u_ref[0] = _dot(h, w_ref[:, n1:])


def mod_matmul_stream(cx, lat, gain, mc, ml, w, tm, n_first):
    bsz, t_ctx, d = cx.shape
    t_lat = lat.shape[1]
    n = w.shape[1]
    nc = t_ctx // tm
    nt = nc + t_lat // tm
    vec = pl.BlockSpec((1, 1, d), lambda b, i: (b, 0, 0))
    return pl.pallas_call(
        functools.partial(_modmm_kernel, nc=nc),
        grid=(bsz, nt),
        in_specs=[pl.BlockSpec((1, tm, d), lambda b, i: (b, jnp.minimum(i, nc - 1), 0)),
                  pl.BlockSpec((1, tm, d), lambda b, i: (b, jnp.maximum(i - nc, 0), 0)),
                  pl.BlockSpec((1, d), lambda b, i: (0, 0)),
                  vec, vec, vec, vec,
                  pl.BlockSpec((d, n), lambda b, i: (0, 0))],
        out_specs=[pl.BlockSpec((1, tm, n_first), lambda b, i: (b, i, 0)),
                   pl.BlockSpec((1, tm, n - n_first), lambda b, i: (b, i, 0))],
        out_shape=[jax.ShapeDtypeStruct((bsz, t_ctx + t_lat, n_first), F32),
                   jax.ShapeDtypeStruct((bsz, t_ctx + t_lat, n - n_first), F32)],
        compiler_params=_cparams(("parallel", "parallel")),
        name="mod_matmul",
    )(cx, lat, gain.reshape(1, d), mc[3], mc[4], ml[3], ml[4], w)


def _rwkv_feat_kernel(p_ref, pu_ref, pd_ref, mu_ref, wup_ref, w0_ref, aup_ref, a0_ref, gup_ref,
                      kk_s_ref, ka_ref, rk_ref, ones_ref,
                      kk_o, v_o, g_o, bonus_o, r_o, k_o, w_o, a_o, *, nt, width):
    i = pl.program_id(1)
    is_ctx = i == 0
    p = p_ref[0]
    tm = p.shape[0]
    row = lax.broadcasted_iota(jnp.int32, p.shape, 0)
    lane = lax.broadcasted_iota(jnp.int32, p.shape, 1) % 4
    prev = pltpu.roll(p, 1, 0)
    nxt = pltpu.roll(p, tm - 1, 0)
    col = jnp.where(is_ctx, row, row % GRID_W)
    last = jnp.where(is_ctx, tm - 1, GRID_W - 1)
    left = jnp.where(col == 0, 0.0, prev)
    right = jnp.where(col == last, 0.0, nxt)
    up_halo = jnp.where(i > 1, pu_ref[0], 0.0)
    dn_halo = jnp.where(i < nt - 1, pd_ref[0], 0.0)
    up = jnp.where(is_ctx, left, jnp.concatenate([up_halo, p[:tm - GRID_W]], axis=0))
    down = jnp.where(is_ctx, right, jnp.concatenate([p[GRID_W:], dn_halo], axis=0))
    shifted = jnp.where(lane == 0, left, jnp.where(lane == 1, right, jnp.where(lane == 2, up, down)))
    q = p + mu_ref[...] * (shifted - p)

    w = width
    r, k, v = q[:, :w], q[:, w:2 * w], q[:, 2 * w:3 * w]
    wd = q[:, 3 * w:3 * w + LANES]
    ad = q[:, 3 * w + LANES:3 * w + 2 * LANES]
    gd = q[:, 3 * w + 2 * LANES:3 * w + 3 * LANES]

    zlin = w0_ref[...] + _dot(jnp.tanh(wd).astype(BF16), wup_ref[...])
    neg = -zlin
    softplus = jnp.maximum(neg, 0.0) + jnp.log(1.0 + jnp.exp(-jnp.abs(neg)))
    decay = jnp.exp(-jnp.exp(-softplus - 0.5))
    a = _sigmoid(a0_ref[...] + _dot(ad.astype(BF16), aup_ref[...]))
    g_o[0] = _dot(_sigmoid(gd).astype(BF16), gup_ref[...])

    ones = ones_ref[...]
    kk = k * kk_s_ref[...]
    ss = _dot_exact(kk * kk, ones)
    kk = kk / jnp.maximum(jnp.sqrt(ss), 1e-12)
    for d in range(2):
        kk_o[d, 0] = kk
        v_o[d, 0] = v
        r_o[d, 0] = r
        k_o[d, 0] = k
    ksum = jnp.zeros_like(k)
    for d in range(2):
        a_d = a[:, d * w:(d + 1) * w]
        ksum = ksum + k * (1.0 + (a_d - 1.0) * ka_ref[...])
        w_o[d, 0] = decay[:, d * w:(d + 1) * w]
        a_o[d, 0] = a_d
    bonus_o[0] = _dot_exact(r * ksum * rk_ref[...], ones) * v


def rwkv_features(p, n_cols, width, consts, tm):
    bsz, t, _ = p.shape
    nt = t // tm
    hb = tm // GRID_W
    nhb = t // GRID_W
    w = width
    full = lambda a: pl.BlockSpec(a.shape, lambda b, i: (0,) * a.ndim)
    tok = pl.BlockSpec((1, tm, w), lambda b, i: (b, i, 0))
    tok2 = pl.BlockSpec((2, 1, tm, w), lambda b, i: (0, b, i, 0))
    sds = jax.ShapeDtypeStruct((bsz, t, w), F32)
    sds2 = jax.ShapeDtypeStruct((2, bsz, t, w), F32)
    return pl.pallas_call(
        functools.partial(_rwkv_feat_kernel, nt=nt, width=w),
        grid=(bsz, nt),
        in_specs=[pl.BlockSpec((1, tm, n_cols), lambda b, i: (b, i, 0)),
                  pl.BlockSpec((1, GRID_W, n_cols), lambda b, i: (b, jnp.maximum(i * hb - 1, 0), 0)),
                  pl.BlockSpec((1, GRID_W, n_cols), lambda b, i: (b, jnp.minimum((i + 1) * hb, nhb - 1), 0))]
                 + [full(a) for a in consts],
        out_specs=[tok2, tok2, tok, tok, tok2, tok2, tok2, tok2],
        out_shape=[sds2, sds2, sds, sds, sds2, sds2, sds2, sds2],
        compiler_params=_cparams(("parallel", "parallel")),
        name="rwkv_features",
    )(p, p, p, *consts)


def _rwkv_scan_kernel(rf, rb, kkf, kkb, kf, kb, wf, wb, af, ab, vf, vb, ka_ref, of_ref, ob_ref,
                      s_ref, vec_ref, v_ref, c_ref, *, tc):
    @pl.when(pl.program_id(0) == 0)
    def _():
        s_ref[...] = jnp.zeros_like(s_ref)

    nj, ni = s_ref.shape[0], s_ref.shape[1]
    quarter = LANES // 4

    def fold(x):
        hi = (lax.broadcasted_iota(jnp.int32, x.shape, 1) % (2 * quarter)) >= quarter
        return x + jnp.where(hi, pltpu.roll(x, quarter, 1), pltpu.roll(x, 3 * quarter, 1))

    def hsum(x):
        return jnp.sum(x, axis=0, keepdims=True)

    def jsum(x):
        return fold(jnp.broadcast_to(hsum(x), (SUBLANES, LANES)))[0:1]

    ka = ka_ref[...]
    fwd_lane = lax.broadcasted_iota(jnp.int32, (nj, LANES), 1) < LANES // 2
    kk_next = jnp.zeros((nj, LANES), F32)
    wr_next = jnp.zeros((nj, LANES), F32)
    for t in reversed(range(tc)):
        merged = lambda f, b: jnp.where(fwd_lane, f[t], b[tc - 1 - t])
        r, kk, k = merged(rf, rb), merged(kkf, kkb), merged(kf, kb)
        w, a = merged(wf, wb), merged(af, ab)
        kka = kk * a
        kd = k * (1.0 + (a - 1.0) * ka)
        wr = w * r
        vec_ref[0, t] = w
        vec_ref[1, t] = kka
        vec_ref[2, t] = kd
        vec_ref[3, t] = w * kk_next
        vec_ref[4, t] = w * wr_next
        c_ref[t, 0:1, :] = jsum(kka * r)
        c_ref[t, 1:2, :] = jsum(kd * r)
        c_ref[t, 2:3, :] = hsum(kka * kk_next)
        c_ref[t, 3:4, :] = hsum(kd * kk_next)
        c_ref[t, 4:5, :] = hsum(kka * wr_next)
        c_ref[t, 5:6, :] = hsum(kd * wr_next)
        v_ref[t] = jnp.concatenate([vf[t], vb[tc - 1 - t]], axis=-1)
        kk_next, wr_next = kk, wr

    def state_sums(y1, y2):
        s1 = jnp.zeros((ni, LANES), F32)
        s2 = jnp.zeros((ni, LANES), F32)
        for j in range(nj):
            sj = s_ref[j]
            s1 = s1 + sj * y1(j)
            s2 = s2 + sj * y2(j)
        return s1, s2

    def step(t, carry):
        row = lambda n: (lambda j: vec_ref[n, t, j:j + 1, :])
        a1, a2 = state_sums(row(3), row(4))
        sa, o1 = fold(carry[0]), fold(carry[1])
        v = v_ref[t]
        out = o1 - sa * c_ref[t, 0:1, :] + v * c_ref[t, 1:2, :]
        of_ref[t] = out
        ob_ref[tc - 1 - t] = out
        nxt = (a1 - sa * c_ref[t, 2:3, :] + v * c_ref[t, 3:4, :],
               a2 - sa * c_ref[t, 4:5, :] + v * c_ref[t, 5:6, :])
        for j in range(nj):
            s_ref[j] = s_ref[j] * row(0)(j) - sa * row(1)(j) + v * row(2)(j)
        return nxt

    first = state_sums(lambda j: kk_next[j:j + 1, :], lambda j: wr_next[j:j + 1, :])
    lax.fori_loop(0, tc, step, first)


def rwkv_scan(r, kk, k, w, a, v, ka, t_ctx):
    ttot, nj, _ = r.shape
    ni, nc = v.shape[1], v.shape[2]
    tc = 32
    ncb, ntb = t_ctx // tc, ttot // tc
    fwd = lambda g: (g, 0, 0)
    bwd = lambda g: (jnp.where(g < ncb, ncb - 1 - g, ntb + ncb - 1 - g), 0, 0)
    jf, jb = pl.BlockSpec((tc, nj, LANES), fwd), pl.BlockSpec((tc, nj, LANES), bwd)
    vf, vb = pl.BlockSpec((tc, ni, nc), fwd), pl.BlockSpec((tc, ni, nc), bwd)
    of, ob = pl.BlockSpec((tc, ni, LANES), fwd), pl.BlockSpec((tc, ni, LANES), bwd)
    osd = jax.ShapeDtypeStruct((ttot, ni, LANES), F32)
    return pl.pallas_call(
        functools.partial(_rwkv_scan_kernel, tc=tc),
        grid=(ntb,),
        in_specs=[jf, jb] * 5 + [vf, vb, pl.BlockSpec(ka.shape, lambda g: (0, 0))],
        out_specs=[of, ob],
        out_shape=[osd, osd],
        scratch_shapes=[pltpu.VMEM((nj, ni, LANES), F32), pltpu.VMEM((5, tc, nj, LANES), F32),
                        pltpu.VMEM((tc, ni, LANES), F32), pltpu.VMEM((tc, SUBLANES, LANES), F32)],
        compiler_params=_cparams(("arbitrary",)),
        name="rwkv_scan",
    )(r, r, kk, kk, k, k, w, w, a, a, v, v, ka)


def _s5_kernel(uf_ref, ub_ref, bm_ref, cm_ref, a_ref, yf_ref, yb_ref, h_ref, bu_ref, ubr_ref, *, tt):
    @pl.when(pl.program_id(0) == 0)
    def _():
        h_ref[...] = jnp.zeros_like(h_ref)

    nsb = bm_ref.shape[0]
    kin = bm_ref.shape[1] // 2
    sw = bm_ref.shape[2]
    hw = sw // 2
    for t in range(tt):
        ubr_ref[t] = ub_ref[tt - 1 - t]
    nothing = jnp.zeros(uf_ref.shape, F32)
    uf = jnp.concatenate([uf_ref[...], nothing], axis=1)
    ub = jnp.concatenate([nothing, ubr_ref[...]], axis=1)
    for sb in range(nsb):
        lhs = jnp.concatenate([uf[:, :, sb * kin:(sb + 1) * kin], ub[:, :, sb * kin:(sb + 1) * kin]], axis=-1)
        lhs = lhs.reshape(tt * SUBLANES, 2 * kin).astype(BF16)
        bu_ref[:, sb * sw:(sb + 1) * sw] = _dot(lhs, bm_ref[sb])

    def step(t, hs):
        rows = pl.ds(pl.multiple_of(t * SUBLANES, SUBLANES), SUBLANES)
        out = []
        for sb in range(nsb):
            hr, hi = hs[2 * sb], hs[2 * sb + 1]
            lo = sb * sw
            ar = a_ref[0, :, lo:lo + hw]
            ai = a_ref[0, :, lo + hw:lo + sw]
            nr = ar * hr - ai * hi + bu_ref[rows, lo:lo + hw]
            ni = ar * hi + ai * hr + bu_ref[rows, lo + hw:lo + sw]
            bu_ref[rows, lo:lo + hw] = nr
            bu_ref[rows, lo + hw:lo + sw] = ni
            out += [nr, ni]
        return tuple(out)

    h0 = []
    for sb in range(nsb):
        h0 += [h_ref[:, sb * sw:sb * sw + hw], h_ref[:, sb * sw + hw:(sb + 1) * sw]]
    hs = lax.fori_loop(0, tt, step, tuple(h0))
    for sb in range(nsb):
        h_ref[:, sb * sw:sb * sw + hw] = hs[2 * sb]
        h_ref[:, sb * sw + hw:(sb + 1) * sw] = hs[2 * sb + 1]

    nout = cm_ref.shape[2] // 2
    fwd_row = (lax.broadcasted_iota(jnp.int32, (tt * SUBLANES, nout), 0) % SUBLANES) < SUBLANES // 2
    ys = []
    for sb in range(nsb):
        yy = _dot(bu_ref[:, sb * sw:(sb + 1) * sw].astype(BF16), cm_ref[sb])
        ys.append(jnp.where(fwd_row, yy[:, :nout], yy[:, nout:]))
    y = jnp.concatenate(ys, axis=1).reshape(tt, SUBLANES, nsb * nout)
    yf_ref[...] = y
    for t in range(tt):
        yb_ref[tt - 1 - t] = y[t]


def s5_scan(u, bm, cm, a, t_ctx):
    ttot, bsz, width = u.shape
    rows = 2 * bsz
    nsb, _, sw = bm.shape
    tt = 64
    ncb, ntb = t_ctx // tt, ttot // tt
    fmap = lambda g: (g, 0, 0)
    bmap = lambda g: (jnp.where(g < ncb, ncb - 1 - g, ntb + ncb - 1 - g), 0, 0)
    full = lambda x: pl.BlockSpec(x.shape, lambda g: (0,) * x.ndim)
    osd = jax.ShapeDtypeStruct((ttot, rows, width), F32)
    return pl.pallas_call(
        functools.partial(_s5_kernel, tt=tt),
        grid=(ntb,),
        in_specs=[pl.BlockSpec((tt, bsz, width), fmap), pl.BlockSpec((tt, bsz, width), bmap),
                  full(bm), full(cm), full(a)],
        out_specs=[pl.BlockSpec((tt, rows, width), fmap), pl.BlockSpec((tt, rows, width), bmap)],
        out_shape=[osd, osd],
        scratch_shapes=[pltpu.VMEM((rows, nsb * sw), F32), pltpu.VMEM((tt * rows, nsb * sw), F32),
                        pltpu.VMEM((tt, bsz, width), F32)],
        compiler_params=_cparams(("arbitrary",)),
        name="s5_scan",
    )(u, u, bm, cm, a)


def _even_out_kernel(x_ref, of_ref, ob_ref, bonus_ref, g_ref, ys_ref, u_ref, m5_ref,
                     ones_ref, gng_ref, gnb_ref, dskip_ref, wglu_ref, bglu_ref, wo1_ref, wo2_ref, o_ref):
    ones = ones_ref[...]
    inv = 1.0 / HEAD_DIM
    o = of_ref[0] + ob_ref[0]
    mean = _dot_exact(o, ones) * inv
    oc = o - mean
    var = _dot_exact(oc * oc, ones) * inv
    y1 = (oc * lax.rsqrt(var + GN_EPS) * gng_ref[...] + gnb_ref[...] + bonus_ref[0]) * g_ref[0]
    y = ys_ref[0] + dskip_ref[...] * u_ref[0]
    y = jax.nn.gelu(y)
    y2 = y * _sigmoid(_dot(y.astype(BF16), wglu_ref[...]) + bglu_ref[...])
    out = _dot(y1.astype(BF16), wo1_ref[...]) + _dot(y2.astype(BF16), wo2_ref[...])
    o_ref[0] = x_ref[0] + m5_ref[0] * out


def even_out(x, off, o_f, o_b, bonus, g, ys, u, m5, consts, tm):
    bsz, t, d = x.shape
    w = o_f.shape[2]
    tokd = pl.BlockSpec((1, tm, d), lambda b, i: (b, i, 0))
    tokw = pl.BlockSpec((1, tm, w), lambda b, i: (b, i + off, 0))
    toks = pl.BlockSpec((1, tm, u.shape[2]), lambda b, i: (b, i + off, 0))
    full = lambda a: pl.BlockSpec(a.shape, lambda b, i: (0,) * a.ndim)
    return pl.pallas_call(
        _even_out_kernel,
        grid=(bsz, t // tm),
        in_specs=[tokd, tokw, tokw, tokw, tokw, toks, toks,
                  pl.BlockSpec((1, 1, d), lambda b, i: (b, 0, 0))] + [full(a) for a in consts],
        out_specs=tokd,
        out_shape=jax.ShapeDtypeStruct((bsz, t, d), F32),
        compiler_params=_cparams(("parallel", "parallel")),
        name="even_out",
    )(x, o_f, o_b, bonus, g, ys, u, m5, *consts)


def _hy_in_kernel(x_ref, xp_ref, xn_ref, gain_ref, shift_ref, scale_ref, w_ref, cw_ref, cb_ref,
                  z_ref, zb_ref, g1_ref, g2_ref, *, nt, c):
    i = pl.program_id(1)
    gain, shift, scale = gain_ref[...], shift_ref[0], scale_ref[0]
    h = _rms_mod(x_ref[0], gain, shift, scale).astype(BF16)
    hp = _rms_mod(xp_ref[0], gain, shift, scale).astype(BF16)
    hn = _rms_mod(xn_ref[0], gain, shift, scale).astype(BF16)
    tm = h.shape[0]
    row = lax.broadcasted_iota(jnp.int32, (tm, c), 0)
    outs = (z_ref, g1_ref, g2_ref)
    for part in range(3):
        wp = w_ref[:, part * c:(part + 1) * c]
        p = _dot(h, wp)
        pp = jnp.where(i > 0, _dot(hp, wp)[7:8], 0.0)
        pn = jnp.where(i < nt - 1, _dot(hn, wp)[0:1], 0.0)
        pm1 = jnp.where(row == 0, pp, pltpu.roll(p, 1, 0))
        pp1 = jnp.where(row == tm - 1, pn, pltpu.roll(p, tm - 1, 0))
        cw = cw_ref[:, part * c:(part + 1) * c]
        q = cw[0:1] * pm1 + cw[1:2] * p + cw[2:3] * pp1 + cb_ref[:, part * c:(part + 1) * c]
        outs[part][0] = q
        if part == 0:
            zb_ref[0] = q.astype(BF16)


def hyena_in(x, gain, shift, scale, w, conv_w, conv_b):
    bsz, t, d = x.shape
    c = w.shape[1] // 3
    tm = min(512, t)
    nt = t // tm
    hb = tm // 8
    nhb = t // 8
    vec = pl.BlockSpec((1, 1, d), lambda b, i: (b, 0, 0))
    tok = pl.BlockSpec((1, tm, c), lambda b, i: (b, i, 0))
    sds = jax.ShapeDtypeStruct((bsz, t, c), F32)
    return pl.pallas_call(
        functools.partial(_hy_in_kernel, nt=nt, c=c),
        grid=(bsz, nt),
        in_specs=[pl.BlockSpec((1, tm, d), lambda b, i: (b, i, 0)),
                  pl.BlockSpec((1, 8, d), lambda b, i: (b, jnp.maximum(i * hb - 1, 0), 0)),
                  pl.BlockSpec((1, 8, d), lambda b, i: (b, jnp.minimum((i + 1) * hb, nhb - 1), 0)),
                  pl.BlockSpec((1, d), lambda b, i: (0, 0)),
                  vec, vec,
                  pl.BlockSpec(w.shape, lambda b, i: (0, 0)),
                  pl.BlockSpec(conv_w.shape, lambda b, i: (0, 0)),
                  pl.BlockSpec((1, 3 * c), lambda b, i: (0, 0))],
        out_specs=[tok, tok, tok, tok],
        out_shape=[sds, jax.ShapeDtypeStruct((bsz, t, c), BF16), sds, sds],
        compiler_params=_cparams(("parallel", "parallel")),
        name="hyena_in",
    )(x, x, x, gain.reshape(1, d), shift, scale, w, conv_w, conv_b.reshape(1, 3 * c))


def _pack_complex(re, im):
    bits = lambda x: lax.bitcast_convert_type(x.astype(BF16).astype(F32), jnp.uint32)
    return lax.shift_right_logical(bits(re), jnp.uint32(16)) | bits(im)


def _unpack_complex(w):
    re = lax.bitcast_convert_type(lax.shift_left(w, jnp.uint32(16)), F32)
    im = lax.bitcast_convert_type(w & jnp.uint32(0xFFFF0000), F32)
    return jnp.concatenate([re, im], axis=0).astype(BF16)


def _fft_conv_kernel(z_ref, kf_ref, g_ref, w1_ref, g2i_ref, f1c_ref, twt_ref, o_ref, a_ref, t_ref,
                     *, n1, k1c):
    s = pl.program_id(2)
    n2 = FFT_N2
    nd = n1 // 2
    slab = n1 // k1c

    @pl.when(s == 0)
    def _():
        w1 = w1_ref[...]

        def body(j, carry):
            x = jnp.concatenate([z_ref[0, j], z_ref[1, j]], axis=0)
            a = _dot(w1, x)
            a_ref[pl.ds(pl.multiple_of(j * n1, n1), n1), :] = _pack_complex(a[:n1], a[n1:])
            return carry

        lax.fori_loop(0, n2, body, 0, unroll=8)

    @pl.when((s > 0) & (s <= k1c))
    def _():
        g2i = g2i_ref[...]
        base = (s - 1) * slab

        def body(kk, carry):
            k1 = base + kk
            ak = _unpack_complex(a_ref[pl.ds(k1, n2, stride=n1), :])
            x = _dot(g_ref[kk], ak)
            xr, xi = x[:n2], x[n2:]
            kr, ki = kf_ref[0, kk], kf_ref[1, kk]
            pr = xr * kr - xi * ki
            pi = xr * ki + xi * kr
            tt = _dot(g2i, jnp.concatenate([pr, pi], axis=0).astype(BF16))
            t_ref[pl.ds(pl.multiple_of(k1 * n2, n2), n2), :] = _pack_complex(tt[:n2], tt[n2:])
            return carry

        lax.fori_loop(0, slab, body, 0, unroll=4)

    @pl.when(s == k1c + 1)
    def _():
        f1r, f1i = f1c_ref[0], f1c_ref[1]

        def body(j, carry):
            twr = twt_ref[0, pl.ds(j, 1), :]
            twi = twt_ref[1, pl.ds(j, 1), :]
            wr = f1r * twr + f1i * twi
            wi = f1i * twr - f1r * twi
            w3 = jnp.concatenate([jnp.concatenate([wr, -wi], axis=1),
                                  jnp.concatenate([wi, wr], axis=1)], axis=0).astype(BF16)
            tn = _unpack_complex(t_ref[pl.ds(j, n1, stride=n2), :])
            y = _dot(w3, tn)
            o_ref[0, j] = y[:nd]
            o_ref[1, j] = y[nd:]
            return carry

        lax.fori_loop(0, n2, body, 0, unroll=8)


def _fft_consts(n1):
    n2 = FFT_N2
    n = n1 * n2
    nd = n1 // 2
    k1 = np.arange(n1)
    f1 = np.exp(-2j * np.pi * np.outer(k1, np.arange(nd)) / n1)
    w1 = np.block([[f1.real, -f1.imag], [f1.imag, f1.real]])
    f2 = np.exp(-2j * np.pi * np.outer(np.arange(n2), np.arange(n2)) / n2)
    f2c = np.conj(f2)
    g2i = np.block([[f2c.real, -f2c.imag], [f2c.imag, f2c.real]])
    f1c = np.exp(2j * np.pi * np.outer(np.arange(nd), k1) / n1) / n
    tw = np.exp(-2j * np.pi * np.outer(k1, np.arange(n2)) / n)
    cplx = lambda m: jnp.asarray(np.stack([m.real, m.imag]), F32)
    (f2r, f2i), (twr, twi) = cplx(f2), cplx(tw)
    gr = f2r[None] * twr[:, None, :] - f2i[None] * twi[:, None, :]
    gi = f2r[None] * twi[:, None, :] + f2i[None] * twr[:, None, :]
    gtw = jnp.concatenate([jnp.concatenate([gr, -gi], axis=2), jnp.concatenate([gi, gr], axis=2)], axis=1)
    return (gtw.astype(BF16), jnp.asarray(w1, BF16), jnp.asarray(g2i, BF16), cplx(f1c), cplx(tw.T))


def fft_conv(zt, kf):
    bsz, n2, nd, c = zt.shape
    n1 = 2 * nd
    cb = LANES
    k1c = 8 if n1 % 8 == 0 and n1 >= 64 else 2
    slab = n1 // k1c
    gtw, *consts = _fft_consts(n1)
    full = lambda a: pl.BlockSpec(a.shape, lambda j, p, s: (0,) * a.ndim)
    blk = pl.BlockSpec((2, n2, nd, cb), lambda j, p, s: (p, 0, 0, j))
    chunk = lambda j, p, s: jnp.clip(s - 1, 0, k1c - 1)
    return pl.pallas_call(
        functools.partial(_fft_conv_kernel, n1=n1, k1c=k1c),
        grid=(c // cb, bsz // 2, k1c + 2),
        in_specs=[blk,
                  pl.BlockSpec((2, slab, n2, cb), lambda j, p, s: (0, chunk(j, p, s), 0, j)),
                  pl.BlockSpec((slab, 2 * n2, 2 * n2), lambda j, p, s: (chunk(j, p, s), 0, 0))]
                 + [full(a) for a in consts],
        out_specs=blk,
        out_shape=jax.ShapeDtypeStruct(zt.shape, F32),
        scratch_shapes=[pltpu.VMEM((n2 * n1, cb), jnp.uint32), pltpu.VMEM((n1 * n2, cb), jnp.uint32)],
        compiler_params=_cparams(("parallel", "parallel", "arbitrary")),
        name="fft_conv",
    )(zt, kf, gtw, *consts)


def _fft_fwd_kernel(z_ref, scale_ref, g_ref, w1_ref, o_ref, a_ref, *, n1, k1c):
    s = pl.program_id(2)
    n2 = FFT_N2
    slab = n1 // k1c
    cb = o_ref.shape[4]

    @pl.when(s == 0)
    def _():
        w1 = w1_ref[...]

        def body(j, carry):
            a = _dot(w1, jnp.concatenate([z_ref[0, j], z_ref[1, j]], axis=1))
            packed = _pack_complex(a[:n1], a[n1:])
            rows = pl.ds(pl.multiple_of(j * n1, n1), n1)
            a_ref[0, rows, :] = packed[:, :cb]
            a_ref[1, rows, :] = packed[:, cb:]
            return carry

        lax.fori_loop(0, n2, body, 0, unroll=8)

    @pl.when(s > 0)
    def _():
        base = (s - 1) * slab

        def body(kk, carry):
            k1 = base + kk
            rows = pl.ds(k1, n2, stride=n1)
            ak = jnp.concatenate([_unpack_complex(a_ref[0, rows, :]), _unpack_complex(a_ref[1, rows, :])], axis=1)
            x = _dot(g_ref[kk], ak)
            pos, neg = x[:, :cb], x[:, cb:]
            scale = scale_ref[0]
            o_ref[0, 0, kk] = (pos[:n2] + neg[:n2]) * scale
            o_ref[0, 1, kk] = (pos[n2:] - neg[n2:]) * scale
            return carry

        lax.fori_loop(0, slab, body, 0, unroll=4)


def filter_spectrum(taps, scale):
    orders, _, n2, nd, c = taps.shape
    n1 = 2 * nd
    cb = LANES
    k1c = 8 if n1 % 8 == 0 and n1 >= 64 else 2
    slab = n1 // k1c
    gtw, w1 = _fft_consts(n1)[:2]
    w1 = w1[:, :nd]
    full = lambda a: pl.BlockSpec(a.shape, lambda j, r, s: (0,) * a.ndim)
    return pl.pallas_call(
        functools.partial(_fft_fwd_kernel, n1=n1, k1c=k1c),
        grid=(c // cb, orders, k1c + 1),
        in_specs=[pl.BlockSpec((None, 2, n2, nd, cb), lambda j, r, s: (r, 0, 0, 0, j)),
                  pl.BlockSpec((1, 1, cb), lambda j, r, s: (r, 0, j)),
                  pl.BlockSpec((slab, 2 * n2, 2 * n2), lambda j, r, s: (jnp.maximum(s - 1, 0), 0, 0)),
                  full(w1)],
        out_specs=pl.BlockSpec((1, 2, slab, n2, cb), lambda j, r, s: (r, 0, jnp.maximum(s - 1, 0), 0, j)),
        out_shape=jax.ShapeDtypeStruct((orders, 2, n1, n2, c), F32),
        scratch_shapes=[pltpu.VMEM((2, n2 * n1, cb), jnp.uint32)],
        compiler_params=_cparams(("parallel", "parallel", "arbitrary")),
        name="fft_forward",
    )(taps, scale, gtw, w1)


def _hy_filter_kernel(feat_ref, fw1_ref, fb1_ref, fw2_ref, fb2_ref, fw3_ref, fb3_ref, fw4_ref, freq_ref,
                      delta_ref, z_ref, ss_ref):
    @pl.when(pl.program_id(1) == 0)
    def _():
        ss_ref[...] = jnp.zeros_like(ss_ref)

    feats = feat_ref[...]
    t = feats[:, 0:1]
    fr = freq_ref[...]
    h = jnp.sin(fr * (_dot_exact(feats, fw1_ref[...]) + fb1_ref[...]))
    h = jnp.sin(fr * (_dot_exact(h, fw2_ref[...]) + fb2_ref[...]))
    h = jnp.sin(fr * (_dot_exact(h, fw3_ref[...]) + fb3_ref[...]))
    lag0 = jnp.where(pl.program_id(0) % 2 == 1, 0.0, 1.0)
    filt = _dot(h.astype(BF16), fw4_ref[...]) * (jnp.exp(-t * delta_ref[...]) * jnp.where(t == 0.0, lag0, 1.0))
    z_ref[0] = filt.astype(BF16)
    ss_ref[0] += jnp.sum(filt * filt, axis=0, keepdims=True)


def hyena_filter(t, fw1, fb1, fw2, fb2, fw3, fb3, fw4, freq, width):
    nd = t // FFT_N2
    pos = (jnp.arange(FFT_N2, dtype=F32)[:, None] + FFT_N2 * jnp.arange(nd, dtype=F32)[None, :]).reshape(-1, 1)
    tt = pos / max(t - 1, 1)
    ang = 2 * math.pi * pos / t
    nb = (HY_EMB - 1) // 2
    bands = jnp.linspace(1e-4, nb - 1, nb, dtype=F32)[None]
    feats = jnp.concatenate([tt, jnp.cos(bands * ang), -jnp.sin(bands * ang),
                             jnp.zeros((t, LANES - HY_EMB), F32)], axis=-1)
    deltas = jnp.abs(jnp.linspace(HY_MIN_DECAY, HY_MAX_DECAY, width, dtype=F32)).reshape(1, width)
    rows = 2 * HY_ORDER
    tm = min(512, t)
    hid = fw1.shape[1]
    full = lambda a: pl.BlockSpec(a.shape, lambda r, i: (0,) * a.ndim)
    fw1p = jnp.concatenate([fw1, jnp.zeros((LANES - HY_EMB, hid), F32)], axis=0)
    vecs = [fw1p, fb1.reshape(1, hid), fw2, fb2.reshape(1, hid), fw3, fb3.reshape(1, hid)]
    return pl.pallas_call(
        _hy_filter_kernel,
        grid=(rows, t // tm),
        in_specs=[pl.BlockSpec((tm, LANES), lambda r, i: (i, 0))] + [full(a) for a in vecs]
                 + [pl.BlockSpec((hid, width), lambda r, i: (0, r)), full(freq.reshape(1, hid)), full(deltas)],
        out_specs=[pl.BlockSpec((1, tm, width), lambda r, i: (r, i, 0)),
                   pl.BlockSpec((1, 1, width), lambda r, i: (r, 0, 0))],
        out_shape=[jax.ShapeDtypeStruct((rows, t, width), BF16), jax.ShapeDtypeStruct((rows, 1, width), F32)],
        compiler_params=_cparams(("parallel", "arbitrary")),
        name="hyena_filter",
    )(feats, *vecs, fw4.astype(BF16), freq.reshape(1, hid), deltas)


def hyena_filter_spectrum_fft(t, fw1, fb1, fw2, fb2, fw3, fb3, fw4, freq, width):
    taps, ss = hyena_filter(t, fw1, fb1, fw2, fb2, fw3, fb3, fw4, freq, width)
    nd = t // FFT_N2
    scale = lax.rsqrt(ss.reshape(HY_ORDER, 2, width).sum(axis=1, keepdims=True) + 1e-6)
    return filter_spectrum(taps.reshape(HY_ORDER, 2, FFT_N2, nd, width), scale)


def _dft_conv_kernel(z_ref, kf_ref, fw_ref, iv_ref, o_ref, *, t):
    x = jnp.concatenate([z_ref[0], z_ref[1]], axis=0)
    spec = _dot(fw_ref[...], x)
    xr, xi = spec[:2 * t], spec[2 * t:]
    kr, ki = kf_ref[0], kf_ref[1]
    pr = xr * kr - xi * ki
    pi = xr * ki + xi * kr
    y = _dot(iv_ref[...], jnp.concatenate([pr, pi], axis=0).astype(BF16))
    o_ref[0] = y[:t]
    o_ref[1] = y[t:]


def dft_conv(zb, kf):
    bsz, t, c = zb.shape
    n = 2 * t
    f = np.exp(-2j * np.pi * np.outer(np.arange(n), np.arange(t)) / n)
    fw = np.block([[f.real, -f.imag], [f.imag, f.real]])
    fi = np.exp(2j * np.pi * np.outer(np.arange(t), np.arange(n)) / n) / n
    iv = np.block([[fi.real, -fi.imag], [fi.imag, fi.real]])
    fw, iv = jnp.asarray(fw, BF16), jnp.asarray(iv, BF16)
    cb = LANES
    blk = pl.BlockSpec((2, t, cb), lambda j, p: (p, 0, j))
    return pl.pallas_call(
        functools.partial(_dft_conv_kernel, t=t),
        grid=(c // cb, bsz // 2),
        in_specs=[blk, pl.BlockSpec((2, n, cb), lambda j, p: (0, 0, j)),
                  pl.BlockSpec(fw.shape, lambda j, p: (0, 0)), pl.BlockSpec(iv.shape, lambda j, p: (0, 0))],
        out_specs=blk,
        out_shape=jax.ShapeDtypeStruct(zb.shape, F32),
        compiler_params=_cparams(("parallel", "parallel")),
        name="dft_conv",
    )(zb, kf, fw, iv)


def _conv_rows(conv_ref):
    if len(conv_ref.shape) == 3:
        return conv_ref[0]
    return jnp.concatenate([conv_ref[0, :, k, :] for k in range(conv_ref.shape[2])], axis=0)


def _conv_spec(conv, tm, c):
    if conv.ndim == 3:
        return pl.BlockSpec((1, tm, c), lambda b, i: (b, i, 0))
    return pl.BlockSpec((1, FFT_N2, tm // FFT_N2, c), lambda b, i: (b, 0, i, 0))


def _hy_gate_kernel(conv_ref, z_ref, gate_ref, bias_ref, o_ref, ob_ref):
    y = gate_ref[0] * (_conv_rows(conv_ref) + bias_ref[...] * z_ref[0])
    o_ref[0] = y
    ob_ref[0] = y.astype(BF16)


def hyena_gate(conv, z, gate, bias):
    bsz, t, c = z.shape
    tm = min(SUBLANES * FFT_N2, t)
    tok = pl.BlockSpec((1, tm, c), lambda b, i: (b, i, 0))
    return pl.pallas_call(
        _hy_gate_kernel,
        grid=(bsz, t // tm),
        in_specs=[_conv_spec(conv, tm, c), tok, tok, pl.BlockSpec((1, c), lambda b, i: (0, 0))],
        out_specs=[tok, tok],
        out_shape=[jax.ShapeDtypeStruct(z.shape, F32), jax.ShapeDtypeStruct(z.shape, BF16)],
        compiler_params=_cparams(("parallel", "parallel")),
        name="hyena_gate",
    )(conv, z, gate, bias.reshape(1, c))


def _hy_out_kernel(x_ref, conv_ref, z_ref, gate_ref, bias_ref, w_ref, m5_ref, o_ref):
    y = gate_ref[0] * (_conv_rows(conv_ref) + bias_ref[...] * z_ref[0])
    o_ref[0] = x_ref[0] + m5_ref[0] * _dot(y.astype(BF16), w_ref[...])


def hyena_out(x, conv, z, gate, bias, w, m5):
    bsz, t, d = x.shape
    c = z.shape[2]
    tm = min(SUBLANES * FFT_N2, t)
    tokd = pl.BlockSpec((1, tm, d), lambda b, i: (b, i, 0))
    tokc = pl.BlockSpec((1, tm, c), lambda b, i: (b, i, 0))
    return pl.pallas_call(
        _hy_out_kernel,
        grid=(bsz, t // tm),
        in_specs=[tokd, _conv_spec(conv, tm, c), tokc, tokc, pl.BlockSpec((1, c), lambda b, i: (0, 0)),
                  pl.BlockSpec(w.shape, lambda b, i: (0, 0)), pl.BlockSpec((1, 1, d), lambda b, i: (b, 0, 0))],
        out_specs=tokd,
        out_shape=jax.ShapeDtypeStruct(x.shape, F32),
        compiler_params=_cparams(("parallel", "parallel")),
        name="hyena_out",
    )(x, conv, z, gate, bias.reshape(1, c), w, m5)


def _hyena_filter_spectrum(t, fw1, fb1, fw2, fb2, fw3, fb3, fw4, freq, width):
    pos = jnp.arange(t, dtype=F32)[:, None]
    tt = pos / max(t - 1, 1)
    ang = 2 * math.pi * pos / t
    nb = (HY_EMB - 1) // 2
    bands = jnp.linspace(1e-4, nb - 1, nb, dtype=F32)[None]
    feats = jnp.concatenate([tt, jnp.cos(bands * ang), -jnp.sin(bands * ang)], axis=-1)
    hdn = jnp.sin(freq * (feats @ fw1 + fb1))
    hdn = jnp.sin(freq * (hdn @ fw2 + fb2))
    hdn = jnp.sin(freq * (hdn @ fw3 + fb3))
    filt = (hdn @ fw4).reshape(t, HY_ORDER, 2, width)
    deltas = jnp.abs(jnp.linspace(HY_MIN_DECAY, HY_MAX_DECAY, width, dtype=F32))
    filt = filt * jnp.exp(-tt[:, :, None, None] * deltas)
    fwd, bwd = filt[:, :, 0], filt[:, :, 1]
    kern = jnp.concatenate([fwd, jnp.zeros_like(fwd[:1]), bwd[:0:-1]], axis=0)
    kern = kern * lax.rsqrt(jnp.sum(kern * kern, axis=0, keepdims=True) + 1e-6)
    spec = jnp.fft.fft(kern, axis=0)
    return jnp.stack([jnp.real(spec), jnp.imag(spec)], axis=1).transpose(2, 1, 0, 3).astype(F32)


def _block_diag_pair(m):
    z = jnp.zeros_like(m[0])
    return jnp.concatenate([jnp.concatenate([m[0], z], axis=1), jnp.concatenate([z, m[1]], axis=1)], axis=0)


def _even_mixer(lat, cx, ml, mc, gain, prm, ctx_out):
    (w_in, mu, w0, w_up, a0, a_up, g_up, k_k, k_a, r_k, gn_g, gn_b,
     lam_re, lam_im, log_dt, b_re, b_im, c_re, c_im, d_skip, w_glu, b_glu, w_out) = prm
    bsz, t_lat, d = lat.shape
    t_ctx = cx.shape[1]
    width = k_k.shape[0]
    heads = width // HEAD_DIM
    n_cols = mu.shape[0]
    s5w = d_skip.shape[0]
    ttot = t_ctx + t_lat
    nch = bsz * heads
    tm = t_ctx
    assert tm % GRID_W == 0 and t_lat % tm == 0 and 4 * nch == LANES and 2 * bsz == SUBLANES

    p_all, u_all = mod_matmul_stream(cx, lat, gain, mc, ml, w_in.astype(BF16), tm, n_cols)

    head_of = np.arange(width) // HEAD_DIM
    ones = jnp.asarray(head_of[:, None] == head_of[None, :], F32)
    feat_consts = (mu.reshape(1, -1), _block_diag_pair(w_up).astype(BF16), w0.reshape(1, -1),
                   _block_diag_pair(a_up).astype(BF16), a0.reshape(1, -1), g_up.astype(BF16),
                   k_k.reshape(1, -1), k_a.reshape(1, -1), r_k.reshape(1, -1), ones)
    kk2, v2, g, bonus, r2, k2, w2, a2 = rwkv_features(p_all, n_cols, width, feat_consts, tm)

    def key_major(x):
        x = x.reshape(2, bsz, ttot, heads, 2, HALF_HEAD)
        return x.transpose(2, 5, 0, 4, 1, 3).reshape(ttot, HALF_HEAD, LANES)

    def value_major(x):
        x = x.reshape(2, bsz, ttot, heads, HEAD_DIM)
        return x.transpose(2, 4, 0, 1, 3).reshape(ttot, HEAD_DIM, 2 * nch)

    ka_t = k_a.reshape(heads, 2, HALF_HEAD).transpose(2, 1, 0)[:, None, :, None, :]
    ka_t = jnp.broadcast_to(ka_t, (HALF_HEAD, 2, 2, bsz, heads)).reshape(HALF_HEAD, LANES)
    o_f, o_b = rwkv_scan(key_major(r2), key_major(kk2), key_major(k2), key_major(w2), key_major(a2),
                         value_major(v2), ka_t, t_ctx)

    def token_major(o, lane0):
        o = o[..., lane0:lane0 + nch].reshape(ttot, HEAD_DIM, bsz, heads)
        return o.transpose(2, 0, 3, 1).reshape(bsz, ttot, width)

    o_f, o_b = token_major(o_f, 0), token_major(o_b, LANES // 2)

    ng = lam_re.shape[1]
    gps = ng // S5_SUPER
    lam = lax.complex(lam_re, lam_im)
    dt = jnp.exp(log_dt)[..., None]
    a_bar = jnp.exp(lam * dt)
    b_bar = ((a_bar - 1) / lam)[..., None] * lax.complex(b_re, b_im)
    eye = jnp.eye(gps, dtype=F32)

    def b_mat(x):
        x = x.reshape(2, S5_SUPER, gps, S5_STATE, S5_GROUP)
        m = jnp.einsum('dsgph,gk->sdghkp', x, eye)
        return m.reshape(S5_SUPER, 2 * gps * S5_GROUP, gps * S5_STATE)

    def c_mat(x):
        x = x.reshape(2, S5_SUPER, gps, S5_GROUP, S5_STATE)
        m = jnp.einsum('dsghp,gk->skpdgh', x, eye)
        return m.reshape(S5_SUPER, gps * S5_STATE, 2 * gps * S5_GROUP)

    bm = jnp.concatenate([b_mat(jnp.real(b_bar)), b_mat(jnp.imag(b_bar))], axis=2).astype(BF16)
    cm = jnp.concatenate([c_mat(c_re), -c_mat(c_im)], axis=1).astype(BF16)

    def a_rows(x):
        x = x.reshape(2, 1, S5_SUPER, gps * S5_STATE)
        return jnp.broadcast_to(x, (2, bsz, S5_SUPER, gps * S5_STATE)).reshape(2 * bsz, S5_SUPER, -1)

    a_arr = jnp.concatenate([a_rows(jnp.real(a_bar)), a_rows(jnp.imag(a_bar))], axis=2)
    a_arr = a_arr.reshape(1, 2 * bsz, -1)

    y_f, y_b = s5_scan(u_all.transpose(1, 0, 2), bm, cm, a_arr, t_ctx)
    ys = (y_f[:, :bsz] + y_b[:, bsz:]).transpose(1, 0, 2)

    wo = w_out.astype(BF16)
    out_consts = (ones, gn_g.reshape(1, -1), gn_b.reshape(1, -1), d_skip.reshape(1, -1),
                  w_glu.astype(BF16), b_glu.reshape(1, -1), wo[:width], wo[width:])
    lat = even_out(lat, 1, o_f, o_b, bonus, g, ys, u_all, ml[5], out_consts, tm)
    if ctx_out:
        cx = even_out(cx, 0, o_f, o_b, bonus, g, ys, u_all, mc[5], out_consts, tm)
    return lat, cx


def _hyena_mixer(x, gain, m, prm):
    (w_in, conv_w, conv_b, fw1, fb1, fw2, fb2, fw3, fb3, fw4, freq, bias_d, w_out) = prm
    bsz, t, d = x.shape
    c = w_out.shape[0]
    use_fft = t % (FFT_N2 * 2) == 0 and t >= 4 * FFT_N2
    if use_fft:
        kf = hyena_filter_spectrum_fft(t, fw1, fb1, fw2, fb2, fw3, fb3, fw4, freq, c)
    else:
        kf = _hyena_filter_spectrum(t, fw1, fb1, fw2, fb2, fw3, fb3, fw4, freq, c)
    z, zb, g1, g2 = hyena_in(x, gain, m[3], m[4], w_in.astype(BF16), conv_w, conv_b)
    gates = (g1, g2)
    for n in range(HY_ORDER):
        if use_fft:
            nd = t // FFT_N2
            zt = zb.reshape(bsz, nd, FFT_N2, c).transpose(0, 2, 1, 3)
            conv = fft_conv(zt, kf[n])
        else:
            conv = dft_conv(zb, kf[n])
        if n < HY_ORDER - 1:
            z, zb = hyena_gate(conv, z, gates[n], bias_d[n])
        else:
            return hyena_out(x, conv, z, gates[n], bias_d[n], w_out.astype(BF16), m[5])


def kernel(x, c, ctx, c_ctx, norm_g, ada_w, ada_b, ffn_wg, ffn_wu, ffn_wd, final_g, ev_w_in, ev_mu, ev_w0, ev_w_up, ev_a0, ev_a_up, ev_g_up, ev_k_k, ev_k_a, ev_r_k, ev_gn_g, ev_gn_b, ev_lam_re, ev_lam_im, ev_log_dt, ev_b_re, ev_b_im, ev_c_re, ev_c_im, ev_d, ev_w_glu, ev_b_glu, ev_w_out, od_w_in, od_conv_w, od_conv_b, od_fw1, od_fb1, od_fw2, od_fb2, od_fw3, od_fb3, od_fw4, od_freq, od_bias, od_w_out):
    depth = norm_g.shape[0]
    bsz, _, d = x.shape
    n_even = (depth + 1) // 2
    last_ctx = 2 * (n_even - 1)

    cond8 = jnp.concatenate([c, c_ctx[None], jnp.zeros((8 - bsz - 1, d), F32)], axis=0)
    mods = ada_mods_all(cond8, ada_w, ada_b)

    wg, wu, wd = ffn_wg.astype(BF16), ffn_wu.astype(BF16), ffn_wd.astype(BF16)
    lat, cx = x, ctx
    for l in range(depth):
        run_ctx = l <= last_ctx
        ctx_out = l < last_ctx
        i = l // 2
        ml = [mods[l, :bsz, None, k * d:(k + 1) * d] for k in range(N_MOD)]
        mc = [jnp.broadcast_to(mods[l, bsz:bsz + 1, None, k * d:(k + 1) * d], (bsz, 1, d))
              for k in range(N_MOD)]
        lat = ffn_half(lat, norm_g[l, 0], ml[0], ml[1], ml[2], wg[l, 0], wu[l, 0], wd[l, 0])
        if run_ctx:
            cx = ffn_half(cx, norm_g[l, 0], mc[0], mc[1], mc[2], wg[l, 0], wu[l, 0], wd[l, 0])
        if l % 2 == 0:
            prm = (ev_w_in[i], ev_mu[i], ev_w0[i], ev_w_up[i], ev_a0[i], ev_a_up[i], ev_g_up[i],
                   ev_k_k[i], ev_k_a[i], ev_r_k[i], ev_gn_g[i], ev_gn_b[i],
                   ev_lam_re[i], ev_lam_im[i], ev_log_dt[i], ev_b_re[i], ev_b_im[i], ev_c_re[i], ev_c_im[i],
                   ev_d[i], ev_w_glu[i], ev_b_glu[i], ev_w_out[i])
            lat, cx = _even_mixer(lat, cx, ml, mc, norm_g[l, 1], prm, ctx_out)
        else:
            prm = (od_w_in[i], od_conv_w[i], od_conv_b[i], od_fw1[i], od_fb1[i], od_fw2[i], od_fb2[i],
                   od_fw3[i], od_fb3[i], od_fw4[i], od_freq[i], od_bias[i], od_w_out[i])
            lat = _hyena_mixer(lat, norm_g[l, 1], ml, prm)
            if ctx_out:
                cx = _hyena_mixer(cx, norm_g[l, 1], mc, prm)
        fin = final_g if l == depth - 1 else None
        lat = ffn_half(lat, norm_g[l, 2], ml[6], ml[7], ml[8], wg[l, 1], wu[l, 1], wd[l, 1], fin)
        if ctx_out:
            cx = ffn_half(cx, norm_g[l, 2], mc[6], mc[7], mc[8], wg[l, 1], wu[l, 1], wd[l, 1])
    return lat
```

```python
import functools
import math

import numpy as np
import jax
import jax.numpy as jnp
from jax import lax
from jax.experimental import pallas as pl
from jax.experimental.pallas import tpu as pltpu

F32 = jnp.float32
BF16 = jnp.bfloat16
HIGHEST = lax.Precision.HIGHEST

N_MOD = 9
NORM_EPS = 1e-6
GN_EPS = 64e-5
GRID_W = 64
HEAD_DIM = 64
HALF_HEAD = HEAD_DIM // 2
S5_GROUP = 16
S5_STATE = 64
S5_SUPER = 4
HY_ORDER = 2
HY_EMB = 33
HY_MIN_DECAY = math.log(1e-2) / 1.5
HY_MAX_DECAY = math.log(1e-2) / 0.3
LANES = 128
SUBLANES = 8
MXU_DIM = 256
FFT_N2 = 128
VMEM_LIMIT = 56 * 1024 * 1024


def _cparams(sem, vmem=VMEM_LIMIT):
    return pltpu.CompilerParams(dimension_semantics=sem, vmem_limit_bytes=vmem)


def _dot(a, b):
    return jnp.dot(a, b, preferred_element_type=F32)


def _dot_exact(a, b):
    return jnp.dot(a, b, preferred_element_type=F32, precision=HIGHEST)


def _rms_mod(x, gain, shift, scale):
    ms = jnp.mean(x * x, axis=-1, keepdims=True)
    return x * lax.rsqrt(ms + NORM_EPS) * gain * (1.0 + scale) + shift


def _sigmoid(x):
    return 1.0 / (1.0 + jnp.exp(-x))


def _silu(x):
    return x * _sigmoid(x)


def _ada_kernel(c_ref, w_ref, b_ref, o_ref):
    s = _silu(c_ref[...])
    o_ref[0] = _dot(s.astype(BF16), w_ref[0].astype(BF16)) + b_ref[0]


def ada_mods_all(cond8, ada_w, ada_b):
    depth, d, n = ada_w.shape
    tn = n // 8
    return pl.pallas_call(
        _ada_kernel,
        grid=(depth, n // tn),
        in_specs=[pl.BlockSpec((8, d), lambda l, j: (0, 0)),
                  pl.BlockSpec((1, d, tn), lambda l, j: (l, 0, j)),
                  pl.BlockSpec((1, 1, tn), lambda l, j: (l, 0, j))],
        out_specs=pl.BlockSpec((1, 8, tn), lambda l, j: (l, 0, j)),
        out_shape=jax.ShapeDtypeStruct((depth, 8, n), F32),
        compiler_params=_cparams(("parallel", "parallel")),
        name="ada_mods",
    )(cond8, ada_w, ada_b.reshape(depth, 1, n))


def _ffn_kernel(x_ref, gain_ref, shift_ref, scale_ref, gate_ref, wg_ref, wu_ref, wd_ref, fg_ref,
                o_ref, *, final_norm, chunks):
    x = x_ref[0]
    h = _rms_mod(x, gain_ref[...], shift_ref[0], scale_ref[0]).astype(BF16)
    acc = None
    for lo, hi in chunks:
        g = _dot(h, wg_ref[:, lo:hi])
        u = _dot(h, wu_ref[:, lo:hi])
        part = _dot((_silu(g) * u).astype(BF16), wd_ref[lo:hi, :])
        acc = part if acc is None else acc + part
    y = x + 0.5 * gate_ref[0] * acc
    if final_norm:
        ms = jnp.mean(y * y, axis=-1, keepdims=True)
        y = y * lax.rsqrt(ms + NORM_EPS) * fg_ref[...]
    o_ref[0] = y


def ffn_half(x, gain, shift, scale, gate, wg, wu, wd, final_g=None):
    bsz, t, d = x.shape
    ff = wg.shape[1]
    tm = min(1024, t)
    step = 2 * MXU_DIM
    chunks = tuple((lo, min(lo + step, ff)) for lo in range(0, ff, step))
    fg = jnp.ones((1, d), F32) if final_g is None else final_g.reshape(1, d)
    vec = pl.BlockSpec((1, 1, d), lambda b, i: (b, 0, 0))
    resident = lambda a: pl.BlockSpec(a.shape, lambda b, i: (0, 0), pipeline_mode=pl.Buffered(1))
    return pl.pallas_call(
        functools.partial(_ffn_kernel, final_norm=final_g is not None, chunks=chunks),
        grid=(bsz, t // tm),
        in_specs=[pl.BlockSpec((1, tm, d), lambda b, i: (b, i, 0)),
                  pl.BlockSpec((1, d), lambda b, i: (0, 0)),
                  vec, vec, vec, resident(wg), resident(wu), resident(wd),
                  pl.BlockSpec((1, d), lambda b, i: (0, 0))],
        out_specs=pl.BlockSpec((1, tm, d), lambda b, i: (b, i, 0)),
        out_shape=jax.ShapeDtypeStruct((bsz, t, d), F32),
        compiler_params=_cparams(("parallel", "parallel")),
        name="ffn_half",
    )(x, gain.reshape(1, d), shift, scale, gate, wg, wu, wd, fg)


def _modmm_kernel(cx_ref, lat_ref, gain_ref, shc_ref, scc_ref, shl_ref, scl_ref, w_ref, o_ref, u_ref, *, nc):
    is_ctx = pl.program_id(1) < nc
    x = jnp.where(is_ctx, cx_ref[0], lat_ref[0])
    shift = jnp.where(is_ctx, shc_ref[0], shl_ref[0])
    scale = jnp.where(is_ctx, scc_ref[0], scl_ref[0])
    h = _rms_mod(x, gain_ref[...], shift, scale).astype(BF16)
    n1 = o_ref.shape[2]
    o_ref[0] = _dot(h, w_ref[:, :n1])
    u_ref[0] = _dot(h, w_ref[:, n1:])


def mod_matmul_stream(cx, lat, gain, mc, ml, w, tm, n_first):
    bsz, t_ctx, d = cx.shape
    t_lat = lat.shape[1]
    n = w.shape[1]
    nc = t_ctx // tm
    nt = nc + t_lat // tm
    vec = pl.BlockSpec((1, 1, d), lambda b, i: (b, 0, 0))
    return pl.pallas_call(
        functools.partial(_modmm_kernel, nc=nc),
        grid=(bsz, nt),
        in_specs=[pl.BlockSpec((1, tm, d), lambda b, i: (b, jnp.minimum(i, nc - 1), 0)),
                  pl.BlockSpec((1, tm, d), lambda b, i: (b, jnp.maximum(i - nc, 0), 0)),
                  pl.BlockSpec((1, d), lambda b, i: (0, 0)),
                  vec, vec, vec, vec,
                  pl.BlockSpec((d, n), lambda b, i: (0, 0))],
        out_specs=[pl.BlockSpec((1, tm, n_first), lambda b, i: (b, i, 0)),
                   pl.BlockSpec((1, tm, n - n_first), lambda b, i: (b, i, 0))],
        out_shape=[jax.ShapeDtypeStruct((bsz, t_ctx + t_lat, n_first), F32),
                   jax.ShapeDtypeStruct((bsz, t_ctx + t_lat, n - n_first), F32)],
        compiler_params=_cparams(("parallel", "parallel")),
        name="mod_matmul",
    )(cx, lat, gain.reshape(1, d), mc[3], mc[4], ml[3], ml[4], w)


def _rwkv_feat_kernel(p_ref, pu_ref, pd_ref, mu_ref, wup_ref, w0_ref, aup_ref, a0_ref, gup_ref,
                      ka_ref, rk_ref, ones_ref,
                      v_o, g_o, bonus_o, r_o, k_o, w_o, a_o, *, nt, width):
    i = pl.program_id(1)
    is_ctx = i == 0
    p = p_ref[0]
    tm = p.shape[0]
    row = lax.broadcasted_iota(jnp.int32, p.shape, 0)
    lane = lax.broadcasted_iota(jnp.int32, p.shape, 1) % 4
    prev = pltpu.roll(p, 1, 0)
    nxt = pltpu.roll(p, tm - 1, 0)
    col = jnp.where(is_ctx, row, row % GRID_W)
    last = jnp.where(is_ctx, tm - 1, GRID_W - 1)
    left = jnp.where(col == 0, 0.0, prev)
    right = jnp.where(col == last, 0.0, nxt)
    up_halo = jnp.where(i > 1, pu_ref[0], 0.0)
    dn_halo = jnp.where(i < nt - 1, pd_ref[0], 0.0)
    up = jnp.where(is_ctx, left, jnp.concatenate([up_halo, p[:tm - GRID_W]], axis=0))
    down = jnp.where(is_ctx, right, jnp.concatenate([p[GRID_W:], dn_halo], axis=0))
    shifted = jnp.where(lane == 0, left, jnp.where(lane == 1, right, jnp.where(lane == 2, up, down)))
    q = p + mu_ref[...] * (shifted - p)

    w = width
    r, k, v = q[:, :w], q[:, w:2 * w], q[:, 2 * w:3 * w]
    wd = q[:, 3 * w:3 * w + LANES]
    ad = q[:, 3 * w + LANES:3 * w + 2 * LANES]
    gd = q[:, 3 * w + 2 * LANES:3 * w + 3 * LANES]

    zlin = w0_ref[...] + _dot(jnp.tanh(wd).astype(BF16), wup_ref[...])
    neg = -zlin
    softplus = jnp.maximum(neg, 0.0) + jnp.log(1.0 + jnp.exp(-jnp.abs(neg)))
    decay = jnp.exp(-jnp.exp(-softplus - 0.5))
    a = _sigmoid(a0_ref[...] + _dot(ad.astype(BF16), aup_ref[...]))
    g_o[0] = _dot(_sigmoid(gd).astype(BF16), gup_ref[...])

    ones = ones_ref[...]
    for d in range(2):
        v_o[d, 0] = v
        r_o[d, 0] = r
        k_o[d, 0] = k
    ksum = jnp.zeros_like(k)
    for d in range(2):
        a_d = a[:, d * w:(d + 1) * w]
        ksum = ksum + k * (1.0 + (a_d - 1.0) * ka_ref[...])
        w_o[d, 0] = decay[:, d * w:(d + 1) * w]
        a_o[d, 0] = a_d
    bonus_o[0] = _dot_exact(r * ksum * rk_ref[...], ones) * v


def rwkv_features(p, n_cols, width, consts, tm):
    bsz, t, _ = p.shape
    nt = t // tm
    hb = tm // GRID_W
    nhb = t // GRID_W
    w = width
    full = lambda a: pl.BlockSpec(a.shape, lambda b, i: (0,) * a.ndim)
    tok = pl.BlockSpec((1, tm, w), lambda b, i: (b, i, 0))
    tok2 = pl.BlockSpec((2, 1, tm, w), lambda b, i: (0, b, i, 0))
    sds = jax.ShapeDtypeStruct((bsz, t, w), F32)
    sds2 = jax.ShapeDtypeStruct((2, bsz, t, w), F32)
    return pl.pallas_call(
        functools.partial(_rwkv_feat_kernel, nt=nt, width=w),
        grid=(bsz, nt),
        in_specs=[pl.BlockSpec((1, tm, n_cols), lambda b, i: (b, i, 0)),
                  pl.BlockSpec((1, GRID_W, n_cols), lambda b, i: (b, jnp.maximum(i * hb - 1, 0), 0)),
                  pl.BlockSpec((1, GRID_W, n_cols), lambda b, i: (b, jnp.minimum((i + 1) * hb, nhb - 1), 0))]
                 + [full(a) for a in consts],
        out_specs=[tok2, tok, tok, tok2, tok2, tok2, tok2],
        out_shape=[sds2, sds, sds, sds2, sds2, sds2, sds2],
        compiler_params=_cparams(("parallel", "parallel")),
        name="rwkv_features",
    )(p, p, p, *consts)


def _rwkv_scan_kernel(rf, rb, kf, kb, wf, wb, af, ab, vf, vb, ka_ref, kks_ref, of_ref, ob_ref,
                      s_ref, vec_ref, v_ref, c_ref, look_ref, *, tc):
    @pl.when(pl.program_id(0) == 0)
    def _():
        s_ref[...] = jnp.zeros_like(s_ref)
        look_ref[:, tc] = jnp.zeros((look_ref.shape[0],) + look_ref.shape[2:], F32)

    nj, ni = s_ref.shape[0], s_ref.shape[1]
    quarter = LANES // 4

    def fold(x):
        hi = (lax.broadcasted_iota(jnp.int32, x.shape, 1) % (2 * quarter)) >= quarter
        return x + jnp.where(hi, pltpu.roll(x, quarter, 1), pltpu.roll(x, 3 * quarter, 1))

    def hsum(x):
        return jnp.sum(x, axis=0, keepdims=True)

    def jsum(x):
        return fold(jnp.broadcast_to(hsum(x), (SUBLANES, LANES)))[0:1]

    ka = ka_ref[...]
    kks = kks_ref[...]
    fwd_lane = lax.broadcasted_iota(jnp.int32, (nj, LANES), 1) < LANES // 2
    def merged(f, b, t):
        return jnp.where(fwd_lane, f[t], b[tc - 1 - t])

    for t in range(tc):
        kk = merged(kf, kb, t) * kks
        look_ref[0, t] = kk / jnp.maximum(jnp.sqrt(jsum(kk * kk)), 1e-12)
        look_ref[1, t] = merged(wf, wb, t) * merged(rf, rb, t)
    for t in range(tc):
        r, k = merged(rf, rb, t), merged(kf, kb, t)
        w, a = merged(wf, wb, t), merged(af, ab, t)
        kk, kk_next, wr_next = look_ref[0, t], look_ref[0, t + 1], look_ref[1, t + 1]
        kka = kk * a
        kd = k * (1.0 + (a - 1.0) * ka)
        vec_ref[0, t] = w
        vec_ref[1, t] = kka
        vec_ref[2, t] = kd
        vec_ref[3, t] = w * kk_next
        vec_ref[4, t] = w * wr_next
        c_ref[t, 0:1, :] = jsum(kka * r)
        c_ref[t, 1:2, :] = jsum(kd * r)
        c_ref[t, 2:3, :] = hsum(kka * kk_next)
        c_ref[t, 3:4, :] = hsum(kd * kk_next)
        c_ref[t, 4:5, :] = hsum(kka * wr_next)
        c_ref[t, 5:6, :] = hsum(kd * wr_next)
        v_ref[t] = jnp.concatenate([vf[t], vb[tc - 1 - t]], axis=-1)

    def state_sums(y1, y2):
        s1 = jnp.zeros((ni, LANES), F32)
        s2 = jnp.zeros((ni, LANES), F32)
        for j in range(nj):
            sj = s_ref[j]
            s1 = s1 + sj * y1(j)
            s2 = s2 + sj * y2(j)
        return s1, s2

    def step(t, carry):
        row = lambda n: (lambda j: vec_ref[n, t, j:j + 1, :])
        a1, a2 = state_sums(row(3), row(4))
        sa, o1 = fold(carry[0]), fold(carry[1])
        v = v_ref[t]
        out = o1 - sa * c_ref[t, 0:1, :] + v * c_ref[t, 1:2, :]
        of_ref[t] = out
        ob_ref[tc - 1 - t] = out
        nxt = (a1 - sa * c_ref[t, 2:3, :] + v * c_ref[t, 3:4, :],
               a2 - sa * c_ref[t, 4:5, :] + v * c_ref[t, 5:6, :])
        for j in range(nj):
            s_ref[j] = s_ref[j] * row(0)(j) - sa * row(1)(j) + v * row(2)(j)
        return nxt

    first = state_sums(lambda j: look_ref[0, 0, j:j + 1, :], lambda j: look_ref[1, 0, j:j + 1, :])
    lax.fori_loop(0, tc, step, first)


def rwkv_scan(r, k, w, a, v, ka, kks, t_ctx):
    ttot, nj, _ = r.shape
    ni, nc = v.shape[1], v.shape[2]
    tc = 32
    ncb, ntb = t_ctx // tc, ttot // tc
    fwd = lambda g: (g, 0, 0)
    bwd = lambda g: (jnp.where(g < ncb, ncb - 1 - g, ntb + ncb - 1 - g), 0, 0)
    jf, jb = pl.BlockSpec((tc, nj, LANES), fwd), pl.BlockSpec((tc, nj, LANES), bwd)
    vf, vb = pl.BlockSpec((tc, ni, nc), fwd), pl.BlockSpec((tc, ni, nc), bwd)
    of, ob = pl.BlockSpec((tc, ni, LANES), fwd), pl.BlockSpec((tc, ni, LANES), bwd)
    osd = jax.ShapeDtypeStruct((ttot, ni, LANES), F32)
    return pl.pallas_call(
        functools.partial(_rwkv_scan_kernel, tc=tc),
        grid=(ntb,),
        in_specs=[jf, jb] * 4 + [vf, vb, pl.BlockSpec(ka.shape, lambda g: (0, 0)),
                                 pl.BlockSpec(kks.shape, lambda g: (0, 0))],
        out_specs=[of, ob],
        out_shape=[osd, osd],
        scratch_shapes=[pltpu.VMEM((nj, ni, LANES), F32), pltpu.VMEM((5, tc, nj, LANES), F32),
                        pltpu.VMEM((tc, ni, LANES), F32), pltpu.VMEM((tc, SUBLANES, LANES), F32),
                        pltpu.VMEM((2, tc + 1, nj, LANES), F32)],
        compiler_params=_cparams(("arbitrary",)),
        name="rwkv_scan",
    )(r, r, k, k, w, w, a, a, v, v, ka, kks)


def _s5_kernel(uf_ref, ub_ref, bm_ref, cm_ref, a_ref, yf_ref, yb_ref, h_ref, bu_ref, ubr_ref, *, tt):
    @pl.when(pl.program_id(0) == 0)
    def _():
        h_ref[...] = jnp.zeros_like(h_ref)

    nsb = bm_ref.shape[0]
    kin = bm_ref.shape[1] // 2
    sw = bm_ref.shape[2]
    hw = sw // 2
    for t in range(tt):
        ubr_ref[t] = ub_ref[tt - 1 - t]
    nothing = jnp.zeros(uf_ref.shape, F32)
    uf = jnp.concatenate([uf_ref[...], nothing], axis=1)
    ub = jnp.concatenate([nothing, ubr_ref[...]], axis=1)
    for sb in range(nsb):
        lhs = jnp.concatenate([uf[:, :, sb * kin:(sb + 1) * kin], ub[:, :, sb * kin:(sb + 1) * kin]], axis=-1)
        lhs = lhs.reshape(tt * SUBLANES, 2 * kin).astype(BF16)
        bu_ref[:, sb * sw:(sb + 1) * sw] = _dot(lhs, bm_ref[sb])

    def step(t, hs):
        rows = pl.ds(pl.multiple_of(t * SUBLANES, SUBLANES), SUBLANES)
        out = []
        for sb in range(nsb):
            hr, hi = hs[2 * sb], hs[2 * sb + 1]
            lo = sb * sw
            ar = a_ref[0, :, lo:lo + hw]
            ai = a_ref[0, :, lo + hw:lo + sw]
            nr = ar * hr - ai * hi + bu_ref[rows, lo:lo + hw]
            ni = ar * hi + ai * hr + bu_ref[rows, lo + hw:lo + sw]
            bu_ref[rows, lo:lo + hw] = nr
            bu_ref[rows, lo + hw:lo + sw] = ni
            out += [nr, ni]
        return tuple(out)

    h0 = []
    for sb in range(nsb):
        h0 += [h_ref[:, sb * sw:sb * sw + hw], h_ref[:, sb * sw + hw:(sb + 1) * sw]]
    hs = lax.fori_loop(0, tt, step, tuple(h0))
    for sb in range(nsb):
        h_ref[:, sb * sw:sb * sw + hw] = hs[2 * sb]
        h_ref[:, sb * sw + hw:(sb + 1) * sw] = hs[2 * sb + 1]

    nout = cm_ref.shape[2] // 2
    fwd_row = (lax.broadcasted_iota(jnp.int32, (tt * SUBLANES, nout), 0) % SUBLANES) < SUBLANES // 2
    ys = []
    for sb in range(nsb):
        yy = _dot(bu_ref[:, sb * sw:(sb + 1) * sw].astype(BF16), cm_ref[sb])
        ys.append(jnp.where(fwd_row, yy[:, :nout], yy[:, nout:]))
    y = jnp.concatenate(ys, axis=1).reshape(tt, SUBLANES, nsb * nout)
    yf_ref[...] = y
    for t in range(tt):
        yb_ref[tt - 1 - t] = y[t]


def s5_scan(u, bm, cm, a, t_ctx):
    ttot, bsz, width = u.shape
    rows = 2 * bsz
    nsb, _, sw = bm.shape
    tt = 64
    ncb, ntb = t_ctx // tt, ttot // tt
    fmap = lambda g: (g, 0, 0)
    bmap = lambda g: (jnp.where(g < ncb, ncb - 1 - g, ntb + ncb - 1 - g), 0, 0)
    full = lambda x: pl.BlockSpec(x.shape, lambda g: (0,) * x.ndim)
    osd = jax.ShapeDtypeStruct((ttot, rows, width), F32)
    return pl.pallas_call(
        functools.partial(_s5_kernel, tt=tt),
        grid=(ntb,),
        in_specs=[pl.BlockSpec((tt, bsz, width), fmap), pl.BlockSpec((tt, bsz, width), bmap),
                  full(bm), full(cm), full(a)],
        out_specs=[pl.BlockSpec((tt, rows, width), fmap), pl.BlockSpec((tt, rows, width), bmap)],
        out_shape=[osd, osd],
        scratch_shapes=[pltpu.VMEM((rows, nsb * sw), F32), pltpu.VMEM((tt * rows, nsb * sw), F32),
                        pltpu.VMEM((tt, bsz, width), F32)],
        compiler_params=_cparams(("arbitrary",)),
        name="s5_scan",
    )(u, u, bm, cm, a)


def _even_out_kernel(x_ref, of_ref, ob_ref, bonus_ref, g_ref, ys_ref, u_ref, m5_ref,
                     ones_ref, gng_ref, gnb_ref, dskip_ref, wglu_ref, bglu_ref, wo1_ref, wo2_ref, o_ref):
    ones = ones_ref[...]
    inv = 1.0 / HEAD_DIM
    o = of_ref[0] + ob_ref[0]
    mean = _dot_exact(o, ones) * inv
    oc = o - mean
    var = _dot_exact(oc * oc, ones) * inv
    y1 = (oc * lax.rsqrt(var + GN_EPS) * gng_ref[...] + gnb_ref[...] + bonus_ref[0]) * g_ref[0]
    y = ys_ref[0] + dskip_ref[...] * u_ref[0]
    y = jax.nn.gelu(y)
    y2 = y * _sigmoid(_dot(y.astype(BF16), wglu_ref[...]) + bglu_ref[...])
    out = _dot(y1.astype(BF16), wo1_ref[...]) + _dot(y2.astype(BF16), wo2_ref[...])
    o_ref[0] = x_ref[0] + m5_ref[0] * out


def even_out(x, off, o_f, o_b, bonus, g, ys, u, m5, consts, tm):
    bsz, t, d = x.shape
    w = o_f.shape[2]
    tokd = pl.BlockSpec((1, tm, d), lambda b, i: (b, i, 0))
    tokw = pl.BlockSpec((1, tm, w), lambda b, i: (b, i + off, 0))
    toks = pl.BlockSpec((1, tm, u.shape[2]), lambda b, i: (b, i + off, 0))
    full = lambda a: pl.BlockSpec(a.shape, lambda b, i: (0,) * a.ndim)
    return pl.pallas_call(
        _even_out_kernel,
        grid=(bsz, t // tm),
        in_specs=[tokd, tokw, tokw, tokw, tokw, toks, toks,
                  pl.BlockSpec((1, 1, d), lambda b, i: (b, 0, 0))] + [full(a) for a in consts],
        out_specs=tokd,
        out_shape=jax.ShapeDtypeStruct((bsz, t, d), F32),
        compiler_params=_cparams(("parallel", "parallel")),
        name="even_out",
    )(x, o_f, o_b, bonus, g, ys, u, m5, *consts)


def _hy_in_kernel(x_ref, xp_ref, xn_ref, gain_ref, shift_ref, scale_ref, w_ref, cw_ref, cb_ref,
                  z_ref, zb_ref, g1_ref, g2_ref, *, nt, c):
    i = pl.program_id(1)
    gain, shift, scale = gain_ref[...], shift_ref[0], scale_ref[0]
    tm = x_ref.shape[1]
    halo = xp_ref.shape[1]
    h = jnp.concatenate([_rms_mod(xr[0], gain, shift, scale) for xr in (xp_ref, x_ref, xn_ref)], axis=0)
    h = h.astype(BF16)
    row = lax.broadcasted_iota(jnp.int32, (tm, c), 0)
    outs = (z_ref, g1_ref, g2_ref)
    for part in range(3):
        wp = w_ref[:, part * c:(part + 1) * c]
        p_ext = _dot(h, wp)
        p = p_ext[halo:halo + tm]
        pp = jnp.where(i > 0, p_ext[halo - 1:halo], 0.0)
        pn = jnp.where(i < nt - 1, p_ext[halo + tm:halo + tm + 1], 0.0)
        pm1 = jnp.where(row == 0, pp, pltpu.roll(p, 1, 0))
        pp1 = jnp.where(row == tm - 1, pn, pltpu.roll(p, tm - 1, 0))
        cw = cw_ref[:, part * c:(part + 1) * c]
        q = cw[0:1] * pm1 + cw[1:2] * p + cw[2:3] * pp1 + cb_ref[:, part * c:(part + 1) * c]
        outs[part][0] = q
        if part == 0:
            zb_ref[0] = q.astype(BF16)


def hyena_in(x, gain, shift, scale, w, conv_w, conv_b):
    bsz, t, d = x.shape
    c = w.shape[1] // 3
    tm = min(512, t)
    nt = t // tm
    hb = tm // 8
    nhb = t // 8
    vec = pl.BlockSpec((1, 1, d), lambda b, i: (b, 0, 0))
    tok = pl.BlockSpec((1, tm, c), lambda b, i: (b, i, 0))
    sds = jax.ShapeDtypeStruct((bsz, t, c), F32)
    return pl.pallas_call(
        functools.partial(_hy_in_kernel, nt=nt, c=c),
        grid=(bsz, nt),
        in_specs=[pl.BlockSpec((1, tm, d), lambda b, i: (b, i, 0)),
                  pl.BlockSpec((1, 8, d), lambda b, i: (b, jnp.maximum(i * hb - 1, 0), 0)),
                  pl.BlockSpec((1, 8, d), lambda b, i: (b, jnp.minimum((i + 1) * hb, nhb - 1), 0)),
                  pl.BlockSpec((1, d), lambda b, i: (0, 0)),
                  vec, vec,
                  pl.BlockSpec(w.shape, lambda b, i: (0, 0)),
                  pl.BlockSpec(conv_w.shape, lambda b, i: (0, 0)),
                  pl.BlockSpec((1, 3 * c), lambda b, i: (0, 0))],
        out_specs=[tok, tok, tok, tok],
        out_shape=[sds, jax.ShapeDtypeStruct((bsz, t, c), BF16), sds, sds],
        compiler_params=_cparams(("parallel", "parallel")),
        name="hyena_in",
    )(x, x, x, gain.reshape(1, d), shift, scale, w, conv_w, conv_b.reshape(1, 3 * c))


def _pack_complex(re, im):
    bits = lambda x: lax.bitcast_convert_type(x.astype(BF16).astype(F32), jnp.uint32)
    return lax.shift_right_logical(bits(re), jnp.uint32(16)) | bits(im)


def _unpack_complex(w):
    re = lax.bitcast_convert_type(lax.shift_left(w, jnp.uint32(16)), F32)
    im = lax.bitcast_convert_type(w & jnp.uint32(0xFFFF0000), F32)
    return jnp.concatenate([re, im], axis=0).astype(BF16)


def _fft_conv_kernel(z_ref, kf_ref, g_ref, w1_ref, g2i_ref, f1c_ref, twt_ref, o_ref, a_ref, t_ref,
                     *, n1, k1c):
    s = pl.program_id(2)
    n2 = FFT_N2
    nd = n1 // 2
    slab = n1 // k1c

    @pl.when(s == 0)
    def _():
        w1 = w1_ref[...]

        def body(j, carry):
            x = jnp.concatenate([z_ref[0, j], z_ref[1, j]], axis=0)
            a = _dot(w1, x)
            a_ref[pl.ds(pl.multiple_of(j * n1, n1), n1), :] = _pack_complex(a[:n1], a[n1:])
            return carry

        lax.fori_loop(0, n2, body, 0, unroll=8)

    @pl.when((s > 0) & (s <= k1c))
    def _():
        g2i = g2i_ref[...]
        base = (s - 1) * slab

        def body(kk, carry):
            k1 = base + kk
            ak = _unpack_complex(a_ref[pl.ds(k1, n2, stride=n1), :])
            x = _dot(g_ref[kk], ak)
            xr, xi = x[:n2], x[n2:]
            kr, ki = kf_ref[0, kk], kf_ref[1, kk]
            pr = xr * kr - xi * ki
            pi = xr * ki + xi * kr
            tt = _dot(g2i, jnp.concatenate([pr, pi], axis=0).astype(BF16))
            t_ref[pl.ds(pl.multiple_of(k1 * n2, n2), n2), :] = _pack_complex(tt[:n2], tt[n2:])
            return carry

        lax.fori_loop(0, slab, body, 0, unroll=4)

    @pl.when(s == k1c + 1)
    def _():
        f1r, f1i = f1c_ref[0], f1c_ref[1]

        def body(j, carry):
            twr = twt_ref[0, pl.ds(j, 1), :]
            twi = twt_ref[1, pl.ds(j, 1), :]
            wr = f1r * twr + f1i * twi
            wi = f1i * twr - f1r * twi
            w3 = jnp.concatenate([jnp.concatenate([wr, -wi], axis=1),
                                  jnp.concatenate([wi, wr], axis=1)], axis=0).astype(BF16)
            tn = _unpack_complex(t_ref[pl.ds(j, n1, stride=n2), :])
            y = _dot(w3, tn)
            o_ref[0, j] = y[:nd]
            o_ref[1, j] = y[nd:]
            return carry

        lax.fori_loop(0, n2, body, 0, unroll=8)


def _fft_consts(n1):
    n2 = FFT_N2
    n = n1 * n2
    nd = n1 // 2
    k1 = np.arange(n1)
    f1 = np.exp(-2j * np.pi * np.outer(k1, np.arange(nd)) / n1)
    w1 = np.block([[f1.real, -f1.imag], [f1.imag, f1.real]])
    f2 = np.exp(-2j * np.pi * np.outer(np.arange(n2), np.arange(n2)) / n2)
    f2c = np.conj(f2)
    g2i = np.block([[f2c.real, -f2c.imag], [f2c.imag, f2c.real]])
    f1c = np.exp(2j * np.pi * np.outer(np.arange(nd), k1) / n1) / n
    tw = np.exp(-2j * np.pi * np.outer(k1, np.arange(n2)) / n)
    cplx = lambda m: jnp.asarray(np.stack([m.real, m.imag]), F32)
    (f2r, f2i), (twr, twi) = cplx(f2), cplx(tw)
    gr = f2r[None] * twr[:, None, :] - f2i[None] * twi[:, None, :]
    gi = f2r[None] * twi[:, None, :] + f2i[None] * twr[:, None, :]
    gtw = jnp.concatenate([jnp.concatenate([gr, -gi], axis=2), jnp.concatenate([gi, gr], axis=2)], axis=1)
    return (gtw.astype(BF16), jnp.asarray(w1, BF16), jnp.asarray(g2i, BF16), cplx(f1c), cplx(tw.T))


def fft_conv(zt, kf):
    bsz, n2, nd, c = zt.shape
    n1 = 2 * nd
    cb = LANES
    k1c = 8 if n1 % 8 == 0 and n1 >= 64 else 2
    slab = n1 // k1c
    gtw, *consts = _fft_consts(n1)
    full = lambda a: pl.BlockSpec(a.shape, lambda j, p, s: (0,) * a.ndim)
    blk = pl.BlockSpec((2, n2, nd, cb), lambda j, p, s: (p, 0, 0, j))
    chunk = lambda j, p, s: jnp.clip(s - 1, 0, k1c - 1)
    return pl.pallas_call(
        functools.partial(_fft_conv_kernel, n1=n1, k1c=k1c),
        grid=(c // cb, bsz // 2, k1c + 2),
        in_specs=[blk,
                  pl.BlockSpec((2, slab, n2, cb), lambda j, p, s: (0, chunk(j, p, s), 0, j)),
                  pl.BlockSpec((slab, 2 * n2, 2 * n2), lambda j, p, s: (chunk(j, p, s), 0, 0))]
                 + [full(a) for a in consts],
        out_specs=blk,
        out_shape=jax.ShapeDtypeStruct(zt.shape, F32),
        scratch_shapes=[pltpu.VMEM((n2 * n1, cb), jnp.uint32), pltpu.VMEM((n1 * n2, cb), jnp.uint32)],
        compiler_params=_cparams(("parallel", "parallel", "arbitrary")),
        name="fft_conv",
    )(zt, kf, gtw, *consts)


def _fft_fwd_kernel(z_ref, scale_ref, g_ref, w1_ref, o_ref, a_ref, *, n1, k1c):
    s = pl.program_id(2)
    n2 = FFT_N2
    slab = n1 // k1c
    cb = o_ref.shape[4]

    @pl.when(s == 0)
    def _():
        w1 = w1_ref[...]

        def body(j, carry):
            a = _dot(w1, jnp.concatenate([z_ref[0, j], z_ref[1, j]], axis=1))
            packed = _pack_complex(a[:n1], a[n1:])
            rows = pl.ds(pl.multiple_of(j * n1, n1), n1)
            a_ref[0, rows, :] = packed[:, :cb]
            a_ref[1, rows, :] = packed[:, cb:]
            return carry

        lax.fori_loop(0, n2, body, 0, unroll=8)

    @pl.when(s > 0)
    def _():
        base = (s - 1) * slab

        def body(kk, carry):
            k1 = base + kk
            rows = pl.ds(k1, n2, stride=n1)
            ak = jnp.concatenate([_unpack_complex(a_ref[0, rows, :]), _unpack_complex(a_ref[1, rows, :])], axis=1)
            x = _dot(g_ref[kk], ak)
            pos, neg = x[:, :cb], x[:, cb:]
            scale = scale_ref[0]
            o_ref[0, 0, kk] = (pos[:n2] + neg[:n2]) * scale
            o_ref[0, 1, kk] = (pos[n2:] - neg[n2:]) * scale
            return carry

        lax.fori_loop(0, slab, body, 0, unroll=4)


def filter_spectrum(taps, scale):
    orders, _, n2, nd, c = taps.shape
    n1 = 2 * nd
    cb = LANES
    k1c = 8 if n1 % 8 == 0 and n1 >= 64 else 2
    slab = n1 // k1c
    gtw, w1 = _fft_consts(n1)[:2]
    w1 = w1[:, :nd]
    full = lambda a: pl.BlockSpec(a.shape, lambda j, r, s: (0,) * a.ndim)
    return pl.pallas_call(
        functools.partial(_fft_fwd_kernel, n1=n1, k1c=k1c),
        grid=(c // cb, orders, k1c + 1),
        in_specs=[pl.BlockSpec((None, 2, n2, nd, cb), lambda j, r, s: (r, 0, 0, 0, j)),
                  pl.BlockSpec((1, 1, cb), lambda j, r, s: (r, 0, j)),
                  pl.BlockSpec((slab, 2 * n2, 2 * n2), lambda j, r, s: (jnp.maximum(s - 1, 0), 0, 0)),
                  full(w1)],
        out_specs=pl.BlockSpec((1, 2, slab, n2, cb), lambda j, r, s: (r, 0, jnp.maximum(s - 1, 0), 0, j)),
        out_shape=jax.ShapeDtypeStruct((orders, 2, n1, n2, c), F32),
        scratch_shapes=[pltpu.VMEM((2, n2 * n1, cb), jnp.uint32)],
        compiler_params=_cparams(("parallel", "parallel", "arbitrary")),
        name="fft_forward",
    )(taps, scale, gtw, w1)


def _hy_filter_kernel(feat_ref, fw1_ref, fb1_ref, fw2_ref, fb2_ref, fw3_ref, fb3_ref, fw4_ref, freq_ref,
                      delta_ref, z_ref, ss_ref):
    @pl.when(pl.program_id(1) == 0)
    def _():
        ss_ref[...] = jnp.zeros_like(ss_ref)

    feats = feat_ref[...]
    t = feats[:, 0:1]
    fr = freq_ref[...]
    h = jnp.sin(fr * (_dot_exact(feats, fw1_ref[...]) + fb1_ref[...]))
    h = jnp.sin(fr * (_dot_exact(h, fw2_ref[...]) + fb2_ref[...]))
    h = jnp.sin(fr * (_dot_exact(h, fw3_ref[...]) + fb3_ref[...]))
    lag0 = jnp.where(pl.program_id(0) % 2 == 1, 0.0, 1.0)
    filt = _dot(h.astype(BF16), fw4_ref[...]) * (jnp.exp(-t * delta_ref[...]) * jnp.where(t == 0.0, lag0, 1.0))
    z_ref[0] = filt.astype(BF16)
    ss_ref[0] += jnp.sum(filt * filt, axis=0, keepdims=True)


def hyena_filter(t, fw1, fb1, fw2, fb2, fw3, fb3, fw4, freq, width):
    nd = t // FFT_N2
    pos = (jnp.arange(FFT_N2, dtype=F32)[:, None] + FFT_N2 * jnp.arange(nd, dtype=F32)[None, :]).reshape(-1, 1)
    tt = pos / max(t - 1, 1)
    ang = 2 * math.pi * pos / t
    nb = (HY_EMB - 1) // 2
    bands = jnp.linspace(1e-4, nb - 1, nb, dtype=F32)[None]
    feats = jnp.concatenate([tt, jnp.cos(bands * ang), -jnp.sin(bands * ang),
                             jnp.zeros((t, LANES - HY_EMB), F32)], axis=-1)
    deltas = jnp.abs(jnp.linspace(HY_MIN_DECAY, HY_MAX_DECAY, width, dtype=F32)).reshape(1, width)
    rows = 2 * HY_ORDER
    tm = min(512, t)
    hid = fw1.shape[1]
    full = lambda a: pl.BlockSpec(a.shape, lambda r, i: (0,) * a.ndim)
    fw1p = jnp.concatenate([fw1, jnp.zeros((LANES - HY_EMB, hid), F32)], axis=0)
    vecs = [fw1p, fb1.reshape(1, hid), fw2, fb2.reshape(1, hid), fw3, fb3.reshape(1, hid)]
    return pl.pallas_call(
        _hy_filter_kernel,
        grid=(rows, t // tm),
        in_specs=[pl.BlockSpec((tm, LANES), lambda r, i: (i, 0))] + [full(a) for a in vecs]
                 + [pl.BlockSpec((hid, width), lambda r, i: (0, r)), full(freq.reshape(1, hid)), full(deltas)],
        out_specs=[pl.BlockSpec((1, tm, width), lambda r, i: (r, i, 0)),
                   pl.BlockSpec((1, 1, width), lambda r, i: (r, 0, 0))],
        out_shape=[jax.ShapeDtypeStruct((rows, t, width), BF16), jax.ShapeDtypeStruct((rows, 1, width), F32)],
        compiler_params=_cparams(("parallel", "arbitrary")),
        name="hyena_filter",
    )(feats, *vecs, fw4.astype(BF16), freq.reshape(1, hid), deltas)


def hyena_filter_spectrum_fft(t, fw1, fb1, fw2, fb2, fw3, fb3, fw4, freq, width):
    taps, ss = hyena_filter(t, fw1, fb1, fw2, fb2, fw3, fb3, fw4, freq, width)
    nd = t // FFT_N2
    scale = lax.rsqrt(ss.reshape(HY_ORDER, 2, width).sum(axis=1, keepdims=True) + 1e-6)
    return filter_spectrum(taps.reshape(HY_ORDER, 2, FFT_N2, nd, width), scale)


def _dft_conv_kernel(z_ref, kf_ref, fw_ref, iv_ref, o_ref, *, t):
    x = jnp.concatenate([z_ref[0], z_ref[1]], axis=0)
    spec = _dot(fw_ref[...], x)
    xr, xi = spec[:2 * t], spec[2 * t:]
    kr, ki = kf_ref[0], kf_ref[1]
    pr = xr * kr - xi * ki
    pi = xr * ki + xi * kr
    y = _dot(iv_ref[...], jnp.concatenate([pr, pi], axis=0).astype(BF16))
    o_ref[0] = y[:t]
    o_ref[1] = y[t:]


def dft_conv(zb, kf):
    bsz, t, c = zb.shape
    n = 2 * t
    f = np.exp(-2j * np.pi * np.outer(np.arange(n), np.arange(t)) / n)
    fw = np.block([[f.real, -f.imag], [f.imag, f.real]])
    fi = np.exp(2j * np.pi * np.outer(np.arange(t), np.arange(n)) / n) / n
    iv = np.block([[fi.real, -fi.imag], [fi.imag, fi.real]])
    fw, iv = jnp.asarray(fw, BF16), jnp.asarray(iv, BF16)
    cb = LANES
    blk = pl.BlockSpec((2, t, cb), lambda j, p: (p, 0, j))
    return pl.pallas_call(
        functools.partial(_dft_conv_kernel, t=t),
        grid=(c // cb, bsz // 2),
        in_specs=[blk, pl.BlockSpec((2, n, cb), lambda j, p: (0, 0, j)),
                  pl.BlockSpec(fw.shape, lambda j, p: (0, 0)), pl.BlockSpec(iv.shape, lambda j, p: (0, 0))],
        out_specs=blk,
        out_shape=jax.ShapeDtypeStruct(zb.shape, F32),
        compiler_params=_cparams(("parallel", "parallel")),
        name="dft_conv",
    )(zb, kf, fw, iv)


def _conv_rows(conv_ref):
    if len(conv_ref.shape) == 3:
        return conv_ref[0]
    return jnp.concatenate([conv_ref[0, :, k, :] for k in range(conv_ref.shape[2])], axis=0)


def _conv_spec(conv, tm, c):
    if conv.ndim == 3:
        return pl.BlockSpec((1, tm, c), lambda b, i: (b, i, 0))
    return pl.BlockSpec((1, FFT_N2, tm // FFT_N2, c), lambda b, i: (b, 0, i, 0))


def _hy_gate_kernel(conv_ref, z_ref, gate_ref, bias_ref, o_ref, ob_ref):
    y = gate_ref[0] * (_conv_rows(conv_ref) + bias_ref[...] * z_ref[0])
    o_ref[0] = y
    ob_ref[0] = y.astype(BF16)


def hyena_gate(conv, z, gate, bias):
    bsz, t, c = z.shape
    tm = min(SUBLANES * FFT_N2, t)
    tok = pl.BlockSpec((1, tm, c), lambda b, i: (b, i, 0))
    return pl.pallas_call(
        _hy_gate_kernel,
        grid=(bsz, t // tm),
        in_specs=[_conv_spec(conv, tm, c), tok, tok, pl.BlockSpec((1, c), lambda b, i: (0, 0))],
        out_specs=[tok, tok],
        out_shape=[jax.ShapeDtypeStruct(z.shape, F32), jax.ShapeDtypeStruct(z.shape, BF16)],
        compiler_params=_cparams(("parallel", "parallel")),
        name="hyena_gate",
    )(conv, z, gate, bias.reshape(1, c))


def _hy_out_kernel(x_ref, conv_ref, z_ref, gate_ref, bias_ref, w_ref, m5_ref, o_ref):
    y = gate_ref[0] * (_conv_rows(conv_ref) + bias_ref[...] * z_ref[0])
    o_ref[0] = x_ref[0] + m5_ref[0] * _dot(y.astype(BF16), w_ref[...])


def hyena_out(x, conv, z, gate, bias, w, m5):
    bsz, t, d = x.shape
    c = z.shape[2]
    tm = min(SUBLANES * FFT_N2, t)
    tokd = pl.BlockSpec((1, tm, d), lambda b, i: (b, i, 0))
    tokc = pl.BlockSpec((1, tm, c), lambda b, i: (b, i, 0))
    return pl.pallas_call(
        _hy_out_kernel,
        grid=(bsz, t // tm),
        in_specs=[tokd, _conv_spec(conv, tm, c), tokc, tokc, pl.BlockSpec((1, c), lambda b, i: (0, 0)),
                  pl.BlockSpec(w.shape, lambda b, i: (0, 0)), pl.BlockSpec((1, 1, d), lambda b, i: (b, 0, 0))],
        out_specs=tokd,
        out_shape=jax.ShapeDtypeStruct(x.shape, F32),
        compiler_params=_cparams(("parallel", "parallel")),
        name="hyena_out",
    )(x, conv, z, gate, bias.reshape(1, c), w, m5)


def _hyena_filter_spectrum(t, fw1, fb1, fw2, fb2, fw3, fb3, fw4, freq, width):
    pos = jnp.arange(t, dtype=F32)[:, None]
    tt = pos / max(t - 1, 1)
    ang = 2 * math.pi * pos / t
    nb = (HY_EMB - 1) // 2
    bands = jnp.linspace(1e-4, nb - 1, nb, dtype=F32)[None]
    feats = jnp.concatenate([tt, jnp.cos(bands * ang), -jnp.sin(bands * ang)], axis=-1)
    hdn = jnp.sin(freq * (feats @ fw1 + fb1))
    hdn = jnp.sin(freq * (hdn @ fw2 + fb2))
    hdn = jnp.sin(freq * (hdn @ fw3 + fb3))
    filt = (hdn @ fw4).reshape(t, HY_ORDER, 2, width)
    deltas = jnp.abs(jnp.linspace(HY_MIN_DECAY, HY_MAX_DECAY, width, dtype=F32))
    filt = filt * jnp.exp(-tt[:, :, None, None] * deltas)
    fwd, bwd = filt[:, :, 0], filt[:, :, 1]
    kern = jnp.concatenate([fwd, jnp.zeros_like(fwd[:1]), bwd[:0:-1]], axis=0)
    kern = kern * lax.rsqrt(jnp.sum(kern * kern, axis=0, keepdims=True) + 1e-6)
    spec = jnp.fft.fft(kern, axis=0)
    return jnp.stack([jnp.real(spec), jnp.imag(spec)], axis=1).transpose(2, 1, 0, 3).astype(F32)


def _block_diag_pair(m):
    z = jnp.zeros_like(m[0])
    return jnp.concatenate([jnp.concatenate([m[0], z], axis=1), jnp.concatenate([z, m[1]], axis=1)], axis=0)


def _even_mixer(lat, cx, ml, mc, gain, prm, ctx_out):
    (w_in, mu, w0, w_up, a0, a_up, g_up, k_k, k_a, r_k, gn_g, gn_b,
     lam_re, lam_im, log_dt, b_re, b_im, c_re, c_im, d_skip, w_glu, b_glu, w_out) = prm
    bsz, t_lat, d = lat.shape
    t_ctx = cx.shape[1]
    width = k_k.shape[0]
    heads = width // HEAD_DIM
    n_cols = mu.shape[0]
    s5w = d_skip.shape[0]
    ttot = t_ctx + t_lat
    nch = bsz * heads
    tm = t_ctx
    assert tm % GRID_W == 0 and t_lat % tm == 0 and 4 * nch == LANES and 2 * bsz == SUBLANES

    p_all, u_all = mod_matmul_stream(cx, lat, gain, mc, ml, w_in.astype(BF16), tm, n_cols)

    head_of = np.arange(width) // HEAD_DIM
    ones = jnp.asarray(head_of[:, None] == head_of[None, :], F32)
    feat_consts = (mu.reshape(1, -1), _block_diag_pair(w_up).astype(BF16), w0.reshape(1, -1),
                   _block_diag_pair(a_up).astype(BF16), a0.reshape(1, -1), g_up.astype(BF16),
                   k_a.reshape(1, -1), r_k.reshape(1, -1), ones)
    v2, g, bonus, r2, k2, w2, a2 = rwkv_features(p_all, n_cols, width, feat_consts, tm)

    def key_major(x):
        x = x.reshape(2, bsz, ttot, heads, 2, HALF_HEAD)
        return x.transpose(2, 5, 0, 4, 1, 3).reshape(ttot, HALF_HEAD, LANES)

    def value_major(x):
        x = x.reshape(2, bsz, ttot, heads, HEAD_DIM)
        return x.transpose(2, 4, 0, 1, 3).reshape(ttot, HEAD_DIM, 2 * nch)

    def key_const(x):
        x = x.reshape(heads, 2, HALF_HEAD).transpose(2, 1, 0)[:, None, :, None, :]
        return jnp.broadcast_to(x, (HALF_HEAD, 2, 2, bsz, heads)).reshape(HALF_HEAD, LANES)

    o_f, o_b = rwkv_scan(key_major(r2), key_major(k2), key_major(w2), key_major(a2), value_major(v2),
                         key_const(k_a), key_const(k_k), t_ctx)

    def token_major(o, lane0):
        o = o[..., lane0:lane0 + nch].reshape(ttot, HEAD_DIM, bsz, heads)
        return o.transpose(2, 0, 3, 1).reshape(bsz, ttot, width)

    o_f, o_b = token_major(o_f, 0), token_major(o_b, LANES // 2)

    ng = lam_re.shape[1]
    gps = ng // S5_SUPER
    lam = lax.complex(lam_re, lam_im)
    dt = jnp.exp(log_dt)[..., None]
    a_bar = jnp.exp(lam * dt)
    b_bar = ((a_bar - 1) / lam)[..., None] * lax.complex(b_re, b_im)
    eye = jnp.eye(gps, dtype=F32)

    def b_mat(x):
        x = x.reshape(2, S5_SUPER, gps, S5_STATE, S5_GROUP)
        m = jnp.einsum('dsgph,gk->sdghkp', x, eye)
        return m.reshape(S5_SUPER, 2 * gps * S5_GROUP, gps * S5_STATE)

    def c_mat(x):
        x = x.reshape(2, S5_SUPER, gps, S5_GROUP, S5_STATE)
        m = jnp.einsum('dsghp,gk->skpdgh', x, eye)
        return m.reshape(S5_SUPER, gps * S5_STATE, 2 * gps * S5_GROUP)

    bm = jnp.concatenate([b_mat(jnp.real(b_bar)), b_mat(jnp.imag(b_bar))], axis=2).astype(BF16)
    cm = jnp.concatenate([c_mat(c_re), -c_mat(c_im)], axis=1).astype(BF16)

    def a_rows(x):
        x = x.reshape(2, 1, S5_SUPER, gps * S5_STATE)
        return jnp.broadcast_to(x, (2, bsz, S5_SUPER, gps * S5_STATE)).reshape(2 * bsz, S5_SUPER, -1)

    a_arr = jnp.concatenate([a_rows(jnp.real(a_bar)), a_rows(jnp.imag(a_bar))], axis=2)
    a_arr = a_arr.reshape(1, 2 * bsz, -1)

    y_f, y_b = s5_scan(u_all.transpose(1, 0, 2), bm, cm, a_arr, t_ctx)
    ys = (y_f[:, :bsz] + y_b[:, bsz:]).transpose(1, 0, 2)

    wo = w_out.astype(BF16)
    out_consts = (ones, gn_g.reshape(1, -1), gn_b.reshape(1, -1), d_skip.reshape(1, -1),
                  w_glu.astype(BF16), b_glu.reshape(1, -1), wo[:width], wo[width:])
    lat = even_out(lat, 1, o_f, o_b, bonus, g, ys, u_all, ml[5], out_consts, tm)
    if ctx_out:
        cx = even_out(cx, 0, o_f, o_b, bonus, g, ys, u_all, mc[5], out_consts, tm)
    return lat, cx


def _hyena_mixer(x, gain, m, prm):
    (w_in, conv_w, conv_b, fw1, fb1, fw2, fb2, fw3, fb3, fw4, freq, bias_d, w_out) = prm
    bsz, t, d = x.shape
    c = w_out.shape[0]
    use_fft = t % (FFT_N2 * 2) == 0 and t >= 4 * FFT_N2
    if use_fft:
        kf = hyena_filter_spectrum_fft(t, fw1, fb1, fw2, fb2, fw3, fb3, fw4, freq, c)
    else:
        kf = _hyena_filter_spectrum(t, fw1, fb1, fw2, fb2, fw3, fb3, fw4, freq, c)
    z, zb, g1, g2 = hyena_in(x, gain, m[3], m[4], w_in.astype(BF16), conv_w, conv_b)
    gates = (g1, g2)
    for n in range(HY_ORDER):
        if use_fft:
            nd = t // FFT_N2
            zt = zb.reshape(bsz, nd, FFT_N2, c).transpose(0, 2, 1, 3)
            conv = fft_conv(zt, kf[n])
        else:
            conv = dft_conv(zb, kf[n])
        if n < HY_ORDER - 1:
            z, zb = hyena_gate(conv, z, gates[n], bias_d[n])
        else:
            return hyena_out(x, conv, z, gates[n], bias_d[n], w_out.astype(BF16), m[5])


def kernel(x, c, ctx, c_ctx, norm_g, ada_w, ada_b, ffn_wg, ffn_wu, ffn_wd, final_g, ev_w_in, ev_mu, ev_w0, ev_w_up, ev_a0, ev_a_up, ev_g_up, ev_k_k, ev_k_a, ev_r_k, ev_gn_g, ev_gn_b, ev_lam_re, ev_lam_im, ev_log_dt, ev_b_re, ev_b_im, ev_c_re, ev_c_im, ev_d, ev_w_glu, ev_b_glu, ev_w_out, od_w_in, od_conv_w, od_conv_b, od_fw1, od_fb1, od_fw2, od_fb2, od_fw3, od_fb3, od_fw4, od_freq, od_bias, od_w_out):
    depth = norm_g.shape[0]
    bsz, _, d = x.shape
    n_even = (depth + 1) // 2
    last_ctx = 2 * (n_even - 1)

    cond8 = jnp.concatenate([c, c_ctx[None], jnp.zeros((8 - bsz - 1, d), F32)], axis=0)
    mods = ada_mods_all(cond8, ada_w, ada_b)

    wg, wu, wd = ffn_wg.astype(BF16), ffn_wu.astype(BF16), ffn_wd.astype(BF16)
    lat, cx = x, ctx
    for l in range(depth):
        run_ctx = l <= last_ctx
        ctx_out = l < last_ctx
        i = l // 2
        ml = [mods[l, :bsz, None, k * d:(k + 1) * d] for k in range(N_MOD)]
        mc = [jnp.broadcast_to(mods[l, bsz:bsz + 1, None, k * d:(k + 1) * d], (bsz, 1, d))
              for k in range(N_MOD)]
        lat = ffn_half(lat, norm_g[l, 0], ml[0], ml[1], ml[2], wg[l, 0], wu[l, 0], wd[l, 0])
        if run_ctx:
            cx = ffn_half(cx, norm_g[l, 0], mc[0], mc[1], mc[2], wg[l, 0], wu[l, 0], wd[l, 0])
        if l % 2 == 0:
            prm = (ev_w_in[i], ev_mu[i], ev_w0[i], ev_w_up[i], ev_a0[i], ev_a_up[i], ev_g_up[i],
                   ev_k_k[i], ev_k_a[i], ev_r_k[i], ev_gn_g[i], ev_gn_b[i],
                   ev_lam_re[i], ev_lam_im[i], ev_log_dt[i], ev_b_re[i], ev_b_im[i], ev_c_re[i], ev_c_im[i],
                   ev_d[i], ev_w_glu[i], ev_b_glu[i], ev_w_out[i])
            lat, cx = _even_mixer(lat, cx, ml, mc, norm_g[l, 1], prm, ctx_out)
        else:
            prm = (od_w_in[i], od_conv_w[i], od_conv_b[i], od_fw1[i], od_fb1[i], od_fw2[i], od_fb2[i],
                   od_fw3[i], od_fb3[i], od_fw4[i], od_freq[i], od_bias[i], od_w_out[i])
            lat = _hyena_mixer(lat, norm_g[l, 1], ml, prm)
            if ctx_out:
                cx = _hyena_mixer(cx, norm_g[l, 1], mc, prm)
        fin = final_g if l == depth - 1 else None
        lat = ffn_half(lat, norm_g[l, 2], ml[6], ml[7], ml[8], wg[l, 1], wu[l, 1], wd[l, 1], fin)
        if ctx_out:
            cx = ffn_half(cx, norm_g[l, 2], mc[6], mc[7], mc[8], wg[l, 1], wu[l, 1], wd[l, 1])
    return lat
```

```python
import functools
import math

import numpy as np
import jax
import jax.numpy as jnp
from jax import lax
from jax.experimental import pallas as pl
from jax.experimental.pallas import tpu as pltpu

F32 = jnp.float32
BF16 = jnp.bfloat16
HIGHEST = lax.Precision.HIGHEST

N_MOD = 9
NORM_EPS = 1e-6
GN_EPS = 64e-5
GRID_W = 64
HEAD_DIM = 64
HALF_HEAD = HEAD_DIM // 2
S5_GROUP = 16
S5_STATE = 64
S5_SUPER = 4
HY_ORDER = 2
HY_EMB = 33
HY_MIN_DECAY = math.log(1e-2) / 1.5
HY_MAX_DECAY = math.log(1e-2) / 0.3
LANES = 128
SUBLANES = 8
MXU_DIM = 256
FFT_N2 = 128
VMEM_LIMIT = 56 * 1024 * 1024


def _cparams(sem, vmem=VMEM_LIMIT):
    return pltpu.CompilerParams(dimension_semantics=sem, vmem_limit_bytes=vmem)


def _dot(a, b):
    return jnp.dot(a, b, preferred_element_type=F32)


def _dot_exact(a, b):
    return jnp.dot(a, b, preferred_element_type=F32, precision=HIGHEST)


def _rms_mod(x, gain, shift, scale):
    ms = jnp.mean(x * x, axis=-1, keepdims=True)
    return x * lax.rsqrt(ms + NORM_EPS) * gain * (1.0 + scale) + shift


def _sigmoid(x):
    return 1.0 / (1.0 + jnp.exp(-x))


def _silu(x):
    return x * _sigmoid(x)


def _ada_kernel(c_ref, w_ref, b_ref, o_ref):
    s = _silu(c_ref[...])
    o_ref[0] = _dot(s.astype(BF16), w_ref[0].astype(BF16)) + b_ref[0]


def ada_mods_all(cond8, ada_w, ada_b):
    depth, d, n = ada_w.shape
    tn = n // 8
    return pl.pallas_call(
        _ada_kernel,
        grid=(depth, n // tn),
        in_specs=[pl.BlockSpec((8, d), lambda l, j: (0, 0)),
                  pl.BlockSpec((1, d, tn), lambda l, j: (l, 0, j)),
                  pl.BlockSpec((1, 1, tn), lambda l, j: (l, 0, j))],
        out_specs=pl.BlockSpec((1, 8, tn), lambda l, j: (l, 0, j)),
        out_shape=jax.ShapeDtypeStruct((depth, 8, n), F32),
        compiler_params=_cparams(("parallel", "parallel")),
        name="ada_mods",
    )(cond8, ada_w, ada_b.reshape(depth, 1, n))


def _ffn_kernel(x_ref, gain_ref, shift_ref, scale_ref, gate_ref, wg_ref, wu_ref, wd_ref, fg_ref,
                o_ref, *, final_norm, chunks):
    x = x_ref[0]
    h = _rms_mod(x, gain_ref[...], shift_ref[0], scale_ref[0]).astype(BF16)
    acc = None
    for lo, hi in chunks:
        g = _dot(h, wg_ref[:, lo:hi])
        u = _dot(h, wu_ref[:, lo:hi])
        part = _dot((_silu(g) * u).astype(BF16), wd_ref[lo:hi, :])
        acc = part if acc is None else acc + part
    y = x + 0.5 * gate_ref[0] * acc
    if final_norm:
        ms = jnp.mean(y * y, axis=-1, keepdims=True)
        y = y * lax.rsqrt(ms + NORM_EPS) * fg_ref[...]
    o_ref[0] = y


def ffn_half(x, gain, shift, scale, gate, wg, wu, wd, final_g=None):
    bsz, t, d = x.shape
    ff = wg.shape[1]
    tm = min(1024, t)
    step = 2 * MXU_DIM
    chunks = tuple((lo, min(lo + step, ff)) for lo in range(0, ff, step))
    fg = jnp.ones((1, d), F32) if final_g is None else final_g.reshape(1, d)
    vec = pl.BlockSpec((1, 1, d), lambda b, i: (b, 0, 0))
    resident = lambda a: pl.BlockSpec(a.shape, lambda b, i: (0, 0), pipeline_mode=pl.Buffered(1))
    return pl.pallas_call(
        functools.partial(_ffn_kernel, final_norm=final_g is not None, chunks=chunks),
        grid=(bsz, t // tm),
        in_specs=[pl.BlockSpec((1, tm, d), lambda b, i: (b, i, 0)),
                  pl.BlockSpec((1, d), lambda b, i: (0, 0)),
                  vec, vec, vec, resident(wg), resident(wu), resident(wd),
                  pl.BlockSpec((1, d), lambda b, i: (0, 0))],
        out_specs=pl.BlockSpec((1, tm, d), lambda b, i: (b, i, 0)),
        out_shape=jax.ShapeDtypeStruct((bsz, t, d), F32),
        compiler_params=_cparams(("parallel", "parallel")),
        name="ffn_half",
    )(x, gain.reshape(1, d), shift, scale, gate, wg, wu, wd, fg)


def _modmm_kernel(cx_ref, lat_ref, gain_ref, shc_ref, scc_ref, shl_ref, scl_ref, w_ref, o_ref, u_ref, *, nc):
    is_ctx = pl.program_id(1) < nc
    x = jnp.where(is_ctx, cx_ref[0], lat_ref[0])
    shift = jnp.where(is_ctx, shc_ref[0], shl_ref[0])
    scale = jnp.where(is_ctx, scc_ref[0], scl_ref[0])
    h = _rms_mod(x, gain_ref[...], shift, scale).astype(BF16)
    n1 = o_ref.shape[2]
    o_ref[0] = _dot(h, w_ref[:, :n1])
    u_ref[0] = _dot(h, w_ref[:, n1:])


def mod_matmul_stream(cx, lat, gain, mc, ml, w, tm, n_first):
    bsz, t_ctx, d = cx.shape
    t_lat = lat.shape[1]
    n = w.shape[1]
    nc = t_ctx // tm
    nt = nc + t_lat // tm
    vec = pl.BlockSpec((1, 1, d), lambda b, i: (b, 0, 0))
    return pl.pallas_call(
        functools.partial(_modmm_kernel, nc=nc),
        grid=(bsz, nt),
        in_specs=[pl.BlockSpec((1, tm, d), lambda b, i: (b, jnp.minimum(i, nc - 1), 0)),
                  pl.BlockSpec((1, tm, d), lambda b, i: (b, jnp.maximum(i - nc, 0), 0)),
                  pl.BlockSpec((1, d), lambda b, i: (0, 0)),
                  vec, vec, vec, vec,
                  pl.BlockSpec((d, n), lambda b, i: (0, 0))],
        out_specs=[pl.BlockSpec((1, tm, n_first), lambda b, i: (b, i, 0)),
                   pl.BlockSpec((1, tm, n - n_first), lambda b, i: (b, i, 0))],
        out_shape=[jax.ShapeDtypeStruct((bsz, t_ctx + t_lat, n_first), F32),
                   jax.ShapeDtypeStruct((bsz, t_ctx + t_lat, n - n_first), F32)],
        compiler_params=_cparams(("parallel", "parallel")),
        name="mod_matmul",
    )(cx, lat, gain.reshape(1, d), mc[3], mc[4], ml[3], ml[4], w)


def _rwkv_feat_kernel(p_ref, pu_ref, pd_ref, mu_ref, wup_ref, w0_ref, aup_ref, a0_ref, gup_ref,
                      ka_ref, rk_ref, ones_ref,
                      v_o, g_o, bonus_o, r_o, k_o, w_o, a_o, *, nt, width):
    i = pl.program_id(1)
    is_ctx = i == 0
    p = p_ref[0]
    tm = p.shape[0]
    row = lax.broadcasted_iota(jnp.int32, p.shape, 0)
    lane = lax.broadcasted_iota(jnp.int32, p.shape, 1) % 4
    prev = pltpu.roll(p, 1, 0)
    nxt = pltpu.roll(p, tm - 1, 0)
    col = jnp.where(is_ctx, row, row % GRID_W)
    last = jnp.where(is_ctx, tm - 1, GRID_W - 1)
    left = jnp.where(col == 0, 0.0, prev)
    right = jnp.where(col == last, 0.0, nxt)
    up_halo = jnp.where(i > 1, pu_ref[0], 0.0)
    dn_halo = jnp.where(i < nt - 1, pd_ref[0], 0.0)
    up = jnp.where(is_ctx, left, jnp.concatenate([up_halo, p[:tm - GRID_W]], axis=0))
    down = jnp.where(is_ctx, right, jnp.concatenate([p[GRID_W:], dn_halo], axis=0))
    shifted = jnp.where(lane == 0, left, jnp.where(lane == 1, right, jnp.where(lane == 2, up, down)))
    q = p + mu_ref[...] * (shifted - p)

    w = width
    r, k, v = q[:, :w], q[:, w:2 * w], q[:, 2 * w:3 * w]
    wd = q[:, 3 * w:3 * w + LANES]
    ad = q[:, 3 * w + LANES:3 * w + 2 * LANES]
    gd = q[:, 3 * w + 2 * LANES:3 * w + 3 * LANES]

    zlin = w0_ref[...] + _dot(jnp.tanh(wd).astype(BF16), wup_ref[...])
    neg = -zlin
    softplus = jnp.maximum(neg, 0.0) + jnp.log(1.0 + jnp.exp(-jnp.abs(neg)))
    decay = jnp.exp(-jnp.exp(-softplus - 0.5))
    a = _sigmoid(a0_ref[...] + _dot(ad.astype(BF16), aup_ref[...]))
    g_o[0] = _dot(_sigmoid(gd).astype(BF16), gup_ref[...])

    ones = ones_ref[...]
    for d in range(2):
        v_o[d, 0] = v
        r_o[d, 0] = r
        k_o[d, 0] = k
    ksum = jnp.zeros_like(k)
    for d in range(2):
        a_d = a[:, d * w:(d + 1) * w]
        ksum = ksum + k * (1.0 + (a_d - 1.0) * ka_ref[...])
        w_o[d, 0] = decay[:, d * w:(d + 1) * w]
        a_o[d, 0] = a_d
    bonus_o[0] = _dot_exact(r * ksum * rk_ref[...], ones) * v


def rwkv_features(p, n_cols, width, consts, tm):
    bsz, t, _ = p.shape
    nt = t // tm
    hb = tm // GRID_W
    nhb = t // GRID_W
    w = width
    full = lambda a: pl.BlockSpec(a.shape, lambda b, i: (0,) * a.ndim)
    tok = pl.BlockSpec((1, tm, w), lambda b, i: (b, i, 0))
    tok2 = pl.BlockSpec((2, 1, tm, w), lambda b, i: (0, b, i, 0))
    sds = jax.ShapeDtypeStruct((bsz, t, w), F32)
    sds2 = jax.ShapeDtypeStruct((2, bsz, t, w), F32)
    return pl.pallas_call(
        functools.partial(_rwkv_feat_kernel, nt=nt, width=w),
        grid=(bsz, nt),
        in_specs=[pl.BlockSpec((1, tm, n_cols), lambda b, i: (b, i, 0)),
                  pl.BlockSpec((1, GRID_W, n_cols), lambda b, i: (b, jnp.maximum(i * hb - 1, 0), 0)),
                  pl.BlockSpec((1, GRID_W, n_cols), lambda b, i: (b, jnp.minimum((i + 1) * hb, nhb - 1), 0))]
                 + [full(a) for a in consts],
        out_specs=[tok2, tok, tok, tok2, tok2, tok2, tok2],
        out_shape=[sds2, sds, sds, sds2, sds2, sds2, sds2],
        compiler_params=_cparams(("parallel", "parallel")),
        name="rwkv_features",
    )(p, p, p, *consts)


def _rwkv_scan_kernel(rf, rb, kf, kb, wf, wb, af, ab, vf, vb, ka_ref, kks_ref, of_ref, ob_ref,
                      s_ref, vec_ref, v_ref, c_ref, look_ref, *, tc):
    @pl.when(pl.program_id(0) == 0)
    def _():
        s_ref[...] = jnp.zeros_like(s_ref)
        look_ref[:, tc] = jnp.zeros((look_ref.shape[0],) + look_ref.shape[2:], F32)

    nj, ni = s_ref.shape[0], s_ref.shape[1]
    quarter = LANES // 4

    def fold(x):
        hi = (lax.broadcasted_iota(jnp.int32, x.shape, 1) % (2 * quarter)) >= quarter
        return x + jnp.where(hi, pltpu.roll(x, quarter, 1), pltpu.roll(x, 3 * quarter, 1))

    def hsum(x):
        return jnp.sum(x, axis=0, keepdims=True)

    def jsum(x):
        return fold(jnp.broadcast_to(hsum(x), (SUBLANES, LANES)))[0:1]

    ka = ka_ref[...]
    kks = kks_ref[...]
    fwd_lane = lax.broadcasted_iota(jnp.int32, (nj, LANES), 1) < LANES // 2
    def merged(f, b, t):
        return jnp.where(fwd_lane, f[t], b[tc - 1 - t])

    cum = jnp.ones((nj, LANES), F32)
    for t in range(tc):
        w = merged(wf, wb, t)
        cum = cum * w
        kk = merged(kf, kb, t) * kks
        look_ref[0, t] = kk / jnp.maximum(jnp.sqrt(jsum(kk * kk)), 1e-12)
        look_ref[1, t] = w * merged(rf, rb, t)
        look_ref[2, t] = cum
    for t in range(tc):
        r, k, a = merged(rf, rb, t), merged(kf, kb, t), merged(af, ab, t)
        kk, kk_next, wr_next = look_ref[0, t], look_ref[0, t + 1], look_ref[1, t + 1]
        cum = look_ref[2, t]
        inv = 1.0 / cum
        kka = kk * a
        kd = k * (1.0 + (a - 1.0) * ka)
        vec_ref[0, t] = kka * inv
        vec_ref[1, t] = kd * inv
        vec_ref[2, t] = cum * kk_next
        vec_ref[3, t] = cum * wr_next
        c_ref[t, 0:1, :] = jsum(kka * r)
        c_ref[t, 1:2, :] = jsum(kd * r)
        c_ref[t, 2:3, :] = hsum(kka * kk_next)
        c_ref[t, 3:4, :] = hsum(kd * kk_next)
        c_ref[t, 4:5, :] = hsum(kka * wr_next)
        c_ref[t, 5:6, :] = hsum(kd * wr_next)
        v_ref[t] = jnp.concatenate([vf[t], vb[tc - 1 - t]], axis=-1)

    def state_sums(y1, y2):
        s1 = jnp.zeros((ni, LANES), F32)
        s2 = jnp.zeros((ni, LANES), F32)
        for j in range(nj):
            sj = s_ref[j]
            s1 = s1 + sj * y1(j)
            s2 = s2 + sj * y2(j)
        return s1, s2

    def step(t, carry):
        row = lambda n: (lambda j: vec_ref[n, t, j:j + 1, :])
        a1, a2 = state_sums(row(2), row(3))
        sa, o1 = fold(carry[0]), fold(carry[1])
        v = v_ref[t]
        out = o1 - sa * c_ref[t, 0:1, :] + v * c_ref[t, 1:2, :]
        of_ref[t] = out
        ob_ref[tc - 1 - t] = out
        nxt = (a1 - sa * c_ref[t, 2:3, :] + v * c_ref[t, 3:4, :],
               a2 - sa * c_ref[t, 4:5, :] + v * c_ref[t, 5:6, :])
        for j in range(nj):
            s_ref[j] = s_ref[j] - sa * row(0)(j) + v * row(1)(j)
        return nxt

    first = state_sums(lambda j: look_ref[0, 0, j:j + 1, :], lambda j: look_ref[1, 0, j:j + 1, :])
    lax.fori_loop(0, tc, step, first)
    for j in range(nj):
        s_ref[j] = s_ref[j] * look_ref[2, tc - 1, j:j + 1, :]


def rwkv_scan(r, k, w, a, v, ka, kks, t_ctx):
    ttot, nj, _ = r.shape
    ni, nc = v.shape[1], v.shape[2]
    tc = 32
    ncb, ntb = t_ctx // tc, ttot // tc
    fwd = lambda g: (g, 0, 0)
    bwd = lambda g: (jnp.where(g < ncb, ncb - 1 - g, ntb + ncb - 1 - g), 0, 0)
    jf, jb = pl.BlockSpec((tc, nj, LANES), fwd), pl.BlockSpec((tc, nj, LANES), bwd)
    vf, vb = pl.BlockSpec((tc, ni, nc), fwd), pl.BlockSpec((tc, ni, nc), bwd)
    of, ob = pl.BlockSpec((tc, ni, LANES), fwd), pl.BlockSpec((tc, ni, LANES), bwd)
    osd = jax.ShapeDtypeStruct((ttot, ni, LANES), F32)
    return pl.pallas_call(
        functools.partial(_rwkv_scan_kernel, tc=tc),
        grid=(ntb,),
        in_specs=[jf, jb] * 4 + [vf, vb, pl.BlockSpec(ka.shape, lambda g: (0, 0)),
                                 pl.BlockSpec(kks.shape, lambda g: (0, 0))],
        out_specs=[of, ob],
        out_shape=[osd, osd],
        scratch_shapes=[pltpu.VMEM((nj, ni, LANES), F32), pltpu.VMEM((4, tc, nj, LANES), F32),
                        pltpu.VMEM((tc, ni, LANES), F32), pltpu.VMEM((tc, SUBLANES, LANES), F32),
                        pltpu.VMEM((3, tc + 1, nj, LANES), F32)],
        compiler_params=_cparams(("arbitrary",)),
        name="rwkv_scan",
    )(r, r, k, k, w, w, a, a, v, v, ka, kks)


def _s5_kernel(uf_ref, ub_ref, bm_ref, cm_ref, a_ref, yf_ref, yb_ref, h_ref, bu_ref, ubr_ref, *, tt):
    @pl.when(pl.program_id(0) == 0)
    def _():
        h_ref[...] = jnp.zeros_like(h_ref)

    nsb = bm_ref.shape[0]
    kin = bm_ref.shape[1] // 2
    sw = bm_ref.shape[2]
    hw = sw // 2
    for t in range(tt):
        ubr_ref[t] = ub_ref[tt - 1 - t]
    nothing = jnp.zeros(uf_ref.shape, F32)
    uf = jnp.concatenate([uf_ref[...], nothing], axis=1)
    ub = jnp.concatenate([nothing, ubr_ref[...]], axis=1)
    for sb in range(nsb):
        lhs = jnp.concatenate([uf[:, :, sb * kin:(sb + 1) * kin], ub[:, :, sb * kin:(sb + 1) * kin]], axis=-1)
        lhs = lhs.reshape(tt * SUBLANES, 2 * kin).astype(BF16)
        bu_ref[:, sb * sw:(sb + 1) * sw] = _dot(lhs, bm_ref[sb])

    def step(t, hs):
        rows = pl.ds(pl.multiple_of(t * SUBLANES, SUBLANES), SUBLANES)
        out = []
        for sb in range(nsb):
            hr, hi = hs[2 * sb], hs[2 * sb + 1]
            lo = sb * sw
            ar = a_ref[0, :, lo:lo + hw]
            ai = a_ref[0, :, lo + hw:lo + sw]
            nr = ar * hr - ai * hi + bu_ref[rows, lo:lo + hw]
            ni = ar * hi + ai * hr + bu_ref[rows, lo + hw:lo + sw]
            bu_ref[rows, lo:lo + hw] = nr
            bu_ref[rows, lo + hw:lo + sw] = ni
            out += [nr, ni]
        return tuple(out)

    h0 = []
    for sb in range(nsb):
        h0 += [h_ref[:, sb * sw:sb * sw + hw], h_ref[:, sb * sw + hw:(sb + 1) * sw]]
    hs = lax.fori_loop(0, tt, step, tuple(h0))
    for sb in range(nsb):
        h_ref[:, sb * sw:sb * sw + hw] = hs[2 * sb]
        h_ref[:, sb * sw + hw:(sb + 1) * sw] = hs[2 * sb + 1]

    nout = cm_ref.shape[2] // 2
    fwd_row = (lax.broadcasted_iota(jnp.int32, (tt * SUBLANES, nout), 0) % SUBLANES) < SUBLANES // 2
    ys = []
    for sb in range(nsb):
        yy = _dot(bu_ref[:, sb * sw:(sb + 1) * sw].astype(BF16), cm_ref[sb])
        ys.append(jnp.where(fwd_row, yy[:, :nout], yy[:, nout:]))
    y = jnp.concatenate(ys, axis=1).reshape(tt, SUBLANES, nsb * nout)
    yf_ref[...] = y
    for t in range(tt):
        yb_ref[tt - 1 - t] = y[t]


def s5_scan(u, bm, cm, a, t_ctx):
    ttot, bsz, width = u.shape
    rows = 2 * bsz
    nsb, _, sw = bm.shape
    tt = 64
    ncb, ntb = t_ctx // tt, ttot // tt
    fmap = lambda g: (g, 0, 0)
    bmap = lambda g: (jnp.where(g < ncb, ncb - 1 - g, ntb + ncb - 1 - g), 0, 0)
    full = lambda x: pl.BlockSpec(x.shape, lambda g: (0,) * x.ndim)
    osd = jax.ShapeDtypeStruct((ttot, rows, width), F32)
    return pl.pallas_call(
        functools.partial(_s5_kernel, tt=tt),
        grid=(ntb,),
        in_specs=[pl.BlockSpec((tt, bsz, width), fmap), pl.BlockSpec((tt, bsz, width), bmap),
                  full(bm), full(cm), full(a)],
        out_specs=[pl.BlockSpec((tt, rows, width), fmap), pl.BlockSpec((tt, rows, width), bmap)],
        out_shape=[osd, osd],
        scratch_shapes=[pltpu.VMEM((rows, nsb * sw), F32), pltpu.VMEM((tt * rows, nsb * sw), F32),
                        pltpu.VMEM((tt, bsz, width), F32)],
        compiler_params=_cparams(("arbitrary",)),
        name="s5_scan",
    )(u, u, bm, cm, a)


def _even_out_kernel(x_ref, of_ref, ob_ref, bonus_ref, g_ref, ys_ref, u_ref, m5_ref,
                     ones_ref, gng_ref, gnb_ref, dskip_ref, wglu_ref, bglu_ref, wo1_ref, wo2_ref, o_ref):
    ones = ones_ref[...]
    inv = 1.0 / HEAD_DIM
    o = of_ref[0] + ob_ref[0]
    mean = _dot_exact(o, ones) * inv
    oc = o - mean
    var = _dot_exact(oc * oc, ones) * inv
    y1 = (oc * lax.rsqrt(var + GN_EPS) * gng_ref[...] + gnb_ref[...] + bonus_ref[0]) * g_ref[0]
    y = ys_ref[0] + dskip_ref[...] * u_ref[0]
    y = jax.nn.gelu(y)
    y2 = y * _sigmoid(_dot(y.astype(BF16), wglu_ref[...]) + bglu_ref[...])
    out = _dot(y1.astype(BF16), wo1_ref[...]) + _dot(y2.astype(BF16), wo2_ref[...])
    o_ref[0] = x_ref[0] + m5_ref[0] * out


def even_out(x, off, o_f, o_b, bonus, g, ys, u, m5, consts, tm):
    bsz, t, d = x.shape
    w = o_f.shape[2]
    tokd = pl.BlockSpec((1, tm, d), lambda b, i: (b, i, 0))
    tokw = pl.BlockSpec((1, tm, w), lambda b, i: (b, i + off, 0))
    toks = pl.BlockSpec((1, tm, u.shape[2]), lambda b, i: (b, i + off, 0))
    full = lambda a: pl.BlockSpec(a.shape, lambda b, i: (0,) * a.ndim)
    return pl.pallas_call(
        _even_out_kernel,
        grid=(bsz, t // tm),
        in_specs=[tokd, tokw, tokw, tokw, tokw, toks, toks,
                  pl.BlockSpec((1, 1, d), lambda b, i: (b, 0, 0))] + [full(a) for a in consts],
        out_specs=tokd,
        out_shape=jax.ShapeDtypeStruct((bsz, t, d), F32),
        compiler_params=_cparams(("parallel", "parallel")),
        name="even_out",
    )(x, o_f, o_b, bonus, g, ys, u, m5, *consts)


def _hy_in_kernel(x_ref, xp_ref, xn_ref, gain_ref, shift_ref, scale_ref, w_ref, cw_ref, cb_ref,
                  z_ref, zb_ref, g1_ref, g2_ref, *, nt, c):
    i = pl.program_id(1)
    gain, shift, scale = gain_ref[...], shift_ref[0], scale_ref[0]
    tm = x_ref.shape[1]
    halo = xp_ref.shape[1]
    h = jnp.concatenate([_rms_mod(xr[0], gain, shift, scale) for xr in (xp_ref, x_ref, xn_ref)], axis=0)
    h = h.astype(BF16)
    row = lax.broadcasted_iota(jnp.int32, (tm, c), 0)
    outs = (z_ref, g1_ref, g2_ref)
    for part in range(3):
        wp = w_ref[:, part * c:(part + 1) * c]
        p_ext = _dot(h, wp)
        p = p_ext[halo:halo + tm]
        pp = jnp.where(i > 0, p_ext[halo - 1:halo], 0.0)
        pn = jnp.where(i < nt - 1, p_ext[halo + tm:halo + tm + 1], 0.0)
        pm1 = jnp.where(row == 0, pp, pltpu.roll(p, 1, 0))
        pp1 = jnp.where(row == tm - 1, pn, pltpu.roll(p, tm - 1, 0))
        cw = cw_ref[:, part * c:(part + 1) * c]
        q = cw[0:1] * pm1 + cw[1:2] * p + cw[2:3] * pp1 + cb_ref[:, part * c:(part + 1) * c]
        outs[part][0] = q
        if part == 0:
            zb_ref[0] = q.astype(BF16)


def hyena_in(x, gain, shift, scale, w, conv_w, conv_b):
    bsz, t, d = x.shape
    c = w.shape[1] // 3
    tm = min(512, t)
    nt = t // tm
    hb = tm // 8
    nhb = t // 8
    vec = pl.BlockSpec((1, 1, d), lambda b, i: (b, 0, 0))
    tok = pl.BlockSpec((1, tm, c), lambda b, i: (b, i, 0))
    sds = jax.ShapeDtypeStruct((bsz, t, c), F32)
    return pl.pallas_call(
        functools.partial(_hy_in_kernel, nt=nt, c=c),
        grid=(bsz, nt),
        in_specs=[pl.BlockSpec((1, tm, d), lambda b, i: (b, i, 0)),
                  pl.BlockSpec((1, 8, d), lambda b, i: (b, jnp.maximum(i * hb - 1, 0), 0)),
                  pl.BlockSpec((1, 8, d), lambda b, i: (b, jnp.minimum((i + 1) * hb, nhb - 1), 0)),
                  pl.BlockSpec((1, d), lambda b, i: (0, 0)),
                  vec, vec,
                  pl.BlockSpec(w.shape, lambda b, i: (0, 0)),
                  pl.BlockSpec(conv_w.shape, lambda b, i: (0, 0)),
                  pl.BlockSpec((1, 3 * c), lambda b, i: (0, 0))],
        out_specs=[tok, tok, tok, tok],
        out_shape=[sds, jax.ShapeDtypeStruct((bsz, t, c), BF16), sds, sds],
        compiler_params=_cparams(("parallel", "parallel")),
        name="hyena_in",
    )(x, x, x, gain.reshape(1, d), shift, scale, w, conv_w, conv_b.reshape(1, 3 * c))


def _pack_complex(re, im):
    bits = lambda x: lax.bitcast_convert_type(x.astype(BF16).astype(F32), jnp.uint32)
    return lax.shift_right_logical(bits(re), jnp.uint32(16)) | bits(im)


def _unpack_complex(w):
    re = lax.bitcast_convert_type(lax.shift_left(w, jnp.uint32(16)), F32)
    im = lax.bitcast_convert_type(w & jnp.uint32(0xFFFF0000), F32)
    return jnp.concatenate([re, im], axis=0).astype(BF16)


def _fft_conv_kernel(z_ref, kf_ref, g_ref, w1_ref, g2i_ref, f1c_ref, twt_ref, o_ref, a_ref, t_ref,
                     *, n1, k1c):
    s = pl.program_id(2)
    n2 = FFT_N2
    nd = n1 // 2
    slab = n1 // k1c

    @pl.when(s == 0)
    def _():
        w1 = w1_ref[...]

        def body(j, carry):
            x = jnp.concatenate([z_ref[0, j], z_ref[1, j]], axis=0)
            a = _dot(w1, x)
            a_ref[pl.ds(pl.multiple_of(j * n1, n1), n1), :] = _pack_complex(a[:n1], a[n1:])
            return carry

        lax.fori_loop(0, n2, body, 0, unroll=8)

    @pl.when((s > 0) & (s <= k1c))
    def _():
        g2i = g2i_ref[...]
        base = (s - 1) * slab

        def body(kk, carry):
            k1 = base + kk
            ak = _unpack_complex(a_ref[pl.ds(k1, n2, stride=n1), :])
            x = _dot(g_ref[kk], ak)
            xr, xi = x[:n2], x[n2:]
            kr, ki = kf_ref[0, kk], kf_ref[1, kk]
            pr = xr * kr - xi * ki
            pi = xr * ki + xi * kr
            tt = _dot(g2i, jnp.concatenate([pr, pi], axis=0).astype(BF16))
            t_ref[pl.ds(pl.multiple_of(k1 * n2, n2), n2), :] = _pack_complex(tt[:n2], tt[n2:])
            return carry

        lax.fori_loop(0, slab, body, 0, unroll=4)

    @pl.when(s == k1c + 1)
    def _():
        f1r, f1i = f1c_ref[0], f1c_ref[1]

        def body(j, carry):
            twr = twt_ref[0, pl.ds(j, 1), :]
            twi = twt_ref[1, pl.ds(j, 1), :]
            wr = f1r * twr + f1i * twi
            wi = f1i * twr - f1r * twi
            w3 = jnp.concatenate([jnp.concatenate([wr, -wi], axis=1),
                                  jnp.concatenate([wi, wr], axis=1)], axis=0).astype(BF16)
            tn = _unpack_complex(t_ref[pl.ds(j, n1, stride=n2), :])
            y = _dot(w3, tn)
            o_ref[0, j] = y[:nd]
            o_ref[1, j] = y[nd:]
            return carry

        lax.fori_loop(0, n2, body, 0, unroll=8)


def _fft_consts(n1):
    n2 = FFT_N2
    n = n1 * n2
    nd = n1 // 2
    k1 = np.arange(n1)
    f1 = np.exp(-2j * np.pi * np.outer(k1, np.arange(nd)) / n1)
    w1 = np.block([[f1.real, -f1.imag], [f1.imag, f1.real]])
    f2 = np.exp(-2j * np.pi * np.outer(np.arange(n2), np.arange(n2)) / n2)
    f2c = np.conj(f2)
    g2i = np.block([[f2c.real, -f2c.imag], [f2c.imag, f2c.real]])
    f1c = np.exp(2j * np.pi * np.outer(np.arange(nd), k1) / n1) / n
    tw = np.exp(-2j * np.pi * np.outer(k1, np.arange(n2)) / n)
    cplx = lambda m: jnp.asarray(np.stack([m.real, m.imag]), F32)
    (f2r, f2i), (twr, twi) = cplx(f2), cplx(tw)
    gr = f2r[None] * twr[:, None, :] - f2i[None] * twi[:, None, :]
    gi = f2r[None] * twi[:, None, :] + f2i[None] * twr[:, None, :]
    gtw = jnp.concatenate([jnp.concatenate([gr, -gi], axis=2), jnp.concatenate([gi, gr], axis=2)], axis=1)
    return (gtw.astype(BF16), jnp.asarray(w1, BF16), jnp.asarray(g2i, BF16), cplx(f1c), cplx(tw.T))


def fft_conv(zt, kf, order):
    bsz, n2, nd, c = zt.shape
    n1 = 2 * nd
    cb = LANES
    k1c = 8 if n1 % 8 == 0 and n1 >= 64 else 2
    slab = n1 // k1c
    gtw, *consts = _fft_consts(n1)
    full = lambda a: pl.BlockSpec(a.shape, lambda j, p, s: (0,) * a.ndim)
    blk = pl.BlockSpec((2, n2, nd, cb), lambda j, p, s: (p, 0, 0, j))
    chunk = lambda j, p, s: jnp.clip(s - 1, 0, k1c - 1)
    return pl.pallas_call(
        functools.partial(_fft_conv_kernel, n1=n1, k1c=k1c),
        grid=(c // cb, bsz // 2, k1c + 2),
        in_specs=[blk,
                  pl.BlockSpec((None, 2, slab, n2, cb), lambda j, p, s: (order, 0, chunk(j, p, s), 0, j)),
                  pl.BlockSpec((slab, 2 * n2, 2 * n2), lambda j, p, s: (chunk(j, p, s), 0, 0))]
                 + [full(a) for a in consts],
        out_specs=blk,
        out_shape=jax.ShapeDtypeStruct(zt.shape, F32),
        scratch_shapes=[pltpu.VMEM((n2 * n1, cb), jnp.uint32), pltpu.VMEM((n1 * n2, cb), jnp.uint32)],
        compiler_params=_cparams(("parallel", "parallel", "arbitrary")),
        name="fft_conv",
    )(zt, kf, gtw, *consts)


def _fft_fwd_kernel(z_ref, scale_ref, g_ref, w1_ref, o_ref, a_ref, *, n1, k1c):
    s = pl.program_id(2)
    n2 = FFT_N2
    slab = n1 // k1c
    cb = o_ref.shape[4]

    @pl.when(s == 0)
    def _():
        w1 = w1_ref[...]

        def body(j, carry):
            a = _dot(w1, jnp.concatenate([z_ref[0, j], z_ref[1, j]], axis=1))
            packed = _pack_complex(a[:n1], a[n1:])
            rows = pl.ds(pl.multiple_of(j * n1, n1), n1)
            a_ref[0, rows, :] = packed[:, :cb]
            a_ref[1, rows, :] = packed[:, cb:]
            return carry

        lax.fori_loop(0, n2, body, 0, unroll=8)

    @pl.when(s > 0)
    def _():
        base = (s - 1) * slab

        def body(kk, carry):
            k1 = base + kk
            rows = pl.ds(k1, n2, stride=n1)
            ak = jnp.concatenate([_unpack_complex(a_ref[0, rows, :]), _unpack_complex(a_ref[1, rows, :])], axis=1)
            x = _dot(g_ref[kk], ak)
            pos, neg = x[:, :cb], x[:, cb:]
            scale = scale_ref[0]
            o_ref[0, 0, kk] = (pos[:n2] + neg[:n2]) * scale
            o_ref[0, 1, kk] = (pos[n2:] - neg[n2:]) * scale
            return carry

        lax.fori_loop(0, slab, body, 0, unroll=4)


def filter_spectrum(taps, scale):
    orders, _, n2, nd, c = taps.shape
    n1 = 2 * nd
    cb = LANES
    k1c = 8 if n1 % 8 == 0 and n1 >= 64 else 2
    slab = n1 // k1c
    gtw, w1 = _fft_consts(n1)[:2]
    w1 = w1[:, :nd]
    full = lambda a: pl.BlockSpec(a.shape, lambda j, r, s: (0,) * a.ndim)
    return pl.pallas_call(
        functools.partial(_fft_fwd_kernel, n1=n1, k1c=k1c),
        grid=(c // cb, orders, k1c + 1),
        in_specs=[pl.BlockSpec((None, 2, n2, nd, cb), lambda j, r, s: (r, 0, 0, 0, j)),
                  pl.BlockSpec((1, 1, cb), lambda j, r, s: (r, 0, j)),
                  pl.BlockSpec((slab, 2 * n2, 2 * n2), lambda j, r, s: (jnp.maximum(s - 1, 0), 0, 0)),
                  full(w1)],
        out_specs=pl.BlockSpec((1, 2, slab, n2, cb), lambda j, r, s: (r, 0, jnp.maximum(s - 1, 0), 0, j)),
        out_shape=jax.ShapeDtypeStruct((orders, 2, n1, n2, c), F32),
        scratch_shapes=[pltpu.VMEM((2, n2 * n1, cb), jnp.uint32)],
        compiler_params=_cparams(("parallel", "parallel", "arbitrary")),
        name="fft_forward",
    )(taps, scale, gtw, w1)


def _hy_filter_kernel(feat_ref, fw1_ref, fb1_ref, fw2_ref, fb2_ref, fw3_ref, fb3_ref, fw4_ref, freq_ref,
                      delta_ref, z_ref, ss_ref):
    @pl.when(pl.program_id(1) == 0)
    def _():
        ss_ref[...] = jnp.zeros_like(ss_ref)

    feats = feat_ref[...]
    t = feats[:, 0:1]
    fr = freq_ref[...]
    h = jnp.sin(fr * (_dot_exact(feats, fw1_ref[...]) + fb1_ref[...]))
    h = jnp.sin(fr * (_dot_exact(h, fw2_ref[...]) + fb2_ref[...]))
    h = jnp.sin(fr * (_dot_exact(h, fw3_ref[...]) + fb3_ref[...]))
    lag0 = jnp.where(pl.program_id(0) % 2 == 1, 0.0, 1.0)
    filt = _dot(h.astype(BF16), fw4_ref[...]) * (jnp.exp(-t * delta_ref[...]) * jnp.where(t == 0.0, lag0, 1.0))
    z_ref[0] = filt.astype(BF16)
    ss_ref[0] += jnp.sum(filt * filt, axis=0, keepdims=True)


def hyena_filter(t, fw1, fb1, fw2, fb2, fw3, fb3, fw4, freq, width):
    nd = t // FFT_N2
    pos = (jnp.arange(FFT_N2, dtype=F32)[:, None] + FFT_N2 * jnp.arange(nd, dtype=F32)[None, :]).reshape(-1, 1)
    tt = pos / max(t - 1, 1)
    ang = 2 * math.pi * pos / t
    nb = (HY_EMB - 1) // 2
    bands = jnp.linspace(1e-4, nb - 1, nb, dtype=F32)[None]
    feats = jnp.concatenate([tt, jnp.cos(bands * ang), -jnp.sin(bands * ang),
                             jnp.zeros((t, LANES - HY_EMB), F32)], axis=-1)
    deltas = jnp.abs(jnp.linspace(HY_MIN_DECAY, HY_MAX_DECAY, width, dtype=F32)).reshape(1, width)
    rows = 2 * HY_ORDER
    tm = min(512, t)
    hid = fw1.shape[1]
    full = lambda a: pl.BlockSpec(a.shape, lambda r, i: (0,) * a.ndim)
    fw1p = jnp.concatenate([fw1, jnp.zeros((LANES - HY_EMB, hid), F32)], axis=0)
    vecs = [fw1p, fb1.reshape(1, hid), fw2, fb2.reshape(1, hid), fw3, fb3.reshape(1, hid)]
    return pl.pallas_call(
        _hy_filter_kernel,
        grid=(rows, t // tm),
        in_specs=[pl.BlockSpec((tm, LANES), lambda r, i: (i, 0))] + [full(a) for a in vecs]
                 + [pl.BlockSpec((hid, width), lambda r, i: (0, r)), full(freq.reshape(1, hid)), full(deltas)],
        out_specs=[pl.BlockSpec((1, tm, width), lambda r, i: (r, i, 0)),
                   pl.BlockSpec((1, 1, width), lambda r, i: (r, 0, 0))],
        out_shape=[jax.ShapeDtypeStruct((rows, t, width), BF16), jax.ShapeDtypeStruct((rows, 1, width), F32)],
        compiler_params=_cparams(("parallel", "arbitrary")),
        name="hyena_filter",
    )(feats, *vecs, fw4.astype(BF16), freq.reshape(1, hid), deltas)


def hyena_filter_spectrum_fft(t, fw1, fb1, fw2, fb2, fw3, fb3, fw4, freq, width):
    taps, ss = hyena_filter(t, fw1, fb1, fw2, fb2, fw3, fb3, fw4, freq, width)
    nd = t // FFT_N2
    scale = lax.rsqrt(ss.reshape(HY_ORDER, 2, width).sum(axis=1, keepdims=True) + 1e-6)
    return filter_spectrum(taps.reshape(HY_ORDER, 2, FFT_N2, nd, width), scale)


def _dft_conv_kernel(z_ref, kf_ref, fw_ref, iv_ref, o_ref, *, t):
    x = jnp.concatenate([z_ref[0], z_ref[1]], axis=0)
    spec = _dot(fw_ref[...], x)
    xr, xi = spec[:2 * t], spec[2 * t:]
    kr, ki = kf_ref[0], kf_ref[1]
    pr = xr * kr - xi * ki
    pi = xr * ki + xi * kr
    y = _dot(iv_ref[...], jnp.concatenate([pr, pi], axis=0).astype(BF16))
    o_ref[0] = y[:t]
    o_ref[1] = y[t:]


def dft_conv(zb, kf):
    bsz, t, c = zb.shape
    n = 2 * t
    f = np.exp(-2j * np.pi * np.outer(np.arange(n), np.arange(t)) / n)
    fw = np.block([[f.real, -f.imag], [f.imag, f.real]])
    fi = np.exp(2j * np.pi * np.outer(np.arange(t), np.arange(n)) / n) / n
    iv = np.block([[fi.real, -fi.imag], [fi.imag, fi.real]])
    fw, iv = jnp.asarray(fw, BF16), jnp.asarray(iv, BF16)
    cb = LANES
    blk = pl.BlockSpec((2, t, cb), lambda j, p: (p, 0, j))
    return pl.pallas_call(
        functools.partial(_dft_conv_kernel, t=t),
        grid=(c // cb, bsz // 2),
        in_specs=[blk, pl.BlockSpec((2, n, cb), lambda j, p: (0, 0, j)),
                  pl.BlockSpec(fw.shape, lambda j, p: (0, 0)), pl.BlockSpec(iv.shape, lambda j, p: (0, 0))],
        out_specs=blk,
        out_shape=jax.ShapeDtypeStruct(zb.shape, F32),
        compiler_params=_cparams(("parallel", "parallel")),
        name="dft_conv",
    )(zb, kf, fw, iv)


def _conv_rows(conv_ref):
    if len(conv_ref.shape) == 3:
        return conv_ref[0]
    return jnp.concatenate([conv_ref[0, :, k, :] for k in range(conv_ref.shape[2])], axis=0)


def _conv_spec(conv, tm, c):
    if conv.ndim == 3:
        return pl.BlockSpec((1, tm, c), lambda b, i: (b, i, 0))
    return pl.BlockSpec((1, FFT_N2, tm // FFT_N2, c), lambda b, i: (b, 0, i, 0))


def _hy_gate_kernel(conv_ref, z_ref, gate_ref, bias_ref, o_ref, ob_ref):
    y = gate_ref[0] * (_conv_rows(conv_ref) + bias_ref[...] * z_ref[0])
    o_ref[0] = y
    ob_ref[0] = y.astype(BF16)


def hyena_gate(conv, z, gate, bias):
    bsz, t, c = z.shape
    tm = min(SUBLANES * FFT_N2, t)
    tok = pl.BlockSpec((1, tm, c), lambda b, i: (b, i, 0))
    return pl.pallas_call(
        _hy_gate_kernel,
        grid=(bsz, t // tm),
        in_specs=[_conv_spec(conv, tm, c), tok, tok, pl.BlockSpec((1, c), lambda b, i: (0, 0))],
        out_specs=[tok, tok],
        out_shape=[jax.ShapeDtypeStruct(z.shape, F32), jax.ShapeDtypeStruct(z.shape, BF16)],
        compiler_params=_cparams(("parallel", "parallel")),
        name="hyena_gate",
    )(conv, z, gate, bias.reshape(1, c))


def _hy_out_kernel(x_ref, conv_ref, z_ref, gate_ref, bias_ref, w_ref, m5_ref, o_ref):
    y = gate_ref[0] * (_conv_rows(conv_ref) + bias_ref[...] * z_ref[0])
    o_ref[0] = x_ref[0] + m5_ref[0] * _dot(y.astype(BF16), w_ref[...])


def hyena_out(x, conv, z, gate, bias, w, m5):
    bsz, t, d = x.shape
    c = z.shape[2]
    tm = min(SUBLANES * FFT_N2, t)
    tokd = pl.BlockSpec((1, tm, d), lambda b, i: (b, i, 0))
    tokc = pl.BlockSpec((1, tm, c), lambda b, i: (b, i, 0))
    return pl.pallas_call(
        _hy_out_kernel,
        grid=(bsz, t // tm),
        in_specs=[tokd, _conv_spec(conv, tm, c), tokc, tokc, pl.BlockSpec((1, c), lambda b, i: (0, 0)),
                  pl.BlockSpec(w.shape, lambda b, i: (0, 0)), pl.BlockSpec((1, 1, d), lambda b, i: (b, 0, 0))],
        out_specs=tokd,
        out_shape=jax.ShapeDtypeStruct(x.shape, F32),
        compiler_params=_cparams(("parallel", "parallel")),
        name="hyena_out",
    )(x, conv, z, gate, bias.reshape(1, c), w, m5)


def _hyena_filter_spectrum(t, fw1, fb1, fw2, fb2, fw3, fb3, fw4, freq, width):
    pos = jnp.arange(t, dtype=F32)[:, None]
    tt = pos / max(t - 1, 1)
    ang = 2 * math.pi * pos / t
    nb = (HY_EMB - 1) // 2
    bands = jnp.linspace(1e-4, nb - 1, nb, dtype=F32)[None]
    feats = jnp.concatenate([tt, jnp.cos(bands * ang), -jnp.sin(bands * ang)], axis=-1)
    hdn = jnp.sin(freq * (feats @ fw1 + fb1))
    hdn = jnp.sin(freq * (hdn @ fw2 + fb2))
    hdn = jnp.sin(freq * (hdn @ fw3 + fb3))
    filt = (hdn @ fw4).reshape(t, HY_ORDER, 2, width)
    deltas = jnp.abs(jnp.linspace(HY_MIN_DECAY, HY_MAX_DECAY, width, dtype=F32))
    filt = filt * jnp.exp(-tt[:, :, None, None] * deltas)
    fwd, bwd = filt[:, :, 0], filt[:, :, 1]
    kern = jnp.concatenate([fwd, jnp.zeros_like(fwd[:1]), bwd[:0:-1]], axis=0)
    kern = kern * lax.rsqrt(jnp.sum(kern * kern, axis=0, keepdims=True) + 1e-6)
    spec = jnp.fft.fft(kern, axis=0)
    return jnp.stack([jnp.real(spec), jnp.imag(spec)], axis=1).transpose(2, 1, 0, 3).astype(F32)


def _block_diag_pair(m):
    z = jnp.zeros_like(m[0])
    return jnp.concatenate([jnp.concatenate([m[0], z], axis=1), jnp.concatenate([z, m[1]], axis=1)], axis=0)


def _even_mixer(lat, cx, ml, mc, gain, prm, ctx_out):
    (w_in, mu, w0, w_up, a0, a_up, g_up, k_k, k_a, r_k, gn_g, gn_b,
     lam_re, lam_im, log_dt, b_re, b_im, c_re, c_im, d_skip, w_glu, b_glu, w_out) = prm
    bsz, t_lat, d = lat.shape
    t_ctx = cx.shape[1]
    width = k_k.shape[0]
    heads = width // HEAD_DIM
    n_cols = mu.shape[0]
    s5w = d_skip.shape[0]
    ttot = t_ctx + t_lat
    nch = bsz * heads
    tm = t_ctx
    assert tm % GRID_W == 0 and t_lat % tm == 0 and 4 * nch == LANES and 2 * bsz == SUBLANES

    p_all, u_all = mod_matmul_stream(cx, lat, gain, mc, ml, w_in.astype(BF16), tm, n_cols)

    head_of = np.arange(width) // HEAD_DIM
    ones = jnp.asarray(head_of[:, None] == head_of[None, :], F32)
    feat_consts = (mu.reshape(1, -1), _block_diag_pair(w_up).astype(BF16), w0.reshape(1, -1),
                   _block_diag_pair(a_up).astype(BF16), a0.reshape(1, -1), g_up.astype(BF16),
                   k_a.reshape(1, -1), r_k.reshape(1, -1), ones)
    v2, g, bonus, r2, k2, w2, a2 = rwkv_features(p_all, n_cols, width, feat_consts, tm)

    def key_major(x):
        x = x.reshape(2, bsz, ttot, heads, 2, HALF_HEAD)
        return x.transpose(2, 5, 0, 4, 1, 3).reshape(ttot, HALF_HEAD, LANES)

    def value_major(x):
        x = x.reshape(2, bsz, ttot, heads, HEAD_DIM)
        return x.transpose(2, 4, 0, 1, 3).reshape(ttot, HEAD_DIM, 2 * nch)

    def key_const(x):
        x = x.reshape(heads, 2, HALF_HEAD).transpose(2, 1, 0)[:, None, :, None, :]
        return jnp.broadcast_to(x, (HALF_HEAD, 2, 2, bsz, heads)).reshape(HALF_HEAD, LANES)

    o_f, o_b = rwkv_scan(key_major(r2), key_major(k2), key_major(w2), key_major(a2), value_major(v2),
                         key_const(k_a), key_const(k_k), t_ctx)

    def token_major(o, lane0):
        o = o[..., lane0:lane0 + nch].reshape(ttot, HEAD_DIM, bsz, heads)
        return o.transpose(2, 0, 3, 1).reshape(bsz, ttot, width)

    o_f, o_b = token_major(o_f, 0), token_major(o_b, LANES // 2)

    ng = lam_re.shape[1]
    gps = ng // S5_SUPER
    lam = lax.complex(lam_re, lam_im)
    dt = jnp.exp(log_dt)[..., None]
    a_bar = jnp.exp(lam * dt)
    b_bar = ((a_bar - 1) / lam)[..., None] * lax.complex(b_re, b_im)
    eye = jnp.eye(gps, dtype=F32)

    def b_mat(x):
        x = x.reshape(2, S5_SUPER, gps, S5_STATE, S5_GROUP)
        m = jnp.einsum('dsgph,gk->sdghkp', x, eye)
        return m.reshape(S5_SUPER, 2 * gps * S5_GROUP, gps * S5_STATE)

    def c_mat(x):
        x = x.reshape(2, S5_SUPER, gps, S5_GROUP, S5_STATE)
        m = jnp.einsum('dsghp,gk->skpdgh', x, eye)
        return m.reshape(S5_SUPER, gps * S5_STATE, 2 * gps * S5_GROUP)

    bm = jnp.concatenate([b_mat(jnp.real(b_bar)), b_mat(jnp.imag(b_bar))], axis=2).astype(BF16)
    cm = jnp.concatenate([c_mat(c_re), -c_mat(c_im)], axis=1).astype(BF16)

    def a_rows(x):
        x = x.reshape(2, 1, S5_SUPER, gps * S5_STATE)
        return jnp.broadcast_to(x, (2, bsz, S5_SUPER, gps * S5_STATE)).reshape(2 * bsz, S5_SUPER, -1)

    a_arr = jnp.concatenate([a_rows(jnp.real(a_bar)), a_rows(jnp.imag(a_bar))], axis=2)
    a_arr = a_arr.reshape(1, 2 * bsz, -1)

    y_f, y_b = s5_scan(u_all.transpose(1, 0, 2), bm, cm, a_arr, t_ctx)
    ys = (y_f[:, :bsz] + y_b[:, bsz:]).transpose(1, 0, 2)

    wo = w_out.astype(BF16)
    out_consts = (ones, gn_g.reshape(1, -1), gn_b.reshape(1, -1), d_skip.reshape(1, -1),
                  w_glu.astype(BF16), b_glu.reshape(1, -1), wo[:width], wo[width:])
    lat = even_out(lat, 1, o_f, o_b, bonus, g, ys, u_all, ml[5], out_consts, tm)
    if ctx_out:
        cx = even_out(cx, 0, o_f, o_b, bonus, g, ys, u_all, mc[5], out_consts, tm)
    return lat, cx


def _hyena_mixer(x, gain, m, prm):
    (w_in, conv_w, conv_b, fw1, fb1, fw2, fb2, fw3, fb3, fw4, freq, bias_d, w_out) = prm
    bsz, t, d = x.shape
    c = w_out.shape[0]
    use_fft = t % (FFT_N2 * 2) == 0 and t >= 4 * FFT_N2
    if use_fft:
        kf = hyena_filter_spectrum_fft(t, fw1, fb1, fw2, fb2, fw3, fb3, fw4, freq, c)
    else:
        kf = _hyena_filter_spectrum(t, fw1, fb1, fw2, fb2, fw3, fb3, fw4, freq, c)
    z, zb, g1, g2 = hyena_in(x, gain, m[3], m[4], w_in.astype(BF16), conv_w, conv_b)
    gates = (g1, g2)
    for n in range(HY_ORDER):
        if use_fft:
            nd = t // FFT_N2
            zt = zb.reshape(bsz, nd, FFT_N2, c).transpose(0, 2, 1, 3)
            conv = fft_conv(zt, kf, n)
        else:
            conv = dft_conv(zb, kf[n])
        if n < HY_ORDER - 1:
            z, zb = hyena_gate(conv, z, gates[n], bias_d[n])
        else:
            return hyena_out(x, conv, z, gates[n], bias_d[n], w_out.astype(BF16), m[5])


def kernel(x, c, ctx, c_ctx, norm_g, ada_w, ada_b, ffn_wg, ffn_wu, ffn_wd, final_g, ev_w_in, ev_mu, ev_w0, ev_w_up, ev_a0, ev_a_up, ev_g_up, ev_k_k, ev_k_a, ev_r_k, ev_gn_g, ev_gn_b, ev_lam_re, ev_lam_im, ev_log_dt, ev_b_re, ev_b_im, ev_c_re, ev_c_im, ev_d, ev_w_glu, ev_b_glu, ev_w_out, od_w_in, od_conv_w, od_conv_b, od_fw1, od_fb1, od_fw2, od_fb2, od_fw3, od_fb3, od_fw4, od_freq, od_bias, od_w_out):
    depth = norm_g.shape[0]
    bsz, _, d = x.shape
    n_even = (depth + 1) // 2
    last_ctx = 2 * (n_even - 1)

    cond8 = jnp.concatenate([c, c_ctx[None], jnp.zeros((8 - bsz - 1, d), F32)], axis=0)
    mods = ada_mods_all(cond8, ada_w, ada_b)

    wg, wu, wd = ffn_wg.astype(BF16), ffn_wu.astype(BF16), ffn_wd.astype(BF16)
    lat, cx = x, ctx
    for l in range(depth):
        run_ctx = l <= last_ctx
        ctx_out = l < last_ctx
        i = l // 2
        ml = [mods[l, :bsz, None, k * d:(k + 1) * d] for k in range(N_MOD)]
        mc = [jnp.broadcast_to(mods[l, bsz:bsz + 1, None, k * d:(k + 1) * d], (bsz, 1, d))
              for k in range(N_MOD)]
        lat = ffn_half(lat, norm_g[l, 0], ml[0], ml[1], ml[2], wg[l, 0], wu[l, 0], wd[l, 0])
        if run_ctx:
            cx = ffn_half(cx, norm_g[l, 0], mc[0], mc[1], mc[2], wg[l, 0], wu[l, 0], wd[l, 0])
        if l % 2 == 0:
            prm = (ev_w_in[i], ev_mu[i], ev_w0[i], ev_w_up[i], ev_a0[i], ev_a_up[i], ev_g_up[i],
                   ev_k_k[i], ev_k_a[i], ev_r_k[i], ev_gn_g[i], ev_gn_b[i],
                   ev_lam_re[i], ev_lam_im[i], ev_log_dt[i], ev_b_re[i], ev_b_im[i], ev_c_re[i], ev_c_im[i],
                   ev_d[i], ev_w_glu[i], ev_b_glu[i], ev_w_out[i])
            lat, cx = _even_mixer(lat, cx, ml, mc, norm_g[l, 1], prm, ctx_out)
        else:
            prm = (od_w_in[i], od_conv_w[i], od_conv_b[i], od_fw1[i], od_fb1[i], od_fw2[i], od_fb2[i],
                   od_fw3[i], od_fb3[i], od_fw4[i], od_freq[i], od_bias[i], od_w_out[i])
            lat = _hyena_mixer(lat, norm_g[l, 1], ml, prm)
            if ctx_out:
                cx = _hyena_mixer(cx, norm_g[l, 1], mc, prm)
        fin = final_g if l == depth - 1 else None
        lat = ffn_half(lat, norm_g[l, 2], ml[6], ml[7], ml[8], wg[l, 1], wu[l, 1], wd[l, 1], fin)
        if ctx_out:
            cx = ffn_half(cx, norm_g[l, 2], mc[6], mc[7], mc[8], wg[l, 1], wu[l, 1], wd[l, 1])
    return lat
```

```python
import functools
import math

import numpy as np
import jax
import jax.numpy as jnp
from jax import lax
from jax.experimental import pallas as pl
from jax.experimental.pallas import tpu as pltpu

F32 = jnp.float32
BF16 = jnp.bfloat16
HIGHEST = lax.Precision.HIGHEST

N_MOD = 9
NORM_EPS = 1e-6
GN_EPS = 64e-5
GRID_W = 64
HEAD_DIM = 64
HALF_HEAD = HEAD_DIM // 2
S5_GROUP = 16
S5_STATE = 64
S5_SUPER = 4
HY_ORDER = 2
HY_EMB = 33
HY_MIN_DECAY = math.log(1e-2) / 1.5
HY_MAX_DECAY = math.log(1e-2) / 0.3
LANES = 128
SUBLANES = 8
MXU_DIM = 256
FFT_N2 = 128
VMEM_LIMIT = 56 * 1024 * 1024


def _cparams(sem, vmem=VMEM_LIMIT):
    return pltpu.CompilerParams(dimension_semantics=sem, vmem_limit_bytes=vmem)


def _dot(a, b):
    return jnp.dot(a, b, preferred_element_type=F32)


def _dot_exact(a, b):
    return jnp.dot(a, b, preferred_element_type=F32, precision=HIGHEST)


def _rms_mod(x, gain, shift, scale):
    ms = jnp.mean(x * x, axis=-1, keepdims=True)
    return x * lax.rsqrt(ms + NORM_EPS) * gain * (1.0 + scale) + shift


def _sigmoid(x):
    return 1.0 / (1.0 + jnp.exp(-x))


def _silu(x):
    return x * _sigmoid(x)


def _ada_kernel(c_ref, w_ref, b_ref, o_ref):
    s = _silu(c_ref[...])
    o_ref[0] = _dot(s.astype(BF16), w_ref[0].astype(BF16)) + b_ref[0]


def ada_mods_all(cond8, ada_w, ada_b):
    depth, d, n = ada_w.shape
    tn = n // 8
    return pl.pallas_call(
        _ada_kernel,
        grid=(depth, n // tn),
        in_specs=[pl.BlockSpec((8, d), lambda l, j: (0, 0)),
                  pl.BlockSpec((1, d, tn), lambda l, j: (l, 0, j)),
                  pl.BlockSpec((1, 1, tn), lambda l, j: (l, 0, j))],
        out_specs=pl.BlockSpec((1, 8, tn), lambda l, j: (l, 0, j)),
        out_shape=jax.ShapeDtypeStruct((depth, 8, n), F32),
        compiler_params=_cparams(("parallel", "parallel")),
        name="ada_mods",
    )(cond8, ada_w, ada_b.reshape(depth, 1, n))


def _ffn_kernel(x_ref, gain_ref, shift_ref, scale_ref, gate_ref, wg_ref, wu_ref, wd_ref, fg_ref,
                o_ref, *, final_norm, chunks):
    x = x_ref[0]
    h = _rms_mod(x, gain_ref[...], shift_ref[0], scale_ref[0]).astype(BF16)
    acc = None
    for lo, hi in chunks:
        g = _dot(h, wg_ref[:, lo:hi])
        u = _dot(h, wu_ref[:, lo:hi])
        part = _dot((_silu(g) * u).astype(BF16), wd_ref[lo:hi, :])
        acc = part if acc is None else acc + part
    y = x + 0.5 * gate_ref[0] * acc
    if final_norm:
        ms = jnp.mean(y * y, axis=-1, keepdims=True)
        y = y * lax.rsqrt(ms + NORM_EPS) * fg_ref[...]
    o_ref[0] = y


def ffn_half(x, gain, shift, scale, gate, wg, wu, wd, final_g=None):
    bsz, t, d = x.shape
    ff = wg.shape[1]
    tm = min(1024, t)
    step = 2 * MXU_DIM
    chunks = tuple((lo, min(lo + step, ff)) for lo in range(0, ff, step))
    fg = jnp.ones((1, d), F32) if final_g is None else final_g.reshape(1, d)
    vec = pl.BlockSpec((1, 1, d), lambda b, i: (b, 0, 0))
    resident = lambda a: pl.BlockSpec(a.shape, lambda b, i: (0, 0), pipeline_mode=pl.Buffered(1))
    return pl.pallas_call(
        functools.partial(_ffn_kernel, final_norm=final_g is not None, chunks=chunks),
        grid=(bsz, t // tm),
        in_specs=[pl.BlockSpec((1, tm, d), lambda b, i: (b, i, 0)),
                  pl.BlockSpec((1, d), lambda b, i: (0, 0)),
                  vec, vec, vec, resident(wg), resident(wu), resident(wd),
                  pl.BlockSpec((1, d), lambda b, i: (0, 0))],
        out_specs=pl.BlockSpec((1, tm, d), lambda b, i: (b, i, 0)),
        out_shape=jax.ShapeDtypeStruct((bsz, t, d), F32),
        compiler_params=_cparams(("parallel", "parallel")),
        name="ffn_half",
    )(x, gain.reshape(1, d), shift, scale, gate, wg, wu, wd, fg)


def _modmm_kernel(cx_ref, lat_ref, gain_ref, shc_ref, scc_ref, shl_ref, scl_ref, w_ref, o_ref, u_ref, *, nc):
    is_ctx = pl.program_id(1) < nc
    x = jnp.where(is_ctx, cx_ref[0], lat_ref[0])
    shift = jnp.where(is_ctx, shc_ref[0], shl_ref[0])
    scale = jnp.where(is_ctx, scc_ref[0], scl_ref[0])
    h = _rms_mod(x, gain_ref[...], shift, scale).astype(BF16)
    n1 = o_ref.shape[2]
    o_ref[0] = _dot(h, w_ref[:, :n1])
    u_ref[0] = _dot(h, w_ref[:, n1:])


def mod_matmul_stream(cx, lat, gain, mc, ml, w, tm, n_first):
    bsz, t_ctx, d = cx.shape
    t_lat = lat.shape[1]
    n = w.shape[1]
    nc = t_ctx // tm
    nt = nc + t_lat // tm
    vec = pl.BlockSpec((1, 1, d), lambda b, i: (b, 0, 0))
    return pl.pallas_call(
        functools.partial(_modmm_kernel, nc=nc),
        grid=(bsz, nt),
        in_specs=[pl.BlockSpec((1, tm, d), lambda b, i: (b, jnp.minimum(i, nc - 1), 0)),
                  pl.BlockSpec((1, tm, d), lambda b, i: (b, jnp.maximum(i - nc, 0), 0)),
                  pl.BlockSpec((1, d), lambda b, i: (0, 0)),
                  vec, vec, vec, vec,
                  pl.BlockSpec((d, n), lambda b, i: (0, 0))],
        out_specs=[pl.BlockSpec((1, tm, n_first), lambda b, i: (b, i, 0)),
                   pl.BlockSpec((1, tm, n - n_first), lambda b, i: (b, i, 0))],
        out_shape=[jax.ShapeDtypeStruct((bsz, t_ctx + t_lat, n_first), F32),
                   jax.ShapeDtypeStruct((bsz, t_ctx + t_lat, n - n_first), F32)],
        compiler_params=_cparams(("parallel", "parallel")),
        name="mod_matmul",
    )(cx, lat, gain.reshape(1, d), mc[3], mc[4], ml[3], ml[4], w)


def _rwkv_feat_kernel(p_ref, pu_ref, pd_ref, mu_ref, wup_ref, w0_ref, aup_ref, a0_ref, gup_ref,
                      ka_ref, rk_ref, ones_ref,
                      v_o, g_o, bonus_o, r_o, k_o, w_o, a_o, *, nt, width):
    i = pl.program_id(1)
    is_ctx = i == 0
    p = p_ref[0]
    tm = p.shape[0]
    row = lax.broadcasted_iota(jnp.int32, p.shape, 0)
    lane = lax.broadcasted_iota(jnp.int32, p.shape, 1) % 4
    prev = pltpu.roll(p, 1, 0)
    nxt = pltpu.roll(p, tm - 1, 0)
    col = jnp.where(is_ctx, row, row % GRID_W)
    last = jnp.where(is_ctx, tm - 1, GRID_W - 1)
    left = jnp.where(col == 0, 0.0, prev)
    right = jnp.where(col == last, 0.0, nxt)
    up_halo = jnp.where(i > 1, pu_ref[0], 0.0)
    dn_halo = jnp.where(i < nt - 1, pd_ref[0], 0.0)
    up = jnp.where(is_ctx, left, jnp.concatenate([up_halo, p[:tm - GRID_W]], axis=0))
    down = jnp.where(is_ctx, right, jnp.concatenate([p[GRID_W:], dn_halo], axis=0))
    shifted = jnp.where(lane == 0, left, jnp.where(lane == 1, right, jnp.where(lane == 2, up, down)))
    q = p + mu_ref[...] * (shifted - p)

    w = width
    r, k, v = q[:, :w], q[:, w:2 * w], q[:, 2 * w:3 * w]
    wd = q[:, 3 * w:3 * w + LANES]
    ad = q[:, 3 * w + LANES:3 * w + 2 * LANES]
    gd = q[:, 3 * w + 2 * LANES:3 * w + 3 * LANES]

    zlin = w0_ref[...] + _dot(jnp.tanh(wd).astype(BF16), wup_ref[...])
    neg = -zlin
    softplus = jnp.maximum(neg, 0.0) + jnp.log(1.0 + jnp.exp(-jnp.abs(neg)))
    decay = jnp.exp(-jnp.exp(-softplus - 0.5))
    a = _sigmoid(a0_ref[...] + _dot(ad.astype(BF16), aup_ref[...]))
    g_o[0] = _dot(_sigmoid(gd).astype(BF16), gup_ref[...])

    ones = ones_ref[...]
    for d in range(2):
        v_o[d, 0] = v
        r_o[d, 0] = r
        k_o[d, 0] = k
    ksum = jnp.zeros_like(k)
    for d in range(2):
        a_d = a[:, d * w:(d + 1) * w]
        ksum = ksum + k * (1.0 + (a_d - 1.0) * ka_ref[...])
        w_o[d, 0] = decay[:, d * w:(d + 1) * w]
        a_o[d, 0] = a_d
    bonus_o[0] = _dot_exact(r * ksum * rk_ref[...], ones) * v


def rwkv_features(p, n_cols, width, consts, tm):
    bsz, t, _ = p.shape
    nt = t // tm
    hb = tm // GRID_W
    nhb = t // GRID_W
    w = width
    full = lambda a: pl.BlockSpec(a.shape, lambda b, i: (0,) * a.ndim)
    tok = pl.BlockSpec((1, tm, w), lambda b, i: (b, i, 0))
    tok2 = pl.BlockSpec((2, 1, tm, w), lambda b, i: (0, b, i, 0))
    sds = jax.ShapeDtypeStruct((bsz, t, w), F32)
    sds2 = jax.ShapeDtypeStruct((2, bsz, t, w), F32)
    return pl.pallas_call(
        functools.partial(_rwkv_feat_kernel, nt=nt, width=w),
        grid=(bsz, nt),
        in_specs=[pl.BlockSpec((1, tm, n_cols), lambda b, i: (b, i, 0)),
                  pl.BlockSpec((1, GRID_W, n_cols), lambda b, i: (b, jnp.maximum(i * hb - 1, 0), 0)),
                  pl.BlockSpec((1, GRID_W, n_cols), lambda b, i: (b, jnp.minimum((i + 1) * hb, nhb - 1), 0))]
                 + [full(a) for a in consts],
        out_specs=[tok2, tok, tok, tok2, tok2, tok2, tok2],
        out_shape=[sds2, sds, sds, sds2, sds2, sds2, sds2],
        compiler_params=_cparams(("parallel", "parallel")),
        name="rwkv_features",
    )(p, p, p, *consts)


def _rwkv_scan_kernel(rf, rb, kf, kb, wf, wb, af, ab, vf, vb, ka_ref, kks_ref, of_ref, ob_ref,
                      s_ref, vec_ref, v_ref, c_ref, look_ref, *, tc):
    @pl.when(pl.program_id(0) == 0)
    def _():
        s_ref[...] = jnp.zeros_like(s_ref)
        look_ref[:, tc] = jnp.zeros((look_ref.shape[0],) + look_ref.shape[2:], F32)

    nj, ni = s_ref.shape[0], s_ref.shape[1]
    quarter = LANES // 4

    def fold(x):
        hi = (lax.broadcasted_iota(jnp.int32, x.shape, 1) % (2 * quarter)) >= quarter
        return x + jnp.where(hi, pltpu.roll(x, quarter, 1), pltpu.roll(x, 3 * quarter, 1))

    def hsum(x):
        return jnp.sum(x, axis=0, keepdims=True)

    def jsum(x):
        return fold(jnp.broadcast_to(hsum(x), (SUBLANES, LANES)))[0:1]

    ka = ka_ref[...]
    kks = kks_ref[...]
    fwd_lane = lax.broadcasted_iota(jnp.int32, (nj, LANES), 1) < LANES // 2
    def merged(f, b, t):
        return jnp.where(fwd_lane, f[t], b[tc - 1 - t])

    cum = jnp.ones((nj, LANES), F32)
    for t in range(tc):
        w = merged(wf, wb, t)
        cum = cum * w
        kk = merged(kf, kb, t) * kks
        look_ref[0, t] = kk / jnp.maximum(jnp.sqrt(jsum(kk * kk)), 1e-12)
        look_ref[1, t] = w * merged(rf, rb, t)
        look_ref[2, t] = cum
    for t in range(tc):
        r, k, a = merged(rf, rb, t), merged(kf, kb, t), merged(af, ab, t)
        kk, kk_next, wr_next = look_ref[0, t], look_ref[0, t + 1], look_ref[1, t + 1]
        cum = look_ref[2, t]
        inv = 1.0 / cum
        kka = kk * a
        kd = k * (1.0 + (a - 1.0) * ka)
        vec_ref[0, t] = kka * inv
        vec_ref[1, t] = kd * inv
        vec_ref[2, t] = cum * kk_next
        vec_ref[3, t] = cum * wr_next
        c_ref[t, 0:1, :] = jsum(kka * r)
        c_ref[t, 1:2, :] = jsum(kd * r)
        c_ref[t, 2:3, :] = hsum(kka * kk_next)
        c_ref[t, 3:4, :] = hsum(kd * kk_next)
        c_ref[t, 4:5, :] = hsum(kka * wr_next)
        c_ref[t, 5:6, :] = hsum(kd * wr_next)
        v_ref[t] = jnp.concatenate([vf[t], vb[tc - 1 - t]], axis=-1)

    def state_sums(y1, y2):
        s1 = jnp.zeros((ni, LANES), F32)
        s2 = jnp.zeros((ni, LANES), F32)
        for j in range(nj):
            sj = s_ref[j]
            s1 = s1 + sj * y1(j)
            s2 = s2 + sj * y2(j)
        return s1, s2

    def step(t, carry):
        row = lambda n: (lambda j: vec_ref[n, t, j:j + 1, :])
        a1, a2 = state_sums(row(2), row(3))
        sa, o1 = fold(carry[0]), fold(carry[1])
        v = v_ref[t]
        out = o1 - sa * c_ref[t, 0:1, :] + v * c_ref[t, 1:2, :]
        of_ref[t] = out
        ob_ref[tc - 1 - t] = out
        nxt = (a1 - sa * c_ref[t, 2:3, :] + v * c_ref[t, 3:4, :],
               a2 - sa * c_ref[t, 4:5, :] + v * c_ref[t, 5:6, :])
        for j in range(nj):
            s_ref[j] = s_ref[j] - sa * row(0)(j) + v * row(1)(j)
        return nxt

    first = state_sums(lambda j: look_ref[0, 0, j:j + 1, :], lambda j: look_ref[1, 0, j:j + 1, :])
    lax.fori_loop(0, tc, step, first)
    for j in range(nj):
        s_ref[j] = s_ref[j] * look_ref[2, tc - 1, j:j + 1, :]


def rwkv_scan(r, k, w, a, v, ka, kks, t_ctx):
    ttot, nj, _ = r.shape
    ni, nc = v.shape[1], v.shape[2]
    tc = 32
    ncb, ntb = t_ctx // tc, ttot // tc
    fwd = lambda g: (g, 0, 0)
    bwd = lambda g: (jnp.where(g < ncb, ncb - 1 - g, ntb + ncb - 1 - g), 0, 0)
    jf, jb = pl.BlockSpec((tc, nj, LANES), fwd), pl.BlockSpec((tc, nj, LANES), bwd)
    vf, vb = pl.BlockSpec((tc, ni, nc), fwd), pl.BlockSpec((tc, ni, nc), bwd)
    of, ob = pl.BlockSpec((tc, ni, LANES), fwd), pl.BlockSpec((tc, ni, LANES), bwd)
    osd = jax.ShapeDtypeStruct((ttot, ni, LANES), F32)
    return pl.pallas_call(
        functools.partial(_rwkv_scan_kernel, tc=tc),
        grid=(ntb,),
        in_specs=[jf, jb] * 4 + [vf, vb, pl.BlockSpec(ka.shape, lambda g: (0, 0)),
                                 pl.BlockSpec(kks.shape, lambda g: (0, 0))],
        out_specs=[of, ob],
        out_shape=[osd, osd],
        scratch_shapes=[pltpu.VMEM((nj, ni, LANES), F32), pltpu.VMEM((4, tc, nj, LANES), F32),
                        pltpu.VMEM((tc, ni, LANES), F32), pltpu.VMEM((tc, SUBLANES, LANES), F32),
                        pltpu.VMEM((3, tc + 1, nj, LANES), F32)],
        compiler_params=_cparams(("arbitrary",)),
        name="rwkv_scan",
    )(r, r, k, k, w, w, a, a, v, v, ka, kks)


def _s5_kernel(uf_ref, ub_ref, bm_ref, cm_ref, a_ref, yf_ref, yb_ref, h_ref, bu_ref, ubr_ref, *, tt):
    @pl.when(pl.program_id(0) == 0)
    def _():
        h_ref[...] = jnp.zeros_like(h_ref)

    nsb = bm_ref.shape[0]
    kin = bm_ref.shape[1] // 2
    sw = bm_ref.shape[2]
    hw = sw // 2
    for t in range(tt):
        ubr_ref[t] = ub_ref[tt - 1 - t]
    nothing = jnp.zeros(uf_ref.shape, F32)
    uf = jnp.concatenate([uf_ref[...], nothing], axis=1)
    ub = jnp.concatenate([nothing, ubr_ref[...]], axis=1)
    for sb in range(nsb):
        lhs = jnp.concatenate([uf[:, :, sb * kin:(sb + 1) * kin], ub[:, :, sb * kin:(sb + 1) * kin]], axis=-1)
        lhs = lhs.reshape(tt * SUBLANES, 2 * kin).astype(BF16)
        bu_ref[:, sb * sw:(sb + 1) * sw] = _dot(lhs, bm_ref[sb])

    def step(t, hs):
        rows = pl.ds(pl.multiple_of(t * SUBLANES, SUBLANES), SUBLANES)
        out = []
        for sb in range(nsb):
            hr, hi = hs[2 * sb], hs[2 * sb + 1]
            lo = sb * sw
            ar = a_ref[0, :, lo:lo + hw]
            ai = a_ref[0, :, lo + hw:lo + sw]
            nr = ar * hr - ai * hi + bu_ref[rows, lo:lo + hw]
            ni = ar * hi + ai * hr + bu_ref[rows, lo + hw:lo + sw]
            bu_ref[rows, lo:lo + hw] = nr
            bu_ref[rows, lo + hw:lo + sw] = ni
            out += [nr, ni]
        return tuple(out)

    h0 = []
    for sb in range(nsb):
        h0 += [h_ref[:, sb * sw:sb * sw + hw], h_ref[:, sb * sw + hw:(sb + 1) * sw]]
    hs = lax.fori_loop(0, tt, step, tuple(h0))
    for sb in range(nsb):
        h_ref[:, sb * sw:sb * sw + hw] = hs[2 * sb]
        h_ref[:, sb * sw + hw:(sb + 1) * sw] = hs[2 * sb + 1]

    nout = cm_ref.shape[2] // 2
    fwd_row = (lax.broadcasted_iota(jnp.int32, (tt * SUBLANES, nout), 0) % SUBLANES) < SUBLANES // 2
    ys = []
    for sb in range(nsb):
        yy = _dot(bu_ref[:, sb * sw:(sb + 1) * sw].astype(BF16), cm_ref[sb])
        ys.append(jnp.where(fwd_row, yy[:, :nout], yy[:, nout:]))
    y = jnp.concatenate(ys, axis=1).reshape(tt, SUBLANES, nsb * nout)
    yf_ref[...] = y
    for t in range(tt):
        yb_ref[tt - 1 - t] = y[t]


def s5_scan(u, bm, cm, a, t_ctx):
    ttot, bsz, width = u.shape
    rows = 2 * bsz
    nsb, _, sw = bm.shape
    tt = 64
    ncb, ntb = t_ctx // tt, ttot // tt
    fmap = lambda g: (g, 0, 0)
    bmap = lambda g: (jnp.where(g < ncb, ncb - 1 - g, ntb + ncb - 1 - g), 0, 0)
    full = lambda x: pl.BlockSpec(x.shape, lambda g: (0,) * x.ndim)
    osd = jax.ShapeDtypeStruct((ttot, rows, width), F32)
    return pl.pallas_call(
        functools.partial(_s5_kernel, tt=tt),
        grid=(ntb,),
        in_specs=[pl.BlockSpec((tt, bsz, width), fmap), pl.BlockSpec((tt, bsz, width), bmap),
                  full(bm), full(cm), full(a)],
        out_specs=[pl.BlockSpec((tt, rows, width), fmap), pl.BlockSpec((tt, rows, width), bmap)],
        out_shape=[osd, osd],
        scratch_shapes=[pltpu.VMEM((rows, nsb * sw), F32), pltpu.VMEM((tt * rows, nsb * sw), F32),
                        pltpu.VMEM((tt, bsz, width), F32)],
        compiler_params=_cparams(("arbitrary",)),
        name="s5_scan",
    )(u, u, bm, cm, a)


def _even_out_kernel(x_ref, of_ref, ob_ref, bonus_ref, g_ref, ys_ref, u_ref, m5_ref,
                     ones_ref, gng_ref, gnb_ref, dskip_ref, wglu_ref, bglu_ref, wo1_ref, wo2_ref, o_ref):
    ones = ones_ref[...]
    inv = 1.0 / HEAD_DIM
    o = of_ref[0] + ob_ref[0]
    mean = _dot_exact(o, ones) * inv
    oc = o - mean
    var = _dot_exact(oc * oc, ones) * inv
    y1 = (oc * lax.rsqrt(var + GN_EPS) * gng_ref[...] + gnb_ref[...] + bonus_ref[0]) * g_ref[0]
    y = ys_ref[0] + dskip_ref[...] * u_ref[0]
    y = jax.nn.gelu(y)
    y2 = y * _sigmoid(_dot(y.astype(BF16), wglu_ref[...]) + bglu_ref[...])
    out = _dot(y1.astype(BF16), wo1_ref[...]) + _dot(y2.astype(BF16), wo2_ref[...])
    o_ref[0] = x_ref[0] + m5_ref[0] * out


def even_out(x, off, o_f, o_b, bonus, g, ys, u, m5, consts, tm):
    bsz, t, d = x.shape
    w = o_f.shape[2]
    tokd = pl.BlockSpec((1, tm, d), lambda b, i: (b, i, 0))
    tokw = pl.BlockSpec((1, tm, w), lambda b, i: (b, i + off, 0))
    toks = pl.BlockSpec((1, tm, u.shape[2]), lambda b, i: (b, i + off, 0))
    full = lambda a: pl.BlockSpec(a.shape, lambda b, i: (0,) * a.ndim)
    return pl.pallas_call(
        _even_out_kernel,
        grid=(bsz, t // tm),
        in_specs=[tokd, tokw, tokw, tokw, tokw, toks, toks,
                  pl.BlockSpec((1, 1, d), lambda b, i: (b, 0, 0))] + [full(a) for a in consts],
        out_specs=tokd,
        out_shape=jax.ShapeDtypeStruct((bsz, t, d), F32),
        compiler_params=_cparams(("parallel", "parallel")),
        name="even_out",
    )(x, o_f, o_b, bonus, g, ys, u, m5, *consts)


def _hy_in_kernel(x_ref, xp_ref, xn_ref, gain_ref, shift_ref, scale_ref, w_ref, cw_ref, cb_ref,
                  z_ref, zb_ref, g1_ref, g2_ref, *, nt, c):
    i = pl.program_id(1)
    gain, shift, scale = gain_ref[...], shift_ref[0], scale_ref[0]
    tm = x_ref.shape[1]
    halo = xp_ref.shape[1]
    h = jnp.concatenate([_rms_mod(xr[0], gain, shift, scale) for xr in (xp_ref, x_ref, xn_ref)], axis=0)
    h = h.astype(BF16)
    row = lax.broadcasted_iota(jnp.int32, (tm, c), 0)
    outs = (z_ref, g1_ref, g2_ref)
    for part in range(3):
        wp = w_ref[:, part * c:(part + 1) * c]
        p_ext = _dot(h, wp)
        p = p_ext[halo:halo + tm]
        pp = jnp.where(i > 0, p_ext[halo - 1:halo], 0.0)
        pn = jnp.where(i < nt - 1, p_ext[halo + tm:halo + tm + 1], 0.0)
        pm1 = jnp.where(row == 0, pp, pltpu.roll(p, 1, 0))
        pp1 = jnp.where(row == tm - 1, pn, pltpu.roll(p, tm - 1, 0))
        cw = cw_ref[:, part * c:(part + 1) * c]
        q = cw[0:1] * pm1 + cw[1:2] * p + cw[2:3] * pp1 + cb_ref[:, part * c:(part + 1) * c]
        outs[part][0] = q
        if part == 0:
            zb_ref[0] = q.astype(BF16)


def hyena_in(x, gain, shift, scale, w, conv_w, conv_b):
    bsz, t, d = x.shape
    c = w.shape[1] // 3
    tm = min(512, t)
    nt = t // tm
    hb = tm // 8
    nhb = t // 8
    vec = pl.BlockSpec((1, 1, d), lambda b, i: (b, 0, 0))
    tok = pl.BlockSpec((1, tm, c), lambda b, i: (b, i, 0))
    sds = jax.ShapeDtypeStruct((bsz, t, c), F32)
    return pl.pallas_call(
        functools.partial(_hy_in_kernel, nt=nt, c=c),
        grid=(bsz, nt),
        in_specs=[pl.BlockSpec((1, tm, d), lambda b, i: (b, i, 0)),
                  pl.BlockSpec((1, 8, d), lambda b, i: (b, jnp.maximum(i * hb - 1, 0), 0)),
                  pl.BlockSpec((1, 8, d), lambda b, i: (b, jnp.minimum((i + 1) * hb, nhb - 1), 0)),
                  pl.BlockSpec((1, d), lambda b, i: (0, 0)),
                  vec, vec,
                  pl.BlockSpec(w.shape, lambda b, i: (0, 0)),
                  pl.BlockSpec(conv_w.shape, lambda b, i: (0, 0)),
                  pl.BlockSpec((1, 3 * c), lambda b, i: (0, 0))],
        out_specs=[tok, tok, tok, tok],
        out_shape=[sds, jax.ShapeDtypeStruct((bsz, t, c), BF16), sds, sds],
        compiler_params=_cparams(("parallel", "parallel")),
        name="hyena_in",
    )(x, x, x, gain.reshape(1, d), shift, scale, w, conv_w, conv_b.reshape(1, 3 * c))


def _pack_complex(re, im):
    bits = lambda x: lax.bitcast_convert_type(x.astype(BF16).astype(F32), jnp.uint32)
    return lax.shift_right_logical(bits(re), jnp.uint32(16)) | bits(im)


def _unpack_complex(w):
    re = lax.bitcast_convert_type(lax.shift_left(w, jnp.uint32(16)), F32)
    im = lax.bitcast_convert_type(w & jnp.uint32(0xFFFF0000), F32)
    return jnp.concatenate([re, im], axis=0).astype(BF16)


def _fft_conv_kernel(z_ref, kf_ref, g_ref, w1_ref, g2i_ref, f1c_ref, twt_ref, o_ref, a_ref, t_ref,
                     *, n1, k1c):
    s = pl.program_id(2)
    n2 = FFT_N2
    nd = n1 // 2
    slab = n1 // k1c

    @pl.when(s == 0)
    def _():
        w1 = w1_ref[...]

        def body(j, carry):
            x = jnp.concatenate([z_ref[0, j], z_ref[1, j]], axis=0)
            a = _dot(w1, x)
            a_ref[pl.ds(pl.multiple_of(j * n1, n1), n1), :] = _pack_complex(a[:n1], a[n1:])
            return carry

        lax.fori_loop(0, n2, body, 0, unroll=8)

    @pl.when((s > 0) & (s <= k1c))
    def _():
        g2i = g2i_ref[...]
        base = (s - 1) * slab

        def body(kk, carry):
            k1 = base + kk
            ak = _unpack_complex(a_ref[pl.ds(k1, n2, stride=n1), :])
            x = _dot(g_ref[kk], ak)
            xr, xi = x[:n2], x[n2:]
            kr, ki = kf_ref[0, kk], kf_ref[1, kk]
            pr = xr * kr - xi * ki
            pi = xr * ki + xi * kr
            tt = _dot(g2i, jnp.concatenate([pr, pi], axis=0).astype(BF16))
            t_ref[pl.ds(pl.multiple_of(k1 * n2, n2), n2), :] = _pack_complex(tt[:n2], tt[n2:])
            return carry

        lax.fori_loop(0, slab, body, 0, unroll=16)

    @pl.when(s == k1c + 1)
    def _():
        f1r, f1i = f1c_ref[0], f1c_ref[1]

        def body(j, carry):
            twr = twt_ref[0, pl.ds(j, 1), :]
            twi = twt_ref[1, pl.ds(j, 1), :]
            wr = f1r * twr + f1i * twi
            wi = f1i * twr - f1r * twi
            w3 = jnp.concatenate([jnp.concatenate([wr, -wi], axis=1),
                                  jnp.concatenate([wi, wr], axis=1)], axis=0).astype(BF16)
            tn = _unpack_complex(t_ref[pl.ds(j, n1, stride=n2), :])
            y = _dot(w3, tn)
            o_ref[0, j] = y[:nd]
            o_ref[1, j] = y[nd:]
            return carry

        lax.fori_loop(0, n2, body, 0, unroll=16)


def _fft_consts(n1):
    n2 = FFT_N2
    n = n1 * n2
    nd = n1 // 2
    k1 = np.arange(n1)
    f1 = np.exp(-2j * np.pi * np.outer(k1, np.arange(nd)) / n1)
    w1 = np.block([[f1.real, -f1.imag], [f1.imag, f1.real]])
    f2 = np.exp(-2j * np.pi * np.outer(np.arange(n2), np.arange(n2)) / n2)
    f2c = np.conj(f2)
    g2i = np.block([[f2c.real, -f2c.imag], [f2c.imag, f2c.real]])
    f1c = np.exp(2j * np.pi * np.outer(np.arange(nd), k1) / n1) / n
    tw = np.exp(-2j * np.pi * np.outer(k1, np.arange(n2)) / n)
    cplx = lambda m: jnp.asarray(np.stack([m.real, m.imag]), F32)
    (f2r, f2i), (twr, twi) = cplx(f2), cplx(tw)
    gr = f2r[None] * twr[:, None, :] - f2i[None] * twi[:, None, :]
    gi = f2r[None] * twi[:, None, :] + f2i[None] * twr[:, None, :]
    gtw = jnp.concatenate([jnp.concatenate([gr, -gi], axis=2), jnp.concatenate([gi, gr], axis=2)], axis=1)
    return (gtw.astype(BF16), jnp.asarray(w1, BF16), jnp.asarray(g2i, BF16), cplx(f1c), cplx(tw.T))


def fft_conv(zt, kf, order):
    bsz, n2, nd, c = zt.shape
    n1 = 2 * nd
    cb = LANES
    k1c = 8 if n1 % 8 == 0 and n1 >= 64 else 2
    slab = n1 // k1c
    gtw, *consts = _fft_consts(n1)
    full = lambda a: pl.BlockSpec(a.shape, lambda j, p, s: (0,) * a.ndim)
    blk = pl.BlockSpec((2, n2, nd, cb), lambda j, p, s: (p, 0, 0, j))
    chunk = lambda j, p, s: jnp.clip(s - 1, 0, k1c - 1)
    return pl.pallas_call(
        functools.partial(_fft_conv_kernel, n1=n1, k1c=k1c),
        grid=(c // cb, bsz // 2, k1c + 2),
        in_specs=[blk,
                  pl.BlockSpec((None, 2, slab, n2, cb), lambda j, p, s: (order, 0, chunk(j, p, s), 0, j)),
                  pl.BlockSpec((slab, 2 * n2, 2 * n2), lambda j, p, s: (chunk(j, p, s), 0, 0))]
                 + [full(a) for a in consts],
        out_specs=blk,
        out_shape=jax.ShapeDtypeStruct(zt.shape, F32),
        scratch_shapes=[pltpu.VMEM((n2 * n1, cb), jnp.uint32), pltpu.VMEM((n1 * n2, cb), jnp.uint32)],
        compiler_params=_cparams(("parallel", "parallel", "arbitrary")),
        name="fft_conv",
    )(zt, kf, gtw, *consts)


def _fft_fwd_kernel(z_ref, scale_ref, g_ref, w1_ref, o_ref, a_ref, *, n1, k1c):
    s = pl.program_id(2)
    n2 = FFT_N2
    slab = n1 // k1c
    cb = o_ref.shape[4]

    @pl.when(s == 0)
    def _():
        w1 = w1_ref[...]

        def body(j, carry):
            a = _dot(w1, jnp.concatenate([z_ref[0, j], z_ref[1, j]], axis=1))
            packed = _pack_complex(a[:n1], a[n1:])
            rows = pl.ds(pl.multiple_of(j * n1, n1), n1)
            a_ref[0, rows, :] = packed[:, :cb]
            a_ref[1, rows, :] = packed[:, cb:]
            return carry

        lax.fori_loop(0, n2, body, 0, unroll=8)

    @pl.when(s > 0)
    def _():
        base = (s - 1) * slab

        def body(kk, carry):
            k1 = base + kk
            rows = pl.ds(k1, n2, stride=n1)
            ak = jnp.concatenate([_unpack_complex(a_ref[0, rows, :]), _unpack_complex(a_ref[1, rows, :])], axis=1)
            x = _dot(g_ref[kk], ak)
            pos, neg = x[:, :cb], x[:, cb:]
            scale = scale_ref[0]
            o_ref[0, 0, kk] = (pos[:n2] + neg[:n2]) * scale
            o_ref[0, 1, kk] = (pos[n2:] - neg[n2:]) * scale
            return carry

        lax.fori_loop(0, slab, body, 0, unroll=16)


def filter_spectrum(taps, scale):
    orders, _, n2, nd, c = taps.shape
    n1 = 2 * nd
    cb = LANES
    k1c = 8 if n1 % 8 == 0 and n1 >= 64 else 2
    slab = n1 // k1c
    gtw, w1 = _fft_consts(n1)[:2]
    w1 = w1[:, :nd]
    full = lambda a: pl.BlockSpec(a.shape, lambda j, r, s: (0,) * a.ndim)
    return pl.pallas_call(
        functools.partial(_fft_fwd_kernel, n1=n1, k1c=k1c),
        grid=(c // cb, orders, k1c + 1),
        in_specs=[pl.BlockSpec((None, 2, n2, nd, cb), lambda j, r, s: (r, 0, 0, 0, j)),
                  pl.BlockSpec((1, 1, cb), lambda j, r, s: (r, 0, j)),
                  pl.BlockSpec((slab, 2 * n2, 2 * n2), lambda j, r, s: (jnp.maximum(s - 1, 0), 0, 0)),
                  full(w1)],
        out_specs=pl.BlockSpec((1, 2, slab, n2, cb), lambda j, r, s: (r, 0, jnp.maximum(s - 1, 0), 0, j)),
        out_shape=jax.ShapeDtypeStruct((orders, 2, n1, n2, c), F32),
        scratch_shapes=[pltpu.VMEM((2, n2 * n1, cb), jnp.uint32)],
        compiler_params=_cparams(("parallel", "parallel", "arbitrary")),
        name="fft_forward",
    )(taps, scale, gtw, w1)


def _hy_filter_kernel(feat_ref, fw1_ref, fb1_ref, fw2_ref, fb2_ref, fw3_ref, fb3_ref, fw4_ref, freq_ref,
                      delta_ref, z_ref, ss_ref):
    @pl.when(pl.program_id(1) == 0)
    def _():
        ss_ref[...] = jnp.zeros_like(ss_ref)

    feats = feat_ref[...]
    t = feats[:, 0:1]
    fr = freq_ref[...]
    h = jnp.sin(fr * (_dot_exact(feats, fw1_ref[...]) + fb1_ref[...]))
    h = jnp.sin(fr * (_dot_exact(h, fw2_ref[...]) + fb2_ref[...]))
    h = jnp.sin(fr * (_dot_exact(h, fw3_ref[...]) + fb3_ref[...]))
    lag0 = jnp.where(pl.program_id(0) % 2 == 1, 0.0, 1.0)
    filt = _dot(h.astype(BF16), fw4_ref[...]) * (jnp.exp(-t * delta_ref[...]) * jnp.where(t == 0.0, lag0, 1.0))
    z_ref[0] = filt.astype(BF16)
    ss_ref[0] += jnp.sum(filt * filt, axis=0, keepdims=True)


def hyena_filter(t, fw1, fb1, fw2, fb2, fw3, fb3, fw4, freq, width):
    nd = t // FFT_N2
    pos = (jnp.arange(FFT_N2, dtype=F32)[:, None] + FFT_N2 * jnp.arange(nd, dtype=F32)[None, :]).reshape(-1, 1)
    tt = pos / max(t - 1, 1)
    ang = 2 * math.pi * pos / t
    nb = (HY_EMB - 1) // 2
    bands = jnp.linspace(1e-4, nb - 1, nb, dtype=F32)[None]
    feats = jnp.concatenate([tt, jnp.cos(bands * ang), -jnp.sin(bands * ang),
                             jnp.zeros((t, LANES - HY_EMB), F32)], axis=-1)
    deltas = jnp.abs(jnp.linspace(HY_MIN_DECAY, HY_MAX_DECAY, width, dtype=F32)).reshape(1, width)
    rows = 2 * HY_ORDER
    tm = min(512, t)
    hid = fw1.shape[1]
    full = lambda a: pl.BlockSpec(a.shape, lambda r, i: (0,) * a.ndim)
    fw1p = jnp.concatenate([fw1, jnp.zeros((LANES - HY_EMB, hid), F32)], axis=0)
    vecs = [fw1p, fb1.reshape(1, hid), fw2, fb2.reshape(1, hid), fw3, fb3.reshape(1, hid)]
    return pl.pallas_call(
        _hy_filter_kernel,
        grid=(rows, t // tm),
        in_specs=[pl.BlockSpec((tm, LANES), lambda r, i: (i, 0))] + [full(a) for a in vecs]
                 + [pl.BlockSpec((hid, width), lambda r, i: (0, r)), full(freq.reshape(1, hid)), full(deltas)],
        out_specs=[pl.BlockSpec((1, tm, width), lambda r, i: (r, i, 0)),
                   pl.BlockSpec((1, 1, width), lambda r, i: (r, 0, 0))],
        out_shape=[jax.ShapeDtypeStruct((rows, t, width), BF16), jax.ShapeDtypeStruct((rows, 1, width), F32)],
        compiler_params=_cparams(("parallel", "arbitrary")),
        name="hyena_filter",
    )(feats, *vecs, fw4.astype(BF16), freq.reshape(1, hid), deltas)


def hyena_filter_spectrum_fft(t, fw1, fb1, fw2, fb2, fw3, fb3, fw4, freq, width):
    taps, ss = hyena_filter(t, fw1, fb1, fw2, fb2, fw3, fb3, fw4, freq, width)
    nd = t // FFT_N2
    scale = lax.rsqrt(ss.reshape(HY_ORDER, 2, width).sum(axis=1, keepdims=True) + 1e-6)
    return filter_spectrum(taps.reshape(HY_ORDER, 2, FFT_N2, nd, width), scale)


def _dft_conv_kernel(z_ref, kf_ref, fw_ref, iv_ref, o_ref, *, t):
    x = jnp.concatenate([z_ref[0], z_ref[1]], axis=0)
    spec = _dot(fw_ref[...], x)
    xr, xi = spec[:2 * t], spec[2 * t:]
    kr, ki = kf_ref[0], kf_ref[1]
    pr = xr * kr - xi * ki
    pi = xr * ki + xi * kr
    y = _dot(iv_ref[...], jnp.concatenate([pr, pi], axis=0).astype(BF16))
    o_ref[0] = y[:t]
    o_ref[1] = y[t:]


def dft_conv(zb, kf):
    bsz, t, c = zb.shape
    n = 2 * t
    f = np.exp(-2j * np.pi * np.outer(np.arange(n), np.arange(t)) / n)
    fw = np.block([[f.real, -f.imag], [f.imag, f.real]])
    fi = np.exp(2j * np.pi * np.outer(np.arange(t), np.arange(n)) / n) / n
    iv = np.block([[fi.real, -fi.imag], [fi.imag, fi.real]])
    fw, iv = jnp.asarray(fw, BF16), jnp.asarray(iv, BF16)
    cb = LANES
    blk = pl.BlockSpec((2, t, cb), lambda j, p: (p, 0, j))
    return pl.pallas_call(
        functools.partial(_dft_conv_kernel, t=t),
        grid=(c // cb, bsz // 2),
        in_specs=[blk, pl.BlockSpec((2, n, cb), lambda j, p: (0, 0, j)),
                  pl.BlockSpec(fw.shape, lambda j, p: (0, 0)), pl.BlockSpec(iv.shape, lambda j, p: (0, 0))],
        out_specs=blk,
        out_shape=jax.ShapeDtypeStruct(zb.shape, F32),
        compiler_params=_cparams(("parallel", "parallel")),
        name="dft_conv",
    )(zb, kf, fw, iv)


def _conv_rows(conv_ref):
    if len(conv_ref.shape) == 3:
        return conv_ref[0]
    return jnp.concatenate([conv_ref[0, :, k, :] for k in range(conv_ref.shape[2])], axis=0)


def _conv_spec(conv, tm, c):
    if conv.ndim == 3:
        return pl.BlockSpec((1, tm, c), lambda b, i: (b, i, 0))
    return pl.BlockSpec((1, FFT_N2, tm // FFT_N2, c), lambda b, i: (b, 0, i, 0))


def _hy_gate_kernel(conv_ref, z_ref, gate_ref, bias_ref, o_ref, ob_ref):
    y = gate_ref[0] * (_conv_rows(conv_ref) + bias_ref[...] * z_ref[0])
    o_ref[0] = y
    ob_ref[0] = y.astype(BF16)


def hyena_gate(conv, z, gate, bias):
    bsz, t, c = z.shape
    tm = min(SUBLANES * FFT_N2, t)
    tok = pl.BlockSpec((1, tm, c), lambda b, i: (b, i, 0))
    return pl.pallas_call(
        _hy_gate_kernel,
        grid=(bsz, t // tm),
        in_specs=[_conv_spec(conv, tm, c), tok, tok, pl.BlockSpec((1, c), lambda b, i: (0, 0))],
        out_specs=[tok, tok],
        out_shape=[jax.ShapeDtypeStruct(z.shape, F32), jax.ShapeDtypeStruct(z.shape, BF16)],
        compiler_params=_cparams(("parallel", "parallel")),
        name="hyena_gate",
    )(conv, z, gate, bias.reshape(1, c))


def _hy_out_kernel(x_ref, conv_ref, z_ref, gate_ref, bias_ref, w_ref, m5_ref, o_ref):
    y = gate_ref[0] * (_conv_rows(conv_ref) + bias_ref[...] * z_ref[0])
    o_ref[0] = x_ref[0] + m5_ref[0] * _dot(y.astype(BF16), w_ref[...])


def hyena_out(x, conv, z, gate, bias, w, m5):
    bsz, t, d = x.shape
    c = z.shape[2]
    tm = min(SUBLANES * FFT_N2, t)
    tokd = pl.BlockSpec((1, tm, d), lambda b, i: (b, i, 0))
    tokc = pl.BlockSpec((1, tm, c), lambda b, i: (b, i, 0))
    return pl.pallas_call(
        _hy_out_kernel,
        grid=(bsz, t // tm),
        in_specs=[tokd, _conv_spec(conv, tm, c), tokc, tokc, pl.BlockSpec((1, c), lambda b, i: (0, 0)),
                  pl.BlockSpec(w.shape, lambda b, i: (0, 0)), pl.BlockSpec((1, 1, d), lambda b, i: (b, 0, 0))],
        out_specs=tokd,
        out_shape=jax.ShapeDtypeStruct(x.shape, F32),
        compiler_params=_cparams(("parallel", "parallel")),
        name="hyena_out",
    )(x, conv, z, gate, bias.reshape(1, c), w, m5)


def _hyena_filter_spectrum(t, fw1, fb1, fw2, fb2, fw3, fb3, fw4, freq, width):
    pos = jnp.arange(t, dtype=F32)[:, None]
    tt = pos / max(t - 1, 1)
    ang = 2 * math.pi * pos / t
    nb = (HY_EMB - 1) // 2
    bands = jnp.linspace(1e-4, nb - 1, nb, dtype=F32)[None]
    feats = jnp.concatenate([tt, jnp.cos(bands * ang), -jnp.sin(bands * ang)], axis=-1)
    hdn = jnp.sin(freq * (feats @ fw1 + fb1))
    hdn = jnp.sin(freq * (hdn @ fw2 + fb2))
    hdn = jnp.sin(freq * (hdn @ fw3 + fb3))
    filt = (hdn @ fw4).reshape(t, HY_ORDER, 2, width)
    deltas = jnp.abs(jnp.linspace(HY_MIN_DECAY, HY_MAX_DECAY, width, dtype=F32))
    filt = filt * jnp.exp(-tt[:, :, None, None] * deltas)
    fwd, bwd = filt[:, :, 0], filt[:, :, 1]
    kern = jnp.concatenate([fwd, jnp.zeros_like(fwd[:1]), bwd[:0:-1]], axis=0)
    kern = kern * lax.rsqrt(jnp.sum(kern * kern, axis=0, keepdims=True) + 1e-6)
    spec = jnp.fft.fft(kern, axis=0)
    return jnp.stack([jnp.real(spec), jnp.imag(spec)], axis=1).transpose(2, 1, 0, 3).astype(F32)


def _block_diag_pair(m):
    z = jnp.zeros_like(m[0])
    return jnp.concatenate([jnp.concatenate([m[0], z], axis=1), jnp.concatenate([z, m[1]], axis=1)], axis=0)


def _even_mixer(lat, cx, ml, mc, gain, prm, ctx_out):
    (w_in, mu, w0, w_up, a0, a_up, g_up, k_k, k_a, r_k, gn_g, gn_b,
     lam_re, lam_im, log_dt, b_re, b_im, c_re, c_im, d_skip, w_glu, b_glu, w_out) = prm
    bsz, t_lat, d = lat.shape
    t_ctx = cx.shape[1]
    width = k_k.shape[0]
    heads = width // HEAD_DIM
    n_cols = mu.shape[0]
    s5w = d_skip.shape[0]
    ttot = t_ctx + t_lat
    nch = bsz * heads
    tm = t_ctx
    assert tm % GRID_W == 0 and t_lat % tm == 0 and 4 * nch == LANES and 2 * bsz == SUBLANES

    p_all, u_all = mod_matmul_stream(cx, lat, gain, mc, ml, w_in.astype(BF16), tm, n_cols)

    head_of = np.arange(width) // HEAD_DIM
    ones = jnp.asarray(head_of[:, None] == head_of[None, :], F32)
    feat_consts = (mu.reshape(1, -1), _block_diag_pair(w_up).astype(BF16), w0.reshape(1, -1),
                   _block_diag_pair(a_up).astype(BF16), a0.reshape(1, -1), g_up.astype(BF16),
                   k_a.reshape(1, -1), r_k.reshape(1, -1), ones)
    v2, g, bonus, r2, k2, w2, a2 = rwkv_features(p_all, n_cols, width, feat_consts, tm)

    def key_major(x):
        x = x.reshape(2, bsz, ttot, heads, 2, HALF_HEAD)
        return x.transpose(2, 5, 0, 4, 1, 3).reshape(ttot, HALF_HEAD, LANES)

    def value_major(x):
        x = x.reshape(2, bsz, ttot, heads, HEAD_DIM)
        return x.transpose(2, 4, 0, 1, 3).reshape(ttot, HEAD_DIM, 2 * nch)

    def key_const(x):
        x = x.reshape(heads, 2, HALF_HEAD).transpose(2, 1, 0)[:, None, :, None, :]
        return jnp.broadcast_to(x, (HALF_HEAD, 2, 2, bsz, heads)).reshape(HALF_HEAD, LANES)

    o_f, o_b = rwkv_scan(key_major(r2), key_major(k2), key_major(w2), key_major(a2), value_major(v2),
                         key_const(k_a), key_const(k_k), t_ctx)

    def token_major(o, lane0):
        o = o[..., lane0:lane0 + nch].reshape(ttot, HEAD_DIM, bsz, heads)
        return o.transpose(2, 0, 3, 1).reshape(bsz, ttot, width)

    o_f, o_b = token_major(o_f, 0), token_major(o_b, LANES // 2)

    ng = lam_re.shape[1]
    gps = ng // S5_SUPER
    lam = lax.complex(lam_re, lam_im)
    dt = jnp.exp(log_dt)[..., None]
    a_bar = jnp.exp(lam * dt)
    b_bar = ((a_bar - 1) / lam)[..., None] * lax.complex(b_re, b_im)
    eye = jnp.eye(gps, dtype=F32)

    def b_mat(x):
        x = x.reshape(2, S5_SUPER, gps, S5_STATE, S5_GROUP)
        m = jnp.einsum('dsgph,gk->sdghkp', x, eye)
        return m.reshape(S5_SUPER, 2 * gps * S5_GROUP, gps * S5_STATE)

    def c_mat(x):
        x = x.reshape(2, S5_SUPER, gps, S5_GROUP, S5_STATE)
        m = jnp.einsum('dsghp,gk->skpdgh', x, eye)
        return m.reshape(S5_SUPER, gps * S5_STATE, 2 * gps * S5_GROUP)

    bm = jnp.concatenate([b_mat(jnp.real(b_bar)), b_mat(jnp.imag(b_bar))], axis=2).astype(BF16)
    cm = jnp.concatenate([c_mat(c_re), -c_mat(c_im)], axis=1).astype(BF16)

    def a_rows(x):
        x = x.reshape(2, 1, S5_SUPER, gps * S5_STATE)
        return jnp.broadcast_to(x, (2, bsz, S5_SUPER, gps * S5_STATE)).reshape(2 * bsz, S5_SUPER, -1)

    a_arr = jnp.concatenate([a_rows(jnp.real(a_bar)), a_rows(jnp.imag(a_bar))], axis=2)
    a_arr = a_arr.reshape(1, 2 * bsz, -1)

    y_f, y_b = s5_scan(u_all.transpose(1, 0, 2), bm, cm, a_arr, t_ctx)
    ys = (y_f[:, :bsz] + y_b[:, bsz:]).transpose(1, 0, 2)

    wo = w_out.astype(BF16)
    out_consts = (ones, gn_g.reshape(1, -1), gn_b.reshape(1, -1), d_skip.reshape(1, -1),
                  w_glu.astype(BF16), b_glu.reshape(1, -1), wo[:width], wo[width:])
    lat = even_out(lat, 1, o_f, o_b, bonus, g, ys, u_all, ml[5], out_consts, tm)
    if ctx_out:
        cx = even_out(cx, 0, o_f, o_b, bonus, g, ys, u_all, mc[5], out_consts, tm)
    return lat, cx


def _hyena_mixer(x, gain, m, prm):
    (w_in, conv_w, conv_b, fw1, fb1, fw2, fb2, fw3, fb3, fw4, freq, bias_d, w_out) = prm
    bsz, t, d = x.shape
    c = w_out.shape[0]
    use_fft = t % (FFT_N2 * 2) == 0 and t >= 4 * FFT_N2
    if use_fft:
        kf = hyena_filter_spectrum_fft(t, fw1, fb1, fw2, fb2, fw3, fb3, fw4, freq, c)
    else:
        kf = _hyena_filter_spectrum(t, fw1, fb1, fw2, fb2, fw3, fb3, fw4, freq, c)
    z, zb, g1, g2 = hyena_in(x, gain, m[3], m[4], w_in.astype(BF16), conv_w, conv_b)
    gates = (g1, g2)
    for n in range(HY_ORDER):
        if use_fft:
            nd = t // FFT_N2
            zt = zb.reshape(bsz, nd, FFT_N2, c).transpose(0, 2, 1, 3)
            conv = fft_conv(zt, kf, n)
        else:
            conv = dft_conv(zb, kf[n])
        if n < HY_ORDER - 1:
            z, zb = hyena_gate(conv, z, gates[n], bias_d[n])
        else:
            return hyena_out(x, conv, z, gates[n], bias_d[n], w_out.astype(BF16), m[5])


def kernel(x, c, ctx, c_ctx, norm_g, ada_w, ada_b, ffn_wg, ffn_wu, ffn_wd, final_g, ev_w_in, ev_mu, ev_w0, ev_w_up, ev_a0, ev_a_up, ev_g_up, ev_k_k, ev_k_a, ev_r_k, ev_gn_g, ev_gn_b, ev_lam_re, ev_lam_im, ev_log_dt, ev_b_re, ev_b_im, ev_c_re, ev_c_im, ev_d, ev_w_glu, ev_b_glu, ev_w_out, od_w_in, od_conv_w, od_conv_b, od_fw1, od_fb1, od_fw2, od_fb2, od_fw3, od_fb3, od_fw4, od_freq, od_bias, od_w_out):
    depth = norm_g.shape[0]
    bsz, _, d = x.shape
    n_even = (depth + 1) // 2
    last_ctx = 2 * (n_even - 1)

    cond8 = jnp.concatenate([c, c_ctx[None], jnp.zeros((8 - bsz - 1, d), F32)], axis=0)
    mods = ada_mods_all(cond8, ada_w, ada_b)

    wg, wu, wd = ffn_wg.astype(BF16), ffn_wu.astype(BF16), ffn_wd.astype(BF16)
    lat, cx = x, ctx
    for l in range(depth):
        run_ctx = l <= last_ctx
        ctx_out = l < last_ctx
        i = l // 2
        ml = [mods[l, :bsz, None, k * d:(k + 1) * d] for k in range(N_MOD)]
        mc = [jnp.broadcast_to(mods[l, bsz:bsz + 1, None, k * d:(k + 1) * d], (bsz, 1, d))
              for k in range(N_MOD)]
        lat = ffn_half(lat, norm_g[l, 0], ml[0], ml[1], ml[2], wg[l, 0], wu[l, 0], wd[l, 0])
        if run_ctx:
            cx = ffn_half(cx, norm_g[l, 0], mc[0], mc[1], mc[2], wg[l, 0], wu[l, 0], wd[l, 0])
        if l % 2 == 0:
            prm = (ev_w_in[i], ev_mu[i], ev_w0[i], ev_w_up[i], ev_a0[i], ev_a_up[i], ev_g_up[i],
                   ev_k_k[i], ev_k_a[i], ev_r_k[i], ev_gn_g[i], ev_gn_b[i],
                   ev_lam_re[i], ev_lam_im[i], ev_log_dt[i], ev_b_re[i], ev_b_im[i], ev_c_re[i], ev_c_im[i],
                   ev_d[i], ev_w_glu[i], ev_b_glu[i], ev_w_out[i])
            lat, cx = _even_mixer(lat, cx, ml, mc, norm_g[l, 1], prm, ctx_out)
        else:
            prm = (od_w_in[i], od_conv_w[i], od_conv_b[i], od_fw1[i], od_fb1[i], od_fw2[i], od_fb2[i],
                   od_fw3[i], od_fb3[i], od_fw4[i], od_freq[i], od_bias[i], od_w_out[i])
            lat = _hyena_mixer(lat, norm_g[l, 1], ml, prm)
            if ctx_out:
                cx = _hyena_mixer(cx, norm_g[l, 1], mc, prm)
        fin = final_g if l == depth - 1 else None
        lat = ffn_half(lat, norm_g[l, 2], ml[6], ml[7], ml[8], wg[l, 1], wu[l, 1], wd[l, 1], fin)
        if ctx_out:
            cx = ffn_half(cx, norm_g[l, 2], mc[6], mc[7], mc[8], wg[l, 1], wu[l, 1], wd[l, 1])
    return lat
```

```python
import functools
import math

import numpy as np
import jax
import jax.numpy as jnp
from jax import lax
from jax.experimental import pallas as pl
from jax.experimental.pallas import tpu as pltpu

F32 = jnp.float32
BF16 = jnp.bfloat16
HIGHEST = lax.Precision.HIGHEST

N_MOD = 9
NORM_EPS = 1e-6
GN_EPS = 64e-5
GRID_W = 64
HEAD_DIM = 64
HALF_HEAD = HEAD_DIM // 2
S5_GROUP = 16
S5_STATE = 64
S5_SUPER = 4
HY_ORDER = 2
HY_EMB = 33
HY_MIN_DECAY = math.log(1e-2) / 1.5
HY_MAX_DECAY = math.log(1e-2) / 0.3
LANES = 128
SUBLANES = 8
MXU_DIM = 256
FFT_N2 = 128
VMEM_LIMIT = 56 * 1024 * 1024


def _cparams(sem, vmem=VMEM_LIMIT):
    return pltpu.CompilerParams(dimension_semantics=sem, vmem_limit_bytes=vmem)


def _dot(a, b):
    return jnp.dot(a, b, preferred_element_type=F32)


def _dot_exact(a, b):
    return jnp.dot(a, b, preferred_element_type=F32, precision=HIGHEST)


def _group_sum(a, ones):
    hi = a.astype(BF16)
    lo = (a - hi.astype(F32)).astype(BF16)
    return _dot(hi, ones) + _dot(lo, ones)


def _rms_mod(x, gain, shift, scale):
    ms = jnp.mean(x * x, axis=-1, keepdims=True)
    return x * lax.rsqrt(ms + NORM_EPS) * gain * (1.0 + scale) + shift


def _sigmoid(x):
    return 1.0 / (1.0 + jnp.exp(-x))


def _silu(x):
    return x * _sigmoid(x)


def _ada_kernel(c_ref, w_ref, b_ref, o_ref):
    s = _silu(c_ref[...])
    o_ref[0] = _dot(s.astype(BF16), w_ref[0].astype(BF16)) + b_ref[0]


def ada_mods_all(cond8, ada_w, ada_b):
    depth, d, n = ada_w.shape
    tn = n // 8
    return pl.pallas_call(
        _ada_kernel,
        grid=(depth, n // tn),
        in_specs=[pl.BlockSpec((8, d), lambda l, j: (0, 0)),
                  pl.BlockSpec((1, d, tn), lambda l, j: (l, 0, j)),
                  pl.BlockSpec((1, 1, tn), lambda l, j: (l, 0, j))],
        out_specs=pl.BlockSpec((1, 8, tn), lambda l, j: (l, 0, j)),
        out_shape=jax.ShapeDtypeStruct((depth, 8, n), F32),
        compiler_params=_cparams(("parallel", "parallel")),
        name="ada_mods",
    )(cond8, ada_w, ada_b.reshape(depth, 1, n))


def _ffn_kernel(x_ref, gain_ref, shift_ref, scale_ref, gate_ref, wg_ref, wu_ref, wd_ref, fg_ref,
                o_ref, *, final_norm, chunks):
    x = x_ref[0]
    h = _rms_mod(x, gain_ref[...], shift_ref[0], scale_ref[0]).astype(BF16)
    acc = None
    for lo, hi in chunks:
        g = _dot(h, wg_ref[:, lo:hi])
        u = _dot(h, wu_ref[:, lo:hi])
        part = _dot((_silu(g) * u).astype(BF16), wd_ref[lo:hi, :])
        acc = part if acc is None else acc + part
    y = x + 0.5 * gate_ref[0] * acc
    if final_norm:
        ms = jnp.mean(y * y, axis=-1, keepdims=True)
        y = y * lax.rsqrt(ms + NORM_EPS) * fg_ref[...]
    o_ref[0] = y


def ffn_half(x, gain, shift, scale, gate, wg, wu, wd, final_g=None):
    bsz, t, d = x.shape
    ff = wg.shape[1]
    tm = min(1024, t)
    step = 2 * MXU_DIM
    chunks = tuple((lo, min(lo + step, ff)) for lo in range(0, ff, step))
    fg = jnp.ones((1, d), F32) if final_g is None else final_g.reshape(1, d)
    vec = pl.BlockSpec((1, 1, d), lambda b, i: (b, 0, 0))
    resident = lambda a: pl.BlockSpec(a.shape, lambda b, i: (0, 0), pipeline_mode=pl.Buffered(1))
    return pl.pallas_call(
        functools.partial(_ffn_kernel, final_norm=final_g is not None, chunks=chunks),
        grid=(bsz, t // tm),
        in_specs=[pl.BlockSpec((1, tm, d), lambda b, i: (b, i, 0)),
                  pl.BlockSpec((1, d), lambda b, i: (0, 0)),
                  vec, vec, vec, resident(wg), resident(wu), resident(wd),
                  pl.BlockSpec((1, d), lambda b, i: (0, 0))],
        out_specs=pl.BlockSpec((1, tm, d), lambda b, i: (b, i, 0)),
        out_shape=jax.ShapeDtypeStruct((bsz, t, d), F32),
        compiler_params=_cparams(("parallel", "parallel")),
        name="ffn_half",
    )(x, gain.reshape(1, d), shift, scale, gate, wg, wu, wd, fg)


def _modmm_kernel(cx_ref, lat_ref, gain_ref, shc_ref, scc_ref, shl_ref, scl_ref, w_ref, o_ref, u_ref, *, nc):
    is_ctx = pl.program_id(1) < nc
    x = jnp.where(is_ctx, cx_ref[0], lat_ref[0])
    shift = jnp.where(is_ctx, shc_ref[0], shl_ref[0])
    scale = jnp.where(is_ctx, scc_ref[0], scl_ref[0])
    h = _rms_mod(x, gain_ref[...], shift, scale).astype(BF16)
    n1 = o_ref.shape[2]
    o_ref[0] = _dot(h, w_ref[:, :n1])
    u_ref[0] = _dot(h, w_ref[:, n1:])


def mod_matmul_stream(cx, lat, gain, mc, ml, w, tm, n_first):
    bsz, t_ctx, d = cx.shape
    t_lat = lat.shape[1]
    n = w.shape[1]
    nc = t_ctx // tm
    nt = nc + t_lat // tm
    vec = pl.BlockSpec((1, 1, d), lambda b, i: (b, 0, 0))
    return pl.pallas_call(
        functools.partial(_modmm_kernel, nc=nc),
        grid=(bsz, nt),
        in_specs=[pl.BlockSpec((1, tm, d), lambda b, i: (b, jnp.minimum(i, nc - 1), 0)),
                  pl.BlockSpec((1, tm, d), lambda b, i: (b, jnp.maximum(i - nc, 0), 0)),
                  pl.BlockSpec((1, d), lambda b, i: (0, 0)),
                  vec, vec, vec, vec,
                  pl.BlockSpec((d, n), lambda b, i: (0, 0))],
        out_specs=[pl.BlockSpec((1, tm, n_first), lambda b, i: (b, i, 0)),
                   pl.BlockSpec((1, tm, n - n_first), lambda b, i: (b, i, 0))],
        out_shape=[jax.ShapeDtypeStruct((bsz, t_ctx + t_lat, n_first), F32),
                   jax.ShapeDtypeStruct((bsz, t_ctx + t_lat, n - n_first), F32)],
        compiler_params=_cparams(("parallel", "parallel")),
        name="mod_matmul",
    )(cx, lat, gain.reshape(1, d), mc[3], mc[4], ml[3], ml[4], w)


def _rwkv_feat_kernel(p_ref, pu_ref, pd_ref, mu_ref, wup_ref, w0_ref, aup_ref, a0_ref, gup_ref,
                      ka_ref, rk_ref, ones_ref,
                      v_o, g_o, bonus_o, r_o, k_o, w_o, a_o, *, nt, width):
    i = pl.program_id(1)
    is_ctx = i == 0
    p = p_ref[0]
    tm = p.shape[0]
    row = lax.broadcasted_iota(jnp.int32, p.shape, 0)
    lane = lax.broadcasted_iota(jnp.int32, p.shape, 1) % 4
    prev = pltpu.roll(p, 1, 0)
    nxt = pltpu.roll(p, tm - 1, 0)
    col = jnp.where(is_ctx, row, row % GRID_W)
    last = jnp.where(is_ctx, tm - 1, GRID_W - 1)
    left = jnp.where(col == 0, 0.0, prev)
    right = jnp.where(col == last, 0.0, nxt)
    up_halo = jnp.where(i > 1, pu_ref[0], 0.0)
    dn_halo = jnp.where(i < nt - 1, pd_ref[0], 0.0)
    up = jnp.where(is_ctx, left, jnp.concatenate([up_halo, p[:tm - GRID_W]], axis=0))
    down = jnp.where(is_ctx, right, jnp.concatenate([p[GRID_W:], dn_halo], axis=0))
    shifted = jnp.where(lane == 0, left, jnp.where(lane == 1, right, jnp.where(lane == 2, up, down)))
    q = p + mu_ref[...] * (shifted - p)

    w = width
    r, k, v = q[:, :w], q[:, w:2 * w], q[:, 2 * w:3 * w]
    wd = q[:, 3 * w:3 * w + LANES]
    ad = q[:, 3 * w + LANES:3 * w + 2 * LANES]
    gd = q[:, 3 * w + 2 * LANES:3 * w + 3 * LANES]

    zlin = w0_ref[...] + _dot(jnp.tanh(wd).astype(BF16), wup_ref[...])
    decay = jnp.exp(-math.exp(-0.5) * _sigmoid(zlin))
    a = _sigmoid(a0_ref[...] + _dot(ad.astype(BF16), aup_ref[...]))
    g_o[0] = _dot(_sigmoid(gd).astype(BF16), gup_ref[...])

    ones = ones_ref[...]
    for d in range(2):
        v_o[d, 0] = v
        r_o[d, 0] = r
        k_o[d, 0] = k
    ksum = jnp.zeros_like(k)
    for d in range(2):
        a_d = a[:, d * w:(d + 1) * w]
        ksum = ksum + k * (1.0 + (a_d - 1.0) * ka_ref[...])
        w_o[d, 0] = decay[:, d * w:(d + 1) * w]
        a_o[d, 0] = a_d
    bonus_o[0] = _group_sum(r * ksum * rk_ref[...], ones) * v


def rwkv_features(p, n_cols, width, consts, tm):
    bsz, t, _ = p.shape
    nt = t // tm
    hb = tm // GRID_W
    nhb = t // GRID_W
    w = width
    full = lambda a: pl.BlockSpec(a.shape, lambda b, i: (0,) * a.ndim)
    tok = pl.BlockSpec((1, tm, w), lambda b, i: (b, i, 0))
    tok2 = pl.BlockSpec((2, 1, tm, w), lambda b, i: (0, b, i, 0))
    sds = jax.ShapeDtypeStruct((bsz, t, w), F32)
    sds2 = jax.ShapeDtypeStruct((2, bsz, t, w), F32)
    return pl.pallas_call(
        functools.partial(_rwkv_feat_kernel, nt=nt, width=w),
        grid=(bsz, nt),
        in_specs=[pl.BlockSpec((1, tm, n_cols), lambda b, i: (b, i, 0)),
                  pl.BlockSpec((1, GRID_W, n_cols), lambda b, i: (b, jnp.maximum(i * hb - 1, 0), 0)),
                  pl.BlockSpec((1, GRID_W, n_cols), lambda b, i: (b, jnp.minimum((i + 1) * hb, nhb - 1), 0))]
                 + [full(a) for a in consts],
        out_specs=[tok2, tok, tok, tok2, tok2, tok2, tok2],
        out_shape=[sds2, sds, sds, sds2, sds2, sds2, sds2],
        compiler_params=_cparams(("parallel", "parallel")),
        name="rwkv_features",
    )(p, p, p, *consts)


def _rwkv_scan_kernel(rf, rb, kf, kb, wf, wb, af, ab, vf, vb, ka_ref, kks_ref, of_ref, ob_ref,
                      s_ref, vec_ref, v_ref, c_ref, look_ref, *, tc):
    @pl.when(pl.program_id(0) == 0)
    def _():
        s_ref[...] = jnp.zeros_like(s_ref)
        look_ref[:, tc] = jnp.zeros((look_ref.shape[0],) + look_ref.shape[2:], F32)

    nj, ni = s_ref.shape[0], s_ref.shape[1]
    quarter = LANES // 4

    def fold(x):
        hi = (lax.broadcasted_iota(jnp.int32, x.shape, 1) % (2 * quarter)) >= quarter
        return x + jnp.where(hi, pltpu.roll(x, quarter, 1), pltpu.roll(x, 3 * quarter, 1))

    def hsum(x):
        return jnp.sum(x, axis=0, keepdims=True)

    def jsum(x):
        return fold(jnp.broadcast_to(hsum(x), (SUBLANES, LANES)))[0:1]

    ka = ka_ref[...]
    kks = kks_ref[...]
    fwd_lane = lax.broadcasted_iota(jnp.int32, (nj, LANES), 1) < LANES // 2
    def merged(f, b, t):
        return jnp.where(fwd_lane, f[t], b[tc - 1 - t])

    cum = jnp.ones((nj, LANES), F32)
    for t in range(tc):
        w = merged(wf, wb, t)
        cum = cum * w
        kk = merged(kf, kb, t) * kks
        look_ref[0, t] = kk / jnp.maximum(jnp.sqrt(jsum(kk * kk)), 1e-12)
        look_ref[1, t] = w * merged(rf, rb, t)
        look_ref[2, t] = cum
    for t in range(tc):
        r, k, a = merged(rf, rb, t), merged(kf, kb, t), merged(af, ab, t)
        kk, kk_next, wr_next = look_ref[0, t], look_ref[0, t + 1], look_ref[1, t + 1]
        cum = look_ref[2, t]
        inv = 1.0 / cum
        kka = kk * a
        kd = k * (1.0 + (a - 1.0) * ka)
        vec_ref[0, t] = kka * inv
        vec_ref[1, t] = kd * inv
        vec_ref[2, t] = cum * kk_next
        vec_ref[3, t] = cum * wr_next
        c_ref[t, 0:1, :] = jsum(kka * r)
        c_ref[t, 1:2, :] = jsum(kd * r)
        c_ref[t, 2:3, :] = hsum(kka * kk_next)
        c_ref[t, 3:4, :] = hsum(kd * kk_next)
        c_ref[t, 4:5, :] = hsum(kka * wr_next)
        c_ref[t, 5:6, :] = hsum(kd * wr_next)
        v_ref[t] = jnp.concatenate([vf[t], vb[tc - 1 - t]], axis=-1)

    def state_sums(y1, y2):
        s1 = jnp.zeros((ni, LANES), F32)
        s2 = jnp.zeros((ni, LANES), F32)
        for j in range(nj):
            sj = s_ref[j]
            s1 = s1 + sj * y1(j)
            s2 = s2 + sj * y2(j)
        return s1, s2

    def step(t, carry):
        row = lambda n: (lambda j: vec_ref[n, t, j:j + 1, :])
        a1, a2 = state_sums(row(2), row(3))
        sa, o1 = fold(carry[0]), fold(carry[1])
        v = v_ref[t]
        out = o1 - sa * c_ref[t, 0:1, :] + v * c_ref[t, 1:2, :]
        of_ref[t] = out
        ob_ref[tc - 1 - t] = out
        nxt = (a1 - sa * c_ref[t, 2:3, :] + v * c_ref[t, 3:4, :],
               a2 - sa * c_ref[t, 4:5, :] + v * c_ref[t, 5:6, :])
        for j in range(nj):
            s_ref[j] = s_ref[j] - sa * row(0)(j) + v * row(1)(j)
        return nxt

    first = state_sums(lambda j: look_ref[0, 0, j:j + 1, :], lambda j: look_ref[1, 0, j:j + 1, :])
    lax.fori_loop(0, tc, step, first)
    for j in range(nj):
        s_ref[j] = s_ref[j] * look_ref[2, tc - 1, j:j + 1, :]


def rwkv_scan(r, k, w, a, v, ka, kks, t_ctx):
    ttot, nj, _ = r.shape
    ni, nc = v.shape[1], v.shape[2]
    tc = 32
    ncb, ntb = t_ctx // tc, ttot // tc
    fwd = lambda g: (g, 0, 0)
    bwd = lambda g: (jnp.where(g < ncb, ncb - 1 - g, ntb + ncb - 1 - g), 0, 0)
    jf, jb = pl.BlockSpec((tc, nj, LANES), fwd), pl.BlockSpec((tc, nj, LANES), bwd)
    vf, vb = pl.BlockSpec((tc, ni, nc), fwd), pl.BlockSpec((tc, ni, nc), bwd)
    of, ob = pl.BlockSpec((tc, ni, LANES), fwd), pl.BlockSpec((tc, ni, LANES), bwd)
    osd = jax.ShapeDtypeStruct((ttot, ni, LANES), F32)
    return pl.pallas_call(
        functools.partial(_rwkv_scan_kernel, tc=tc),
        grid=(ntb,),
        in_specs=[jf, jb] * 4 + [vf, vb, pl.BlockSpec(ka.shape, lambda g: (0, 0)),
                                 pl.BlockSpec(kks.shape, lambda g: (0, 0))],
        out_specs=[of, ob],
        out_shape=[osd, osd],
        scratch_shapes=[pltpu.VMEM((nj, ni, LANES), F32), pltpu.VMEM((4, tc, nj, LANES), F32),
                        pltpu.VMEM((tc, ni, LANES), F32), pltpu.VMEM((tc, SUBLANES, LANES), F32),
                        pltpu.VMEM((3, tc + 1, nj, LANES), F32)],
        compiler_params=_cparams(("arbitrary",)),
        name="rwkv_scan",
    )(r, r, k, k, w, w, a, a, v, v, ka, kks)


def _s5_kernel(uf_ref, ub_ref, bm_ref, cm_ref, a_ref, yf_ref, yb_ref, h_ref, bu_ref, ubr_ref, *, tt):
    @pl.when(pl.program_id(0) == 0)
    def _():
        h_ref[...] = jnp.zeros_like(h_ref)

    nsb = bm_ref.shape[0]
    kin = bm_ref.shape[1] // 2
    sw = bm_ref.shape[2]
    hw = sw // 2
    for t in range(tt):
        ubr_ref[t] = ub_ref[tt - 1 - t]
    nothing = jnp.zeros(uf_ref.shape, F32)
    uf = jnp.concatenate([uf_ref[...], nothing], axis=1)
    ub = jnp.concatenate([nothing, ubr_ref[...]], axis=1)
    for sb in range(nsb):
        lhs = jnp.concatenate([uf[:, :, sb * kin:(sb + 1) * kin], ub[:, :, sb * kin:(sb + 1) * kin]], axis=-1)
        lhs = lhs.reshape(tt * SUBLANES, 2 * kin).astype(BF16)
        bu_ref[:, sb * sw:(sb + 1) * sw] = _dot(lhs, bm_ref[sb])

    def step(t, hs):
        rows = pl.ds(pl.multiple_of(t * SUBLANES, SUBLANES), SUBLANES)
        out = []
        for sb in range(nsb):
            hr, hi = hs[2 * sb], hs[2 * sb + 1]
            lo = sb * sw
            ar = a_ref[0, :, lo:lo + hw]
            ai = a_ref[0, :, lo + hw:lo + sw]
            nr = ar * hr - ai * hi + bu_ref[rows, lo:lo + hw]
            ni = ar * hi + ai * hr + bu_ref[rows, lo + hw:lo + sw]
            bu_ref[rows, lo:lo + hw] = nr
            bu_ref[rows, lo + hw:lo + sw] = ni
            out += [nr, ni]
        return tuple(out)

    h0 = []
    for sb in range(nsb):
        h0 += [h_ref[:, sb * sw:sb * sw + hw], h_ref[:, sb * sw + hw:(sb + 1) * sw]]
    hs = lax.fori_loop(0, tt, step, tuple(h0))
    for sb in range(nsb):
        h_ref[:, sb * sw:sb * sw + hw] = hs[2 * sb]
        h_ref[:, sb * sw + hw:(sb + 1) * sw] = hs[2 * sb + 1]

    nout = cm_ref.shape[2] // 2
    fwd_row = (lax.broadcasted_iota(jnp.int32, (tt * SUBLANES, nout), 0) % SUBLANES) < SUBLANES // 2
    ys = []
    for sb in range(nsb):
        yy = _dot(bu_ref[:, sb * sw:(sb + 1) * sw].astype(BF16), cm_ref[sb])
        ys.append(jnp.where(fwd_row, yy[:, :nout], yy[:, nout:]))
    y = jnp.concatenate(ys, axis=1).reshape(tt, SUBLANES, nsb * nout)
    yf_ref[...] = y
    for t in range(tt):
        yb_ref[tt - 1 - t] = y[t]


def s5_scan(u, bm, cm, a, t_ctx):
    ttot, bsz, width = u.shape
    rows = 2 * bsz
    nsb, _, sw = bm.shape
    tt = 64
    ncb, ntb = t_ctx // tt, ttot // tt
    fmap = lambda g: (g, 0, 0)
    bmap = lambda g: (jnp.where(g < ncb, ncb - 1 - g, ntb + ncb - 1 - g), 0, 0)
    full = lambda x: pl.BlockSpec(x.shape, lambda g: (0,) * x.ndim)
    osd = jax.ShapeDtypeStruct((ttot, rows, width), F32)
    return pl.pallas_call(
        functools.partial(_s5_kernel, tt=tt),
        grid=(ntb,),
        in_specs=[pl.BlockSpec((tt, bsz, width), fmap), pl.BlockSpec((tt, bsz, width), bmap),
                  full(bm), full(cm), full(a)],
        out_specs=[pl.BlockSpec((tt, rows, width), fmap), pl.BlockSpec((tt, rows, width), bmap)],
        out_shape=[osd, osd],
        scratch_shapes=[pltpu.VMEM((rows, nsb * sw), F32), pltpu.VMEM((tt * rows, nsb * sw), F32),
                        pltpu.VMEM((tt, bsz, width), F32)],
        compiler_params=_cparams(("arbitrary",)),
        name="s5_scan",
    )(u, u, bm, cm, a)


def _even_out_kernel(x_ref, of_ref, ob_ref, bonus_ref, g_ref, ys_ref, u_ref, m5_ref,
                     ones_ref, gng_ref, gnb_ref, dskip_ref, wglu_ref, bglu_ref, wo1_ref, wo2_ref, o_ref):
    ones = ones_ref[...]
    inv = 1.0 / HEAD_DIM
    o = of_ref[0] + ob_ref[0]
    mean = _group_sum(o, ones) * inv
    oc = o - mean
    var = _group_sum(oc * oc, ones) * inv
    y1 = (oc * lax.rsqrt(var + GN_EPS) * gng_ref[...] + gnb_ref[...] + bonus_ref[0]) * g_ref[0]
    y = ys_ref[0] + dskip_ref[...] * u_ref[0]
    y = jax.nn.gelu(y)
    y2 = y * _sigmoid(_dot(y.astype(BF16), wglu_ref[...]) + bglu_ref[...])
    out = _dot(y1.astype(BF16), wo1_ref[...]) + _dot(y2.astype(BF16), wo2_ref[...])
    o_ref[0] = x_ref[0] + m5_ref[0] * out


def even_out(x, off, o_f, o_b, bonus, g, ys, u, m5, consts, tm):
    bsz, t, d = x.shape
    w = o_f.shape[2]
    tokd = pl.BlockSpec((1, tm, d), lambda b, i: (b, i, 0))
    tokw = pl.BlockSpec((1, tm, w), lambda b, i: (b, i + off, 0))
    toks = pl.BlockSpec((1, tm, u.shape[2]), lambda b, i: (b, i + off, 0))
    full = lambda a: pl.BlockSpec(a.shape, lambda b, i: (0,) * a.ndim)
    return pl.pallas_call(
        _even_out_kernel,
        grid=(bsz, t // tm),
        in_specs=[tokd, tokw, tokw, tokw, tokw, toks, toks,
                  pl.BlockSpec((1, 1, d), lambda b, i: (b, 0, 0))] + [full(a) for a in consts],
        out_specs=tokd,
        out_shape=jax.ShapeDtypeStruct((bsz, t, d), F32),
        compiler_params=_cparams(("parallel", "parallel")),
        name="even_out",
    )(x, o_f, o_b, bonus, g, ys, u, m5, *consts)


def _hy_in_kernel(x_ref, xp_ref, xn_ref, gain_ref, shift_ref, scale_ref, w_ref, cw_ref, cb_ref,
                  z_ref, zb_ref, g1_ref, g2_ref, *, nt, c):
    i = pl.program_id(1)
    gain, shift, scale = gain_ref[...], shift_ref[0], scale_ref[0]
    tm = x_ref.shape[1]
    halo = xp_ref.shape[1]
    h = jnp.concatenate([_rms_mod(xr[0], gain, shift, scale) for xr in (xp_ref, x_ref, xn_ref)], axis=0)
    h = h.astype(BF16)
    row = lax.broadcasted_iota(jnp.int32, (tm, c), 0)
    outs = (z_ref, g1_ref, g2_ref)
    for part in range(3):
        wp = w_ref[:, part * c:(part + 1) * c]
        p_ext = _dot(h, wp)
        p = p_ext[halo:halo + tm]
        pp = jnp.where(i > 0, p_ext[halo - 1:halo], 0.0)
        pn = jnp.where(i < nt - 1, p_ext[halo + tm:halo + tm + 1], 0.0)
        pm1 = jnp.where(row == 0, pp, pltpu.roll(p, 1, 0))
        pp1 = jnp.where(row == tm - 1, pn, pltpu.roll(p, tm - 1, 0))
        cw = cw_ref[:, part * c:(part + 1) * c]
        q = cw[0:1] * pm1 + cw[1:2] * p + cw[2:3] * pp1 + cb_ref[:, part * c:(part + 1) * c]
        outs[part][0] = q
        if part == 0:
            zb_ref[0] = q.astype(BF16)


def hyena_in(x, gain, shift, scale, w, conv_w, conv_b):
    bsz, t, d = x.shape
    c = w.shape[1] // 3
    tm = min(512, t)
    nt = t // tm
    hb = tm // 8
    nhb = t // 8
    vec = pl.BlockSpec((1, 1, d), lambda b, i: (b, 0, 0))
    tok = pl.BlockSpec((1, tm, c), lambda b, i: (b, i, 0))
    sds = jax.ShapeDtypeStruct((bsz, t, c), F32)
    return pl.pallas_call(
        functools.partial(_hy_in_kernel, nt=nt, c=c),
        grid=(bsz, nt),
        in_specs=[pl.BlockSpec((1, tm, d), lambda b, i: (b, i, 0)),
                  pl.BlockSpec((1, 8, d), lambda b, i: (b, jnp.maximum(i * hb - 1, 0), 0)),
                  pl.BlockSpec((1, 8, d), lambda b, i: (b, jnp.minimum((i + 1) * hb, nhb - 1), 0)),
                  pl.BlockSpec((1, d), lambda b, i: (0, 0)),
                  vec, vec,
                  pl.BlockSpec(w.shape, lambda b, i: (0, 0)),
                  pl.BlockSpec(conv_w.shape, lambda b, i: (0, 0)),
                  pl.BlockSpec((1, 3 * c), lambda b, i: (0, 0))],
        out_specs=[tok, tok, tok, tok],
        out_shape=[sds, jax.ShapeDtypeStruct((bsz, t, c), BF16), sds, sds],
        compiler_params=_cparams(("parallel", "parallel")),
        name="hyena_in",
    )(x, x, x, gain.reshape(1, d), shift, scale, w, conv_w, conv_b.reshape(1, 3 * c))


def _pack_complex(re, im):
    bits = lambda x: lax.bitcast_convert_type(x.astype(BF16).astype(F32), jnp.uint32)
    return lax.shift_right_logical(bits(re), jnp.uint32(16)) | bits(im)


def _unpack_complex(w):
    re = lax.bitcast_convert_type(lax.shift_left(w, jnp.uint32(16)), F32)
    im = lax.bitcast_convert_type(w & jnp.uint32(0xFFFF0000), F32)
    return jnp.concatenate([re, im], axis=0).astype(BF16)


def _fft_conv_kernel(z_ref, kf_ref, g_ref, w1_ref, g2i_ref, f1c_ref, twt_ref, o_ref, a_ref, t_ref,
                     *, n1, k1c):
    s = pl.program_id(2)
    n2 = FFT_N2
    nd = n1 // 2
    slab = n1 // k1c

    @pl.when(s == 0)
    def _():
        w1 = w1_ref[...]

        def body(j, carry):
            x = jnp.concatenate([z_ref[0, j], z_ref[1, j]], axis=0)
            a = _dot(w1, x)
            a_ref[pl.ds(pl.multiple_of(j * n1, n1), n1), :] = _pack_complex(a[:n1], a[n1:])
            return carry

        lax.fori_loop(0, n2, body, 0, unroll=16)

    @pl.when((s > 0) & (s <= k1c))
    def _():
        g2i = g2i_ref[...]
        base = (s - 1) * slab

        def body(kk, carry):
            k1 = base + kk
            ak = _unpack_complex(a_ref[pl.ds(k1, n2, stride=n1), :])
            x = _dot(g_ref[kk], ak)
            xr, xi = x[:n2], x[n2:]
            kr, ki = kf_ref[0, kk], kf_ref[1, kk]
            pr = xr * kr - xi * ki
            pi = xr * ki + xi * kr
            tt = _dot(g2i, jnp.concatenate([pr, pi], axis=0).astype(BF16))
            t_ref[pl.ds(pl.multiple_of(k1 * n2, n2), n2), :] = _pack_complex(tt[:n2], tt[n2:])
            return carry

        lax.fori_loop(0, slab, body, 0, unroll=16)

    @pl.when(s == k1c + 1)
    def _():
        f1r, f1i = f1c_ref[0], f1c_ref[1]

        def body(j, carry):
            twr = twt_ref[0, pl.ds(j, 1), :]
            twi = twt_ref[1, pl.ds(j, 1), :]
            wr = f1r * twr + f1i * twi
            wi = f1i * twr - f1r * twi
            w3 = jnp.concatenate([jnp.concatenate([wr, -wi], axis=1),
                                  jnp.concatenate([wi, wr], axis=1)], axis=0).astype(BF16)
            tn = _unpack_complex(t_ref[pl.ds(j, n1, stride=n2), :])
            y = _dot(w3, tn)
            o_ref[0, j] = y[:nd]
            o_ref[1, j] = y[nd:]
            return carry

        lax.fori_loop(0, n2, body, 0, unroll=16)


def _fft_consts(n1):
    n2 = FFT_N2
    n = n1 * n2
    nd = n1 // 2
    k1 = np.arange(n1)
    f1 = np.exp(-2j * np.pi * np.outer(k1, np.arange(nd)) / n1)
    w1 = np.block([[f1.real, -f1.imag], [f1.imag, f1.real]])
    f2 = np.exp(-2j * np.pi * np.outer(np.arange(n2), np.arange(n2)) / n2)
    f2c = np.conj(f2)
    g2i = np.block([[f2c.real, -f2c.imag], [f2c.imag, f2c.real]])
    f1c = np.exp(2j * np.pi * np.outer(np.arange(nd), k1) / n1) / n
    tw = np.exp(-2j * np.pi * np.outer(k1, np.arange(n2)) / n)
    cplx = lambda m: jnp.asarray(np.stack([m.real, m.imag]), F32)
    (f2r, f2i), (twr, twi) = cplx(f2), cplx(tw)
    gr = f2r[None] * twr[:, None, :] - f2i[None] * twi[:, None, :]
    gi = f2r[None] * twi[:, None, :] + f2i[None] * twr[:, None, :]
    gtw = jnp.concatenate([jnp.concatenate([gr, -gi], axis=2), jnp.concatenate([gi, gr], axis=2)], axis=1)
    return (gtw.astype(BF16), jnp.asarray(w1, BF16), jnp.asarray(g2i, BF16), cplx(f1c), cplx(tw.T))


def fft_conv(zt, kf, order):
    bsz, n2, nd, c = zt.shape
    n1 = 2 * nd
    cb = LANES
    k1c = 8 if n1 % 8 == 0 and n1 >= 64 else 2
    slab = n1 // k1c
    gtw, *consts = _fft_consts(n1)
    full = lambda a: pl.BlockSpec(a.shape, lambda j, p, s: (0,) * a.ndim)
    blk = pl.BlockSpec((2, n2, nd, cb), lambda j, p, s: (p, 0, 0, j))
    chunk = lambda j, p, s: jnp.clip(s - 1, 0, k1c - 1)
    return pl.pallas_call(
        functools.partial(_fft_conv_kernel, n1=n1, k1c=k1c),
        grid=(c // cb, bsz // 2, k1c + 2),
        in_specs=[blk,
                  pl.BlockSpec((None, 2, slab, n2, cb), lambda j, p, s: (order, 0, chunk(j, p, s), 0, j)),
                  pl.BlockSpec((slab, 2 * n2, 2 * n2), lambda j, p, s: (chunk(j, p, s), 0, 0))]
                 + [full(a) for a in consts],
        out_specs=blk,
        out_shape=jax.ShapeDtypeStruct(zt.shape, F32),
        scratch_shapes=[pltpu.VMEM((n2 * n1, cb), jnp.uint32), pltpu.VMEM((n1 * n2, cb), jnp.uint32)],
        compiler_params=_cparams(("parallel", "parallel", "arbitrary")),
        name="fft_conv",
    )(zt, kf, gtw, *consts)


def _fft_fwd_kernel(z_ref, scale_ref, g_ref, w1_ref, o_ref, a_ref, *, n1, k1c):
    s = pl.program_id(2)
    n2 = FFT_N2
    slab = n1 // k1c
    cb = o_ref.shape[4]

    @pl.when(s == 0)
    def _():
        w1 = w1_ref[...]

        def body(j, carry):
            a = _dot(w1, jnp.concatenate([z_ref[0, j], z_ref[1, j]], axis=1))
            packed = _pack_complex(a[:n1], a[n1:])
            rows = pl.ds(pl.multiple_of(j * n1, n1), n1)
            a_ref[0, rows, :] = packed[:, :cb]
            a_ref[1, rows, :] = packed[:, cb:]
            return carry

        lax.fori_loop(0, n2, body, 0, unroll=16)

    @pl.when(s > 0)
    def _():
        base = (s - 1) * slab

        def body(kk, carry):
            k1 = base + kk
            rows = pl.ds(k1, n2, stride=n1)
            ak = jnp.concatenate([_unpack_complex(a_ref[0, rows, :]), _unpack_complex(a_ref[1, rows, :])], axis=1)
            x = _dot(g_ref[kk], ak)
            pos, neg = x[:, :cb], x[:, cb:]
            scale = scale_ref[0]
            o_ref[0, 0, kk] = (pos[:n2] + neg[:n2]) * scale
            o_ref[0, 1, kk] = (pos[n2:] - neg[n2:]) * scale
            return carry

        lax.fori_loop(0, slab, body, 0, unroll=16)


def filter_spectrum(taps, scale):
    orders, _, n2, nd, c = taps.shape
    n1 = 2 * nd
    cb = LANES
    k1c = 8 if n1 % 8 == 0 and n1 >= 64 else 2
    slab = n1 // k1c
    gtw, w1 = _fft_consts(n1)[:2]
    w1 = w1[:, :nd]
    full = lambda a: pl.BlockSpec(a.shape, lambda j, r, s: (0,) * a.ndim)
    return pl.pallas_call(
        functools.partial(_fft_fwd_kernel, n1=n1, k1c=k1c),
        grid=(c // cb, orders, k1c + 1),
        in_specs=[pl.BlockSpec((None, 2, n2, nd, cb), lambda j, r, s: (r, 0, 0, 0, j)),
                  pl.BlockSpec((1, 1, cb), lambda j, r, s: (r, 0, j)),
                  pl.BlockSpec((slab, 2 * n2, 2 * n2), lambda j, r, s: (jnp.maximum(s - 1, 0), 0, 0)),
                  full(w1)],
        out_specs=pl.BlockSpec((1, 2, slab, n2, cb), lambda j, r, s: (r, 0, jnp.maximum(s - 1, 0), 0, j)),
        out_shape=jax.ShapeDtypeStruct((orders, 2, n1, n2, c), F32),
        scratch_shapes=[pltpu.VMEM((2, n2 * n1, cb), jnp.uint32)],
        compiler_params=_cparams(("parallel", "parallel", "arbitrary")),
        name="fft_forward",
    )(taps, scale, gtw, w1)


def _hy_filter_kernel(feat_ref, fw1_ref, fb1_ref, fw2_ref, fb2_ref, fw3_ref, fb3_ref, fw4_ref, freq_ref,
                      delta_ref, z_ref, ss_ref):
    @pl.when(pl.program_id(1) == 0)
    def _():
        ss_ref[...] = jnp.zeros_like(ss_ref)

    feats = feat_ref[...]
    t = feats[:, 0:1]
    fr = freq_ref[...]
    h = jnp.sin(fr * (_dot_exact(feats, fw1_ref[...]) + fb1_ref[...]))
    h = jnp.sin(fr * (_dot_exact(h, fw2_ref[...]) + fb2_ref[...]))
    h = jnp.sin(fr * (_dot_exact(h, fw3_ref[...]) + fb3_ref[...]))
    lag0 = jnp.where(pl.program_id(0) % 2 == 1, 0.0, 1.0)
    filt = _dot(h.astype(BF16), fw4_ref[...]) * (jnp.exp(-t * delta_ref[...]) * jnp.where(t == 0.0, lag0, 1.0))
    z_ref[0] = filt.astype(BF16)
    ss_ref[0] += jnp.sum(filt * filt, axis=0, keepdims=True)


def hyena_filter(t, fw1, fb1, fw2, fb2, fw3, fb3, fw4, freq, width):
    nd = t // FFT_N2
    pos = (jnp.arange(FFT_N2, dtype=F32)[:, None] + FFT_N2 * jnp.arange(nd, dtype=F32)[None, :]).reshape(-1, 1)
    tt = pos / max(t - 1, 1)
    ang = 2 * math.pi * pos / t
    nb = (HY_EMB - 1) // 2
    bands = jnp.linspace(1e-4, nb - 1, nb, dtype=F32)[None]
    feats = jnp.concatenate([tt, jnp.cos(bands * ang), -jnp.sin(bands * ang),
                             jnp.zeros((t, LANES - HY_EMB), F32)], axis=-1)
    deltas = jnp.abs(jnp.linspace(HY_MIN_DECAY, HY_MAX_DECAY, width, dtype=F32)).reshape(1, width)
    rows = 2 * HY_ORDER
    tm = min(512, t)
    hid = fw1.shape[1]
    full = lambda a: pl.BlockSpec(a.shape, lambda r, i: (0,) * a.ndim)
    fw1p = jnp.concatenate([fw1, jnp.zeros((LANES - HY_EMB, hid), F32)], axis=0)
    vecs = [fw1p, fb1.reshape(1, hid), fw2, fb2.reshape(1, hid), fw3, fb3.reshape(1, hid)]
    return pl.pallas_call(
        _hy_filter_kernel,
        grid=(rows, t // tm),
        in_specs=[pl.BlockSpec((tm, LANES), lambda r, i: (i, 0))] + [full(a) for a in vecs]
                 + [pl.BlockSpec((hid, width), lambda r, i: (0, r)), full(freq.reshape(1, hid)), full(deltas)],
        out_specs=[pl.BlockSpec((1, tm, width), lambda r, i: (r, i, 0)),
                   pl.BlockSpec((1, 1, width), lambda r, i: (r, 0, 0))],
        out_shape=[jax.ShapeDtypeStruct((rows, t, width), BF16), jax.ShapeDtypeStruct((rows, 1, width), F32)],
        compiler_params=_cparams(("parallel", "arbitrary")),
        name="hyena_filter",
    )(feats, *vecs, fw4.astype(BF16), freq.reshape(1, hid), deltas)


def hyena_filter_spectrum_fft(t, fw1, fb1, fw2, fb2, fw3, fb3, fw4, freq, width):
    taps, ss = hyena_filter(t, fw1, fb1, fw2, fb2, fw3, fb3, fw4, freq, width)
    nd = t // FFT_N2
    scale = lax.rsqrt(ss.reshape(HY_ORDER, 2, width).sum(axis=1, keepdims=True) + 1e-6)
    return filter_spectrum(taps.reshape(HY_ORDER, 2, FFT_N2, nd, width), scale)


def _dft_conv_kernel(z_ref, kf_ref, fw_ref, iv_ref, o_ref, *, t):
    x = jnp.concatenate([z_ref[0], z_ref[1]], axis=0)
    spec = _dot(fw_ref[...], x)
    xr, xi = spec[:2 * t], spec[2 * t:]
    kr, ki = kf_ref[0], kf_ref[1]
    pr = xr * kr - xi * ki
    pi = xr * ki + xi * kr
    y = _dot(iv_ref[...], jnp.concatenate([pr, pi], axis=0).astype(BF16))
    o_ref[0] = y[:t]
    o_ref[1] = y[t:]


def dft_conv(zb, kf):
    bsz, t, c = zb.shape
    n = 2 * t
    f = np.exp(-2j * np.pi * np.outer(np.arange(n), np.arange(t)) / n)
    fw = np.block([[f.real, -f.imag], [f.imag, f.real]])
    fi = np.exp(2j * np.pi * np.outer(np.arange(t), np.arange(n)) / n) / n
    iv = np.block([[fi.real, -fi.imag], [fi.imag, fi.real]])
    fw, iv = jnp.asarray(fw, BF16), jnp.asarray(iv, BF16)
    cb = LANES
    blk = pl.BlockSpec((2, t, cb), lambda j, p: (p, 0, j))
    return pl.pallas_call(
        functools.partial(_dft_conv_kernel, t=t),
        grid=(c // cb, bsz // 2),
        in_specs=[blk, pl.BlockSpec((2, n, cb), lambda j, p: (0, 0, j)),
                  pl.BlockSpec(fw.shape, lambda j, p: (0, 0)), pl.BlockSpec(iv.shape, lambda j, p: (0, 0))],
        out_specs=blk,
        out_shape=jax.ShapeDtypeStruct(zb.shape, F32),
        compiler_params=_cparams(("parallel", "parallel")),
        name="dft_conv",
    )(zb, kf, fw, iv)


def _conv_rows(conv_ref):
    if len(conv_ref.shape) == 3:
        return conv_ref[0]
    return jnp.concatenate([conv_ref[0, :, k, :] for k in range(conv_ref.shape[2])], axis=0)


def _conv_spec(conv, tm, c):
    if conv.ndim == 3:
        return pl.BlockSpec((1, tm, c), lambda b, i: (b, i, 0))
    return pl.BlockSpec((1, FFT_N2, tm // FFT_N2, c), lambda b, i: (b, 0, i, 0))


def _hy_gate_kernel(conv_ref, z_ref, gate_ref, bias_ref, o_ref, ob_ref):
    y = gate_ref[0] * (_conv_rows(conv_ref) + bias_ref[...] * z_ref[0])
    o_ref[0] = y
    ob_ref[0] = y.astype(BF16)


def hyena_gate(conv, z, gate, bias):
    bsz, t, c = z.shape
    tm = min(SUBLANES * FFT_N2, t)
    tok = pl.BlockSpec((1, tm, c), lambda b, i: (b, i, 0))
    return pl.pallas_call(
        _hy_gate_kernel,
        grid=(bsz, t // tm),
        in_specs=[_conv_spec(conv, tm, c), tok, tok, pl.BlockSpec((1, c), lambda b, i: (0, 0))],
        out_specs=[tok, tok],
        out_shape=[jax.ShapeDtypeStruct(z.shape, F32), jax.ShapeDtypeStruct(z.shape, BF16)],
        compiler_params=_cparams(("parallel", "parallel")),
        name="hyena_gate",
    )(conv, z, gate, bias.reshape(1, c))


def _hy_out_kernel(x_ref, conv_ref, z_ref, gate_ref, bias_ref, w_ref, m5_ref, o_ref):
    y = gate_ref[0] * (_conv_rows(conv_ref) + bias_ref[...] * z_ref[0])
    o_ref[0] = x_ref[0] + m5_ref[0] * _dot(y.astype(BF16), w_ref[...])


def hyena_out(x, conv, z, gate, bias, w, m5):
    bsz, t, d = x.shape
    c = z.shape[2]
    tm = min(SUBLANES * FFT_N2, t)
    tokd = pl.BlockSpec((1, tm, d), lambda b, i: (b, i, 0))
    tokc = pl.BlockSpec((1, tm, c), lambda b, i: (b, i, 0))
    return pl.pallas_call(
        _hy_out_kernel,
        grid=(bsz, t // tm),
        in_specs=[tokd, _conv_spec(conv, tm, c), tokc, tokc, pl.BlockSpec((1, c), lambda b, i: (0, 0)),
                  pl.BlockSpec(w.shape, lambda b, i: (0, 0)), pl.BlockSpec((1, 1, d), lambda b, i: (b, 0, 0))],
        out_specs=tokd,
        out_shape=jax.ShapeDtypeStruct(x.shape, F32),
        compiler_params=_cparams(("parallel", "parallel")),
        name="hyena_out",
    )(x, conv, z, gate, bias.reshape(1, c), w, m5)


def _hyena_filter_spectrum(t, fw1, fb1, fw2, fb2, fw3, fb3, fw4, freq, width):
    pos = jnp.arange(t, dtype=F32)[:, None]
    tt = pos / max(t - 1, 1)
    ang = 2 * math.pi * pos / t
    nb = (HY_EMB - 1) // 2
    bands = jnp.linspace(1e-4, nb - 1, nb, dtype=F32)[None]
    feats = jnp.concatenate([tt, jnp.cos(bands * ang), -jnp.sin(bands * ang)], axis=-1)
    hdn = jnp.sin(freq * (feats @ fw1 + fb1))
    hdn = jnp.sin(freq * (hdn @ fw2 + fb2))
    hdn = jnp.sin(freq * (hdn @ fw3 + fb3))
    filt = (hdn @ fw4).reshape(t, HY_ORDER, 2, width)
    deltas = jnp.abs(jnp.linspace(HY_MIN_DECAY, HY_MAX_DECAY, width, dtype=F32))
    filt = filt * jnp.exp(-tt[:, :, None, None] * deltas)
    fwd, bwd = filt[:, :, 0], filt[:, :, 1]
    kern = jnp.concatenate([fwd, jnp.zeros_like(fwd[:1]), bwd[:0:-1]], axis=0)
    kern = kern * lax.rsqrt(jnp.sum(kern * kern, axis=0, keepdims=True) + 1e-6)
    spec = jnp.fft.fft(kern, axis=0)
    return jnp.stack([jnp.real(spec), jnp.imag(spec)], axis=1).transpose(2, 1, 0, 3).astype(F32)


def _block_diag_pair(m):
    z = jnp.zeros_like(m[0])
    return jnp.concatenate([jnp.concatenate([m[0], z], axis=1), jnp.concatenate([z, m[1]], axis=1)], axis=0)


def _even_mixer(lat, cx, ml, mc, gain, prm, ctx_out):
    (w_in, mu, w0, w_up, a0, a_up, g_up, k_k, k_a, r_k, gn_g, gn_b,
     lam_re, lam_im, log_dt, b_re, b_im, c_re, c_im, d_skip, w_glu, b_glu, w_out) = prm
    bsz, t_lat, d = lat.shape
    t_ctx = cx.shape[1]
    width = k_k.shape[0]
    heads = width // HEAD_DIM
    n_cols = mu.shape[0]
    s5w = d_skip.shape[0]
    ttot = t_ctx + t_lat
    nch = bsz * heads
    tm = t_ctx
    assert tm % GRID_W == 0 and t_lat % tm == 0 and 4 * nch == LANES and 2 * bsz == SUBLANES

    p_all, u_all = mod_matmul_stream(cx, lat, gain, mc, ml, w_in.astype(BF16), tm, n_cols)

    head_of = np.arange(width) // HEAD_DIM
    ones = jnp.asarray(head_of[:, None] == head_of[None, :], BF16)
    feat_consts = (mu.reshape(1, -1), _block_diag_pair(w_up).astype(BF16), w0.reshape(1, -1),
                   _block_diag_pair(a_up).astype(BF16), a0.reshape(1, -1), g_up.astype(BF16),
                   k_a.reshape(1, -1), r_k.reshape(1, -1), ones)
    v2, g, bonus, r2, k2, w2, a2 = rwkv_features(p_all, n_cols, width, feat_consts, tm)

    def key_major(x):
        x = x.reshape(2, bsz, ttot, heads, 2, HALF_HEAD)
        return x.transpose(2, 5, 0, 4, 1, 3).reshape(ttot, HALF_HEAD, LANES)

    def value_major(x):
        x = x.reshape(2, bsz, ttot, heads, HEAD_DIM)
        return x.transpose(2, 4, 0, 1, 3).reshape(ttot, HEAD_DIM, 2 * nch)

    def key_const(x):
        x = x.reshape(heads, 2, HALF_HEAD).transpose(2, 1, 0)[:, None, :, None, :]
        return jnp.broadcast_to(x, (HALF_HEAD, 2, 2, bsz, heads)).reshape(HALF_HEAD, LANES)

    o_f, o_b = rwkv_scan(key_major(r2), key_major(k2), key_major(w2), key_major(a2), value_major(v2),
                         key_const(k_a), key_const(k_k), t_ctx)

    def token_major(o, lane0):
        o = o[..., lane0:lane0 + nch].reshape(ttot, HEAD_DIM, bsz, heads)
        return o.transpose(2, 0, 3, 1).reshape(bsz, ttot, width)

    o_f, o_b = token_major(o_f, 0), token_major(o_b, LANES // 2)

    ng = lam_re.shape[1]
    gps = ng // S5_SUPER
    lam = lax.complex(lam_re, lam_im)
    dt = jnp.exp(log_dt)[..., None]
    a_bar = jnp.exp(lam * dt)
    b_bar = ((a_bar - 1) / lam)[..., None] * lax.complex(b_re, b_im)
    eye = jnp.eye(gps, dtype=F32)

    def b_mat(x):
        x = x.reshape(2, S5_SUPER, gps, S5_STATE, S5_GROUP)
        m = jnp.einsum('dsgph,gk->sdghkp', x, eye)
        return m.reshape(S5_SUPER, 2 * gps * S5_GROUP, gps * S5_STATE)

    def c_mat(x):
        x = x.reshape(2, S5_SUPER, gps, S5_GROUP, S5_STATE)
        m = jnp.einsum('dsghp,gk->skpdgh', x, eye)
        return m.reshape(S5_SUPER, gps * S5_STATE, 2 * gps * S5_GROUP)

    bm = jnp.concatenate([b_mat(jnp.real(b_bar)), b_mat(jnp.imag(b_bar))], axis=2).astype(BF16)
    cm = jnp.concatenate([c_mat(c_re), -c_mat(c_im)], axis=1).astype(BF16)

    def a_rows(x):
        x = x.reshape(2, 1, S5_SUPER, gps * S5_STATE)
        return jnp.broadcast_to(x, (2, bsz, S5_SUPER, gps * S5_STATE)).reshape(2 * bsz, S5_SUPER, -1)

    a_arr = jnp.concatenate([a_rows(jnp.real(a_bar)), a_rows(jnp.imag(a_bar))], axis=2)
    a_arr = a_arr.reshape(1, 2 * bsz, -1)

    y_f, y_b = s5_scan(u_all.transpose(1, 0, 2), bm, cm, a_arr, t_ctx)
    ys = (y_f[:, :bsz] + y_b[:, bsz:]).transpose(1, 0, 2)

    wo = w_out.astype(BF16)
    out_consts = (ones, gn_g.reshape(1, -1), gn_b.reshape(1, -1), d_skip.reshape(1, -1),
                  w_glu.astype(BF16), b_glu.reshape(1, -1), wo[:width], wo[width:])
    lat = even_out(lat, 1, o_f, o_b, bonus, g, ys, u_all, ml[5], out_consts, tm)
    if ctx_out:
        cx = even_out(cx, 0, o_f, o_b, bonus, g, ys, u_all, mc[5], out_consts, tm)
    return lat, cx


def _hyena_mixer(x, gain, m, prm):
    (w_in, conv_w, conv_b, fw1, fb1, fw2, fb2, fw3, fb3, fw4, freq, bias_d, w_out) = prm
    bsz, t, d = x.shape
    c = w_out.shape[0]
    use_fft = t % (FFT_N2 * 2) == 0 and t >= 4 * FFT_N2
    if use_fft:
        kf = hyena_filter_spectrum_fft(t, fw1, fb1, fw2, fb2, fw3, fb3, fw4, freq, c)
    else:
        kf = _hyena_filter_spectrum(t, fw1, fb1, fw2, fb2, fw3, fb3, fw4, freq, c)
    z, zb, g1, g2 = hyena_in(x, gain, m[3], m[4], w_in.astype(BF16), conv_w, conv_b)
    gates = (g1, g2)
    for n in range(HY_ORDER):
        if use_fft:
            nd = t // FFT_N2
            zt = zb.reshape(bsz, nd, FFT_N2, c).transpose(0, 2, 1, 3)
            conv = fft_conv(zt, kf, n)
        else:
            conv = dft_conv(zb, kf[n])
        if n < HY_ORDER - 1:
            z, zb = hyena_gate(conv, z, gates[n], bias_d[n])
        else:
            return hyena_out(x, conv, z, gates[n], bias_d[n], w_out.astype(BF16), m[5])


def kernel(x, c, ctx, c_ctx, norm_g, ada_w, ada_b, ffn_wg, ffn_wu, ffn_wd, final_g, ev_w_in, ev_mu, ev_w0, ev_w_up, ev_a0, ev_a_up, ev_g_up, ev_k_k, ev_k_a, ev_r_k, ev_gn_g, ev_gn_b, ev_lam_re, ev_lam_im, ev_log_dt, ev_b_re, ev_b_im, ev_c_re, ev_c_im, ev_d, ev_w_glu, ev_b_glu, ev_w_out, od_w_in, od_conv_w, od_conv_b, od_fw1, od_fb1, od_fw2, od_fb2, od_fw3, od_fb3, od_fw4, od_freq, od_bias, od_w_out):
    depth = norm_g.shape[0]
    bsz, _, d = x.shape
    n_even = (depth + 1) // 2
    last_ctx = 2 * (n_even - 1)

    cond8 = jnp.concatenate([c, c_ctx[None], jnp.zeros((8 - bsz - 1, d), F32)], axis=0)
    mods = ada_mods_all(cond8, ada_w, ada_b)

    wg, wu, wd = ffn_wg.astype(BF16), ffn_wu.astype(BF16), ffn_wd.astype(BF16)
    lat, cx = x, ctx
    for l in range(depth):
        run_ctx = l <= last_ctx
        ctx_out = l < last_ctx
        i = l // 2
        ml = [mods[l, :bsz, None, k * d:(k + 1) * d] for k in range(N_MOD)]
        mc = [jnp.broadcast_to(mods[l, bsz:bsz + 1, None, k * d:(k + 1) * d], (bsz, 1, d))
              for k in range(N_MOD)]
        lat = ffn_half(lat, norm_g[l, 0], ml[0], ml[1], ml[2], wg[l, 0], wu[l, 0], wd[l, 0])
        if run_ctx:
            cx = ffn_half(cx, norm_g[l, 0], mc[0], mc[1], mc[2], wg[l, 0], wu[l, 0], wd[l, 0])
        if l % 2 == 0:
            prm = (ev_w_in[i], ev_mu[i], ev_w0[i], ev_w_up[i], ev_a0[i], ev_a_up[i], ev_g_up[i],
                   ev_k_k[i], ev_k_a[i], ev_r_k[i], ev_gn_g[i], ev_gn_b[i],
                   ev_lam_re[i], ev_lam_im[i], ev_log_dt[i], ev_b_re[i], ev_b_im[i], ev_c_re[i], ev_c_im[i],
                   ev_d[i], ev_w_glu[i], ev_b_glu[i], ev_w_out[i])
            lat, cx = _even_mixer(lat, cx, ml, mc, norm_g[l, 1], prm, ctx_out)
        else:
            prm = (od_w_in[i], od_conv_w[i], od_conv_b[i], od_fw1[i], od_fb1[i], od_fw2[i], od_fb2[i],
                   od_fw3[i], od_fb3[i], od_fw4[i], od_freq[i], od_bias[i], od_w_out[i])
            lat = _hyena_mixer(lat, norm_g[l, 1], ml, prm)
            if ctx_out:
                cx = _hyena_mixer(cx, norm_g[l, 1], mc, prm)
        fin = final_g if l == depth - 1 else None
        lat = ffn_half(lat, norm_g[l, 2], ml[6], ml[7], ml[8], wg[l, 1], wu[l, 1], wd[l, 1], fin)
        if ctx_out:
            cx = ffn_half(cx, norm_g[l, 2], mc[6], mc[7], mc[8], wg[l, 1], wu[l, 1], wd[l, 1])
    return lat
```

```python
import functools
import math

import numpy as np
import jax
import jax.numpy as jnp
from jax import lax
from jax.experimental import pallas as pl
from jax.experimental.pallas import tpu as pltpu

F32 = jnp.float32
BF16 = jnp.bfloat16
HIGHEST = lax.Precision.HIGHEST

N_MOD = 9
NORM_EPS = 1e-6
GN_EPS = 64e-5
GRID_W = 64
HEAD_DIM = 64
HALF_HEAD = HEAD_DIM // 2
S5_GROUP = 16
S5_STATE = 64
S5_SUPER = 4
HY_ORDER = 2
HY_EMB = 33
HY_MIN_DECAY = math.log(1e-2) / 1.5
HY_MAX_DECAY = math.log(1e-2) / 0.3
LANES = 128
SUBLANES = 8
MXU_DIM = 256
FFT_N2 = 128
VMEM_LIMIT = 56 * 1024 * 1024


def _cparams(sem, vmem=VMEM_LIMIT):
    return pltpu.CompilerParams(dimension_semantics=sem, vmem_limit_bytes=vmem)


def _dot(a, b):
    return jnp.dot(a, b, preferred_element_type=F32)


def _dot_exact(a, b):
    return jnp.dot(a, b, preferred_element_type=F32, precision=HIGHEST)


def _group_sum(a, ones):
    hi = a.astype(BF16)
    lo = (a - hi.astype(F32)).astype(BF16)
    return _dot(hi, ones) + _dot(lo, ones)


def _rms_mod(x, gain, shift, scale):
    ms = jnp.mean(x * x, axis=-1, keepdims=True)
    return x * lax.rsqrt(ms + NORM_EPS) * gain * (1.0 + scale) + shift


def _sigmoid(x):
    return 1.0 / (1.0 + jnp.exp(-x))


def _silu(x):
    return x * _sigmoid(x)


def _ada_kernel(c_ref, w_ref, b_ref, o_ref):
    s = _silu(c_ref[...])
    o_ref[0] = _dot(s.astype(BF16), w_ref[0].astype(BF16)) + b_ref[0]


def ada_mods_all(cond8, ada_w, ada_b):
    depth, d, n = ada_w.shape
    tn = n // 8
    return pl.pallas_call(
        _ada_kernel,
        grid=(depth, n // tn),
        in_specs=[pl.BlockSpec((8, d), lambda l, j: (0, 0)),
                  pl.BlockSpec((1, d, tn), lambda l, j: (l, 0, j)),
                  pl.BlockSpec((1, 1, tn), lambda l, j: (l, 0, j))],
        out_specs=pl.BlockSpec((1, 8, tn), lambda l, j: (l, 0, j)),
        out_shape=jax.ShapeDtypeStruct((depth, 8, n), F32),
        compiler_params=_cparams(("parallel", "parallel")),
        name="ada_mods",
    )(cond8, ada_w, ada_b.reshape(depth, 1, n))


def _ffn_kernel(x_ref, gain_ref, shift_ref, scale_ref, gate_ref, wg_ref, wu_ref, wd_ref, fg_ref,
                o_ref, *, final_norm, chunks):
    x = x_ref[0]
    h = _rms_mod(x, gain_ref[...], shift_ref[0], scale_ref[0]).astype(BF16)
    acc = None
    for lo, hi in chunks:
        g = _dot(h, wg_ref[:, lo:hi])
        u = _dot(h, wu_ref[:, lo:hi])
        part = _dot((_silu(g) * u).astype(BF16), wd_ref[lo:hi, :])
        acc = part if acc is None else acc + part
    y = x + 0.5 * gate_ref[0] * acc
    if final_norm:
        ms = jnp.mean(y * y, axis=-1, keepdims=True)
        y = y * lax.rsqrt(ms + NORM_EPS) * fg_ref[...]
    o_ref[0] = y


def ffn_half(x, gain, shift, scale, gate, wg, wu, wd, final_g=None):
    bsz, t, d = x.shape
    ff = wg.shape[1]
    tm = min(1024, t)
    step = 2 * MXU_DIM
    chunks = tuple((lo, min(lo + step, ff)) for lo in range(0, ff, step))
    fg = jnp.ones((1, d), F32) if final_g is None else final_g.reshape(1, d)
    vec = pl.BlockSpec((1, 1, d), lambda b, i: (b, 0, 0))
    resident = lambda a: pl.BlockSpec(a.shape, lambda b, i: (0, 0), pipeline_mode=pl.Buffered(1))
    return pl.pallas_call(
        functools.partial(_ffn_kernel, final_norm=final_g is not None, chunks=chunks),
        grid=(bsz, t // tm),
        in_specs=[pl.BlockSpec((1, tm, d), lambda b, i: (b, i, 0)),
                  pl.BlockSpec((1, d), lambda b, i: (0, 0)),
                  vec, vec, vec, resident(wg), resident(wu), resident(wd),
                  pl.BlockSpec((1, d), lambda b, i: (0, 0))],
        out_specs=pl.BlockSpec((1, tm, d), lambda b, i: (b, i, 0)),
        out_shape=jax.ShapeDtypeStruct((bsz, t, d), F32),
        compiler_params=_cparams(("parallel", "parallel")),
        name="ffn_half",
    )(x, gain.reshape(1, d), shift, scale, gate, wg, wu, wd, fg)


def _modmm_kernel(cx_ref, lat_ref, gain_ref, shc_ref, scc_ref, shl_ref, scl_ref, w_ref, o_ref, u_ref, *, nc):
    is_ctx = pl.program_id(1) < nc
    x = jnp.where(is_ctx, cx_ref[0], lat_ref[0])
    shift = jnp.where(is_ctx, shc_ref[0], shl_ref[0])
    scale = jnp.where(is_ctx, scc_ref[0], scl_ref[0])
    h = _rms_mod(x, gain_ref[...], shift, scale).astype(BF16)
    n1 = o_ref.shape[2]
    o_ref[0] = _dot(h, w_ref[:, :n1])
    u_ref[0] = _dot(h, w_ref[:, n1:])


def mod_matmul_stream(cx, lat, gain, mc, ml, w, tm, n_first):
    bsz, t_ctx, d = cx.shape
    t_lat = lat.shape[1]
    n = w.shape[1]
    nc = t_ctx // tm
    nt = nc + t_lat // tm
    vec = pl.BlockSpec((1, 1, d), lambda b, i: (b, 0, 0))
    return pl.pallas_call(
        functools.partial(_modmm_kernel, nc=nc),
        grid=(bsz, nt),
        in_specs=[pl.BlockSpec((1, tm, d), lambda b, i: (b, jnp.minimum(i, nc - 1), 0)),
                  pl.BlockSpec((1, tm, d), lambda b, i: (b, jnp.maximum(i - nc, 0), 0)),
                  pl.BlockSpec((1, d), lambda b, i: (0, 0)),
                  vec, vec, vec, vec,
                  pl.BlockSpec((d, n), lambda b, i: (0, 0))],
        out_specs=[pl.BlockSpec((1, tm, n_first), lambda b, i: (b, i, 0)),
                   pl.BlockSpec((1, tm, n - n_first), lambda b, i: (b, i, 0))],
        out_shape=[jax.ShapeDtypeStruct((bsz, t_ctx + t_lat, n_first), F32),
                   jax.ShapeDtypeStruct((bsz, t_ctx + t_lat, n - n_first), F32)],
        compiler_params=_cparams(("parallel", "parallel")),
        name="mod_matmul",
    )(cx, lat, gain.reshape(1, d), mc[3], mc[4], ml[3], ml[4], w)


def _rwkv_feat_kernel(p_ref, pu_ref, pd_ref, mu_ref, wup_ref, w0_ref, aup_ref, a0_ref, gup_ref,
                      ka_ref, rk_ref, ones_ref,
                      v_o, g_o, bonus_o, r_o, k_o, w_o, a_o, *, nt, width):
    i = pl.program_id(1)
    is_ctx = i == 0
    p = p_ref[0]
    tm = p.shape[0]
    row = lax.broadcasted_iota(jnp.int32, p.shape, 0)
    lane = lax.broadcasted_iota(jnp.int32, p.shape, 1) % 4
    prev = pltpu.roll(p, 1, 0)
    nxt = pltpu.roll(p, tm - 1, 0)
    col = jnp.where(is_ctx, row, row % GRID_W)
    last = jnp.where(is_ctx, tm - 1, GRID_W - 1)
    left = jnp.where(col == 0, 0.0, prev)
    right = jnp.where(col == last, 0.0, nxt)
    up_halo = jnp.where(i > 1, pu_ref[0], 0.0)
    dn_halo = jnp.where(i < nt - 1, pd_ref[0], 0.0)
    up = jnp.where(is_ctx, left, jnp.concatenate([up_halo, p[:tm - GRID_W]], axis=0))
    down = jnp.where(is_ctx, right, jnp.concatenate([p[GRID_W:], dn_halo], axis=0))
    shifted = jnp.where(lane == 0, left, jnp.where(lane == 1, right, jnp.where(lane == 2, up, down)))
    q = p + mu_ref[...] * (shifted - p)

    w = width
    r, k, v = q[:, :w], q[:, w:2 * w], q[:, 2 * w:3 * w]
    wd = q[:, 3 * w:3 * w + LANES]
    ad = q[:, 3 * w + LANES:3 * w + 2 * LANES]
    gd = q[:, 3 * w + 2 * LANES:3 * w + 3 * LANES]

    zlin = w0_ref[...] + _dot(jnp.tanh(wd).astype(BF16), wup_ref[...])
    decay = jnp.exp(-math.exp(-0.5) * _sigmoid(zlin))
    a = _sigmoid(a0_ref[...] + _dot(ad.astype(BF16), aup_ref[...]))
    g_o[0] = _dot(_sigmoid(gd).astype(BF16), gup_ref[...])

    ones = ones_ref[...]
    for d in range(2):
        v_o[d, 0] = v
        r_o[d, 0] = r
        k_o[d, 0] = k
    ksum = jnp.zeros_like(k)
    for d in range(2):
        a_d = a[:, d * w:(d + 1) * w]
        ksum = ksum + k * (1.0 + (a_d - 1.0) * ka_ref[...])
        w_o[d, 0] = decay[:, d * w:(d + 1) * w]
        a_o[d, 0] = a_d
    bonus_o[0] = _group_sum(r * ksum * rk_ref[...], ones) * v


def rwkv_features(p, n_cols, width, consts, tm):
    bsz, t, _ = p.shape
    nt = t // tm
    hb = tm // GRID_W
    nhb = t // GRID_W
    w = width
    full = lambda a: pl.BlockSpec(a.shape, lambda b, i: (0,) * a.ndim)
    tok = pl.BlockSpec((1, tm, w), lambda b, i: (b, i, 0))
    tok2 = pl.BlockSpec((2, 1, tm, w), lambda b, i: (0, b, i, 0))
    sds = jax.ShapeDtypeStruct((bsz, t, w), F32)
    sds2 = jax.ShapeDtypeStruct((2, bsz, t, w), F32)
    return pl.pallas_call(
        functools.partial(_rwkv_feat_kernel, nt=nt, width=w),
        grid=(bsz, nt),
        in_specs=[pl.BlockSpec((1, tm, n_cols), lambda b, i: (b, i, 0)),
                  pl.BlockSpec((1, GRID_W, n_cols), lambda b, i: (b, jnp.maximum(i * hb - 1, 0), 0)),
                  pl.BlockSpec((1, GRID_W, n_cols), lambda b, i: (b, jnp.minimum((i + 1) * hb, nhb - 1), 0))]
                 + [full(a) for a in consts],
        out_specs=[tok2, tok, tok, tok2, tok2, tok2, tok2],
        out_shape=[sds2, sds, sds, sds2, sds2, sds2, sds2],
        compiler_params=_cparams(("parallel", "parallel")),
        name="rwkv_features",
    )(p, p, p, *consts)


def _rwkv_scan_kernel(rf, rb, kf, kb, wf, wb, af, ab, vf, vb, ka_ref, kks_ref, of_ref, ob_ref,
                      s_ref, vec_ref, v_ref, c_ref, look_ref, *, tc):
    @pl.when(pl.program_id(0) == 0)
    def _():
        s_ref[...] = jnp.zeros_like(s_ref)
        look_ref[:, tc] = jnp.zeros((look_ref.shape[0],) + look_ref.shape[2:], F32)

    nj, ni = s_ref.shape[0], s_ref.shape[1]
    quarter = LANES // 4

    def fold(x):
        hi = (lax.broadcasted_iota(jnp.int32, x.shape, 1) % (2 * quarter)) >= quarter
        return x + jnp.where(hi, pltpu.roll(x, quarter, 1), pltpu.roll(x, 3 * quarter, 1))

    def hsum(x):
        return jnp.sum(x, axis=0, keepdims=True)

    def jsum(x):
        return fold(jnp.broadcast_to(hsum(x), (SUBLANES, LANES)))[0:1]

    ka = ka_ref[...]
    kks = kks_ref[...]
    fwd_lane = lax.broadcasted_iota(jnp.int32, (nj, LANES), 1) < LANES // 2
    def merged(f, b, t):
        return jnp.where(fwd_lane, f[t], b[tc - 1 - t])

    cum = jnp.ones((nj, LANES), F32)
    for t in range(tc):
        w = merged(wf, wb, t)
        cum = cum * w
        kk = merged(kf, kb, t) * kks
        look_ref[0, t] = kk / jnp.maximum(jnp.sqrt(jsum(kk * kk)), 1e-12)
        look_ref[1, t] = w * merged(rf, rb, t)
        look_ref[2, t] = cum
    for t in range(tc):
        r, k, a = merged(rf, rb, t), merged(kf, kb, t), merged(af, ab, t)
        kk, kk_next, wr_next = look_ref[0, t], look_ref[0, t + 1], look_ref[1, t + 1]
        cum = look_ref[2, t]
        inv = 1.0 / cum
        kka = kk * a
        kd = k * (1.0 + (a - 1.0) * ka)
        vec_ref[0, t] = kka * inv
        vec_ref[1, t] = kd * inv
        vec_ref[2, t] = cum * kk_next
        vec_ref[3, t] = cum * wr_next
        c_ref[t, 0:1, :] = jsum(kka * r)
        c_ref[t, 1:2, :] = jsum(kd * r)
        c_ref[t, 2:3, :] = hsum(kka * kk_next)
        c_ref[t, 3:4, :] = hsum(kd * kk_next)
        c_ref[t, 4:5, :] = hsum(kka * wr_next)
        c_ref[t, 5:6, :] = hsum(kd * wr_next)
        v_ref[t] = jnp.concatenate([vf[t], vb[tc - 1 - t]], axis=-1)

    def state_sums(y1, y2):
        s1 = jnp.zeros((ni, LANES), F32)
        s2 = jnp.zeros((ni, LANES), F32)
        for j in range(nj):
            sj = s_ref[j]
            s1 = s1 + sj * y1(j)
            s2 = s2 + sj * y2(j)
        return s1, s2

    def step(t, carry):
        row = lambda n: (lambda j: vec_ref[n, t, j:j + 1, :])
        a1, a2 = state_sums(row(2), row(3))
        sa, o1 = fold(carry[0]), fold(carry[1])
        v = v_ref[t]
        out = o1 - sa * c_ref[t, 0:1, :] + v * c_ref[t, 1:2, :]
        of_ref[t] = out
        ob_ref[tc - 1 - t] = out
        nxt = (a1 - sa * c_ref[t, 2:3, :] + v * c_ref[t, 3:4, :],
               a2 - sa * c_ref[t, 4:5, :] + v * c_ref[t, 5:6, :])
        for j in range(nj):
            s_ref[j] = s_ref[j] - sa * row(0)(j) + v * row(1)(j)
        return nxt

    first = state_sums(lambda j: look_ref[0, 0, j:j + 1, :], lambda j: look_ref[1, 0, j:j + 1, :])
    lax.fori_loop(0, tc, step, first)
    for j in range(nj):
        s_ref[j] = s_ref[j] * look_ref[2, tc - 1, j:j + 1, :]


def rwkv_scan(r, k, w, a, v, ka, kks, t_ctx):
    ttot, nj, _ = r.shape
    ni, nc = v.shape[1], v.shape[2]
    tc = 32
    ncb, ntb = t_ctx // tc, ttot // tc
    fwd = lambda g: (g, 0, 0)
    bwd = lambda g: (jnp.where(g < ncb, ncb - 1 - g, ntb + ncb - 1 - g), 0, 0)
    jf, jb = pl.BlockSpec((tc, nj, LANES), fwd), pl.BlockSpec((tc, nj, LANES), bwd)
    vf, vb = pl.BlockSpec((tc, ni, nc), fwd), pl.BlockSpec((tc, ni, nc), bwd)
    of, ob = pl.BlockSpec((tc, ni, LANES), fwd), pl.BlockSpec((tc, ni, LANES), bwd)
    osd = jax.ShapeDtypeStruct((ttot, ni, LANES), F32)
    return pl.pallas_call(
        functools.partial(_rwkv_scan_kernel, tc=tc),
        grid=(ntb,),
        in_specs=[jf, jb] * 4 + [vf, vb, pl.BlockSpec(ka.shape, lambda g: (0, 0)),
                                 pl.BlockSpec(kks.shape, lambda g: (0, 0))],
        out_specs=[of, ob],
        out_shape=[osd, osd],
        scratch_shapes=[pltpu.VMEM((nj, ni, LANES), F32), pltpu.VMEM((4, tc, nj, LANES), F32),
                        pltpu.VMEM((tc, ni, LANES), F32), pltpu.VMEM((tc, SUBLANES, LANES), F32),
                        pltpu.VMEM((3, tc + 1, nj, LANES), F32)],
        compiler_params=_cparams(("arbitrary",)),
        name="rwkv_scan",
    )(r, r, k, k, w, w, a, a, v, v, ka, kks)


def _s5_kernel(uf_ref, ub_ref, bm_ref, cm_ref, a_ref, yf_ref, yb_ref, h_ref, bu_ref, ubr_ref, *, tt):
    @pl.when(pl.program_id(0) == 0)
    def _():
        h_ref[...] = jnp.zeros_like(h_ref)

    nsb = bm_ref.shape[0]
    kin = bm_ref.shape[1] // 2
    sw = bm_ref.shape[2]
    hw = sw // 2
    for t in range(tt):
        ubr_ref[t] = ub_ref[tt - 1 - t]
    nothing = jnp.zeros(uf_ref.shape, F32)
    uf = jnp.concatenate([uf_ref[...], nothing], axis=1)
    ub = jnp.concatenate([nothing, ubr_ref[...]], axis=1)
    for sb in range(nsb):
        lhs = jnp.concatenate([uf[:, :, sb * kin:(sb + 1) * kin], ub[:, :, sb * kin:(sb + 1) * kin]], axis=-1)
        lhs = lhs.reshape(tt * SUBLANES, 2 * kin).astype(BF16)
        bu_ref[:, sb * sw:(sb + 1) * sw] = _dot(lhs, bm_ref[sb])

    def step(t, hs):
        rows = pl.ds(pl.multiple_of(t * SUBLANES, SUBLANES), SUBLANES)
        out = []
        for sb in range(nsb):
            hr, hi = hs[2 * sb], hs[2 * sb + 1]
            lo = sb * sw
            ar = a_ref[0, :, lo:lo + hw]
            ai = a_ref[0, :, lo + hw:lo + sw]
            nr = ar * hr - ai * hi + bu_ref[rows, lo:lo + hw]
            ni = ar * hi + ai * hr + bu_ref[rows, lo + hw:lo + sw]
            bu_ref[rows, lo:lo + hw] = nr
            bu_ref[rows, lo + hw:lo + sw] = ni
            out += [nr, ni]
        return tuple(out)

    h0 = []
    for sb in range(nsb):
        h0 += [h_ref[:, sb * sw:sb * sw + hw], h_ref[:, sb * sw + hw:(sb + 1) * sw]]
    hs = lax.fori_loop(0, tt, step, tuple(h0))
    for sb in range(nsb):
        h_ref[:, sb * sw:sb * sw + hw] = hs[2 * sb]
        h_ref[:, sb * sw + hw:(sb + 1) * sw] = hs[2 * sb + 1]

    nout = cm_ref.shape[2] // 2
    fwd_row = (lax.broadcasted_iota(jnp.int32, (tt * SUBLANES, nout), 0) % SUBLANES) < SUBLANES // 2
    ys = []
    for sb in range(nsb):
        yy = _dot(bu_ref[:, sb * sw:(sb + 1) * sw].astype(BF16), cm_ref[sb])
        ys.append(jnp.where(fwd_row, yy[:, :nout], yy[:, nout:]))
    y = jnp.concatenate(ys, axis=1).reshape(tt, SUBLANES, nsb * nout)
    yf_ref[...] = y
    for t in range(tt):
        yb_ref[tt - 1 - t] = y[t]


def s5_scan(u, bm, cm, a, t_ctx):
    ttot, bsz, width = u.shape
    rows = 2 * bsz
    nsb, _, sw = bm.shape
    tt = 64
    ncb, ntb = t_ctx // tt, ttot // tt
    fmap = lambda g: (g, 0, 0)
    bmap = lambda g: (jnp.where(g < ncb, ncb - 1 - g, ntb + ncb - 1 - g), 0, 0)
    full = lambda x: pl.BlockSpec(x.shape, lambda g: (0,) * x.ndim)
    osd = jax.ShapeDtypeStruct((ttot, rows, width), F32)
    return pl.pallas_call(
        functools.partial(_s5_kernel, tt=tt),
        grid=(ntb,),
        in_specs=[pl.BlockSpec((tt, bsz, width), fmap), pl.BlockSpec((tt, bsz, width), bmap),
                  full(bm), full(cm), full(a)],
        out_specs=[pl.BlockSpec((tt, rows, width), fmap), pl.BlockSpec((tt, rows, width), bmap)],
        out_shape=[osd, osd],
        scratch_shapes=[pltpu.VMEM((rows, nsb * sw), F32), pltpu.VMEM((tt * rows, nsb * sw), F32),
                        pltpu.VMEM((tt, bsz, width), F32)],
        compiler_params=_cparams(("arbitrary",)),
        name="s5_scan",
    )(u, u, bm, cm, a)


def _even_out_kernel(x_ref, of_ref, ob_ref, bonus_ref, g_ref, ys_ref, u_ref, m5_ref,
                     ones_ref, gng_ref, gnb_ref, dskip_ref, wglu_ref, bglu_ref, wo1_ref, wo2_ref, o_ref):
    ones = ones_ref[...]
    inv = 1.0 / HEAD_DIM
    o = of_ref[0] + ob_ref[0]
    mean = _group_sum(o, ones) * inv
    oc = o - mean
    var = _group_sum(oc * oc, ones) * inv
    y1 = (oc * lax.rsqrt(var + GN_EPS) * gng_ref[...] + gnb_ref[...] + bonus_ref[0]) * g_ref[0]
    y = ys_ref[0] + dskip_ref[...] * u_ref[0]
    y = jax.nn.gelu(y)
    y2 = y * _sigmoid(_dot(y.astype(BF16), wglu_ref[...]) + bglu_ref[...])
    out = _dot(y1.astype(BF16), wo1_ref[...]) + _dot(y2.astype(BF16), wo2_ref[...])
    o_ref[0] = x_ref[0] + m5_ref[0] * out


def even_out(x, off, o_f, o_b, bonus, g, ys, u, m5, consts, tm):
    bsz, t, d = x.shape
    w = o_f.shape[2]
    tokd = pl.BlockSpec((1, tm, d), lambda b, i: (b, i, 0))
    tokw = pl.BlockSpec((1, tm, w), lambda b, i: (b, i + off, 0))
    toks = pl.BlockSpec((1, tm, u.shape[2]), lambda b, i: (b, i + off, 0))
    full = lambda a: pl.BlockSpec(a.shape, lambda b, i: (0,) * a.ndim)
    return pl.pallas_call(
        _even_out_kernel,
        grid=(bsz, t // tm),
        in_specs=[tokd, tokw, tokw, tokw, tokw, toks, toks,
                  pl.BlockSpec((1, 1, d), lambda b, i: (b, 0, 0))] + [full(a) for a in consts],
        out_specs=tokd,
        out_shape=jax.ShapeDtypeStruct((bsz, t, d), F32),
        compiler_params=_cparams(("parallel", "parallel")),
        name="even_out",
    )(x, o_f, o_b, bonus, g, ys, u, m5, *consts)


def _hy_in_kernel(x_ref, xp_ref, xn_ref, gain_ref, shift_ref, scale_ref, w_ref, cw_ref, cb_ref,
                  z_ref, zb_ref, g1_ref, g2_ref, *, nt, c):
    i = pl.program_id(1)
    gain, shift, scale = gain_ref[...], shift_ref[0], scale_ref[0]
    tm = x_ref.shape[1]
    halo = xp_ref.shape[1]
    h = jnp.concatenate([_rms_mod(xr[0], gain, shift, scale) for xr in (xp_ref, x_ref, xn_ref)], axis=0)
    h = h.astype(BF16)
    row = lax.broadcasted_iota(jnp.int32, (tm, c), 0)
    outs = (z_ref, g1_ref, g2_ref)
    for part in range(3):
        wp = w_ref[:, part * c:(part + 1) * c]
        p_ext = _dot(h, wp)
        p = p_ext[halo:halo + tm]
        pp = jnp.where(i > 0, p_ext[halo - 1:halo], 0.0)
        pn = jnp.where(i < nt - 1, p_ext[halo + tm:halo + tm + 1], 0.0)
        pm1 = jnp.where(row == 0, pp, pltpu.roll(p, 1, 0))
        pp1 = jnp.where(row == tm - 1, pn, pltpu.roll(p, tm - 1, 0))
        cw = cw_ref[:, part * c:(part + 1) * c]
        q = cw[0:1] * pm1 + cw[1:2] * p + cw[2:3] * pp1 + cb_ref[:, part * c:(part + 1) * c]
        outs[part][0] = q
        if part == 0:
            zb_ref[0] = q.astype(BF16)


def hyena_in(x, gain, shift, scale, w, conv_w, conv_b):
    bsz, t, d = x.shape
    c = w.shape[1] // 3
    tm = min(512, t)
    nt = t // tm
    hb = tm // 8
    nhb = t // 8
    vec = pl.BlockSpec((1, 1, d), lambda b, i: (b, 0, 0))
    tok = pl.BlockSpec((1, tm, c), lambda b, i: (b, i, 0))
    sds = jax.ShapeDtypeStruct((bsz, t, c), F32)
    return pl.pallas_call(
        functools.partial(_hy_in_kernel, nt=nt, c=c),
        grid=(bsz, nt),
        in_specs=[pl.BlockSpec((1, tm, d), lambda b, i: (b, i, 0)),
                  pl.BlockSpec((1, 8, d), lambda b, i: (b, jnp.maximum(i * hb - 1, 0), 0)),
                  pl.BlockSpec((1, 8, d), lambda b, i: (b, jnp.minimum((i + 1) * hb, nhb - 1), 0)),
                  pl.BlockSpec((1, d), lambda b, i: (0, 0)),
                  vec, vec,
                  pl.BlockSpec(w.shape, lambda b, i: (0, 0)),
                  pl.BlockSpec(conv_w.shape, lambda b, i: (0, 0)),
                  pl.BlockSpec((1, 3 * c), lambda b, i: (0, 0))],
        out_specs=[tok, tok, tok, tok],
        out_shape=[sds, jax.ShapeDtypeStruct((bsz, t, c), BF16), sds, sds],
        compiler_params=_cparams(("parallel", "parallel")),
        name="hyena_in",
    )(x, x, x, gain.reshape(1, d), shift, scale, w, conv_w, conv_b.reshape(1, 3 * c))


def _pack_complex(re, im):
    bits = lambda x: lax.bitcast_convert_type(x.astype(BF16).astype(F32), jnp.uint32)
    return lax.shift_right_logical(bits(re), jnp.uint32(16)) | bits(im)


def _unpack_complex(w):
    re = lax.bitcast_convert_type(lax.shift_left(w, jnp.uint32(16)), F32)
    im = lax.bitcast_convert_type(w & jnp.uint32(0xFFFF0000), F32)
    return jnp.concatenate([re, im], axis=0).astype(BF16)


def _fft_conv_kernel(z_ref, kf_ref, g_ref, w1_ref, g2i_ref, f1c_ref, twt_ref, o_ref, a_ref, t_ref,
                     *, n1, k1c):
    s = pl.program_id(2)
    n2 = FFT_N2
    nd = n1 // 2
    slab = n1 // k1c

    @pl.when(s == 0)
    def _():
        w1 = w1_ref[...]

        def body(j, carry):
            x = jnp.concatenate([z_ref[0, j], z_ref[1, j]], axis=0)
            a = _dot(w1, x)
            a_ref[pl.ds(pl.multiple_of(j * n1, n1), n1), :] = _pack_complex(a[:n1], a[n1:])
            return carry

        lax.fori_loop(0, n2, body, 0, unroll=16)

    @pl.when((s > 0) & (s <= k1c))
    def _():
        g2i = g2i_ref[...]
        base = (s - 1) * slab

        def body(kk, carry):
            k1 = base + kk
            ak = _unpack_complex(a_ref[pl.ds(k1, n2, stride=n1), :])
            x = _dot(g_ref[kk], ak)
            xr, xi = x[:n2], x[n2:]
            kr, ki = kf_ref[0, kk], kf_ref[1, kk]
            pr = xr * kr - xi * ki
            pi = xr * ki + xi * kr
            tt = _dot(g2i, jnp.concatenate([pr, pi], axis=0).astype(BF16))
            t_ref[pl.ds(pl.multiple_of(k1 * n2, n2), n2), :] = _pack_complex(tt[:n2], tt[n2:])
            return carry

        lax.fori_loop(0, slab, body, 0, unroll=16)

    @pl.when(s == k1c + 1)
    def _():
        f1r, f1i = f1c_ref[0], f1c_ref[1]

        def body(j, carry):
            twr = twt_ref[0, pl.ds(j, 1), :]
            twi = twt_ref[1, pl.ds(j, 1), :]
            wr = f1r * twr + f1i * twi
            wi = f1i * twr - f1r * twi
            w3 = jnp.concatenate([jnp.concatenate([wr, -wi], axis=1),
                                  jnp.concatenate([wi, wr], axis=1)], axis=0).astype(BF16)
            tn = _unpack_complex(t_ref[pl.ds(j, n1, stride=n2), :])
            y = _dot(w3, tn)
            o_ref[0, j] = y[:nd]
            o_ref[1, j] = y[nd:]
            return carry

        lax.fori_loop(0, n2, body, 0, unroll=16)


def _fft_consts(n1):
    n2 = FFT_N2
    n = n1 * n2
    nd = n1 // 2
    k1 = np.arange(n1)
    f1 = np.exp(-2j * np.pi * np.outer(k1, np.arange(nd)) / n1)
    w1 = np.block([[f1.real, -f1.imag], [f1.imag, f1.real]])
    f2 = np.exp(-2j * np.pi * np.outer(np.arange(n2), np.arange(n2)) / n2)
    f2c = np.conj(f2)
    g2i = np.block([[f2c.real, -f2c.imag], [f2c.imag, f2c.real]])
    f1c = np.exp(2j * np.pi * np.outer(np.arange(nd), k1) / n1) / n
    tw = np.exp(-2j * np.pi * np.outer(k1, np.arange(n2)) / n)
    cplx = lambda m: jnp.asarray(np.stack([m.real, m.imag]), F32)
    (f2r, f2i), (twr, twi) = cplx(f2), cplx(tw)
    gr = f2r[None] * twr[:, None, :] - f2i[None] * twi[:, None, :]
    gi = f2r[None] * twi[:, None, :] + f2i[None] * twr[:, None, :]
    gtw = jnp.concatenate([jnp.concatenate([gr, -gi], axis=2), jnp.concatenate([gi, gr], axis=2)], axis=1)
    return (gtw.astype(BF16), jnp.asarray(w1, BF16), jnp.asarray(g2i, BF16), cplx(f1c), cplx(tw.T))


def fft_conv(zt, kf, order):
    bsz, n2, nd, c = zt.shape
    n1 = 2 * nd
    cb = LANES
    k1c = 8 if n1 % 8 == 0 and n1 >= 64 else 2
    slab = n1 // k1c
    gtw, *consts = _fft_consts(n1)
    full = lambda a: pl.BlockSpec(a.shape, lambda j, p, s: (0,) * a.ndim)
    blk = pl.BlockSpec((2, n2, nd, cb), lambda j, p, s: (p, 0, 0, j))
    chunk = lambda j, p, s: jnp.clip(s - 1, 0, k1c - 1)
    return pl.pallas_call(
        functools.partial(_fft_conv_kernel, n1=n1, k1c=k1c),
        grid=(c // cb, bsz // 2, k1c + 2),
        in_specs=[blk,
                  pl.BlockSpec((None, 2, slab, n2, cb), lambda j, p, s: (order, 0, chunk(j, p, s), 0, j)),
                  pl.BlockSpec((slab, 2 * n2, 2 * n2), lambda j, p, s: (chunk(j, p, s), 0, 0))]
                 + [full(a) for a in consts],
        out_specs=blk,
        out_shape=jax.ShapeDtypeStruct(zt.shape, F32),
        scratch_shapes=[pltpu.VMEM((n2 * n1, cb), jnp.uint32), pltpu.VMEM((n1 * n2, cb), jnp.uint32)],
        compiler_params=_cparams(("parallel", "parallel", "arbitrary")),
        name="fft_conv",
    )(zt, kf, gtw, *consts)


def _fft_fwd_kernel(z_ref, scale_ref, g_ref, w1_ref, o_ref, a_ref, *, n1, k1c):
    s = pl.program_id(2)
    n2 = FFT_N2
    slab = n1 // k1c
    cb = o_ref.shape[4]

    @pl.when(s == 0)
    def _():
        w1 = w1_ref[...]

        def body(j, carry):
            a = _dot(w1, jnp.concatenate([z_ref[0, j], z_ref[1, j]], axis=1))
            packed = _pack_complex(a[:n1], a[n1:])
            rows = pl.ds(pl.multiple_of(j * n1, n1), n1)
            a_ref[0, rows, :] = packed[:, :cb]
            a_ref[1, rows, :] = packed[:, cb:]
            return carry

        lax.fori_loop(0, n2, body, 0, unroll=16)

    @pl.when(s > 0)
    def _():
        base = (s - 1) * slab

        def body(kk, carry):
            k1 = base + kk
            rows = pl.ds(k1, n2, stride=n1)
            ak = jnp.concatenate([_unpack_complex(a_ref[0, rows, :]), _unpack_complex(a_ref[1, rows, :])], axis=1)
            x = _dot(g_ref[kk], ak)
            pos, neg = x[:, :cb], x[:, cb:]
            scale = scale_ref[0]
            o_ref[0, 0, kk] = (pos[:n2] + neg[:n2]) * scale
            o_ref[0, 1, kk] = (pos[n2:] - neg[n2:]) * scale
            return carry

        lax.fori_loop(0, slab, body, 0, unroll=16)


def filter_spectrum(taps, scale):
    orders, _, n2, nd, c = taps.shape
    n1 = 2 * nd
    cb = LANES
    k1c = 8 if n1 % 8 == 0 and n1 >= 64 else 2
    slab = n1 // k1c
    gtw, w1 = _fft_consts(n1)[:2]
    w1 = w1[:, :nd]
    full = lambda a: pl.BlockSpec(a.shape, lambda j, r, s: (0,) * a.ndim)
    return pl.pallas_call(
        functools.partial(_fft_fwd_kernel, n1=n1, k1c=k1c),
        grid=(c // cb, orders, k1c + 1),
        in_specs=[pl.BlockSpec((None, 2, n2, nd, cb), lambda j, r, s: (r, 0, 0, 0, j)),
                  pl.BlockSpec((1, 1, cb), lambda j, r, s: (r, 0, j)),
                  pl.BlockSpec((slab, 2 * n2, 2 * n2), lambda j, r, s: (jnp.maximum(s - 1, 0), 0, 0)),
                  full(w1)],
        out_specs=pl.BlockSpec((1, 2, slab, n2, cb), lambda j, r, s: (r, 0, jnp.maximum(s - 1, 0), 0, j)),
        out_shape=jax.ShapeDtypeStruct((orders, 2, n1, n2, c), F32),
        scratch_shapes=[pltpu.VMEM((2, n2 * n1, cb), jnp.uint32)],
        compiler_params=_cparams(("parallel", "parallel", "arbitrary")),
        name="fft_forward",
    )(taps, scale, gtw, w1)


def _hy_filter_kernel(feat_ref, fw1_ref, fb1_ref, fw2_ref, fb2_ref, fw3_ref, fb3_ref, fw4_ref, freq_ref,
                      delta_ref, z_ref, ss_ref):
    @pl.when(pl.program_id(0) == 0)
    def _():
        ss_ref[...] = jnp.zeros_like(ss_ref)

    feats = feat_ref[...]
    t = feats[:, 0:1]
    fr = freq_ref[...]
    h = jnp.sin(fr * (_dot_exact(feats, fw1_ref[...]) + fb1_ref[...]))
    h = jnp.sin(fr * (_dot_exact(h, fw2_ref[...]) + fb2_ref[...]))
    h = jnp.sin(fr * (_dot_exact(h, fw3_ref[...]) + fb3_ref[...])).astype(BF16)
    decay = jnp.exp(-t * delta_ref[...])
    width = delta_ref.shape[1]
    for r in range(z_ref.shape[0]):
        filt = _dot(h, fw4_ref[:, r * width:(r + 1) * width]) * decay
        if r % 2 == 1:
            filt = jnp.where(t == 0.0, 0.0, filt)
        z_ref[r] = filt.astype(BF16)
        ss_ref[r] += jnp.sum(filt * filt, axis=0, keepdims=True)


def hyena_filter(t, fw1, fb1, fw2, fb2, fw3, fb3, fw4, freq, width):
    nd = t // FFT_N2
    pos = (jnp.arange(FFT_N2, dtype=F32)[:, None] + FFT_N2 * jnp.arange(nd, dtype=F32)[None, :]).reshape(-1, 1)
    tt = pos / max(t - 1, 1)
    ang = 2 * math.pi * pos / t
    nb = (HY_EMB - 1) // 2
    bands = jnp.linspace(1e-4, nb - 1, nb, dtype=F32)[None]
    feats = jnp.concatenate([tt, jnp.cos(bands * ang), -jnp.sin(bands * ang),
                             jnp.zeros((t, LANES - HY_EMB), F32)], axis=-1)
    deltas = jnp.abs(jnp.linspace(HY_MIN_DECAY, HY_MAX_DECAY, width, dtype=F32)).reshape(1, width)
    rows = 2 * HY_ORDER
    tm = min(512, t)
    hid = fw1.shape[1]
    full = lambda a: pl.BlockSpec(a.shape, lambda i: (0,) * a.ndim)
    fw1p = jnp.concatenate([fw1, jnp.zeros((LANES - HY_EMB, hid), F32)], axis=0)
    vecs = [fw1p, fb1.reshape(1, hid), fw2, fb2.reshape(1, hid), fw3, fb3.reshape(1, hid),
            fw4.astype(BF16), freq.reshape(1, hid), deltas]
    return pl.pallas_call(
        _hy_filter_kernel,
        grid=(t // tm,),
        in_specs=[pl.BlockSpec((tm, LANES), lambda i: (i, 0))] + [full(a) for a in vecs],
        out_specs=[pl.BlockSpec((rows, tm, width), lambda i: (0, i, 0)),
                   pl.BlockSpec((rows, 1, width), lambda i: (0, 0, 0))],
        out_shape=[jax.ShapeDtypeStruct((rows, t, width), BF16), jax.ShapeDtypeStruct((rows, 1, width), F32)],
        compiler_params=_cparams(("arbitrary",)),
        name="hyena_filter",
    )(feats, *vecs)


def hyena_filter_spectrum_fft(t, fw1, fb1, fw2, fb2, fw3, fb3, fw4, freq, width):
    taps, ss = hyena_filter(t, fw1, fb1, fw2, fb2, fw3, fb3, fw4, freq, width)
    nd = t // FFT_N2
    scale = lax.rsqrt(ss.reshape(HY_ORDER, 2, width).sum(axis=1, keepdims=True) + 1e-6)
    return filter_spectrum(taps.reshape(HY_ORDER, 2, FFT_N2, nd, width), scale)


def _dft_conv_kernel(z_ref, kf_ref, fw_ref, iv_ref, o_ref, *, t):
    x = jnp.concatenate([z_ref[0], z_ref[1]], axis=0)
    spec = _dot(fw_ref[...], x)
    xr, xi = spec[:2 * t], spec[2 * t:]
    kr, ki = kf_ref[0], kf_ref[1]
    pr = xr * kr - xi * ki
    pi = xr * ki + xi * kr
    y = _dot(iv_ref[...], jnp.concatenate([pr, pi], axis=0).astype(BF16))
    o_ref[0] = y[:t]
    o_ref[1] = y[t:]


def dft_conv(zb, kf):
    bsz, t, c = zb.shape
    n = 2 * t
    f = np.exp(-2j * np.pi * np.outer(np.arange(n), np.arange(t)) / n)
    fw = np.block([[f.real, -f.imag], [f.imag, f.real]])
    fi = np.exp(2j * np.pi * np.outer(np.arange(t), np.arange(n)) / n) / n
    iv = np.block([[fi.real, -fi.imag], [fi.imag, fi.real]])
    fw, iv = jnp.asarray(fw, BF16), jnp.asarray(iv, BF16)
    cb = LANES
    blk = pl.BlockSpec((2, t, cb), lambda j, p: (p, 0, j))
    return pl.pallas_call(
        functools.partial(_dft_conv_kernel, t=t),
        grid=(c // cb, bsz // 2),
        in_specs=[blk, pl.BlockSpec((2, n, cb), lambda j, p: (0, 0, j)),
                  pl.BlockSpec(fw.shape, lambda j, p: (0, 0)), pl.BlockSpec(iv.shape, lambda j, p: (0, 0))],
        out_specs=blk,
        out_shape=jax.ShapeDtypeStruct(zb.shape, F32),
        compiler_params=_cparams(("parallel", "parallel")),
        name="dft_conv",
    )(zb, kf, fw, iv)


def _conv_rows(conv_ref):
    if len(conv_ref.shape) == 3:
        return conv_ref[0]
    return jnp.concatenate([conv_ref[0, :, k, :] for k in range(conv_ref.shape[2])], axis=0)


def _conv_spec(conv, tm, c):
    if conv.ndim == 3:
        return pl.BlockSpec((1, tm, c), lambda b, i: (b, i, 0))
    return pl.BlockSpec((1, FFT_N2, tm // FFT_N2, c), lambda b, i: (b, 0, i, 0))


def _hy_gate_kernel(conv_ref, z_ref, gate_ref, bias_ref, o_ref, ob_ref):
    y = gate_ref[0] * (_conv_rows(conv_ref) + bias_ref[...] * z_ref[0])
    o_ref[0] = y
    ob_ref[0] = y.astype(BF16)


def hyena_gate(conv, z, gate, bias):
    bsz, t, c = z.shape
    tm = min(SUBLANES * FFT_N2, t)
    tok = pl.BlockSpec((1, tm, c), lambda b, i: (b, i, 0))
    return pl.pallas_call(
        _hy_gate_kernel,
        grid=(bsz, t // tm),
        in_specs=[_conv_spec(conv, tm, c), tok, tok, pl.BlockSpec((1, c), lambda b, i: (0, 0))],
        out_specs=[tok, tok],
        out_shape=[jax.ShapeDtypeStruct(z.shape, F32), jax.ShapeDtypeStruct(z.shape, BF16)],
        compiler_params=_cparams(("parallel", "parallel")),
        name="hyena_gate",
    )(conv, z, gate, bias.reshape(1, c))


def _hy_out_kernel(x_ref, conv_ref, z_ref, gate_ref, bias_ref, w_ref, m5_ref, o_ref):
    y = gate_ref[0] * (_conv_rows(conv_ref) + bias_ref[...] * z_ref[0])
    o_ref[0] = x_ref[0] + m5_ref[0] * _dot(y.astype(BF16), w_ref[...])


def hyena_out(x, conv, z, gate, bias, w, m5):
    bsz, t, d = x.shape
    c = z.shape[2]
    tm = min(SUBLANES * FFT_N2, t)
    tokd = pl.BlockSpec((1, tm, d), lambda b, i: (b, i, 0))
    tokc = pl.BlockSpec((1, tm, c), lambda b, i: (b, i, 0))
    return pl.pallas_call(
        _hy_out_kernel,
        grid=(bsz, t // tm),
        in_specs=[tokd, _conv_spec(conv, tm, c), tokc, tokc, pl.BlockSpec((1, c), lambda b, i: (0, 0)),
                  pl.BlockSpec(w.shape, lambda b, i: (0, 0)), pl.BlockSpec((1, 1, d), lambda b, i: (b, 0, 0))],
        out_specs=tokd,
        out_shape=jax.ShapeDtypeStruct(x.shape, F32),
        compiler_params=_cparams(("parallel", "parallel")),
        name="hyena_out",
    )(x, conv, z, gate, bias.reshape(1, c), w, m5)


def _hyena_filter_spectrum(t, fw1, fb1, fw2, fb2, fw3, fb3, fw4, freq, width):
    pos = jnp.arange(t, dtype=F32)[:, None]
    tt = pos / max(t - 1, 1)
    ang = 2 * math.pi * pos / t
    nb = (HY_EMB - 1) // 2
    bands = jnp.linspace(1e-4, nb - 1, nb, dtype=F32)[None]
    feats = jnp.concatenate([tt, jnp.cos(bands * ang), -jnp.sin(bands * ang)], axis=-1)
    hdn = jnp.sin(freq * (feats @ fw1 + fb1))
    hdn = jnp.sin(freq * (hdn @ fw2 + fb2))
    hdn = jnp.sin(freq * (hdn @ fw3 + fb3))
    filt = (hdn @ fw4).reshape(t, HY_ORDER, 2, width)
    deltas = jnp.abs(jnp.linspace(HY_MIN_DECAY, HY_MAX_DECAY, width, dtype=F32))
    filt = filt * jnp.exp(-tt[:, :, None, None] * deltas)
    fwd, bwd = filt[:, :, 0], filt[:, :, 1]
    kern = jnp.concatenate([fwd, jnp.zeros_like(fwd[:1]), bwd[:0:-1]], axis=0)
    kern = kern * lax.rsqrt(jnp.sum(kern * kern, axis=0, keepdims=True) + 1e-6)
    spec = jnp.fft.fft(kern, axis=0)
    return jnp.stack([jnp.real(spec), jnp.imag(spec)], axis=1).transpose(2, 1, 0, 3).astype(F32)


def _block_diag_pair(m):
    z = jnp.zeros_like(m[0])
    return jnp.concatenate([jnp.concatenate([m[0], z], axis=1), jnp.concatenate([z, m[1]], axis=1)], axis=0)


def _even_mixer(lat, cx, ml, mc, gain, prm, ctx_out):
    (w_in, mu, w0, w_up, a0, a_up, g_up, k_k, k_a, r_k, gn_g, gn_b,
     lam_re, lam_im, log_dt, b_re, b_im, c_re, c_im, d_skip, w_glu, b_glu, w_out) = prm
    bsz, t_lat, d = lat.shape
    t_ctx = cx.shape[1]
    width = k_k.shape[0]
    heads = width // HEAD_DIM
    n_cols = mu.shape[0]
    s5w = d_skip.shape[0]
    ttot = t_ctx + t_lat
    nch = bsz * heads
    tm = t_ctx
    assert tm % GRID_W == 0 and t_lat % tm == 0 and 4 * nch == LANES and 2 * bsz == SUBLANES

    p_all, u_all = mod_matmul_stream(cx, lat, gain, mc, ml, w_in.astype(BF16), tm, n_cols)

    head_of = np.arange(width) // HEAD_DIM
    ones = jnp.asarray(head_of[:, None] == head_of[None, :], BF16)
    feat_consts = (mu.reshape(1, -1), _block_diag_pair(w_up).astype(BF16), w0.reshape(1, -1),
                   _block_diag_pair(a_up).astype(BF16), a0.reshape(1, -1), g_up.astype(BF16),
                   k_a.reshape(1, -1), r_k.reshape(1, -1), ones)
    v2, g, bonus, r2, k2, w2, a2 = rwkv_features(p_all, n_cols, width, feat_consts, tm)

    def key_major(x):
        x = x.reshape(2, bsz, ttot, heads, 2, HALF_HEAD)
        return x.transpose(2, 5, 0, 4, 1, 3).reshape(ttot, HALF_HEAD, LANES)

    def value_major(x):
        x = x.reshape(2, bsz, ttot, heads, HEAD_DIM)
        return x.transpose(2, 4, 0, 1, 3).reshape(ttot, HEAD_DIM, 2 * nch)

    def key_const(x):
        x = x.reshape(heads, 2, HALF_HEAD).transpose(2, 1, 0)[:, None, :, None, :]
        return jnp.broadcast_to(x, (HALF_HEAD, 2, 2, bsz, heads)).reshape(HALF_HEAD, LANES)

    o_f, o_b = rwkv_scan(key_major(r2), key_major(k2), key_major(w2), key_major(a2), value_major(v2),
                         key_const(k_a), key_const(k_k), t_ctx)

    def token_major(o, lane0):
        o = o[..., lane0:lane0 + nch].reshape(ttot, HEAD_DIM, bsz, heads)
        return o.transpose(2, 0, 3, 1).reshape(bsz, ttot, width)

    o_f, o_b = token_major(o_f, 0), token_major(o_b, LANES // 2)

    ng = lam_re.shape[1]
    gps = ng // S5_SUPER
    lam = lax.complex(lam_re, lam_im)
    dt = jnp.exp(log_dt)[..., None]
    a_bar = jnp.exp(lam * dt)
    b_bar = ((a_bar - 1) / lam)[..., None] * lax.complex(b_re, b_im)
    eye = jnp.eye(gps, dtype=F32)

    def b_mat(x):
        x = x.reshape(2, S5_SUPER, gps, S5_STATE, S5_GROUP)
        m = jnp.einsum('dsgph,gk->sdghkp', x, eye)
        return m.reshape(S5_SUPER, 2 * gps * S5_GROUP, gps * S5_STATE)

    def c_mat(x):
        x = x.reshape(2, S5_SUPER, gps, S5_GROUP, S5_STATE)
        m = jnp.einsum('dsghp,gk->skpdgh', x, eye)
        return m.reshape(S5_SUPER, gps * S5_STATE, 2 * gps * S5_GROUP)

    bm = jnp.concatenate([b_mat(jnp.real(b_bar)), b_mat(jnp.imag(b_bar))], axis=2).astype(BF16)
    cm = jnp.concatenate([c_mat(c_re), -c_mat(c_im)], axis=1).astype(BF16)

    def a_rows(x):
        x = x.reshape(2, 1, S5_SUPER, gps * S5_STATE)
        return jnp.broadcast_to(x, (2, bsz, S5_SUPER, gps * S5_STATE)).reshape(2 * bsz, S5_SUPER, -1)

    a_arr = jnp.concatenate([a_rows(jnp.real(a_bar)), a_rows(jnp.imag(a_bar))], axis=2)
    a_arr = a_arr.reshape(1, 2 * bsz, -1)

    y_f, y_b = s5_scan(u_all.transpose(1, 0, 2), bm, cm, a_arr, t_ctx)
    ys = (y_f[:, :bsz] + y_b[:, bsz:]).transpose(1, 0, 2)

    wo = w_out.astype(BF16)
    out_consts = (ones, gn_g.reshape(1, -1), gn_b.reshape(1, -1), d_skip.reshape(1, -1),
                  w_glu.astype(BF16), b_glu.reshape(1, -1), wo[:width], wo[width:])
    lat = even_out(lat, 1, o_f, o_b, bonus, g, ys, u_all, ml[5], out_consts, tm)
    if ctx_out:
        cx = even_out(cx, 0, o_f, o_b, bonus, g, ys, u_all, mc[5], out_consts, tm)
    return lat, cx


def _hyena_mixer(x, gain, m, prm):
    (w_in, conv_w, conv_b, fw1, fb1, fw2, fb2, fw3, fb3, fw4, freq, bias_d, w_out) = prm
    bsz, t, d = x.shape
    c = w_out.shape[0]
    use_fft = t % (FFT_N2 * 2) == 0 and t >= 4 * FFT_N2
    if use_fft:
        kf = hyena_filter_spectrum_fft(t, fw1, fb1, fw2, fb2, fw3, fb3, fw4, freq, c)
    else:
        kf = _hyena_filter_spectrum(t, fw1, fb1, fw2, fb2, fw3, fb3, fw4, freq, c)
    z, zb, g1, g2 = hyena_in(x, gain, m[3], m[4], w_in.astype(BF16), conv_w, conv_b)
    gates = (g1, g2)
    for n in range(HY_ORDER):
        if use_fft:
            nd = t // FFT_N2
            zt = zb.reshape(bsz, nd, FFT_N2, c).transpose(0, 2, 1, 3)
            conv = fft_conv(zt, kf, n)
        else:
            conv = dft_conv(zb, kf[n])
        if n < HY_ORDER - 1:
            z, zb = hyena_gate(conv, z, gates[n], bias_d[n])
        else:
            return hyena_out(x, conv, z, gates[n], bias_d[n], w_out.astype(BF16), m[5])


def kernel(x, c, ctx, c_ctx, norm_g, ada_w, ada_b, ffn_wg, ffn_wu, ffn_wd, final_g, ev_w_in, ev_mu, ev_w0, ev_w_up, ev_a0, ev_a_up, ev_g_up, ev_k_k, ev_k_a, ev_r_k, ev_gn_g, ev_gn_b, ev_lam_re, ev_lam_im, ev_log_dt, ev_b_re, ev_b_im, ev_c_re, ev_c_im, ev_d, ev_w_glu, ev_b_glu, ev_w_out, od_w_in, od_conv_w, od_conv_b, od_fw1, od_fb1, od_fw2, od_fb2, od_fw3, od_fb3, od_fw4, od_freq, od_bias, od_w_out):
    depth = norm_g.shape[0]
    bsz, _, d = x.shape
    n_even = (depth + 1) // 2
    last_ctx = 2 * (n_even - 1)

    cond8 = jnp.concatenate([c, c_ctx[None], jnp.zeros((8 - bsz - 1, d), F32)], axis=0)
    mods = ada_mods_all(cond8, ada_w, ada_b)

    wg, wu, wd = ffn_wg.astype(BF16), ffn_wu.astype(BF16), ffn_wd.astype(BF16)
    lat, cx = x, ctx
    for l in range(depth):
        run_ctx = l <= last_ctx
        ctx_out = l < last_ctx
        i = l // 2
        ml = [mods[l, :bsz, None, k * d:(k + 1) * d] for k in range(N_MOD)]
        mc = [jnp.broadcast_to(mods[l, bsz:bsz + 1, None, k * d:(k + 1) * d], (bsz, 1, d))
              for k in range(N_MOD)]
        lat = ffn_half(lat, norm_g[l, 0], ml[0], ml[1], ml[2], wg[l, 0], wu[l, 0], wd[l, 0])
        if run_ctx:
            cx = ffn_half(cx, norm_g[l, 0], mc[0], mc[1], mc[2], wg[l, 0], wu[l, 0], wd[l, 0])
        if l % 2 == 0:
            prm = (ev_w_in[i], ev_mu[i], ev_w0[i], ev_w_up[i], ev_a0[i], ev_a_up[i], ev_g_up[i],
                   ev_k_k[i], ev_k_a[i], ev_r_k[i], ev_gn_g[i], ev_gn_b[i],
                   ev_lam_re[i], ev_lam_im[i], ev_log_dt[i], ev_b_re[i], ev_b_im[i], ev_c_re[i], ev_c_im[i],
                   ev_d[i], ev_w_glu[i], ev_b_glu[i], ev_w_out[i])
            lat, cx = _even_mixer(lat, cx, ml, mc, norm_g[l, 1], prm, ctx_out)
        else:
            prm = (od_w_in[i], od_conv_w[i], od_conv_b[i], od_fw1[i], od_fb1[i], od_fw2[i], od_fb2[i],
                   od_fw3[i], od_fb3[i], od_fw4[i], od_freq[i], od_bias[i], od_w_out[i])
            lat = _hyena_mixer(lat, norm_g[l, 1], ml, prm)
            if ctx_out:
                cx = _hyena_mixer(cx, norm_g[l, 1], mc, prm)
        fin = final_g if l == depth - 1 else None
        lat = ffn_half(lat, norm_g[l, 2], ml[6], ml[7], ml[8], wg[l, 1], wu[l, 1], wd[l, 1], fin)
        if ctx_out:
            cx = ffn_half(cx, norm_g[l, 2], mc[6], mc[7], mc[8], wg[l, 1], wu[l, 1], wd[l, 1])
    return lat
```

```python
import functools
import math

import numpy as np
import jax
import jax.numpy as jnp
from jax import lax
from jax.experimental import pallas as pl
from jax.experimental.pallas import tpu as pltpu

F32 = jnp.float32
BF16 = jnp.bfloat16
HIGHEST = lax.Precision.HIGHEST

N_MOD = 9
NORM_EPS = 1e-6
GN_EPS = 64e-5
GRID_W = 64
HEAD_DIM = 64
HALF_HEAD = HEAD_DIM // 2
S5_GROUP = 16
S5_STATE = 64
S5_SUPER = 4
HY_ORDER = 2
HY_EMB = 33
HY_MIN_DECAY = math.log(1e-2) / 1.5
HY_MAX_DECAY = math.log(1e-2) / 0.3
LANES = 128
SUBLANES = 8
MXU_DIM = 256
FFT_N2 = 128
VMEM_LIMIT = 56 * 1024 * 1024


def _cparams(sem, vmem=VMEM_LIMIT):
    return pltpu.CompilerParams(dimension_semantics=sem, vmem_limit_bytes=vmem)


def _dot(a, b):
    return jnp.dot(a, b, preferred_element_type=F32)


def _dot_exact(a, b):
    return jnp.dot(a, b, preferred_element_type=F32, precision=HIGHEST)


def _group_sum(a, ones):
    hi = a.astype(BF16)
    lo = (a - hi.astype(F32)).astype(BF16)
    return _dot(hi, ones) + _dot(lo, ones)


def _rms_mod(x, gain, shift, scale):
    ms = jnp.mean(x * x, axis=-1, keepdims=True)
    return x * lax.rsqrt(ms + NORM_EPS) * gain * (1.0 + scale) + shift


def _sigmoid(x):
    return 1.0 / (1.0 + jnp.exp(-x))


def _silu(x):
    return x * _sigmoid(x)


def _ada_kernel(c_ref, w_ref, b_ref, o_ref):
    s = _silu(c_ref[...])
    o_ref[0] = _dot(s.astype(BF16), w_ref[0].astype(BF16)) + b_ref[0]


def ada_mods_all(cond8, ada_w, ada_b):
    depth, d, n = ada_w.shape
    tn = n // 8
    return pl.pallas_call(
        _ada_kernel,
        grid=(depth, n // tn),
        in_specs=[pl.BlockSpec((8, d), lambda l, j: (0, 0)),
                  pl.BlockSpec((1, d, tn), lambda l, j: (l, 0, j)),
                  pl.BlockSpec((1, 1, tn), lambda l, j: (l, 0, j))],
        out_specs=pl.BlockSpec((1, 8, tn), lambda l, j: (l, 0, j)),
        out_shape=jax.ShapeDtypeStruct((depth, 8, n), F32),
        compiler_params=_cparams(("parallel", "parallel")),
        name="ada_mods",
    )(cond8, ada_w, ada_b.reshape(depth, 1, n))


def _ffn_kernel(x_ref, gain_ref, shift_ref, scale_ref, gate_ref, wg_ref, wu_ref, wd_ref, fg_ref,
                o_ref, *, final_norm, chunks):
    x = x_ref[0]
    h = _rms_mod(x, gain_ref[...], shift_ref[0], scale_ref[0]).astype(BF16)
    acc = None
    for lo, hi in chunks:
        g = _dot(h, wg_ref[:, lo:hi])
        u = _dot(h, wu_ref[:, lo:hi])
        part = _dot((_silu(g) * u).astype(BF16), wd_ref[lo:hi, :])
        acc = part if acc is None else acc + part
    y = x + 0.5 * gate_ref[0] * acc
    if final_norm:
        ms = jnp.mean(y * y, axis=-1, keepdims=True)
        y = y * lax.rsqrt(ms + NORM_EPS) * fg_ref[...]
    o_ref[0] = y


def ffn_half(x, gain, shift, scale, gate, wg, wu, wd, final_g=None):
    bsz, t, d = x.shape
    ff = wg.shape[1]
    tm = min(1024, t)
    step = 2 * MXU_DIM
    chunks = tuple((lo, min(lo + step, ff)) for lo in range(0, ff, step))
    fg = jnp.ones((1, d), F32) if final_g is None else final_g.reshape(1, d)
    vec = pl.BlockSpec((1, 1, d), lambda b, i: (b, 0, 0))
    resident = lambda a: pl.BlockSpec(a.shape, lambda b, i: (0, 0), pipeline_mode=pl.Buffered(1))
    return pl.pallas_call(
        functools.partial(_ffn_kernel, final_norm=final_g is not None, chunks=chunks),
        grid=(bsz, t // tm),
        in_specs=[pl.BlockSpec((1, tm, d), lambda b, i: (b, i, 0)),
                  pl.BlockSpec((1, d), lambda b, i: (0, 0)),
                  vec, vec, vec, resident(wg), resident(wu), resident(wd),
                  pl.BlockSpec((1, d), lambda b, i: (0, 0))],
        out_specs=pl.BlockSpec((1, tm, d), lambda b, i: (b, i, 0)),
        out_shape=jax.ShapeDtypeStruct((bsz, t, d), F32),
        compiler_params=_cparams(("parallel", "parallel")),
        name="ffn_half",
    )(x, gain.reshape(1, d), shift, scale, gate, wg, wu, wd, fg)


def _modmm_kernel(cx_ref, lat_ref, gain_ref, shc_ref, scc_ref, shl_ref, scl_ref, w_ref, o_ref, u_ref, *, nc):
    is_ctx = pl.program_id(1) < nc
    x = jnp.where(is_ctx, cx_ref[0], lat_ref[0])
    shift = jnp.where(is_ctx, shc_ref[0], shl_ref[0])
    scale = jnp.where(is_ctx, scc_ref[0], scl_ref[0])
    h = _rms_mod(x, gain_ref[...], shift, scale).astype(BF16)
    n1 = o_ref.shape[2]
    o_ref[0] = _dot(h, w_ref[:, :n1])
    u_ref[0] = _dot(h, w_ref[:, n1:])


def mod_matmul_stream(cx, lat, gain, mc, ml, w, tm, n_first):
    bsz, t_ctx, d = cx.shape
    t_lat = lat.shape[1]
    n = w.shape[1]
    nc = t_ctx // tm
    nt = nc + t_lat // tm
    vec = pl.BlockSpec((1, 1, d), lambda b, i: (b, 0, 0))
    return pl.pallas_call(
        functools.partial(_modmm_kernel, nc=nc),
        grid=(bsz, nt),
        in_specs=[pl.BlockSpec((1, tm, d), lambda b, i: (b, jnp.minimum(i, nc - 1), 0)),
                  pl.BlockSpec((1, tm, d), lambda b, i: (b, jnp.maximum(i - nc, 0), 0)),
                  pl.BlockSpec((1, d), lambda b, i: (0, 0)),
                  vec, vec, vec, vec,
                  pl.BlockSpec((d, n), lambda b, i: (0, 0))],
        out_specs=[pl.BlockSpec((1, tm, n_first), lambda b, i: (b, i, 0)),
                   pl.BlockSpec((1, tm, n - n_first), lambda b, i: (b, i, 0))],
        out_shape=[jax.ShapeDtypeStruct((bsz, t_ctx + t_lat, n_first), F32),
                   jax.ShapeDtypeStruct((bsz, t_ctx + t_lat, n - n_first), F32)],
        compiler_params=_cparams(("parallel", "parallel")),
        name="mod_matmul",
    )(cx, lat, gain.reshape(1, d), mc[3], mc[4], ml[3], ml[4], w)


def _rwkv_feat_kernel(p_ref, pu_ref, pd_ref, mu_ref, wup_ref, w0_ref, aup_ref, a0_ref, gup_ref,
                      ka_ref, rk_ref, ones_ref,
                      v_o, g_o, bonus_o, r_o, k_o, w_o, a_o, *, nt, width):
    i = pl.program_id(1)
    is_ctx = i == 0
    p = p_ref[0]
    tm = p.shape[0]
    row = lax.broadcasted_iota(jnp.int32, p.shape, 0)
    lane = lax.broadcasted_iota(jnp.int32, p.shape, 1) % 4
    prev = pltpu.roll(p, 1, 0)
    nxt = pltpu.roll(p, tm - 1, 0)
    col = jnp.where(is_ctx, row, row % GRID_W)
    last = jnp.where(is_ctx, tm - 1, GRID_W - 1)
    left = jnp.where(col == 0, 0.0, prev)
    right = jnp.where(col == last, 0.0, nxt)
    up_halo = jnp.where(i > 1, pu_ref[0], 0.0)
    dn_halo = jnp.where(i < nt - 1, pd_ref[0], 0.0)
    up = jnp.where(is_ctx, left, jnp.concatenate([up_halo, p[:tm - GRID_W]], axis=0))
    down = jnp.where(is_ctx, right, jnp.concatenate([p[GRID_W:], dn_halo], axis=0))
    shifted = jnp.where(lane == 0, left, jnp.where(lane == 1, right, jnp.where(lane == 2, up, down)))
    q = p + mu_ref[...] * (shifted - p)

    w = width
    r, k, v = q[:, :w], q[:, w:2 * w], q[:, 2 * w:3 * w]
    wd = q[:, 3 * w:3 * w + LANES]
    ad = q[:, 3 * w + LANES:3 * w + 2 * LANES]
    gd = q[:, 3 * w + 2 * LANES:3 * w + 3 * LANES]

    zlin = w0_ref[...] + _dot(jnp.tanh(wd).astype(BF16), wup_ref[...])
    decay = jnp.exp(-math.exp(-0.5) * _sigmoid(zlin))
    a = _sigmoid(a0_ref[...] + _dot(ad.astype(BF16), aup_ref[...]))
    g_o[0] = _dot(_sigmoid(gd).astype(BF16), gup_ref[...])

    ones = ones_ref[...]
    for d in range(2):
        v_o[d, 0] = v
        r_o[d, 0] = r
        k_o[d, 0] = k
    ksum = jnp.zeros_like(k)
    for d in range(2):
        a_d = a[:, d * w:(d + 1) * w]
        ksum = ksum + k * (1.0 + (a_d - 1.0) * ka_ref[...])
        w_o[d, 0] = decay[:, d * w:(d + 1) * w]
        a_o[d, 0] = a_d
    bonus_o[0] = _group_sum(r * ksum * rk_ref[...], ones) * v


def rwkv_features(p, n_cols, width, consts, tm):
    bsz, t, _ = p.shape
    nt = t // tm
    hb = tm // GRID_W
    nhb = t // GRID_W
    w = width
    full = lambda a: pl.BlockSpec(a.shape, lambda b, i: (0,) * a.ndim)
    tok = pl.BlockSpec((1, tm, w), lambda b, i: (b, i, 0))
    tok2 = pl.BlockSpec((2, 1, tm, w), lambda b, i: (0, b, i, 0))
    sds = jax.ShapeDtypeStruct((bsz, t, w), F32)
    sds2 = jax.ShapeDtypeStruct((2, bsz, t, w), F32)
    return pl.pallas_call(
        functools.partial(_rwkv_feat_kernel, nt=nt, width=w),
        grid=(bsz, nt),
        in_specs=[pl.BlockSpec((1, tm, n_cols), lambda b, i: (b, i, 0)),
                  pl.BlockSpec((1, GRID_W, n_cols), lambda b, i: (b, jnp.maximum(i * hb - 1, 0), 0)),
                  pl.BlockSpec((1, GRID_W, n_cols), lambda b, i: (b, jnp.minimum((i + 1) * hb, nhb - 1), 0))]
                 + [full(a) for a in consts],
        out_specs=[tok2, tok, tok, tok2, tok2, tok2, tok2],
        out_shape=[sds2, sds, sds, sds2, sds2, sds2, sds2],
        compiler_params=_cparams(("parallel", "parallel")),
        name="rwkv_features",
    )(p, p, p, *consts)


def _rwkv_scan_kernel(rf, rb, kf, kb, wf, wb, af, ab, vf, vb, ka_ref, kks_ref, of_ref, ob_ref,
                      s_ref, vec_ref, v_ref, c_ref, look_ref, *, tc):
    @pl.when(pl.program_id(0) == 0)
    def _():
        s_ref[...] = jnp.zeros_like(s_ref)
        look_ref[:, tc] = jnp.zeros((look_ref.shape[0],) + look_ref.shape[2:], F32)

    nj, ni = s_ref.shape[0], s_ref.shape[1]
    quarter = LANES // 4

    def fold(x):
        hi = (lax.broadcasted_iota(jnp.int32, x.shape, 1) % (2 * quarter)) >= quarter
        return x + jnp.where(hi, pltpu.roll(x, quarter, 1), pltpu.roll(x, 3 * quarter, 1))

    def hsum(x):
        return jnp.sum(x, axis=0, keepdims=True)

    def jsum(x):
        return fold(jnp.broadcast_to(hsum(x), (SUBLANES, LANES)))[0:1]

    ka = ka_ref[...]
    kks = kks_ref[...]
    fwd_lane = lax.broadcasted_iota(jnp.int32, (nj, LANES), 1) < LANES // 2
    def merged(f, b, t):
        return jnp.where(fwd_lane, f[t], b[tc - 1 - t])

    cum = jnp.ones((nj, LANES), F32)
    for t in range(tc):
        w = merged(wf, wb, t)
        cum = cum * w
        kk = merged(kf, kb, t) * kks
        look_ref[0, t] = kk / jnp.maximum(jnp.sqrt(jsum(kk * kk)), 1e-12)
        look_ref[1, t] = w * merged(rf, rb, t)
        look_ref[2, t] = cum
    for t in range(tc):
        r, k, a = merged(rf, rb, t), merged(kf, kb, t), merged(af, ab, t)
        kk, kk_next, wr_next = look_ref[0, t], look_ref[0, t + 1], look_ref[1, t + 1]
        cum = look_ref[2, t]
        inv = 1.0 / cum
        kka = kk * a
        kd = k * (1.0 + (a - 1.0) * ka)
        vec_ref[0, t] = kka * inv
        vec_ref[1, t] = kd * inv
        vec_ref[2, t] = cum * kk_next
        vec_ref[3, t] = cum * wr_next
        c_ref[t, 0:1, :] = jsum(kka * r)
        c_ref[t, 1:2, :] = jsum(kd * r)
        c_ref[t, 2:3, :] = hsum(kka * kk_next)
        c_ref[t, 3:4, :] = hsum(kd * kk_next)
        c_ref[t, 4:5, :] = hsum(kka * wr_next)
        c_ref[t, 5:6, :] = hsum(kd * wr_next)
        v_ref[t] = jnp.concatenate([vf[t], vb[tc - 1 - t]], axis=-1)

    def state_sums(y1, y2):
        s1 = jnp.zeros((ni, LANES), F32)
        s2 = jnp.zeros((ni, LANES), F32)
        for j in range(nj):
            sj = s_ref[j]
            s1 = s1 + sj * y1(j)
            s2 = s2 + sj * y2(j)
        return s1, s2

    def step(t, carry):
        row = lambda n: (lambda j: vec_ref[n, t, j:j + 1, :])
        a1, a2 = state_sums(row(2), row(3))
        sa, o1 = fold(carry[0]), fold(carry[1])
        v = v_ref[t]
        out = o1 - sa * c_ref[t, 0:1, :] + v * c_ref[t, 1:2, :]
        of_ref[t] = out
        ob_ref[tc - 1 - t] = out
        nxt = (a1 - sa * c_ref[t, 2:3, :] + v * c_ref[t, 3:4, :],
               a2 - sa * c_ref[t, 4:5, :] + v * c_ref[t, 5:6, :])
        for j in range(nj):
            s_ref[j] = s_ref[j] - sa * row(0)(j) + v * row(1)(j)
        return nxt

    first = state_sums(lambda j: look_ref[0, 0, j:j + 1, :], lambda j: look_ref[1, 0, j:j + 1, :])
    lax.fori_loop(0, tc, step, first)
    for j in range(nj):
        s_ref[j] = s_ref[j] * look_ref[2, tc - 1, j:j + 1, :]


def rwkv_scan(r, k, w, a, v, ka, kks, t_ctx):
    ttot, nj, _ = r.shape
    ni, nc = v.shape[1], v.shape[2]
    tc = 32
    ncb, ntb = t_ctx // tc, ttot // tc
    fwd = lambda g: (g, 0, 0)
    bwd = lambda g: (jnp.where(g < ncb, ncb - 1 - g, ntb + ncb - 1 - g), 0, 0)
    jf, jb = pl.BlockSpec((tc, nj, LANES), fwd), pl.BlockSpec((tc, nj, LANES), bwd)
    vf, vb = pl.BlockSpec((tc, ni, nc), fwd), pl.BlockSpec((tc, ni, nc), bwd)
    of, ob = pl.BlockSpec((tc, ni, LANES), fwd), pl.BlockSpec((tc, ni, LANES), bwd)
    osd = jax.ShapeDtypeStruct((ttot, ni, LANES), F32)
    return pl.pallas_call(
        functools.partial(_rwkv_scan_kernel, tc=tc),
        grid=(ntb,),
        in_specs=[jf, jb] * 4 + [vf, vb, pl.BlockSpec(ka.shape, lambda g: (0, 0)),
                                 pl.BlockSpec(kks.shape, lambda g: (0, 0))],
        out_specs=[of, ob],
        out_shape=[osd, osd],
        scratch_shapes=[pltpu.VMEM((nj, ni, LANES), F32), pltpu.VMEM((4, tc, nj, LANES), F32),
                        pltpu.VMEM((tc, ni, LANES), F32), pltpu.VMEM((tc, SUBLANES, LANES), F32),
                        pltpu.VMEM((3, tc + 1, nj, LANES), F32)],
        compiler_params=_cparams(("arbitrary",)),
        name="rwkv_scan",
    )(r, r, k, k, w, w, a, a, v, v, ka, kks)


def _s5_kernel(uf_ref, ub_ref, bm_ref, cm_ref, a_ref, yf_ref, yb_ref, h_ref, bu_ref, ubr_ref, *, tt):
    @pl.when(pl.program_id(0) == 0)
    def _():
        h_ref[...] = jnp.zeros_like(h_ref)

    nsb = bm_ref.shape[0]
    kin = bm_ref.shape[1] // 2
    sw = bm_ref.shape[2]
    hw = sw // 2
    for t in range(tt):
        ubr_ref[t] = ub_ref[tt - 1 - t]
    nothing = jnp.zeros(uf_ref.shape, F32)
    uf = jnp.concatenate([uf_ref[...], nothing], axis=1)
    ub = jnp.concatenate([nothing, ubr_ref[...]], axis=1)
    for sb in range(nsb):
        lhs = jnp.concatenate([uf[:, :, sb * kin:(sb + 1) * kin], ub[:, :, sb * kin:(sb + 1) * kin]], axis=-1)
        lhs = lhs.reshape(tt * SUBLANES, 2 * kin).astype(BF16)
        bu_ref[:, sb * sw:(sb + 1) * sw] = _dot(lhs, bm_ref[sb])

    def step(t, hs):
        rows = pl.ds(pl.multiple_of(t * SUBLANES, SUBLANES), SUBLANES)
        out = []
        for sb in range(nsb):
            hr, hi = hs[2 * sb], hs[2 * sb + 1]
            lo = sb * sw
            ar = a_ref[0, :, lo:lo + hw]
            ai = a_ref[0, :, lo + hw:lo + sw]
            nr = ar * hr - ai * hi + bu_ref[rows, lo:lo + hw]
            ni = ar * hi + ai * hr + bu_ref[rows, lo + hw:lo + sw]
            bu_ref[rows, lo:lo + hw] = nr
            bu_ref[rows, lo + hw:lo + sw] = ni
            out += [nr, ni]
        return tuple(out)

    h0 = []
    for sb in range(nsb):
        h0 += [h_ref[:, sb * sw:sb * sw + hw], h_ref[:, sb * sw + hw:(sb + 1) * sw]]
    hs = lax.fori_loop(0, tt, step, tuple(h0))
    for sb in range(nsb):
        h_ref[:, sb * sw:sb * sw + hw] = hs[2 * sb]
        h_ref[:, sb * sw + hw:(sb + 1) * sw] = hs[2 * sb + 1]

    nout = cm_ref.shape[2] // 2
    fwd_row = (lax.broadcasted_iota(jnp.int32, (tt * SUBLANES, nout), 0) % SUBLANES) < SUBLANES // 2
    ys = []
    for sb in range(nsb):
        yy = _dot(bu_ref[:, sb * sw:(sb + 1) * sw].astype(BF16), cm_ref[sb])
        ys.append(jnp.where(fwd_row, yy[:, :nout], yy[:, nout:]))
    y = jnp.concatenate(ys, axis=1).reshape(tt, SUBLANES, nsb * nout)
    yf_ref[...] = y
    for t in range(tt):
        yb_ref[tt - 1 - t] = y[t]


def s5_scan(u, bm, cm, a, t_ctx):
    ttot, bsz, width = u.shape
    rows = 2 * bsz
    nsb, _, sw = bm.shape
    tt = 128
    ncb, ntb = t_ctx // tt, ttot // tt
    fmap = lambda g: (g, 0, 0)
    bmap = lambda g: (jnp.where(g < ncb, ncb - 1 - g, ntb + ncb - 1 - g), 0, 0)
    full = lambda x: pl.BlockSpec(x.shape, lambda g: (0,) * x.ndim)
    osd = jax.ShapeDtypeStruct((ttot, rows, width), F32)
    return pl.pallas_call(
        functools.partial(_s5_kernel, tt=tt),
        grid=(ntb,),
        in_specs=[pl.BlockSpec((tt, bsz, width), fmap), pl.BlockSpec((tt, bsz, width), bmap),
                  full(bm), full(cm), full(a)],
        out_specs=[pl.BlockSpec((tt, rows, width), fmap), pl.BlockSpec((tt, rows, width), bmap)],
        out_shape=[osd, osd],
        scratch_shapes=[pltpu.VMEM((rows, nsb * sw), F32), pltpu.VMEM((tt * rows, nsb * sw), F32),
                        pltpu.VMEM((tt, bsz, width), F32)],
        compiler_params=_cparams(("arbitrary",)),
        name="s5_scan",
    )(u, u, bm, cm, a)


def _even_out_kernel(x_ref, of_ref, ob_ref, bonus_ref, g_ref, ys_ref, u_ref, m5_ref,
                     ones_ref, gng_ref, gnb_ref, dskip_ref, wglu_ref, bglu_ref, wo1_ref, wo2_ref, o_ref):
    ones = ones_ref[...]
    inv = 1.0 / HEAD_DIM
    o = of_ref[0] + ob_ref[0]
    mean = _group_sum(o, ones) * inv
    oc = o - mean
    var = _group_sum(oc * oc, ones) * inv
    y1 = (oc * lax.rsqrt(var + GN_EPS) * gng_ref[...] + gnb_ref[...] + bonus_ref[0]) * g_ref[0]
    y = ys_ref[0] + dskip_ref[...] * u_ref[0]
    y = jax.nn.gelu(y)
    y2 = y * _sigmoid(_dot(y.astype(BF16), wglu_ref[...]) + bglu_ref[...])
    out = _dot(y1.astype(BF16), wo1_ref[...]) + _dot(y2.astype(BF16), wo2_ref[...])
    o_ref[0] = x_ref[0] + m5_ref[0] * out


def even_out(x, off, o_f, o_b, bonus, g, ys, u, m5, consts, tm):
    bsz, t, d = x.shape
    w = o_f.shape[2]
    tokd = pl.BlockSpec((1, tm, d), lambda b, i: (b, i, 0))
    tokw = pl.BlockSpec((1, tm, w), lambda b, i: (b, i + off, 0))
    toks = pl.BlockSpec((1, tm, u.shape[2]), lambda b, i: (b, i + off, 0))
    full = lambda a: pl.BlockSpec(a.shape, lambda b, i: (0,) * a.ndim)
    return pl.pallas_call(
        _even_out_kernel,
        grid=(bsz, t // tm),
        in_specs=[tokd, tokw, tokw, tokw, tokw, toks, toks,
                  pl.BlockSpec((1, 1, d), lambda b, i: (b, 0, 0))] + [full(a) for a in consts],
        out_specs=tokd,
        out_shape=jax.ShapeDtypeStruct((bsz, t, d), F32),
        compiler_params=_cparams(("parallel", "parallel")),
        name="even_out",
    )(x, o_f, o_b, bonus, g, ys, u, m5, *consts)


def _hy_in_kernel(x_ref, xp_ref, xn_ref, gain_ref, shift_ref, scale_ref, w_ref, cw_ref, cb_ref,
                  z_ref, zb_ref, g1_ref, g2_ref, *, nt, c):
    i = pl.program_id(1)
    gain, shift, scale = gain_ref[...], shift_ref[0], scale_ref[0]
    tm = x_ref.shape[1]
    halo = xp_ref.shape[1]
    h = jnp.concatenate([_rms_mod(xr[0], gain, shift, scale) for xr in (xp_ref, x_ref, xn_ref)], axis=0)
    h = h.astype(BF16)
    row = lax.broadcasted_iota(jnp.int32, (tm, c), 0)
    outs = (z_ref, g1_ref, g2_ref)
    for part in range(3):
        wp = w_ref[:, part * c:(part + 1) * c]
        p_ext = _dot(h, wp)
        p = p_ext[halo:halo + tm]
        pp = jnp.where(i > 0, p_ext[halo - 1:halo], 0.0)
        pn = jnp.where(i < nt - 1, p_ext[halo + tm:halo + tm + 1], 0.0)
        pm1 = jnp.where(row == 0, pp, pltpu.roll(p, 1, 0))
        pp1 = jnp.where(row == tm - 1, pn, pltpu.roll(p, tm - 1, 0))
        cw = cw_ref[:, part * c:(part + 1) * c]
        q = cw[0:1] * pm1 + cw[1:2] * p + cw[2:3] * pp1 + cb_ref[:, part * c:(part + 1) * c]
        outs[part][0] = q
        if part == 0:
            zb_ref[0] = q.astype(BF16)


def hyena_in(x, gain, shift, scale, w, conv_w, conv_b):
    bsz, t, d = x.shape
    c = w.shape[1] // 3
    tm = min(512, t)
    nt = t // tm
    hb = tm // 8
    nhb = t // 8
    vec = pl.BlockSpec((1, 1, d), lambda b, i: (b, 0, 0))
    tok = pl.BlockSpec((1, tm, c), lambda b, i: (b, i, 0))
    sds = jax.ShapeDtypeStruct((bsz, t, c), F32)
    return pl.pallas_call(
        functools.partial(_hy_in_kernel, nt=nt, c=c),
        grid=(bsz, nt),
        in_specs=[pl.BlockSpec((1, tm, d), lambda b, i: (b, i, 0)),
                  pl.BlockSpec((1, 8, d), lambda b, i: (b, jnp.maximum(i * hb - 1, 0), 0)),
                  pl.BlockSpec((1, 8, d), lambda b, i: (b, jnp.minimum((i + 1) * hb, nhb - 1), 0)),
                  pl.BlockSpec((1, d), lambda b, i: (0, 0)),
                  vec, vec,
                  pl.BlockSpec(w.shape, lambda b, i: (0, 0)),
                  pl.BlockSpec(conv_w.shape, lambda b, i: (0, 0)),
                  pl.BlockSpec((1, 3 * c), lambda b, i: (0, 0))],
        out_specs=[tok, tok, tok, tok],
        out_shape=[sds, jax.ShapeDtypeStruct((bsz, t, c), BF16), sds, sds],
        compiler_params=_cparams(("parallel", "parallel")),
        name="hyena_in",
    )(x, x, x, gain.reshape(1, d), shift, scale, w, conv_w, conv_b.reshape(1, 3 * c))


def _pack_complex(re, im):
    bits = lambda x: lax.bitcast_convert_type(x.astype(BF16).astype(F32), jnp.uint32)
    return lax.shift_right_logical(bits(re), jnp.uint32(16)) | bits(im)


def _unpack_complex(w):
    re = lax.bitcast_convert_type(lax.shift_left(w, jnp.uint32(16)), F32)
    im = lax.bitcast_convert_type(w & jnp.uint32(0xFFFF0000), F32)
    return jnp.concatenate([re, im], axis=0).astype(BF16)


def _fft_conv_kernel(z_ref, kf_ref, g_ref, w1_ref, g2i_ref, f1c_ref, twt_ref, o_ref, a_ref, t_ref,
                     *, n1, k1c):
    s = pl.program_id(2)
    n2 = FFT_N2
    nd = n1 // 2
    slab = n1 // k1c

    @pl.when(s == 0)
    def _():
        w1 = w1_ref[...]

        def body(j, carry):
            x = jnp.concatenate([z_ref[0, j], z_ref[1, j]], axis=0)
            a = _dot(w1, x)
            a_ref[pl.ds(pl.multiple_of(j * n1, n1), n1), :] = _pack_complex(a[:n1], a[n1:])
            return carry

        lax.fori_loop(0, n2, body, 0, unroll=16)

    @pl.when((s > 0) & (s <= k1c))
    def _():
        g2i = g2i_ref[...]
        base = (s - 1) * slab

        def body(kk, carry):
            k1 = base + kk
            ak = _unpack_complex(a_ref[pl.ds(k1, n2, stride=n1), :])
            x = _dot(g_ref[kk], ak)
            xr, xi = x[:n2], x[n2:]
            kr, ki = kf_ref[0, kk], kf_ref[1, kk]
            pr = xr * kr - xi * ki
            pi = xr * ki + xi * kr
            tt = _dot(g2i, jnp.concatenate([pr, pi], axis=0).astype(BF16))
            t_ref[pl.ds(pl.multiple_of(k1 * n2, n2), n2), :] = _pack_complex(tt[:n2], tt[n2:])
            return carry

        lax.fori_loop(0, slab, body, 0, unroll=16)

    @pl.when(s == k1c + 1)
    def _():
        f1r, f1i = f1c_ref[0], f1c_ref[1]

        def body(j, carry):
            twr = twt_ref[0, pl.ds(j, 1), :]
            twi = twt_ref[1, pl.ds(j, 1), :]
            wr = f1r * twr + f1i * twi
            wi = f1i * twr - f1r * twi
            w3 = jnp.concatenate([jnp.concatenate([wr, -wi], axis=1),
                                  jnp.concatenate([wi, wr], axis=1)], axis=0).astype(BF16)
            tn = _unpack_complex(t_ref[pl.ds(j, n1, stride=n2), :])
            y = _dot(w3, tn)
            o_ref[0, j] = y[:nd]
            o_ref[1, j] = y[nd:]
            return carry

        lax.fori_loop(0, n2, body, 0, unroll=16)


def _fft_consts(n1):
    n2 = FFT_N2
    n = n1 * n2
    nd = n1 // 2
    k1 = np.arange(n1)
    f1 = np.exp(-2j * np.pi * np.outer(k1, np.arange(nd)) / n1)
    w1 = np.block([[f1.real, -f1.imag], [f1.imag, f1.real]])
    f2 = np.exp(-2j * np.pi * np.outer(np.arange(n2), np.arange(n2)) / n2)
    f2c = np.conj(f2)
    g2i = np.block([[f2c.real, -f2c.imag], [f2c.imag, f2c.real]])
    f1c = np.exp(2j * np.pi * np.outer(np.arange(nd), k1) / n1) / n
    tw = np.exp(-2j * np.pi * np.outer(k1, np.arange(n2)) / n)
    cplx = lambda m: jnp.asarray(np.stack([m.real, m.imag]), F32)
    (f2r, f2i), (twr, twi) = cplx(f2), cplx(tw)
    gr = f2r[None] * twr[:, None, :] - f2i[None] * twi[:, None, :]
    gi = f2r[None] * twi[:, None, :] + f2i[None] * twr[:, None, :]
    gtw = jnp.concatenate([jnp.concatenate([gr, -gi], axis=2), jnp.concatenate([gi, gr], axis=2)], axis=1)
    return (gtw.astype(BF16), jnp.asarray(w1, BF16), jnp.asarray(g2i, BF16), cplx(f1c), cplx(tw.T))


def fft_conv(zt, kf, order):
    bsz, n2, nd, c = zt.shape
    n1 = 2 * nd
    cb = LANES
    k1c = 8 if n1 % 8 == 0 and n1 >= 64 else 2
    slab = n1 // k1c
    gtw, *consts = _fft_consts(n1)
    full = lambda a: pl.BlockSpec(a.shape, lambda j, p, s: (0,) * a.ndim)
    blk = pl.BlockSpec((2, n2, nd, cb), lambda j, p, s: (p, 0, 0, j))
    chunk = lambda j, p, s: jnp.clip(s - 1, 0, k1c - 1)
    return pl.pallas_call(
        functools.partial(_fft_conv_kernel, n1=n1, k1c=k1c),
        grid=(c // cb, bsz // 2, k1c + 2),
        in_specs=[blk,
                  pl.BlockSpec((None, 2, slab, n2, cb), lambda j, p, s: (order, 0, chunk(j, p, s), 0, j)),
                  pl.BlockSpec((slab, 2 * n2, 2 * n2), lambda j, p, s: (chunk(j, p, s), 0, 0))]
                 + [full(a) for a in consts],
        out_specs=blk,
        out_shape=jax.ShapeDtypeStruct(zt.shape, F32),
        scratch_shapes=[pltpu.VMEM((n2 * n1, cb), jnp.uint32), pltpu.VMEM((n1 * n2, cb), jnp.uint32)],
        compiler_params=_cparams(("parallel", "parallel", "arbitrary")),
        name="fft_conv",
    )(zt, kf, gtw, *consts)


def _fft_fwd_kernel(z_ref, scale_ref, g_ref, w1_ref, o_ref, a_ref, *, n1, k1c):
    s = pl.program_id(2)
    n2 = FFT_N2
    slab = n1 // k1c
    cb = o_ref.shape[4]

    @pl.when(s == 0)
    def _():
        w1 = w1_ref[...]

        def body(j, carry):
            a = _dot(w1, jnp.concatenate([z_ref[0, j], z_ref[1, j]], axis=1))
            packed = _pack_complex(a[:n1], a[n1:])
            rows = pl.ds(pl.multiple_of(j * n1, n1), n1)
            a_ref[0, rows, :] = packed[:, :cb]
            a_ref[1, rows, :] = packed[:, cb:]
            return carry

        lax.fori_loop(0, n2, body, 0, unroll=16)

    @pl.when(s > 0)
    def _():
        base = (s - 1) * slab

        def body(kk, carry):
            k1 = base + kk
            rows = pl.ds(k1, n2, stride=n1)
            ak = jnp.concatenate([_unpack_complex(a_ref[0, rows, :]), _unpack_complex(a_ref[1, rows, :])], axis=1)
            x = _dot(g_ref[kk], ak)
            pos, neg = x[:, :cb], x[:, cb:]
            scale = scale_ref[0]
            o_ref[0, 0, kk] = (pos[:n2] + neg[:n2]) * scale
            o_ref[0, 1, kk] = (pos[n2:] - neg[n2:]) * scale
            return carry

        lax.fori_loop(0, slab, body, 0, unroll=16)


def filter_spectrum(taps, scale):
    orders, _, n2, nd, c = taps.shape
    n1 = 2 * nd
    cb = LANES
    k1c = 8 if n1 % 8 == 0 and n1 >= 64 else 2
    slab = n1 // k1c
    gtw, w1 = _fft_consts(n1)[:2]
    w1 = w1[:, :nd]
    full = lambda a: pl.BlockSpec(a.shape, lambda j, r, s: (0,) * a.ndim)
    return pl.pallas_call(
        functools.partial(_fft_fwd_kernel, n1=n1, k1c=k1c),
        grid=(c // cb, orders, k1c + 1),
        in_specs=[pl.BlockSpec((None, 2, n2, nd, cb), lambda j, r, s: (r, 0, 0, 0, j)),
                  pl.BlockSpec((1, 1, cb), lambda j, r, s: (r, 0, j)),
                  pl.BlockSpec((slab, 2 * n2, 2 * n2), lambda j, r, s: (jnp.maximum(s - 1, 0), 0, 0)),
                  full(w1)],
        out_specs=pl.BlockSpec((1, 2, slab, n2, cb), lambda j, r, s: (r, 0, jnp.maximum(s - 1, 0), 0, j)),
        out_shape=jax.ShapeDtypeStruct((orders, 2, n1, n2, c), F32),
        scratch_shapes=[pltpu.VMEM((2, n2 * n1, cb), jnp.uint32)],
        compiler_params=_cparams(("parallel", "parallel", "arbitrary")),
        name="fft_forward",
    )(taps, scale, gtw, w1)


def _hy_filter_kernel(feat_ref, fw1_ref, fb1_ref, fw2_ref, fb2_ref, fw3_ref, fb3_ref, fw4_ref, freq_ref,
                      delta_ref, z_ref, ss_ref):
    @pl.when(pl.program_id(0) == 0)
    def _():
        ss_ref[...] = jnp.zeros_like(ss_ref)

    feats = feat_ref[...]
    t = feats[:, 0:1]
    fr = freq_ref[...]
    h = jnp.sin(fr * (_dot_exact(feats, fw1_ref[...]) + fb1_ref[...]))
    h = jnp.sin(fr * (_dot_exact(h, fw2_ref[...]) + fb2_ref[...]))
    h = jnp.sin(fr * (_dot_exact(h, fw3_ref[...]) + fb3_ref[...])).astype(BF16)
    decay = jnp.exp(-t * delta_ref[...])
    width = delta_ref.shape[1]
    for r in range(z_ref.shape[0]):
        filt = _dot(h, fw4_ref[:, r * width:(r + 1) * width]) * decay
        if r % 2 == 1:
            filt = jnp.where(t == 0.0, 0.0, filt)
        z_ref[r] = filt.astype(BF16)
        ss_ref[r] += jnp.sum(filt * filt, axis=0, keepdims=True)


def hyena_filter(t, fw1, fb1, fw2, fb2, fw3, fb3, fw4, freq, width):
    nd = t // FFT_N2
    pos = (jnp.arange(FFT_N2, dtype=F32)[:, None] + FFT_N2 * jnp.arange(nd, dtype=F32)[None, :]).reshape(-1, 1)
    tt = pos / max(t - 1, 1)
    ang = 2 * math.pi * pos / t
    nb = (HY_EMB - 1) // 2
    bands = jnp.linspace(1e-4, nb - 1, nb, dtype=F32)[None]
    feats = jnp.concatenate([tt, jnp.cos(bands * ang), -jnp.sin(bands * ang),
                             jnp.zeros((t, LANES - HY_EMB), F32)], axis=-1)
    deltas = jnp.abs(jnp.linspace(HY_MIN_DECAY, HY_MAX_DECAY, width, dtype=F32)).reshape(1, width)
    rows = 2 * HY_ORDER
    tm = min(512, t)
    hid = fw1.shape[1]
    full = lambda a: pl.BlockSpec(a.shape, lambda i: (0,) * a.ndim)
    fw1p = jnp.concatenate([fw1, jnp.zeros((LANES - HY_EMB, hid), F32)], axis=0)
    vecs = [fw1p, fb1.reshape(1, hid), fw2, fb2.reshape(1, hid), fw3, fb3.reshape(1, hid),
            fw4.astype(BF16), freq.reshape(1, hid), deltas]
    return pl.pallas_call(
        _hy_filter_kernel,
        grid=(t // tm,),
        in_specs=[pl.BlockSpec((tm, LANES), lambda i: (i, 0))] + [full(a) for a in vecs],
        out_specs=[pl.BlockSpec((rows, tm, width), lambda i: (0, i, 0)),
                   pl.BlockSpec((rows, 1, width), lambda i: (0, 0, 0))],
        out_shape=[jax.ShapeDtypeStruct((rows, t, width), BF16), jax.ShapeDtypeStruct((rows, 1, width), F32)],
        compiler_params=_cparams(("arbitrary",)),
        name="hyena_filter",
    )(feats, *vecs)


def hyena_filter_spectrum_fft(t, fw1, fb1, fw2, fb2, fw3, fb3, fw4, freq, width):
    taps, ss = hyena_filter(t, fw1, fb1, fw2, fb2, fw3, fb3, fw4, freq, width)
    nd = t // FFT_N2
    scale = lax.rsqrt(ss.reshape(HY_ORDER, 2, width).sum(axis=1, keepdims=True) + 1e-6)
    return filter_spectrum(taps.reshape(HY_ORDER, 2, FFT_N2, nd, width), scale)


def _dft_conv_kernel(z_ref, kf_ref, fw_ref, iv_ref, o_ref, *, t):
    x = jnp.concatenate([z_ref[0], z_ref[1]], axis=0)
    spec = _dot(fw_ref[...], x)
    xr, xi = spec[:2 * t], spec[2 * t:]
    kr, ki = kf_ref[0], kf_ref[1]
    pr = xr * kr - xi * ki
    pi = xr * ki + xi * kr
    y = _dot(iv_ref[...], jnp.concatenate([pr, pi], axis=0).astype(BF16))
    o_ref[0] = y[:t]
    o_ref[1] = y[t:]


def dft_conv(zb, kf):
    bsz, t, c = zb.shape
    n = 2 * t
    f = np.exp(-2j * np.pi * np.outer(np.arange(n), np.arange(t)) / n)
    fw = np.block([[f.real, -f.imag], [f.imag, f.real]])
    fi = np.exp(2j * np.pi * np.outer(np.arange(t), np.arange(n)) / n) / n
    iv = np.block([[fi.real, -fi.imag], [fi.imag, fi.real]])
    fw, iv = jnp.asarray(fw, BF16), jnp.asarray(iv, BF16)
    cb = LANES
    blk = pl.BlockSpec((2, t, cb), lambda j, p: (p, 0, j))
    return pl.pallas_call(
        functools.partial(_dft_conv_kernel, t=t),
        grid=(c // cb, bsz // 2),
        in_specs=[blk, pl.BlockSpec((2, n, cb), lambda j, p: (0, 0, j)),
                  pl.BlockSpec(fw.shape, lambda j, p: (0, 0)), pl.BlockSpec(iv.shape, lambda j, p: (0, 0))],
        out_specs=blk,
        out_shape=jax.ShapeDtypeStruct(zb.shape, F32),
        compiler_params=_cparams(("parallel", "parallel")),
        name="dft_conv",
    )(zb, kf, fw, iv)


def _conv_rows(conv_ref):
    if len(conv_ref.shape) == 3:
        return conv_ref[0]
    return jnp.concatenate([conv_ref[0, :, k, :] for k in range(conv_ref.shape[2])], axis=0)


def _conv_spec(conv, tm, c):
    if conv.ndim == 3:
        return pl.BlockSpec((1, tm, c), lambda b, i: (b, i, 0))
    return pl.BlockSpec((1, FFT_N2, tm // FFT_N2, c), lambda b, i: (b, 0, i, 0))


def _hy_gate_kernel(conv_ref, z_ref, gate_ref, bias_ref, o_ref, ob_ref):
    y = gate_ref[0] * (_conv_rows(conv_ref) + bias_ref[...] * z_ref[0])
    o_ref[0] = y
    ob_ref[0] = y.astype(BF16)


def hyena_gate(conv, z, gate, bias):
    bsz, t, c = z.shape
    tm = min(SUBLANES * FFT_N2, t)
    tok = pl.BlockSpec((1, tm, c), lambda b, i: (b, i, 0))
    return pl.pallas_call(
        _hy_gate_kernel,
        grid=(bsz, t // tm),
        in_specs=[_conv_spec(conv, tm, c), tok, tok, pl.BlockSpec((1, c), lambda b, i: (0, 0))],
        out_specs=[tok, tok],
        out_shape=[jax.ShapeDtypeStruct(z.shape, F32), jax.ShapeDtypeStruct(z.shape, BF16)],
        compiler_params=_cparams(("parallel", "parallel")),
        name="hyena_gate",
    )(conv, z, gate, bias.reshape(1, c))


def _hy_out_kernel(x_ref, conv_ref, z_ref, gate_ref, bias_ref, w_ref, m5_ref, o_ref):
    y = gate_ref[0] * (_conv_rows(conv_ref) + bias_ref[...] * z_ref[0])
    o_ref[0] = x_ref[0] + m5_ref[0] * _dot(y.astype(BF16), w_ref[...])


def hyena_out(x, conv, z, gate, bias, w, m5):
    bsz, t, d = x.shape
    c = z.shape[2]
    tm = min(SUBLANES * FFT_N2, t)
    tokd = pl.BlockSpec((1, tm, d), lambda b, i: (b, i, 0))
    tokc = pl.BlockSpec((1, tm, c), lambda b, i: (b, i, 0))
    return pl.pallas_call(
        _hy_out_kernel,
        grid=(bsz, t // tm),
        in_specs=[tokd, _conv_spec(conv, tm, c), tokc, tokc, pl.BlockSpec((1, c), lambda b, i: (0, 0)),
                  pl.BlockSpec(w.shape, lambda b, i: (0, 0)), pl.BlockSpec((1, 1, d), lambda b, i: (b, 0, 0))],
        out_specs=tokd,
        out_shape=jax.ShapeDtypeStruct(x.shape, F32),
        compiler_params=_cparams(("parallel", "parallel")),
        name="hyena_out",
    )(x, conv, z, gate, bias.reshape(1, c), w, m5)


def _hyena_filter_spectrum(t, fw1, fb1, fw2, fb2, fw3, fb3, fw4, freq, width):
    pos = jnp.arange(t, dtype=F32)[:, None]
    tt = pos / max(t - 1, 1)
    ang = 2 * math.pi * pos / t
    nb = (HY_EMB - 1) // 2
    bands = jnp.linspace(1e-4, nb - 1, nb, dtype=F32)[None]
    feats = jnp.concatenate([tt, jnp.cos(bands * ang), -jnp.sin(bands * ang)], axis=-1)
    hdn = jnp.sin(freq * (feats @ fw1 + fb1))
    hdn = jnp.sin(freq * (hdn @ fw2 + fb2))
    hdn = jnp.sin(freq * (hdn @ fw3 + fb3))
    filt = (hdn @ fw4).reshape(t, HY_ORDER, 2, width)
    deltas = jnp.abs(jnp.linspace(HY_MIN_DECAY, HY_MAX_DECAY, width, dtype=F32))
    filt = filt * jnp.exp(-tt[:, :, None, None] * deltas)
    fwd, bwd = filt[:, :, 0], filt[:, :, 1]
    kern = jnp.concatenate([fwd, jnp.zeros_like(fwd[:1]), bwd[:0:-1]], axis=0)
    kern = kern * lax.rsqrt(jnp.sum(kern * kern, axis=0, keepdims=True) + 1e-6)
    spec = jnp.fft.fft(kern, axis=0)
    return jnp.stack([jnp.real(spec), jnp.imag(spec)], axis=1).transpose(2, 1, 0, 3).astype(F32)


def _block_diag_pair(m):
    z = jnp.zeros_like(m[0])
    return jnp.concatenate([jnp.concatenate([m[0], z], axis=1), jnp.concatenate([z, m[1]], axis=1)], axis=0)


def _even_mixer(lat, cx, ml, mc, gain, prm, ctx_out):
    (w_in, mu, w0, w_up, a0, a_up, g_up, k_k, k_a, r_k, gn_g, gn_b,
     lam_re, lam_im, log_dt, b_re, b_im, c_re, c_im, d_skip, w_glu, b_glu, w_out) = prm
    bsz, t_lat, d = lat.shape
    t_ctx = cx.shape[1]
    width = k_k.shape[0]
    heads = width // HEAD_DIM
    n_cols = mu.shape[0]
    s5w = d_skip.shape[0]
    ttot = t_ctx + t_lat
    nch = bsz * heads
    tm = t_ctx
    assert tm % GRID_W == 0 and t_lat % tm == 0 and 4 * nch == LANES and 2 * bsz == SUBLANES

    p_all, u_all = mod_matmul_stream(cx, lat, gain, mc, ml, w_in.astype(BF16), tm, n_cols)

    head_of = np.arange(width) // HEAD_DIM
    ones = jnp.asarray(head_of[:, None] == head_of[None, :], BF16)
    feat_consts = (mu.reshape(1, -1), _block_diag_pair(w_up).astype(BF16), w0.reshape(1, -1),
                   _block_diag_pair(a_up).astype(BF16), a0.reshape(1, -1), g_up.astype(BF16),
                   k_a.reshape(1, -1), r_k.reshape(1, -1), ones)
    v2, g, bonus, r2, k2, w2, a2 = rwkv_features(p_all, n_cols, width, feat_consts, tm)

    def key_major(x):
        x = x.reshape(2, bsz, ttot, heads, 2, HALF_HEAD)
        return x.transpose(2, 5, 0, 4, 1, 3).reshape(ttot, HALF_HEAD, LANES)

    def value_major(x):
        x = x.reshape(2, bsz, ttot, heads, HEAD_DIM)
        return x.transpose(2, 4, 0, 1, 3).reshape(ttot, HEAD_DIM, 2 * nch)

    def key_const(x):
        x = x.reshape(heads, 2, HALF_HEAD).transpose(2, 1, 0)[:, None, :, None, :]
        return jnp.broadcast_to(x, (HALF_HEAD, 2, 2, bsz, heads)).reshape(HALF_HEAD, LANES)

    o_f, o_b = rwkv_scan(key_major(r2), key_major(k2), key_major(w2), key_major(a2), value_major(v2),
                         key_const(k_a), key_const(k_k), t_ctx)

    def token_major(o, lane0):
        o = o[..., lane0:lane0 + nch].reshape(ttot, HEAD_DIM, bsz, heads)
        return o.transpose(2, 0, 3, 1).reshape(bsz, ttot, width)

    o_f, o_b = token_major(o_f, 0), token_major(o_b, LANES // 2)

    ng = lam_re.shape[1]
    gps = ng // S5_SUPER
    lam = lax.complex(lam_re, lam_im)
    dt = jnp.exp(log_dt)[..., None]
    a_bar = jnp.exp(lam * dt)
    b_bar = ((a_bar - 1) / lam)[..., None] * lax.complex(b_re, b_im)
    eye = jnp.eye(gps, dtype=F32)

    def b_mat(x):
        x = x.reshape(2, S5_SUPER, gps, S5_STATE, S5_GROUP)
        m = jnp.einsum('dsgph,gk->sdghkp', x, eye)
        return m.reshape(S5_SUPER, 2 * gps * S5_GROUP, gps * S5_STATE)

    def c_mat(x):
        x = x.reshape(2, S5_SUPER, gps, S5_GROUP, S5_STATE)
        m = jnp.einsum('dsghp,gk->skpdgh', x, eye)
        return m.reshape(S5_SUPER, gps * S5_STATE, 2 * gps * S5_GROUP)

    bm = jnp.concatenate([b_mat(jnp.real(b_bar)), b_mat(jnp.imag(b_bar))], axis=2).astype(BF16)
    cm = jnp.concatenate([c_mat(c_re), -c_mat(c_im)], axis=1).astype(BF16)

    def a_rows(x):
        x = x.reshape(2, 1, S5_SUPER, gps * S5_STATE)
        return jnp.broadcast_to(x, (2, bsz, S5_SUPER, gps * S5_STATE)).reshape(2 * bsz, S5_SUPER, -1)

    a_arr = jnp.concatenate([a_rows(jnp.real(a_bar)), a_rows(jnp.imag(a_bar))], axis=2)
    a_arr = a_arr.reshape(1, 2 * bsz, -1)

    y_f, y_b = s5_scan(u_all.transpose(1, 0, 2), bm, cm, a_arr, t_ctx)
    ys = (y_f[:, :bsz] + y_b[:, bsz:]).transpose(1, 0, 2)

    wo = w_out.astype(BF16)
    out_consts = (ones, gn_g.reshape(1, -1), gn_b.reshape(1, -1), d_skip.reshape(1, -1),
                  w_glu.astype(BF16), b_glu.reshape(1, -1), wo[:width], wo[width:])
    lat = even_out(lat, 1, o_f, o_b, bonus, g, ys, u_all, ml[5], out_consts, tm)
    if ctx_out:
        cx = even_out(cx, 0, o_f, o_b, bonus, g, ys, u_all, mc[5], out_consts, tm)
    return lat, cx


def _hyena_mixer(x, gain, m, prm):
    (w_in, conv_w, conv_b, fw1, fb1, fw2, fb2, fw3, fb3, fw4, freq, bias_d, w_out) = prm
    bsz, t, d = x.shape
    c = w_out.shape[0]
    use_fft = t % (FFT_N2 * 2) == 0 and t >= 4 * FFT_N2
    if use_fft:
        kf = hyena_filter_spectrum_fft(t, fw1, fb1, fw2, fb2, fw3, fb3, fw4, freq, c)
    else:
        kf = _hyena_filter_spectrum(t, fw1, fb1, fw2, fb2, fw3, fb3, fw4, freq, c)
    z, zb, g1, g2 = hyena_in(x, gain, m[3], m[4], w_in.astype(BF16), conv_w, conv_b)
    gates = (g1, g2)
    for n in range(HY_ORDER):
        if use_fft:
            nd = t // FFT_N2
            zt = zb.reshape(bsz, nd, FFT_N2, c).transpose(0, 2, 1, 3)
            conv = fft_conv(zt, kf, n)
        else:
            conv = dft_conv(zb, kf[n])
        if n < HY_ORDER - 1:
            z, zb = hyena_gate(conv, z, gates[n], bias_d[n])
        else:
            return hyena_out(x, conv, z, gates[n], bias_d[n], w_out.astype(BF16), m[5])


def kernel(x, c, ctx, c_ctx, norm_g, ada_w, ada_b, ffn_wg, ffn_wu, ffn_wd, final_g, ev_w_in, ev_mu, ev_w0, ev_w_up, ev_a0, ev_a_up, ev_g_up, ev_k_k, ev_k_a, ev_r_k, ev_gn_g, ev_gn_b, ev_lam_re, ev_lam_im, ev_log_dt, ev_b_re, ev_b_im, ev_c_re, ev_c_im, ev_d, ev_w_glu, ev_b_glu, ev_w_out, od_w_in, od_conv_w, od_conv_b, od_fw1, od_fb1, od_fw2, od_fb2, od_fw3, od_fb3, od_fw4, od_freq, od_bias, od_w_out):
    depth = norm_g.shape[0]
    bsz, _, d = x.shape
    n_even = (depth + 1) // 2
    last_ctx = 2 * (n_even - 1)

    cond8 = jnp.concatenate([c, c_ctx[None], jnp.zeros((8 - bsz - 1, d), F32)], axis=0)
    mods = ada_mods_all(cond8, ada_w, ada_b)

    wg, wu, wd = ffn_wg.astype(BF16), ffn_wu.astype(BF16), ffn_wd.astype(BF16)
    lat, cx = x, ctx
    for l in range(depth):
        run_ctx = l <= last_ctx
        ctx_out = l < last_ctx
        i = l // 2
        ml = [mods[l, :bsz, None, k * d:(k + 1) * d] for k in range(N_MOD)]
        mc = [jnp.broadcast_to(mods[l, bsz:bsz + 1, None, k * d:(k + 1) * d], (bsz, 1, d))
              for k in range(N_MOD)]
        lat = ffn_half(lat, norm_g[l, 0], ml[0], ml[1], ml[2], wg[l, 0], wu[l, 0], wd[l, 0])
        if run_ctx:
            cx = ffn_half(cx, norm_g[l, 0], mc[0], mc[1], mc[2], wg[l, 0], wu[l, 0], wd[l, 0])
        if l % 2 == 0:
            prm = (ev_w_in[i], ev_mu[i], ev_w0[i], ev_w_up[i], ev_a0[i], ev_a_up[i], ev_g_up[i],
                   ev_k_k[i], ev_k_a[i], ev_r_k[i], ev_gn_g[i], ev_gn_b[i],
                   ev_lam_re[i], ev_lam_im[i], ev_log_dt[i], ev_b_re[i], ev_b_im[i], ev_c_re[i], ev_c_im[i],
                   ev_d[i], ev_w_glu[i], ev_b_glu[i], ev_w_out[i])
            lat, cx = _even_mixer(lat, cx, ml, mc, norm_g[l, 1], prm, ctx_out)
        else:
            prm = (od_w_in[i], od_conv_w[i], od_conv_b[i], od_fw1[i], od_fb1[i], od_fw2[i], od_fb2[i],
                   od_fw3[i], od_fb3[i], od_fw4[i], od_freq[i], od_bias[i], od_w_out[i])
            lat = _hyena_mixer(lat, norm_g[l, 1], ml, prm)
            if ctx_out:
                cx = _hyena_mixer(cx, norm_g[l, 1], mc, prm)
        fin = final_g if l == depth - 1 else None
        lat = ffn_half(lat, norm_g[l, 2], ml[6], ml[7], ml[8], wg[l, 1], wu[l, 1], wd[l, 1], fin)
        if ctx_out:
            cx = ffn_half(cx, norm_g[l, 2], mc[6], mc[7], mc[8], wg[l, 1], wu[l, 1], wd[l, 1])
    return lat
```

```python
import functools
import math

import numpy as np
import jax
import jax.numpy as jnp
from jax import lax
from jax.experimental import pallas as pl
from jax.experimental.pallas import tpu as pltpu

F32 = jnp.float32
BF16 = jnp.bfloat16
HIGHEST = lax.Precision.HIGHEST

N_MOD = 9
NORM_EPS = 1e-6
GN_EPS = 64e-5
GRID_W = 64
HEAD_DIM = 64
HALF_HEAD = HEAD_DIM // 2
S5_GROUP = 16
S5_STATE = 64
S5_SUPER = 4
HY_ORDER = 2
HY_EMB = 33
HY_MIN_DECAY = math.log(1e-2) / 1.5
HY_MAX_DECAY = math.log(1e-2) / 0.3
LANES = 128
SUBLANES = 8
MXU_DIM = 256
FFT_N2 = 128
VMEM_LIMIT = 56 * 1024 * 1024


def _cparams(sem, vmem=VMEM_LIMIT):
    return pltpu.CompilerParams(dimension_semantics=sem, vmem_limit_bytes=vmem)


def _dot(a, b):
    return jnp.dot(a, b, preferred_element_type=F32)


def _dot_exact(a, b):
    return jnp.dot(a, b, preferred_element_type=F32, precision=HIGHEST)


def _group_sum(a, ones):
    hi = a.astype(BF16)
    lo = (a - hi.astype(F32)).astype(BF16)
    return _dot(hi, ones) + _dot(lo, ones)


def _rms_mod(x, gain, shift, scale):
    ms = jnp.mean(x * x, axis=-1, keepdims=True)
    return x * lax.rsqrt(ms + NORM_EPS) * gain * (1.0 + scale) + shift


def _sigmoid(x):
    return 1.0 / (1.0 + jnp.exp(-x))


def _silu(x):
    return x * _sigmoid(x)


def _ada_kernel(c_ref, w_ref, b_ref, o_ref):
    s = _silu(c_ref[...])
    o_ref[0] = _dot(s.astype(BF16), w_ref[0].astype(BF16)) + b_ref[0]


def ada_mods_all(cond8, ada_w, ada_b):
    depth, d, n = ada_w.shape
    tn = n // 8
    return pl.pallas_call(
        _ada_kernel,
        grid=(depth, n // tn),
        in_specs=[pl.BlockSpec((8, d), lambda l, j: (0, 0)),
                  pl.BlockSpec((1, d, tn), lambda l, j: (l, 0, j)),
                  pl.BlockSpec((1, 1, tn), lambda l, j: (l, 0, j))],
        out_specs=pl.BlockSpec((1, 8, tn), lambda l, j: (l, 0, j)),
        out_shape=jax.ShapeDtypeStruct((depth, 8, n), F32),
        compiler_params=_cparams(("parallel", "parallel")),
        name="ada_mods",
    )(cond8, ada_w, ada_b.reshape(depth, 1, n))


def _ffn_kernel(x_ref, gain_ref, shift_ref, scale_ref, gate_ref, wg_ref, wu_ref, wd_ref, fg_ref,
                o_ref, *, final_norm, chunks):
    x = x_ref[0]
    h = _rms_mod(x, gain_ref[...], shift_ref[0], scale_ref[0]).astype(BF16)
    acc = None
    for lo, hi in chunks:
        g = _dot(h, wg_ref[:, lo:hi])
        u = _dot(h, wu_ref[:, lo:hi])
        part = _dot((_silu(g) * u).astype(BF16), wd_ref[lo:hi, :])
        acc = part if acc is None else acc + part
    y = x + 0.5 * gate_ref[0] * acc
    if final_norm:
        ms = jnp.mean(y * y, axis=-1, keepdims=True)
        y = y * lax.rsqrt(ms + NORM_EPS) * fg_ref[...]
    o_ref[0] = y


def ffn_half(x, gain, shift, scale, gate, wg, wu, wd, final_g=None):
    bsz, t, d = x.shape
    ff = wg.shape[1]
    tm = min(1024, t)
    step = 2 * MXU_DIM
    chunks = tuple((lo, min(lo + step, ff)) for lo in range(0, ff, step))
    fg = jnp.ones((1, d), F32) if final_g is None else final_g.reshape(1, d)
    vec = pl.BlockSpec((1, 1, d), lambda b, i: (b, 0, 0))
    resident = lambda a: pl.BlockSpec(a.shape, lambda b, i: (0, 0), pipeline_mode=pl.Buffered(1))
    return pl.pallas_call(
        functools.partial(_ffn_kernel, final_norm=final_g is not None, chunks=chunks),
        grid=(bsz, t // tm),
        in_specs=[pl.BlockSpec((1, tm, d), lambda b, i: (b, i, 0)),
                  pl.BlockSpec((1, d), lambda b, i: (0, 0)),
                  vec, vec, vec, resident(wg), resident(wu), resident(wd),
                  pl.BlockSpec((1, d), lambda b, i: (0, 0))],
        out_specs=pl.BlockSpec((1, tm, d), lambda b, i: (b, i, 0)),
        out_shape=jax.ShapeDtypeStruct((bsz, t, d), F32),
        compiler_params=_cparams(("parallel", "parallel")),
        name="ffn_half",
    )(x, gain.reshape(1, d), shift, scale, gate, wg, wu, wd, fg)


def _modmm_kernel(cx_ref, lat_ref, gain_ref, shc_ref, scc_ref, shl_ref, scl_ref, w_ref, o_ref, u_ref, *, nc):
    is_ctx = pl.program_id(1) < nc
    x = jnp.where(is_ctx, cx_ref[0], lat_ref[0])
    shift = jnp.where(is_ctx, shc_ref[0], shl_ref[0])
    scale = jnp.where(is_ctx, scc_ref[0], scl_ref[0])
    h = _rms_mod(x, gain_ref[...], shift, scale).astype(BF16)
    n1 = o_ref.shape[2]
    o_ref[0] = _dot(h, w_ref[:, :n1])
    u_ref[0] = _dot(h, w_ref[:, n1:])


def mod_matmul_stream(cx, lat, gain, mc, ml, w, tm, n_first):
    bsz, t_ctx, d = cx.shape
    t_lat = lat.shape[1]
    n = w.shape[1]
    nc = t_ctx // tm
    nt = nc + t_lat // tm
    vec = pl.BlockSpec((1, 1, d), lambda b, i: (b, 0, 0))
    return pl.pallas_call(
        functools.partial(_modmm_kernel, nc=nc),
        grid=(bsz, nt),
        in_specs=[pl.BlockSpec((1, tm, d), lambda b, i: (b, jnp.minimum(i, nc - 1), 0)),
                  pl.BlockSpec((1, tm, d), lambda b, i: (b, jnp.maximum(i - nc, 0), 0)),
                  pl.BlockSpec((1, d), lambda b, i: (0, 0)),
                  vec, vec, vec, vec,
                  pl.BlockSpec((d, n), lambda b, i: (0, 0))],
        out_specs=[pl.BlockSpec((1, tm, n_first), lambda b, i: (b, i, 0)),
                   pl.BlockSpec((1, tm, n - n_first), lambda b, i: (b, i, 0))],
        out_shape=[jax.ShapeDtypeStruct((bsz, t_ctx + t_lat, n_first), F32),
                   jax.ShapeDtypeStruct((bsz, t_ctx + t_lat, n - n_first), F32)],
        compiler_params=_cparams(("parallel", "parallel")),
        name="mod_matmul",
    )(cx, lat, gain.reshape(1, d), mc[3], mc[4], ml[3], ml[4], w)


def _rwkv_feat_kernel(p_ref, pu_ref, pd_ref, mu_ref, wup_ref, w0_ref, aup_ref, a0_ref, gup_ref,
                      ka_ref, rk_ref, ones_ref,
                      v_o, g_o, bonus_o, r_o, k_o, w_o, a_o, *, nt, width):
    i = pl.program_id(1)
    is_ctx = i == 0
    p = p_ref[0]
    tm = p.shape[0]
    row = lax.broadcasted_iota(jnp.int32, p.shape, 0)
    lane = lax.broadcasted_iota(jnp.int32, p.shape, 1) % 4
    prev = pltpu.roll(p, 1, 0)
    nxt = pltpu.roll(p, tm - 1, 0)
    col = jnp.where(is_ctx, row, row % GRID_W)
    last = jnp.where(is_ctx, tm - 1, GRID_W - 1)
    left = jnp.where(col == 0, 0.0, prev)
    right = jnp.where(col == last, 0.0, nxt)
    up_halo = jnp.where(i > 1, pu_ref[0], 0.0)
    dn_halo = jnp.where(i < nt - 1, pd_ref[0], 0.0)
    up = jnp.where(is_ctx, left, jnp.concatenate([up_halo, p[:tm - GRID_W]], axis=0))
    down = jnp.where(is_ctx, right, jnp.concatenate([p[GRID_W:], dn_halo], axis=0))
    shifted = jnp.where(lane == 0, left, jnp.where(lane == 1, right, jnp.where(lane == 2, up, down)))
    q = p + mu_ref[...] * (shifted - p)

    w = width
    r, k, v = q[:, :w], q[:, w:2 * w], q[:, 2 * w:3 * w]
    wd = q[:, 3 * w:3 * w + LANES]
    ad = q[:, 3 * w + LANES:3 * w + 2 * LANES]
    gd = q[:, 3 * w + 2 * LANES:3 * w + 3 * LANES]

    zlin = w0_ref[...] + _dot(jnp.tanh(wd).astype(BF16), wup_ref[...])
    decay = jnp.exp(-math.exp(-0.5) * _sigmoid(zlin))
    a = _sigmoid(a0_ref[...] + _dot(ad.astype(BF16), aup_ref[...]))
    g_o[0] = _dot(_sigmoid(gd).astype(BF16), gup_ref[...])

    ones = ones_ref[...]
    for d in range(2):
        v_o[d, 0] = v
        r_o[d, 0] = r
        k_o[d, 0] = k
    ksum = jnp.zeros_like(k)
    for d in range(2):
        a_d = a[:, d * w:(d + 1) * w]
        ksum = ksum + k * (1.0 + (a_d - 1.0) * ka_ref[...])
        w_o[d, 0] = decay[:, d * w:(d + 1) * w]
        a_o[d, 0] = a_d
    bonus_o[0] = _group_sum(r * ksum * rk_ref[...], ones) * v


def rwkv_features(p, n_cols, width, consts, tm):
    bsz, t, _ = p.shape
    nt = t // tm
    hb = tm // GRID_W
    nhb = t // GRID_W
    w = width
    full = lambda a: pl.BlockSpec(a.shape, lambda b, i: (0,) * a.ndim)
    tok = pl.BlockSpec((1, tm, w), lambda b, i: (b, i, 0))
    tok2 = pl.BlockSpec((2, 1, tm, w), lambda b, i: (0, b, i, 0))
    sds = jax.ShapeDtypeStruct((bsz, t, w), F32)
    sds2 = jax.ShapeDtypeStruct((2, bsz, t, w), F32)
    return pl.pallas_call(
        functools.partial(_rwkv_feat_kernel, nt=nt, width=w),
        grid=(bsz, nt),
        in_specs=[pl.BlockSpec((1, tm, n_cols), lambda b, i: (b, i, 0)),
                  pl.BlockSpec((1, GRID_W, n_cols), lambda b, i: (b, jnp.maximum(i * hb - 1, 0), 0)),
                  pl.BlockSpec((1, GRID_W, n_cols), lambda b, i: (b, jnp.minimum((i + 1) * hb, nhb - 1), 0))]
                 + [full(a) for a in consts],
        out_specs=[tok2, tok, tok, tok2, tok2, tok2, tok2],
        out_shape=[sds2, sds, sds, sds2, sds2, sds2, sds2],
        compiler_params=_cparams(("parallel", "parallel")),
        name="rwkv_features",
    )(p, p, p, *consts)


def _rwkv_scan_kernel(rf, rb, kf, kb, wf, wb, af, ab, vf, vb, ka_ref, kks_ref, of_ref, ob_ref,
                      s_ref, vec_ref, v_ref, c_ref, look_ref, *, tc):
    @pl.when(pl.program_id(0) == 0)
    def _():
        s_ref[...] = jnp.zeros_like(s_ref)
        look_ref[:, tc] = jnp.zeros((look_ref.shape[0],) + look_ref.shape[2:], F32)

    nj, ni = s_ref.shape[0], s_ref.shape[1]
    quarter = LANES // 4

    def fold(x):
        hi = (lax.broadcasted_iota(jnp.int32, x.shape, 1) % (2 * quarter)) >= quarter
        return x + jnp.where(hi, pltpu.roll(x, quarter, 1), pltpu.roll(x, 3 * quarter, 1))

    def hsum(x):
        return jnp.sum(x, axis=0, keepdims=True)

    def jsum(x):
        return fold(jnp.broadcast_to(hsum(x), (SUBLANES, LANES)))[0:1]

    ka = ka_ref[...]
    kks = kks_ref[...]
    fwd_lane = lax.broadcasted_iota(jnp.int32, (nj, LANES), 1) < LANES // 2
    def merged(f, b, t):
        return jnp.where(fwd_lane, f[t], b[tc - 1 - t])

    cum = jnp.ones((nj, LANES), F32)
    for t in range(tc):
        w = merged(wf, wb, t)
        cum = cum * w
        kk = merged(kf, kb, t) * kks
        look_ref[0, t] = kk / jnp.maximum(jnp.sqrt(jsum(kk * kk)), 1e-12)
        look_ref[1, t] = w * merged(rf, rb, t)
        look_ref[2, t] = cum
    for t in range(tc):
        r, k, a = merged(rf, rb, t), merged(kf, kb, t), merged(af, ab, t)
        kk, kk_next, wr_next = look_ref[0, t], look_ref[0, t + 1], look_ref[1, t + 1]
        cum = look_ref[2, t]
        inv = 1.0 / cum
        kka = kk * a
        kd = k * (1.0 + (a - 1.0) * ka)
        vec_ref[0, t] = kka * inv
        vec_ref[1, t] = kd * inv
        vec_ref[2, t] = cum * kk_next
        vec_ref[3, t] = cum * wr_next
        c_ref[t, 0:1, :] = jsum(kka * r)
        c_ref[t, 1:2, :] = jsum(kd * r)
        c_ref[t, 2:3, :] = hsum(kka * kk_next)
        c_ref[t, 3:4, :] = hsum(kd * kk_next)
        c_ref[t, 4:5, :] = hsum(kka * wr_next)
        c_ref[t, 5:6, :] = hsum(kd * wr_next)
        v_ref[t] = jnp.concatenate([vf[t], vb[tc - 1 - t]], axis=-1)

    def state_sums(y1, y2):
        s1 = jnp.zeros((ni, LANES), F32)
        s2 = jnp.zeros((ni, LANES), F32)
        for j in range(nj):
            sj = s_ref[j]
            s1 = s1 + sj * y1(j)
            s2 = s2 + sj * y2(j)
        return s1, s2

    def step(t, carry):
        row = lambda n: (lambda j: vec_ref[n, t, j:j + 1, :])
        a1, a2 = state_sums(row(2), row(3))
        sa, o1 = fold(carry[0]), fold(carry[1])
        v = v_ref[t]
        out = o1 - sa * c_ref[t, 0:1, :] + v * c_ref[t, 1:2, :]
        of_ref[t] = out
        ob_ref[tc - 1 - t] = out
        nxt = (a1 - sa * c_ref[t, 2:3, :] + v * c_ref[t, 3:4, :],
               a2 - sa * c_ref[t, 4:5, :] + v * c_ref[t, 5:6, :])
        for j in range(nj):
            s_ref[j] = s_ref[j] - sa * row(0)(j) + v * row(1)(j)
        return nxt

    first = state_sums(lambda j: look_ref[0, 0, j:j + 1, :], lambda j: look_ref[1, 0, j:j + 1, :])
    lax.fori_loop(0, tc, step, first)
    for j in range(nj):
        s_ref[j] = s_ref[j] * look_ref[2, tc - 1, j:j + 1, :]


def rwkv_scan(r, k, w, a, v, ka, kks, t_ctx):
    ttot, nj, _ = r.shape
    ni, nc = v.shape[1], v.shape[2]
    tc = 64
    ncb, ntb = t_ctx // tc, ttot // tc
    fwd = lambda g: (g, 0, 0)
    bwd = lambda g: (jnp.where(g < ncb, ncb - 1 - g, ntb + ncb - 1 - g), 0, 0)
    jf, jb = pl.BlockSpec((tc, nj, LANES), fwd), pl.BlockSpec((tc, nj, LANES), bwd)
    vf, vb = pl.BlockSpec((tc, ni, nc), fwd), pl.BlockSpec((tc, ni, nc), bwd)
    of, ob = pl.BlockSpec((tc, ni, LANES), fwd), pl.BlockSpec((tc, ni, LANES), bwd)
    osd = jax.ShapeDtypeStruct((ttot, ni, LANES), F32)
    return pl.pallas_call(
        functools.partial(_rwkv_scan_kernel, tc=tc),
        grid=(ntb,),
        in_specs=[jf, jb] * 4 + [vf, vb, pl.BlockSpec(ka.shape, lambda g: (0, 0)),
                                 pl.BlockSpec(kks.shape, lambda g: (0, 0))],
        out_specs=[of, ob],
        out_shape=[osd, osd],
        scratch_shapes=[pltpu.VMEM((nj, ni, LANES), F32), pltpu.VMEM((4, tc, nj, LANES), F32),
                        pltpu.VMEM((tc, ni, LANES), F32), pltpu.VMEM((tc, SUBLANES, LANES), F32),
                        pltpu.VMEM((3, tc + 1, nj, LANES), F32)],
        compiler_params=_cparams(("arbitrary",)),
        name="rwkv_scan",
    )(r, r, k, k, w, w, a, a, v, v, ka, kks)


def _s5_kernel(uf_ref, ub_ref, bm_ref, cm_ref, a_ref, yf_ref, yb_ref, h_ref, bu_ref, ubr_ref, *, tt):
    @pl.when(pl.program_id(0) == 0)
    def _():
        h_ref[...] = jnp.zeros_like(h_ref)

    nsb = bm_ref.shape[0]
    kin = bm_ref.shape[1] // 2
    sw = bm_ref.shape[2]
    hw = sw // 2
    for t in range(tt):
        ubr_ref[t] = ub_ref[tt - 1 - t]
    nothing = jnp.zeros(uf_ref.shape, F32)
    uf = jnp.concatenate([uf_ref[...], nothing], axis=1)
    ub = jnp.concatenate([nothing, ubr_ref[...]], axis=1)
    for sb in range(nsb):
        lhs = jnp.concatenate([uf[:, :, sb * kin:(sb + 1) * kin], ub[:, :, sb * kin:(sb + 1) * kin]], axis=-1)
        lhs = lhs.reshape(tt * SUBLANES, 2 * kin).astype(BF16)
        bu_ref[:, sb * sw:(sb + 1) * sw] = _dot(lhs, bm_ref[sb])

    def step(t, hs):
        rows = pl.ds(pl.multiple_of(t * SUBLANES, SUBLANES), SUBLANES)
        out = []
        for sb in range(nsb):
            hr, hi = hs[2 * sb], hs[2 * sb + 1]
            lo = sb * sw
            ar = a_ref[0, :, lo:lo + hw]
            ai = a_ref[0, :, lo + hw:lo + sw]
            nr = ar * hr - ai * hi + bu_ref[rows, lo:lo + hw]
            ni = ar * hi + ai * hr + bu_ref[rows, lo + hw:lo + sw]
            bu_ref[rows, lo:lo + hw] = nr
            bu_ref[rows, lo + hw:lo + sw] = ni
            out += [nr, ni]
        return tuple(out)

    h0 = []
    for sb in range(nsb):
        h0 += [h_ref[:, sb * sw:sb * sw + hw], h_ref[:, sb * sw + hw:(sb + 1) * sw]]
    hs = lax.fori_loop(0, tt, step, tuple(h0))
    for sb in range(nsb):
        h_ref[:, sb * sw:sb * sw + hw] = hs[2 * sb]
        h_ref[:, sb * sw + hw:(sb + 1) * sw] = hs[2 * sb + 1]

    nout = cm_ref.shape[2] // 2
    fwd_row = (lax.broadcasted_iota(jnp.int32, (tt * SUBLANES, nout), 0) % SUBLANES) < SUBLANES // 2
    ys = []
    for sb in range(nsb):
        yy = _dot(bu_ref[:, sb * sw:(sb + 1) * sw].astype(BF16), cm_ref[sb])
        ys.append(jnp.where(fwd_row, yy[:, :nout], yy[:, nout:]))
    y = jnp.concatenate(ys, axis=1).reshape(tt, SUBLANES, nsb * nout)
    yf_ref[...] = y
    for t in range(tt):
        yb_ref[tt - 1 - t] = y[t]


def s5_scan(u, bm, cm, a, t_ctx):
    ttot, bsz, width = u.shape
    rows = 2 * bsz
    nsb, _, sw = bm.shape
    tt = 128
    ncb, ntb = t_ctx // tt, ttot // tt
    fmap = lambda g: (g, 0, 0)
    bmap = lambda g: (jnp.where(g < ncb, ncb - 1 - g, ntb + ncb - 1 - g), 0, 0)
    full = lambda x: pl.BlockSpec(x.shape, lambda g: (0,) * x.ndim)
    osd = jax.ShapeDtypeStruct((ttot, rows, width), F32)
    return pl.pallas_call(
        functools.partial(_s5_kernel, tt=tt),
        grid=(ntb,),
        in_specs=[pl.BlockSpec((tt, bsz, width), fmap), pl.BlockSpec((tt, bsz, width), bmap),
                  full(bm), full(cm), full(a)],
        out_specs=[pl.BlockSpec((tt, rows, width), fmap), pl.BlockSpec((tt, rows, width), bmap)],
        out_shape=[osd, osd],
        scratch_shapes=[pltpu.VMEM((rows, nsb * sw), F32), pltpu.VMEM((tt * rows, nsb * sw), F32),
                        pltpu.VMEM((tt, bsz, width), F32)],
        compiler_params=_cparams(("arbitrary",)),
        name="s5_scan",
    )(u, u, bm, cm, a)


def _even_out_kernel(x_ref, of_ref, ob_ref, bonus_ref, g_ref, ys_ref, u_ref, m5_ref,
                     ones_ref, gng_ref, gnb_ref, dskip_ref, wglu_ref, bglu_ref, wo1_ref, wo2_ref, o_ref):
    ones = ones_ref[...]
    inv = 1.0 / HEAD_DIM
    o = of_ref[0] + ob_ref[0]
    mean = _group_sum(o, ones) * inv
    oc = o - mean
    var = _group_sum(oc * oc, ones) * inv
    y1 = (oc * lax.rsqrt(var + GN_EPS) * gng_ref[...] + gnb_ref[...] + bonus_ref[0]) * g_ref[0]
    y = ys_ref[0] + dskip_ref[...] * u_ref[0]
    y = jax.nn.gelu(y)
    y2 = y * _sigmoid(_dot(y.astype(BF16), wglu_ref[...]) + bglu_ref[...])
    out = _dot(y1.astype(BF16), wo1_ref[...]) + _dot(y2.astype(BF16), wo2_ref[...])
    o_ref[0] = x_ref[0] + m5_ref[0] * out


def even_out(x, off, o_f, o_b, bonus, g, ys, u, m5, consts, tm):
    bsz, t, d = x.shape
    w = o_f.shape[2]
    tokd = pl.BlockSpec((1, tm, d), lambda b, i: (b, i, 0))
    tokw = pl.BlockSpec((1, tm, w), lambda b, i: (b, i + off, 0))
    toks = pl.BlockSpec((1, tm, u.shape[2]), lambda b, i: (b, i + off, 0))
    full = lambda a: pl.BlockSpec(a.shape, lambda b, i: (0,) * a.ndim)
    return pl.pallas_call(
        _even_out_kernel,
        grid=(bsz, t // tm),
        in_specs=[tokd, tokw, tokw, tokw, tokw, toks, toks,
                  pl.BlockSpec((1, 1, d), lambda b, i: (b, 0, 0))] + [full(a) for a in consts],
        out_specs=tokd,
        out_shape=jax.ShapeDtypeStruct((bsz, t, d), F32),
        compiler_params=_cparams(("parallel", "parallel")),
        name="even_out",
    )(x, o_f, o_b, bonus, g, ys, u, m5, *consts)


def _hy_in_kernel(x_ref, xp_ref, xn_ref, gain_ref, shift_ref, scale_ref, w_ref, cw_ref, cb_ref,
                  z_ref, zb_ref, g1_ref, g2_ref, *, nt, c):
    i = pl.program_id(1)
    gain, shift, scale = gain_ref[...], shift_ref[0], scale_ref[0]
    tm = x_ref.shape[1]
    halo = xp_ref.shape[1]
    h = jnp.concatenate([_rms_mod(xr[0], gain, shift, scale) for xr in (xp_ref, x_ref, xn_ref)], axis=0)
    h = h.astype(BF16)
    row = lax.broadcasted_iota(jnp.int32, (tm, c), 0)
    outs = (z_ref, g1_ref, g2_ref)
    for part in range(3):
        wp = w_ref[:, part * c:(part + 1) * c]
        p_ext = _dot(h, wp)
        p = p_ext[halo:halo + tm]
        pp = jnp.where(i > 0, p_ext[halo - 1:halo], 0.0)
        pn = jnp.where(i < nt - 1, p_ext[halo + tm:halo + tm + 1], 0.0)
        pm1 = jnp.where(row == 0, pp, pltpu.roll(p, 1, 0))
        pp1 = jnp.where(row == tm - 1, pn, pltpu.roll(p, tm - 1, 0))
        cw = cw_ref[:, part * c:(part + 1) * c]
        q = cw[0:1] * pm1 + cw[1:2] * p + cw[2:3] * pp1 + cb_ref[:, part * c:(part + 1) * c]
        outs[part][0] = q
        if part == 0:
            zb_ref[0] = q.astype(BF16)


def hyena_in(x, gain, shift, scale, w, conv_w, conv_b):
    bsz, t, d = x.shape
    c = w.shape[1] // 3
    tm = min(512, t)
    nt = t // tm
    hb = tm // 8
    nhb = t // 8
    vec = pl.BlockSpec((1, 1, d), lambda b, i: (b, 0, 0))
    tok = pl.BlockSpec((1, tm, c), lambda b, i: (b, i, 0))
    sds = jax.ShapeDtypeStruct((bsz, t, c), F32)
    return pl.pallas_call(
        functools.partial(_hy_in_kernel, nt=nt, c=c),
        grid=(bsz, nt),
        in_specs=[pl.BlockSpec((1, tm, d), lambda b, i: (b, i, 0)),
                  pl.BlockSpec((1, 8, d), lambda b, i: (b, jnp.maximum(i * hb - 1, 0), 0)),
                  pl.BlockSpec((1, 8, d), lambda b, i: (b, jnp.minimum((i + 1) * hb, nhb - 1), 0)),
                  pl.BlockSpec((1, d), lambda b, i: (0, 0)),
                  vec, vec,
                  pl.BlockSpec(w.shape, lambda b, i: (0, 0)),
                  pl.BlockSpec(conv_w.shape, lambda b, i: (0, 0)),
                  pl.BlockSpec((1, 3 * c), lambda b, i: (0, 0))],
        out_specs=[tok, tok, tok, tok],
        out_shape=[sds, jax.ShapeDtypeStruct((bsz, t, c), BF16), sds, sds],
        compiler_params=_cparams(("parallel", "parallel")),
        name="hyena_in",
    )(x, x, x, gain.reshape(1, d), shift, scale, w, conv_w, conv_b.reshape(1, 3 * c))


def _pack_complex(re, im):
    bits = lambda x: lax.bitcast_convert_type(x.astype(BF16).astype(F32), jnp.uint32)
    return lax.shift_right_logical(bits(re), jnp.uint32(16)) | bits(im)


def _unpack_complex(w):
    re = lax.bitcast_convert_type(lax.shift_left(w, jnp.uint32(16)), F32)
    im = lax.bitcast_convert_type(w & jnp.uint32(0xFFFF0000), F32)
    return jnp.concatenate([re, im], axis=0).astype(BF16)


def _fft_conv_kernel(z_ref, kf_ref, g_ref, w1_ref, g2i_ref, f1c_ref, twt_ref, o_ref, a_ref, t_ref,
                     *, n1, k1c):
    s = pl.program_id(2)
    n2 = FFT_N2
    nd = n1 // 2
    slab = n1 // k1c

    @pl.when(s == 0)
    def _():
        w1 = w1_ref[...]

        def body(j, carry):
            x = jnp.concatenate([z_ref[0, j], z_ref[1, j]], axis=0)
            a = _dot(w1, x)
            a_ref[pl.ds(pl.multiple_of(j * n1, n1), n1), :] = _pack_complex(a[:n1], a[n1:])
            return carry

        lax.fori_loop(0, n2, body, 0, unroll=16)

    @pl.when((s > 0) & (s <= k1c))
    def _():
        g2i = g2i_ref[...]
        base = (s - 1) * slab

        def body(kk, carry):
            k1 = base + kk
            ak = _unpack_complex(a_ref[pl.ds(k1, n2, stride=n1), :])
            x = _dot(g_ref[kk], ak)
            xr, xi = x[:n2], x[n2:]
            kr, ki = kf_ref[0, kk], kf_ref[1, kk]
            pr = xr * kr - xi * ki
            pi = xr * ki + xi * kr
            tt = _dot(g2i, jnp.concatenate([pr, pi], axis=0).astype(BF16))
            t_ref[pl.ds(pl.multiple_of(k1 * n2, n2), n2), :] = _pack_complex(tt[:n2], tt[n2:])
            return carry

        lax.fori_loop(0, slab, body, 0, unroll=16)

    @pl.when(s == k1c + 1)
    def _():
        f1r, f1i = f1c_ref[0], f1c_ref[1]

        def body(j, carry):
            twr = twt_ref[0, pl.ds(j, 1), :]
            twi = twt_ref[1, pl.ds(j, 1), :]
            wr = f1r * twr + f1i * twi
            wi = f1i * twr - f1r * twi
            w3 = jnp.concatenate([jnp.concatenate([wr, -wi], axis=1),
                                  jnp.concatenate([wi, wr], axis=1)], axis=0).astype(BF16)
            tn = _unpack_complex(t_ref[pl.ds(j, n1, stride=n2), :])
            y = _dot(w3, tn)
            o_ref[0, j] = y[:nd]
            o_ref[1, j] = y[nd:]
            return carry

        lax.fori_loop(0, n2, body, 0, unroll=16)


def _fft_consts(n1):
    n2 = FFT_N2
    n = n1 * n2
    nd = n1 // 2
    k1 = np.arange(n1)
    f1 = np.exp(-2j * np.pi * np.outer(k1, np.arange(nd)) / n1)
    w1 = np.block([[f1.real, -f1.imag], [f1.imag, f1.real]])
    f2 = np.exp(-2j * np.pi * np.outer(np.arange(n2), np.arange(n2)) / n2)
    f2c = np.conj(f2)
    g2i = np.block([[f2c.real, -f2c.imag], [f2c.imag, f2c.real]])
    f1c = np.exp(2j * np.pi * np.outer(np.arange(nd), k1) / n1) / n
    tw = np.exp(-2j * np.pi * np.outer(k1, np.arange(n2)) / n)
    cplx = lambda m: jnp.asarray(np.stack([m.real, m.imag]), F32)
    (f2r, f2i), (twr, twi) = cplx(f2), cplx(tw)
    gr = f2r[None] * twr[:, None, :] - f2i[None] * twi[:, None, :]
    gi = f2r[None] * twi[:, None, :] + f2i[None] * twr[:, None, :]
    gtw = jnp.concatenate([jnp.concatenate([gr, -gi], axis=2), jnp.concatenate([gi, gr], axis=2)], axis=1)
    return (gtw.astype(BF16), jnp.asarray(w1, BF16), jnp.asarray(g2i, BF16), cplx(f1c), cplx(tw.T))


def fft_conv(zt, kf, order):
    bsz, n2, nd, c = zt.shape
    n1 = 2 * nd
    cb = LANES
    k1c = 8 if n1 % 8 == 0 and n1 >= 64 else 2
    slab = n1 // k1c
    gtw, *consts = _fft_consts(n1)
    full = lambda a: pl.BlockSpec(a.shape, lambda j, p, s: (0,) * a.ndim)
    blk = pl.BlockSpec((2, n2, nd, cb), lambda j, p, s: (p, 0, 0, j))
    chunk = lambda j, p, s: jnp.clip(s - 1, 0, k1c - 1)
    return pl.pallas_call(
        functools.partial(_fft_conv_kernel, n1=n1, k1c=k1c),
        grid=(c // cb, bsz // 2, k1c + 2),
        in_specs=[blk,
                  pl.BlockSpec((None, 2, slab, n2, cb), lambda j, p, s: (order, 0, chunk(j, p, s), 0, j)),
                  pl.BlockSpec((slab, 2 * n2, 2 * n2), lambda j, p, s: (chunk(j, p, s), 0, 0))]
                 + [full(a) for a in consts],
        out_specs=blk,
        out_shape=jax.ShapeDtypeStruct(zt.shape, F32),
        scratch_shapes=[pltpu.VMEM((n2 * n1, cb), jnp.uint32), pltpu.VMEM((n1 * n2, cb), jnp.uint32)],
        compiler_params=_cparams(("parallel", "parallel", "arbitrary")),
        name="fft_conv",
    )(zt, kf, gtw, *consts)


def _fft_fwd_kernel(z_ref, scale_ref, g_ref, w1_ref, o_ref, a_ref, *, n1, k1c):
    s = pl.program_id(2)
    n2 = FFT_N2
    slab = n1 // k1c
    cb = o_ref.shape[4]

    @pl.when(s == 0)
    def _():
        w1 = w1_ref[...]

        def body(j, carry):
            a = _dot(w1, jnp.concatenate([z_ref[0, j], z_ref[1, j]], axis=1))
            packed = _pack_complex(a[:n1], a[n1:])
            rows = pl.ds(pl.multiple_of(j * n1, n1), n1)
            a_ref[0, rows, :] = packed[:, :cb]
            a_ref[1, rows, :] = packed[:, cb:]
            return carry

        lax.fori_loop(0, n2, body, 0, unroll=16)

    @pl.when(s > 0)
    def _():
        base = (s - 1) * slab

        def body(kk, carry):
            k1 = base + kk
            rows = pl.ds(k1, n2, stride=n1)
            ak = jnp.concatenate([_unpack_complex(a_ref[0, rows, :]), _unpack_complex(a_ref[1, rows, :])], axis=1)
            x = _dot(g_ref[kk], ak)
            pos, neg = x[:, :cb], x[:, cb:]
            scale = scale_ref[0]
            o_ref[0, 0, kk] = (pos[:n2] + neg[:n2]) * scale
            o_ref[0, 1, kk] = (pos[n2:] - neg[n2:]) * scale
            return carry

        lax.fori_loop(0, slab, body, 0, unroll=16)


def filter_spectrum(taps, scale):
    orders, _, n2, nd, c = taps.shape
    n1 = 2 * nd
    cb = LANES
    k1c = 8 if n1 % 8 == 0 and n1 >= 64 else 2
    slab = n1 // k1c
    gtw, w1 = _fft_consts(n1)[:2]
    w1 = w1[:, :nd]
    full = lambda a: pl.BlockSpec(a.shape, lambda j, r, s: (0,) * a.ndim)
    return pl.pallas_call(
        functools.partial(_fft_fwd_kernel, n1=n1, k1c=k1c),
        grid=(c // cb, orders, k1c + 1),
        in_specs=[pl.BlockSpec((None, 2, n2, nd, cb), lambda j, r, s: (r, 0, 0, 0, j)),
                  pl.BlockSpec((1, 1, cb), lambda j, r, s: (r, 0, j)),
                  pl.BlockSpec((slab, 2 * n2, 2 * n2), lambda j, r, s: (jnp.maximum(s - 1, 0), 0, 0)),
                  full(w1)],
        out_specs=pl.BlockSpec((1, 2, slab, n2, cb), lambda j, r, s: (r, 0, jnp.maximum(s - 1, 0), 0, j)),
        out_shape=jax.ShapeDtypeStruct((orders, 2, n1, n2, c), F32),
        scratch_shapes=[pltpu.VMEM((2, n2 * n1, cb), jnp.uint32)],
        compiler_params=_cparams(("parallel", "parallel", "arbitrary")),
        name="fft_forward",
    )(taps, scale, gtw, w1)


def _hy_filter_kernel(feat_ref, fw1_ref, fb1_ref, fw2_ref, fb2_ref, fw3_ref, fb3_ref, fw4_ref, freq_ref,
                      delta_ref, z_ref, ss_ref):
    @pl.when(pl.program_id(0) == 0)
    def _():
        ss_ref[...] = jnp.zeros_like(ss_ref)

    feats = feat_ref[...]
    t = feats[:, 0:1]
    fr = freq_ref[...]
    h = jnp.sin(fr * (_dot_exact(feats, fw1_ref[...]) + fb1_ref[...]))
    h = jnp.sin(fr * (_dot_exact(h, fw2_ref[...]) + fb2_ref[...]))
    h = jnp.sin(fr * (_dot_exact(h, fw3_ref[...]) + fb3_ref[...])).astype(BF16)
    decay = jnp.exp(-t * delta_ref[...])
    width = delta_ref.shape[1]
    for r in range(z_ref.shape[0]):
        filt = _dot(h, fw4_ref[:, r * width:(r + 1) * width]) * decay
        if r % 2 == 1:
            filt = jnp.where(t == 0.0, 0.0, filt)
        z_ref[r] = filt.astype(BF16)
        ss_ref[r] += jnp.sum(filt * filt, axis=0, keepdims=True)


def hyena_filter(t, fw1, fb1, fw2, fb2, fw3, fb3, fw4, freq, width):
    nd = t // FFT_N2
    pos = (jnp.arange(FFT_N2, dtype=F32)[:, None] + FFT_N2 * jnp.arange(nd, dtype=F32)[None, :]).reshape(-1, 1)
    tt = pos / max(t - 1, 1)
    ang = 2 * math.pi * pos / t
    nb = (HY_EMB - 1) // 2
    bands = jnp.linspace(1e-4, nb - 1, nb, dtype=F32)[None]
    feats = jnp.concatenate([tt, jnp.cos(bands * ang), -jnp.sin(bands * ang),
                             jnp.zeros((t, LANES - HY_EMB), F32)], axis=-1)
    deltas = jnp.abs(jnp.linspace(HY_MIN_DECAY, HY_MAX_DECAY, width, dtype=F32)).reshape(1, width)
    rows = 2 * HY_ORDER
    tm = min(512, t)
    hid = fw1.shape[1]
    full = lambda a: pl.BlockSpec(a.shape, lambda i: (0,) * a.ndim)
    fw1p = jnp.concatenate([fw1, jnp.zeros((LANES - HY_EMB, hid), F32)], axis=0)
    vecs = [fw1p, fb1.reshape(1, hid), fw2, fb2.reshape(1, hid), fw3, fb3.reshape(1, hid),
            fw4.astype(BF16), freq.reshape(1, hid), deltas]
    return pl.pallas_call(
        _hy_filter_kernel,
        grid=(t // tm,),
        in_specs=[pl.BlockSpec((tm, LANES), lambda i: (i, 0))] + [full(a) for a in vecs],
        out_specs=[pl.BlockSpec((rows, tm, width), lambda i: (0, i, 0)),
                   pl.BlockSpec((rows, 1, width), lambda i: (0, 0, 0))],
        out_shape=[jax.ShapeDtypeStruct((rows, t, width), BF16), jax.ShapeDtypeStruct((rows, 1, width), F32)],
        compiler_params=_cparams(("arbitrary",)),
        name="hyena_filter",
    )(feats, *vecs)


def hyena_filter_spectrum_fft(t, fw1, fb1, fw2, fb2, fw3, fb3, fw4, freq, width):
    taps, ss = hyena_filter(t, fw1, fb1, fw2, fb2, fw3, fb3, fw4, freq, width)
    nd = t // FFT_N2
    scale = lax.rsqrt(ss.reshape(HY_ORDER, 2, width).sum(axis=1, keepdims=True) + 1e-6)
    return filter_spectrum(taps.reshape(HY_ORDER, 2, FFT_N2, nd, width), scale)


def _dft_conv_kernel(z_ref, kf_ref, fw_ref, iv_ref, o_ref, *, t):
    x = jnp.concatenate([z_ref[0], z_ref[1]], axis=0)
    spec = _dot(fw_ref[...], x)
    xr, xi = spec[:2 * t], spec[2 * t:]
    kr, ki = kf_ref[0], kf_ref[1]
    pr = xr * kr - xi * ki
    pi = xr * ki + xi * kr
    y = _dot(iv_ref[...], jnp.concatenate([pr, pi], axis=0).astype(BF16))
    o_ref[0] = y[:t]
    o_ref[1] = y[t:]


def dft_conv(zb, kf):
    bsz, t, c = zb.shape
    n = 2 * t
    f = np.exp(-2j * np.pi * np.outer(np.arange(n), np.arange(t)) / n)
    fw = np.block([[f.real, -f.imag], [f.imag, f.real]])
    fi = np.exp(2j * np.pi * np.outer(np.arange(t), np.arange(n)) / n) / n
    iv = np.block([[fi.real, -fi.imag], [fi.imag, fi.real]])
    fw, iv = jnp.asarray(fw, BF16), jnp.asarray(iv, BF16)
    cb = LANES
    blk = pl.BlockSpec((2, t, cb), lambda j, p: (p, 0, j))
    return pl.pallas_call(
        functools.partial(_dft_conv_kernel, t=t),
        grid=(c // cb, bsz // 2),
        in_specs=[blk, pl.BlockSpec((2, n, cb), lambda j, p: (0, 0, j)),
                  pl.BlockSpec(fw.shape, lambda j, p: (0, 0)), pl.BlockSpec(iv.shape, lambda j, p: (0, 0))],
        out_specs=blk,
        out_shape=jax.ShapeDtypeStruct(zb.shape, F32),
        compiler_params=_cparams(("parallel", "parallel")),
        name="dft_conv",
    )(zb, kf, fw, iv)


def _conv_rows(conv_ref):
    if len(conv_ref.shape) == 3:
        return conv_ref[0]
    return jnp.concatenate([conv_ref[0, :, k, :] for k in range(conv_ref.shape[2])], axis=0)


def _conv_spec(conv, tm, c):
    if conv.ndim == 3:
        return pl.BlockSpec((1, tm, c), lambda b, i: (b, i, 0))
    return pl.BlockSpec((1, FFT_N2, tm // FFT_N2, c), lambda b, i: (b, 0, i, 0))


def _hy_gate_kernel(conv_ref, z_ref, gate_ref, bias_ref, o_ref, ob_ref):
    y = gate_ref[0] * (_conv_rows(conv_ref) + bias_ref[...] * z_ref[0])
    o_ref[0] = y
    ob_ref[0] = y.astype(BF16)


def hyena_gate(conv, z, gate, bias):
    bsz, t, c = z.shape
    tm = min(SUBLANES * FFT_N2, t)
    tok = pl.BlockSpec((1, tm, c), lambda b, i: (b, i, 0))
    return pl.pallas_call(
        _hy_gate_kernel,
        grid=(bsz, t // tm),
        in_specs=[_conv_spec(conv, tm, c), tok, tok, pl.BlockSpec((1, c), lambda b, i: (0, 0))],
        out_specs=[tok, tok],
        out_shape=[jax.ShapeDtypeStruct(z.shape, F32), jax.ShapeDtypeStruct(z.shape, BF16)],
        compiler_params=_cparams(("parallel", "parallel")),
        name="hyena_gate",
    )(conv, z, gate, bias.reshape(1, c))


def _hy_out_kernel(x_ref, conv_ref, z_ref, gate_ref, bias_ref, w_ref, m5_ref, o_ref):
    y = gate_ref[0] * (_conv_rows(conv_ref) + bias_ref[...] * z_ref[0])
    o_ref[0] = x_ref[0] + m5_ref[0] * _dot(y.astype(BF16), w_ref[...])


def hyena_out(x, conv, z, gate, bias, w, m5):
    bsz, t, d = x.shape
    c = z.shape[2]
    tm = min(SUBLANES * FFT_N2, t)
    tokd = pl.BlockSpec((1, tm, d), lambda b, i: (b, i, 0))
    tokc = pl.BlockSpec((1, tm, c), lambda b, i: (b, i, 0))
    return pl.pallas_call(
        _hy_out_kernel,
        grid=(bsz, t // tm),
        in_specs=[tokd, _conv_spec(conv, tm, c), tokc, tokc, pl.BlockSpec((1, c), lambda b, i: (0, 0)),
                  pl.BlockSpec(w.shape, lambda b, i: (0, 0)), pl.BlockSpec((1, 1, d), lambda b, i: (b, 0, 0))],
        out_specs=tokd,
        out_shape=jax.ShapeDtypeStruct(x.shape, F32),
        compiler_params=_cparams(("parallel", "parallel")),
        name="hyena_out",
    )(x, conv, z, gate, bias.reshape(1, c), w, m5)


def _hyena_filter_spectrum(t, fw1, fb1, fw2, fb2, fw3, fb3, fw4, freq, width):
    pos = jnp.arange(t, dtype=F32)[:, None]
    tt = pos / max(t - 1, 1)
    ang = 2 * math.pi * pos / t
    nb = (HY_EMB - 1) // 2
    bands = jnp.linspace(1e-4, nb - 1, nb, dtype=F32)[None]
    feats = jnp.concatenate([tt, jnp.cos(bands * ang), -jnp.sin(bands * ang)], axis=-1)
    hdn = jnp.sin(freq * (feats @ fw1 + fb1))
    hdn = jnp.sin(freq * (hdn @ fw2 + fb2))
    hdn = jnp.sin(freq * (hdn @ fw3 + fb3))
    filt = (hdn @ fw4).reshape(t, HY_ORDER, 2, width)
    deltas = jnp.abs(jnp.linspace(HY_MIN_DECAY, HY_MAX_DECAY, width, dtype=F32))
    filt = filt * jnp.exp(-tt[:, :, None, None] * deltas)
    fwd, bwd = filt[:, :, 0], filt[:, :, 1]
    kern = jnp.concatenate([fwd, jnp.zeros_like(fwd[:1]), bwd[:0:-1]], axis=0)
    kern = kern * lax.rsqrt(jnp.sum(kern * kern, axis=0, keepdims=True) + 1e-6)
    spec = jnp.fft.fft(kern, axis=0)
    return jnp.stack([jnp.real(spec), jnp.imag(spec)], axis=1).transpose(2, 1, 0, 3).astype(F32)


def _block_diag_pair(m):
    z = jnp.zeros_like(m[0])
    return jnp.concatenate([jnp.concatenate([m[0], z], axis=1), jnp.concatenate([z, m[1]], axis=1)], axis=0)


def _even_mixer(lat, cx, ml, mc, gain, prm, ctx_out):
    (w_in, mu, w0, w_up, a0, a_up, g_up, k_k, k_a, r_k, gn_g, gn_b,
     lam_re, lam_im, log_dt, b_re, b_im, c_re, c_im, d_skip, w_glu, b_glu, w_out) = prm
    bsz, t_lat, d = lat.shape
    t_ctx = cx.shape[1]
    width = k_k.shape[0]
    heads = width // HEAD_DIM
    n_cols = mu.shape[0]
    s5w = d_skip.shape[0]
    ttot = t_ctx + t_lat
    nch = bsz * heads
    tm = t_ctx
    assert tm % GRID_W == 0 and t_lat % tm == 0 and 4 * nch == LANES and 2 * bsz == SUBLANES

    p_all, u_all = mod_matmul_stream(cx, lat, gain, mc, ml, w_in.astype(BF16), tm, n_cols)

    head_of = np.arange(width) // HEAD_DIM
    ones = jnp.asarray(head_of[:, None] == head_of[None, :], BF16)
    feat_consts = (mu.reshape(1, -1), _block_diag_pair(w_up).astype(BF16), w0.reshape(1, -1),
                   _block_diag_pair(a_up).astype(BF16), a0.reshape(1, -1), g_up.astype(BF16),
                   k_a.reshape(1, -1), r_k.reshape(1, -1), ones)
    v2, g, bonus, r2, k2, w2, a2 = rwkv_features(p_all, n_cols, width, feat_consts, tm)

    def key_major(x):
        x = x.reshape(2, bsz, ttot, heads, 2, HALF_HEAD)
        return x.transpose(2, 5, 0, 4, 1, 3).reshape(ttot, HALF_HEAD, LANES)

    def value_major(x):
        x = x.reshape(2, bsz, ttot, heads, HEAD_DIM)
        return x.transpose(2, 4, 0, 1, 3).reshape(ttot, HEAD_DIM, 2 * nch)

    def key_const(x):
        x = x.reshape(heads, 2, HALF_HEAD).transpose(2, 1, 0)[:, None, :, None, :]
        return jnp.broadcast_to(x, (HALF_HEAD, 2, 2, bsz, heads)).reshape(HALF_HEAD, LANES)

    o_f, o_b = rwkv_scan(key_major(r2), key_major(k2), key_major(w2), key_major(a2), value_major(v2),
                         key_const(k_a), key_const(k_k), t_ctx)

    def token_major(o, lane0):
        o = o[..., lane0:lane0 + nch].reshape(ttot, HEAD_DIM, bsz, heads)
        return o.transpose(2, 0, 3, 1).reshape(bsz, ttot, width)

    o_f, o_b = token_major(o_f, 0), token_major(o_b, LANES // 2)

    ng = lam_re.shape[1]
    gps = ng // S5_SUPER
    lam = lax.complex(lam_re, lam_im)
    dt = jnp.exp(log_dt)[..., None]
    a_bar = jnp.exp(lam * dt)
    b_bar = ((a_bar - 1) / lam)[..., None] * lax.complex(b_re, b_im)
    eye = jnp.eye(gps, dtype=F32)

    def b_mat(x):
        x = x.reshape(2, S5_SUPER, gps, S5_STATE, S5_GROUP)
        m = jnp.einsum('dsgph,gk->sdghkp', x, eye)
        return m.reshape(S5_SUPER, 2 * gps * S5_GROUP, gps * S5_STATE)

    def c_mat(x):
        x = x.reshape(2, S5_SUPER, gps, S5_GROUP, S5_STATE)
        m = jnp.einsum('dsghp,gk->skpdgh', x, eye)
        return m.reshape(S5_SUPER, gps * S5_STATE, 2 * gps * S5_GROUP)

    bm = jnp.concatenate([b_mat(jnp.real(b_bar)), b_mat(jnp.imag(b_bar))], axis=2).astype(BF16)
    cm = jnp.concatenate([c_mat(c_re), -c_mat(c_im)], axis=1).astype(BF16)

    def a_rows(x):
        x = x.reshape(2, 1, S5_SUPER, gps * S5_STATE)
        return jnp.broadcast_to(x, (2, bsz, S5_SUPER, gps * S5_STATE)).reshape(2 * bsz, S5_SUPER, -1)

    a_arr = jnp.concatenate([a_rows(jnp.real(a_bar)), a_rows(jnp.imag(a_bar))], axis=2)
    a_arr = a_arr.reshape(1, 2 * bsz, -1)

    y_f, y_b = s5_scan(u_all.transpose(1, 0, 2), bm, cm, a_arr, t_ctx)
    ys = (y_f[:, :bsz] + y_b[:, bsz:]).transpose(1, 0, 2)

    wo = w_out.astype(BF16)
    out_consts = (ones, gn_g.reshape(1, -1), gn_b.reshape(1, -1), d_skip.reshape(1, -1),
                  w_glu.astype(BF16), b_glu.reshape(1, -1), wo[:width], wo[width:])
    lat = even_out(lat, 1, o_f, o_b, bonus, g, ys, u_all, ml[5], out_consts, tm)
    if ctx_out:
        cx = even_out(cx, 0, o_f, o_b, bonus, g, ys, u_all, mc[5], out_consts, tm)
    return lat, cx


def _hyena_mixer(x, gain, m, prm):
    (w_in, conv_w, conv_b, fw1, fb1, fw2, fb2, fw3, fb3, fw4, freq, bias_d, w_out) = prm
    bsz, t, d = x.shape
    c = w_out.shape[0]
    use_fft = t % (FFT_N2 * 2) == 0 and t >= 4 * FFT_N2
    if use_fft:
        kf = hyena_filter_spectrum_fft(t, fw1, fb1, fw2, fb2, fw3, fb3, fw4, freq, c)
    else:
        kf = _hyena_filter_spectrum(t, fw1, fb1, fw2, fb2, fw3, fb3, fw4, freq, c)
    z, zb, g1, g2 = hyena_in(x, gain, m[3], m[4], w_in.astype(BF16), conv_w, conv_b)
    gates = (g1, g2)
    for n in range(HY_ORDER):
        if use_fft:
            nd = t // FFT_N2
            zt = zb.reshape(bsz, nd, FFT_N2, c).transpose(0, 2, 1, 3)
            conv = fft_conv(zt, kf, n)
        else:
            conv = dft_conv(zb, kf[n])
        if n < HY_ORDER - 1:
            z, zb = hyena_gate(conv, z, gates[n], bias_d[n])
        else:
            return hyena_out(x, conv, z, gates[n], bias_d[n], w_out.astype(BF16), m[5])


def kernel(x, c, ctx, c_ctx, norm_g, ada_w, ada_b, ffn_wg, ffn_wu, ffn_wd, final_g, ev_w_in, ev_mu, ev_w0, ev_w_up, ev_a0, ev_a_up, ev_g_up, ev_k_k, ev_k_a, ev_r_k, ev_gn_g, ev_gn_b, ev_lam_re, ev_lam_im, ev_log_dt, ev_b_re, ev_b_im, ev_c_re, ev_c_im, ev_d, ev_w_glu, ev_b_glu, ev_w_out, od_w_in, od_conv_w, od_conv_b, od_fw1, od_fb1, od_fw2, od_fb2, od_fw3, od_fb3, od_fw4, od_freq, od_bias, od_w_out):
    depth = norm_g.shape[0]
    bsz, _, d = x.shape
    n_even = (depth + 1) // 2
    last_ctx = 2 * (n_even - 1)

    cond8 = jnp.concatenate([c, c_ctx[None], jnp.zeros((8 - bsz - 1, d), F32)], axis=0)
    mods = ada_mods_all(cond8, ada_w, ada_b)

    wg, wu, wd = ffn_wg.astype(BF16), ffn_wu.astype(BF16), ffn_wd.astype(BF16)
    lat, cx = x, ctx
    for l in range(depth):
        run_ctx = l <= last_ctx
        ctx_out = l < last_ctx
        i = l // 2
        ml = [mods[l, :bsz, None, k * d:(k + 1) * d] for k in range(N_MOD)]
        mc = [jnp.broadcast_to(mods[l, bsz:bsz + 1, None, k * d:(k + 1) * d], (bsz, 1, d))
              for k in range(N_MOD)]
        lat = ffn_half(lat, norm_g[l, 0], ml[0], ml[1], ml[2], wg[l, 0], wu[l, 0], wd[l, 0])
        if run_ctx:
            cx = ffn_half(cx, norm_g[l, 0], mc[0], mc[1], mc[2], wg[l, 0], wu[l, 0], wd[l, 0])
        if l % 2 == 0:
            prm = (ev_w_in[i], ev_mu[i], ev_w0[i], ev_w_up[i], ev_a0[i], ev_a_up[i], ev_g_up[i],
                   ev_k_k[i], ev_k_a[i], ev_r_k[i], ev_gn_g[i], ev_gn_b[i],
                   ev_lam_re[i], ev_lam_im[i], ev_log_dt[i], ev_b_re[i], ev_b_im[i], ev_c_re[i], ev_c_im[i],
                   ev_d[i], ev_w_glu[i], ev_b_glu[i], ev_w_out[i])
            lat, cx = _even_mixer(lat, cx, ml, mc, norm_g[l, 1], prm, ctx_out)
        else:
            prm = (od_w_in[i], od_conv_w[i], od_conv_b[i], od_fw1[i], od_fb1[i], od_fw2[i], od_fb2[i],
                   od_fw3[i], od_fb3[i], od_fw4[i], od_freq[i], od_bias[i], od_w_out[i])
            lat = _hyena_mixer(lat, norm_g[l, 1], ml, prm)
            if ctx_out:
                cx = _hyena_mixer(cx, norm_g[l, 1], mc, prm)
        fin = final_g if l == depth - 1 else None
        lat = ffn_half(lat, norm_g[l, 2], ml[6], ml[7], ml[8], wg[l, 1], wu[l, 1], wd[l, 1], fin)
        if ctx_out:
            cx = ffn_half(cx, norm_g[l, 2], mc[6], mc[7], mc[8], wg[l, 1], wu[l, 1], wd[l, 1])
    return lat
```

```python
import functools
import math

import numpy as np
import jax
import jax.numpy as jnp
from jax import lax
from jax.experimental import pallas as pl
from jax.experimental.pallas import tpu as pltpu

F32 = jnp.float32
BF16 = jnp.bfloat16
HIGHEST = lax.Precision.HIGHEST

N_MOD = 9
NORM_EPS = 1e-6
GN_EPS = 64e-5
GRID_W = 64
HEAD_DIM = 64
HALF_HEAD = HEAD_DIM // 2
S5_GROUP = 16
S5_STATE = 64
S5_SUPER = 4
HY_ORDER = 2
HY_EMB = 33
HY_MIN_DECAY = math.log(1e-2) / 1.5
HY_MAX_DECAY = math.log(1e-2) / 0.3
LANES = 128
SUBLANES = 8
MXU_DIM = 256
FFT_N2 = 128
VMEM_LIMIT = 56 * 1024 * 1024


def _cparams(sem, vmem=VMEM_LIMIT):
    return pltpu.CompilerParams(dimension_semantics=sem, vmem_limit_bytes=vmem)


def _dot(a, b):
    return jnp.dot(a, b, preferred_element_type=F32)


def _dot_exact(a, b):
    return jnp.dot(a, b, preferred_element_type=F32, precision=HIGHEST)


def _group_sum(a, ones):
    hi = a.astype(BF16)
    lo = (a - hi.astype(F32)).astype(BF16)
    return _dot(hi, ones) + _dot(lo, ones)


def _rms_mod(x, gain, shift, scale):
    ms = jnp.mean(x * x, axis=-1, keepdims=True)
    return x * lax.rsqrt(ms + NORM_EPS) * gain * (1.0 + scale) + shift


def _sigmoid(x):
    return 1.0 / (1.0 + jnp.exp(-x))


def _silu(x):
    return x * _sigmoid(x)


def _ada_kernel(c_ref, w_ref, b_ref, o_ref):
    s = _silu(c_ref[...])
    o_ref[0] = _dot(s.astype(BF16), w_ref[0].astype(BF16)) + b_ref[0]


def ada_mods_all(cond8, ada_w, ada_b):
    depth, d, n = ada_w.shape
    tn = n // 8
    return pl.pallas_call(
        _ada_kernel,
        grid=(depth, n // tn),
        in_specs=[pl.BlockSpec((8, d), lambda l, j: (0, 0)),
                  pl.BlockSpec((1, d, tn), lambda l, j: (l, 0, j)),
                  pl.BlockSpec((1, 1, tn), lambda l, j: (l, 0, j))],
        out_specs=pl.BlockSpec((1, 8, tn), lambda l, j: (l, 0, j)),
        out_shape=jax.ShapeDtypeStruct((depth, 8, n), F32),
        compiler_params=_cparams(("parallel", "parallel")),
        name="ada_mods",
    )(cond8, ada_w, ada_b.reshape(depth, 1, n))


def _ffn_kernel(x_ref, gain_ref, shift_ref, scale_ref, gate_ref, wg_ref, wu_ref, wd_ref, fg_ref,
                o_ref, *, final_norm, chunks):
    x = x_ref[0]
    h = _rms_mod(x, gain_ref[...], shift_ref[0], scale_ref[0]).astype(BF16)
    acc = None
    for lo, hi in chunks:
        g = _dot(h, wg_ref[:, lo:hi])
        u = _dot(h, wu_ref[:, lo:hi])
        part = _dot((_silu(g) * u).astype(BF16), wd_ref[lo:hi, :])
        acc = part if acc is None else acc + part
    y = x + 0.5 * gate_ref[0] * acc
    if final_norm:
        ms = jnp.mean(y * y, axis=-1, keepdims=True)
        y = y * lax.rsqrt(ms + NORM_EPS) * fg_ref[...]
    o_ref[0] = y


def ffn_half(x, gain, shift, scale, gate, wg, wu, wd, final_g=None):
    bsz, t, d = x.shape
    ff = wg.shape[1]
    tm = min(1024, t)
    step = 4 * MXU_DIM
    chunks = tuple((lo, min(lo + step, ff)) for lo in range(0, ff, step))
    fg = jnp.ones((1, d), F32) if final_g is None else final_g.reshape(1, d)
    vec = pl.BlockSpec((1, 1, d), lambda b, i: (b, 0, 0))
    resident = lambda a: pl.BlockSpec(a.shape, lambda b, i: (0, 0), pipeline_mode=pl.Buffered(1))
    return pl.pallas_call(
        functools.partial(_ffn_kernel, final_norm=final_g is not None, chunks=chunks),
        grid=(bsz, t // tm),
        in_specs=[pl.BlockSpec((1, tm, d), lambda b, i: (b, i, 0)),
                  pl.BlockSpec((1, d), lambda b, i: (0, 0)),
                  vec, vec, vec, resident(wg), resident(wu), resident(wd),
                  pl.BlockSpec((1, d), lambda b, i: (0, 0))],
        out_specs=pl.BlockSpec((1, tm, d), lambda b, i: (b, i, 0)),
        out_shape=jax.ShapeDtypeStruct((bsz, t, d), F32),
        compiler_params=_cparams(("parallel", "parallel")),
        name="ffn_half",
    )(x, gain.reshape(1, d), shift, scale, gate, wg, wu, wd, fg)


def _modmm_kernel(cx_ref, lat_ref, gain_ref, shc_ref, scc_ref, shl_ref, scl_ref, w_ref, o_ref, u_ref, *, nc):
    is_ctx = pl.program_id(1) < nc
    x = jnp.where(is_ctx, cx_ref[0], lat_ref[0])
    shift = jnp.where(is_ctx, shc_ref[0], shl_ref[0])
    scale = jnp.where(is_ctx, scc_ref[0], scl_ref[0])
    h = _rms_mod(x, gain_ref[...], shift, scale).astype(BF16)
    n1 = o_ref.shape[2]
    o_ref[0] = _dot(h, w_ref[:, :n1])
    u_ref[0] = _dot(h, w_ref[:, n1:])


def mod_matmul_stream(cx, lat, gain, mc, ml, w, tm, n_first):
    bsz, t_ctx, d = cx.shape
    t_lat = lat.shape[1]
    n = w.shape[1]
    nc = t_ctx // tm
    nt = nc + t_lat // tm
    vec = pl.BlockSpec((1, 1, d), lambda b, i: (b, 0, 0))
    return pl.pallas_call(
        functools.partial(_modmm_kernel, nc=nc),
        grid=(bsz, nt),
        in_specs=[pl.BlockSpec((1, tm, d), lambda b, i: (b, jnp.minimum(i, nc - 1), 0)),
                  pl.BlockSpec((1, tm, d), lambda b, i: (b, jnp.maximum(i - nc, 0), 0)),
                  pl.BlockSpec((1, d), lambda b, i: (0, 0)),
                  vec, vec, vec, vec,
                  pl.BlockSpec((d, n), lambda b, i: (0, 0))],
        out_specs=[pl.BlockSpec((1, tm, n_first), lambda b, i: (b, i, 0)),
                   pl.BlockSpec((1, tm, n - n_first), lambda b, i: (b, i, 0))],
        out_shape=[jax.ShapeDtypeStruct((bsz, t_ctx + t_lat, n_first), F32),
                   jax.ShapeDtypeStruct((bsz, t_ctx + t_lat, n - n_first), F32)],
        compiler_params=_cparams(("parallel", "parallel")),
        name="mod_matmul",
    )(cx, lat, gain.reshape(1, d), mc[3], mc[4], ml[3], ml[4], w)


def _rwkv_feat_kernel(p_ref, pu_ref, pd_ref, mu_ref, wup_ref, w0_ref, aup_ref, a0_ref, gup_ref,
                      ka_ref, rk_ref, ones_ref,
                      v_o, g_o, bonus_o, r_o, k_o, w_o, a_o, *, nt, width):
    i = pl.program_id(1)
    is_ctx = i == 0
    p = p_ref[0]
    tm = p.shape[0]
    row = lax.broadcasted_iota(jnp.int32, p.shape, 0)
    lane = lax.broadcasted_iota(jnp.int32, p.shape, 1) % 4
    prev = pltpu.roll(p, 1, 0)
    nxt = pltpu.roll(p, tm - 1, 0)
    col = jnp.where(is_ctx, row, row % GRID_W)
    last = jnp.where(is_ctx, tm - 1, GRID_W - 1)
    left = jnp.where(col == 0, 0.0, prev)
    right = jnp.where(col == last, 0.0, nxt)
    up_halo = jnp.where(i > 1, pu_ref[0], 0.0)
    dn_halo = jnp.where(i < nt - 1, pd_ref[0], 0.0)
    up = jnp.where(is_ctx, left, jnp.concatenate([up_halo, p[:tm - GRID_W]], axis=0))
    down = jnp.where(is_ctx, right, jnp.concatenate([p[GRID_W:], dn_halo], axis=0))
    shifted = jnp.where(lane == 0, left, jnp.where(lane == 1, right, jnp.where(lane == 2, up, down)))
    q = p + mu_ref[...] * (shifted - p)

    w = width
    r, k, v = q[:, :w], q[:, w:2 * w], q[:, 2 * w:3 * w]
    wd = q[:, 3 * w:3 * w + LANES]
    ad = q[:, 3 * w + LANES:3 * w + 2 * LANES]
    gd = q[:, 3 * w + 2 * LANES:3 * w + 3 * LANES]

    zlin = w0_ref[...] + _dot(jnp.tanh(wd).astype(BF16), wup_ref[...])
    decay = jnp.exp(-math.exp(-0.5) * _sigmoid(zlin))
    a = _sigmoid(a0_ref[...] + _dot(ad.astype(BF16), aup_ref[...]))
    g_o[0] = _dot(_sigmoid(gd).astype(BF16), gup_ref[...])

    ones = ones_ref[...]
    for d in range(2):
        v_o[d, 0] = v
        r_o[d, 0] = r
        k_o[d, 0] = k
    ksum = jnp.zeros_like(k)
    for d in range(2):
        a_d = a[:, d * w:(d + 1) * w]
        ksum = ksum + k * (1.0 + (a_d - 1.0) * ka_ref[...])
        w_o[d, 0] = decay[:, d * w:(d + 1) * w]
        a_o[d, 0] = a_d
    bonus_o[0] = _group_sum(r * ksum * rk_ref[...], ones) * v


def rwkv_features(p, n_cols, width, consts, tm):
    bsz, t, _ = p.shape
    nt = t // tm
    hb = tm // GRID_W
    nhb = t // GRID_W
    w = width
    full = lambda a: pl.BlockSpec(a.shape, lambda b, i: (0,) * a.ndim)
    tok = pl.BlockSpec((1, tm, w), lambda b, i: (b, i, 0))
    tok2 = pl.BlockSpec((2, 1, tm, w), lambda b, i: (0, b, i, 0))
    sds = jax.ShapeDtypeStruct((bsz, t, w), F32)
    sds2 = jax.ShapeDtypeStruct((2, bsz, t, w), F32)
    return pl.pallas_call(
        functools.partial(_rwkv_feat_kernel, nt=nt, width=w),
        grid=(bsz, nt),
        in_specs=[pl.BlockSpec((1, tm, n_cols), lambda b, i: (b, i, 0)),
                  pl.BlockSpec((1, GRID_W, n_cols), lambda b, i: (b, jnp.maximum(i * hb - 1, 0), 0)),
                  pl.BlockSpec((1, GRID_W, n_cols), lambda b, i: (b, jnp.minimum((i + 1) * hb, nhb - 1), 0))]
                 + [full(a) for a in consts],
        out_specs=[tok2, tok, tok, tok2, tok2, tok2, tok2],
        out_shape=[sds2, sds, sds, sds2, sds2, sds2, sds2],
        compiler_params=_cparams(("parallel", "parallel")),
        name="rwkv_features",
    )(p, p, p, *consts)


def _rwkv_scan_kernel(rf, rb, kf, kb, wf, wb, af, ab, vf, vb, ka_ref, kks_ref, of_ref, ob_ref,
                      s_ref, vec_ref, v_ref, c_ref, look_ref, *, tc):
    @pl.when(pl.program_id(0) == 0)
    def _():
        s_ref[...] = jnp.zeros_like(s_ref)
        look_ref[:, tc] = jnp.zeros((look_ref.shape[0],) + look_ref.shape[2:], F32)

    nj, ni = s_ref.shape[0], s_ref.shape[1]
    quarter = LANES // 4

    def fold(x):
        hi = (lax.broadcasted_iota(jnp.int32, x.shape, 1) % (2 * quarter)) >= quarter
        return x + jnp.where(hi, pltpu.roll(x, quarter, 1), pltpu.roll(x, 3 * quarter, 1))

    def hsum(x):
        return jnp.sum(x, axis=0, keepdims=True)

    def jsum(x):
        return fold(jnp.broadcast_to(hsum(x), (SUBLANES, LANES)))[0:1]

    ka = ka_ref[...]
    kks = kks_ref[...]
    fwd_lane = lax.broadcasted_iota(jnp.int32, (nj, LANES), 1) < LANES // 2
    def merged(f, b, t):
        return jnp.where(fwd_lane, f[t], b[tc - 1 - t])

    cum = jnp.ones((nj, LANES), F32)
    for t in range(tc):
        w = merged(wf, wb, t)
        cum = cum * w
        kk = merged(kf, kb, t) * kks
        look_ref[0, t] = kk / jnp.maximum(jnp.sqrt(jsum(kk * kk)), 1e-12)
        look_ref[1, t] = w * merged(rf, rb, t)
        look_ref[2, t] = cum
    for t in range(tc):
        r, k, a = merged(rf, rb, t), merged(kf, kb, t), merged(af, ab, t)
        kk, kk_next, wr_next = look_ref[0, t], look_ref[0, t + 1], look_ref[1, t + 1]
        cum = look_ref[2, t]
        inv = 1.0 / cum
        kka = kk * a
        kd = k * (1.0 + (a - 1.0) * ka)
        vec_ref[0, t] = kka * inv
        vec_ref[1, t] = kd * inv
        vec_ref[2, t] = cum * kk_next
        vec_ref[3, t] = cum * wr_next
        c_ref[t, 0:1, :] = jsum(kka * r)
        c_ref[t, 1:2, :] = jsum(kd * r)
        c_ref[t, 2:3, :] = hsum(kka * kk_next)
        c_ref[t, 3:4, :] = hsum(kd * kk_next)
        c_ref[t, 4:5, :] = hsum(kka * wr_next)
        c_ref[t, 5:6, :] = hsum(kd * wr_next)
        v_ref[t] = jnp.concatenate([vf[t], vb[tc - 1 - t]], axis=-1)

    def state_sums(y1, y2):
        s1 = jnp.zeros((ni, LANES), F32)
        s2 = jnp.zeros((ni, LANES), F32)
        for j in range(nj):
            sj = s_ref[j]
            s1 = s1 + sj * y1(j)
            s2 = s2 + sj * y2(j)
        return s1, s2

    def step(t, carry):
        row = lambda n: (lambda j: vec_ref[n, t, j:j + 1, :])
        a1, a2 = state_sums(row(2), row(3))
        sa, o1 = fold(carry[0]), fold(carry[1])
        v = v_ref[t]
        out = o1 - sa * c_ref[t, 0:1, :] + v * c_ref[t, 1:2, :]
        of_ref[t] = out
        ob_ref[tc - 1 - t] = out
        nxt = (a1 - sa * c_ref[t, 2:3, :] + v * c_ref[t, 3:4, :],
               a2 - sa * c_ref[t, 4:5, :] + v * c_ref[t, 5:6, :])
        for j in range(nj):
            s_ref[j] = s_ref[j] - sa * row(0)(j) + v * row(1)(j)
        return nxt

    first = state_sums(lambda j: look_ref[0, 0, j:j + 1, :], lambda j: look_ref[1, 0, j:j + 1, :])
    lax.fori_loop(0, tc, step, first)
    for j in range(nj):
        s_ref[j] = s_ref[j] * look_ref[2, tc - 1, j:j + 1, :]


def rwkv_scan(r, k, w, a, v, ka, kks, t_ctx):
    ttot, nj, _ = r.shape
    ni, nc = v.shape[1], v.shape[2]
    tc = 64
    ncb, ntb = t_ctx // tc, ttot // tc
    fwd = lambda g: (g, 0, 0)
    bwd = lambda g: (jnp.where(g < ncb, ncb - 1 - g, ntb + ncb - 1 - g), 0, 0)
    jf, jb = pl.BlockSpec((tc, nj, LANES), fwd), pl.BlockSpec((tc, nj, LANES), bwd)
    vf, vb = pl.BlockSpec((tc, ni, nc), fwd), pl.BlockSpec((tc, ni, nc), bwd)
    of, ob = pl.BlockSpec((tc, ni, LANES), fwd), pl.BlockSpec((tc, ni, LANES), bwd)
    osd = jax.ShapeDtypeStruct((ttot, ni, LANES), F32)
    return pl.pallas_call(
        functools.partial(_rwkv_scan_kernel, tc=tc),
        grid=(ntb,),
        in_specs=[jf, jb] * 4 + [vf, vb, pl.BlockSpec(ka.shape, lambda g: (0, 0)),
                                 pl.BlockSpec(kks.shape, lambda g: (0, 0))],
        out_specs=[of, ob],
        out_shape=[osd, osd],
        scratch_shapes=[pltpu.VMEM((nj, ni, LANES), F32), pltpu.VMEM((4, tc, nj, LANES), F32),
                        pltpu.VMEM((tc, ni, LANES), F32), pltpu.VMEM((tc, SUBLANES, LANES), F32),
                        pltpu.VMEM((3, tc + 1, nj, LANES), F32)],
        compiler_params=_cparams(("arbitrary",)),
        name="rwkv_scan",
    )(r, r, k, k, w, w, a, a, v, v, ka, kks)


def _s5_kernel(uf_ref, ub_ref, bm_ref, cm_ref, a_ref, yf_ref, yb_ref, h_ref, bu_ref, ubr_ref, *, tt):
    @pl.when(pl.program_id(0) == 0)
    def _():
        h_ref[...] = jnp.zeros_like(h_ref)

    nsb = bm_ref.shape[0]
    kin = bm_ref.shape[1] // 2
    sw = bm_ref.shape[2]
    hw = sw // 2
    for t in range(tt):
        ubr_ref[t] = ub_ref[tt - 1 - t]
    nothing = jnp.zeros(uf_ref.shape, F32)
    uf = jnp.concatenate([uf_ref[...], nothing], axis=1)
    ub = jnp.concatenate([nothing, ubr_ref[...]], axis=1)
    for sb in range(nsb):
        lhs = jnp.concatenate([uf[:, :, sb * kin:(sb + 1) * kin], ub[:, :, sb * kin:(sb + 1) * kin]], axis=-1)
        lhs = lhs.reshape(tt * SUBLANES, 2 * kin).astype(BF16)
        bu_ref[:, sb * sw:(sb + 1) * sw] = _dot(lhs, bm_ref[sb])

    def step(t, hs):
        rows = pl.ds(pl.multiple_of(t * SUBLANES, SUBLANES), SUBLANES)
        out = []
        for sb in range(nsb):
            hr, hi = hs[2 * sb], hs[2 * sb + 1]
            lo = sb * sw
            ar = a_ref[0, :, lo:lo + hw]
            ai = a_ref[0, :, lo + hw:lo + sw]
            nr = ar * hr - ai * hi + bu_ref[rows, lo:lo + hw]
            ni = ar * hi + ai * hr + bu_ref[rows, lo + hw:lo + sw]
            bu_ref[rows, lo:lo + hw] = nr
            bu_ref[rows, lo + hw:lo + sw] = ni
            out += [nr, ni]
        return tuple(out)

    h0 = []
    for sb in range(nsb):
        h0 += [h_ref[:, sb * sw:sb * sw + hw], h_ref[:, sb * sw + hw:(sb + 1) * sw]]
    hs = lax.fori_loop(0, tt, step, tuple(h0))
    for sb in range(nsb):
        h_ref[:, sb * sw:sb * sw + hw] = hs[2 * sb]
        h_ref[:, sb * sw + hw:(sb + 1) * sw] = hs[2 * sb + 1]

    nout = cm_ref.shape[2] // 2
    fwd_row = (lax.broadcasted_iota(jnp.int32, (tt * SUBLANES, nout), 0) % SUBLANES) < SUBLANES // 2
    ys = []
    for sb in range(nsb):
        yy = _dot(bu_ref[:, sb * sw:(sb + 1) * sw].astype(BF16), cm_ref[sb])
        ys.append(jnp.where(fwd_row, yy[:, :nout], yy[:, nout:]))
    y = jnp.concatenate(ys, axis=1).reshape(tt, SUBLANES, nsb * nout)
    yf_ref[...] = y
    for t in range(tt):
        yb_ref[tt - 1 - t] = y[t]


def s5_scan(u, bm, cm, a, t_ctx):
    ttot, bsz, width = u.shape
    rows = 2 * bsz
    nsb, _, sw = bm.shape
    tt = 128
    ncb, ntb = t_ctx // tt, ttot // tt
    fmap = lambda g: (g, 0, 0)
    bmap = lambda g: (jnp.where(g < ncb, ncb - 1 - g, ntb + ncb - 1 - g), 0, 0)
    full = lambda x: pl.BlockSpec(x.shape, lambda g: (0,) * x.ndim)
    osd = jax.ShapeDtypeStruct((ttot, rows, width), F32)
    return pl.pallas_call(
        functools.partial(_s5_kernel, tt=tt),
        grid=(ntb,),
        in_specs=[pl.BlockSpec((tt, bsz, width), fmap), pl.BlockSpec((tt, bsz, width), bmap),
                  full(bm), full(cm), full(a)],
        out_specs=[pl.BlockSpec((tt, rows, width), fmap), pl.BlockSpec((tt, rows, width), bmap)],
        out_shape=[osd, osd],
        scratch_shapes=[pltpu.VMEM((rows, nsb * sw), F32), pltpu.VMEM((tt * rows, nsb * sw), F32),
                        pltpu.VMEM((tt, bsz, width), F32)],
        compiler_params=_cparams(("arbitrary",)),
        name="s5_scan",
    )(u, u, bm, cm, a)


def _even_out_kernel(x_ref, of_ref, ob_ref, bonus_ref, g_ref, ys_ref, u_ref, m5_ref,
                     ones_ref, gng_ref, gnb_ref, dskip_ref, wglu_ref, bglu_ref, wo1_ref, wo2_ref, o_ref):
    ones = ones_ref[...]
    inv = 1.0 / HEAD_DIM
    o = of_ref[0] + ob_ref[0]
    mean = _group_sum(o, ones) * inv
    oc = o - mean
    var = _group_sum(oc * oc, ones) * inv
    y1 = (oc * lax.rsqrt(var + GN_EPS) * gng_ref[...] + gnb_ref[...] + bonus_ref[0]) * g_ref[0]
    y = ys_ref[0] + dskip_ref[...] * u_ref[0]
    y = jax.nn.gelu(y)
    y2 = y * _sigmoid(_dot(y.astype(BF16), wglu_ref[...]) + bglu_ref[...])
    out = _dot(y1.astype(BF16), wo1_ref[...]) + _dot(y2.astype(BF16), wo2_ref[...])
    o_ref[0] = x_ref[0] + m5_ref[0] * out


def even_out(x, off, o_f, o_b, bonus, g, ys, u, m5, consts, tm):
    bsz, t, d = x.shape
    w = o_f.shape[2]
    tokd = pl.BlockSpec((1, tm, d), lambda b, i: (b, i, 0))
    tokw = pl.BlockSpec((1, tm, w), lambda b, i: (b, i + off, 0))
    toks = pl.BlockSpec((1, tm, u.shape[2]), lambda b, i: (b, i + off, 0))
    full = lambda a: pl.BlockSpec(a.shape, lambda b, i: (0,) * a.ndim)
    return pl.pallas_call(
        _even_out_kernel,
        grid=(bsz, t // tm),
        in_specs=[tokd, tokw, tokw, tokw, tokw, toks, toks,
                  pl.BlockSpec((1, 1, d), lambda b, i: (b, 0, 0))] + [full(a) for a in consts],
        out_specs=tokd,
        out_shape=jax.ShapeDtypeStruct((bsz, t, d), F32),
        compiler_params=_cparams(("parallel", "parallel")),
        name="even_out",
    )(x, o_f, o_b, bonus, g, ys, u, m5, *consts)


def _hy_in_kernel(x_ref, xp_ref, xn_ref, gain_ref, shift_ref, scale_ref, w_ref, cw_ref, cb_ref,
                  z_ref, zb_ref, g1_ref, g2_ref, *, nt, c):
    i = pl.program_id(1)
    gain, shift, scale = gain_ref[...], shift_ref[0], scale_ref[0]
    tm = x_ref.shape[1]
    halo = xp_ref.shape[1]
    h = jnp.concatenate([_rms_mod(xr[0], gain, shift, scale) for xr in (xp_ref, x_ref, xn_ref)], axis=0)
    h = h.astype(BF16)
    row = lax.broadcasted_iota(jnp.int32, (tm, c), 0)
    outs = (z_ref, g1_ref, g2_ref)
    for part in range(3):
        wp = w_ref[:, part * c:(part + 1) * c]
        p_ext = _dot(h, wp)
        p = p_ext[halo:halo + tm]
        pp = jnp.where(i > 0, p_ext[halo - 1:halo], 0.0)
        pn = jnp.where(i < nt - 1, p_ext[halo + tm:halo + tm + 1], 0.0)
        pm1 = jnp.where(row == 0, pp, pltpu.roll(p, 1, 0))
        pp1 = jnp.where(row == tm - 1, pn, pltpu.roll(p, tm - 1, 0))
        cw = cw_ref[:, part * c:(part + 1) * c]
        q = cw[0:1] * pm1 + cw[1:2] * p + cw[2:3] * pp1 + cb_ref[:, part * c:(part + 1) * c]
        outs[part][0] = q
        if part == 0:
            zb_ref[0] = q.astype(BF16)


def hyena_in(x, gain, shift, scale, w, conv_w, conv_b):
    bsz, t, d = x.shape
    c = w.shape[1] // 3
    tm = min(512, t)
    nt = t // tm
    hb = tm // 8
    nhb = t // 8
    vec = pl.BlockSpec((1, 1, d), lambda b, i: (b, 0, 0))
    tok = pl.BlockSpec((1, tm, c), lambda b, i: (b, i, 0))
    sds = jax.ShapeDtypeStruct((bsz, t, c), F32)
    return pl.pallas_call(
        functools.partial(_hy_in_kernel, nt=nt, c=c),
        grid=(bsz, nt),
        in_specs=[pl.BlockSpec((1, tm, d), lambda b, i: (b, i, 0)),
                  pl.BlockSpec((1, 8, d), lambda b, i: (b, jnp.maximum(i * hb - 1, 0), 0)),
                  pl.BlockSpec((1, 8, d), lambda b, i: (b, jnp.minimum((i + 1) * hb, nhb - 1), 0)),
                  pl.BlockSpec((1, d), lambda b, i: (0, 0)),
                  vec, vec,
                  pl.BlockSpec(w.shape, lambda b, i: (0, 0)),
                  pl.BlockSpec(conv_w.shape, lambda b, i: (0, 0)),
                  pl.BlockSpec((1, 3 * c), lambda b, i: (0, 0))],
        out_specs=[tok, tok, tok, tok],
        out_shape=[sds, jax.ShapeDtypeStruct((bsz, t, c), BF16), sds, sds],
        compiler_params=_cparams(("parallel", "parallel")),
        name="hyena_in",
    )(x, x, x, gain.reshape(1, d), shift, scale, w, conv_w, conv_b.reshape(1, 3 * c))


def _pack_complex(re, im):
    bits = lambda x: lax.bitcast_convert_type(x.astype(BF16).astype(F32), jnp.uint32)
    return lax.shift_right_logical(bits(re), jnp.uint32(16)) | bits(im)


def _unpack_complex(w):
    re = lax.bitcast_convert_type(lax.shift_left(w, jnp.uint32(16)), F32)
    im = lax.bitcast_convert_type(w & jnp.uint32(0xFFFF0000), F32)
    return jnp.concatenate([re, im], axis=0).astype(BF16)


def _fft_conv_kernel(z_ref, kf_ref, g_ref, w1_ref, g2i_ref, f1c_ref, twt_ref, o_ref, a_ref, t_ref,
                     *, n1, k1c):
    s = pl.program_id(2)
    n2 = FFT_N2
    nd = n1 // 2
    slab = n1 // k1c

    @pl.when(s == 0)
    def _():
        w1 = w1_ref[...]

        def body(j, carry):
            x = jnp.concatenate([z_ref[0, j], z_ref[1, j]], axis=0)
            a = _dot(w1, x)
            a_ref[pl.ds(pl.multiple_of(j * n1, n1), n1), :] = _pack_complex(a[:n1], a[n1:])
            return carry

        lax.fori_loop(0, n2, body, 0, unroll=16)

    @pl.when((s > 0) & (s <= k1c))
    def _():
        g2i = g2i_ref[...]
        base = (s - 1) * slab

        def body(kk, carry):
            k1 = base + kk
            ak = _unpack_complex(a_ref[pl.ds(k1, n2, stride=n1), :])
            x = _dot(g_ref[kk], ak)
            xr, xi = x[:n2], x[n2:]
            kr, ki = kf_ref[0, kk], kf_ref[1, kk]
            pr = xr * kr - xi * ki
            pi = xr * ki + xi * kr
            tt = _dot(g2i, jnp.concatenate([pr, pi], axis=0).astype(BF16))
            t_ref[pl.ds(pl.multiple_of(k1 * n2, n2), n2), :] = _pack_complex(tt[:n2], tt[n2:])
            return carry

        lax.fori_loop(0, slab, body, 0, unroll=16)

    @pl.when(s == k1c + 1)
    def _():
        f1r, f1i = f1c_ref[0], f1c_ref[1]

        def body(j, carry):
            twr = twt_ref[0, pl.ds(j, 1), :]
            twi = twt_ref[1, pl.ds(j, 1), :]
            wr = f1r * twr + f1i * twi
            wi = f1i * twr - f1r * twi
            w3 = jnp.concatenate([jnp.concatenate([wr, -wi], axis=1),
                                  jnp.concatenate([wi, wr], axis=1)], axis=0).astype(BF16)
            tn = _unpack_complex(t_ref[pl.ds(j, n1, stride=n2), :])
            y = _dot(w3, tn)
            o_ref[0, j] = y[:nd]
            o_ref[1, j] = y[nd:]
            return carry

        lax.fori_loop(0, n2, body, 0, unroll=16)


def _fft_consts(n1):
    n2 = FFT_N2
    n = n1 * n2
    nd = n1 // 2
    k1 = np.arange(n1)
    f1 = np.exp(-2j * np.pi * np.outer(k1, np.arange(nd)) / n1)
    w1 = np.block([[f1.real, -f1.imag], [f1.imag, f1.real]])
    f2 = np.exp(-2j * np.pi * np.outer(np.arange(n2), np.arange(n2)) / n2)
    f2c = np.conj(f2)
    g2i = np.block([[f2c.real, -f2c.imag], [f2c.imag, f2c.real]])
    f1c = np.exp(2j * np.pi * np.outer(np.arange(nd), k1) / n1) / n
    tw = np.exp(-2j * np.pi * np.outer(k1, np.arange(n2)) / n)
    cplx = lambda m: jnp.asarray(np.stack([m.real, m.imag]), F32)
    (f2r, f2i), (twr, twi) = cplx(f2), cplx(tw)
    gr = f2r[None] * twr[:, None, :] - f2i[None] * twi[:, None, :]
    gi = f2r[None] * twi[:, None, :] + f2i[None] * twr[:, None, :]
    gtw = jnp.concatenate([jnp.concatenate([gr, -gi], axis=2), jnp.concatenate([gi, gr], axis=2)], axis=1)
    return (gtw.astype(BF16), jnp.asarray(w1, BF16), jnp.asarray(g2i, BF16), cplx(f1c), cplx(tw.T))


def fft_conv(zt, kf, order):
    bsz, n2, nd, c = zt.shape
    n1 = 2 * nd
    cb = LANES
    k1c = 8 if n1 % 8 == 0 and n1 >= 64 else 2
    slab = n1 // k1c
    gtw, *consts = _fft_consts(n1)
    full = lambda a: pl.BlockSpec(a.shape, lambda j, p, s: (0,) * a.ndim)
    blk = pl.BlockSpec((2, n2, nd, cb), lambda j, p, s: (p, 0, 0, j))
    chunk = lambda j, p, s: jnp.clip(s - 1, 0, k1c - 1)
    return pl.pallas_call(
        functools.partial(_fft_conv_kernel, n1=n1, k1c=k1c),
        grid=(c // cb, bsz // 2, k1c + 2),
        in_specs=[blk,
                  pl.BlockSpec((None, 2, slab, n2, cb), lambda j, p, s: (order, 0, chunk(j, p, s), 0, j)),
                  pl.BlockSpec((slab, 2 * n2, 2 * n2), lambda j, p, s: (chunk(j, p, s), 0, 0))]
                 + [full(a) for a in consts],
        out_specs=blk,
        out_shape=jax.ShapeDtypeStruct(zt.shape, F32),
        scratch_shapes=[pltpu.VMEM((n2 * n1, cb), jnp.uint32), pltpu.VMEM((n1 * n2, cb), jnp.uint32)],
        compiler_params=_cparams(("parallel", "parallel", "arbitrary")),
        name="fft_conv",
    )(zt, kf, gtw, *consts)


def _fft_fwd_kernel(z_ref, scale_ref, g_ref, w1_ref, o_ref, a_ref, *, n1, k1c):
    s = pl.program_id(2)
    n2 = FFT_N2
    slab = n1 // k1c
    cb = o_ref.shape[4]

    @pl.when(s == 0)
    def _():
        w1 = w1_ref[...]

        def body(j, carry):
            a = _dot(w1, jnp.concatenate([z_ref[0, j], z_ref[1, j]], axis=1))
            packed = _pack_complex(a[:n1], a[n1:])
            rows = pl.ds(pl.multiple_of(j * n1, n1), n1)
            a_ref[0, rows, :] = packed[:, :cb]
            a_ref[1, rows, :] = packed[:, cb:]
            return carry

        lax.fori_loop(0, n2, body, 0, unroll=16)

    @pl.when(s > 0)
    def _():
        base = (s - 1) * slab

        def body(kk, carry):
            k1 = base + kk
            rows = pl.ds(k1, n2, stride=n1)
            ak = jnp.concatenate([_unpack_complex(a_ref[0, rows, :]), _unpack_complex(a_ref[1, rows, :])], axis=1)
            x = _dot(g_ref[kk], ak)
            pos, neg = x[:, :cb], x[:, cb:]
            scale = scale_ref[0]
            o_ref[0, 0, kk] = (pos[:n2] + neg[:n2]) * scale
            o_ref[0, 1, kk] = (pos[n2:] - neg[n2:]) * scale
            return carry

        lax.fori_loop(0, slab, body, 0, unroll=16)


def filter_spectrum(taps, scale):
    orders, _, n2, nd, c = taps.shape
    n1 = 2 * nd
    cb = LANES
    k1c = 8 if n1 % 8 == 0 and n1 >= 64 else 2
    slab = n1 // k1c
    gtw, w1 = _fft_consts(n1)[:2]
    w1 = w1[:, :nd]
    full = lambda a: pl.BlockSpec(a.shape, lambda j, r, s: (0,) * a.ndim)
    return pl.pallas_call(
        functools.partial(_fft_fwd_kernel, n1=n1, k1c=k1c),
        grid=(c // cb, orders, k1c + 1),
        in_specs=[pl.BlockSpec((None, 2, n2, nd, cb), lambda j, r, s: (r, 0, 0, 0, j)),
                  pl.BlockSpec((1, 1, cb), lambda j, r, s: (r, 0, j)),
                  pl.BlockSpec((slab, 2 * n2, 2 * n2), lambda j, r, s: (jnp.maximum(s - 1, 0), 0, 0)),
                  full(w1)],
        out_specs=pl.BlockSpec((1, 2, slab, n2, cb), lambda j, r, s: (r, 0, jnp.maximum(s - 1, 0), 0, j)),
        out_shape=jax.ShapeDtypeStruct((orders, 2, n1, n2, c), F32),
        scratch_shapes=[pltpu.VMEM((2, n2 * n1, cb), jnp.uint32)],
        compiler_params=_cparams(("parallel", "parallel", "arbitrary")),
        name="fft_forward",
    )(taps, scale, gtw, w1)


def _hy_filter_kernel(feat_ref, fw1_ref, fb1_ref, fw2_ref, fb2_ref, fw3_ref, fb3_ref, fw4_ref, freq_ref,
                      delta_ref, z_ref, ss_ref):
    @pl.when(pl.program_id(0) == 0)
    def _():
        ss_ref[...] = jnp.zeros_like(ss_ref)

    feats = feat_ref[...]
    t = feats[:, 0:1]
    fr = freq_ref[...]
    h = jnp.sin(fr * (_dot_exact(feats, fw1_ref[...]) + fb1_ref[...]))
    h = jnp.sin(fr * (_dot_exact(h, fw2_ref[...]) + fb2_ref[...]))
    h = jnp.sin(fr * (_dot_exact(h, fw3_ref[...]) + fb3_ref[...])).astype(BF16)
    decay = jnp.exp(-t * delta_ref[...])
    width = delta_ref.shape[1]
    for r in range(z_ref.shape[0]):
        filt = _dot(h, fw4_ref[:, r * width:(r + 1) * width]) * decay
        if r % 2 == 1:
            filt = jnp.where(t == 0.0, 0.0, filt)
        z_ref[r] = filt.astype(BF16)
        ss_ref[r] += jnp.sum(filt * filt, axis=0, keepdims=True)


def hyena_filter(t, fw1, fb1, fw2, fb2, fw3, fb3, fw4, freq, width):
    nd = t // FFT_N2
    pos = (jnp.arange(FFT_N2, dtype=F32)[:, None] + FFT_N2 * jnp.arange(nd, dtype=F32)[None, :]).reshape(-1, 1)
    tt = pos / max(t - 1, 1)
    ang = 2 * math.pi * pos / t
    nb = (HY_EMB - 1) // 2
    bands = jnp.linspace(1e-4, nb - 1, nb, dtype=F32)[None]
    feats = jnp.concatenate([tt, jnp.cos(bands * ang), -jnp.sin(bands * ang),
                             jnp.zeros((t, LANES - HY_EMB), F32)], axis=-1)
    deltas = jnp.abs(jnp.linspace(HY_MIN_DECAY, HY_MAX_DECAY, width, dtype=F32)).reshape(1, width)
    rows = 2 * HY_ORDER
    tm = min(512, t)
    hid = fw1.shape[1]
    full = lambda a: pl.BlockSpec(a.shape, lambda i: (0,) * a.ndim)
    fw1p = jnp.concatenate([fw1, jnp.zeros((LANES - HY_EMB, hid), F32)], axis=0)
    vecs = [fw1p, fb1.reshape(1, hid), fw2, fb2.reshape(1, hid), fw3, fb3.reshape(1, hid),
            fw4.astype(BF16), freq.reshape(1, hid), deltas]
    return pl.pallas_call(
        _hy_filter_kernel,
        grid=(t // tm,),
        in_specs=[pl.BlockSpec((tm, LANES), lambda i: (i, 0))] + [full(a) for a in vecs],
        out_specs=[pl.BlockSpec((rows, tm, width), lambda i: (0, i, 0)),
                   pl.BlockSpec((rows, 1, width), lambda i: (0, 0, 0))],
        out_shape=[jax.ShapeDtypeStruct((rows, t, width), BF16), jax.ShapeDtypeStruct((rows, 1, width), F32)],
        compiler_params=_cparams(("arbitrary",)),
        name="hyena_filter",
    )(feats, *vecs)


def hyena_filter_spectrum_fft(t, fw1, fb1, fw2, fb2, fw3, fb3, fw4, freq, width):
    taps, ss = hyena_filter(t, fw1, fb1, fw2, fb2, fw3, fb3, fw4, freq, width)
    nd = t // FFT_N2
    scale = lax.rsqrt(ss.reshape(HY_ORDER, 2, width).sum(axis=1, keepdims=True) + 1e-6)
    return filter_spectrum(taps.reshape(HY_ORDER, 2, FFT_N2, nd, width), scale)


def _dft_conv_kernel(z_ref, kf_ref, fw_ref, iv_ref, o_ref, *, t):
    x = jnp.concatenate([z_ref[0], z_ref[1]], axis=0)
    spec = _dot(fw_ref[...], x)
    xr, xi = spec[:2 * t], spec[2 * t:]
    kr, ki = kf_ref[0], kf_ref[1]
    pr = xr * kr - xi * ki
    pi = xr * ki + xi * kr
    y = _dot(iv_ref[...], jnp.concatenate([pr, pi], axis=0).astype(BF16))
    o_ref[0] = y[:t]
    o_ref[1] = y[t:]


def dft_conv(zb, kf):
    bsz, t, c = zb.shape
    n = 2 * t
    f = np.exp(-2j * np.pi * np.outer(np.arange(n), np.arange(t)) / n)
    fw = np.block([[f.real, -f.imag], [f.imag, f.real]])
    fi = np.exp(2j * np.pi * np.outer(np.arange(t), np.arange(n)) / n) / n
    iv = np.block([[fi.real, -fi.imag], [fi.imag, fi.real]])
    fw, iv = jnp.asarray(fw, BF16), jnp.asarray(iv, BF16)
    cb = LANES
    blk = pl.BlockSpec((2, t, cb), lambda j, p: (p, 0, j))
    return pl.pallas_call(
        functools.partial(_dft_conv_kernel, t=t),
        grid=(c // cb, bsz // 2),
        in_specs=[blk, pl.BlockSpec((2, n, cb), lambda j, p: (0, 0, j)),
                  pl.BlockSpec(fw.shape, lambda j, p: (0, 0)), pl.BlockSpec(iv.shape, lambda j, p: (0, 0))],
        out_specs=blk,
        out_shape=jax.ShapeDtypeStruct(zb.shape, F32),
        compiler_params=_cparams(("parallel", "parallel")),
        name="dft_conv",
    )(zb, kf, fw, iv)


def _conv_rows(conv_ref):
    if len(conv_ref.shape) == 3:
        return conv_ref[0]
    return jnp.concatenate([conv_ref[0, :, k, :] for k in range(conv_ref.shape[2])], axis=0)


def _conv_spec(conv, tm, c):
    if conv.ndim == 3:
        return pl.BlockSpec((1, tm, c), lambda b, i: (b, i, 0))
    return pl.BlockSpec((1, FFT_N2, tm // FFT_N2, c), lambda b, i: (b, 0, i, 0))


def _hy_gate_kernel(conv_ref, z_ref, gate_ref, bias_ref, o_ref, ob_ref):
    y = gate_ref[0] * (_conv_rows(conv_ref) + bias_ref[...] * z_ref[0])
    o_ref[0] = y
    ob_ref[0] = y.astype(BF16)


def hyena_gate(conv, z, gate, bias):
    bsz, t, c = z.shape
    tm = min(SUBLANES * FFT_N2, t)
    tok = pl.BlockSpec((1, tm, c), lambda b, i: (b, i, 0))
    return pl.pallas_call(
        _hy_gate_kernel,
        grid=(bsz, t // tm),
        in_specs=[_conv_spec(conv, tm, c), tok, tok, pl.BlockSpec((1, c), lambda b, i: (0, 0))],
        out_specs=[tok, tok],
        out_shape=[jax.ShapeDtypeStruct(z.shape, F32), jax.ShapeDtypeStruct(z.shape, BF16)],
        compiler_params=_cparams(("parallel", "parallel")),
        name="hyena_gate",
    )(conv, z, gate, bias.reshape(1, c))


def _hy_out_kernel(x_ref, conv_ref, z_ref, gate_ref, bias_ref, w_ref, m5_ref, o_ref):
    y = gate_ref[0] * (_conv_rows(conv_ref) + bias_ref[...] * z_ref[0])
    o_ref[0] = x_ref[0] + m5_ref[0] * _dot(y.astype(BF16), w_ref[...])


def hyena_out(x, conv, z, gate, bias, w, m5):
    bsz, t, d = x.shape
    c = z.shape[2]
    tm = min(SUBLANES * FFT_N2, t)
    tokd = pl.BlockSpec((1, tm, d), lambda b, i: (b, i, 0))
    tokc = pl.BlockSpec((1, tm, c), lambda b, i: (b, i, 0))
    return pl.pallas_call(
        _hy_out_kernel,
        grid=(bsz, t // tm),
        in_specs=[tokd, _conv_spec(conv, tm, c), tokc, tokc, pl.BlockSpec((1, c), lambda b, i: (0, 0)),
                  pl.BlockSpec(w.shape, lambda b, i: (0, 0)), pl.BlockSpec((1, 1, d), lambda b, i: (b, 0, 0))],
        out_specs=tokd,
        out_shape=jax.ShapeDtypeStruct(x.shape, F32),
        compiler_params=_cparams(("parallel", "parallel")),
        name="hyena_out",
    )(x, conv, z, gate, bias.reshape(1, c), w, m5)


def _hyena_filter_spectrum(t, fw1, fb1, fw2, fb2, fw3, fb3, fw4, freq, width):
    pos = jnp.arange(t, dtype=F32)[:, None]
    tt = pos / max(t - 1, 1)
    ang = 2 * math.pi * pos / t
    nb = (HY_EMB - 1) // 2
    bands = jnp.linspace(1e-4, nb - 1, nb, dtype=F32)[None]
    feats = jnp.concatenate([tt, jnp.cos(bands * ang), -jnp.sin(bands * ang)], axis=-1)
    hdn = jnp.sin(freq * (feats @ fw1 + fb1))
    hdn = jnp.sin(freq * (hdn @ fw2 + fb2))
    hdn = jnp.sin(freq * (hdn @ fw3 + fb3))
    filt = (hdn @ fw4).reshape(t, HY_ORDER, 2, width)
    deltas = jnp.abs(jnp.linspace(HY_MIN_DECAY, HY_MAX_DECAY, width, dtype=F32))
    filt = filt * jnp.exp(-tt[:, :, None, None] * deltas)
    fwd, bwd = filt[:, :, 0], filt[:, :, 1]
    kern = jnp.concatenate([fwd, jnp.zeros_like(fwd[:1]), bwd[:0:-1]], axis=0)
    kern = kern * lax.rsqrt(jnp.sum(kern * kern, axis=0, keepdims=True) + 1e-6)
    spec = jnp.fft.fft(kern, axis=0)
    return jnp.stack([jnp.real(spec), jnp.imag(spec)], axis=1).transpose(2, 1, 0, 3).astype(F32)


def _block_diag_pair(m):
    z = jnp.zeros_like(m[0])
    return jnp.concatenate([jnp.concatenate([m[0], z], axis=1), jnp.concatenate([z, m[1]], axis=1)], axis=0)


def _even_mixer(lat, cx, ml, mc, gain, prm, ctx_out):
    (w_in, mu, w0, w_up, a0, a_up, g_up, k_k, k_a, r_k, gn_g, gn_b,
     lam_re, lam_im, log_dt, b_re, b_im, c_re, c_im, d_skip, w_glu, b_glu, w_out) = prm
    bsz, t_lat, d = lat.shape
    t_ctx = cx.shape[1]
    width = k_k.shape[0]
    heads = width // HEAD_DIM
    n_cols = mu.shape[0]
    s5w = d_skip.shape[0]
    ttot = t_ctx + t_lat
    nch = bsz * heads
    tm = t_ctx
    assert tm % GRID_W == 0 and t_lat % tm == 0 and 4 * nch == LANES and 2 * bsz == SUBLANES

    p_all, u_all = mod_matmul_stream(cx, lat, gain, mc, ml, w_in.astype(BF16), tm, n_cols)

    head_of = np.arange(width) // HEAD_DIM
    ones = jnp.asarray(head_of[:, None] == head_of[None, :], BF16)
    feat_consts = (mu.reshape(1, -1), _block_diag_pair(w_up).astype(BF16), w0.reshape(1, -1),
                   _block_diag_pair(a_up).astype(BF16), a0.reshape(1, -1), g_up.astype(BF16),
                   k_a.reshape(1, -1), r_k.reshape(1, -1), ones)
    v2, g, bonus, r2, k2, w2, a2 = rwkv_features(p_all, n_cols, width, feat_consts, tm)

    def key_major(x):
        x = x.reshape(2, bsz, ttot, heads, 2, HALF_HEAD)
        return x.transpose(2, 5, 0, 4, 1, 3).reshape(ttot, HALF_HEAD, LANES)

    def value_major(x):
        x = x.reshape(2, bsz, ttot, heads, HEAD_DIM)
        return x.transpose(2, 4, 0, 1, 3).reshape(ttot, HEAD_DIM, 2 * nch)

    def key_const(x):
        x = x.reshape(heads, 2, HALF_HEAD).transpose(2, 1, 0)[:, None, :, None, :]
        return jnp.broadcast_to(x, (HALF_HEAD, 2, 2, bsz, heads)).reshape(HALF_HEAD, LANES)

    o_f, o_b = rwkv_scan(key_major(r2), key_major(k2), key_major(w2), key_major(a2), value_major(v2),
                         key_const(k_a), key_const(k_k), t_ctx)

    def token_major(o, lane0):
        o = o[..., lane0:lane0 + nch].reshape(ttot, HEAD_DIM, bsz, heads)
        return o.transpose(2, 0, 3, 1).reshape(bsz, ttot, width)

    o_f, o_b = token_major(o_f, 0), token_major(o_b, LANES // 2)

    ng = lam_re.shape[1]
    gps = ng // S5_SUPER
    lam = lax.complex(lam_re, lam_im)
    dt = jnp.exp(log_dt)[..., None]
    a_bar = jnp.exp(lam * dt)
    b_bar = ((a_bar - 1) / lam)[..., None] * lax.complex(b_re, b_im)
    eye = jnp.eye(gps, dtype=F32)

    def b_mat(x):
        x = x.reshape(2, S5_SUPER, gps, S5_STATE, S5_GROUP)
        m = jnp.einsum('dsgph,gk->sdghkp', x, eye)
        return m.reshape(S5_SUPER, 2 * gps * S5_GROUP, gps * S5_STATE)

    def c_mat(x):
        x = x.reshape(2, S5_SUPER, gps, S5_GROUP, S5_STATE)
        m = jnp.einsum('dsghp,gk->skpdgh', x, eye)
        return m.reshape(S5_SUPER, gps * S5_STATE, 2 * gps * S5_GROUP)

    bm = jnp.concatenate([b_mat(jnp.real(b_bar)), b_mat(jnp.imag(b_bar))], axis=2).astype(BF16)
    cm = jnp.concatenate([c_mat(c_re), -c_mat(c_im)], axis=1).astype(BF16)

    def a_rows(x):
        x = x.reshape(2, 1, S5_SUPER, gps * S5_STATE)
        return jnp.broadcast_to(x, (2, bsz, S5_SUPER, gps * S5_STATE)).reshape(2 * bsz, S5_SUPER, -1)

    a_arr = jnp.concatenate([a_rows(jnp.real(a_bar)), a_rows(jnp.imag(a_bar))], axis=2)
    a_arr = a_arr.reshape(1, 2 * bsz, -1)

    y_f, y_b = s5_scan(u_all.transpose(1, 0, 2), bm, cm, a_arr, t_ctx)
    ys = (y_f[:, :bsz] + y_b[:, bsz:]).transpose(1, 0, 2)

    wo = w_out.astype(BF16)
    out_consts = (ones, gn_g.reshape(1, -1), gn_b.reshape(1, -1), d_skip.reshape(1, -1),
                  w_glu.astype(BF16), b_glu.reshape(1, -1), wo[:width], wo[width:])
    lat = even_out(lat, 1, o_f, o_b, bonus, g, ys, u_all, ml[5], out_consts, tm)
    if ctx_out:
        cx = even_out(cx, 0, o_f, o_b, bonus, g, ys, u_all, mc[5], out_consts, tm)
    return lat, cx


def _hyena_mixer(x, gain, m, prm):
    (w_in, conv_w, conv_b, fw1, fb1, fw2, fb2, fw3, fb3, fw4, freq, bias_d, w_out) = prm
    bsz, t, d = x.shape
    c = w_out.shape[0]
    use_fft = t % (FFT_N2 * 2) == 0 and t >= 4 * FFT_N2
    if use_fft:
        kf = hyena_filter_spectrum_fft(t, fw1, fb1, fw2, fb2, fw3, fb3, fw4, freq, c)
    else:
        kf = _hyena_filter_spectrum(t, fw1, fb1, fw2, fb2, fw3, fb3, fw4, freq, c)
    z, zb, g1, g2 = hyena_in(x, gain, m[3], m[4], w_in.astype(BF16), conv_w, conv_b)
    gates = (g1, g2)
    for n in range(HY_ORDER):
        if use_fft:
            nd = t // FFT_N2
            zt = zb.reshape(bsz, nd, FFT_N2, c).transpose(0, 2, 1, 3)
            conv = fft_conv(zt, kf, n)
        else:
            conv = dft_conv(zb, kf[n])
        if n < HY_ORDER - 1:
            z, zb = hyena_gate(conv, z, gates[n], bias_d[n])
        else:
            return hyena_out(x, conv, z, gates[n], bias_d[n], w_out.astype(BF16), m[5])


def kernel(x, c, ctx, c_ctx, norm_g, ada_w, ada_b, ffn_wg, ffn_wu, ffn_wd, final_g, ev_w_in, ev_mu, ev_w0, ev_w_up, ev_a0, ev_a_up, ev_g_up, ev_k_k, ev_k_a, ev_r_k, ev_gn_g, ev_gn_b, ev_lam_re, ev_lam_im, ev_log_dt, ev_b_re, ev_b_im, ev_c_re, ev_c_im, ev_d, ev_w_glu, ev_b_glu, ev_w_out, od_w_in, od_conv_w, od_conv_b, od_fw1, od_fb1, od_fw2, od_fb2, od_fw3, od_fb3, od_fw4, od_freq, od_bias, od_w_out):
    depth = norm_g.shape[0]
    bsz, _, d = x.shape
    n_even = (depth + 1) // 2
    last_ctx = 2 * (n_even - 1)

    cond8 = jnp.concatenate([c, c_ctx[None], jnp.zeros((8 - bsz - 1, d), F32)], axis=0)
    mods = ada_mods_all(cond8, ada_w, ada_b)

    wg, wu, wd = ffn_wg.astype(BF16), ffn_wu.astype(BF16), ffn_wd.astype(BF16)
    lat, cx = x, ctx
    for l in range(depth):
        run_ctx = l <= last_ctx
        ctx_out = l < last_ctx
        i = l // 2
        ml = [mods[l, :bsz, None, k * d:(k + 1) * d] for k in range(N_MOD)]
        mc = [jnp.broadcast_to(mods[l, bsz:bsz + 1, None, k * d:(k + 1) * d], (bsz, 1, d))
              for k in range(N_MOD)]
        lat = ffn_half(lat, norm_g[l, 0], ml[0], ml[1], ml[2], wg[l, 0], wu[l, 0], wd[l, 0])
        if run_ctx:
            cx = ffn_half(cx, norm_g[l, 0], mc[0], mc[1], mc[2], wg[l, 0], wu[l, 0], wd[l, 0])
        if l % 2 == 0:
            prm = (ev_w_in[i], ev_mu[i], ev_w0[i], ev_w_up[i], ev_a0[i], ev_a_up[i], ev_g_up[i],
                   ev_k_k[i], ev_k_a[i], ev_r_k[i], ev_gn_g[i], ev_gn_b[i],
                   ev_lam_re[i], ev_lam_im[i], ev_log_dt[i], ev_b_re[i], ev_b_im[i], ev_c_re[i], ev_c_im[i],
                   ev_d[i], ev_w_glu[i], ev_b_glu[i], ev_w_out[i])
            lat, cx = _even_mixer(lat, cx, ml, mc, norm_g[l, 1], prm, ctx_out)
        else:
            prm = (od_w_in[i], od_conv_w[i], od_conv_b[i], od_fw1[i], od_fb1[i], od_fw2[i], od_fb2[i],
                   od_fw3[i], od_fb3[i], od_fw4[i], od_freq[i], od_bias[i], od_w_out[i])
            lat = _hyena_mixer(lat, norm_g[l, 1], ml, prm)
            if ctx_out:
                cx = _hyena_mixer(cx, norm_g[l, 1], mc, prm)
        fin = final_g if l == depth - 1 else None
        lat = ffn_half(lat, norm_g[l, 2], ml[6], ml[7], ml[8], wg[l, 1], wu[l, 1], wd[l, 1], fin)
        if ctx_out:
            cx = ffn_half(cx, norm_g[l, 2], mc[6], mc[7], mc[8], wg[l, 1], wu[l, 1], wd[l, 1])
    return lat
```

```python
import functools
import math

import numpy as np
import jax
import jax.numpy as jnp
from jax import lax
from jax.experimental import pallas as pl
from jax.experimental.pallas import tpu as pltpu

F32 = jnp.float32
BF16 = jnp.bfloat16
HIGHEST = lax.Precision.HIGHEST

N_MOD = 9
NORM_EPS = 1e-6
GN_EPS = 64e-5
GRID_W = 64
HEAD_DIM = 64
HALF_HEAD = HEAD_DIM // 2
S5_GROUP = 16
S5_STATE = 64
S5_SUPER = 4
HY_ORDER = 2
HY_EMB = 33
HY_MIN_DECAY = math.log(1e-2) / 1.5
HY_MAX_DECAY = math.log(1e-2) / 0.3
LANES = 128
SUBLANES = 8
MXU_DIM = 256
FFT_N2 = 128
VMEM_LIMIT = 56 * 1024 * 1024


def _cparams(sem, vmem=VMEM_LIMIT):
    return pltpu.CompilerParams(dimension_semantics=sem, vmem_limit_bytes=vmem)


def _dot(a, b):
    return jnp.dot(a, b, preferred_element_type=F32)


def _dot_exact(a, b):
    return jnp.dot(a, b, preferred_element_type=F32, precision=HIGHEST)


def _group_sum(a, ones):
    hi = a.astype(BF16)
    lo = (a - hi.astype(F32)).astype(BF16)
    return _dot(hi, ones) + _dot(lo, ones)


def _rms_mod(x, gain, shift, scale):
    ms = jnp.mean(x * x, axis=-1, keepdims=True)
    return x * lax.rsqrt(ms + NORM_EPS) * gain * (1.0 + scale) + shift


def _sigmoid(x):
    return 1.0 / (1.0 + jnp.exp(-x))


def _silu(x):
    return x * _sigmoid(x)


def _ada_kernel(c_ref, w_ref, b_ref, o_ref):
    s = _silu(c_ref[...])
    o_ref[0] = _dot(s.astype(BF16), w_ref[0].astype(BF16)) + b_ref[0]


def ada_mods_all(cond8, ada_w, ada_b):
    depth, d, n = ada_w.shape
    tn = n // 8
    return pl.pallas_call(
        _ada_kernel,
        grid=(depth, n // tn),
        in_specs=[pl.BlockSpec((8, d), lambda l, j: (0, 0)),
                  pl.BlockSpec((1, d, tn), lambda l, j: (l, 0, j)),
                  pl.BlockSpec((1, 1, tn), lambda l, j: (l, 0, j))],
        out_specs=pl.BlockSpec((1, 8, tn), lambda l, j: (l, 0, j)),
        out_shape=jax.ShapeDtypeStruct((depth, 8, n), F32),
        compiler_params=_cparams(("parallel", "parallel")),
        name="ada_mods",
    )(cond8, ada_w, ada_b.reshape(depth, 1, n))


def _ffn_kernel(x_ref, gain_ref, shift_ref, scale_ref, gate_ref, wg_ref, wu_ref, wd_ref, fg_ref,
                o_ref, *, final_norm, chunks):
    x = x_ref[0]
    h = _rms_mod(x, gain_ref[...], shift_ref[0], scale_ref[0]).astype(BF16)
    acc = None
    for lo, hi in chunks:
        g = _dot(h, wg_ref[:, lo:hi])
        u = _dot(h, wu_ref[:, lo:hi])
        part = _dot((_silu(g) * u).astype(BF16), wd_ref[lo:hi, :])
        acc = part if acc is None else acc + part
    y = x + 0.5 * gate_ref[0] * acc
    if final_norm:
        ms = jnp.mean(y * y, axis=-1, keepdims=True)
        y = y * lax.rsqrt(ms + NORM_EPS) * fg_ref[...]
    o_ref[0] = y


def ffn_half(x, gain, shift, scale, gate, wg, wu, wd, final_g=None):
    bsz, t, d = x.shape
    ff = wg.shape[1]
    tm = min(1024, t)
    step = 4 * MXU_DIM
    chunks = tuple((lo, min(lo + step, ff)) for lo in range(0, ff, step))
    fg = jnp.ones((1, d), F32) if final_g is None else final_g.reshape(1, d)
    vec = pl.BlockSpec((1, 1, d), lambda b, i: (b, 0, 0))
    resident = lambda a: pl.BlockSpec(a.shape, lambda b, i: (0, 0), pipeline_mode=pl.Buffered(1))
    return pl.pallas_call(
        functools.partial(_ffn_kernel, final_norm=final_g is not None, chunks=chunks),
        grid=(bsz, t // tm),
        in_specs=[pl.BlockSpec((1, tm, d), lambda b, i: (b, i, 0)),
                  pl.BlockSpec((1, d), lambda b, i: (0, 0)),
                  vec, vec, vec, resident(wg), resident(wu), resident(wd),
                  pl.BlockSpec((1, d), lambda b, i: (0, 0))],
        out_specs=pl.BlockSpec((1, tm, d), lambda b, i: (b, i, 0)),
        out_shape=jax.ShapeDtypeStruct((bsz, t, d), F32),
        compiler_params=_cparams(("parallel", "parallel")),
        name="ffn_half",
    )(x, gain.reshape(1, d), shift, scale, gate, wg, wu, wd, fg)


def _modmm_kernel(cx_ref, lat_ref, gain_ref, shc_ref, scc_ref, shl_ref, scl_ref, w_ref, o_ref, u_ref, *, nc):
    is_ctx = pl.program_id(1) < nc
    x = jnp.where(is_ctx, cx_ref[0], lat_ref[0])
    shift = jnp.where(is_ctx, shc_ref[0], shl_ref[0])
    scale = jnp.where(is_ctx, scc_ref[0], scl_ref[0])
    h = _rms_mod(x, gain_ref[...], shift, scale).astype(BF16)
    n1 = o_ref.shape[2]
    o_ref[0] = _dot(h, w_ref[:, :n1])
    u_ref[0] = _dot(h, w_ref[:, n1:])


def mod_matmul_stream(cx, lat, gain, mc, ml, w, tm, n_first):
    bsz, t_ctx, d = cx.shape
    t_lat = lat.shape[1]
    n = w.shape[1]
    nc = t_ctx // tm
    nt = nc + t_lat // tm
    vec = pl.BlockSpec((1, 1, d), lambda b, i: (b, 0, 0))
    return pl.pallas_call(
        functools.partial(_modmm_kernel, nc=nc),
        grid=(bsz, nt),
        in_specs=[pl.BlockSpec((1, tm, d), lambda b, i: (b, jnp.minimum(i, nc - 1), 0)),
                  pl.BlockSpec((1, tm, d), lambda b, i: (b, jnp.maximum(i - nc, 0), 0)),
                  pl.BlockSpec((1, d), lambda b, i: (0, 0)),
                  vec, vec, vec, vec,
                  pl.BlockSpec((d, n), lambda b, i: (0, 0))],
        out_specs=[pl.BlockSpec((1, tm, n_first), lambda b, i: (b, i, 0)),
                   pl.BlockSpec((1, tm, n - n_first), lambda b, i: (b, i, 0))],
        out_shape=[jax.ShapeDtypeStruct((bsz, t_ctx + t_lat, n_first), F32),
                   jax.ShapeDtypeStruct((bsz, t_ctx + t_lat, n - n_first), F32)],
        compiler_params=_cparams(("parallel", "parallel")),
        name="mod_matmul",
    )(cx, lat, gain.reshape(1, d), mc[3], mc[4], ml[3], ml[4], w)


def _rwkv_feat_kernel(p_ref, pu_ref, pd_ref, mu_ref, wup_ref, w0_ref, aup_ref, a0_ref, gup_ref,
                      ka_ref, rk_ref, ones_ref,
                      v_o, g_o, bonus_o, r_o, k_o, w_o, a_o, *, nt, width):
    i = pl.program_id(1)
    is_ctx = i == 0
    p = p_ref[0]
    tm = p.shape[0]
    row = lax.broadcasted_iota(jnp.int32, p.shape, 0)
    lane = lax.broadcasted_iota(jnp.int32, p.shape, 1) % 4
    prev = pltpu.roll(p, 1, 0)
    nxt = pltpu.roll(p, tm - 1, 0)
    col = jnp.where(is_ctx, row, row % GRID_W)
    last = jnp.where(is_ctx, tm - 1, GRID_W - 1)
    left = jnp.where(col == 0, 0.0, prev)
    right = jnp.where(col == last, 0.0, nxt)
    up_halo = jnp.where(i > 1, pu_ref[0], 0.0)
    dn_halo = jnp.where(i < nt - 1, pd_ref[0], 0.0)
    up = jnp.where(is_ctx, left, jnp.concatenate([up_halo, p[:tm - GRID_W]], axis=0))
    down = jnp.where(is_ctx, right, jnp.concatenate([p[GRID_W:], dn_halo], axis=0))
    shifted = jnp.where(lane == 0, left, jnp.where(lane == 1, right, jnp.where(lane == 2, up, down)))
    q = p + mu_ref[...] * (shifted - p)

    w = width
    r, k, v = q[:, :w], q[:, w:2 * w], q[:, 2 * w:3 * w]
    wd = q[:, 3 * w:3 * w + LANES]
    ad = q[:, 3 * w + LANES:3 * w + 2 * LANES]
    gd = q[:, 3 * w + 2 * LANES:3 * w + 3 * LANES]

    zlin = w0_ref[...] + _dot(jnp.tanh(wd).astype(BF16), wup_ref[...])
    decay = jnp.exp(-math.exp(-0.5) * _sigmoid(zlin))
    a = _sigmoid(a0_ref[...] + _dot(ad.astype(BF16), aup_ref[...]))
    g_o[0] = _dot(_sigmoid(gd).astype(BF16), gup_ref[...])

    ones = ones_ref[...]
    for d in range(2):
        v_o[d, 0] = v
        r_o[d, 0] = r
        k_o[d, 0] = k
    ksum = jnp.zeros_like(k)
    for d in range(2):
        a_d = a[:, d * w:(d + 1) * w]
        ksum = ksum + k * (1.0 + (a_d - 1.0) * ka_ref[...])
        w_o[d, 0] = decay[:, d * w:(d + 1) * w]
        a_o[d, 0] = a_d
    bonus_o[0] = _group_sum(r * ksum * rk_ref[...], ones) * v


def rwkv_features(p, n_cols, width, consts, tm):
    bsz, t, _ = p.shape
    nt = t // tm
    hb = tm // GRID_W
    nhb = t // GRID_W
    w = width
    full = lambda a: pl.BlockSpec(a.shape, lambda b, i: (0,) * a.ndim)
    tok = pl.BlockSpec((1, tm, w), lambda b, i: (b, i, 0))
    tok2 = pl.BlockSpec((2, 1, tm, w), lambda b, i: (0, b, i, 0))
    sds = jax.ShapeDtypeStruct((bsz, t, w), F32)
    sds2 = jax.ShapeDtypeStruct((2, bsz, t, w), F32)
    return pl.pallas_call(
        functools.partial(_rwkv_feat_kernel, nt=nt, width=w),
        grid=(bsz, nt),
        in_specs=[pl.BlockSpec((1, tm, n_cols), lambda b, i: (b, i, 0)),
                  pl.BlockSpec((1, GRID_W, n_cols), lambda b, i: (b, jnp.maximum(i * hb - 1, 0), 0)),
                  pl.BlockSpec((1, GRID_W, n_cols), lambda b, i: (b, jnp.minimum((i + 1) * hb, nhb - 1), 0))]
                 + [full(a) for a in consts],
        out_specs=[tok2, tok, tok, tok2, tok2, tok2, tok2],
        out_shape=[sds2, sds, sds, sds2, sds2, sds2, sds2],
        compiler_params=_cparams(("parallel", "parallel")),
        name="rwkv_features",
    )(p, p, p, *consts)


def _rwkv_scan_kernel(rf, rb, kf, kb, wf, wb, af, ab, vf, vb, ka_ref, kks_ref, of_ref, ob_ref,
                      s_ref, vec_ref, v_ref, c_ref, look_ref, *, tc):
    @pl.when(pl.program_id(0) == 0)
    def _():
        s_ref[...] = jnp.zeros_like(s_ref)
        look_ref[:, tc] = jnp.zeros((look_ref.shape[0],) + look_ref.shape[2:], F32)

    nj, ni = s_ref.shape[0], s_ref.shape[1]
    quarter = LANES // 4

    def fold(x):
        hi = (lax.broadcasted_iota(jnp.int32, x.shape, 1) % (2 * quarter)) >= quarter
        return x + jnp.where(hi, pltpu.roll(x, quarter, 1), pltpu.roll(x, 3 * quarter, 1))

    def hsum(x):
        return jnp.sum(x, axis=0, keepdims=True)

    def jsum(x):
        return fold(jnp.broadcast_to(hsum(x), (SUBLANES, LANES)))[0:1]

    ka = ka_ref[...]
    kks = kks_ref[...]
    fwd_lane = lax.broadcasted_iota(jnp.int32, (nj, LANES), 1) < LANES // 2
    def merged(f, b, t):
        return jnp.where(fwd_lane, f[t], b[tc - 1 - t])

    cum = jnp.ones((nj, LANES), F32)
    for t in range(tc):
        w = merged(wf, wb, t)
        cum = cum * w
        kk = merged(kf, kb, t) * kks
        look_ref[0, t] = kk / jnp.maximum(jnp.sqrt(jsum(kk * kk)), 1e-12)
        look_ref[1, t] = w * merged(rf, rb, t)
        look_ref[2, t] = cum
    for t in range(tc):
        r, k, a = merged(rf, rb, t), merged(kf, kb, t), merged(af, ab, t)
        kk, kk_next, wr_next = look_ref[0, t], look_ref[0, t + 1], look_ref[1, t + 1]
        cum = look_ref[2, t]
        inv = 1.0 / cum
        kka = kk * a
        kd = k * (1.0 + (a - 1.0) * ka)
        vec_ref[0, t] = kka * inv
        vec_ref[1, t] = kd * inv
        vec_ref[2, t] = cum * kk_next
        vec_ref[3, t] = cum * wr_next
        c_ref[t, 0:1, :] = jsum(kka * r)
        c_ref[t, 1:2, :] = jsum(kd * r)
        c_ref[t, 2:3, :] = hsum(kka * kk_next)
        c_ref[t, 3:4, :] = hsum(kd * kk_next)
        c_ref[t, 4:5, :] = hsum(kka * wr_next)
        c_ref[t, 5:6, :] = hsum(kd * wr_next)
        v_ref[t] = jnp.concatenate([vf[t], vb[tc - 1 - t]], axis=-1)

    def state_sums(y1, y2):
        s1 = jnp.zeros((ni, LANES), F32)
        s2 = jnp.zeros((ni, LANES), F32)
        for j in range(nj):
            sj = s_ref[j]
            s1 = s1 + sj * y1(j)
            s2 = s2 + sj * y2(j)
        return s1, s2

    def step(t, carry):
        row = lambda n: (lambda j: vec_ref[n, t, j:j + 1, :])
        a1, a2 = state_sums(row(2), row(3))
        sa, o1 = fold(carry[0]), fold(carry[1])
        v = v_ref[t]
        out = o1 - sa * c_ref[t, 0:1, :] + v * c_ref[t, 1:2, :]
        of_ref[t] = out
        ob_ref[tc - 1 - t] = out
        nxt = (a1 - sa * c_ref[t, 2:3, :] + v * c_ref[t, 3:4, :],
               a2 - sa * c_ref[t, 4:5, :] + v * c_ref[t, 5:6, :])
        for j in range(nj):
            s_ref[j] = s_ref[j] - sa * row(0)(j) + v * row(1)(j)
        return nxt

    first = state_sums(lambda j: look_ref[0, 0, j:j + 1, :], lambda j: look_ref[1, 0, j:j + 1, :])
    lax.fori_loop(0, tc, step, first)
    for j in range(nj):
        s_ref[j] = s_ref[j] * look_ref[2, tc - 1, j:j + 1, :]


def rwkv_scan(r, k, w, a, v, ka, kks, t_ctx):
    ttot, nj, _ = r.shape
    ni, nc = v.shape[1], v.shape[2]
    tc = 64
    ncb, ntb = t_ctx // tc, ttot // tc
    fwd = lambda g: (g, 0, 0)
    bwd = lambda g: (jnp.where(g < ncb, ncb - 1 - g, ntb + ncb - 1 - g), 0, 0)
    jf, jb = pl.BlockSpec((tc, nj, LANES), fwd), pl.BlockSpec((tc, nj, LANES), bwd)
    vf, vb = pl.BlockSpec((tc, ni, nc), fwd), pl.BlockSpec((tc, ni, nc), bwd)
    of, ob = pl.BlockSpec((tc, ni, LANES), fwd), pl.BlockSpec((tc, ni, LANES), bwd)
    osd = jax.ShapeDtypeStruct((ttot, ni, LANES), F32)
    return pl.pallas_call(
        functools.partial(_rwkv_scan_kernel, tc=tc),
        grid=(ntb,),
        in_specs=[jf, jb] * 4 + [vf, vb, pl.BlockSpec(ka.shape, lambda g: (0, 0)),
                                 pl.BlockSpec(kks.shape, lambda g: (0, 0))],
        out_specs=[of, ob],
        out_shape=[osd, osd],
        scratch_shapes=[pltpu.VMEM((nj, ni, LANES), F32), pltpu.VMEM((4, tc, nj, LANES), F32),
                        pltpu.VMEM((tc, ni, LANES), F32), pltpu.VMEM((tc, SUBLANES, LANES), F32),
                        pltpu.VMEM((3, tc + 1, nj, LANES), F32)],
        compiler_params=_cparams(("arbitrary",)),
        name="rwkv_scan",
    )(r, r, k, k, w, w, a, a, v, v, ka, kks)


def _s5_kernel(uf_ref, ub_ref, bm_ref, cm_ref, a_ref, yf_ref, yb_ref, h_ref, bu_ref, ubr_ref, *, tt):
    @pl.when(pl.program_id(0) == 0)
    def _():
        h_ref[...] = jnp.zeros_like(h_ref)

    nsb = bm_ref.shape[0]
    kin = bm_ref.shape[1] // 2
    sw = bm_ref.shape[2]
    hw = sw // 2
    for t in range(tt):
        ubr_ref[t] = ub_ref[tt - 1 - t]
    nothing = jnp.zeros(uf_ref.shape, F32)
    uf = jnp.concatenate([uf_ref[...], nothing], axis=1)
    ub = jnp.concatenate([nothing, ubr_ref[...]], axis=1)
    for sb in range(nsb):
        lhs = jnp.concatenate([uf[:, :, sb * kin:(sb + 1) * kin], ub[:, :, sb * kin:(sb + 1) * kin]], axis=-1)
        lhs = lhs.reshape(tt * SUBLANES, 2 * kin).astype(BF16)
        bu_ref[:, sb * sw:(sb + 1) * sw] = _dot(lhs, bm_ref[sb])

    def step(t, hs):
        rows = pl.ds(pl.multiple_of(t * SUBLANES, SUBLANES), SUBLANES)
        out = []
        for sb in range(nsb):
            hr, hi = hs[2 * sb], hs[2 * sb + 1]
            lo = sb * sw
            ar = a_ref[0, :, lo:lo + hw]
            ai = a_ref[0, :, lo + hw:lo + sw]
            nr = ar * hr - ai * hi + bu_ref[rows, lo:lo + hw]
            ni = ar * hi + ai * hr + bu_ref[rows, lo + hw:lo + sw]
            bu_ref[rows, lo:lo + hw] = nr
            bu_ref[rows, lo + hw:lo + sw] = ni
            out += [nr, ni]
        return tuple(out)

    h0 = []
    for sb in range(nsb):
        h0 += [h_ref[:, sb * sw:sb * sw + hw], h_ref[:, sb * sw + hw:(sb + 1) * sw]]
    hs = lax.fori_loop(0, tt, step, tuple(h0))
    for sb in range(nsb):
        h_ref[:, sb * sw:sb * sw + hw] = hs[2 * sb]
        h_ref[:, sb * sw + hw:(sb + 1) * sw] = hs[2 * sb + 1]

    nout = cm_ref.shape[2] // 2
    fwd_row = (lax.broadcasted_iota(jnp.int32, (tt * SUBLANES, nout), 0) % SUBLANES) < SUBLANES // 2
    ys = []
    for sb in range(nsb):
        yy = _dot(bu_ref[:, sb * sw:(sb + 1) * sw].astype(BF16), cm_ref[sb])
        ys.append(jnp.where(fwd_row, yy[:, :nout], yy[:, nout:]))
    y = jnp.concatenate(ys, axis=1).reshape(tt, SUBLANES, nsb * nout)
    yf_ref[...] = y
    for t in range(tt):
        yb_ref[tt - 1 - t] = y[t]


def s5_scan(u, bm, cm, a, t_ctx):
    ttot, bsz, width = u.shape
    rows = 2 * bsz
    nsb, _, sw = bm.shape
    tt = 128
    ncb, ntb = t_ctx // tt, ttot // tt
    fmap = lambda g: (g, 0, 0)
    bmap = lambda g: (jnp.where(g < ncb, ncb - 1 - g, ntb + ncb - 1 - g), 0, 0)
    full = lambda x: pl.BlockSpec(x.shape, lambda g: (0,) * x.ndim)
    osd = jax.ShapeDtypeStruct((ttot, rows, width), F32)
    return pl.pallas_call(
        functools.partial(_s5_kernel, tt=tt),
        grid=(ntb,),
        in_specs=[pl.BlockSpec((tt, bsz, width), fmap), pl.BlockSpec((tt, bsz, width), bmap),
                  full(bm), full(cm), full(a)],
        out_specs=[pl.BlockSpec((tt, rows, width), fmap), pl.BlockSpec((tt, rows, width), bmap)],
        out_shape=[osd, osd],
        scratch_shapes=[pltpu.VMEM((rows, nsb * sw), F32), pltpu.VMEM((tt * rows, nsb * sw), F32),
                        pltpu.VMEM((tt, bsz, width), F32)],
        compiler_params=_cparams(("arbitrary",)),
        name="s5_scan",
    )(u, u, bm, cm, a)


def _even_out_kernel(x_ref, of_ref, ob_ref, bonus_ref, g_ref, ys_ref, u_ref, m5_ref,
                     ones_ref, gng_ref, gnb_ref, dskip_ref, wglu_ref, bglu_ref, wo1_ref, wo2_ref, o_ref):
    ones = ones_ref[...]
    inv = 1.0 / HEAD_DIM
    o = of_ref[0] + ob_ref[0]
    mean = _group_sum(o, ones) * inv
    oc = o - mean
    var = _group_sum(oc * oc, ones) * inv
    y1 = (oc * lax.rsqrt(var + GN_EPS) * gng_ref[...] + gnb_ref[...] + bonus_ref[0]) * g_ref[0]
    y = ys_ref[0] + dskip_ref[...] * u_ref[0]
    y = jax.nn.gelu(y)
    y2 = y * _sigmoid(_dot(y.astype(BF16), wglu_ref[...]) + bglu_ref[...])
    out = _dot(y1.astype(BF16), wo1_ref[...]) + _dot(y2.astype(BF16), wo2_ref[...])
    o_ref[0] = x_ref[0] + m5_ref[0] * out


def even_out(x, off, o_f, o_b, bonus, g, ys, u, m5, consts, tm):
    bsz, t, d = x.shape
    w = o_f.shape[2]
    tokd = pl.BlockSpec((1, tm, d), lambda b, i: (b, i, 0))
    tokw = pl.BlockSpec((1, tm, w), lambda b, i: (b, i + off, 0))
    toks = pl.BlockSpec((1, tm, u.shape[2]), lambda b, i: (b, i + off, 0))
    full = lambda a: pl.BlockSpec(a.shape, lambda b, i: (0,) * a.ndim)
    return pl.pallas_call(
        _even_out_kernel,
        grid=(bsz, t // tm),
        in_specs=[tokd, tokw, tokw, tokw, tokw, toks, toks,
                  pl.BlockSpec((1, 1, d), lambda b, i: (b, 0, 0))] + [full(a) for a in consts],
        out_specs=tokd,
        out_shape=jax.ShapeDtypeStruct((bsz, t, d), F32),
        compiler_params=_cparams(("parallel", "parallel")),
        name="even_out",
    )(x, o_f, o_b, bonus, g, ys, u, m5, *consts)


def _hy_in_kernel(x_ref, xp_ref, xn_ref, gain_ref, shift_ref, scale_ref, w_ref, cw_ref, cb_ref,
                  z_ref, zb_ref, g1_ref, g2_ref, *, nt, c):
    i = pl.program_id(1)
    gain, shift, scale = gain_ref[...], shift_ref[0], scale_ref[0]
    tm = x_ref.shape[1]
    halo = xp_ref.shape[1]
    h = jnp.concatenate([_rms_mod(xr[0], gain, shift, scale) for xr in (xp_ref, x_ref, xn_ref)], axis=0)
    h = h.astype(BF16)
    row = lax.broadcasted_iota(jnp.int32, (tm, c), 0)
    outs = (z_ref, g1_ref, g2_ref)
    for part in range(3):
        wp = w_ref[:, part * c:(part + 1) * c]
        p_ext = _dot(h, wp)
        p = p_ext[halo:halo + tm]
        pp = jnp.where(i > 0, p_ext[halo - 1:halo], 0.0)
        pn = jnp.where(i < nt - 1, p_ext[halo + tm:halo + tm + 1], 0.0)
        pm1 = jnp.where(row == 0, pp, pltpu.roll(p, 1, 0))
        pp1 = jnp.where(row == tm - 1, pn, pltpu.roll(p, tm - 1, 0))
        cw = cw_ref[:, part * c:(part + 1) * c]
        q = cw[0:1] * pm1 + cw[1:2] * p + cw[2:3] * pp1 + cb_ref[:, part * c:(part + 1) * c]
        outs[part][0] = q
        if part == 0:
            zb_ref[0] = q.astype(BF16)


def hyena_in(x, gain, shift, scale, w, conv_w, conv_b):
    bsz, t, d = x.shape
    c = w.shape[1] // 3
    tm = min(512, t)
    nt = t // tm
    hb = tm // 8
    nhb = t // 8
    vec = pl.BlockSpec((1, 1, d), lambda b, i: (b, 0, 0))
    tok = pl.BlockSpec((1, tm, c), lambda b, i: (b, i, 0))
    sds = jax.ShapeDtypeStruct((bsz, t, c), F32)
    return pl.pallas_call(
        functools.partial(_hy_in_kernel, nt=nt, c=c),
        grid=(bsz, nt),
        in_specs=[pl.BlockSpec((1, tm, d), lambda b, i: (b, i, 0)),
                  pl.BlockSpec((1, 8, d), lambda b, i: (b, jnp.maximum(i * hb - 1, 0), 0)),
                  pl.BlockSpec((1, 8, d), lambda b, i: (b, jnp.minimum((i + 1) * hb, nhb - 1), 0)),
                  pl.BlockSpec((1, d), lambda b, i: (0, 0)),
                  vec, vec,
                  pl.BlockSpec(w.shape, lambda b, i: (0, 0)),
                  pl.BlockSpec(conv_w.shape, lambda b, i: (0, 0)),
                  pl.BlockSpec((1, 3 * c), lambda b, i: (0, 0))],
        out_specs=[tok, tok, tok, tok],
        out_shape=[sds, jax.ShapeDtypeStruct((bsz, t, c), BF16), sds, sds],
        compiler_params=_cparams(("parallel", "parallel")),
        name="hyena_in",
    )(x, x, x, gain.reshape(1, d), shift, scale, w, conv_w, conv_b.reshape(1, 3 * c))


def _pack_complex(re, im):
    bits = lambda x: lax.bitcast_convert_type(x.astype(BF16).astype(F32), jnp.uint32)
    return lax.shift_right_logical(bits(re), jnp.uint32(16)) | bits(im)


def _unpack_complex(w):
    re = lax.bitcast_convert_type(lax.shift_left(w, jnp.uint32(16)), F32)
    im = lax.bitcast_convert_type(w & jnp.uint32(0xFFFF0000), F32)
    return jnp.concatenate([re, im], axis=0).astype(BF16)


def _fft_conv_kernel(z_ref, kf_ref, g_ref, w1_ref, g2i_ref, f1c_ref, twt_ref, o_ref, a_ref, t_ref,
                     *, n1, k1c):
    s = pl.program_id(2)
    n2 = FFT_N2
    nd = n1 // 2
    slab = n1 // k1c

    @pl.when(s == 0)
    def _():
        w1 = w1_ref[...]

        def body(j, carry):
            x = jnp.concatenate([z_ref[0, j], z_ref[1, j]], axis=0)
            a = _dot(w1, x)
            a_ref[pl.ds(pl.multiple_of(j * n1, n1), n1), :] = _pack_complex(a[:n1], a[n1:])
            return carry

        lax.fori_loop(0, n2, body, 0, unroll=16)

    @pl.when((s > 0) & (s <= k1c))
    def _():
        g2i = g2i_ref[...]
        base = (s - 1) * slab

        def body(kk, carry):
            k1 = base + kk
            ak = _unpack_complex(a_ref[pl.ds(k1, n2, stride=n1), :])
            x = _dot(g_ref[kk], ak)
            xr, xi = x[:n2], x[n2:]
            kr, ki = kf_ref[0, kk], kf_ref[1, kk]
            pr = xr * kr - xi * ki
            pi = xr * ki + xi * kr
            tt = _dot(g2i, jnp.concatenate([pr, pi], axis=0).astype(BF16))
            t_ref[pl.ds(pl.multiple_of(k1 * n2, n2), n2), :] = _pack_complex(tt[:n2], tt[n2:])
            return carry

        lax.fori_loop(0, slab, body, 0, unroll=16)

    @pl.when(s == k1c + 1)
    def _():
        f1r, f1i = f1c_ref[0], f1c_ref[1]

        def body(j, carry):
            twr = twt_ref[0, pl.ds(j, 1), :]
            twi = twt_ref[1, pl.ds(j, 1), :]
            wr = f1r * twr + f1i * twi
            wi = f1i * twr - f1r * twi
            w3 = jnp.concatenate([jnp.concatenate([wr, -wi], axis=1),
                                  jnp.concatenate([wi, wr], axis=1)], axis=0).astype(BF16)
            tn = _unpack_complex(t_ref[pl.ds(j, n1, stride=n2), :])
            y = _dot(w3, tn)
            o_ref[0, j] = y[:nd]
            o_ref[1, j] = y[nd:]
            return carry

        lax.fori_loop(0, n2, body, 0, unroll=16)


def _fft_consts(n1):
    n2 = FFT_N2
    n = n1 * n2
    nd = n1 // 2
    k1 = np.arange(n1)
    f1 = np.exp(-2j * np.pi * np.outer(k1, np.arange(nd)) / n1)
    w1 = np.block([[f1.real, -f1.imag], [f1.imag, f1.real]])
    f2 = np.exp(-2j * np.pi * np.outer(np.arange(n2), np.arange(n2)) / n2)
    f2c = np.conj(f2)
    g2i = np.block([[f2c.real, -f2c.imag], [f2c.imag, f2c.real]])
    f1c = np.exp(2j * np.pi * np.outer(np.arange(nd), k1) / n1) / n
    tw = np.exp(-2j * np.pi * np.outer(k1, np.arange(n2)) / n)
    cplx = lambda m: jnp.asarray(np.stack([m.real, m.imag]), F32)
    (f2r, f2i), (twr, twi) = cplx(f2), cplx(tw)
    gr = f2r[None] * twr[:, None, :] - f2i[None] * twi[:, None, :]
    gi = f2r[None] * twi[:, None, :] + f2i[None] * twr[:, None, :]
    gtw = jnp.concatenate([jnp.concatenate([gr, -gi], axis=2), jnp.concatenate([gi, gr], axis=2)], axis=1)
    return (gtw.astype(BF16), jnp.asarray(w1, BF16), jnp.asarray(g2i, BF16), cplx(f1c), cplx(tw.T))


def fft_conv(zt, kf, order):
    bsz, n2, nd, c = zt.shape
    n1 = 2 * nd
    cb = LANES
    k1c = 8 if n1 % 8 == 0 and n1 >= 64 else 2
    slab = n1 // k1c
    gtw, *consts = _fft_consts(n1)
    full = lambda a: pl.BlockSpec(a.shape, lambda j, p, s: (0,) * a.ndim)
    blk = pl.BlockSpec((2, n2, nd, cb), lambda j, p, s: (p, 0, 0, j))
    chunk = lambda j, p, s: jnp.clip(s - 1, 0, k1c - 1)
    return pl.pallas_call(
        functools.partial(_fft_conv_kernel, n1=n1, k1c=k1c),
        grid=(c // cb, bsz // 2, k1c + 2),
        in_specs=[blk,
                  pl.BlockSpec((None, 2, slab, n2, cb), lambda j, p, s: (order, 0, chunk(j, p, s), 0, j)),
                  pl.BlockSpec((slab, 2 * n2, 2 * n2), lambda j, p, s: (chunk(j, p, s), 0, 0))]
                 + [full(a) for a in consts],
        out_specs=blk,
        out_shape=jax.ShapeDtypeStruct(zt.shape, F32),
        scratch_shapes=[pltpu.VMEM((n2 * n1, cb), jnp.uint32), pltpu.VMEM((n1 * n2, cb), jnp.uint32)],
        compiler_params=_cparams(("parallel", "parallel", "arbitrary")),
        name="fft_conv",
    )(zt, kf, gtw, *consts)


def _fft_fwd_kernel(z_ref, scale_ref, g_ref, w1_ref, o_ref, a_ref, *, n1, k1c):
    s = pl.program_id(2)
    n2 = FFT_N2
    slab = n1 // k1c
    cb = o_ref.shape[4]

    @pl.when(s == 0)
    def _():
        w1 = w1_ref[...]

        def body(j, carry):
            a = _dot(w1, jnp.concatenate([z_ref[0, j], z_ref[1, j]], axis=1))
            packed = _pack_complex(a[:n1], a[n1:])
            rows = pl.ds(pl.multiple_of(j * n1, n1), n1)
            a_ref[0, rows, :] = packed[:, :cb]
            a_ref[1, rows, :] = packed[:, cb:]
            return carry

        lax.fori_loop(0, n2, body, 0, unroll=16)

    @pl.when(s > 0)
    def _():
        base = (s - 1) * slab

        def body(kk, carry):
            k1 = base + kk
            rows = pl.ds(k1, n2, stride=n1)
            ak = jnp.concatenate([_unpack_complex(a_ref[0, rows, :]), _unpack_complex(a_ref[1, rows, :])], axis=1)
            x = _dot(g_ref[kk], ak)
            pos, neg = x[:, :cb], x[:, cb:]
            scale = scale_ref[0]
            o_ref[0, 0, kk] = (pos[:n2] + neg[:n2]) * scale
            o_ref[0, 1, kk] = (pos[n2:] - neg[n2:]) * scale
            return carry

        lax.fori_loop(0, slab, body, 0, unroll=16)


def filter_spectrum(taps, scale):
    orders, _, n2, nd, c = taps.shape
    n1 = 2 * nd
    cb = LANES
    k1c = 8 if n1 % 8 == 0 and n1 >= 64 else 2
    slab = n1 // k1c
    gtw, w1 = _fft_consts(n1)[:2]
    w1 = w1[:, :nd]
    full = lambda a: pl.BlockSpec(a.shape, lambda j, r, s: (0,) * a.ndim)
    return pl.pallas_call(
        functools.partial(_fft_fwd_kernel, n1=n1, k1c=k1c),
        grid=(c // cb, orders, k1c + 1),
        in_specs=[pl.BlockSpec((None, 2, n2, nd, cb), lambda j, r, s: (r, 0, 0, 0, j)),
                  pl.BlockSpec((1, 1, cb), lambda j, r, s: (r, 0, j)),
                  pl.BlockSpec((slab, 2 * n2, 2 * n2), lambda j, r, s: (jnp.maximum(s - 1, 0), 0, 0)),
                  full(w1)],
        out_specs=pl.BlockSpec((1, 2, slab, n2, cb), lambda j, r, s: (r, 0, jnp.maximum(s - 1, 0), 0, j)),
        out_shape=jax.ShapeDtypeStruct((orders, 2, n1, n2, c), F32),
        scratch_shapes=[pltpu.VMEM((2, n2 * n1, cb), jnp.uint32)],
        compiler_params=_cparams(("parallel", "parallel", "arbitrary")),
        name="fft_forward",
    )(taps, scale, gtw, w1)


def _hy_filter_kernel(feat_ref, fw1_ref, fb1_ref, fw2_ref, fb2_ref, fw3_ref, fb3_ref, fw4_ref, freq_ref,
                      delta_ref, z_ref, ss_ref):
    @pl.when(pl.program_id(0) == 0)
    def _():
        ss_ref[...] = jnp.zeros_like(ss_ref)

    feats = feat_ref[...]
    t = feats[:, 0:1]
    fr = freq_ref[...]
    h = jnp.sin(fr * (_dot_exact(feats, fw1_ref[...]) + fb1_ref[...]))
    h = jnp.sin(fr * (_dot_exact(h, fw2_ref[...]) + fb2_ref[...]))
    h = jnp.sin(fr * (_dot_exact(h, fw3_ref[...]) + fb3_ref[...])).astype(BF16)
    decay = jnp.exp(-t * delta_ref[...])
    width = delta_ref.shape[1]
    for r in range(z_ref.shape[0]):
        filt = _dot(h, fw4_ref[:, r * width:(r + 1) * width]) * decay
        if r % 2 == 1:
            filt = jnp.where(t == 0.0, 0.0, filt)
        z_ref[r] = filt.astype(BF16)
        ss_ref[r] += jnp.sum(filt * filt, axis=0, keepdims=True)


def hyena_filter(t, fw1, fb1, fw2, fb2, fw3, fb3, fw4, freq, width):
    nd = t // FFT_N2
    pos = (jnp.arange(FFT_N2, dtype=F32)[:, None] + FFT_N2 * jnp.arange(nd, dtype=F32)[None, :]).reshape(-1, 1)
    tt = pos / max(t - 1, 1)
    ang = 2 * math.pi * pos / t
    nb = (HY_EMB - 1) // 2
    bands = jnp.linspace(1e-4, nb - 1, nb, dtype=F32)[None]
    feats = jnp.concatenate([tt, jnp.cos(bands * ang), -jnp.sin(bands * ang),
                             jnp.zeros((t, LANES - HY_EMB), F32)], axis=-1)
    deltas = jnp.abs(jnp.linspace(HY_MIN_DECAY, HY_MAX_DECAY, width, dtype=F32)).reshape(1, width)
    rows = 2 * HY_ORDER
    tm = min(512, t)
    hid = fw1.shape[1]
    full = lambda a: pl.BlockSpec(a.shape, lambda i: (0,) * a.ndim)
    fw1p = jnp.concatenate([fw1, jnp.zeros((LANES - HY_EMB, hid), F32)], axis=0)
    vecs = [fw1p, fb1.reshape(1, hid), fw2, fb2.reshape(1, hid), fw3, fb3.reshape(1, hid),
            fw4.astype(BF16), freq.reshape(1, hid), deltas]
    return pl.pallas_call(
        _hy_filter_kernel,
        grid=(t // tm,),
        in_specs=[pl.BlockSpec((tm, LANES), lambda i: (i, 0))] + [full(a) for a in vecs],
        out_specs=[pl.BlockSpec((rows, tm, width), lambda i: (0, i, 0)),
                   pl.BlockSpec((rows, 1, width), lambda i: (0, 0, 0))],
        out_shape=[jax.ShapeDtypeStruct((rows, t, width), BF16), jax.ShapeDtypeStruct((rows, 1, width), F32)],
        compiler_params=_cparams(("arbitrary",)),
        name="hyena_filter",
    )(feats, *vecs)


def hyena_filter_spectrum_fft(t, fw1, fb1, fw2, fb2, fw3, fb3, fw4, freq, width):
    taps, ss = hyena_filter(t, fw1, fb1, fw2, fb2, fw3, fb3, fw4, freq, width)
    nd = t // FFT_N2
    scale = lax.rsqrt(ss.reshape(HY_ORDER, 2, width).sum(axis=1, keepdims=True) + 1e-6)
    return filter_spectrum(taps.reshape(HY_ORDER, 2, FFT_N2, nd, width), scale)


def _dft_conv_kernel(z_ref, kf_ref, fw_ref, iv_ref, o_ref, *, t):
    x = jnp.concatenate([z_ref[0], z_ref[1]], axis=0)
    spec = _dot(fw_ref[...], x)
    xr, xi = spec[:2 * t], spec[2 * t:]
    kr, ki = kf_ref[0], kf_ref[1]
    pr = xr * kr - xi * ki
    pi = xr * ki + xi * kr
    y = _dot(iv_ref[...], jnp.concatenate([pr, pi], axis=0).astype(BF16))
    o_ref[0] = y[:t]
    o_ref[1] = y[t:]


def dft_conv(zb, kf):
    bsz, t, c = zb.shape
    n = 2 * t
    f = np.exp(-2j * np.pi * np.outer(np.arange(n), np.arange(t)) / n)
    fw = np.block([[f.real, -f.imag], [f.imag, f.real]])
    fi = np.exp(2j * np.pi * np.outer(np.arange(t), np.arange(n)) / n) / n
    iv = np.block([[fi.real, -fi.imag], [fi.imag, fi.real]])
    fw, iv = jnp.asarray(fw, BF16), jnp.asarray(iv, BF16)
    cb = LANES
    blk = pl.BlockSpec((2, t, cb), lambda j, p: (p, 0, j))
    return pl.pallas_call(
        functools.partial(_dft_conv_kernel, t=t),
        grid=(c // cb, bsz // 2),
        in_specs=[blk, pl.BlockSpec((2, n, cb), lambda j, p: (0, 0, j)),
                  pl.BlockSpec(fw.shape, lambda j, p: (0, 0)), pl.BlockSpec(iv.shape, lambda j, p: (0, 0))],
        out_specs=blk,
        out_shape=jax.ShapeDtypeStruct(zb.shape, F32),
        compiler_params=_cparams(("parallel", "parallel")),
        name="dft_conv",
    )(zb, kf, fw, iv)


def _conv_rows(conv_ref):
    if len(conv_ref.shape) == 3:
        return conv_ref[0]
    return jnp.concatenate([conv_ref[0, :, k, :] for k in range(conv_ref.shape[2])], axis=0)


def _conv_spec(conv, tm, c):
    if conv.ndim == 3:
        return pl.BlockSpec((1, tm, c), lambda b, i: (b, i, 0))
    return pl.BlockSpec((1, FFT_N2, tm // FFT_N2, c), lambda b, i: (b, 0, i, 0))


def _hy_gate_kernel(conv_ref, z_ref, gate_ref, bias_ref, o_ref, ob_ref):
    y = gate_ref[0] * (_conv_rows(conv_ref) + bias_ref[...] * z_ref[0])
    o_ref[0] = y
    ob_ref[0] = y.astype(BF16)


def hyena_gate(conv, z, gate, bias):
    bsz, t, c = z.shape
    tm = min(SUBLANES * FFT_N2, t)
    tok = pl.BlockSpec((1, tm, c), lambda b, i: (b, i, 0))
    return pl.pallas_call(
        _hy_gate_kernel,
        grid=(bsz, t // tm),
        in_specs=[_conv_spec(conv, tm, c), tok, tok, pl.BlockSpec((1, c), lambda b, i: (0, 0))],
        out_specs=[tok, tok],
        out_shape=[jax.ShapeDtypeStruct(z.shape, F32), jax.ShapeDtypeStruct(z.shape, BF16)],
        compiler_params=_cparams(("parallel", "parallel")),
        name="hyena_gate",
    )(conv, z, gate, bias.reshape(1, c))


def _hy_out_kernel(x_ref, conv_ref, z_ref, gate_ref, bias_ref, w_ref, m5_ref, o_ref):
    y = gate_ref[0] * (_conv_rows(conv_ref) + bias_ref[...] * z_ref[0])
    o_ref[0] = x_ref[0] + m5_ref[0] * _dot(y.astype(BF16), w_ref[...])


def hyena_out(x, conv, z, gate, bias, w, m5):
    bsz, t, d = x.shape
    c = z.shape[2]
    tm = min(SUBLANES * FFT_N2, t)
    tokd = pl.BlockSpec((1, tm, d), lambda b, i: (b, i, 0))
    tokc = pl.BlockSpec((1, tm, c), lambda b, i: (b, i, 0))
    return pl.pallas_call(
        _hy_out_kernel,
        grid=(bsz, t // tm),
        in_specs=[tokd, _conv_spec(conv, tm, c), tokc, tokc, pl.BlockSpec((1, c), lambda b, i: (0, 0)),
                  pl.BlockSpec(w.shape, lambda b, i: (0, 0)), pl.BlockSpec((1, 1, d), lambda b, i: (b, 0, 0))],
        out_specs=tokd,
        out_shape=jax.ShapeDtypeStruct(x.shape, F32),
        compiler_params=_cparams(("parallel", "parallel")),
        name="hyena_out",
    )(x, conv, z, gate, bias.reshape(1, c), w, m5)


def _hyena_filter_spectrum(t, fw1, fb1, fw2, fb2, fw3, fb3, fw4, freq, width):
    pos = jnp.arange(t, dtype=F32)[:, None]
    tt = pos / max(t - 1, 1)
    ang = 2 * math.pi * pos / t
    nb = (HY_EMB - 1) // 2
    bands = jnp.linspace(1e-4, nb - 1, nb, dtype=F32)[None]
    feats = jnp.concatenate([tt, jnp.cos(bands * ang), -jnp.sin(bands * ang)], axis=-1)
    hdn = jnp.sin(freq * (feats @ fw1 + fb1))
    hdn = jnp.sin(freq * (hdn @ fw2 + fb2))
    hdn = jnp.sin(freq * (hdn @ fw3 + fb3))
    filt = (hdn @ fw4).reshape(t, HY_ORDER, 2, width)
    deltas = jnp.abs(jnp.linspace(HY_MIN_DECAY, HY_MAX_DECAY, width, dtype=F32))
    filt = filt * jnp.exp(-tt[:, :, None, None] * deltas)
    fwd, bwd = filt[:, :, 0], filt[:, :, 1]
    kern = jnp.concatenate([fwd, jnp.zeros_like(fwd[:1]), bwd[:0:-1]], axis=0)
    kern = kern * lax.rsqrt(jnp.sum(kern * kern, axis=0, keepdims=True) + 1e-6)
    spec = jnp.fft.fft(kern, axis=0)
    return jnp.stack([jnp.real(spec), jnp.imag(spec)], axis=1).transpose(2, 1, 0, 3).astype(F32)


def _block_diag_pair(m):
    z = jnp.zeros_like(m[0])
    return jnp.concatenate([jnp.concatenate([m[0], z], axis=1), jnp.concatenate([z, m[1]], axis=1)], axis=0)


def _even_mixer(lat, cx, ml, mc, gain, prm, ctx_out):
    (w_in, mu, w0, w_up, a0, a_up, g_up, k_k, k_a, r_k, gn_g, gn_b,
     lam_re, lam_im, log_dt, b_re, b_im, c_re, c_im, d_skip, w_glu, b_glu, w_out) = prm
    bsz, t_lat, d = lat.shape
    t_ctx = cx.shape[1]
    width = k_k.shape[0]
    heads = width // HEAD_DIM
    n_cols = mu.shape[0]
    s5w = d_skip.shape[0]
    ttot = t_ctx + t_lat
    nch = bsz * heads
    tm = t_ctx
    assert tm % GRID_W == 0 and t_lat % tm == 0 and 4 * nch == LANES and 2 * bsz == SUBLANES

    p_all, u_all = mod_matmul_stream(cx, lat, gain, mc, ml, w_in.astype(BF16), tm, n_cols)

    head_of = np.arange(width) // HEAD_DIM
    ones = jnp.asarray(head_of[:, None] == head_of[None, :], BF16)
    feat_consts = (mu.reshape(1, -1), _block_diag_pair(w_up).astype(BF16), w0.reshape(1, -1),
                   _block_diag_pair(a_up).astype(BF16), a0.reshape(1, -1), g_up.astype(BF16),
                   k_a.reshape(1, -1), r_k.reshape(1, -1), ones)
    v2, g, bonus, r2, k2, w2, a2 = rwkv_features(p_all, n_cols, width, feat_consts, tm)

    def key_major(x):
        x = x.reshape(2, bsz, ttot, heads, 2, HALF_HEAD)
        return x.transpose(2, 5, 0, 4, 1, 3).reshape(ttot, HALF_HEAD, LANES)

    def value_major(x):
        x = x.reshape(2, bsz, ttot, heads, HEAD_DIM)
        return x.transpose(2, 4, 0, 1, 3).reshape(ttot, HEAD_DIM, 2 * nch)

    def key_const(x):
        x = x.reshape(heads, 2, HALF_HEAD).transpose(2, 1, 0)[:, None, :, None, :]
        return jnp.broadcast_to(x, (HALF_HEAD, 2, 2, bsz, heads)).reshape(HALF_HEAD, LANES)

    o_f, o_b = rwkv_scan(key_major(r2), key_major(k2), key_major(w2), key_major(a2), value_major(v2),
                         key_const(k_a), key_const(k_k), t_ctx)

    def token_major(o, lane0):
        o = o[..., lane0:lane0 + nch].reshape(ttot, HEAD_DIM, bsz, heads)
        return o.transpose(2, 0, 3, 1).reshape(bsz, ttot, width)

    o_f, o_b = token_major(o_f, 0), token_major(o_b, LANES // 2)

    ng = lam_re.shape[1]
    gps = ng // S5_SUPER
    lam = lax.complex(lam_re, lam_im)
    dt = jnp.exp(log_dt)[..., None]
    a_bar = jnp.exp(lam * dt)
    b_bar = ((a_bar - 1) / lam)[..., None] * lax.complex(b_re, b_im)
    eye = jnp.eye(gps, dtype=F32)

    def b_mat(x):
        x = x.reshape(2, S5_SUPER, gps, S5_STATE, S5_GROUP)
        m = jnp.einsum('dsgph,gk->sdghkp', x, eye)
        return m.reshape(S5_SUPER, 2 * gps * S5_GROUP, gps * S5_STATE)

    def c_mat(x):
        x = x.reshape(2, S5_SUPER, gps, S5_GROUP, S5_STATE)
        m = jnp.einsum('dsghp,gk->skpdgh', x, eye)
        return m.reshape(S5_SUPER, gps * S5_STATE, 2 * gps * S5_GROUP)

    bm = jnp.concatenate([b_mat(jnp.real(b_bar)), b_mat(jnp.imag(b_bar))], axis=2).astype(BF16)
    cm = jnp.concatenate([c_mat(c_re), -c_mat(c_im)], axis=1).astype(BF16)

    def a_rows(x):
        x = x.reshape(2, 1, S5_SUPER, gps * S5_STATE)
        return jnp.broadcast_to(x, (2, bsz, S5_SUPER, gps * S5_STATE)).reshape(2 * bsz, S5_SUPER, -1)

    a_arr = jnp.concatenate([a_rows(jnp.real(a_bar)), a_rows(jnp.imag(a_bar))], axis=2)
    a_arr = a_arr.reshape(1, 2 * bsz, -1)

    y_f, y_b = s5_scan(u_all.transpose(1, 0, 2), bm, cm, a_arr, t_ctx)
    ys = (y_f[:, :bsz] + y_b[:, bsz:]).transpose(1, 0, 2)

    wo = w_out.astype(BF16)
    out_consts = (ones, gn_g.reshape(1, -1), gn_b.reshape(1, -1), d_skip.reshape(1, -1),
                  w_glu.astype(BF16), b_glu.reshape(1, -1), wo[:width], wo[width:])
    lat = even_out(lat, 1, o_f, o_b, bonus, g, ys, u_all, ml[5], out_consts, tm)
    if ctx_out:
        cx = even_out(cx, 0, o_f, o_b, bonus, g, ys, u_all, mc[5], out_consts, tm)
    return lat, cx


def _hyena_mixer(x, gain, m, prm):
    (w_in, conv_w, conv_b, fw1, fb1, fw2, fb2, fw3, fb3, fw4, freq, bias_d, w_out) = prm
    bsz, t, d = x.shape
    c = w_out.shape[0]
    use_fft = t % (FFT_N2 * 2) == 0 and t >= 4 * FFT_N2
    if use_fft:
        kf = hyena_filter_spectrum_fft(t, fw1, fb1, fw2, fb2, fw3, fb3, fw4, freq, c)
    else:
        kf = _hyena_filter_spectrum(t, fw1, fb1, fw2, fb2, fw3, fb3, fw4, freq, c)
    z, zb, g1, g2 = hyena_in(x, gain, m[3], m[4], w_in.astype(BF16), conv_w, conv_b)
    gates = (g1, g2)
    for n in range(HY_ORDER):
        if use_fft:
            nd = t // FFT_N2
            zt = zb.reshape(bsz, nd, FFT_N2, c).transpose(0, 2, 1, 3)
            conv = fft_conv(zt, kf, n)
        else:
            conv = dft_conv(zb, kf[n])
        if n < HY_ORDER - 1:
            z, zb = hyena_gate(conv, z, gates[n], bias_d[n])
        else:
            return hyena_out(x, conv, z, gates[n], bias_d[n], w_out.astype(BF16), m[5])


def kernel(x, c, ctx, c_ctx, norm_g, ada_w, ada_b, ffn_wg, ffn_wu, ffn_wd, final_g, ev_w_in, ev_mu, ev_w0, ev_w_up, ev_a0, ev_a_up, ev_g_up, ev_k_k, ev_k_a, ev_r_k, ev_gn_g, ev_gn_b, ev_lam_re, ev_lam_im, ev_log_dt, ev_b_re, ev_b_im, ev_c_re, ev_c_im, ev_d, ev_w_glu, ev_b_glu, ev_w_out, od_w_in, od_conv_w, od_conv_b, od_fw1, od_fb1, od_fw2, od_fb2, od_fw3, od_fb3, od_fw4, od_freq, od_bias, od_w_out):
    depth = norm_g.shape[0]
    bsz, _, d = x.shape
    n_even = (depth + 1) // 2
    last_ctx = 2 * (n_even - 1)

    cond8 = jnp.concatenate([c, c_ctx[None], jnp.zeros((8 - bsz - 1, d), F32)], axis=0)
    mods = ada_mods_all(cond8, ada_w, ada_b)

    wg, wu, wd = ffn_wg.astype(BF16), ffn_wu.astype(BF16), ffn_wd.astype(BF16)
    lat, cx = x, ctx
    for l in range(depth):
        run_ctx = l <= last_ctx
        ctx_out = l < last_ctx
        i = l // 2
        ml = [mods[l, :bsz, None, k * d:(k + 1) * d] for k in range(N_MOD)]
        mc = [jnp.broadcast_to(mods[l, bsz:bsz + 1, None, k * d:(k + 1) * d], (bsz, 1, d))
              for k in range(N_MOD)]
        lat = ffn_half(lat, norm_g[l, 0], ml[0], ml[1], ml[2], wg[l, 0], wu[l, 0], wd[l, 0])
        if run_ctx:
            cx = ffn_half(cx.reshape(1, -1, d), norm_g[l, 0], mc[0][:1], mc[1][:1], mc[2][:1],
                          wg[l, 0], wu[l, 0], wd[l, 0]).reshape(cx.shape)
        if l % 2 == 0:
            prm = (ev_w_in[i], ev_mu[i], ev_w0[i], ev_w_up[i], ev_a0[i], ev_a_up[i], ev_g_up[i],
                   ev_k_k[i], ev_k_a[i], ev_r_k[i], ev_gn_g[i], ev_gn_b[i],
                   ev_lam_re[i], ev_lam_im[i], ev_log_dt[i], ev_b_re[i], ev_b_im[i], ev_c_re[i], ev_c_im[i],
                   ev_d[i], ev_w_glu[i], ev_b_glu[i], ev_w_out[i])
            lat, cx = _even_mixer(lat, cx, ml, mc, norm_g[l, 1], prm, ctx_out)
        else:
            prm = (od_w_in[i], od_conv_w[i], od_conv_b[i], od_fw1[i], od_fb1[i], od_fw2[i], od_fb2[i],
                   od_fw3[i], od_fb3[i], od_fw4[i], od_freq[i], od_bias[i], od_w_out[i])
            lat = _hyena_mixer(lat, norm_g[l, 1], ml, prm)
            if ctx_out:
                cx = _hyena_mixer(cx, norm_g[l, 1], mc, prm)
        fin = final_g if l == depth - 1 else None
        lat = ffn_half(lat, norm_g[l, 2], ml[6], ml[7], ml[8], wg[l, 1], wu[l, 1], wd[l, 1], fin)
        if ctx_out:
            cx = ffn_half(cx.reshape(1, -1, d), norm_g[l, 2], mc[6][:1], mc[7][:1], mc[8][:1],
                          wg[l, 1], wu[l, 1], wd[l, 1]).reshape(cx.shape)
    return lat
```
